```python
import math
import jax, jax.numpy as jnp
from jax import lax
import numpy as np

D_MODEL = 2048
BATCH = 4
SEQ = 2048
DEPTH = 1
DEC_BATCH = 32
DEC_SEQ = 8
PAST_LEN = 8192
PAGE_SIZE = 128

HEAD_DIM_A = 128
HEADS_PER_GROUP_A = 4
DIL_GROUPS = ((128, 1), (512, 4), (2048, 16))
N_HEADS_A = len(DIL_GROUPS) * HEADS_PER_GROUP_A
WIDTH_A = N_HEADS_A * HEAD_DIM_A
OUT_A = HEADS_PER_GROUP_A * HEAD_DIM_A
CHUNK = 128
N_GROUPS_B = 8
WIDTH_B = D_MODEL // 2
GROUP_DIM_B = WIDTH_B // N_GROUPS_B
N_MEM = 256
N_HEADS_M = 4
HEAD_DIM_M = 256
WIDTH_M = N_HEADS_M * HEAD_DIM_M
N_BUCKETS = 32
MAX_EXACT = N_BUCKETS // 2
MAX_DISTANCE = 2048
N_EXPERT_GROUPS = 4
EXPERTS_PER_GROUP = 8
N_EXPERTS = N_EXPERT_GROUPS * EXPERTS_PER_GROUP
TOP_K = 2
D_EXPERT = 512
MOE_BLOCK = 128
N_BRANCHES = 3
IN_SPLITS = (WIDTH_A, WIDTH_A, WIDTH_A, WIDTH_B, WIDTH_B, WIDTH_M, N_BRANCHES * D_MODEL)
D_IN = sum(IN_SPLITS)
EPS = 1e-6
NEG_INF = -1e30
F32 = jnp.float32

kernel_name = 'hybrid_dilated_gmlp_memory_hmoe_step'


def rmsnorm(x, g):
    xf = x.astype(F32)
    y = xf * lax.rsqrt(jnp.mean(xf * xf, axis=-1, keepdims=True) + EPS)
    return (y * g.astype(F32)).astype(x.dtype)


def rel_bucket(dist):
    d = jnp.maximum(dist, 1).astype(F32)
    large = MAX_EXACT + (jnp.log(d / MAX_EXACT) / math.log(MAX_DISTANCE / MAX_EXACT)
                         * (N_BUCKETS - MAX_EXACT)).astype(jnp.int32)
    return jnp.where(dist < MAX_EXACT, dist, jnp.minimum(large, N_BUCKETS - 1))


def softmax_stats(logits):
    m = jnp.max(logits, axis=-1, keepdims=True)
    p = jnp.exp(logits - m)
    s = jnp.sum(p, axis=-1, keepdims=True)
    return p / s, (m + jnp.log(s))[..., 0]


def dilated_prompt(q, k, v, bias_tab, win, dil):
    b, s, h, dh = q.shape
    wk = win // dil
    ln = s // dil
    n = b * dil
    blk = wk
    nb = -(-ln // blk)
    lp = nb * blk

    def to_blocks(t):
        t = t.reshape(b, ln, dil, h, dh).transpose(0, 2, 1, 3, 4).reshape(n, ln, h, dh)
        return jnp.pad(t, ((0, 0), (0, lp - ln), (0, 0), (0, 0))).reshape(n, nb, blk, h, dh)

    def with_prev(t):
        prev = jnp.pad(t, ((0, 0), (1, 0), (0, 0), (0, 0), (0, 0)))[:, :-1]
        return jnp.concatenate([prev, t], axis=2)

    qb = to_blocks(q)
    kk = with_prev(to_blocks(k))
    vv = with_prev(to_blocks(v))
    a = jnp.arange(blk)[:, None]
    c = jnp.arange(2 * blk)[None, :]
    dist = blk + a - c
    band = (dist >= 0) & (dist <= wk)
    valid = band[None] & ((jnp.arange(nb)[:, None, None] > 0) | (c >= blk)[None])
    bias = bias_tab[rel_bucket(jnp.maximum(dist, 0) * dil)].transpose(2, 0, 1)
    logits = (jnp.einsum('nbqhd,nbkhd->nbhqk', qb, kk).astype(F32) * dh ** -0.5
              + bias[None, None].astype(F32))
    logits = jnp.where(valid[None, :, None], logits, NEG_INF)
    p, lse = softmax_stats(logits)
    o = jnp.einsum('nbhqk,nbkhd->nbqhd', p.astype(v.dtype), vv)
    o = o.reshape(n, lp, h, dh)[:, :ln]
    o = o.reshape(b, dil, ln, h, dh).transpose(0, 2, 1, 3, 4).reshape(b, s, h, dh)
    lse = lse.transpose(0, 1, 3, 2).reshape(n, lp, h)[:, :ln]
    lse = lse.reshape(b, dil, ln, h).transpose(0, 2, 1, 3).reshape(b, s, h)
    return o, lse


def dilated_sample(q, k_new, v_new, kv_buf, bias_tab, win, dil):
    n, t, h, dh = q.shape
    wk = win // dil
    lb = kv_buf.shape[1]
    kv_all = jnp.concatenate([kv_buf, jnp.stack([k_new, v_new], axis=2).astype(kv_buf.dtype)], axis=1)
    j = jnp.arange(wk + 1)
    idx = lb + jnp.arange(t)[:, None] - j[None, :] * dil
    valid = idx >= 0
    kv_g = kv_all[:, jnp.maximum(idx, 0)]
    bias = bias_tab[rel_bucket(j * dil)].T
    logits = (jnp.einsum('nthd,ntjhd->nhtj', q, kv_g[:, :, :, 0]).astype(F32) * dh ** -0.5
              + bias[None, :, None, :].astype(F32))
    logits = jnp.where(valid[None, None], logits, NEG_INF)
    p, lse = softmax_stats(logits)
    o = jnp.einsum('nhtj,ntjhd->nthd', p.astype(kv_g.dtype), kv_g[:, :, :, 1])
    new_buf = kv_all[:, lb + t - min(win, lb + t):]
    return o, lse.transpose(0, 2, 1), new_buf


def combine_groups(outs, lses):
    w = jax.nn.softmax(jnp.stack(lses, axis=0), axis=0)
    o = jnp.einsum('gbth,gbthd->bthd', w, jnp.stack(outs, axis=0).astype(F32))
    return o.astype(outs[0].dtype)


def chunk_gmlp(u, v, w_s, b_s):
    b, t, _ = v.shape
    nc = -(-t // CHUNK)
    vp = jnp.pad(v, ((0, 0), (0, nc * CHUNK - t), (0, 0))).reshape(b, nc, CHUNK, N_GROUPS_B, GROUP_DIM_B)
    ws = w_s * jnp.tril(jnp.ones((CHUNK, CHUNK), w_s.dtype))
    sg = jnp.einsum('gpq,bcqgd->bcpgd', ws, vp) + b_s.T[None, None, :, :, None]
    return u * sg.reshape(b, nc * CHUNK, WIDTH_B)[:, :t].astype(u.dtype)


def memory_kv(mem, g_mem, w_mk, w_mv, g_km):
    n, m, _ = mem.shape
    mn = rmsnorm(mem, g_mem)
    k = rmsnorm((mn @ w_mk).reshape(n, m, N_HEADS_M, HEAD_DIM_M), g_km)
    v = (mn @ w_mv).reshape(n, m, N_HEADS_M, HEAD_DIM_M)
    return jnp.stack([k, v], axis=2)


def memory_attn(q, kv):
    logits = jnp.einsum('nthd,nmhd->nhtm', q, kv[:, :, 0]).astype(F32) * HEAD_DIM_M ** -0.5
    p, _ = softmax_stats(logits)
    return jnp.einsum('nhtm,nmhd->nthd', p.astype(kv.dtype), kv[:, :, 1])


def mixer_inputs(h, w_in, g_qa, g_ka, g_vb, g_qm):
    b, t, _ = h.shape
    cuts = [int(c) for c in np.cumsum(IN_SPLITS)[:-1]]
    qa, ka, va, ub, vb, qm, gt = jnp.split(h @ w_in, cuts, axis=-1)
    qa = rmsnorm(qa.reshape(b, t, N_HEADS_A, HEAD_DIM_A), g_qa)
    ka = rmsnorm(ka.reshape(b, t, N_HEADS_A, HEAD_DIM_A), g_ka)
    va = va.reshape(b, t, N_HEADS_A, HEAD_DIM_A)
    vb = rmsnorm(vb, g_vb)
    qm = rmsnorm(qm.reshape(b, t, N_HEADS_M, HEAD_DIM_M), g_qm)
    gates = jax.nn.sigmoid(gt.astype(F32)).reshape(b, t, N_BRANCHES, D_MODEL)
    return qa, ka, va, ub, vb, qm, gates


def merge(gates, oa, ob, om, w_pa, w_pb, w_pm, w_o):
    b, t = oa.shape[:2]
    pa = oa.reshape(b, t, OUT_A) @ w_pa
    pb = ob @ w_pb
    pm = om.reshape(b, t, WIDTH_M) @ w_pm
    z = gates[:, :, 0] * pa + gates[:, :, 1] * pb + gates[:, :, 2] * pm
    return z.astype(oa.dtype) @ w_o


def hmoe(x, w_rg, b_rg, w_re, b_re, w_gate, w_up, w_down):
    n_tok, d = x.shape
    xf = x.astype(F32)
    lg = xf @ w_rg.astype(F32) + b_rg.astype(F32)
    gsel = jnp.argmax(lg, axis=-1).astype(jnp.int32)
    pg_sel = jnp.take_along_axis(jax.nn.softmax(lg, axis=-1), gsel[:, None], axis=-1)
    le = (xf @ w_re.astype(F32) + b_re.astype(F32)).reshape(n_tok, N_EXPERT_GROUPS, EXPERTS_PER_GROUP)
    le_sel = jnp.take_along_axis(le, gsel[:, None, None], axis=1)[:, 0]
    top_p, top_i = lax.top_k(jax.nn.softmax(le_sel, axis=-1), TOP_K)
    slot_w = (pg_sel * top_p / jnp.sum(top_p, axis=-1, keepdims=True)).reshape(-1)
    eid = (gsel[:, None] * EXPERTS_PER_GROUP + top_i.astype(jnp.int32)).reshape(-1)
    n_slot = n_tok * TOP_K
    tok = jnp.arange(n_slot) // TOP_K
    order = jnp.argsort(eid)
    eid_s, tok_s, w_s_ = eid[order], tok[order], slot_w[order]
    counts = jnp.zeros((N_EXPERTS,), jnp.int32).at[eid].add(1)
    start = jnp.cumsum(counts) - counts
    padded = (counts + MOE_BLOCK - 1) // MOE_BLOCK * MOE_BLOCK
    pend = jnp.cumsum(padded)
    pstart = pend - padded
    dest = pstart[eid_s] + jnp.arange(n_slot) - start[eid_s]
    n_blocks = -(-n_slot // MOE_BLOCK) + N_EXPERTS
    buf = jnp.zeros((n_blocks * MOE_BLOCK, d), x.dtype).at[dest].set(x[tok_s])
    blk_e = jnp.minimum(jnp.searchsorted(pend, jnp.arange(n_blocks) * MOE_BLOCK, side='right'), N_EXPERTS - 1)

    def expert_block(args):
        xb, e = args
        return (jax.nn.silu(xb @ w_gate[e]) * (xb @ w_up[e])) @ w_down[e]

    y = lax.map(expert_block, (buf.reshape(n_blocks, MOE_BLOCK, d), blk_e)).reshape(-1, d)
    out = jnp.zeros((n_tok, d), F32).at[tok_s].add(y[dest].astype(F32) * w_s_[:, None])
    return out.astype(x.dtype)


def setup_inputs(seed: int = 0) -> dict:
    key = jax.random.key(seed)
    ks = iter(jax.random.split(key, 40))
    D = D_MODEL

    def nrm(shape, scale=1.0):
        return jax.random.normal(next(ks), shape, F32) * scale

    def gain(shape):
        return 1.0 + 0.1 * jax.random.normal(next(ks), shape, F32)

    la = [min(w, PAST_LEN) for w, _ in DIL_GROUPS]
    return {
        'x_prompt': nrm((BATCH, SEQ, D)),
        'x_sample': nrm((DEC_BATCH, DEC_SEQ, D)),
        'mem_prompt': nrm((BATCH, N_MEM, D)),
        'cache_a0_kv': nrm((DEPTH, DEC_BATCH, la[0], 2, HEADS_PER_GROUP_A, HEAD_DIM_A)),
        'cache_a1_kv': nrm((DEPTH, DEC_BATCH, la[1], 2, HEADS_PER_GROUP_A, HEAD_DIM_A)),
        'cache_a2_kv': nrm((DEPTH, DEC_BATCH, la[2], 2, HEADS_PER_GROUP_A, HEAD_DIM_A)),
        'cache_mem_kv': nrm((DEPTH, DEC_BATCH, N_MEM, 2, N_HEADS_M, HEAD_DIM_M)),
        'rel_bias': nrm((N_BUCKETS, N_HEADS_A), 0.5),
        'g_mix': gain((DEPTH, D)),
        'w_in': nrm((DEPTH, D, D_IN), D ** -0.5),
        'g_qa': gain((DEPTH, HEAD_DIM_A)),
        'g_ka': gain((DEPTH, HEAD_DIM_A)),
        'w_pa': nrm((DEPTH, OUT_A, D), OUT_A ** -0.5),
        'g_vb': gain((DEPTH, WIDTH_B)),
        'w_s': nrm((DEPTH, N_GROUPS_B, CHUNK, CHUNK), CHUNK ** -0.5),
        'b_s': gain((DEPTH, N_GROUPS_B, CHUNK)),
        'w_pb': nrm((DEPTH, WIDTH_B, D), WIDTH_B ** -0.5),
        'g_mem': gain((DEPTH, D)),
        'w_mk': nrm((DEPTH, D, WIDTH_M), D ** -0.5),
        'w_mv': nrm((DEPTH, D, WIDTH_M), D ** -0.5),
        'g_qm': gain((DEPTH, HEAD_DIM_M)),
        'g_km': gain((DEPTH, HEAD_DIM_M)),
        'w_pm': nrm((DEPTH, WIDTH_M, D), WIDTH_M ** -0.5),
        'w_o': nrm((DEPTH, D, D), D ** -0.5),
        'g_ffn': gain((DEPTH, D)),
        'w_rg': nrm((DEPTH, D, N_EXPERT_GROUPS), D ** -0.5),
        'b_rg': nrm((DEPTH, N_EXPERT_GROUPS), 0.01),
        'w_re': nrm((DEPTH, D, N_EXPERTS), D ** -0.5),
        'b_re': nrm((DEPTH, N_EXPERTS), 0.01),
        'w_gate': nrm((DEPTH, N_EXPERTS, D, D_EXPERT), D ** -0.5),
        'w_up': nrm((DEPTH, N_EXPERTS, D, D_EXPERT), D ** -0.5),
        'w_down': nrm((DEPTH, N_EXPERTS, D_EXPERT, D), D_EXPERT ** -0.5),
    }


def reference(x_prompt, x_sample, mem_prompt, cache_a0_kv, cache_a1_kv, cache_a2_kv, cache_mem_kv,
              rel_bias, g_mix, w_in, g_qa, g_ka, w_pa, g_vb, w_s, b_s, w_pb, g_mem, w_mk, w_mv,
              g_qm, g_km, w_pm, w_o, g_ffn, w_rg, b_rg, w_re, b_re, w_gate, w_up, w_down):
    caches_a = (cache_a0_kv, cache_a1_kv, cache_a2_kv)
    hpg = HEADS_PER_GROUP_A
    xp, xs = x_prompt, x_sample
    st_pa = [[] for _ in DIL_GROUPS]
    st_sa = [[] for _ in DIL_GROUPS]
    st_pm, st_sv = [], []
    for l in range(DEPTH):
        s_len = xp.shape[1]
        qa, ka, va, ub, vb, qm, gates = mixer_inputs(rmsnorm(xp, g_mix[l]), w_in[l], g_qa[l], g_ka[l], g_vb[l], g_qm[l])
        outs, lses = [], []
        for g, (win, dil) in enumerate(DIL_GROUPS):
            hs = slice(g * hpg, (g + 1) * hpg)
            o, lse = dilated_prompt(qa[:, :, hs], ka[:, :, hs], va[:, :, hs], rel_bias[:, hs], win, dil)
            outs.append(o)
            lses.append(lse)
            st_pa[g].append(jnp.stack([ka[:, :, hs], va[:, :, hs]], axis=2)[:, s_len - min(win, s_len):])
        oa = combine_groups(outs, lses)
        ob = chunk_gmlp(ub, vb, w_s[l], b_s[l])
        mkv = memory_kv(mem_prompt, g_mem[l], w_mk[l], w_mv[l], g_km[l])
        st_pm.append(mkv)
        om = memory_attn(qm, mkv)
        xp = xp + merge(gates, oa, ob, om, w_pa[l], w_pb[l], w_pm[l], w_o[l])

        qa, ka, va, ub, vb, qm, gates = mixer_inputs(rmsnorm(xs, g_mix[l]), w_in[l], g_qa[l], g_ka[l], g_vb[l], g_qm[l])
        outs, lses = [], []
        for g, (win, dil) in enumerate(DIL_GROUPS):
            hs = slice(g * hpg, (g + 1) * hpg)
            o, lse, nbuf = dilated_sample(qa[:, :, hs], ka[:, :, hs], va[:, :, hs], caches_a[g][l],
                                          rel_bias[:, hs], win, dil)
            outs.append(o)
            lses.append(lse)
            st_sa[g].append(nbuf)
        oa = combine_groups(outs, lses)
        ob = chunk_gmlp(ub, vb, w_s[l], b_s[l])
        st_sv.append(vb)
        om = memory_attn(qm, cache_mem_kv[l])
        xs = xs + merge(gates, oa, ob, om, w_pa[l], w_pb[l], w_pm[l], w_o[l])

        n_p = xp.shape[0] * xp.shape[1]
        flat = jnp.concatenate([xp.reshape(-1, D_MODEL), xs.reshape(-1, D_MODEL)], axis=0)
        f = hmoe(rmsnorm(flat, g_ffn[l]), w_rg[l], b_rg[l], w_re[l], b_re[l], w_gate[l], w_up[l], w_down[l])
        xp = xp + f[:n_p].reshape(xp.shape)
        xs = xs + f[n_p:].reshape(xs.shape)

    new_a0_p = jnp.stack(st_pa[0], axis=0)
    new_a1_p = jnp.stack(st_pa[1], axis=0)
    new_a2_p = jnp.stack(st_pa[2], axis=0)
    new_mem_p = jnp.stack(st_pm, axis=0)
    new_a0_s = jnp.stack(st_sa[0], axis=0)
    new_a1_s = jnp.stack(st_sa[1], axis=0)
    new_a2_s = jnp.stack(st_sa[2], axis=0)
    new_vb_s = jnp.stack(st_sv, axis=0)
    return (xp, xs, new_a0_p, new_a1_p, new_a2_p, new_mem_p, new_a0_s, new_a1_s, new_a2_s, new_vb_s)
```

```python
import functools
import math

import jax
import jax.numpy as jnp
from jax import lax
from jax.experimental import pallas as pl
from jax.experimental.pallas import tpu as pltpu

F32 = jnp.float32
BF16 = jnp.bfloat16
I32 = jnp.int32
U32 = jnp.uint32

EPS = 1e-6
NEG_INF = -1e30

HEAD_DIM_A = 128
HEADS_PER_GROUP_A = 4
DIL_GROUPS = ((128, 1), (512, 4), (2048, 16))
GROUP_W_A = HEADS_PER_GROUP_A * HEAD_DIM_A
WIDTH_A = len(DIL_GROUPS) * GROUP_W_A
CHUNK = 128
N_GROUPS_B = 8
N_HEADS_M = 4
HEAD_DIM_M = 256
N_BUCKETS = 32
MAX_EXACT = N_BUCKETS // 2
MAX_DISTANCE = 2048
N_EXPERT_GROUPS = 4
EXPERTS_PER_GROUP = 8
N_EXPERTS = N_EXPERT_GROUPS * EXPERTS_PER_GROUP
TOP_K = 2
LANES = 128
MOE_ROWS = 256
VMEM_LIMIT = 56 * 1024 * 1024


def _cparams(n_grid, vmem=VMEM_LIMIT):
    return pltpu.CompilerParams(dimension_semantics=("arbitrary",) * n_grid, vmem_limit_bytes=vmem)


def _rms(x, g):
    return x * lax.rsqrt(jnp.mean(x * x, axis=-1, keepdims=True) + EPS) * g


def _pack_bf16_pair(x):
    n = x.shape[1] // 2
    lo = lax.bitcast_convert_type(x[:, :n].astype(BF16).astype(F32), U32)
    hi = lax.bitcast_convert_type(x[:, n:].astype(BF16).astype(F32), U32)
    return (hi & jnp.uint32(0xFFFF0000)) | (lo >> 16)


def _unpack_bf16_pair(w):
    lo = lax.bitcast_convert_type(w << 16, F32)
    hi = lax.bitcast_convert_type(w & jnp.uint32(0xFFFF0000), F32)
    return lo, hi


def _rel_bucket(dist):
    d = jnp.maximum(dist, 1).astype(F32)
    large = MAX_EXACT + (jnp.log(d / MAX_EXACT) / math.log(MAX_DISTANCE / MAX_EXACT)
                         * (N_BUCKETS - MAX_EXACT)).astype(I32)
    return jnp.where(dist < MAX_EXACT, dist, jnp.minimum(large, N_BUCKETS - 1)).astype(I32)


def _prep_body(x_ref, g_ref, o_ref):
    o_ref[...] = _rms(x_ref[...], g_ref[...]).astype(o_ref.dtype)


def _prep(x2d, g, tm):
    m, d = x2d.shape
    return pl.pallas_call(
        _prep_body, name="prep", grid=(m // tm,),
        in_specs=[pl.BlockSpec((tm, d), lambda i: (i, 0)), pl.BlockSpec((1, d), lambda i: (0, 0))],
        out_specs=pl.BlockSpec((tm, d), lambda i: (i, 0)),
        out_shape=jax.ShapeDtypeStruct((m, d), BF16), compiler_params=_cparams(1),
    )(x2d, g.reshape(1, d))


def _headnorm(acc, g, hd, scale):
    outs = []
    for j in range(acc.shape[1] // hd):
        sl = acc[:, j * hd:(j + 1) * hd]
        outs.append(sl * lax.rsqrt(jnp.mean(sl * sl, axis=-1, keepdims=True) + EPS))
    y = outs[0] if len(outs) == 1 else jnp.concatenate(outs, axis=1)
    y = y * g
    return y * scale if scale != 1.0 else y


def _proj_body(h_ref, w_ref, g_ref, o_ref, *, mode, hd, scale):
    acc = jnp.dot(h_ref[...], w_ref[...], preferred_element_type=F32)
    if mode == "none":
        o_ref[...] = acc.astype(o_ref.dtype)
    elif mode == "sigmoid":
        o_ref[...] = jax.nn.sigmoid(acc).astype(o_ref.dtype)
    elif mode == "norm":
        o_ref[...] = _headnorm(acc, g_ref[...], hd, scale).astype(o_ref.dtype)
    else:
        @pl.when(pl.program_id(0) == 0)
        def _():
            o_ref[...] = _headnorm(acc, g_ref[...], hd, scale).astype(o_ref.dtype)

        @pl.when(pl.program_id(0) != 0)
        def _():
            o_ref[...] = acc.astype(o_ref.dtype)


def _proj(h, w, tn, tm, mode, out_dtype, gain=None, hd=None, scale=1.0, name="proj"):
    m, k = h.shape
    n = w.shape[1]
    if gain is None:
        g = jnp.ones((1, tn), F32)
    else:
        g = jnp.tile(gain.astype(F32).reshape(1, -1), (1, tn // gain.shape[-1]))
    body = functools.partial(_proj_body, mode=mode, hd=hd, scale=scale)
    return pl.pallas_call(
        body, name=name, grid=(n // tn, m // tm),
        in_specs=[pl.BlockSpec((tm, k), lambda j, i: (i, 0)),
                  pl.BlockSpec((k, tn), lambda j, i: (0, j)),
                  pl.BlockSpec((1, tn), lambda j, i: (0, 0))],
        out_specs=pl.BlockSpec((tm, tn), lambda j, i: (i, j)),
        out_shape=jax.ShapeDtypeStruct((m, n), out_dtype), compiler_params=_cparams(2),
    )(h, w, g)


def _bias_from_table(tab_ref, bidx, col):
    acc = jnp.zeros(bidx.shape, F32)
    for kb in range(N_BUCKETS):
        acc = jnp.where(bidx == kb, tab_ref[kb, col], acc)
    return acc


def _attn_p_body(tab_ref, bidx_ref, q_ref, kp_ref, kc_ref, vp_ref, vc_ref, o_ref, lse_ref, bias_scr, *, g):
    first = (pl.program_id(0) == 0) & (pl.program_id(1) == 0) & (pl.program_id(2) == 0)

    @pl.when(first)
    def _():
        bidx = bidx_ref[...]
        for h in range(HEADS_PER_GROUP_A):
            bias_scr[h] = _bias_from_table(tab_ref, bidx, g * HEADS_PER_GROUP_A + h)

    qb = pl.program_id(2)
    blk = q_ref.shape[0]
    q = q_ref[...]
    k = jnp.concatenate([kp_ref[...], kc_ref[...]], axis=0).astype(BF16)
    v = jnp.concatenate([vp_ref[...], vc_ref[...]], axis=0).astype(BF16)
    row = lax.broadcasted_iota(I32, (blk, 2 * blk), 0)
    col = lax.broadcasted_iota(I32, (blk, 2 * blk), 1)
    dist = blk + row - col
    valid = (dist >= 0) & (dist <= blk) & ((qb > 0) | (col >= blk))
    lane = lax.broadcasted_iota(I32, (blk, LANES), 1)
    lse_blk = jnp.zeros((blk, LANES), F32)
    outs = []
    for h in range(HEADS_PER_GROUP_A):
        hs = slice(h * HEAD_DIM_A, (h + 1) * HEAD_DIM_A)
        s = lax.dot_general(q[:, hs], k[:, hs], (((1,), (1,)), ((), ())), preferred_element_type=F32)
        s = jnp.where(valid, s + bias_scr[h], NEG_INF)
        m = jnp.max(s, axis=-1, keepdims=True)
        p = jnp.exp(s - m)
        l = jnp.sum(p, axis=-1, keepdims=True)
        o = jnp.dot(p.astype(BF16), v[:, hs], preferred_element_type=F32) / l
        outs.append(o)
        lse_blk = jnp.where(lane == h, m + jnp.log(l), lse_blk)
    o_ref[...] = jnp.concatenate(outs, axis=1).astype(o_ref.dtype)
    lse_ref[...] = lse_blk


def _attn_prompt(q, kv, rel_bias, g, n_batch, seq):
    win, dil = DIL_GROUPS[g]
    blk = win // dil
    ln = seq // dil
    nb = ln // blk
    rows = n_batch * seq // dil
    a = jnp.arange(blk)[:, None]
    c = jnp.arange(2 * blk)[None, :]
    bidx = _rel_bucket(jnp.maximum(blk + a - c, 0) * dil)
    qv = q.reshape(rows, dil * WIDTH_A)
    kvv = kv.reshape(rows, dil * 2 * GROUP_W_A)
    n_qc = WIDTH_A // GROUP_W_A

    def cur(b, r, j):
        return b * nb + j

    def prev(b, r, j):
        return b * nb + jnp.maximum(j - 1, 0)

    o, lse = pl.pallas_call(
        functools.partial(_attn_p_body, g=g), name=f"attn_p{g}", grid=(n_batch, dil, nb),
        in_specs=[
            pl.BlockSpec(memory_space=pltpu.SMEM),
            pl.BlockSpec((blk, 2 * blk), lambda b, r, j: (0, 0)),
            pl.BlockSpec((blk, GROUP_W_A), lambda b, r, j: (cur(b, r, j), r * n_qc + g)),
            pl.BlockSpec((blk, GROUP_W_A), lambda b, r, j: (prev(b, r, j), r * 2)),
            pl.BlockSpec((blk, GROUP_W_A), lambda b, r, j: (cur(b, r, j), r * 2)),
            pl.BlockSpec((blk, GROUP_W_A), lambda b, r, j: (prev(b, r, j), r * 2 + 1)),
            pl.BlockSpec((blk, GROUP_W_A), lambda b, r, j: (cur(b, r, j), r * 2 + 1)),
        ],
        out_specs=[pl.BlockSpec((blk, GROUP_W_A), lambda b, r, j: (cur(b, r, j), r)),
                   pl.BlockSpec((blk, LANES), lambda b, r, j: (cur(b, r, j), r))],
        out_shape=[jax.ShapeDtypeStruct((rows, dil * GROUP_W_A), BF16),
                   jax.ShapeDtypeStruct((rows, dil * LANES), F32)],
        scratch_shapes=[pltpu.VMEM((HEADS_PER_GROUP_A, blk, 2 * blk), F32)],
        compiler_params=_cparams(3),
    )(rel_bias, bidx, qv, kvv, kvv, kvv, kvv)
    return o.reshape(n_batch * seq, GROUP_W_A), lse.reshape(n_batch * seq, LANES)


def _attn_s_body(tab_ref, bidx_ref, q_ref, kvn_ref, cache_ref, o_ref, lse_ref, kv_scr, bias_scr, *, g, dil):
    lb = cache_ref.shape[0]
    t_new = q_ref.shape[0]
    nh = HEADS_PER_GROUP_A
    n_rows = nh * t_new
    n_keys = kv_scr.shape[0]

    @pl.when(pl.program_id(0) == 0)
    def _():
        kv_scr[lb:, :] = jnp.zeros((n_keys - lb, kv_scr.shape[1]), BF16)
        bidx = bidx_ref[...]
        for h in range(nh):
            bias_scr[h * t_new:(h + 1) * t_new, :] = _bias_from_table(tab_ref, bidx, g * nh + h)

    kv_scr[:lb, :] = cache_ref[...].astype(BF16)
    new = kvn_ref[...]
    kv_scr[lb:lb + 2 * t_new, :] = jnp.concatenate([new, jnp.zeros_like(new)], axis=0).astype(BF16)

    q = q_ref[...]
    qt = jnp.concatenate([q] * nh, axis=0)
    rr = lax.broadcasted_iota(I32, (n_rows, GROUP_W_A), 0)
    cc = lax.broadcasted_iota(I32, (n_rows, GROUP_W_A), 1)
    qbd = jnp.where(cc // HEAD_DIM_A == rr // t_new, qt, 0.0).astype(BF16)

    s = lax.dot_general(qbd, kv_scr[:, :GROUP_W_A], (((1,), (1,)), ((), ())), preferred_element_type=F32)
    row = lax.broadcasted_iota(I32, (n_rows, n_keys), 0)
    col = lax.broadcasted_iota(I32, (n_rows, n_keys), 1)
    delta = lb + (row & (t_new - 1)) - col
    valid = (delta >= 0) & (delta <= lb) & ((delta & (dil - 1)) == 0)
    s = jnp.where(valid, s + bias_scr[...], NEG_INF)
    m = jnp.max(s, axis=-1, keepdims=True)
    p = jnp.exp(s - m)
    l = jnp.sum(p, axis=-1, keepdims=True)
    o = jnp.dot(p.astype(BF16), kv_scr[:, GROUP_W_A:], preferred_element_type=F32) / l
    lse = m + jnp.log(l)
    lane = lax.broadcasted_iota(I32, (t_new, LANES), 1)
    lse_blk = jnp.zeros((t_new, LANES), F32)
    outs = []
    for h in range(nh):
        outs.append(o[h * t_new:(h + 1) * t_new, h * HEAD_DIM_A:(h + 1) * HEAD_DIM_A])
        lse_blk = jnp.where(lane == h, lse[h * t_new:(h + 1) * t_new, :], lse_blk)
    o_ref[...] = jnp.concatenate(outs, axis=1)
    lse_ref[...] = lse_blk


def _attn_sample(q, kv_new, cache, rel_bias, g, t_new):
    win, dil = DIL_GROUPS[g]
    n_req, lb, _ = cache.shape
    assert t_new & (t_new - 1) == 0 and dil & (dil - 1) == 0 and lb == win
    n_keys = lb + LANES
    t = jnp.arange(t_new)[:, None]
    c = jnp.arange(n_keys)[None, :]
    bidx = _rel_bucket(jnp.clip(lb + t - c, 0, lb))
    n_qc = WIDTH_A // GROUP_W_A
    return pl.pallas_call(
        functools.partial(_attn_s_body, g=g, dil=dil), name=f"attn_s{g}", grid=(n_req,),
        in_specs=[
            pl.BlockSpec(memory_space=pltpu.SMEM),
            pl.BlockSpec((t_new, n_keys), lambda n: (0, 0)),
            pl.BlockSpec((t_new, GROUP_W_A), lambda n: (n, g)),
            pl.BlockSpec((t_new, 2 * GROUP_W_A), lambda n: (n, 0)),
            pl.BlockSpec((None, lb, 2 * GROUP_W_A), lambda n: (n, 0, 0)),
        ],
        out_specs=[pl.BlockSpec((t_new, GROUP_W_A), lambda n: (n, 0)),
                   pl.BlockSpec((t_new, LANES), lambda n: (n, 0))],
        out_shape=[jax.ShapeDtypeStruct((n_req * t_new, GROUP_W_A), F32),
                   jax.ShapeDtypeStruct((n_req * t_new, LANES), F32)],
        scratch_shapes=[pltpu.VMEM((n_keys, 2 * GROUP_W_A), BF16),
                        pltpu.VMEM((HEADS_PER_GROUP_A * t_new, n_keys), F32)],
        compiler_params=_cparams(1),
    )(rel_bias, bidx, q, kv_new, cache)


def _gmlp_body(u_ref, v_ref, w_ref, b_ref, o_ref, *, period):
    c = w_ref.shape[1]
    nch = u_ref.shape[0] // c
    gd = u_ref.shape[1] // N_GROUPS_B
    i = lax.broadcasted_iota(I32, (c, c), 0)
    j = lax.broadcasted_iota(I32, (c, c), 1)
    mask = (j <= i) & ((i // period) == (j // period))
    b = b_ref[...]
    for g in range(N_GROUPS_B):
        gs = slice(g * gd, (g + 1) * gd)
        wg = jnp.where(mask, w_ref[g], 0.0).astype(BF16)
        vg = [v_ref[ch * c:(ch + 1) * c, gs].astype(BF16) for ch in range(nch)]
        vg = vg[0] if nch == 1 else jnp.concatenate(vg, axis=1)
        sg = jnp.dot(wg, vg, preferred_element_type=F32) + b[:, g:g + 1]
        for ch in range(nch):
            u = u_ref[ch * c:(ch + 1) * c, gs].astype(F32)
            o_ref[ch * c:(ch + 1) * c, gs] = (u * sg[:, ch * gd:(ch + 1) * gd]).astype(o_ref.dtype)


def _gmlp(u, v, w, b, tm, period):
    m, wb = u.shape
    c = w.shape[1]
    return pl.pallas_call(
        functools.partial(_gmlp_body, period=period), name="gmlp", grid=(m // tm,),
        in_specs=[pl.BlockSpec((tm, wb), lambda i: (i, 0)), pl.BlockSpec((tm, wb), lambda i: (i, 0)),
                  pl.BlockSpec((N_GROUPS_B, c, c), lambda i: (0, 0, 0)),
                  pl.BlockSpec((c, N_GROUPS_B), lambda i: (0, 0))],
        out_specs=pl.BlockSpec((tm, wb), lambda i: (i, 0)),
        out_shape=jax.ShapeDtypeStruct((m, wb), BF16), compiler_params=_cparams(1),
    )(u, v, w, b)


def _memattn_body(q_ref, kv_ref, o_ref):
    wm = N_HEADS_M * HEAD_DIM_M
    q = q_ref[...].astype(BF16)
    outs = []
    for h in range(N_HEADS_M):
        hs = slice(h * HEAD_DIM_M, (h + 1) * HEAD_DIM_M)
        k = kv_ref[:, hs].astype(BF16)
        v = kv_ref[:, wm + h * HEAD_DIM_M:wm + (h + 1) * HEAD_DIM_M].astype(BF16)
        s = lax.dot_general(q[:, hs], k, (((1,), (1,)), ((), ())), preferred_element_type=F32)
        m = jnp.max(s, axis=-1, keepdims=True)
        p = jnp.exp(s - m)
        l = jnp.sum(p, axis=-1, keepdims=True)
        outs.append(jnp.dot(p.astype(BF16), v, preferred_element_type=F32) / l)
    o_ref[...] = jnp.concatenate(outs, axis=1).astype(o_ref.dtype)


def _memattn(q, kv, tq, out_dtype):
    m, wm = q.shape
    n, nm, _ = kv.shape
    per = m // n // tq
    return pl.pallas_call(
        _memattn_body, name="memattn", grid=(m // tq,),
        in_specs=[pl.BlockSpec((tq, wm), lambda i: (i, 0)),
                  pl.BlockSpec((None, nm, 2 * wm), lambda i: (i // per, 0, 0))],
        out_specs=pl.BlockSpec((tq, wm), lambda i: (i, 0)),
        out_shape=jax.ShapeDtypeStruct((m, wm), out_dtype), compiler_params=_cparams(1),
    )(q, kv)


def _mix_body(gt_ref, o0_ref, o1_ref, o2_ref, l0_ref, l1_ref, l2_ref, ob_ref, om_ref,
              wpa_ref, wpb_ref, wpm_ref, z_ref):
    d = z_ref.shape[1]
    l0, l1, l2 = l0_ref[...], l1_ref[...], l2_ref[...]
    mx = jnp.maximum(jnp.maximum(l0, l1), l2)
    e0, e1, e2 = jnp.exp(l0 - mx), jnp.exp(l1 - mx), jnp.exp(l2 - mx)
    den = e0 + e1 + e2
    w0, w1, w2 = e0 / den, e1 / den, e2 / den
    cols = []
    for h in range(HEADS_PER_GROUP_A):
        hs = slice(h * HEAD_DIM_A, (h + 1) * HEAD_DIM_A)
        cols.append(w0[:, h:h + 1] * o0_ref[:, hs].astype(F32) + w1[:, h:h + 1] * o1_ref[:, hs].astype(F32)
                    + w2[:, h:h + 1] * o2_ref[:, hs].astype(F32))
    oa = jnp.concatenate(cols, axis=1).astype(BF16)
    pa = jnp.dot(oa, wpa_ref[...], preferred_element_type=F32)
    pb = jnp.dot(ob_ref[...].astype(BF16), wpb_ref[...], preferred_element_type=F32)
    pm = jnp.dot(om_ref[...].astype(BF16), wpm_ref[...], preferred_element_type=F32)
    z = (gt_ref[:, 0:d].astype(F32) * pa + gt_ref[:, d:2 * d].astype(F32) * pb
         + gt_ref[:, 2 * d:3 * d].astype(F32) * pm)
    z_ref[...] = z.astype(z_ref.dtype)


def _mix(gates, o_list, lse_list, ob, om, wpa, wpb, wpm, tm):
    m = gates.shape[0]
    d = wpa.shape[1]

    def rows(width):
        return pl.BlockSpec((tm, width), lambda i: (i, 0))

    def whole(a):
        return pl.BlockSpec(a.shape, lambda i: (0, 0))

    return pl.pallas_call(
        _mix_body, name="mix", grid=(m // tm,),
        in_specs=[rows(gates.shape[1])] + [rows(GROUP_W_A)] * 3 + [rows(LANES)] * 3
                 + [rows(ob.shape[1]), rows(om.shape[1]), whole(wpa), whole(wpb), whole(wpm)],
        out_specs=rows(d),
        out_shape=jax.ShapeDtypeStruct((m, d), BF16), compiler_params=_cparams(1),
    )(gates, *o_list, *lse_list, ob, om, wpa, wpb, wpm)


def _route(logits):
    lane = lax.broadcasted_iota(I32, logits.shape, 1)
    lane_f = lane.astype(F32)
    is_g = lane < N_EXPERT_GROUPS
    gmax = jnp.max(jnp.where(is_g, logits, -jnp.inf), axis=1, keepdims=True)
    gsel = jnp.min(jnp.where(is_g & (logits == gmax), lane_f, float(LANES)), axis=1, keepdims=True).astype(I32)
    gden = jnp.sum(jnp.where(is_g, jnp.exp(logits - gmax), 0.0), axis=1, keepdims=True)
    pg = 1.0 / gden
    e_lane = lane - N_EXPERT_GROUPS
    in_grp = (e_lane >= 0) & (e_lane < N_EXPERTS) & ((e_lane // EXPERTS_PER_GROUP) == gsel)
    m1 = jnp.max(jnp.where(in_grp, logits, -jnp.inf), axis=1, keepdims=True)
    i1 = jnp.min(jnp.where(in_grp & (logits == m1), lane_f, float(LANES)), axis=1, keepdims=True).astype(I32)
    rest = in_grp & (lane != i1)
    m2 = jnp.max(jnp.where(rest, logits, -jnp.inf), axis=1, keepdims=True)
    i2 = jnp.min(jnp.where(rest & (logits == m2), lane_f, float(LANES)), axis=1, keepdims=True).astype(I32)
    e2 = jnp.exp(m2 - m1)
    w1 = pg / (1.0 + e2)
    w2 = pg * e2 / (1.0 + e2)
    eid = jnp.where(lane == 0, i1 - N_EXPERT_GROUPS, jnp.where(lane == 1, i2 - N_EXPERT_GROUPS, 0))
    wts = jnp.where(lane == 0, w1, jnp.where(lane == 1, w2, 0.0))
    return eid, wts


def _resid_body(x_ref, z_ref, wo_ref, gf_ref, wr_ref, br_ref, xmid_ref, hpk_ref, eid_ref, wts_ref):
    xm = x_ref[...] + jnp.dot(z_ref[...], wo_ref[...], preferred_element_type=F32)
    xmid_ref[...] = xm
    hf = _rms(xm, gf_ref[...])
    logits = jnp.dot(hf, wr_ref[...], preferred_element_type=F32, precision=lax.Precision.HIGHEST) + br_ref[...]
    eid, wts = _route(logits)
    eid_ref[...] = eid
    wts_ref[...] = wts
    hpk_ref[...] = _pack_bf16_pair(hf)


def _resid(x2d, z, wo, g_ffn, w_r, b_r, tm):
    m, d = x2d.shape

    def rows(width):
        return pl.BlockSpec((tm, width), lambda i: (i, 0))

    def whole(a):
        return pl.BlockSpec(a.shape, lambda i: (0, 0))

    gf = g_ffn.reshape(1, d)
    return pl.pallas_call(
        _resid_body, name="resid", grid=(m // tm,),
        in_specs=[rows(d), rows(d), whole(wo), whole(gf), whole(w_r), whole(b_r)],
        out_specs=[rows(d), rows(d // 2), rows(LANES), rows(LANES)],
        out_shape=[jax.ShapeDtypeStruct((m, d), F32), jax.ShapeDtypeStruct((m, d // 2), U32),
                   jax.ShapeDtypeStruct((m, LANES), I32), jax.ShapeDtypeStruct((m, LANES), F32)],
        compiler_params=_cparams(1),
    )(x2d, z, wo, gf, w_r, b_r)


def _gather_rows(idx_ref, src_hbm, dst, sem, n):
    def body(i, carry):
        pltpu.make_async_copy(src_hbm.at[pl.ds(idx_ref[0, 0, i], 1)], dst.at[pl.ds(i, 1)], sem).start()
        return carry
    lax.fori_loop(0, n, body, 0)


def _ffn_body(blk_e_ref, nused_ref, idx_cur_ref, idx_nxt_ref, h_hbm, wg_ref, wu_ref, wd_ref, y_ref, buf, sem):
    j = pl.program_id(0)
    nused = nused_ref[0]
    slot = j % 2
    rows = buf.shape[1]

    @pl.when(j == 0)
    def _():
        _gather_rows(idx_cur_ref, h_hbm, buf.at[0], sem.at[0], rows)

    @pl.when(j + 1 < nused)
    def _():
        _gather_rows(idx_nxt_ref, h_hbm, buf.at[1 - slot], sem.at[1 - slot], rows)

    @pl.when(j < nused)
    def _():
        pltpu.make_async_copy(h_hbm.at[pl.ds(0, rows)], buf.at[slot], sem.at[slot]).wait()
        lo, hi = _unpack_bf16_pair(buf[slot])
        x = jnp.concatenate([lo.astype(BF16), hi.astype(BF16)], axis=1)
        a = jnp.dot(x, wg_ref[...], preferred_element_type=F32)
        b = jnp.dot(x, wu_ref[...], preferred_element_type=F32)
        hm = (a * jax.nn.sigmoid(a) * b).astype(BF16)
        y_ref[...] = _pack_bf16_pair(jnp.dot(hm, wd_ref[...], preferred_element_type=F32))

    @pl.when(j >= nused)
    def _():
        y_ref[...] = jnp.zeros(y_ref.shape, y_ref.dtype)


def _ffn(hpk, src_tok, blk_e, nused, wg, wu, wd):
    n_blocks = blk_e.shape[0]
    rows = MOE_ROWS
    half = hpk.shape[1]
    d, de = wg.shape[1], wg.shape[2]
    idx3 = src_tok.reshape(n_blocks, 1, rows)
    grid_spec = pltpu.PrefetchScalarGridSpec(
        num_scalar_prefetch=2, grid=(n_blocks,),
        in_specs=[
            pl.BlockSpec((1, 1, rows), lambda j, be, nu: (j, 0, 0), memory_space=pltpu.SMEM),
            pl.BlockSpec((1, 1, rows), lambda j, be, nu: (jnp.minimum(j + 1, n_blocks - 1), 0, 0),
                         memory_space=pltpu.SMEM),
            pl.BlockSpec(memory_space=pl.ANY),
            pl.BlockSpec((None, d, de), lambda j, be, nu: (be[j], 0, 0)),
            pl.BlockSpec((None, d, de), lambda j, be, nu: (be[j], 0, 0)),
            pl.BlockSpec((None, de, d), lambda j, be, nu: (be[j], 0, 0)),
        ],
        out_specs=pl.BlockSpec((rows, half), lambda j, be, nu: (j, 0)),
        scratch_shapes=[pltpu.VMEM((2, rows, half), U32), pltpu.SemaphoreType.DMA((2,))],
    )
    return pl.pallas_call(
        _ffn_body, name="ffn", grid_spec=grid_spec,
        out_shape=jax.ShapeDtypeStruct((n_blocks * rows, half), U32), compiler_params=_cparams(1),
    )(blk_e, nused, idx3, idx3, hpk, wg, wu, wd)


def _combine_body(idx_cur_ref, idx_nxt_ref, x_ref, w_ref, y_hbm, o_ref, buf, sem, *, n):
    i = pl.program_id(0)
    slot = i % 2
    rows = buf.shape[1]
    tm = rows // TOP_K

    @pl.when(i == 0)
    def _():
        _gather_rows(idx_cur_ref, y_hbm, buf.at[0], sem.at[0], rows)

    if n > 1:
        @pl.when(i + 1 < n)
        def _():
            _gather_rows(idx_nxt_ref, y_hbm, buf.at[1 - slot], sem.at[1 - slot], rows)

    pltpu.make_async_copy(y_hbm.at[pl.ds(0, rows)], buf.at[slot], sem.at[slot]).wait()
    w = w_ref[...]
    lo0, hi0 = _unpack_bf16_pair(buf[slot, 0:tm])
    lo1, hi1 = _unpack_bf16_pair(buf[slot, tm:2 * tm])
    y0 = jnp.concatenate([lo0, hi0], axis=1)
    y1 = jnp.concatenate([lo1, hi1], axis=1)
    o_ref[...] = x_ref[...] + (w[:, 0:1] * y0 + w[:, 1:2] * y1)


def _combine(xmid, wts, dest, ypk, tm):
    m, d = xmid.shape
    nt = m // tm
    idx3 = dest.reshape(nt, tm, TOP_K).transpose(0, 2, 1).reshape(nt, 1, TOP_K * tm)
    return pl.pallas_call(
        functools.partial(_combine_body, n=nt), name="combine", grid=(nt,),
        in_specs=[
            pl.BlockSpec((1, 1, TOP_K * tm), lambda i: (i, 0, 0), memory_space=pltpu.SMEM),
            pl.BlockSpec((1, 1, TOP_K * tm), lambda i: (jnp.minimum(i + 1, nt - 1), 0, 0), memory_space=pltpu.SMEM),
            pl.BlockSpec((tm, d), lambda i: (i, 0)),
            pl.BlockSpec((tm, LANES), lambda i: (i, 0)),
            pl.BlockSpec(memory_space=pl.ANY),
        ],
        out_specs=pl.BlockSpec((tm, d), lambda i: (i, 0)),
        out_shape=jax.ShapeDtypeStruct((m, d), F32),
        scratch_shapes=[pltpu.VMEM((2, TOP_K * tm, d // 2), U32), pltpu.SemaphoreType.DMA((2,))],
        compiler_params=_cparams(1),
    )(idx3, idx3, xmid, wts, ypk)


def _moe_plan(eid2):
    n_slot = eid2.shape[0] * TOP_K
    eid = eid2.reshape(n_slot)
    onehot = (eid[:, None] == jnp.arange(N_EXPERTS, dtype=I32)[None, :]).astype(I32)
    csum = jnp.cumsum(onehot, axis=0)
    rank = jnp.take_along_axis(csum, eid[:, None], axis=1)[:, 0] - 1
    counts = csum[-1]
    padded = (counts + MOE_ROWS - 1) // MOE_ROWS * MOE_ROWS
    pend = jnp.cumsum(padded)
    pstart = pend - padded
    dest = pstart[eid] + rank
    n_blocks = -(-n_slot // MOE_ROWS) + N_EXPERTS
    nused = (pend[-1] // MOE_ROWS).astype(I32)
    blk = jnp.minimum(jnp.arange(n_blocks, dtype=I32), nused - 1)
    blk_e = jnp.minimum(jnp.searchsorted(pend, blk * MOE_ROWS, side="right"), N_EXPERTS - 1).astype(I32)
    src_tok = jnp.zeros((n_blocks * MOE_ROWS,), I32).at[dest].set(jnp.arange(n_slot, dtype=I32) // TOP_K)
    return dest.astype(I32), src_tok, blk_e, nused.reshape(1)


def _mixer(x2d, tm, p, attn_fn, gmlp_fn, mem_fn):
    d = x2d.shape[1]
    h = _prep(x2d, p["g_mix"], tm)
    qdt = p["q_dtype"]
    q = _proj(h, p["w_q"], 512, tm, "norm", qdt, p["g_qa"], HEAD_DIM_A, HEAD_DIM_A ** -0.5, name="proj_q")
    kvs = [_proj(h, p["w_kv"][g], GROUP_W_A, tm, "norm_first", F32, p["g_ka"], HEAD_DIM_A, name=f"proj_kv{g}")
           for g in range(len(DIL_GROUPS))]
    ub = _proj(h, p["w_ub"], 512, tm, "none", BF16, name="proj_ub")
    vb = _proj(h, p["w_vb"], p["w_vb"].shape[1], tm, "norm", p["vb_dtype"], p["g_vb"], p["w_vb"].shape[1],
               name="proj_vb")
    qm = _proj(h, p["w_qm"], 512, tm, "norm", qdt, p["g_qm"], HEAD_DIM_M, HEAD_DIM_M ** -0.5, name="proj_qm")
    gates = _proj(h, p["w_gt"], 1024, tm, "sigmoid", BF16, name="proj_gates")
    o_list, lse_list = attn_fn(q, kvs)
    ob = gmlp_fn(ub, vb)
    om = mem_fn(qm)
    z = _mix(gates, o_list, lse_list, ob, om, p["w_pa"], p["w_pb"], p["w_pm"], min(tm, 256))
    xmid, hpk, eid, wts = _resid(x2d, z, p["w_o"], p["g_ffn"], p["w_r"], p["b_r"], min(tm, 256))
    return kvs, vb, xmid, hpk, eid, wts


def kernel(x_prompt, x_sample, mem_prompt, cache_a0_kv, cache_a1_kv, cache_a2_kv, cache_mem_kv, rel_bias, g_mix, w_in, g_qa, g_ka, w_pa, g_vb, w_s, b_s, w_pb, g_mem, w_mk, w_mv, g_qm, g_km, w_pm, w_o, g_ffn, w_rg, b_rg, w_re, b_re, w_gate, w_up, w_down):
    n_b, seq, d = x_prompt.shape
    n_s, t_s, _ = x_sample.shape
    depth = w_in.shape[0]
    assert depth == 1
    l = 0
    caches = (cache_a0_kv, cache_a1_kv, cache_a2_kv)
    n_g = len(DIL_GROUPS)
    wb = w_pb.shape[1]
    wm = N_HEADS_M * HEAD_DIM_M

    wi = w_in[l]
    off_u = 3 * WIDTH_A
    off_v = off_u + wb
    off_qm = off_v + wb
    off_gt = off_qm + wm
    p = {
        "g_mix": g_mix[l], "g_qa": g_qa[l], "g_ka": g_ka[l], "g_vb": g_vb[l], "g_qm": g_qm[l], "g_ffn": g_ffn[l],
        "w_q": wi[:, :WIDTH_A].astype(BF16),
        "w_kv": [jnp.concatenate([wi[:, WIDTH_A + g * GROUP_W_A:WIDTH_A + (g + 1) * GROUP_W_A],
                                  wi[:, 2 * WIDTH_A + g * GROUP_W_A:2 * WIDTH_A + (g + 1) * GROUP_W_A]],
                                 axis=1).astype(BF16) for g in range(n_g)],
        "w_ub": wi[:, off_u:off_v].astype(BF16),
        "w_vb": wi[:, off_v:off_qm].astype(BF16),
        "w_qm": wi[:, off_qm:off_gt].astype(BF16),
        "w_gt": wi[:, off_gt:].astype(BF16),
        "w_pa": w_pa[l].astype(BF16), "w_pb": w_pb[l].astype(BF16), "w_pm": w_pm[l].astype(BF16),
        "w_o": w_o[l].astype(BF16),
    }
    n_r = N_EXPERT_GROUPS + N_EXPERTS
    p["w_r"] = jnp.concatenate([w_rg[l], w_re[l], jnp.zeros((d, LANES - n_r), F32)], axis=1)
    p["b_r"] = jnp.concatenate([b_rg[l], b_re[l], jnp.zeros((LANES - n_r,), F32)]).reshape(1, LANES)

    n_mem = mem_prompt.shape[1]
    h_mem = _prep(mem_prompt.reshape(n_b * n_mem, d), g_mem[l], 256)
    w_mkv = jnp.concatenate([w_mk[l], w_mv[l]], axis=1).astype(BF16)
    mkv = _proj(h_mem, w_mkv, wm, 256, "norm_first", F32, g_km[l], HEAD_DIM_M, name="proj_mkv")
    new_mem_p = mkv.reshape(1, n_b, n_mem, 2, N_HEADS_M, HEAD_DIM_M)

    pp = dict(p, q_dtype=BF16, vb_dtype=BF16)

    def attn_prompt(q, kvs):
        res = [_attn_prompt(q, kvs[g], rel_bias, g, n_b, seq) for g in range(n_g)]
        return [r[0] for r in res], [r[1] for r in res]

    bs_t = b_s[l].T
    kvs_p, _, xmid_p, hpk_p, eid_p, wts_p = _mixer(
        x_prompt.reshape(n_b * seq, d), 512, pp, attn_prompt,
        lambda u, v: _gmlp(u, v, w_s[l], bs_t, 512, CHUNK),
        lambda qm: _memattn(qm, mkv.reshape(n_b, n_mem, 2 * wm), 256, BF16))
    new_p = []
    for g, (win, _) in enumerate(DIL_GROUPS):
        keep = min(win, seq)
        kv5 = kvs_p[g].reshape(n_b, seq, 2, HEADS_PER_GROUP_A, HEAD_DIM_A)
        new_p.append((kv5 if keep == seq else kv5[:, seq - keep:])[None])

    ps = dict(p, q_dtype=F32, vb_dtype=F32)
    m_s = n_s * t_s

    def attn_sample(q, kvs):
        res = [_attn_sample(q, kvs[g], caches[g][l].reshape(n_s, caches[g].shape[2], 2 * GROUP_W_A),
                            rel_bias, g, t_s) for g in range(n_g)]
        return [r[0] for r in res], [r[1] for r in res]

    w_s_small = jnp.tile(w_s[l][:, :t_s, :t_s], (1, n_s, n_s))
    b_s_small = jnp.tile(bs_t[:t_s], (n_s, 1))
    kvs_s, vb_s, xmid_s, hpk_s, eid_s, wts_s = _mixer(
        x_sample.reshape(m_s, d), m_s, ps, attn_sample,
        lambda u, v: _gmlp(u, v, w_s_small, b_s_small, m_s, t_s),
        lambda qm: _memattn(qm, cache_mem_kv[l].reshape(n_s, n_mem, 2 * wm), t_s, F32))
    new_s = []
    for g, (win, _) in enumerate(DIL_GROUPS):
        kv_all = jnp.concatenate(
            [caches[g][l], kvs_s[g].reshape(n_s, t_s, 2, HEADS_PER_GROUP_A, HEAD_DIM_A)], axis=1)
        new_s.append(kv_all[:, kv_all.shape[1] - min(win, kv_all.shape[1]):][None])
    new_vb_s = vb_s.reshape(1, n_s, t_s, wb)

    n_p = n_b * seq
    hpk = jnp.concatenate([hpk_p, hpk_s], axis=0)
    eid2 = jnp.concatenate([eid_p[:, :TOP_K], eid_s[:, :TOP_K]], axis=0)
    dest, src_tok, blk_e, nused = _moe_plan(eid2)
    ypk = _ffn(hpk, src_tok, blk_e, nused, w_gate[l].astype(BF16), w_up[l].astype(BF16), w_down[l].astype(BF16))
    dest2 = dest.reshape(-1, TOP_K)
    y_p = _combine(xmid_p, wts_p, dest2[:n_p], ypk, 256)
    y_s = _combine(xmid_s, wts_s, dest2[n_p:], ypk, m_s)

    return (y_p.reshape(n_b, seq, d), y_s.reshape(n_s, t_s, d), new_p[0], new_p[1], new_p[2], new_mem_p,
            new_s[0], new_s[1], new_s[2], new_vb_s)
```

```python
import functools
import math

import jax
import jax.numpy as jnp
from jax import lax
from jax.experimental import pallas as pl
from jax.experimental.pallas import tpu as pltpu

F32 = jnp.float32
BF16 = jnp.bfloat16
I32 = jnp.int32
U32 = jnp.uint32

EPS = 1e-6
NEG_INF = -1e30

HEAD_DIM_A = 128
HEADS_PER_GROUP_A = 4
DIL_GROUPS = ((128, 1), (512, 4), (2048, 16))
GROUP_W_A = HEADS_PER_GROUP_A * HEAD_DIM_A
WIDTH_A = len(DIL_GROUPS) * GROUP_W_A
CHUNK = 128
N_GROUPS_B = 8
N_HEADS_M = 4
HEAD_DIM_M = 256
N_BUCKETS = 32
MAX_EXACT = N_BUCKETS // 2
MAX_DISTANCE = 2048
N_EXPERT_GROUPS = 4
EXPERTS_PER_GROUP = 8
N_EXPERTS = N_EXPERT_GROUPS * EXPERTS_PER_GROUP
TOP_K = 2
LANES = 128
MOE_ROWS = 256
VMEM_LIMIT = 56 * 1024 * 1024


def _cparams(n_grid, vmem=VMEM_LIMIT):
    return pltpu.CompilerParams(dimension_semantics=("arbitrary",) * n_grid, vmem_limit_bytes=vmem)


def _rms(x, g):
    return x * lax.rsqrt(jnp.mean(x * x, axis=-1, keepdims=True) + EPS) * g


def _pack_bf16_pair(x):
    n = x.shape[1] // 2
    lo = lax.bitcast_convert_type(x[:, :n].astype(BF16).astype(F32), U32)
    hi = lax.bitcast_convert_type(x[:, n:].astype(BF16).astype(F32), U32)
    return (hi & jnp.uint32(0xFFFF0000)) | (lo >> 16)


def _unpack_bf16_pair(w):
    lo = lax.bitcast_convert_type(w << 16, F32)
    hi = lax.bitcast_convert_type(w & jnp.uint32(0xFFFF0000), F32)
    return lo, hi


def _store_token_tiles(ref, x):
    w = _pack_bf16_pair(x)
    m, n = w.shape
    nt = n // LANES
    for c in range(nt):
        ref[pl.ds(c, m, stride=nt), :] = w[:, c * LANES:(c + 1) * LANES]


def _load_token_tiles(ref, lead, row0, m, nt):
    los, his = [], []
    for c in range(nt):
        lo, hi = _unpack_bf16_pair(ref[lead, pl.ds(row0 * nt + c, m, stride=nt), :])
        los.append(lo)
        his.append(hi)
    return jnp.concatenate(los + his, axis=1)


def _rel_bucket(dist):
    d = jnp.maximum(dist, 1).astype(F32)
    large = MAX_EXACT + (jnp.log(d / MAX_EXACT) / math.log(MAX_DISTANCE / MAX_EXACT)
                         * (N_BUCKETS - MAX_EXACT)).astype(I32)
    return jnp.where(dist < MAX_EXACT, dist, jnp.minimum(large, N_BUCKETS - 1)).astype(I32)


def _prep_body(x_ref, g_ref, o_ref):
    o_ref[...] = _rms(x_ref[...], g_ref[...]).astype(o_ref.dtype)


def _prep(x2d, g, tm):
    m, d = x2d.shape
    return pl.pallas_call(
        _prep_body, name="prep", grid=(m // tm,),
        in_specs=[pl.BlockSpec((tm, d), lambda i: (i, 0)), pl.BlockSpec((1, d), lambda i: (0, 0))],
        out_specs=pl.BlockSpec((tm, d), lambda i: (i, 0)),
        out_shape=jax.ShapeDtypeStruct((m, d), BF16), compiler_params=_cparams(1),
    )(x2d, g.reshape(1, d))


def _headnorm(acc, g, hd, scale):
    outs = []
    for j in range(acc.shape[1] // hd):
        sl = acc[:, j * hd:(j + 1) * hd]
        outs.append(sl * lax.rsqrt(jnp.mean(sl * sl, axis=-1, keepdims=True) + EPS))
    y = outs[0] if len(outs) == 1 else jnp.concatenate(outs, axis=1)
    y = y * g
    return y * scale if scale != 1.0 else y


def _proj_body(h_ref, w_ref, g_ref, o_ref, *, mode, hd, scale):
    acc = jnp.dot(h_ref[...], w_ref[...], preferred_element_type=F32)
    if mode == "none":
        o_ref[...] = acc.astype(o_ref.dtype)
    elif mode == "sigmoid":
        o_ref[...] = jax.nn.sigmoid(acc).astype(o_ref.dtype)
    elif mode == "norm":
        o_ref[...] = _headnorm(acc, g_ref[...], hd, scale).astype(o_ref.dtype)
    else:
        @pl.when(pl.program_id(0) == 0)
        def _():
            o_ref[...] = _headnorm(acc, g_ref[...], hd, scale).astype(o_ref.dtype)

        @pl.when(pl.program_id(0) != 0)
        def _():
            o_ref[...] = acc.astype(o_ref.dtype)


def _proj(h, w, tn, tm, mode, out_dtype, gain=None, hd=None, scale=1.0, name="proj"):
    m, k = h.shape
    n = w.shape[1]
    if gain is None:
        g = jnp.ones((1, tn), F32)
    else:
        g = jnp.tile(gain.astype(F32).reshape(1, -1), (1, tn // gain.shape[-1]))
    body = functools.partial(_proj_body, mode=mode, hd=hd, scale=scale)
    return pl.pallas_call(
        body, name=name, grid=(n // tn, m // tm),
        in_specs=[pl.BlockSpec((tm, k), lambda j, i: (i, 0)),
                  pl.BlockSpec((k, tn), lambda j, i: (0, j)),
                  pl.BlockSpec((1, tn), lambda j, i: (0, 0))],
        out_specs=pl.BlockSpec((tm, tn), lambda j, i: (i, j)),
        out_shape=jax.ShapeDtypeStruct((m, n), out_dtype), compiler_params=_cparams(2),
    )(h, w, g)


def _bias_from_table(tab_ref, bidx, col):
    acc = jnp.zeros(bidx.shape, F32)
    for kb in range(N_BUCKETS):
        acc = jnp.where(bidx == kb, tab_ref[kb, col], acc)
    return acc


def _attn_p_body(tab_ref, bidx_ref, q_ref, kp_ref, kc_ref, vp_ref, vc_ref, o_ref, lse_ref, bias_scr, *, g):
    first = (pl.program_id(0) == 0) & (pl.program_id(1) == 0) & (pl.program_id(2) == 0)

    @pl.when(first)
    def _():
        bidx = bidx_ref[...]
        for h in range(HEADS_PER_GROUP_A):
            bias_scr[h] = _bias_from_table(tab_ref, bidx, g * HEADS_PER_GROUP_A + h)

    qb = pl.program_id(2)
    blk = q_ref.shape[0]
    q = q_ref[...]
    k = jnp.concatenate([kp_ref[...], kc_ref[...]], axis=0).astype(BF16)
    v = jnp.concatenate([vp_ref[...], vc_ref[...]], axis=0).astype(BF16)
    row = lax.broadcasted_iota(I32, (blk, 2 * blk), 0)
    col = lax.broadcasted_iota(I32, (blk, 2 * blk), 1)
    dist = blk + row - col
    valid = (dist >= 0) & (dist <= blk) & ((qb > 0) | (col >= blk))
    lane = lax.broadcasted_iota(I32, (blk, LANES), 1)
    lse_blk = jnp.zeros((blk, LANES), F32)
    outs = []
    for h in range(HEADS_PER_GROUP_A):
        hs = slice(h * HEAD_DIM_A, (h + 1) * HEAD_DIM_A)
        s = lax.dot_general(q[:, hs], k[:, hs], (((1,), (1,)), ((), ())), preferred_element_type=F32)
        s = jnp.where(valid, s + bias_scr[h], NEG_INF)
        m = jnp.max(s, axis=-1, keepdims=True)
        p = jnp.exp(s - m)
        l = jnp.sum(p, axis=-1, keepdims=True)
        o = jnp.dot(p.astype(BF16), v[:, hs], preferred_element_type=F32) / l
        outs.append(o)
        lse_blk = jnp.where(lane == h, m + jnp.log(l), lse_blk)
    o_ref[...] = jnp.concatenate(outs, axis=1).astype(o_ref.dtype)
    lse_ref[...] = lse_blk


def _attn_prompt(q, kv, rel_bias, g, n_batch, seq):
    win, dil = DIL_GROUPS[g]
    blk = win // dil
    ln = seq // dil
    nb = ln // blk
    rows = n_batch * seq // dil
    a = jnp.arange(blk)[:, None]
    c = jnp.arange(2 * blk)[None, :]
    bidx = _rel_bucket(jnp.maximum(blk + a - c, 0) * dil)
    qv = q.reshape(rows, dil * WIDTH_A)
    kvv = kv.reshape(rows, dil * 2 * GROUP_W_A)
    n_qc = WIDTH_A // GROUP_W_A

    def cur(b, r, j):
        return b * nb + j

    def prev(b, r, j):
        return b * nb + jnp.maximum(j - 1, 0)

    o, lse = pl.pallas_call(
        functools.partial(_attn_p_body, g=g), name=f"attn_p{g}", grid=(n_batch, dil, nb),
        in_specs=[
            pl.BlockSpec(memory_space=pltpu.SMEM),
            pl.BlockSpec((blk, 2 * blk), lambda b, r, j: (0, 0)),
            pl.BlockSpec((blk, GROUP_W_A), lambda b, r, j: (cur(b, r, j), r * n_qc + g)),
            pl.BlockSpec((blk, GROUP_W_A), lambda b, r, j: (prev(b, r, j), r * 2)),
            pl.BlockSpec((blk, GROUP_W_A), lambda b, r, j: (cur(b, r, j), r * 2)),
            pl.BlockSpec((blk, GROUP_W_A), lambda b, r, j: (prev(b, r, j), r * 2 + 1)),
            pl.BlockSpec((blk, GROUP_W_A), lambda b, r, j: (cur(b, r, j), r * 2 + 1)),
        ],
        out_specs=[pl.BlockSpec((blk, GROUP_W_A), lambda b, r, j: (cur(b, r, j), r)),
                   pl.BlockSpec((blk, LANES), lambda b, r, j: (cur(b, r, j), r))],
        out_shape=[jax.ShapeDtypeStruct((rows, dil * GROUP_W_A), BF16),
                   jax.ShapeDtypeStruct((rows, dil * LANES), F32)],
        scratch_shapes=[pltpu.VMEM((HEADS_PER_GROUP_A, blk, 2 * blk), F32)],
        compiler_params=_cparams(3),
    )(rel_bias, bidx, qv, kvv, kvv, kvv, kvv)
    return o.reshape(n_batch * seq, GROUP_W_A), lse.reshape(n_batch * seq, LANES)


def _attn_s_body(tab_ref, bidx_ref, q_ref, kvn_ref, cache_hbm, o_ref, lse_ref, newc_hbm,
                 raw, kv_scr, bias_scr, sem_in, sem_out, *, g, dil, lb, n_req):
    t_new = q_ref.shape[0]
    nh = HEADS_PER_GROUP_A
    n_rows = nh * t_new
    n_keys = kv_scr.shape[0]
    n_col = 2 * nh
    lb8 = lb * n_col
    n = pl.program_id(0)
    slot = n % 2

    def fetch(req, s):
        return pltpu.make_async_copy(cache_hbm.at[req], raw.at[s, pl.ds(0, lb8)], sem_in.at[s])

    def flush(req, s):
        return pltpu.make_async_copy(raw.at[s, pl.ds(t_new * n_col, lb8)], newc_hbm.at[req], sem_out.at[s])

    @pl.when(n == 0)
    def _():
        kv_scr[lb:, :] = jnp.zeros((n_keys - lb, kv_scr.shape[1]), BF16)
        bidx = bidx_ref[...]
        for h in range(nh):
            bias_scr[h * t_new:(h + 1) * t_new, :] = _bias_from_table(tab_ref, bidx, g * nh + h)
        fetch(0, 0).start()

    @pl.when(n >= 1)
    def _():
        flush(n - 1, 1 - slot).wait()

    @pl.when(n + 1 < n_req)
    def _():
        fetch(n + 1, 1 - slot).start()

    fetch(n, slot).wait()
    new = kvn_ref[...]
    for c in range(n_col):
        raw[slot, pl.ds(lb8 + c, t_new, stride=n_col), :] = new[:, c * LANES:(c + 1) * LANES]
    flush(n, slot).start()

    rows_per = min(lb, 256)
    for c in range(n_col):
        for r0 in range(0, lb, rows_per):
            kv_scr[r0:r0 + rows_per, c * LANES:(c + 1) * LANES] = raw[
                slot, pl.ds(r0 * n_col + c, rows_per, stride=n_col), :].astype(BF16)
    kv_scr[lb:lb + 2 * t_new, :] = jnp.concatenate([new, jnp.zeros_like(new)], axis=0).astype(BF16)

    q = q_ref[...]
    qt = jnp.concatenate([q] * nh, axis=0)
    rr = lax.broadcasted_iota(I32, (n_rows, GROUP_W_A), 0)
    cc = lax.broadcasted_iota(I32, (n_rows, GROUP_W_A), 1)
    qbd = jnp.where(cc // HEAD_DIM_A == rr // t_new, qt, 0.0).astype(BF16)

    s = lax.dot_general(qbd, kv_scr[:, :GROUP_W_A], (((1,), (1,)), ((), ())), preferred_element_type=F32)
    row = lax.broadcasted_iota(I32, (n_rows, n_keys), 0)
    col = lax.broadcasted_iota(I32, (n_rows, n_keys), 1)
    delta = lb + (row & (t_new - 1)) - col
    valid = (delta >= 0) & (delta <= lb) & ((delta & (dil - 1)) == 0)
    s = jnp.where(valid, s + bias_scr[...], NEG_INF)
    m = jnp.max(s, axis=-1, keepdims=True)
    p = jnp.exp(s - m)
    l = jnp.sum(p, axis=-1, keepdims=True)
    o = jnp.dot(p.astype(BF16), kv_scr[:, GROUP_W_A:], preferred_element_type=F32) / l
    lse = m + jnp.log(l)
    lane = lax.broadcasted_iota(I32, (t_new, LANES), 1)
    lse_blk = jnp.zeros((t_new, LANES), F32)
    outs = []
    for h in range(nh):
        outs.append(o[h * t_new:(h + 1) * t_new, h * HEAD_DIM_A:(h + 1) * HEAD_DIM_A])
        lse_blk = jnp.where(lane == h, lse[h * t_new:(h + 1) * t_new, :], lse_blk)
    o_ref[...] = jnp.concatenate(outs, axis=1)
    lse_ref[...] = lse_blk

    @pl.when(n == n_req - 1)
    def _():
        flush(n, slot).wait()


def _attn_sample(q, kv_new, cache, rel_bias, g, t_new):
    win, dil = DIL_GROUPS[g]
    _, n_req, lb = cache.shape[:3]
    n_col = 2 * HEADS_PER_GROUP_A
    assert t_new & (t_new - 1) == 0 and dil & (dil - 1) == 0 and lb == win and cache.shape[3:] == (2, 4, LANES)
    n_keys = lb + LANES
    t = jnp.arange(t_new)[:, None]
    c = jnp.arange(n_keys)[None, :]
    bidx = _rel_bucket(jnp.clip(lb + t - c, 0, lb))
    o, lse, newc = pl.pallas_call(
        functools.partial(_attn_s_body, g=g, dil=dil, lb=lb, n_req=n_req), name=f"attn_s{g}", grid=(n_req,),
        in_specs=[
            pl.BlockSpec(memory_space=pltpu.SMEM),
            pl.BlockSpec((t_new, n_keys), lambda n: (0, 0)),
            pl.BlockSpec((t_new, GROUP_W_A), lambda n: (n, g)),
            pl.BlockSpec((t_new, 2 * GROUP_W_A), lambda n: (n, 0)),
            pl.BlockSpec(memory_space=pl.ANY),
        ],
        out_specs=[pl.BlockSpec((t_new, GROUP_W_A), lambda n: (n, 0)),
                   pl.BlockSpec((t_new, LANES), lambda n: (n, 0)),
                   pl.BlockSpec(memory_space=pl.ANY)],
        out_shape=[jax.ShapeDtypeStruct((n_req * t_new, GROUP_W_A), F32),
                   jax.ShapeDtypeStruct((n_req * t_new, LANES), F32),
                   jax.ShapeDtypeStruct((n_req, lb * n_col, LANES), F32)],
        scratch_shapes=[pltpu.VMEM((2, (lb + t_new) * n_col, LANES), F32),
                        pltpu.VMEM((n_keys, 2 * GROUP_W_A), BF16),
                        pltpu.VMEM((HEADS_PER_GROUP_A * t_new, n_keys), F32),
                        pltpu.SemaphoreType.DMA((2,)), pltpu.SemaphoreType.DMA((2,))],
        compiler_params=_cparams(1),
    )(rel_bias, bidx, q, kv_new, cache.reshape(n_req, lb * n_col, LANES))
    return o, lse, newc.reshape(cache.shape)


def _gmlp_body(u_ref, v_ref, w_ref, b_ref, o_ref, *, period):
    c = w_ref.shape[1]
    nch = u_ref.shape[0] // c
    gd = u_ref.shape[1] // N_GROUPS_B
    i = lax.broadcasted_iota(I32, (c, c), 0)
    j = lax.broadcasted_iota(I32, (c, c), 1)
    mask = (j <= i) & ((i // period) == (j // period))
    b = b_ref[...]
    for g in range(N_GROUPS_B):
        gs = slice(g * gd, (g + 1) * gd)
        wg = jnp.where(mask, w_ref[g], 0.0).astype(BF16)
        vg = [v_ref[ch * c:(ch + 1) * c, gs].astype(BF16) for ch in range(nch)]
        vg = vg[0] if nch == 1 else jnp.concatenate(vg, axis=1)
        sg = jnp.dot(wg, vg, preferred_element_type=F32) + b[:, g:g + 1]
        for ch in range(nch):
            u = u_ref[ch * c:(ch + 1) * c, gs].astype(F32)
            o_ref[ch * c:(ch + 1) * c, gs] = (u * sg[:, ch * gd:(ch + 1) * gd]).astype(o_ref.dtype)


def _gmlp(u, v, w, b, tm, period):
    m, wb = u.shape
    c = w.shape[1]
    return pl.pallas_call(
        functools.partial(_gmlp_body, period=period), name="gmlp", grid=(m // tm,),
        in_specs=[pl.BlockSpec((tm, wb), lambda i: (i, 0)), pl.BlockSpec((tm, wb), lambda i: (i, 0)),
                  pl.BlockSpec((N_GROUPS_B, c, c), lambda i: (0, 0, 0)),
                  pl.BlockSpec((c, N_GROUPS_B), lambda i: (0, 0))],
        out_specs=pl.BlockSpec((tm, wb), lambda i: (i, 0)),
        out_shape=jax.ShapeDtypeStruct((m, wb), BF16), compiler_params=_cparams(1),
    )(u, v, w, b)


def _memattn_body(q_ref, kv_ref, o_ref):
    wm = N_HEADS_M * HEAD_DIM_M
    q = q_ref[...].astype(BF16)
    outs = []
    for h in range(N_HEADS_M):
        hs = slice(h * HEAD_DIM_M, (h + 1) * HEAD_DIM_M)
        k = kv_ref[:, hs].astype(BF16)
        v = kv_ref[:, wm + h * HEAD_DIM_M:wm + (h + 1) * HEAD_DIM_M].astype(BF16)
        s = lax.dot_general(q[:, hs], k, (((1,), (1,)), ((), ())), preferred_element_type=F32)
        m = jnp.max(s, axis=-1, keepdims=True)
        p = jnp.exp(s - m)
        l = jnp.sum(p, axis=-1, keepdims=True)
        outs.append(jnp.dot(p.astype(BF16), v, preferred_element_type=F32) / l)
    o_ref[...] = jnp.concatenate(outs, axis=1).astype(o_ref.dtype)


def _memattn(q, kv, tq, out_dtype):
    m, wm = q.shape
    n, nm, _ = kv.shape
    per = m // n // tq
    return pl.pallas_call(
        _memattn_body, name="memattn", grid=(m // tq,),
        in_specs=[pl.BlockSpec((tq, wm), lambda i: (i, 0)),
                  pl.BlockSpec((None, nm, 2 * wm), lambda i: (i // per, 0, 0))],
        out_specs=pl.BlockSpec((tq, wm), lambda i: (i, 0)),
        out_shape=jax.ShapeDtypeStruct((m, wm), out_dtype), compiler_params=_cparams(1),
    )(q, kv)


def _mix_body(gt_ref, o0_ref, o1_ref, o2_ref, l0_ref, l1_ref, l2_ref, ob_ref, om_ref,
              wpa_ref, wpb_ref, wpm_ref, z_ref):
    d = z_ref.shape[1]
    l0, l1, l2 = l0_ref[...], l1_ref[...], l2_ref[...]
    mx = jnp.maximum(jnp.maximum(l0, l1), l2)
    e0, e1, e2 = jnp.exp(l0 - mx), jnp.exp(l1 - mx), jnp.exp(l2 - mx)
    den = e0 + e1 + e2
    w0, w1, w2 = e0 / den, e1 / den, e2 / den
    cols = []
    for h in range(HEADS_PER_GROUP_A):
        hs = slice(h * HEAD_DIM_A, (h + 1) * HEAD_DIM_A)
        cols.append(w0[:, h:h + 1] * o0_ref[:, hs].astype(F32) + w1[:, h:h + 1] * o1_ref[:, hs].astype(F32)
                    + w2[:, h:h + 1] * o2_ref[:, hs].astype(F32))
    oa = jnp.concatenate(cols, axis=1).astype(BF16)
    pa = jnp.dot(oa, wpa_ref[...], preferred_element_type=F32)
    pb = jnp.dot(ob_ref[...].astype(BF16), wpb_ref[...], preferred_element_type=F32)
    pm = jnp.dot(om_ref[...].astype(BF16), wpm_ref[...], preferred_element_type=F32)
    z = (gt_ref[:, 0:d].astype(F32) * pa + gt_ref[:, d:2 * d].astype(F32) * pb
         + gt_ref[:, 2 * d:3 * d].astype(F32) * pm)
    z_ref[...] = z.astype(z_ref.dtype)


def _mix(gates, o_list, lse_list, ob, om, wpa, wpb, wpm, tm):
    m = gates.shape[0]
    d = wpa.shape[1]

    def rows(width):
        return pl.BlockSpec((tm, width), lambda i: (i, 0))

    def whole(a):
        return pl.BlockSpec(a.shape, lambda i: (0, 0))

    return pl.pallas_call(
        _mix_body, name="mix", grid=(m // tm,),
        in_specs=[rows(gates.shape[1])] + [rows(GROUP_W_A)] * 3 + [rows(LANES)] * 3
                 + [rows(ob.shape[1]), rows(om.shape[1]), whole(wpa), whole(wpb), whole(wpm)],
        out_specs=rows(d),
        out_shape=jax.ShapeDtypeStruct((m, d), BF16), compiler_params=_cparams(1),
    )(gates, *o_list, *lse_list, ob, om, wpa, wpb, wpm)


def _route(logits):
    lane = lax.broadcasted_iota(I32, logits.shape, 1)
    lane_f = lane.astype(F32)
    is_g = lane < N_EXPERT_GROUPS
    gmax = jnp.max(jnp.where(is_g, logits, -jnp.inf), axis=1, keepdims=True)
    gsel = jnp.min(jnp.where(is_g & (logits == gmax), lane_f, float(LANES)), axis=1, keepdims=True).astype(I32)
    gden = jnp.sum(jnp.where(is_g, jnp.exp(logits - gmax), 0.0), axis=1, keepdims=True)
    pg = 1.0 / gden
    e_lane = lane - N_EXPERT_GROUPS
    in_grp = (e_lane >= 0) & (e_lane < N_EXPERTS) & ((e_lane // EXPERTS_PER_GROUP) == gsel)
    m1 = jnp.max(jnp.where(in_grp, logits, -jnp.inf), axis=1, keepdims=True)
    i1 = jnp.min(jnp.where(in_grp & (logits == m1), lane_f, float(LANES)), axis=1, keepdims=True).astype(I32)
    rest = in_grp & (lane != i1)
    m2 = jnp.max(jnp.where(rest, logits, -jnp.inf), axis=1, keepdims=True)
    i2 = jnp.min(jnp.where(rest & (logits == m2), lane_f, float(LANES)), axis=1, keepdims=True).astype(I32)
    e2 = jnp.exp(m2 - m1)
    w1 = pg / (1.0 + e2)
    w2 = pg * e2 / (1.0 + e2)
    eid = jnp.where(lane == 0, i1 - N_EXPERT_GROUPS, jnp.where(lane == 1, i2 - N_EXPERT_GROUPS, 0))
    wts = jnp.where(lane == 0, w1, jnp.where(lane == 1, w2, 0.0))
    return eid, wts


def _resid_body(x_ref, z_ref, wo_ref, gf_ref, wr_ref, br_ref, xmid_ref, hpk_ref, eid_ref, wts_ref):
    xm = x_ref[...] + jnp.dot(z_ref[...], wo_ref[...], preferred_element_type=F32)
    xmid_ref[...] = xm
    hf = _rms(xm, gf_ref[...])
    wr = wr_ref[...]
    wr_hi = wr.astype(BF16)
    wr_lo = (wr - wr_hi.astype(F32)).astype(BF16)
    hf_hi = hf.astype(BF16)
    hf_lo = (hf - hf_hi.astype(F32)).astype(BF16)
    logits = (jnp.dot(hf_hi, wr_hi, preferred_element_type=F32) + jnp.dot(hf_hi, wr_lo, preferred_element_type=F32)
              + jnp.dot(hf_lo, wr_hi, preferred_element_type=F32) + br_ref[...])
    eid, wts = _route(logits)
    eid_ref[...] = eid
    wts_ref[...] = wts
    _store_token_tiles(hpk_ref, hf)


def _resid(x2d, z, wo, g_ffn, w_r, b_r, tm):
    m, d = x2d.shape

    def rows(width):
        return pl.BlockSpec((tm, width), lambda i: (i, 0))

    def whole(a):
        return pl.BlockSpec(a.shape, lambda i: (0, 0))

    gf = g_ffn.reshape(1, d)
    nt = d // 2 // LANES
    return pl.pallas_call(
        _resid_body, name="resid", grid=(m // tm,),
        in_specs=[rows(d), rows(d), whole(wo), whole(gf), whole(w_r), whole(b_r)],
        out_specs=[rows(d), pl.BlockSpec((tm * nt, LANES), lambda i: (i, 0)), rows(LANES), rows(LANES)],
        out_shape=[jax.ShapeDtypeStruct((m, d), F32), jax.ShapeDtypeStruct((m * nt, LANES), U32),
                   jax.ShapeDtypeStruct((m, LANES), I32), jax.ShapeDtypeStruct((m, LANES), F32)],
        compiler_params=_cparams(1),
    )(x2d, z, wo, gf, w_r, b_r)


TOKEN_TILE = 8
GATHER_UNROLL = 8


def _gather_rows(idx_ref, src_hbm, dst, sem, n):
    def body(i, carry):
        src = pl.multiple_of(idx_ref[0, 0, i], TOKEN_TILE)
        dst_row = pl.multiple_of(i * TOKEN_TILE, TOKEN_TILE)
        pltpu.make_async_copy(src_hbm.at[pl.ds(src, TOKEN_TILE)], dst.at[pl.ds(dst_row, TOKEN_TILE)], sem).start()
        return carry
    lax.fori_loop(0, n, body, 0, unroll=GATHER_UNROLL)


def _ffn_body(blk_e_ref, nused_ref, idx_cur_ref, idx_nxt_ref, h_hbm, wg_ref, wu_ref, wd_ref, y_ref, buf, sem):
    j = pl.program_id(0)
    nused = nused_ref[0]
    slot = j % 2
    rows = buf.shape[1] // TOKEN_TILE

    @pl.when(j == 0)
    def _():
        _gather_rows(idx_cur_ref, h_hbm, buf.at[0], sem.at[0], rows)

    @pl.when(j + 1 < nused)
    def _():
        _gather_rows(idx_nxt_ref, h_hbm, buf.at[1 - slot], sem.at[1 - slot], rows)

    @pl.when(j < nused)
    def _():
        pltpu.make_async_copy(h_hbm.at[pl.ds(0, rows * TOKEN_TILE)], buf.at[slot], sem.at[slot]).wait()
        x = _load_token_tiles(buf, slot, 0, rows, TOKEN_TILE).astype(BF16)
        a = jnp.dot(x, wg_ref[...], preferred_element_type=F32)
        b = jnp.dot(x, wu_ref[...], preferred_element_type=F32)
        hm = (a * jax.nn.sigmoid(a) * b).astype(BF16)
        _store_token_tiles(y_ref, jnp.dot(hm, wd_ref[...], preferred_element_type=F32))

    @pl.when(j >= nused)
    def _():
        y_ref[...] = jnp.zeros(y_ref.shape, y_ref.dtype)


def _ffn(hpk, src_tok, blk_e, nused, wg, wu, wd):
    n_blocks = blk_e.shape[0]
    rows = MOE_ROWS
    d, de = wg.shape[1], wg.shape[2]
    assert d // 2 == TOKEN_TILE * LANES
    idx3 = (src_tok * TOKEN_TILE).reshape(n_blocks, 1, rows)
    grid_spec = pltpu.PrefetchScalarGridSpec(
        num_scalar_prefetch=2, grid=(n_blocks,),
        in_specs=[
            pl.BlockSpec((1, 1, rows), lambda j, be, nu: (j, 0, 0), memory_space=pltpu.SMEM),
            pl.BlockSpec((1, 1, rows), lambda j, be, nu: (jnp.minimum(j + 1, n_blocks - 1), 0, 0),
                         memory_space=pltpu.SMEM),
            pl.BlockSpec(memory_space=pl.ANY),
            pl.BlockSpec((None, d, de), lambda j, be, nu: (be[j], 0, 0)),
            pl.BlockSpec((None, d, de), lambda j, be, nu: (be[j], 0, 0)),
            pl.BlockSpec((None, de, d), lambda j, be, nu: (be[j], 0, 0)),
        ],
        out_specs=pl.BlockSpec((rows * TOKEN_TILE, LANES), lambda j, be, nu: (j, 0)),
        scratch_shapes=[pltpu.VMEM((2, rows * TOKEN_TILE, LANES), U32), pltpu.SemaphoreType.DMA((2,))],
    )
    return pl.pallas_call(
        _ffn_body, name="ffn", grid_spec=grid_spec,
        out_shape=jax.ShapeDtypeStruct((n_blocks * rows * TOKEN_TILE, LANES), U32), compiler_params=_cparams(1),
    )(blk_e, nused, idx3, idx3, hpk, wg, wu, wd)


def _combine_body(idx_cur_ref, idx_nxt_ref, x_ref, w_ref, y_hbm, o_ref, buf, sem, *, n):
    i = pl.program_id(0)
    slot = i % 2
    rows = buf.shape[1] // TOKEN_TILE
    tm = rows // TOP_K

    @pl.when(i == 0)
    def _():
        _gather_rows(idx_cur_ref, y_hbm, buf.at[0], sem.at[0], rows)

    if n > 1:
        @pl.when(i + 1 < n)
        def _():
            _gather_rows(idx_nxt_ref, y_hbm, buf.at[1 - slot], sem.at[1 - slot], rows)

    pltpu.make_async_copy(y_hbm.at[pl.ds(0, rows * TOKEN_TILE)], buf.at[slot], sem.at[slot]).wait()
    w = w_ref[...]
    y0 = _load_token_tiles(buf, slot, 0, tm, TOKEN_TILE)
    y1 = _load_token_tiles(buf, slot, tm, tm, TOKEN_TILE)
    o_ref[...] = x_ref[...] + (w[:, 0:1] * y0 + w[:, 1:2] * y1)


def _combine(xmid, wts, dest, ypk, tm):
    m, d = xmid.shape
    nt = m // tm
    assert d // 2 == TOKEN_TILE * LANES
    idx3 = (dest * TOKEN_TILE).reshape(nt, tm, TOP_K).transpose(0, 2, 1).reshape(nt, 1, TOP_K * tm)
    return pl.pallas_call(
        functools.partial(_combine_body, n=nt), name="combine", grid=(nt,),
        in_specs=[
            pl.BlockSpec((1, 1, TOP_K * tm), lambda i: (i, 0, 0), memory_space=pltpu.SMEM),
            pl.BlockSpec((1, 1, TOP_K * tm), lambda i: (jnp.minimum(i + 1, nt - 1), 0, 0), memory_space=pltpu.SMEM),
            pl.BlockSpec((tm, d), lambda i: (i, 0)),
            pl.BlockSpec((tm, LANES), lambda i: (i, 0)),
            pl.BlockSpec(memory_space=pl.ANY),
        ],
        out_specs=pl.BlockSpec((tm, d), lambda i: (i, 0)),
        out_shape=jax.ShapeDtypeStruct((m, d), F32),
        scratch_shapes=[pltpu.VMEM((2, TOP_K * tm * TOKEN_TILE, LANES), U32), pltpu.SemaphoreType.DMA((2,))],
        compiler_params=_cparams(1),
    )(idx3, idx3, xmid, wts, ypk)


def _moe_plan(eid2):
    n_slot = eid2.shape[0] * TOP_K
    eid = eid2.reshape(n_slot)
    onehot = (eid[:, None] == jnp.arange(N_EXPERTS, dtype=I32)[None, :]).astype(I32)
    csum = jnp.cumsum(onehot, axis=0)
    rank = jnp.take_along_axis(csum, eid[:, None], axis=1)[:, 0] - 1
    counts = csum[-1]
    padded = (counts + MOE_ROWS - 1) // MOE_ROWS * MOE_ROWS
    pend = jnp.cumsum(padded)
    pstart = pend - padded
    dest = pstart[eid] + rank
    n_blocks = -(-n_slot // MOE_ROWS) + N_EXPERTS
    nused = (pend[-1] // MOE_ROWS).astype(I32)
    blk = jnp.minimum(jnp.arange(n_blocks, dtype=I32), nused - 1)
    blk_e = jnp.minimum(jnp.searchsorted(pend, blk * MOE_ROWS, side="right"), N_EXPERTS - 1).astype(I32)
    src_tok = jnp.zeros((n_blocks * MOE_ROWS,), I32).at[dest].set(jnp.arange(n_slot, dtype=I32) // TOP_K)
    return dest.astype(I32), src_tok, blk_e, nused.reshape(1)


def _mixer(x2d, tm, p, attn_fn, gmlp_fn, mem_fn):
    d = x2d.shape[1]
    h = _prep(x2d, p["g_mix"], tm)
    qdt = p["q_dtype"]
    q = _proj(h, p["w_q"], 512, tm, "norm", qdt, p["g_qa"], HEAD_DIM_A, HEAD_DIM_A ** -0.5, name="proj_q")
    kvs = [_proj(h, p["w_kv"][g], GROUP_W_A, tm, "norm_first", F32, p["g_ka"], HEAD_DIM_A, name=f"proj_kv{g}")
           for g in range(len(DIL_GROUPS))]
    ub = _proj(h, p["w_ub"], 512, tm, "none", BF16, name="proj_ub")
    vb = _proj(h, p["w_vb"], p["w_vb"].shape[1], tm, "norm", p["vb_dtype"], p["g_vb"], p["w_vb"].shape[1],
               name="proj_vb")
    qm = _proj(h, p["w_qm"], 512, tm, "norm", qdt, p["g_qm"], HEAD_DIM_M, HEAD_DIM_M ** -0.5, name="proj_qm")
    gates = _proj(h, p["w_gt"], 1024, tm, "sigmoid", BF16, name="proj_gates")
    o_list, lse_list = attn_fn(q, kvs)
    ob = gmlp_fn(ub, vb)
    om = mem_fn(qm)
    z = _mix(gates, o_list, lse_list, ob, om, p["w_pa"], p["w_pb"], p["w_pm"], min(tm, 256))
    xmid, hpk, eid, wts = _resid(x2d, z, p["w_o"], p["g_ffn"], p["w_r"], p["b_r"], min(tm, 256))
    return kvs, vb, xmid, hpk, eid, wts


def kernel(x_prompt, x_sample, mem_prompt, cache_a0_kv, cache_a1_kv, cache_a2_kv, cache_mem_kv, rel_bias, g_mix, w_in, g_qa, g_ka, w_pa, g_vb, w_s, b_s, w_pb, g_mem, w_mk, w_mv, g_qm, g_km, w_pm, w_o, g_ffn, w_rg, b_rg, w_re, b_re, w_gate, w_up, w_down):
    n_b, seq, d = x_prompt.shape
    n_s, t_s, _ = x_sample.shape
    depth = w_in.shape[0]
    assert depth == 1
    l = 0
    caches = (cache_a0_kv, cache_a1_kv, cache_a2_kv)
    n_g = len(DIL_GROUPS)
    wb = w_pb.shape[1]
    wm = N_HEADS_M * HEAD_DIM_M

    wi = w_in[l]
    off_u = 3 * WIDTH_A
    off_v = off_u + wb
    off_qm = off_v + wb
    off_gt = off_qm + wm
    p = {
        "g_mix": g_mix[l], "g_qa": g_qa[l], "g_ka": g_ka[l], "g_vb": g_vb[l], "g_qm": g_qm[l], "g_ffn": g_ffn[l],
        "w_q": wi[:, :WIDTH_A].astype(BF16),
        "w_kv": [jnp.concatenate([wi[:, WIDTH_A + g * GROUP_W_A:WIDTH_A + (g + 1) * GROUP_W_A],
                                  wi[:, 2 * WIDTH_A + g * GROUP_W_A:2 * WIDTH_A + (g + 1) * GROUP_W_A]],
                                 axis=1).astype(BF16) for g in range(n_g)],
        "w_ub": wi[:, off_u:off_v].astype(BF16),
        "w_vb": wi[:, off_v:off_qm].astype(BF16),
        "w_qm": wi[:, off_qm:off_gt].astype(BF16),
        "w_gt": wi[:, off_gt:].astype(BF16),
        "w_pa": w_pa[l].astype(BF16), "w_pb": w_pb[l].astype(BF16), "w_pm": w_pm[l].astype(BF16),
        "w_o": w_o[l].astype(BF16),
    }
    n_r = N_EXPERT_GROUPS + N_EXPERTS
    p["w_r"] = jnp.concatenate([w_rg[l], w_re[l], jnp.zeros((d, LANES - n_r), F32)], axis=1)
    p["b_r"] = jnp.concatenate([b_rg[l], b_re[l], jnp.zeros((LANES - n_r,), F32)]).reshape(1, LANES)

    n_mem = mem_prompt.shape[1]
    h_mem = _prep(mem_prompt.reshape(n_b * n_mem, d), g_mem[l], 256)
    w_mkv = jnp.concatenate([w_mk[l], w_mv[l]], axis=1).astype(BF16)
    mkv = _proj(h_mem, w_mkv, wm, 256, "norm_first", F32, g_km[l], HEAD_DIM_M, name="proj_mkv")
    new_mem_p = mkv.reshape(1, n_b, n_mem, 2, N_HEADS_M, HEAD_DIM_M)

    pp = dict(p, q_dtype=BF16, vb_dtype=BF16)

    def attn_prompt(q, kvs):
        res = [_attn_prompt(q, kvs[g], rel_bias, g, n_b, seq) for g in range(n_g)]
        return [r[0] for r in res], [r[1] for r in res]

    bs_t = b_s[l].T
    kvs_p, _, xmid_p, hpk_p, eid_p, wts_p = _mixer(
        x_prompt.reshape(n_b * seq, d), 512, pp, attn_prompt,
        lambda u, v: _gmlp(u, v, w_s[l], bs_t, 512, CHUNK),
        lambda qm: _memattn(qm, mkv.reshape(n_b, n_mem, 2 * wm), 256, BF16))
    new_p = []
    for g, (win, _) in enumerate(DIL_GROUPS):
        keep = min(win, seq)
        kv5 = kvs_p[g].reshape(n_b, seq, 2, HEADS_PER_GROUP_A, HEAD_DIM_A)
        new_p.append((kv5 if keep == seq else kv5[:, seq - keep:])[None])

    ps = dict(p, q_dtype=F32, vb_dtype=F32)
    m_s = n_s * t_s

    new_s = []

    def attn_sample(q, kvs):
        res = [_attn_sample(q, kvs[g], caches[g][l:l + 1], rel_bias, g, t_s) for g in range(n_g)]
        new_s.extend(r[2] for r in res)
        return [r[0] for r in res], [r[1] for r in res]

    w_s_small = jnp.tile(w_s[l][:, :t_s, :t_s], (1, n_s, n_s))
    b_s_small = jnp.tile(bs_t[:t_s], (n_s, 1))
    kvs_s, vb_s, xmid_s, hpk_s, eid_s, wts_s = _mixer(
        x_sample.reshape(m_s, d), m_s, ps, attn_sample,
        lambda u, v: _gmlp(u, v, w_s_small, b_s_small, m_s, t_s),
        lambda qm: _memattn(qm, cache_mem_kv[l].reshape(n_s, n_mem, 2 * wm), t_s, F32))
    new_vb_s = vb_s.reshape(1, n_s, t_s, wb)

    n_p = n_b * seq
    hpk = jnp.concatenate([hpk_p, hpk_s], axis=0)
    eid2 = jnp.concatenate([eid_p[:, :TOP_K], eid_s[:, :TOP_K]], axis=0)
    dest, src_tok, blk_e, nused = _moe_plan(eid2)
    ypk = _ffn(hpk, src_tok, blk_e, nused, w_gate[l].astype(BF16), w_up[l].astype(BF16), w_down[l].astype(BF16))
    dest2 = dest.reshape(-1, TOP_K)
    y_p = _combine(xmid_p, wts_p, dest2[:n_p], ypk, 256)
    y_s = _combine(xmid_s, wts_s, dest2[n_p:], ypk, m_s)

    return (y_p.reshape(n_b, seq, d), y_s.reshape(n_s, t_s, d), new_p[0], new_p[1], new_p[2], new_mem_p,
            new_s[0], new_s[1], new_s[2], new_vb_s)
```

```python
import functools
import math

import jax
import jax.numpy as jnp
from jax import lax
from jax.experimental import pallas as pl
from jax.experimental.pallas import tpu as pltpu

F32 = jnp.float32
BF16 = jnp.bfloat16
I32 = jnp.int32
U32 = jnp.uint32

EPS = 1e-6
NEG_INF = -1e30

HEAD_DIM_A = 128
HEADS_PER_GROUP_A = 4
DIL_GROUPS = ((128, 1), (512, 4), (2048, 16))
GROUP_W_A = HEADS_PER_GROUP_A * HEAD_DIM_A
WIDTH_A = len(DIL_GROUPS) * GROUP_W_A
CHUNK = 128
N_GROUPS_B = 8
N_HEADS_M = 4
HEAD_DIM_M = 256
N_BUCKETS = 32
MAX_EXACT = N_BUCKETS // 2
MAX_DISTANCE = 2048
N_EXPERT_GROUPS = 4
EXPERTS_PER_GROUP = 8
N_EXPERTS = N_EXPERT_GROUPS * EXPERTS_PER_GROUP
TOP_K = 2
LANES = 128
MOE_ROWS = 256
VMEM_LIMIT = 56 * 1024 * 1024


def _cparams(n_grid, vmem=VMEM_LIMIT):
    return pltpu.CompilerParams(dimension_semantics=("arbitrary",) * n_grid, vmem_limit_bytes=vmem)


def _rms(x, g):
    return x * lax.rsqrt(jnp.mean(x * x, axis=-1, keepdims=True) + EPS) * g


def _pack_bf16_pair(x):
    n = x.shape[1] // 2
    lo = lax.bitcast_convert_type(x[:, :n].astype(BF16).astype(F32), U32)
    hi = lax.bitcast_convert_type(x[:, n:].astype(BF16).astype(F32), U32)
    return (hi & jnp.uint32(0xFFFF0000)) | (lo >> 16)


def _unpack_bf16_pair(w):
    lo = lax.bitcast_convert_type(w << 16, F32)
    hi = lax.bitcast_convert_type(w & jnp.uint32(0xFFFF0000), F32)
    return lo, hi


def _store_token_tiles(ref, x):
    w = _pack_bf16_pair(x)
    m, n = w.shape
    nt = n // LANES
    for c in range(nt):
        ref[pl.ds(c, m, stride=nt), :] = w[:, c * LANES:(c + 1) * LANES]


def _load_token_tiles(ref, lead, row0, m, nt):
    los, his = [], []
    for c in range(nt):
        lo, hi = _unpack_bf16_pair(ref[lead, pl.ds(row0 * nt + c, m, stride=nt), :])
        los.append(lo)
        his.append(hi)
    return jnp.concatenate(los + his, axis=1)


def _rel_bucket(dist):
    d = jnp.maximum(dist, 1).astype(F32)
    large = MAX_EXACT + (jnp.log(d / MAX_EXACT) / math.log(MAX_DISTANCE / MAX_EXACT)
                         * (N_BUCKETS - MAX_EXACT)).astype(I32)
    return jnp.where(dist < MAX_EXACT, dist, jnp.minimum(large, N_BUCKETS - 1)).astype(I32)


def _prep_body(x_ref, g_ref, o_ref):
    o_ref[...] = _rms(x_ref[...], g_ref[...]).astype(o_ref.dtype)


def _prep(x2d, g, tm):
    m, d = x2d.shape
    return pl.pallas_call(
        _prep_body, name="prep", grid=(m // tm,),
        in_specs=[pl.BlockSpec((tm, d), lambda i: (i, 0)), pl.BlockSpec((1, d), lambda i: (0, 0))],
        out_specs=pl.BlockSpec((tm, d), lambda i: (i, 0)),
        out_shape=jax.ShapeDtypeStruct((m, d), BF16), compiler_params=_cparams(1),
    )(x2d, g.reshape(1, d))


def _headnorm(acc, g, hd, scale):
    outs = []
    for j in range(acc.shape[1] // hd):
        sl = acc[:, j * hd:(j + 1) * hd]
        outs.append(sl * lax.rsqrt(jnp.mean(sl * sl, axis=-1, keepdims=True) + EPS))
    y = outs[0] if len(outs) == 1 else jnp.concatenate(outs, axis=1)
    y = y * g
    return y * scale if scale != 1.0 else y


def _proj_body(h_ref, *refs, n_w, mode, hd, scale):
    w_refs, (g_ref, o_ref, w_scr) = refs[:n_w], refs[n_w:]
    cw = w_refs[0].shape[1]

    @pl.when(pl.program_id(1) == 0)
    def _():
        for i, w_ref in enumerate(w_refs):
            w_scr[:, i * cw:(i + 1) * cw] = w_ref[...].astype(w_scr.dtype)

    acc = jnp.dot(h_ref[...], w_scr[...], preferred_element_type=F32)
    if mode == "none":
        o_ref[...] = acc.astype(o_ref.dtype)
    elif mode == "sigmoid":
        o_ref[...] = jax.nn.sigmoid(acc).astype(o_ref.dtype)
    elif mode == "norm":
        o_ref[...] = _headnorm(acc, g_ref[...], hd, scale).astype(o_ref.dtype)
    else:
        @pl.when(pl.program_id(0) == 0)
        def _():
            o_ref[...] = _headnorm(acc, g_ref[...], hd, scale).astype(o_ref.dtype)

        @pl.when(pl.program_id(0) != 0)
        def _():
            o_ref[...] = acc.astype(o_ref.dtype)


W_COLS = 512


def _proj(h, w, cols, tm, mode, out_dtype, gain=None, hd=None, scale=1.0, name="proj"):
    m, k = h.shape
    base, stride, n_w, n_tiles = cols
    tn = n_w * W_COLS
    if gain is None:
        g = jnp.ones((1, tn), F32)
    else:
        g = jnp.tile(gain.astype(F32).reshape(1, -1), (1, tn // gain.shape[-1]))
    body = functools.partial(_proj_body, n_w=n_w, mode=mode, hd=hd, scale=scale)
    w_specs = [pl.BlockSpec((k, W_COLS), functools.partial(lambda j, i, off: (0, base + stride * j + off), off=off))
               for off in range(n_w)]
    return pl.pallas_call(
        body, name=name, grid=(n_tiles, m // tm),
        in_specs=[pl.BlockSpec((tm, k), lambda j, i: (i, 0))] + w_specs + [pl.BlockSpec((1, tn), lambda j, i: (0, 0))],
        out_specs=pl.BlockSpec((tm, tn), lambda j, i: (i, j)),
        out_shape=jax.ShapeDtypeStruct((m, n_tiles * tn), out_dtype),
        scratch_shapes=[pltpu.VMEM((k, tn), BF16)], compiler_params=_cparams(2),
    )(h, *([w] * n_w), g)


def _bias_from_table(tab_ref, bidx, col):
    acc = jnp.zeros(bidx.shape, F32)
    for kb in range(N_BUCKETS):
        acc = jnp.where(bidx == kb, tab_ref[kb, col], acc)
    return acc


def _attn_p_body(tab_ref, bidx_ref, q_ref, kp_ref, kc_ref, vp_ref, vc_ref, o_ref, lse_ref, bias_scr, *, g):
    first = (pl.program_id(0) == 0) & (pl.program_id(1) == 0) & (pl.program_id(2) == 0)

    @pl.when(first)
    def _():
        bidx = bidx_ref[...]
        for h in range(HEADS_PER_GROUP_A):
            bias_scr[h] = _bias_from_table(tab_ref, bidx, g * HEADS_PER_GROUP_A + h)

    qb = pl.program_id(2)
    blk = q_ref.shape[0]
    q = q_ref[...]
    k = jnp.concatenate([kp_ref[...], kc_ref[...]], axis=0).astype(BF16)
    v = jnp.concatenate([vp_ref[...], vc_ref[...]], axis=0).astype(BF16)
    row = lax.broadcasted_iota(I32, (blk, 2 * blk), 0)
    col = lax.broadcasted_iota(I32, (blk, 2 * blk), 1)
    dist = blk + row - col
    valid = (dist >= 0) & (dist <= blk) & ((qb > 0) | (col >= blk))
    lane = lax.broadcasted_iota(I32, (blk, LANES), 1)
    lse_blk = jnp.zeros((blk, LANES), F32)
    outs = []
    for h in range(HEADS_PER_GROUP_A):
        hs = slice(h * HEAD_DIM_A, (h + 1) * HEAD_DIM_A)
        s = lax.dot_general(q[:, hs], k[:, hs], (((1,), (1,)), ((), ())), preferred_element_type=F32)
        s = jnp.where(valid, s + bias_scr[h], NEG_INF)
        m = jnp.max(s, axis=-1, keepdims=True)
        p = jnp.exp(s - m)
        l = jnp.sum(p, axis=-1, keepdims=True)
        o = jnp.dot(p.astype(BF16), v[:, hs], preferred_element_type=F32) / l
        outs.append(o)
        lse_blk = jnp.where(lane == h, m + jnp.log(l), lse_blk)
    o_ref[...] = jnp.concatenate(outs, axis=1).astype(o_ref.dtype)
    lse_ref[...] = lse_blk


def _attn_prompt(q, kv, rel_bias, g, n_batch, seq):
    win, dil = DIL_GROUPS[g]
    blk = win // dil
    ln = seq // dil
    nb = ln // blk
    rows = n_batch * seq // dil
    a = jnp.arange(blk)[:, None]
    c = jnp.arange(2 * blk)[None, :]
    bidx = _rel_bucket(jnp.maximum(blk + a - c, 0) * dil)
    qv = q.reshape(rows, dil * WIDTH_A)
    kvv = kv.reshape(rows, dil * 2 * GROUP_W_A)
    n_qc = WIDTH_A // GROUP_W_A

    def cur(b, r, j):
        return b * nb + j

    def prev(b, r, j):
        return b * nb + jnp.maximum(j - 1, 0)

    o, lse = pl.pallas_call(
        functools.partial(_attn_p_body, g=g), name=f"attn_p{g}", grid=(n_batch, dil, nb),
        in_specs=[
            pl.BlockSpec(memory_space=pltpu.SMEM),
            pl.BlockSpec((blk, 2 * blk), lambda b, r, j: (0, 0)),
            pl.BlockSpec((blk, GROUP_W_A), lambda b, r, j: (cur(b, r, j), r * n_qc + g)),
            pl.BlockSpec((blk, GROUP_W_A), lambda b, r, j: (prev(b, r, j), r * 2)),
            pl.BlockSpec((blk, GROUP_W_A), lambda b, r, j: (cur(b, r, j), r * 2)),
            pl.BlockSpec((blk, GROUP_W_A), lambda b, r, j: (prev(b, r, j), r * 2 + 1)),
            pl.BlockSpec((blk, GROUP_W_A), lambda b, r, j: (cur(b, r, j), r * 2 + 1)),
        ],
        out_specs=[pl.BlockSpec((blk, GROUP_W_A), lambda b, r, j: (cur(b, r, j), r)),
                   pl.BlockSpec((blk, LANES), lambda b, r, j: (cur(b, r, j), r))],
        out_shape=[jax.ShapeDtypeStruct((rows, dil * GROUP_W_A), BF16),
                   jax.ShapeDtypeStruct((rows, dil * LANES), F32)],
        scratch_shapes=[pltpu.VMEM((HEADS_PER_GROUP_A, blk, 2 * blk), F32)],
        compiler_params=_cparams(3),
    )(rel_bias, bidx, qv, kvv, kvv, kvv, kvv)
    return o.reshape(n_batch * seq, GROUP_W_A), lse.reshape(n_batch * seq, LANES)


def _attn_s_body(tab_ref, bidx_ref, q_ref, kvn_ref, cache_hbm, o_ref, lse_ref, newc_hbm,
                 raw, kv_scr, bias_scr, sem_in, sem_out, *, g, dil, lb, n_req):
    t_new = q_ref.shape[0]
    nh = HEADS_PER_GROUP_A
    n_rows = nh * t_new
    n_keys = kv_scr.shape[0]
    n_col = 2 * nh
    lb8 = lb * n_col
    n = pl.program_id(0)
    slot = n % 2

    def fetch(req, s):
        return pltpu.make_async_copy(cache_hbm.at[req], raw.at[s, pl.ds(0, lb8)], sem_in.at[s])

    def flush(req, s):
        return pltpu.make_async_copy(raw.at[s, pl.ds(t_new * n_col, lb8)], newc_hbm.at[req], sem_out.at[s])

    @pl.when(n == 0)
    def _():
        kv_scr[lb:, :] = jnp.zeros((n_keys - lb, kv_scr.shape[1]), BF16)
        bidx = bidx_ref[...]
        for h in range(nh):
            bias_scr[h * t_new:(h + 1) * t_new, :] = _bias_from_table(tab_ref, bidx, g * nh + h)
        fetch(0, 0).start()

    @pl.when(n >= 1)
    def _():
        flush(n - 1, 1 - slot).wait()

    @pl.when(n + 1 < n_req)
    def _():
        fetch(n + 1, 1 - slot).start()

    fetch(n, slot).wait()
    new = kvn_ref[...]
    for c in range(n_col):
        raw[slot, pl.ds(lb8 + c, t_new, stride=n_col), :] = new[:, c * LANES:(c + 1) * LANES]
    flush(n, slot).start()

    rows_per = min(lb, 256)
    for c in range(n_col):
        for r0 in range(0, lb, rows_per):
            kv_scr[r0:r0 + rows_per, c * LANES:(c + 1) * LANES] = raw[
                slot, pl.ds(r0 * n_col + c, rows_per, stride=n_col), :].astype(BF16)
    kv_scr[lb:lb + 2 * t_new, :] = jnp.concatenate([new, jnp.zeros_like(new)], axis=0).astype(BF16)

    q = q_ref[...]
    qt = jnp.concatenate([q] * nh, axis=0)
    rr = lax.broadcasted_iota(I32, (n_rows, GROUP_W_A), 0)
    cc = lax.broadcasted_iota(I32, (n_rows, GROUP_W_A), 1)
    qbd = jnp.where(cc // HEAD_DIM_A == rr // t_new, qt, 0.0).astype(BF16)

    s = lax.dot_general(qbd, kv_scr[:, :GROUP_W_A], (((1,), (1,)), ((), ())), preferred_element_type=F32)
    row = lax.broadcasted_iota(I32, (n_rows, n_keys), 0)
    col = lax.broadcasted_iota(I32, (n_rows, n_keys), 1)
    delta = lb + (row & (t_new - 1)) - col
    valid = (delta >= 0) & (delta <= lb) & ((delta & (dil - 1)) == 0)
    s = jnp.where(valid, s + bias_scr[...], NEG_INF)
    m = jnp.max(s, axis=-1, keepdims=True)
    p = jnp.exp(s - m)
    l = jnp.sum(p, axis=-1, keepdims=True)
    o = jnp.dot(p.astype(BF16), kv_scr[:, GROUP_W_A:], preferred_element_type=F32) / l
    lse = m + jnp.log(l)
    lane = lax.broadcasted_iota(I32, (t_new, LANES), 1)
    lse_blk = jnp.zeros((t_new, LANES), F32)
    outs = []
    for h in range(nh):
        outs.append(o[h * t_new:(h + 1) * t_new, h * HEAD_DIM_A:(h + 1) * HEAD_DIM_A])
        lse_blk = jnp.where(lane == h, lse[h * t_new:(h + 1) * t_new, :], lse_blk)
    o_ref[...] = jnp.concatenate(outs, axis=1)
    lse_ref[...] = lse_blk

    @pl.when(n == n_req - 1)
    def _():
        flush(n, slot).wait()


def _attn_sample(q, kv_new, cache, rel_bias, g, t_new):
    win, dil = DIL_GROUPS[g]
    _, n_req, lb = cache.shape[:3]
    n_col = 2 * HEADS_PER_GROUP_A
    assert t_new & (t_new - 1) == 0 and dil & (dil - 1) == 0 and lb == win and cache.shape[3:] == (2, 4, LANES)
    n_keys = lb + LANES
    t = jnp.arange(t_new)[:, None]
    c = jnp.arange(n_keys)[None, :]
    bidx = _rel_bucket(jnp.clip(lb + t - c, 0, lb))
    o, lse, newc = pl.pallas_call(
        functools.partial(_attn_s_body, g=g, dil=dil, lb=lb, n_req=n_req), name=f"attn_s{g}", grid=(n_req,),
        in_specs=[
            pl.BlockSpec(memory_space=pltpu.SMEM),
            pl.BlockSpec((t_new, n_keys), lambda n: (0, 0)),
            pl.BlockSpec((t_new, GROUP_W_A), lambda n: (n, g)),
            pl.BlockSpec((t_new, 2 * GROUP_W_A), lambda n: (n, 0)),
            pl.BlockSpec(memory_space=pl.ANY),
        ],
        out_specs=[pl.BlockSpec((t_new, GROUP_W_A), lambda n: (n, 0)),
                   pl.BlockSpec((t_new, LANES), lambda n: (n, 0)),
                   pl.BlockSpec(memory_space=pl.ANY)],
        out_shape=[jax.ShapeDtypeStruct((n_req * t_new, GROUP_W_A), F32),
                   jax.ShapeDtypeStruct((n_req * t_new, LANES), F32),
                   jax.ShapeDtypeStruct((n_req, lb * n_col, LANES), F32)],
        scratch_shapes=[pltpu.VMEM((2, (lb + t_new) * n_col, LANES), F32),
                        pltpu.VMEM((n_keys, 2 * GROUP_W_A), BF16),
                        pltpu.VMEM((HEADS_PER_GROUP_A * t_new, n_keys), F32),
                        pltpu.SemaphoreType.DMA((2,)), pltpu.SemaphoreType.DMA((2,))],
        compiler_params=_cparams(1),
    )(rel_bias, bidx, q, kv_new, cache.reshape(n_req, lb * n_col, LANES))
    return o, lse, newc.reshape(cache.shape)


def _gmlp_body(u_ref, v_ref, w_ref, b_ref, o_ref, *, period):
    c = w_ref.shape[1]
    nch = u_ref.shape[0] // c
    gd = u_ref.shape[1] // N_GROUPS_B
    i = lax.broadcasted_iota(I32, (c, c), 0)
    j = lax.broadcasted_iota(I32, (c, c), 1)
    mask = (j <= i) & ((i // period) == (j // period))
    b = b_ref[...]
    for g in range(N_GROUPS_B):
        gs = slice(g * gd, (g + 1) * gd)
        wg = jnp.where(mask, w_ref[g], 0.0).astype(BF16)
        vg = [v_ref[ch * c:(ch + 1) * c, gs].astype(BF16) for ch in range(nch)]
        vg = vg[0] if nch == 1 else jnp.concatenate(vg, axis=1)
        sg = jnp.dot(wg, vg, preferred_element_type=F32) + b[:, g:g + 1]
        for ch in range(nch):
            u = u_ref[ch * c:(ch + 1) * c, gs].astype(F32)
            o_ref[ch * c:(ch + 1) * c, gs] = (u * sg[:, ch * gd:(ch + 1) * gd]).astype(o_ref.dtype)


def _gmlp(u, v, w, b, tm, period):
    m, wb = u.shape
    c = w.shape[1]
    return pl.pallas_call(
        functools.partial(_gmlp_body, period=period), name="gmlp", grid=(m // tm,),
        in_specs=[pl.BlockSpec((tm, wb), lambda i: (i, 0)), pl.BlockSpec((tm, wb), lambda i: (i, 0)),
                  pl.BlockSpec((N_GROUPS_B, c, c), lambda i: (0, 0, 0)),
                  pl.BlockSpec((c, N_GROUPS_B), lambda i: (0, 0))],
        out_specs=pl.BlockSpec((tm, wb), lambda i: (i, 0)),
        out_shape=jax.ShapeDtypeStruct((m, wb), BF16), compiler_params=_cparams(1),
    )(u, v, w, b)


def _memattn_body(q_ref, kv_ref, o_ref):
    wm = N_HEADS_M * HEAD_DIM_M
    q = q_ref[...].astype(BF16)
    outs = []
    for h in range(N_HEADS_M):
        hs = slice(h * HEAD_DIM_M, (h + 1) * HEAD_DIM_M)
        k = kv_ref[:, hs].astype(BF16)
        v = kv_ref[:, wm + h * HEAD_DIM_M:wm + (h + 1) * HEAD_DIM_M].astype(BF16)
        s = lax.dot_general(q[:, hs], k, (((1,), (1,)), ((), ())), preferred_element_type=F32)
        m = jnp.max(s, axis=-1, keepdims=True)
        p = jnp.exp(s - m)
        l = jnp.sum(p, axis=-1, keepdims=True)
        outs.append(jnp.dot(p.astype(BF16), v, preferred_element_type=F32) / l)
    o_ref[...] = jnp.concatenate(outs, axis=1).astype(o_ref.dtype)


def _memattn(q, kv, tq, out_dtype):
    m, wm = q.shape
    n, nm, _ = kv.shape
    per = m // n // tq
    return pl.pallas_call(
        _memattn_body, name="memattn", grid=(m // tq,),
        in_specs=[pl.BlockSpec((tq, wm), lambda i: (i, 0)),
                  pl.BlockSpec((None, nm, 2 * wm), lambda i: (i // per, 0, 0))],
        out_specs=pl.BlockSpec((tq, wm), lambda i: (i, 0)),
        out_shape=jax.ShapeDtypeStruct((m, wm), out_dtype), compiler_params=_cparams(1),
    )(q, kv)


def _mix_body(gt_ref, o0_ref, o1_ref, o2_ref, l0_ref, l1_ref, l2_ref, ob_ref, om_ref,
              wpa_ref, wpb_ref, wpm_ref, z_ref):
    d = z_ref.shape[1]
    l0, l1, l2 = l0_ref[...], l1_ref[...], l2_ref[...]
    mx = jnp.maximum(jnp.maximum(l0, l1), l2)
    e0, e1, e2 = jnp.exp(l0 - mx), jnp.exp(l1 - mx), jnp.exp(l2 - mx)
    den = e0 + e1 + e2
    w0, w1, w2 = e0 / den, e1 / den, e2 / den
    cols = []
    for h in range(HEADS_PER_GROUP_A):
        hs = slice(h * HEAD_DIM_A, (h + 1) * HEAD_DIM_A)
        cols.append(w0[:, h:h + 1] * o0_ref[:, hs].astype(F32) + w1[:, h:h + 1] * o1_ref[:, hs].astype(F32)
                    + w2[:, h:h + 1] * o2_ref[:, hs].astype(F32))
    oa = jnp.concatenate(cols, axis=1).astype(BF16)
    pa = jnp.dot(oa, wpa_ref[...], preferred_element_type=F32)
    pb = jnp.dot(ob_ref[...].astype(BF16), wpb_ref[...], preferred_element_type=F32)
    pm = jnp.dot(om_ref[...].astype(BF16), wpm_ref[...], preferred_element_type=F32)
    z = (gt_ref[:, 0:d].astype(F32) * pa + gt_ref[:, d:2 * d].astype(F32) * pb
         + gt_ref[:, 2 * d:3 * d].astype(F32) * pm)
    z_ref[...] = z.astype(z_ref.dtype)


def _mix(gates, o_list, lse_list, ob, om, wpa, wpb, wpm, tm):
    m = gates.shape[0]
    d = wpa.shape[1]

    def rows(width):
        return pl.BlockSpec((tm, width), lambda i: (i, 0))

    def whole(a):
        return pl.BlockSpec(a.shape, lambda i: (0, 0))

    return pl.pallas_call(
        _mix_body, name="mix", grid=(m // tm,),
        in_specs=[rows(gates.shape[1])] + [rows(GROUP_W_A)] * 3 + [rows(LANES)] * 3
                 + [rows(ob.shape[1]), rows(om.shape[1]), whole(wpa), whole(wpb), whole(wpm)],
        out_specs=rows(d),
        out_shape=jax.ShapeDtypeStruct((m, d), BF16), compiler_params=_cparams(1),
    )(gates, *o_list, *lse_list, ob, om, wpa, wpb, wpm)


def _route(logits):
    lane = lax.broadcasted_iota(I32, logits.shape, 1)
    lane_f = lane.astype(F32)
    is_g = lane < N_EXPERT_GROUPS
    gmax = jnp.max(jnp.where(is_g, logits, -jnp.inf), axis=1, keepdims=True)
    gsel = jnp.min(jnp.where(is_g & (logits == gmax), lane_f, float(LANES)), axis=1, keepdims=True).astype(I32)
    gden = jnp.sum(jnp.where(is_g, jnp.exp(logits - gmax), 0.0), axis=1, keepdims=True)
    pg = 1.0 / gden
    e_lane = lane - N_EXPERT_GROUPS
    in_grp = (e_lane >= 0) & (e_lane < N_EXPERTS) & ((e_lane // EXPERTS_PER_GROUP) == gsel)
    m1 = jnp.max(jnp.where(in_grp, logits, -jnp.inf), axis=1, keepdims=True)
    i1 = jnp.min(jnp.where(in_grp & (logits == m1), lane_f, float(LANES)), axis=1, keepdims=True).astype(I32)
    rest = in_grp & (lane != i1)
    m2 = jnp.max(jnp.where(rest, logits, -jnp.inf), axis=1, keepdims=True)
    i2 = jnp.min(jnp.where(rest & (logits == m2), lane_f, float(LANES)), axis=1, keepdims=True).astype(I32)
    e2 = jnp.exp(m2 - m1)
    w1 = pg / (1.0 + e2)
    w2 = pg * e2 / (1.0 + e2)
    eid = jnp.where(lane == 0, i1 - N_EXPERT_GROUPS, jnp.where(lane == 1, i2 - N_EXPERT_GROUPS, 0))
    wts = jnp.where(lane == 0, w1, jnp.where(lane == 1, w2, 0.0))
    return eid, wts


def _resid_body(x_ref, z_ref, wo_ref, gf_ref, wr_ref, br_ref, xmid_ref, hpk_ref, eid_ref, wts_ref):
    xm = x_ref[...] + jnp.dot(z_ref[...], wo_ref[...], preferred_element_type=F32)
    xmid_ref[...] = xm
    hf = _rms(xm, gf_ref[...])
    wr = wr_ref[...]
    wr_hi = wr.astype(BF16)
    wr_lo = (wr - wr_hi.astype(F32)).astype(BF16)
    hf_hi = hf.astype(BF16)
    hf_lo = (hf - hf_hi.astype(F32)).astype(BF16)
    logits = (jnp.dot(hf_hi, wr_hi, preferred_element_type=F32) + jnp.dot(hf_hi, wr_lo, preferred_element_type=F32)
              + jnp.dot(hf_lo, wr_hi, preferred_element_type=F32) + br_ref[...])
    eid, wts = _route(logits)
    eid_ref[...] = eid
    wts_ref[...] = wts
    _store_token_tiles(hpk_ref, hf)


def _resid(x2d, z, wo, g_ffn, w_r, b_r, tm):
    m, d = x2d.shape

    def rows(width):
        return pl.BlockSpec((tm, width), lambda i: (i, 0))

    def whole(a):
        return pl.BlockSpec(a.shape, lambda i: (0, 0))

    gf = g_ffn.reshape(1, d)
    nt = d // 2 // LANES
    return pl.pallas_call(
        _resid_body, name="resid", grid=(m // tm,),
        in_specs=[rows(d), rows(d), whole(wo), whole(gf), whole(w_r), whole(b_r)],
        out_specs=[rows(d), pl.BlockSpec((tm * nt, LANES), lambda i: (i, 0)), rows(LANES), rows(LANES)],
        out_shape=[jax.ShapeDtypeStruct((m, d), F32), jax.ShapeDtypeStruct((m * nt, LANES), U32),
                   jax.ShapeDtypeStruct((m, LANES), I32), jax.ShapeDtypeStruct((m, LANES), F32)],
        compiler_params=_cparams(1),
    )(x2d, z, wo, gf, w_r, b_r)


TOKEN_TILE = 8
GATHER_UNROLL = 8


def _gather_rows(idx_ref, src_hbm, dst, sem, n, priority=0):
    def body(i, carry):
        src = pl.multiple_of(idx_ref[0, 0, i], TOKEN_TILE)
        dst_row = pl.multiple_of(i * TOKEN_TILE, TOKEN_TILE)
        pltpu.make_async_copy(src_hbm.at[pl.ds(src, TOKEN_TILE)], dst.at[pl.ds(dst_row, TOKEN_TILE)],
                              sem).start(priority=priority)
        return carry
    lax.fori_loop(0, n, body, 0, unroll=GATHER_UNROLL)


def _ffn_body(blk_e_ref, nused_ref, idx_cur_ref, idx_nxt_ref, h_hbm, wg_ref, wu_ref, wd_ref, y_ref,
              buf, sem, wg_s, wu_s, wd_s):
    j = pl.program_id(0)
    nused = nused_ref[0]
    slot = j % 2
    rows = buf.shape[1] // TOKEN_TILE
    new_expert = (j == 0) | (blk_e_ref[j] != blk_e_ref[jnp.maximum(j - 1, 0)])

    @pl.when(j == 0)
    def _():
        _gather_rows(idx_cur_ref, h_hbm, buf.at[0], sem.at[0], rows, priority=1)

    @pl.when(j + 1 < nused)
    def _():
        _gather_rows(idx_nxt_ref, h_hbm, buf.at[1 - slot], sem.at[1 - slot], rows, priority=1)

    @pl.when(new_expert)
    def _():
        wg_s[...] = wg_ref[...].astype(BF16)
        wu_s[...] = wu_ref[...].astype(BF16)
        wd_s[...] = wd_ref[...].astype(BF16)

    @pl.when(j < nused)
    def _():
        pltpu.make_async_copy(h_hbm.at[pl.ds(0, rows * TOKEN_TILE)], buf.at[slot], sem.at[slot]).wait()
        x = _load_token_tiles(buf, slot, 0, rows, TOKEN_TILE).astype(BF16)
        a = jnp.dot(x, wg_s[...], preferred_element_type=F32)
        b = jnp.dot(x, wu_s[...], preferred_element_type=F32)
        hm = (a * jax.nn.sigmoid(a) * b).astype(BF16)
        _store_token_tiles(y_ref, jnp.dot(hm, wd_s[...], preferred_element_type=F32))

    @pl.when(j >= nused)
    def _():
        y_ref[...] = jnp.zeros(y_ref.shape, y_ref.dtype)


def _ffn(hpk, src_tok, blk_e, nused, wg, wu, wd):
    n_blocks = blk_e.shape[0]
    rows = MOE_ROWS
    d, de = wg.shape[1], wg.shape[2]
    assert d // 2 == TOKEN_TILE * LANES
    idx3 = (src_tok * TOKEN_TILE).reshape(n_blocks, 1, rows)
    grid_spec = pltpu.PrefetchScalarGridSpec(
        num_scalar_prefetch=2, grid=(n_blocks,),
        in_specs=[
            pl.BlockSpec((1, 1, rows), lambda j, be, nu: (j, 0, 0), memory_space=pltpu.SMEM),
            pl.BlockSpec((1, 1, rows), lambda j, be, nu: (jnp.minimum(j + 1, n_blocks - 1), 0, 0),
                         memory_space=pltpu.SMEM),
            pl.BlockSpec(memory_space=pl.ANY),
            pl.BlockSpec((None, d, de), lambda j, be, nu: (be[j], 0, 0)),
            pl.BlockSpec((None, d, de), lambda j, be, nu: (be[j], 0, 0)),
            pl.BlockSpec((None, de, d), lambda j, be, nu: (be[j], 0, 0)),
        ],
        out_specs=pl.BlockSpec((rows * TOKEN_TILE, LANES), lambda j, be, nu: (j, 0)),
        scratch_shapes=[pltpu.VMEM((2, rows * TOKEN_TILE, LANES), U32), pltpu.SemaphoreType.DMA((2,)),
                        pltpu.VMEM((d, de), BF16), pltpu.VMEM((d, de), BF16), pltpu.VMEM((de, d), BF16)],
    )
    return pl.pallas_call(
        _ffn_body, name="ffn", grid_spec=grid_spec,
        out_shape=jax.ShapeDtypeStruct((n_blocks * rows * TOKEN_TILE, LANES), U32), compiler_params=_cparams(1),
    )(blk_e, nused, idx3, idx3, hpk, wg, wu, wd)


def _combine_body(idx_cur_ref, idx_nxt_ref, x_ref, w_ref, y_hbm, o_ref, buf, sem, *, n):
    i = pl.program_id(0)
    slot = i % 2
    rows = buf.shape[1] // TOKEN_TILE
    tm = rows // TOP_K

    @pl.when(i == 0)
    def _():
        _gather_rows(idx_cur_ref, y_hbm, buf.at[0], sem.at[0], rows)

    if n > 1:
        @pl.when(i + 1 < n)
        def _():
            _gather_rows(idx_nxt_ref, y_hbm, buf.at[1 - slot], sem.at[1 - slot], rows)

    pltpu.make_async_copy(y_hbm.at[pl.ds(0, rows * TOKEN_TILE)], buf.at[slot], sem.at[slot]).wait()
    w = w_ref[...]
    y0 = _load_token_tiles(buf, slot, 0, tm, TOKEN_TILE)
    y1 = _load_token_tiles(buf, slot, tm, tm, TOKEN_TILE)
    o_ref[...] = x_ref[...] + (w[:, 0:1] * y0 + w[:, 1:2] * y1)


def _combine(xmid, wts, dest, ypk, tm):
    m, d = xmid.shape
    nt = m // tm
    assert d // 2 == TOKEN_TILE * LANES
    idx3 = (dest * TOKEN_TILE).reshape(nt, tm, TOP_K).transpose(0, 2, 1).reshape(nt, 1, TOP_K * tm)
    return pl.pallas_call(
        functools.partial(_combine_body, n=nt), name="combine", grid=(nt,),
        in_specs=[
            pl.BlockSpec((1, 1, TOP_K * tm), lambda i: (i, 0, 0), memory_space=pltpu.SMEM),
            pl.BlockSpec((1, 1, TOP_K * tm), lambda i: (jnp.minimum(i + 1, nt - 1), 0, 0), memory_space=pltpu.SMEM),
            pl.BlockSpec((tm, d), lambda i: (i, 0)),
            pl.BlockSpec((tm, LANES), lambda i: (i, 0)),
            pl.BlockSpec(memory_space=pl.ANY),
        ],
        out_specs=pl.BlockSpec((tm, d), lambda i: (i, 0)),
        out_shape=jax.ShapeDtypeStruct((m, d), F32),
        scratch_shapes=[pltpu.VMEM((2, TOP_K * tm * TOKEN_TILE, LANES), U32), pltpu.SemaphoreType.DMA((2,))],
        compiler_params=_cparams(1),
    )(idx3, idx3, xmid, wts, ypk)


def _moe_plan(eid2):
    n_slot = eid2.shape[0] * TOP_K
    eid = eid2.reshape(n_slot)
    onehot = (eid[:, None] == jnp.arange(N_EXPERTS, dtype=I32)[None, :]).astype(I32)
    csum = jnp.cumsum(onehot, axis=0)
    rank = jnp.take_along_axis(csum, eid[:, None], axis=1)[:, 0] - 1
    counts = csum[-1]
    padded = (counts + MOE_ROWS - 1) // MOE_ROWS * MOE_ROWS
    pend = jnp.cumsum(padded)
    pstart = pend - padded
    dest = pstart[eid] + rank
    n_blocks = -(-n_slot // MOE_ROWS) + N_EXPERTS
    nused = (pend[-1] // MOE_ROWS).astype(I32)
    blk = jnp.minimum(jnp.arange(n_blocks, dtype=I32), nused - 1)
    blk_e = jnp.minimum(jnp.searchsorted(pend, blk * MOE_ROWS, side="right"), N_EXPERTS - 1).astype(I32)
    src_tok = jnp.zeros((n_blocks * MOE_ROWS,), I32).at[dest].set(jnp.arange(n_slot, dtype=I32) // TOP_K)
    return dest.astype(I32), src_tok, blk_e, nused.reshape(1)


def _mixer(x2d, tm, p, attn_fn, gmlp_fn, mem_fn):
    d = x2d.shape[1]
    h = _prep(x2d, p["g_mix"], tm)
    qdt = p["q_dtype"]
    wi, sec = p["w_in"], p["sec"]
    n_g = len(DIL_GROUPS)
    q = _proj(h, wi, (sec["q"], 1, 1, n_g), tm, "norm", qdt, p["g_qa"], HEAD_DIM_A, HEAD_DIM_A ** -0.5,
              name="proj_q")
    kvs = [_proj(h, wi, (sec["k"] + g, sec["v"] - sec["k"], 1, 2), tm, "norm_first", F32, p["g_ka"], HEAD_DIM_A,
                 name=f"proj_kv{g}") for g in range(n_g)]
    ub = _proj(h, wi, (sec["u"], 1, 1, sec["vb"] - sec["u"]), tm, "none", BF16, name="proj_ub")
    n_vb = sec["qm"] - sec["vb"]
    vb = _proj(h, wi, (sec["vb"], n_vb, n_vb, 1), tm, "norm", p["vb_dtype"], p["g_vb"], n_vb * W_COLS,
               name="proj_vb")
    qm = _proj(h, wi, (sec["qm"], 1, 1, sec["gt"] - sec["qm"]), tm, "norm", qdt, p["g_qm"], HEAD_DIM_M,
               HEAD_DIM_M ** -0.5, name="proj_qm")
    gates = _proj(h, wi, (sec["gt"], 2, 2, (sec["end"] - sec["gt"]) // 2), tm, "sigmoid", BF16, name="proj_gates")
    o_list, lse_list = attn_fn(q, kvs)
    ob = gmlp_fn(ub, vb)
    om = mem_fn(qm)
    z = _mix(gates, o_list, lse_list, ob, om, p["w_pa"], p["w_pb"], p["w_pm"], min(tm, 256))
    xmid, hpk, eid, wts = _resid(x2d, z, p["w_o"], p["g_ffn"], p["w_r"], p["b_r"], min(tm, 256))
    return kvs, vb, xmid, hpk, eid, wts


def kernel(x_prompt, x_sample, mem_prompt, cache_a0_kv, cache_a1_kv, cache_a2_kv, cache_mem_kv, rel_bias, g_mix, w_in, g_qa, g_ka, w_pa, g_vb, w_s, b_s, w_pb, g_mem, w_mk, w_mv, g_qm, g_km, w_pm, w_o, g_ffn, w_rg, b_rg, w_re, b_re, w_gate, w_up, w_down):
    n_b, seq, d = x_prompt.shape
    n_s, t_s, _ = x_sample.shape
    depth = w_in.shape[0]
    assert depth == 1
    l = 0
    caches = (cache_a0_kv, cache_a1_kv, cache_a2_kv)
    n_g = len(DIL_GROUPS)
    wb = w_pb.shape[1]
    wm = N_HEADS_M * HEAD_DIM_M

    offs = [0, WIDTH_A, 2 * WIDTH_A, 3 * WIDTH_A, 3 * WIDTH_A + wb, 3 * WIDTH_A + 2 * wb,
            3 * WIDTH_A + 2 * wb + wm, w_in.shape[2]]
    assert all(o % W_COLS == 0 for o in offs) and (offs[7] - offs[6]) % (2 * W_COLS) == 0
    sec = dict(zip(("q", "k", "v", "u", "vb", "qm", "gt", "end"), (o // W_COLS for o in offs)))
    p = {
        "g_mix": g_mix[l], "g_qa": g_qa[l], "g_ka": g_ka[l], "g_vb": g_vb[l], "g_qm": g_qm[l], "g_ffn": g_ffn[l],
        "w_in": w_in[l], "sec": sec,
        "w_pa": w_pa[l].astype(BF16), "w_pb": w_pb[l].astype(BF16), "w_pm": w_pm[l].astype(BF16),
        "w_o": w_o[l].astype(BF16),
    }
    n_r = N_EXPERT_GROUPS + N_EXPERTS
    p["w_r"] = jnp.concatenate([w_rg[l], w_re[l], jnp.zeros((d, LANES - n_r), F32)], axis=1)
    p["b_r"] = jnp.concatenate([b_rg[l], b_re[l], jnp.zeros((LANES - n_r,), F32)]).reshape(1, LANES)

    n_mem = mem_prompt.shape[1]
    h_mem = _prep(mem_prompt.reshape(n_b * n_mem, d), g_mem[l], 256)
    w_mkv = jnp.concatenate([w_mk[l], w_mv[l]], axis=1)
    n_mw = wm // W_COLS
    mkv = _proj(h_mem, w_mkv, (0, n_mw, n_mw, 2), 256, "norm_first", F32, g_km[l], HEAD_DIM_M, name="proj_mkv")
    new_mem_p = mkv.reshape(1, n_b, n_mem, 2, N_HEADS_M, HEAD_DIM_M)

    pp = dict(p, q_dtype=BF16, vb_dtype=BF16)

    def attn_prompt(q, kvs):
        res = [_attn_prompt(q, kvs[g], rel_bias, g, n_b, seq) for g in range(n_g)]
        return [r[0] for r in res], [r[1] for r in res]

    bs_t = b_s[l].T
    kvs_p, _, xmid_p, hpk_p, eid_p, wts_p = _mixer(
        x_prompt.reshape(n_b * seq, d), 512, pp, attn_prompt,
        lambda u, v: _gmlp(u, v, w_s[l], bs_t, 512, CHUNK),
        lambda qm: _memattn(qm, mkv.reshape(n_b, n_mem, 2 * wm), 256, BF16))
    new_p = []
    for g, (win, _) in enumerate(DIL_GROUPS):
        keep = min(win, seq)
        kv5 = kvs_p[g].reshape(n_b, seq, 2, HEADS_PER_GROUP_A, HEAD_DIM_A)
        new_p.append((kv5 if keep == seq else kv5[:, seq - keep:])[None])

    ps = dict(p, q_dtype=F32, vb_dtype=F32)
    m_s = n_s * t_s

    new_s = []

    def attn_sample(q, kvs):
        res = [_attn_sample(q, kvs[g], caches[g][l:l + 1], rel_bias, g, t_s) for g in range(n_g)]
        new_s.extend(r[2] for r in res)
        return [r[0] for r in res], [r[1] for r in res]

    w_s_small = jnp.tile(w_s[l][:, :t_s, :t_s], (1, n_s, n_s))
    b_s_small = jnp.tile(bs_t[:t_s], (n_s, 1))
    kvs_s, vb_s, xmid_s, hpk_s, eid_s, wts_s = _mixer(
        x_sample.reshape(m_s, d), m_s, ps, attn_sample,
        lambda u, v: _gmlp(u, v, w_s_small, b_s_small, m_s, t_s),
        lambda qm: _memattn(qm, cache_mem_kv[l].reshape(n_s, n_mem, 2 * wm), t_s, F32))
    new_vb_s = vb_s.reshape(1, n_s, t_s, wb)

    n_p = n_b * seq
    hpk = jnp.concatenate([hpk_p, hpk_s], axis=0)
    eid2 = jnp.concatenate([eid_p[:, :TOP_K], eid_s[:, :TOP_K]], axis=0)
    dest, src_tok, blk_e, nused = _moe_plan(eid2)
    ypk = _ffn(hpk, src_tok, blk_e, nused, w_gate[l], w_up[l], w_down[l])
    dest2 = dest.reshape(-1, TOP_K)
    y_p = _combine(xmid_p, wts_p, dest2[:n_p], ypk, 256)
    y_s = _combine(xmid_s, wts_s, dest2[n_p:], ypk, m_s)

    return (y_p.reshape(n_b, seq, d), y_s.reshape(n_s, t_s, d), new_p[0], new_p[1], new_p[2], new_mem_p,
            new_s[0], new_s[1], new_s[2], new_vb_s)
```

```python
import functools
import math

import jax
import jax.numpy as jnp
from jax import lax
from jax.experimental import pallas as pl
from jax.experimental.pallas import tpu as pltpu

F32 = jnp.float32
BF16 = jnp.bfloat16
I32 = jnp.int32
U32 = jnp.uint32

EPS = 1e-6
NEG_INF = -1e30

HEAD_DIM_A = 128
HEADS_PER_GROUP_A = 4
DIL_GROUPS = ((128, 1), (512, 4), (2048, 16))
GROUP_W_A = HEADS_PER_GROUP_A * HEAD_DIM_A
WIDTH_A = len(DIL_GROUPS) * GROUP_W_A
CHUNK = 128
N_GROUPS_B = 8
N_HEADS_M = 4
HEAD_DIM_M = 256
N_BUCKETS = 32
MAX_EXACT = N_BUCKETS // 2
MAX_DISTANCE = 2048
N_EXPERT_GROUPS = 4
EXPERTS_PER_GROUP = 8
N_EXPERTS = N_EXPERT_GROUPS * EXPERTS_PER_GROUP
TOP_K = 2
LANES = 128
MOE_ROWS = 256
VMEM_LIMIT = 56 * 1024 * 1024


def _cparams(n_grid, vmem=VMEM_LIMIT):
    return pltpu.CompilerParams(dimension_semantics=("arbitrary",) * n_grid, vmem_limit_bytes=vmem)


def _rms(x, g):
    return x * lax.rsqrt(jnp.mean(x * x, axis=-1, keepdims=True) + EPS) * g


def _pack_bf16_pair(x):
    n = x.shape[1] // 2
    lo = lax.bitcast_convert_type(x[:, :n].astype(BF16).astype(F32), U32)
    hi = lax.bitcast_convert_type(x[:, n:].astype(BF16).astype(F32), U32)
    return (hi & jnp.uint32(0xFFFF0000)) | (lo >> 16)


def _unpack_bf16_pair(w):
    lo = lax.bitcast_convert_type(w << 16, F32)
    hi = lax.bitcast_convert_type(w & jnp.uint32(0xFFFF0000), F32)
    return lo, hi


def _store_token_tiles(ref, x):
    w = _pack_bf16_pair(x)
    m, n = w.shape
    nt = n // LANES
    for c in range(nt):
        ref[pl.ds(c, m, stride=nt), :] = w[:, c * LANES:(c + 1) * LANES]


def _load_token_tiles(ref, lead, row0, m, nt):
    los, his = [], []
    for c in range(nt):
        lo, hi = _unpack_bf16_pair(ref[lead, pl.ds(row0 * nt + c, m, stride=nt), :])
        los.append(lo)
        his.append(hi)
    return jnp.concatenate(los + his, axis=1)


def _rel_bucket(dist):
    d = jnp.maximum(dist, 1).astype(F32)
    large = MAX_EXACT + (jnp.log(d / MAX_EXACT) / math.log(MAX_DISTANCE / MAX_EXACT)
                         * (N_BUCKETS - MAX_EXACT)).astype(I32)
    return jnp.where(dist < MAX_EXACT, dist, jnp.minimum(large, N_BUCKETS - 1)).astype(I32)


def _prep_body(x_ref, g_ref, o_ref):
    o_ref[...] = _rms(x_ref[...], g_ref[...]).astype(o_ref.dtype)


def _prep(x2d, g, tm):
    m, d = x2d.shape
    return pl.pallas_call(
        _prep_body, name="prep", grid=(m // tm,),
        in_specs=[pl.BlockSpec((tm, d), lambda i: (i, 0)), pl.BlockSpec((1, d), lambda i: (0, 0))],
        out_specs=pl.BlockSpec((tm, d), lambda i: (i, 0)),
        out_shape=jax.ShapeDtypeStruct((m, d), BF16), compiler_params=_cparams(1),
    )(x2d, g.reshape(1, d))


def _headnorm(acc, g, hd, scale):
    outs = []
    for j in range(acc.shape[1] // hd):
        sl = acc[:, j * hd:(j + 1) * hd]
        outs.append(sl * lax.rsqrt(jnp.mean(sl * sl, axis=-1, keepdims=True) + EPS))
    y = outs[0] if len(outs) == 1 else jnp.concatenate(outs, axis=1)
    y = y * g
    return y * scale if scale != 1.0 else y


def _proj_body(h_ref, *refs, n_w, mode, hd, scale):
    w_refs, (g_ref, o_ref, w_scr) = refs[:n_w], refs[n_w:]
    cw = w_refs[0].shape[1]

    @pl.when(pl.program_id(1) == 0)
    def _():
        for i, w_ref in enumerate(w_refs):
            w_scr[:, i * cw:(i + 1) * cw] = w_ref[...].astype(w_scr.dtype)

    acc = jnp.dot(h_ref[...], w_scr[...], preferred_element_type=F32)
    if mode == "none":
        o_ref[...] = acc.astype(o_ref.dtype)
    elif mode == "sigmoid":
        o_ref[...] = jax.nn.sigmoid(acc).astype(o_ref.dtype)
    elif mode == "norm":
        o_ref[...] = _headnorm(acc, g_ref[...], hd, scale).astype(o_ref.dtype)
    else:
        @pl.when(pl.program_id(0) == 0)
        def _():
            o_ref[...] = _headnorm(acc, g_ref[...], hd, scale).astype(o_ref.dtype)

        @pl.when(pl.program_id(0) != 0)
        def _():
            o_ref[...] = acc.astype(o_ref.dtype)


W_COLS = 512


def _proj(h, w, cols, tm, mode, out_dtype, gain=None, hd=None, scale=1.0, name="proj"):
    m, k = h.shape
    base, stride, n_w, n_tiles = cols
    tn = n_w * W_COLS
    if gain is None:
        g = jnp.ones((1, tn), F32)
    else:
        g = jnp.tile(gain.astype(F32).reshape(1, -1), (1, tn // gain.shape[-1]))
    body = functools.partial(_proj_body, n_w=n_w, mode=mode, hd=hd, scale=scale)
    w_specs = [pl.BlockSpec((k, W_COLS), functools.partial(lambda j, i, off: (0, base + stride * j + off), off=off))
               for off in range(n_w)]
    return pl.pallas_call(
        body, name=name, grid=(n_tiles, m // tm),
        in_specs=[pl.BlockSpec((tm, k), lambda j, i: (i, 0))] + w_specs + [pl.BlockSpec((1, tn), lambda j, i: (0, 0))],
        out_specs=pl.BlockSpec((tm, tn), lambda j, i: (i, j)),
        out_shape=jax.ShapeDtypeStruct((m, n_tiles * tn), out_dtype),
        scratch_shapes=[pltpu.VMEM((k, tn), BF16)], compiler_params=_cparams(2),
    )(h, *([w] * n_w), g)


PERM_ROWS = 256


def _perm_matrix(n, dil, inverse=False):
    per = n // dil
    o = lax.broadcasted_iota(I32, (n, n), 1 if inverse else 0)
    s = lax.broadcasted_iota(I32, (n, n), 0 if inverse else 1)
    return (s == (o % per) * dil + o // per).astype(BF16)


def _qkv_body(h_ref, wq_ref, wk_ref, wv_ref, gq_ref, gk_ref, q_ref, kv_ref, tok_ref, w_scr, *, dil, scale):
    c = pl.program_id(1)
    tm = h_ref.shape[0]
    nh = HEADS_PER_GROUP_A

    @pl.when(pl.program_id(0) == 0)
    def _():
        for i, w_ref in enumerate((wq_ref, wk_ref, wv_ref)):
            @pl.when(c == i)
            def _():
                w_scr[i] = w_ref[...].astype(BF16)

    acc = jnp.dot(h_ref[...], w_scr[c], preferred_element_type=F32)

    def put_perm(dst_ref, y, col0):
        if dil == 1:
            dst_ref[0, :, col0:col0 + GROUP_W_A] = y
            return
        pm = _perm_matrix(PERM_ROWS, dil)
        per = PERM_ROWS // dil
        for t in range(tm // PERM_ROWS):
            yp = jnp.dot(pm, y[t * PERM_ROWS:(t + 1) * PERM_ROWS], preferred_element_type=F32).astype(BF16)
            for r in range(dil):
                dst_ref[r, t * per:(t + 1) * per, col0:col0 + GROUP_W_A] = yp[r * per:(r + 1) * per]

    def put_tok(y, c0):
        for i in range(nh):
            tok_ref[pl.ds(c0 + i, tm, stride=2 * nh), :] = y[:, i * HEAD_DIM_A:(i + 1) * HEAD_DIM_A]

    @pl.when(c == 0)
    def _():
        put_perm(q_ref, _headnorm(acc, gq_ref[...], HEAD_DIM_A, scale).astype(BF16), 0)

    @pl.when(c == 1)
    def _():
        y = _headnorm(acc, gk_ref[...], HEAD_DIM_A, 1.0)
        put_tok(y, 0)
        put_perm(kv_ref, y.astype(BF16), 0)

    @pl.when(c == 2)
    def _():
        put_tok(acc, nh)
        put_perm(kv_ref, acc.astype(BF16), 0)


def _proj_qkv(h, w, sec, g, g_qa, g_ka, n_batch, seq, tm):
    n, k = h.shape
    _, dil = DIL_GROUPS[g]
    ln = seq // dil
    per = tm // dil
    mt = seq // tm
    assert seq % tm == 0 and tm % PERM_ROWS == 0 and (PERM_ROWS // dil) % 16 == 0
    tile = lambda v: jnp.tile(v.astype(F32).reshape(1, -1), (1, GROUP_W_A // v.shape[-1]))
    wspec = lambda blk: pl.BlockSpec((k, W_COLS), lambda m, c: (0, blk))
    q, kv, tok = pl.pallas_call(
        functools.partial(_qkv_body, dil=dil, scale=HEAD_DIM_A ** -0.5), name=f"proj_qkv{g}", grid=(n // tm, 3),
        in_specs=[pl.BlockSpec((tm, k), lambda m, c: (m, 0)),
                  wspec(sec["q"] + g), wspec(sec["k"] + g), wspec(sec["v"] + g),
                  pl.BlockSpec((1, GROUP_W_A), lambda m, c: (0, 0)), pl.BlockSpec((1, GROUP_W_A), lambda m, c: (0, 0))],
        out_specs=[pl.BlockSpec((None, dil, per, GROUP_W_A), lambda m, c: (m // mt, 0, m % mt, 0)),
                   pl.BlockSpec((None, dil, per, GROUP_W_A), lambda m, c: (m // mt, 0, m % mt, jnp.maximum(c - 1, 0))),
                   pl.BlockSpec((tm * 2 * HEADS_PER_GROUP_A, LANES), lambda m, c: (m, 0))],
        out_shape=[jax.ShapeDtypeStruct((n_batch, dil, ln, GROUP_W_A), BF16),
                   jax.ShapeDtypeStruct((n_batch, dil, ln, 2 * GROUP_W_A), BF16),
                   jax.ShapeDtypeStruct((n * 2 * HEADS_PER_GROUP_A, LANES), F32)],
        scratch_shapes=[pltpu.VMEM((3, k, W_COLS), BF16)], compiler_params=_cparams(2),
    )(h, w, w, w, tile(g_qa), tile(g_ka))
    return q.reshape(n, GROUP_W_A), kv.reshape(n, 2 * GROUP_W_A), tok


def _bias_from_table(tab_ref, bidx, col):
    acc = jnp.zeros(bidx.shape, F32)
    for kb in range(N_BUCKETS):
        acc = jnp.where(bidx == kb, tab_ref[kb, col], acc)
    return acc


def _split_hi_lo(x):
    hi = x.astype(BF16)
    return hi, (x - hi.astype(F32)).astype(BF16)


def _attn_p_body(tab_ref, bidx_ref, q_ref, kp_ref, kc_ref, vp_ref, vc_ref, o_ref, bias_scr, *, g):
    first = (pl.program_id(0) == 0) & (pl.program_id(1) == 0)

    @pl.when(first)
    def _():
        bidx = bidx_ref[...]
        for h in range(HEADS_PER_GROUP_A):
            bias_scr[h] = _bias_from_table(tab_ref, bidx, g * HEADS_PER_GROUP_A + h)

    qb = pl.program_id(1)
    blk = q_ref.shape[0]
    q = q_ref[...]
    k = jnp.concatenate([kp_ref[...], kc_ref[...]], axis=0)
    v = jnp.concatenate([vp_ref[...], vc_ref[...]], axis=0)
    row = lax.broadcasted_iota(I32, (blk, 2 * blk), 0)
    col = lax.broadcasted_iota(I32, (blk, 2 * blk), 1)
    dist = blk + row - col
    valid = (dist >= 0) & (dist <= blk) & ((qb > 0) | (col >= blk))
    lane = lax.broadcasted_iota(I32, (blk, LANES), 1)
    lse_blk = jnp.zeros((blk, LANES), F32)
    outs = []
    for h in range(HEADS_PER_GROUP_A):
        hs = slice(h * HEAD_DIM_A, (h + 1) * HEAD_DIM_A)
        s = lax.dot_general(q[:, hs], k[:, hs], (((1,), (1,)), ((), ())), preferred_element_type=F32)
        s = jnp.where(valid, s + bias_scr[h], NEG_INF)
        m = jnp.max(s, axis=-1, keepdims=True)
        p = jnp.exp(s - m)
        l = jnp.sum(p, axis=-1, keepdims=True)
        o = jnp.dot(p.astype(BF16), v[:, hs], preferred_element_type=F32) / l
        outs.append(o)
        lse_blk = jnp.where(lane == h, m + jnp.log(l), lse_blk)
    lse_hi, lse_lo = _split_hi_lo(lse_blk)
    o_ref[...] = jnp.concatenate([jnp.concatenate(outs, axis=1).astype(BF16), lse_hi, lse_lo], axis=1)


O_EXT_W = GROUP_W_A + 2 * LANES


def _attn_prompt(q, kv, rel_bias, g, n_batch, seq):
    win, dil = DIL_GROUPS[g]
    blk = win // dil
    nb = seq // dil // blk
    n = n_batch * seq
    a = jnp.arange(blk)[:, None]
    c = jnp.arange(2 * blk)[None, :]
    bidx = _rel_bucket(jnp.maximum(blk + a - c, 0) * dil)

    def cur(s, j):
        return s * nb + j

    def prev(s, j):
        return s * nb + jnp.maximum(j - 1, 0)

    return pl.pallas_call(
        functools.partial(_attn_p_body, g=g), name=f"attn_p{g}", grid=(n_batch * dil, nb),
        in_specs=[
            pl.BlockSpec(memory_space=pltpu.SMEM),
            pl.BlockSpec((blk, 2 * blk), lambda s, j: (0, 0)),
            pl.BlockSpec((blk, GROUP_W_A), lambda s, j: (cur(s, j), 0)),
            pl.BlockSpec((blk, GROUP_W_A), lambda s, j: (prev(s, j), 0)),
            pl.BlockSpec((blk, GROUP_W_A), lambda s, j: (cur(s, j), 0)),
            pl.BlockSpec((blk, GROUP_W_A), lambda s, j: (prev(s, j), 1)),
            pl.BlockSpec((blk, GROUP_W_A), lambda s, j: (cur(s, j), 1)),
        ],
        out_specs=pl.BlockSpec((blk, O_EXT_W), lambda s, j: (cur(s, j), 0)),
        out_shape=jax.ShapeDtypeStruct((n, O_EXT_W), BF16),
        scratch_shapes=[pltpu.VMEM((HEADS_PER_GROUP_A, blk, 2 * blk), F32)],
        compiler_params=_cparams(2),
    )(rel_bias, bidx, q, kv, kv, kv, kv)


def _attn_s_body(tab_ref, bidx_ref, q_ref, kvn_ref, cache_hbm, o_ref, newc_hbm,
                 raw, kv_scr, bias_scr, sem_in, sem_out, *, g, dil, lb, n_req):
    t_new = q_ref.shape[0]
    nh = HEADS_PER_GROUP_A
    n_rows = nh * t_new
    n_keys = kv_scr.shape[0]
    n_col = 2 * nh
    lb8 = lb * n_col
    n = pl.program_id(0)
    slot = n % 2

    def fetch(req, s):
        return pltpu.make_async_copy(cache_hbm.at[req], raw.at[s, pl.ds(0, lb8)], sem_in.at[s])

    def flush(req, s):
        return pltpu.make_async_copy(raw.at[s, pl.ds(t_new * n_col, lb8)], newc_hbm.at[req], sem_out.at[s])

    @pl.when(n == 0)
    def _():
        kv_scr[lb:, :] = jnp.zeros((n_keys - lb, kv_scr.shape[1]), BF16)
        bidx = bidx_ref[...]
        for h in range(nh):
            bias_scr[h * t_new:(h + 1) * t_new, :] = _bias_from_table(tab_ref, bidx, g * nh + h)
        fetch(0, 0).start()

    @pl.when(n >= 1)
    def _():
        flush(n - 1, 1 - slot).wait()

    @pl.when(n + 1 < n_req)
    def _():
        fetch(n + 1, 1 - slot).start()

    fetch(n, slot).wait()
    new = kvn_ref[...]
    for c in range(n_col):
        raw[slot, pl.ds(lb8 + c, t_new, stride=n_col), :] = new[:, c * LANES:(c + 1) * LANES]
    flush(n, slot).start()

    rows_per = min(lb, 256)
    for c in range(n_col):
        for r0 in range(0, lb, rows_per):
            kv_scr[r0:r0 + rows_per, c * LANES:(c + 1) * LANES] = raw[
                slot, pl.ds(r0 * n_col + c, rows_per, stride=n_col), :].astype(BF16)
    kv_scr[lb:lb + 2 * t_new, :] = jnp.concatenate([new, jnp.zeros_like(new)], axis=0).astype(BF16)

    q = q_ref[...]
    qt = jnp.concatenate([q] * nh, axis=0)
    rr = lax.broadcasted_iota(I32, (n_rows, GROUP_W_A), 0)
    cc = lax.broadcasted_iota(I32, (n_rows, GROUP_W_A), 1)
    qbd = jnp.where(cc // HEAD_DIM_A == rr // t_new, qt, 0.0).astype(BF16)

    s = lax.dot_general(qbd, kv_scr[:, :GROUP_W_A], (((1,), (1,)), ((), ())), preferred_element_type=F32)
    row = lax.broadcasted_iota(I32, (n_rows, n_keys), 0)
    col = lax.broadcasted_iota(I32, (n_rows, n_keys), 1)
    delta = lb + (row & (t_new - 1)) - col
    valid = (delta >= 0) & (delta <= lb) & ((delta & (dil - 1)) == 0)
    s = jnp.where(valid, s + bias_scr[...], NEG_INF)
    m = jnp.max(s, axis=-1, keepdims=True)
    p = jnp.exp(s - m)
    l = jnp.sum(p, axis=-1, keepdims=True)
    o = jnp.dot(p.astype(BF16), kv_scr[:, GROUP_W_A:], preferred_element_type=F32) / l
    lse = m + jnp.log(l)
    lane = lax.broadcasted_iota(I32, (t_new, LANES), 1)
    lse_blk = jnp.zeros((t_new, LANES), F32)
    outs = []
    for h in range(nh):
        outs.append(o[h * t_new:(h + 1) * t_new, h * HEAD_DIM_A:(h + 1) * HEAD_DIM_A])
        lse_blk = jnp.where(lane == h, lse[h * t_new:(h + 1) * t_new, :], lse_blk)
    o_ref[...] = jnp.concatenate(outs + [lse_blk, jnp.zeros_like(lse_blk)], axis=1)

    @pl.when(n == n_req - 1)
    def _():
        flush(n, slot).wait()


def _attn_sample(q, kv_new, cache, rel_bias, g, t_new):
    win, dil = DIL_GROUPS[g]
    _, n_req, lb = cache.shape[:3]
    n_col = 2 * HEADS_PER_GROUP_A
    assert t_new & (t_new - 1) == 0 and dil & (dil - 1) == 0 and lb == win and cache.shape[3:] == (2, 4, LANES)
    n_keys = lb + LANES
    t = jnp.arange(t_new)[:, None]
    c = jnp.arange(n_keys)[None, :]
    bidx = _rel_bucket(jnp.clip(lb + t - c, 0, lb))
    o, newc = pl.pallas_call(
        functools.partial(_attn_s_body, g=g, dil=dil, lb=lb, n_req=n_req), name=f"attn_s{g}", grid=(n_req,),
        in_specs=[
            pl.BlockSpec(memory_space=pltpu.SMEM),
            pl.BlockSpec((t_new, n_keys), lambda n: (0, 0)),
            pl.BlockSpec((t_new, GROUP_W_A), lambda n: (n, g)),
            pl.BlockSpec((t_new, 2 * GROUP_W_A), lambda n: (n, 0)),
            pl.BlockSpec(memory_space=pl.ANY),
        ],
        out_specs=[pl.BlockSpec((t_new, O_EXT_W), lambda n: (n, 0)),
                   pl.BlockSpec(memory_space=pl.ANY)],
        out_shape=[jax.ShapeDtypeStruct((n_req * t_new, O_EXT_W), F32),
                   jax.ShapeDtypeStruct((n_req, lb * n_col, LANES), F32)],
        scratch_shapes=[pltpu.VMEM((2, (lb + t_new) * n_col, LANES), F32),
                        pltpu.VMEM((n_keys, 2 * GROUP_W_A), BF16),
                        pltpu.VMEM((HEADS_PER_GROUP_A * t_new, n_keys), F32),
                        pltpu.SemaphoreType.DMA((2,)), pltpu.SemaphoreType.DMA((2,))],
        compiler_params=_cparams(1),
    )(rel_bias, bidx, q, kv_new, cache.reshape(n_req, lb * n_col, LANES))
    return o, newc.reshape(cache.shape)


def _gmlp_body(u_ref, v_ref, w_ref, b_ref, o_ref, *, period):
    c = w_ref.shape[1]
    nch = u_ref.shape[0] // c
    gd = u_ref.shape[1] // N_GROUPS_B
    i = lax.broadcasted_iota(I32, (c, c), 0)
    j = lax.broadcasted_iota(I32, (c, c), 1)
    mask = (j <= i) & ((i // period) == (j // period))
    b = b_ref[...]
    for g in range(N_GROUPS_B):
        gs = slice(g * gd, (g + 1) * gd)
        wg = jnp.where(mask, w_ref[g], 0.0).astype(BF16)
        vg = [v_ref[ch * c:(ch + 1) * c, gs].astype(BF16) for ch in range(nch)]
        vg = vg[0] if nch == 1 else jnp.concatenate(vg, axis=1)
        sg = jnp.dot(wg, vg, preferred_element_type=F32) + b[:, g:g + 1]
        for ch in range(nch):
            u = u_ref[ch * c:(ch + 1) * c, gs].astype(F32)
            o_ref[ch * c:(ch + 1) * c, gs] = (u * sg[:, ch * gd:(ch + 1) * gd]).astype(o_ref.dtype)


def _gmlp(u, v, w, b, tm, period):
    m, wb = u.shape
    c = w.shape[1]
    return pl.pallas_call(
        functools.partial(_gmlp_body, period=period), name="gmlp", grid=(m // tm,),
        in_specs=[pl.BlockSpec((tm, wb), lambda i: (i, 0)), pl.BlockSpec((tm, wb), lambda i: (i, 0)),
                  pl.BlockSpec((N_GROUPS_B, c, c), lambda i: (0, 0, 0)),
                  pl.BlockSpec((c, N_GROUPS_B), lambda i: (0, 0))],
        out_specs=pl.BlockSpec((tm, wb), lambda i: (i, 0)),
        out_shape=jax.ShapeDtypeStruct((m, wb), BF16), compiler_params=_cparams(1),
    )(u, v, w, b)


def _memattn_body(q_ref, kv_ref, o_ref):
    wm = N_HEADS_M * HEAD_DIM_M
    q = q_ref[...].astype(BF16)
    outs = []
    for h in range(N_HEADS_M):
        hs = slice(h * HEAD_DIM_M, (h + 1) * HEAD_DIM_M)
        k = kv_ref[:, hs].astype(BF16)
        v = kv_ref[:, wm + h * HEAD_DIM_M:wm + (h + 1) * HEAD_DIM_M].astype(BF16)
        s = lax.dot_general(q[:, hs], k, (((1,), (1,)), ((), ())), preferred_element_type=F32)
        m = jnp.max(s, axis=-1, keepdims=True)
        p = jnp.exp(s - m)
        l = jnp.sum(p, axis=-1, keepdims=True)
        outs.append(jnp.dot(p.astype(BF16), v, preferred_element_type=F32) / l)
    o_ref[...] = jnp.concatenate(outs, axis=1).astype(o_ref.dtype)


def _memattn(q, kv, tq, out_dtype):
    m, wm = q.shape
    n, nm, _ = kv.shape
    per = m // n // tq
    return pl.pallas_call(
        _memattn_body, name="memattn", grid=(m // tq,),
        in_specs=[pl.BlockSpec((tq, wm), lambda i: (i, 0)),
                  pl.BlockSpec((None, nm, 2 * wm), lambda i: (i // per, 0, 0))],
        out_specs=pl.BlockSpec((tq, wm), lambda i: (i, 0)),
        out_shape=jax.ShapeDtypeStruct((m, wm), out_dtype), compiler_params=_cparams(1),
    )(q, kv)


def _mix_body(gt_ref, o0_ref, o1_ref, o2_ref, ob_ref, om_ref, wpa_ref, wpb_ref, wpm_ref, z_ref, *, dils):
    d = z_ref.shape[1]
    tm = z_ref.shape[0]
    os_, ls = [], []
    for o_ref, dil in zip((o0_ref, o1_ref, o2_ref), dils):
        if dil == 1:
            x = o_ref[...].astype(F32)
        else:
            rows = jnp.concatenate([o_ref[r] for r in range(dil)], axis=0)
            x = jnp.dot(_perm_matrix(tm, dil, inverse=True), rows, preferred_element_type=F32)
        os_.append(x[:, :GROUP_W_A])
        ls.append(x[:, GROUP_W_A:GROUP_W_A + LANES] + x[:, GROUP_W_A + LANES:])
    l0, l1, l2 = ls
    mx = jnp.maximum(jnp.maximum(l0, l1), l2)
    e0, e1, e2 = jnp.exp(l0 - mx), jnp.exp(l1 - mx), jnp.exp(l2 - mx)
    den = e0 + e1 + e2
    w0, w1, w2 = e0 / den, e1 / den, e2 / den
    cols = []
    for h in range(HEADS_PER_GROUP_A):
        hs = slice(h * HEAD_DIM_A, (h + 1) * HEAD_DIM_A)
        cols.append(w0[:, h:h + 1] * os_[0][:, hs] + w1[:, h:h + 1] * os_[1][:, hs] + w2[:, h:h + 1] * os_[2][:, hs])
    oa = jnp.concatenate(cols, axis=1).astype(BF16)
    pa = jnp.dot(oa, wpa_ref[...], preferred_element_type=F32)
    pb = jnp.dot(ob_ref[...].astype(BF16), wpb_ref[...], preferred_element_type=F32)
    pm = jnp.dot(om_ref[...].astype(BF16), wpm_ref[...], preferred_element_type=F32)
    z = (gt_ref[:, 0:d].astype(F32) * pa + gt_ref[:, d:2 * d].astype(F32) * pb
         + gt_ref[:, 2 * d:3 * d].astype(F32) * pm)
    z_ref[...] = z.astype(z_ref.dtype)


def _mix(gates, o_list, dils, seq, ob, om, wpa, wpb, wpm, tm):
    m = gates.shape[0]
    d = wpa.shape[1]
    mt = seq // tm if any(dl > 1 for dl in dils) else 1

    def rows(width):
        return pl.BlockSpec((tm, width), lambda i: (i, 0))

    def whole(a):
        return pl.BlockSpec(a.shape, lambda i: (0, 0))

    o_specs, o_args = [], []
    for o, dil in zip(o_list, dils):
        if dil == 1:
            o_specs.append(rows(O_EXT_W))
            o_args.append(o)
        else:
            assert seq % tm == 0 and (tm // dil) % 16 == 0
            o_specs.append(pl.BlockSpec((None, dil, tm // dil, O_EXT_W), lambda i: (i // mt, 0, i % mt, 0)))
            o_args.append(o.reshape(m // seq, dil, seq // dil, O_EXT_W))
    return pl.pallas_call(
        functools.partial(_mix_body, dils=tuple(dils)), name="mix", grid=(m // tm,),
        in_specs=[rows(gates.shape[1])] + o_specs
                 + [rows(ob.shape[1]), rows(om.shape[1]), whole(wpa), whole(wpb), whole(wpm)],
        out_specs=rows(d),
        out_shape=jax.ShapeDtypeStruct((m, d), BF16), compiler_params=_cparams(1),
    )(gates, *o_args, ob, om, wpa, wpb, wpm)


def _route(logits):
    lane = lax.broadcasted_iota(I32, logits.shape, 1)
    lane_f = lane.astype(F32)
    is_g = lane < N_EXPERT_GROUPS
    gmax = jnp.max(jnp.where(is_g, logits, -jnp.inf), axis=1, keepdims=True)
    gsel = jnp.min(jnp.where(is_g & (logits == gmax), lane_f, float(LANES)), axis=1, keepdims=True).astype(I32)
    gden = jnp.sum(jnp.where(is_g, jnp.exp(logits - gmax), 0.0), axis=1, keepdims=True)
    pg = 1.0 / gden
    e_lane = lane - N_EXPERT_GROUPS
    in_grp = (e_lane >= 0) & (e_lane < N_EXPERTS) & ((e_lane // EXPERTS_PER_GROUP) == gsel)
    m1 = jnp.max(jnp.where(in_grp, logits, -jnp.inf), axis=1, keepdims=True)
    i1 = jnp.min(jnp.where(in_grp & (logits == m1), lane_f, float(LANES)), axis=1, keepdims=True).astype(I32)
    rest = in_grp & (lane != i1)
    m2 = jnp.max(jnp.where(rest, logits, -jnp.inf), axis=1, keepdims=True)
    i2 = jnp.min(jnp.where(rest & (logits == m2), lane_f, float(LANES)), axis=1, keepdims=True).astype(I32)
    e2 = jnp.exp(m2 - m1)
    w1 = pg / (1.0 + e2)
    w2 = pg * e2 / (1.0 + e2)
    eid = jnp.where(lane == 0, i1 - N_EXPERT_GROUPS, jnp.where(lane == 1, i2 - N_EXPERT_GROUPS, 0))
    wts = jnp.where(lane == 0, w1, jnp.where(lane == 1, w2, 0.0))
    return eid, wts


def _resid_body(x_ref, z_ref, wo_ref, gf_ref, wr_ref, br_ref, xmid_ref, hpk_ref, eid_ref, wts_ref):
    xm = x_ref[...] + jnp.dot(z_ref[...], wo_ref[...], preferred_element_type=F32)
    xmid_ref[...] = xm
    hf = _rms(xm, gf_ref[...])
    wr = wr_ref[...]
    wr_hi = wr.astype(BF16)
    wr_lo = (wr - wr_hi.astype(F32)).astype(BF16)
    hf_hi = hf.astype(BF16)
    hf_lo = (hf - hf_hi.astype(F32)).astype(BF16)
    logits = (jnp.dot(hf_hi, wr_hi, preferred_element_type=F32) + jnp.dot(hf_hi, wr_lo, preferred_element_type=F32)
              + jnp.dot(hf_lo, wr_hi, preferred_element_type=F32) + br_ref[...])
    eid, wts = _route(logits)
    eid_ref[...] = eid
    wts_ref[...] = wts
    _store_token_tiles(hpk_ref, hf)


def _resid(x2d, z, wo, g_ffn, w_r, b_r, tm):
    m, d = x2d.shape

    def rows(width):
        return pl.BlockSpec((tm, width), lambda i: (i, 0))

    def whole(a):
        return pl.BlockSpec(a.shape, lambda i: (0, 0))

    gf = g_ffn.reshape(1, d)
    nt = d // 2 // LANES
    return pl.pallas_call(
        _resid_body, name="resid", grid=(m // tm,),
        in_specs=[rows(d), rows(d), whole(wo), whole(gf), whole(w_r), whole(b_r)],
        out_specs=[rows(d), pl.BlockSpec((tm * nt, LANES), lambda i: (i, 0)), rows(LANES), rows(LANES)],
        out_shape=[jax.ShapeDtypeStruct((m, d), F32), jax.ShapeDtypeStruct((m * nt, LANES), U32),
                   jax.ShapeDtypeStruct((m, LANES), I32), jax.ShapeDtypeStruct((m, LANES), F32)],
        compiler_params=_cparams(1),
    )(x2d, z, wo, gf, w_r, b_r)


TOKEN_TILE = 8
GATHER_UNROLL = 8


def _gather_rows(idx_ref, src_hbm, dst, sem, n, priority=0):
    def body(i, carry):
        src = pl.multiple_of(idx_ref[0, 0, i], TOKEN_TILE)
        dst_row = pl.multiple_of(i * TOKEN_TILE, TOKEN_TILE)
        pltpu.make_async_copy(src_hbm.at[pl.ds(src, TOKEN_TILE)], dst.at[pl.ds(dst_row, TOKEN_TILE)],
                              sem).start(priority=priority)
        return carry
    lax.fori_loop(0, n, body, 0, unroll=GATHER_UNROLL)


def _ffn_body(blk_e_ref, nused_ref, eord_ref, enext_ref, idx_cur_ref, idx_nxt_ref, h_hbm, wg_hbm, wu_hbm, wd_hbm,
              y_ref, buf, sem, wg_f, wu_f, wd_f, wsem, wg_s, wu_s, wd_s):
    j = pl.program_id(0)
    nused = nused_ref[0]
    slot = j % 2
    rows = buf.shape[1] // TOKEN_TILE
    e = blk_e_ref[j]
    new_expert = (j == 0) | (e != blk_e_ref[jnp.maximum(j - 1, 0)])
    ws = eord_ref[j] % 2

    def fetch_w(expert, s):
        return [pltpu.make_async_copy(src.at[expert], dst.at[s], wsem.at[s, i])
                for i, (src, dst) in enumerate(((wg_hbm, wg_f), (wu_hbm, wu_f), (wd_hbm, wd_f)))]

    @pl.when(j == 0)
    def _():
        for c in fetch_w(e, ws):
            c.start()
        _gather_rows(idx_cur_ref, h_hbm, buf.at[0], sem.at[0], rows, priority=1)

    @pl.when(j + 1 < nused)
    def _():
        _gather_rows(idx_nxt_ref, h_hbm, buf.at[1 - slot], sem.at[1 - slot], rows, priority=1)

    @pl.when(new_expert)
    def _():
        nxt = enext_ref[j]

        @pl.when(nxt >= 0)
        def _():
            for c in fetch_w(nxt, 1 - ws):
                c.start()

        for c in fetch_w(e, ws):
            c.wait()
        wg_s[...] = wg_f[ws].astype(BF16)
        wu_s[...] = wu_f[ws].astype(BF16)
        wd_s[...] = wd_f[ws].astype(BF16)

    @pl.when(j < nused)
    def _():
        pltpu.make_async_copy(h_hbm.at[pl.ds(0, rows * TOKEN_TILE)], buf.at[slot], sem.at[slot]).wait()
        x = _load_token_tiles(buf, slot, 0, rows, TOKEN_TILE).astype(BF16)
        a = jnp.dot(x, wg_s[...], preferred_element_type=F32)
        b = jnp.dot(x, wu_s[...], preferred_element_type=F32)
        hm = (a * jax.nn.sigmoid(a) * b).astype(BF16)
        _store_token_tiles(y_ref, jnp.dot(hm, wd_s[...], preferred_element_type=F32))

    @pl.when(j >= nused)
    def _():
        y_ref[...] = jnp.zeros(y_ref.shape, y_ref.dtype)


def _ffn(hpk, src_tok, blk_e, nused, eord, enext, wg, wu, wd):
    n_blocks = blk_e.shape[0]
    rows = MOE_ROWS
    d, de = wg.shape[1], wg.shape[2]
    assert d // 2 == TOKEN_TILE * LANES
    idx3 = (src_tok * TOKEN_TILE).reshape(n_blocks, 1, rows)
    hbm = pl.BlockSpec(memory_space=pl.ANY)
    grid_spec = pltpu.PrefetchScalarGridSpec(
        num_scalar_prefetch=4, grid=(n_blocks,),
        in_specs=[
            pl.BlockSpec((1, 1, rows), lambda j, *_: (j, 0, 0), memory_space=pltpu.SMEM),
            pl.BlockSpec((1, 1, rows), lambda j, *_: (jnp.minimum(j + 1, n_blocks - 1), 0, 0),
                         memory_space=pltpu.SMEM),
            hbm, hbm, hbm, hbm,
        ],
        out_specs=pl.BlockSpec((rows * TOKEN_TILE, LANES), lambda j, *_: (j, 0)),
        scratch_shapes=[pltpu.VMEM((2, rows * TOKEN_TILE, LANES), U32), pltpu.SemaphoreType.DMA((2,)),
                        pltpu.VMEM((2, d, de), wg.dtype), pltpu.VMEM((2, d, de), wu.dtype),
                        pltpu.VMEM((2, de, d), wd.dtype), pltpu.SemaphoreType.DMA((2, 3)),
                        pltpu.VMEM((d, de), BF16), pltpu.VMEM((d, de), BF16), pltpu.VMEM((de, d), BF16)],
    )
    return pl.pallas_call(
        _ffn_body, name="ffn", grid_spec=grid_spec,
        out_shape=jax.ShapeDtypeStruct((n_blocks * rows * TOKEN_TILE, LANES), U32), compiler_params=_cparams(1),
    )(blk_e, nused, eord, enext, idx3, idx3, hpk, wg, wu, wd)


def _combine_body(idx_cur_ref, idx_nxt_ref, x_ref, w_ref, y_hbm, o_ref, buf, sem, *, n):
    i = pl.program_id(0)
    slot = i % 2
    rows = buf.shape[1] // TOKEN_TILE
    tm = rows // TOP_K

    @pl.when(i == 0)
    def _():
        _gather_rows(idx_cur_ref, y_hbm, buf.at[0], sem.at[0], rows)

    if n > 1:
        @pl.when(i + 1 < n)
        def _():
            _gather_rows(idx_nxt_ref, y_hbm, buf.at[1 - slot], sem.at[1 - slot], rows)

    pltpu.make_async_copy(y_hbm.at[pl.ds(0, rows * TOKEN_TILE)], buf.at[slot], sem.at[slot]).wait()
    w = w_ref[...]
    y0 = _load_token_tiles(buf, slot, 0, tm, TOKEN_TILE)
    y1 = _load_token_tiles(buf, slot, tm, tm, TOKEN_TILE)
    o_ref[...] = x_ref[...] + (w[:, 0:1] * y0 + w[:, 1:2] * y1)


def _combine(xmid, wts, dest, ypk, tm):
    m, d = xmid.shape
    nt = m // tm
    assert d // 2 == TOKEN_TILE * LANES
    idx3 = (dest * TOKEN_TILE).reshape(nt, tm, TOP_K).transpose(0, 2, 1).reshape(nt, 1, TOP_K * tm)
    return pl.pallas_call(
        functools.partial(_combine_body, n=nt), name="combine", grid=(nt,),
        in_specs=[
            pl.BlockSpec((1, 1, TOP_K * tm), lambda i: (i, 0, 0), memory_space=pltpu.SMEM),
            pl.BlockSpec((1, 1, TOP_K * tm), lambda i: (jnp.minimum(i + 1, nt - 1), 0, 0), memory_space=pltpu.SMEM),
            pl.BlockSpec((tm, d), lambda i: (i, 0)),
            pl.BlockSpec((tm, LANES), lambda i: (i, 0)),
            pl.BlockSpec(memory_space=pl.ANY),
        ],
        out_specs=pl.BlockSpec((tm, d), lambda i: (i, 0)),
        out_shape=jax.ShapeDtypeStruct((m, d), F32),
        scratch_shapes=[pltpu.VMEM((2, TOP_K * tm * TOKEN_TILE, LANES), U32), pltpu.SemaphoreType.DMA((2,))],
        compiler_params=_cparams(1),
    )(idx3, idx3, xmid, wts, ypk)


def _moe_plan(eid2):
    n_slot = eid2.shape[0] * TOP_K
    eid = eid2.reshape(n_slot)
    onehot = (eid[:, None] == jnp.arange(N_EXPERTS, dtype=I32)[None, :]).astype(I32)
    csum = jnp.cumsum(onehot, axis=0)
    rank = jnp.take_along_axis(csum, eid[:, None], axis=1)[:, 0] - 1
    counts = csum[-1]
    padded = (counts + MOE_ROWS - 1) // MOE_ROWS * MOE_ROWS
    pend = jnp.cumsum(padded)
    pstart = pend - padded
    dest = pstart[eid] + rank
    n_blocks = -(-n_slot // MOE_ROWS) + N_EXPERTS
    nused = (pend[-1] // MOE_ROWS).astype(I32)
    blk = jnp.minimum(jnp.arange(n_blocks, dtype=I32), nused - 1)
    blk_e = jnp.minimum(jnp.searchsorted(pend, blk * MOE_ROWS, side="right"), N_EXPERTS - 1).astype(I32)
    src_tok = jnp.zeros((n_blocks * MOE_ROWS,), I32).at[dest].set(jnp.arange(n_slot, dtype=I32) // TOP_K)
    has_rows = counts > 0
    ord_e = jnp.cumsum(has_rows.astype(I32)) - 1
    ids = jnp.arange(N_EXPERTS, dtype=I32)
    later = has_rows[None, :] & (ids[None, :] > ids[:, None])
    next_e = jnp.where(later.any(axis=1), jnp.argmax(later, axis=1), -1).astype(I32)
    return dest.astype(I32), src_tok, blk_e, nused.reshape(1), ord_e[blk_e].astype(I32), next_e[blk_e]


def _mixer(x2d, tm, p, attn_fn, gmlp_fn, mem_fn):
    h = _prep(x2d, p["g_mix"], tm)
    qdt = p["q_dtype"]
    wi, sec = p["w_in"], p["sec"]
    o_list, dils, seq = attn_fn(h)
    ub = _proj(h, wi, (sec["u"], 1, 1, sec["vb"] - sec["u"]), tm, "none", BF16, name="proj_ub")
    n_vb = sec["qm"] - sec["vb"]
    vb = _proj(h, wi, (sec["vb"], n_vb, n_vb, 1), tm, "norm", p["vb_dtype"], p["g_vb"], n_vb * W_COLS,
               name="proj_vb")
    qm = _proj(h, wi, (sec["qm"], 1, 1, sec["gt"] - sec["qm"]), tm, "norm", qdt, p["g_qm"], HEAD_DIM_M,
               HEAD_DIM_M ** -0.5, name="proj_qm")
    gates = _proj(h, wi, (sec["gt"], 2, 2, (sec["end"] - sec["gt"]) // 2), tm, "sigmoid", BF16, name="proj_gates")
    ob = gmlp_fn(ub, vb)
    om = mem_fn(qm)
    z = _mix(gates, o_list, dils, seq, ob, om, p["w_pa"], p["w_pb"], p["w_pm"], min(tm, 256))
    xmid, hpk, eid, wts = _resid(x2d, z, p["w_o"], p["g_ffn"], p["w_r"], p["b_r"], min(tm, 256))
    return vb, xmid, hpk, eid, wts


def kernel(x_prompt, x_sample, mem_prompt, cache_a0_kv, cache_a1_kv, cache_a2_kv, cache_mem_kv, rel_bias, g_mix, w_in, g_qa, g_ka, w_pa, g_vb, w_s, b_s, w_pb, g_mem, w_mk, w_mv, g_qm, g_km, w_pm, w_o, g_ffn, w_rg, b_rg, w_re, b_re, w_gate, w_up, w_down):
    n_b, seq, d = x_prompt.shape
    n_s, t_s, _ = x_sample.shape
    depth = w_in.shape[0]
    assert depth == 1
    l = 0
    caches = (cache_a0_kv, cache_a1_kv, cache_a2_kv)
    n_g = len(DIL_GROUPS)
    wb = w_pb.shape[1]
    wm = N_HEADS_M * HEAD_DIM_M

    offs = [0, WIDTH_A, 2 * WIDTH_A, 3 * WIDTH_A, 3 * WIDTH_A + wb, 3 * WIDTH_A + 2 * wb,
            3 * WIDTH_A + 2 * wb + wm, w_in.shape[2]]
    assert all(o % W_COLS == 0 for o in offs) and (offs[7] - offs[6]) % (2 * W_COLS) == 0
    sec = dict(zip(("q", "k", "v", "u", "vb", "qm", "gt", "end"), (o // W_COLS for o in offs)))
    p = {
        "g_mix": g_mix[l], "g_qa": g_qa[l], "g_ka": g_ka[l], "g_vb": g_vb[l], "g_qm": g_qm[l], "g_ffn": g_ffn[l],
        "w_in": w_in[l], "sec": sec,
        "w_pa": w_pa[l].astype(BF16), "w_pb": w_pb[l].astype(BF16), "w_pm": w_pm[l].astype(BF16),
        "w_o": w_o[l].astype(BF16),
    }
    n_r = N_EXPERT_GROUPS + N_EXPERTS
    p["w_r"] = jnp.concatenate([w_rg[l], w_re[l], jnp.zeros((d, LANES - n_r), F32)], axis=1)
    p["b_r"] = jnp.concatenate([b_rg[l], b_re[l], jnp.zeros((LANES - n_r,), F32)]).reshape(1, LANES)

    n_mem = mem_prompt.shape[1]
    h_mem = _prep(mem_prompt.reshape(n_b * n_mem, d), g_mem[l], 256)
    w_mkv = jnp.concatenate([w_mk[l], w_mv[l]], axis=1)
    n_mw = wm // W_COLS
    mkv = _proj(h_mem, w_mkv, (0, n_mw, n_mw, 2), 256, "norm_first", F32, g_km[l], HEAD_DIM_M, name="proj_mkv")
    new_mem_p = mkv.reshape(1, n_b, n_mem, 2, N_HEADS_M, HEAD_DIM_M)

    pp = dict(p, q_dtype=BF16, vb_dtype=BF16)

    new_p = []

    def attn_prompt(h):
        o_list = []
        for g, (win, _) in enumerate(DIL_GROUPS):
            q, kv, tok = _proj_qkv(h, w_in[l], sec, g, g_qa[l], g_ka[l], n_b, seq, 512)
            o_list.append(_attn_prompt(q, kv, rel_bias, g, n_b, seq))
            keep = min(win, seq)
            kv5 = tok.reshape(n_b, seq, 2, HEADS_PER_GROUP_A, HEAD_DIM_A)
            new_p.append((kv5 if keep == seq else kv5[:, seq - keep:])[None])
        return o_list, [dl for _, dl in DIL_GROUPS], seq

    bs_t = b_s[l].T
    _, xmid_p, hpk_p, eid_p, wts_p = _mixer(
        x_prompt.reshape(n_b * seq, d), 512, pp, attn_prompt,
        lambda u, v: _gmlp(u, v, w_s[l], bs_t, 512, CHUNK),
        lambda qm: _memattn(qm, mkv.reshape(n_b, n_mem, 2 * wm), 256, BF16))

    ps = dict(p, q_dtype=F32, vb_dtype=F32)
    m_s = n_s * t_s

    new_s = []

    def attn_sample(h):
        q = _proj(h, w_in[l], (sec["q"], 1, 1, n_g), m_s, "norm", F32, g_qa[l], HEAD_DIM_A, HEAD_DIM_A ** -0.5,
                  name="proj_q")
        o_list = []
        for g in range(n_g):
            kv = _proj(h, w_in[l], (sec["k"] + g, sec["v"] - sec["k"], 1, 2), m_s, "norm_first", F32, g_ka[l],
                       HEAD_DIM_A, name=f"proj_kv{g}")
            o, newc = _attn_sample(q, kv, caches[g][l:l + 1], rel_bias, g, t_s)
            o_list.append(o)
            new_s.append(newc)
        return o_list, [1] * n_g, t_s

    w_s_small = jnp.tile(w_s[l][:, :t_s, :t_s], (1, n_s, n_s))
    b_s_small = jnp.tile(bs_t[:t_s], (n_s, 1))
    vb_s, xmid_s, hpk_s, eid_s, wts_s = _mixer(
        x_sample.reshape(m_s, d), m_s, ps, attn_sample,
        lambda u, v: _gmlp(u, v, w_s_small, b_s_small, m_s, t_s),
        lambda qm: _memattn(qm, cache_mem_kv[l].reshape(n_s, n_mem, 2 * wm), t_s, F32))
    new_vb_s = vb_s.reshape(1, n_s, t_s, wb)

    n_p = n_b * seq
    hpk = jnp.concatenate([hpk_p, hpk_s], axis=0)
    eid2 = jnp.concatenate([eid_p[:, :TOP_K], eid_s[:, :TOP_K]], axis=0)
    dest, src_tok, blk_e, nused, eord, enext = _moe_plan(eid2)
    ypk = _ffn(hpk, src_tok, blk_e, nused, eord, enext, w_gate[l], w_up[l], w_down[l])
    dest2 = dest.reshape(-1, TOP_K)
    y_p = _combine(xmid_p, wts_p, dest2[:n_p], ypk, 256)
    y_s = _combine(xmid_s, wts_s, dest2[n_p:], ypk, m_s)

    return (y_p.reshape(n_b, seq, d), y_s.reshape(n_s, t_s, d), new_p[0], new_p[1], new_p[2], new_mem_p,
            new_s[0], new_s[1], new_s[2], new_vb_s)
```

```python
import functools
import math

import jax
import jax.numpy as jnp
from jax import lax
from jax.experimental import pallas as pl
from jax.experimental.pallas import tpu as pltpu

F32 = jnp.float32
BF16 = jnp.bfloat16
I32 = jnp.int32
U32 = jnp.uint32

EPS = 1e-6
NEG_INF = -1e30

HEAD_DIM_A = 128
HEADS_PER_GROUP_A = 4
DIL_GROUPS = ((128, 1), (512, 4), (2048, 16))
GROUP_W_A = HEADS_PER_GROUP_A * HEAD_DIM_A
WIDTH_A = len(DIL_GROUPS) * GROUP_W_A
CHUNK = 128
N_GROUPS_B = 8
N_HEADS_M = 4
HEAD_DIM_M = 256
N_BUCKETS = 32
MAX_EXACT = N_BUCKETS // 2
MAX_DISTANCE = 2048
N_EXPERT_GROUPS = 4
EXPERTS_PER_GROUP = 8
N_EXPERTS = N_EXPERT_GROUPS * EXPERTS_PER_GROUP
TOP_K = 2
LANES = 128
MOE_ROWS = 256
VMEM_LIMIT = 56 * 1024 * 1024


def _cparams(n_grid, vmem=VMEM_LIMIT):
    return pltpu.CompilerParams(dimension_semantics=("arbitrary",) * n_grid, vmem_limit_bytes=vmem)


def _rms(x, g):
    return x * lax.rsqrt(jnp.mean(x * x, axis=-1, keepdims=True) + EPS) * g


def _pack_bf16_pair(x):
    n = x.shape[1] // 2
    lo = lax.bitcast_convert_type(x[:, :n].astype(BF16).astype(F32), U32)
    hi = lax.bitcast_convert_type(x[:, n:].astype(BF16).astype(F32), U32)
    return (hi & jnp.uint32(0xFFFF0000)) | (lo >> 16)


def _unpack_bf16_pair(w):
    lo = lax.bitcast_convert_type(w << 16, F32)
    hi = lax.bitcast_convert_type(w & jnp.uint32(0xFFFF0000), F32)
    return lo, hi


def _store_token_tiles(ref, x):
    w = _pack_bf16_pair(x)
    m, n = w.shape
    nt = n // LANES
    for c in range(nt):
        ref[pl.ds(c, m, stride=nt), :] = w[:, c * LANES:(c + 1) * LANES]


def _load_token_tiles(ref, lead, row0, m, nt):
    los, his = [], []
    for c in range(nt):
        lo, hi = _unpack_bf16_pair(ref[lead, pl.ds(row0 * nt + c, m, stride=nt), :])
        los.append(lo)
        his.append(hi)
    return jnp.concatenate(los + his, axis=1)


def _rel_bucket(dist):
    d = jnp.maximum(dist, 1).astype(F32)
    large = MAX_EXACT + (jnp.log(d / MAX_EXACT) / math.log(MAX_DISTANCE / MAX_EXACT)
                         * (N_BUCKETS - MAX_EXACT)).astype(I32)
    return jnp.where(dist < MAX_EXACT, dist, jnp.minimum(large, N_BUCKETS - 1)).astype(I32)


def _prep_body(x_ref, g_ref, o_ref):
    o_ref[...] = _rms(x_ref[...], g_ref[...]).astype(o_ref.dtype)


def _prep(x2d, g, tm):
    m, d = x2d.shape
    return pl.pallas_call(
        _prep_body, name="prep", grid=(m // tm,),
        in_specs=[pl.BlockSpec((tm, d), lambda i: (i, 0)), pl.BlockSpec((1, d), lambda i: (0, 0))],
        out_specs=pl.BlockSpec((tm, d), lambda i: (i, 0)),
        out_shape=jax.ShapeDtypeStruct((m, d), BF16), compiler_params=_cparams(1),
    )(x2d, g.reshape(1, d))


def _headnorm(acc, g, hd, scale):
    outs = []
    for j in range(acc.shape[1] // hd):
        sl = acc[:, j * hd:(j + 1) * hd]
        outs.append(sl * lax.rsqrt(jnp.mean(sl * sl, axis=-1, keepdims=True) + EPS))
    y = outs[0] if len(outs) == 1 else jnp.concatenate(outs, axis=1)
    y = y * g
    return y * scale if scale != 1.0 else y


def _window_update_copies(cache, tok, newc, sems, req):
    rows, t8 = cache.shape[1], tok.shape[1]
    return (pltpu.make_async_copy(cache.at[req, pl.ds(t8, rows - t8)], newc.at[req, pl.ds(0, rows - t8)], sems.at[0]),
            pltpu.make_async_copy(tok.at[req], newc.at[req, pl.ds(rows - t8, t8)], sems.at[1]))


def _proj_body(h_ref, *refs, n_w, n_bg, mode, hd, scale):
    w_refs, g_ref = refs[:n_w], refs[n_w]
    bg_in = refs[n_w + 1:n_w + 1 + 2 * n_bg]
    o_ref = refs[n_w + 1 + 2 * n_bg]
    bg_out = refs[n_w + 2 + 2 * n_bg:n_w + 2 + 3 * n_bg]
    w_scr = refs[n_w + 2 + 3 * n_bg]
    cw = w_refs[0].shape[1]

    if n_bg:
        bg_sem = refs[n_w + 3 + 3 * n_bg]
        step = pl.program_id(0) * pl.num_programs(1) + pl.program_id(1)
        n_req = bg_in[0].shape[0]

        @pl.when(step < n_req)
        def _():
            for b in range(n_bg):
                for c in _window_update_copies(bg_in[2 * b], bg_in[2 * b + 1], bg_out[b], bg_sem.at[b], step):
                    c.start(priority=1)

        @pl.when(step == pl.num_programs(0) * pl.num_programs(1) - 1)
        def _():
            for b in range(n_bg):
                for c in _window_update_copies(bg_in[2 * b], bg_in[2 * b + 1], bg_out[b], bg_sem.at[b],
                                               slice(None)):
                    c.wait()

    @pl.when(pl.program_id(1) == 0)
    def _():
        for i, w_ref in enumerate(w_refs):
            w_scr[:, i * cw:(i + 1) * cw] = w_ref[...].astype(w_scr.dtype)

    acc = jnp.dot(h_ref[...], w_scr[...], preferred_element_type=F32)
    if mode == "none":
        o_ref[...] = acc.astype(o_ref.dtype)
    elif mode == "sigmoid":
        o_ref[...] = jax.nn.sigmoid(acc).astype(o_ref.dtype)
    elif mode == "norm":
        o_ref[...] = _headnorm(acc, g_ref[...], hd, scale).astype(o_ref.dtype)
    else:
        @pl.when(pl.program_id(0) == 0)
        def _():
            o_ref[...] = _headnorm(acc, g_ref[...], hd, scale).astype(o_ref.dtype)

        @pl.when(pl.program_id(0) != 0)
        def _():
            o_ref[...] = acc.astype(o_ref.dtype)


W_COLS = 512


def _proj(h, w, cols, tm, mode, out_dtype, gain=None, hd=None, scale=1.0, name="proj", bg=()):
    m, k = h.shape
    base, stride, n_w, n_tiles = cols
    tn = n_w * W_COLS
    if gain is None:
        g = jnp.ones((1, tn), F32)
    else:
        g = jnp.tile(gain.astype(F32).reshape(1, -1), (1, tn // gain.shape[-1]))
    n_bg = len(bg)
    assert all(c.shape[0] <= n_tiles * (m // tm) for c, _ in bg)
    body = functools.partial(_proj_body, n_w=n_w, n_bg=n_bg, mode=mode, hd=hd, scale=scale)
    w_specs = [pl.BlockSpec((k, W_COLS), functools.partial(lambda j, i, off: (0, base + stride * j + off), off=off))
               for off in range(n_w)]
    hbm = pl.BlockSpec(memory_space=pl.ANY)
    res = pl.pallas_call(
        body, name=name, grid=(n_tiles, m // tm),
        in_specs=[pl.BlockSpec((tm, k), lambda j, i: (i, 0))] + w_specs + [pl.BlockSpec((1, tn), lambda j, i: (0, 0))]
                 + [hbm] * (2 * n_bg),
        out_specs=[pl.BlockSpec((tm, tn), lambda j, i: (i, j))] + [hbm] * n_bg,
        out_shape=[jax.ShapeDtypeStruct((m, n_tiles * tn), out_dtype)]
                  + [jax.ShapeDtypeStruct(c.shape, c.dtype) for c, _ in bg],
        scratch_shapes=[pltpu.VMEM((k, tn), BF16)] + ([pltpu.SemaphoreType.DMA((n_bg, 2))] if n_bg else []),
        compiler_params=_cparams(2),
    )(h, *([w] * n_w), g, *[a for pair in bg for a in pair])
    return res if n_bg else res[0]


PERM_ROWS = 256


def _perm_matrix(n, dil, inverse=False):
    per = n // dil
    o = lax.broadcasted_iota(I32, (n, n), 1 if inverse else 0)
    s = lax.broadcasted_iota(I32, (n, n), 0 if inverse else 1)
    return (s == (o % per) * dil + o // per).astype(BF16)


def _qkv_body(h_ref, wq_ref, wk_ref, wv_ref, gq_ref, gk_ref, q_ref, kv_ref, tok_ref, w_scr, *, dil, scale):
    c = pl.program_id(1)
    tm = h_ref.shape[0]
    nh = HEADS_PER_GROUP_A

    @pl.when(pl.program_id(0) == 0)
    def _():
        for i, w_ref in enumerate((wq_ref, wk_ref, wv_ref)):
            @pl.when(c == i)
            def _():
                w_scr[i] = w_ref[...].astype(BF16)

    acc = jnp.dot(h_ref[...], w_scr[c], preferred_element_type=F32)

    def put_perm(dst_ref, y, col0):
        if dil == 1:
            dst_ref[0, :, col0:col0 + GROUP_W_A] = y.astype(dst_ref.dtype)
            return
        pm = _perm_matrix(PERM_ROWS, dil)
        per = PERM_ROWS // dil
        yb = y.astype(BF16)
        for t in range(tm // PERM_ROWS):
            yp = jnp.dot(pm, yb[t * PERM_ROWS:(t + 1) * PERM_ROWS], preferred_element_type=F32).astype(BF16)
            for r in range(dil):
                dst_ref[r, t * per:(t + 1) * per, col0:col0 + GROUP_W_A] = yp[r * per:(r + 1) * per]

    def put_tok(y, c0):
        for i in range(nh):
            tok_ref[pl.ds(c0 + i, tm, stride=2 * nh), :] = y[:, i * HEAD_DIM_A:(i + 1) * HEAD_DIM_A]

    @pl.when(c == 0)
    def _():
        put_perm(q_ref, _headnorm(acc, gq_ref[...], HEAD_DIM_A, scale), 0)

    @pl.when(c == 1)
    def _():
        y = _headnorm(acc, gk_ref[...], HEAD_DIM_A, 1.0)
        put_tok(y, 0)
        put_perm(kv_ref, y, 0)

    @pl.when(c == 2)
    def _():
        put_tok(acc, nh)
        put_perm(kv_ref, acc, 0)


def _proj_qkv(h, w, sec, g, g_qa, g_ka, n_batch, seq, tm, dil, act_dtype=BF16):
    n, k = h.shape
    ln = seq // dil
    per = tm // dil
    mt = seq // tm
    assert seq % tm == 0
    assert dil == 1 or (act_dtype == BF16 and tm % PERM_ROWS == 0 and (PERM_ROWS // dil) % 16 == 0)
    tile = lambda v: jnp.tile(v.astype(F32).reshape(1, -1), (1, GROUP_W_A // v.shape[-1]))
    wspec = lambda blk: pl.BlockSpec((k, W_COLS), lambda m, c: (0, blk))
    q, kv, tok = pl.pallas_call(
        functools.partial(_qkv_body, dil=dil, scale=HEAD_DIM_A ** -0.5), name=f"proj_qkv{g}", grid=(n // tm, 3),
        in_specs=[pl.BlockSpec((tm, k), lambda m, c: (m, 0)),
                  wspec(sec["q"] + g), wspec(sec["k"] + g), wspec(sec["v"] + g),
                  pl.BlockSpec((1, GROUP_W_A), lambda m, c: (0, 0)), pl.BlockSpec((1, GROUP_W_A), lambda m, c: (0, 0))],
        out_specs=[pl.BlockSpec((None, dil, per, GROUP_W_A), lambda m, c: (m // mt, 0, m % mt, 0)),
                   pl.BlockSpec((None, dil, per, GROUP_W_A), lambda m, c: (m // mt, 0, m % mt, jnp.maximum(c - 1, 0))),
                   pl.BlockSpec((tm * 2 * HEADS_PER_GROUP_A, LANES), lambda m, c: (m, 0))],
        out_shape=[jax.ShapeDtypeStruct((n_batch, dil, ln, GROUP_W_A), act_dtype),
                   jax.ShapeDtypeStruct((n_batch, dil, ln, 2 * GROUP_W_A), act_dtype),
                   jax.ShapeDtypeStruct((n * 2 * HEADS_PER_GROUP_A, LANES), F32)],
        scratch_shapes=[pltpu.VMEM((3, k, W_COLS), BF16)], compiler_params=_cparams(2),
    )(h, w, w, w, tile(g_qa), tile(g_ka))
    return q.reshape(n, GROUP_W_A), kv.reshape(n, 2 * GROUP_W_A), tok


def _bias_from_table(tab_ref, bidx, col):
    acc = jnp.zeros(bidx.shape, F32)
    for kb in range(N_BUCKETS):
        acc = jnp.where(bidx == kb, tab_ref[kb, col], acc)
    return acc


def _split_hi_lo(x):
    hi = x.astype(BF16)
    return hi, (x - hi.astype(F32)).astype(BF16)


def _attn_p_body(tab_ref, bidx_ref, q_ref, kp_ref, kc_ref, vp_ref, vc_ref, o_ref, bias_scr, *, g):
    first = (pl.program_id(0) == 0) & (pl.program_id(1) == 0)

    @pl.when(first)
    def _():
        bidx = bidx_ref[...]
        for h in range(HEADS_PER_GROUP_A):
            bias_scr[h] = _bias_from_table(tab_ref, bidx, g * HEADS_PER_GROUP_A + h)

    qb = pl.program_id(1)
    blk = q_ref.shape[0]
    q = q_ref[...]
    k = jnp.concatenate([kp_ref[...], kc_ref[...]], axis=0)
    v = jnp.concatenate([vp_ref[...], vc_ref[...]], axis=0)
    row = lax.broadcasted_iota(I32, (blk, 2 * blk), 0)
    col = lax.broadcasted_iota(I32, (blk, 2 * blk), 1)
    dist = blk + row - col
    valid = (dist >= 0) & (dist <= blk) & ((qb > 0) | (col >= blk))
    lane = lax.broadcasted_iota(I32, (blk, LANES), 1)
    lse_blk = jnp.zeros((blk, LANES), F32)
    outs = []
    for h in range(HEADS_PER_GROUP_A):
        hs = slice(h * HEAD_DIM_A, (h + 1) * HEAD_DIM_A)
        s = lax.dot_general(q[:, hs], k[:, hs], (((1,), (1,)), ((), ())), preferred_element_type=F32)
        s = jnp.where(valid, s + bias_scr[h], NEG_INF)
        m = jnp.max(s, axis=-1, keepdims=True)
        p = jnp.exp(s - m)
        l = jnp.sum(p, axis=-1, keepdims=True)
        o = jnp.dot(p.astype(BF16), v[:, hs], preferred_element_type=F32) / l
        outs.append(o)
        lse_blk = jnp.where(lane == h, m + jnp.log(l), lse_blk)
    lse_hi, lse_lo = _split_hi_lo(lse_blk)
    o_ref[...] = jnp.concatenate([jnp.concatenate(outs, axis=1).astype(BF16), lse_hi, lse_lo], axis=1)


O_EXT_W = GROUP_W_A + 2 * LANES


def _attn_prompt(q, kv, rel_bias, g, n_batch, seq):
    win, dil = DIL_GROUPS[g]
    blk = win // dil
    nb = seq // dil // blk
    n = n_batch * seq
    a = jnp.arange(blk)[:, None]
    c = jnp.arange(2 * blk)[None, :]
    bidx = _rel_bucket(jnp.maximum(blk + a - c, 0) * dil)

    def cur(s, j):
        return s * nb + j

    def prev(s, j):
        return s * nb + jnp.maximum(j - 1, 0)

    return pl.pallas_call(
        functools.partial(_attn_p_body, g=g), name=f"attn_p{g}", grid=(n_batch * dil, nb),
        in_specs=[
            pl.BlockSpec(memory_space=pltpu.SMEM),
            pl.BlockSpec((blk, 2 * blk), lambda s, j: (0, 0)),
            pl.BlockSpec((blk, GROUP_W_A), lambda s, j: (cur(s, j), 0)),
            pl.BlockSpec((blk, GROUP_W_A), lambda s, j: (prev(s, j), 0)),
            pl.BlockSpec((blk, GROUP_W_A), lambda s, j: (cur(s, j), 0)),
            pl.BlockSpec((blk, GROUP_W_A), lambda s, j: (prev(s, j), 1)),
            pl.BlockSpec((blk, GROUP_W_A), lambda s, j: (cur(s, j), 1)),
        ],
        out_specs=pl.BlockSpec((blk, O_EXT_W), lambda s, j: (cur(s, j), 0)),
        out_shape=jax.ShapeDtypeStruct((n, O_EXT_W), BF16),
        scratch_shapes=[pltpu.VMEM((HEADS_PER_GROUP_A, blk, 2 * blk), F32)],
        compiler_params=_cparams(2),
    )(rel_bias, bidx, q, kv, kv, kv, kv)


def _attn_s_body(tab_ref, bidx_ref, q_ref, kvn_ref, cache_hbm, o_ref, raw, kv_scr, bias_scr, sem_in,
                 *, g, dil, lb, n_req):
    t_new = q_ref.shape[0]
    nh = HEADS_PER_GROUP_A
    n_rows = nh * t_new
    n_keys = kv_scr.shape[0]
    n_col = 2 * nh
    lb8 = lb * n_col
    n = pl.program_id(0)
    slot = n % 2

    def fetch(req, s):
        return pltpu.make_async_copy(cache_hbm.at[req], raw.at[s], sem_in.at[s])

    @pl.when(n == 0)
    def _():
        kv_scr[lb:, :] = jnp.zeros((n_keys - lb, kv_scr.shape[1]), BF16)
        bidx = bidx_ref[...]
        for h in range(nh):
            bias_scr[h * t_new:(h + 1) * t_new, :] = _bias_from_table(tab_ref, bidx, g * nh + h)
        fetch(0, 0).start()

    @pl.when(n + 1 < n_req)
    def _():
        fetch(n + 1, 1 - slot).start()

    fetch(n, slot).wait()
    new = kvn_ref[...]

    rows_per = min(lb, 256)
    for c in range(n_col):
        for r0 in range(0, lb, rows_per):
            kv_scr[r0:r0 + rows_per, c * LANES:(c + 1) * LANES] = raw[
                slot, pl.ds(r0 * n_col + c, rows_per, stride=n_col), :].astype(BF16)
    kv_scr[lb:lb + 2 * t_new, :] = jnp.concatenate([new, jnp.zeros_like(new)], axis=0).astype(BF16)

    q = q_ref[...]
    qt = jnp.concatenate([q] * nh, axis=0)
    rr = lax.broadcasted_iota(I32, (n_rows, GROUP_W_A), 0)
    cc = lax.broadcasted_iota(I32, (n_rows, GROUP_W_A), 1)
    qbd = jnp.where(cc // HEAD_DIM_A == rr // t_new, qt, 0.0).astype(BF16)

    s = lax.dot_general(qbd, kv_scr[:, :GROUP_W_A], (((1,), (1,)), ((), ())), preferred_element_type=F32)
    row = lax.broadcasted_iota(I32, (n_rows, n_keys), 0)
    col = lax.broadcasted_iota(I32, (n_rows, n_keys), 1)
    delta = lb + (row & (t_new - 1)) - col
    valid = (delta >= 0) & (delta <= lb) & ((delta & (dil - 1)) == 0)
    s = jnp.where(valid, s + bias_scr[...], NEG_INF)
    m = jnp.max(s, axis=-1, keepdims=True)
    p = jnp.exp(s - m)
    l = jnp.sum(p, axis=-1, keepdims=True)
    o = jnp.dot(p.astype(BF16), kv_scr[:, GROUP_W_A:], preferred_element_type=F32) / l
    lse = m + jnp.log(l)
    lane = lax.broadcasted_iota(I32, (t_new, LANES), 1)
    lse_blk = jnp.zeros((t_new, LANES), F32)
    outs = []
    for h in range(nh):
        outs.append(o[h * t_new:(h + 1) * t_new, h * HEAD_DIM_A:(h + 1) * HEAD_DIM_A])
        lse_blk = jnp.where(lane == h, lse[h * t_new:(h + 1) * t_new, :], lse_blk)
    o_ref[...] = jnp.concatenate(outs + [lse_blk, jnp.zeros_like(lse_blk)], axis=1)


def _attn_sample(q, kv_new, cache, rel_bias, g, t_new):
    win, dil = DIL_GROUPS[g]
    n_col = 2 * HEADS_PER_GROUP_A
    n_req, lb = cache.shape[0], cache.shape[1] // n_col
    assert t_new & (t_new - 1) == 0 and dil & (dil - 1) == 0 and lb == win
    n_keys = lb + LANES
    t = jnp.arange(t_new)[:, None]
    c = jnp.arange(n_keys)[None, :]
    bidx = _rel_bucket(jnp.clip(lb + t - c, 0, lb))
    return pl.pallas_call(
        functools.partial(_attn_s_body, g=g, dil=dil, lb=lb, n_req=n_req), name=f"attn_s{g}", grid=(n_req,),
        in_specs=[
            pl.BlockSpec(memory_space=pltpu.SMEM),
            pl.BlockSpec((t_new, n_keys), lambda n: (0, 0)),
            pl.BlockSpec((t_new, GROUP_W_A), lambda n: (n, 0)),
            pl.BlockSpec((t_new, 2 * GROUP_W_A), lambda n: (n, 0)),
            pl.BlockSpec(memory_space=pl.ANY),
        ],
        out_specs=pl.BlockSpec((t_new, O_EXT_W), lambda n: (n, 0)),
        out_shape=jax.ShapeDtypeStruct((n_req * t_new, O_EXT_W), F32),
        scratch_shapes=[pltpu.VMEM((2, lb * n_col, LANES), F32),
                        pltpu.VMEM((n_keys, 2 * GROUP_W_A), BF16),
                        pltpu.VMEM((HEADS_PER_GROUP_A * t_new, n_keys), F32),
                        pltpu.SemaphoreType.DMA((2,))],
        compiler_params=_cparams(1),
    )(rel_bias, bidx, q, kv_new, cache)


def _gmlp_body(u_ref, v_ref, w_ref, b_ref, o_ref, *, period):
    c = w_ref.shape[1]
    nch = u_ref.shape[0] // c
    gd = u_ref.shape[1] // N_GROUPS_B
    i = lax.broadcasted_iota(I32, (c, c), 0)
    j = lax.broadcasted_iota(I32, (c, c), 1)
    mask = (j <= i) & ((i // period) == (j // period))
    b = b_ref[...]
    for g in range(N_GROUPS_B):
        gs = slice(g * gd, (g + 1) * gd)
        wg = jnp.where(mask, w_ref[g], 0.0).astype(BF16)
        vg = [v_ref[ch * c:(ch + 1) * c, gs].astype(BF16) for ch in range(nch)]
        vg = vg[0] if nch == 1 else jnp.concatenate(vg, axis=1)
        sg = jnp.dot(wg, vg, preferred_element_type=F32) + b[:, g:g + 1]
        for ch in range(nch):
            u = u_ref[ch * c:(ch + 1) * c, gs].astype(F32)
            o_ref[ch * c:(ch + 1) * c, gs] = (u * sg[:, ch * gd:(ch + 1) * gd]).astype(o_ref.dtype)


def _gmlp(u, v, w, b, tm, period):
    m, wb = u.shape
    c = w.shape[1]
    return pl.pallas_call(
        functools.partial(_gmlp_body, period=period), name="gmlp", grid=(m // tm,),
        in_specs=[pl.BlockSpec((tm, wb), lambda i: (i, 0)), pl.BlockSpec((tm, wb), lambda i: (i, 0)),
                  pl.BlockSpec((N_GROUPS_B, c, c), lambda i: (0, 0, 0)),
                  pl.BlockSpec((c, N_GROUPS_B), lambda i: (0, 0))],
        out_specs=pl.BlockSpec((tm, wb), lambda i: (i, 0)),
        out_shape=jax.ShapeDtypeStruct((m, wb), BF16), compiler_params=_cparams(1),
    )(u, v, w, b)


def _gmlp_new_body(u_ref, v_ref, w_ref, b_ref, o_ref):
    t = w_ref.shape[0]
    n = u_ref.shape[0] // t
    width = u_ref.shape[1]
    v = v_ref[...].astype(F32).reshape(n, t, width)
    sg = jnp.broadcast_to(b_ref[...][None], (n, t, width))
    p_row = lax.broadcasted_iota(I32, (t, width), 0)
    for q in range(t):
        sg = sg + jnp.where(p_row >= q, w_ref[q], 0.0)[None] * v[:, q:q + 1, :]
    o_ref[...] = (u_ref[...].astype(F32).reshape(n, t, width) * sg).reshape(n * t, width).astype(o_ref.dtype)


def _gmlp_new(u, v, w_s, b_s, t):
    m, wb = u.shape
    gd = wb // N_GROUPS_B
    wq = jnp.repeat(jnp.transpose(w_s[:, :t, :t], (2, 1, 0)), gd, axis=-1)
    bq = jnp.repeat(b_s[:, :t].T, gd, axis=-1)
    whole = lambda a: pl.BlockSpec(a.shape, lambda i: (0,) * a.ndim)
    return pl.pallas_call(
        _gmlp_new_body, name="gmlp_new", grid=(1,),
        in_specs=[whole(u), whole(v), whole(wq), whole(bq)], out_specs=whole(u),
        out_shape=jax.ShapeDtypeStruct((m, wb), BF16), compiler_params=_cparams(1),
    )(u, v, wq, bq)


def _memattn_body(q_ref, kv_ref, o_ref):
    wm = N_HEADS_M * HEAD_DIM_M
    q = q_ref[...].astype(BF16)
    outs = []
    for h in range(N_HEADS_M):
        hs = slice(h * HEAD_DIM_M, (h + 1) * HEAD_DIM_M)
        if len(kv_ref.shape) == 4:
            k = kv_ref[:, 0, h, :].astype(BF16)
            v = kv_ref[:, 1, h, :].astype(BF16)
        else:
            k = kv_ref[:, hs].astype(BF16)
            v = kv_ref[:, wm + h * HEAD_DIM_M:wm + (h + 1) * HEAD_DIM_M].astype(BF16)
        s = lax.dot_general(q[:, hs], k, (((1,), (1,)), ((), ())), preferred_element_type=F32)
        m = jnp.max(s, axis=-1, keepdims=True)
        p = jnp.exp(s - m)
        l = jnp.sum(p, axis=-1, keepdims=True)
        outs.append(jnp.dot(p.astype(BF16), v, preferred_element_type=F32) / l)
    o_ref[...] = jnp.concatenate(outs, axis=1).astype(o_ref.dtype)


def _memattn(q, kv, tq, out_dtype):
    m, wm = q.shape
    n = kv.shape[0]
    per = m // n // tq
    kv_spec = pl.BlockSpec((None,) + kv.shape[1:], lambda i: (i // per,) + (0,) * (kv.ndim - 1))
    return pl.pallas_call(
        _memattn_body, name="memattn", grid=(m // tq,),
        in_specs=[pl.BlockSpec((tq, wm), lambda i: (i, 0)), kv_spec],
        out_specs=pl.BlockSpec((tq, wm), lambda i: (i, 0)),
        out_shape=jax.ShapeDtypeStruct((m, wm), out_dtype), compiler_params=_cparams(1),
    )(q, kv)


def _mix_body(gt_ref, o0_ref, o1_ref, o2_ref, ob_ref, om_ref, wpa_ref, wpb_ref, wpm_ref, z_ref, *, dils):
    d = z_ref.shape[1]
    tm = z_ref.shape[0]
    os_, ls = [], []
    for o_ref, dil in zip((o0_ref, o1_ref, o2_ref), dils):
        if dil == 1:
            x = o_ref[...].astype(F32)
        else:
            rows = jnp.concatenate([o_ref[r] for r in range(dil)], axis=0)
            x = jnp.dot(_perm_matrix(tm, dil, inverse=True), rows, preferred_element_type=F32)
        os_.append(x[:, :GROUP_W_A])
        ls.append(x[:, GROUP_W_A:GROUP_W_A + LANES] + x[:, GROUP_W_A + LANES:])
    l0, l1, l2 = ls
    mx = jnp.maximum(jnp.maximum(l0, l1), l2)
    e0, e1, e2 = jnp.exp(l0 - mx), jnp.exp(l1 - mx), jnp.exp(l2 - mx)
    den = e0 + e1 + e2
    w0, w1, w2 = e0 / den, e1 / den, e2 / den
    cols = []
    for h in range(HEADS_PER_GROUP_A):
        hs = slice(h * HEAD_DIM_A, (h + 1) * HEAD_DIM_A)
        cols.append(w0[:, h:h + 1] * os_[0][:, hs] + w1[:, h:h + 1] * os_[1][:, hs] + w2[:, h:h + 1] * os_[2][:, hs])
    oa = jnp.concatenate(cols, axis=1).astype(BF16)
    pa = jnp.dot(oa, wpa_ref[...], preferred_element_type=F32)
    pb = jnp.dot(ob_ref[...].astype(BF16), wpb_ref[...], preferred_element_type=F32)
    pm = jnp.dot(om_ref[...].astype(BF16), wpm_ref[...], preferred_element_type=F32)
    z = (gt_ref[:, 0:d].astype(F32) * pa + gt_ref[:, d:2 * d].astype(F32) * pb
         + gt_ref[:, 2 * d:3 * d].astype(F32) * pm)
    z_ref[...] = z.astype(z_ref.dtype)


def _mix(gates, o_list, dils, seq, ob, om, wpa, wpb, wpm, tm):
    m = gates.shape[0]
    d = wpa.shape[1]
    mt = seq // tm if any(dl > 1 for dl in dils) else 1

    def rows(width):
        return pl.BlockSpec((tm, width), lambda i: (i, 0))

    def whole(a):
        return pl.BlockSpec(a.shape, lambda i: (0, 0))

    o_specs, o_args = [], []
    for o, dil in zip(o_list, dils):
        if dil == 1:
            o_specs.append(rows(O_EXT_W))
            o_args.append(o)
        else:
            assert seq % tm == 0 and (tm // dil) % 16 == 0
            o_specs.append(pl.BlockSpec((None, dil, tm // dil, O_EXT_W), lambda i: (i // mt, 0, i % mt, 0)))
            o_args.append(o.reshape(m // seq, dil, seq // dil, O_EXT_W))
    return pl.pallas_call(
        functools.partial(_mix_body, dils=tuple(dils)), name="mix", grid=(m // tm,),
        in_specs=[rows(gates.shape[1])] + o_specs
                 + [rows(ob.shape[1]), rows(om.shape[1]), whole(wpa), whole(wpb), whole(wpm)],
        out_specs=rows(d),
        out_shape=jax.ShapeDtypeStruct((m, d), BF16), compiler_params=_cparams(1),
    )(gates, *o_args, ob, om, wpa, wpb, wpm)


def _route(logits):
    lane = lax.broadcasted_iota(I32, logits.shape, 1)
    lane_f = lane.astype(F32)
    is_g = lane < N_EXPERT_GROUPS
    gmax = jnp.max(jnp.where(is_g, logits, -jnp.inf), axis=1, keepdims=True)
    gsel = jnp.min(jnp.where(is_g & (logits == gmax), lane_f, float(LANES)), axis=1, keepdims=True).astype(I32)
    gden = jnp.sum(jnp.where(is_g, jnp.exp(logits - gmax), 0.0), axis=1, keepdims=True)
    pg = 1.0 / gden
    e_lane = lane - N_EXPERT_GROUPS
    in_grp = (e_lane >= 0) & (e_lane < N_EXPERTS) & ((e_lane // EXPERTS_PER_GROUP) == gsel)
    m1 = jnp.max(jnp.where(in_grp, logits, -jnp.inf), axis=1, keepdims=True)
    i1 = jnp.min(jnp.where(in_grp & (logits == m1), lane_f, float(LANES)), axis=1, keepdims=True).astype(I32)
    rest = in_grp & (lane != i1)
    m2 = jnp.max(jnp.where(rest, logits, -jnp.inf), axis=1, keepdims=True)
    i2 = jnp.min(jnp.where(rest & (logits == m2), lane_f, float(LANES)), axis=1, keepdims=True).astype(I32)
    e2 = jnp.exp(m2 - m1)
    w1 = pg / (1.0 + e2)
    w2 = pg * e2 / (1.0 + e2)
    eid = jnp.where(lane == 0, i1 - N_EXPERT_GROUPS, jnp.where(lane == 1, i2 - N_EXPERT_GROUPS, 0))
    wts = jnp.where(lane == 0, w1, jnp.where(lane == 1, w2, 0.0))
    return eid, wts


def _resid_body(x_ref, z_ref, wo_ref, gf_ref, wr_ref, br_ref, xmid_ref, hpk_ref, eid_ref, wts_ref):
    xm = x_ref[...] + jnp.dot(z_ref[...], wo_ref[...], preferred_element_type=F32)
    xmid_ref[...] = xm
    hf = _rms(xm, gf_ref[...])
    wr = wr_ref[...]
    wr_hi = wr.astype(BF16)
    wr_lo = (wr - wr_hi.astype(F32)).astype(BF16)
    hf_hi = hf.astype(BF16)
    hf_lo = (hf - hf_hi.astype(F32)).astype(BF16)
    logits = (jnp.dot(hf_hi, wr_hi, preferred_element_type=F32) + jnp.dot(hf_hi, wr_lo, preferred_element_type=F32)
              + jnp.dot(hf_lo, wr_hi, preferred_element_type=F32) + br_ref[...])
    eid, wts = _route(logits)
    eid_ref[...] = eid
    wts_ref[...] = wts
    _store_token_tiles(hpk_ref, hf)


def _resid(x2d, z, wo, g_ffn, w_r, b_r, tm):
    m, d = x2d.shape

    def rows(width):
        return pl.BlockSpec((tm, width), lambda i: (i, 0))

    def whole(a):
        return pl.BlockSpec(a.shape, lambda i: (0, 0))

    gf = g_ffn.reshape(1, d)
    nt = d // 2 // LANES
    return pl.pallas_call(
        _resid_body, name="resid", grid=(m // tm,),
        in_specs=[rows(d), rows(d), whole(wo), whole(gf), whole(w_r), whole(b_r)],
        out_specs=[rows(d), pl.BlockSpec((tm * nt, LANES), lambda i: (i, 0)), rows(LANES), rows(LANES)],
        out_shape=[jax.ShapeDtypeStruct((m, d), F32), jax.ShapeDtypeStruct((m * nt, LANES), U32),
                   jax.ShapeDtypeStruct((m, LANES), I32), jax.ShapeDtypeStruct((m, LANES), F32)],
        compiler_params=_cparams(1),
    )(x2d, z, wo, gf, w_r, b_r)


TOKEN_TILE = 8
GATHER_UNROLL = 8


def _gather_rows(idx_ref, src_hbm, dst, sem, n, priority=0):
    def body(i, carry):
        src = pl.multiple_of(idx_ref[0, 0, i], TOKEN_TILE)
        dst_row = pl.multiple_of(i * TOKEN_TILE, TOKEN_TILE)
        pltpu.make_async_copy(src_hbm.at[pl.ds(src, TOKEN_TILE)], dst.at[pl.ds(dst_row, TOKEN_TILE)],
                              sem).start(priority=priority)
        return carry
    lax.fori_loop(0, n, body, 0, unroll=GATHER_UNROLL)


FFN_CHUNKS = 8


def _ffn_body(blk_e_ref, nused_ref, eord_ref, enext_ref, idx_cur_ref, idx_nxt_ref, idx_nx2_ref, h_hbm,
              wg_hbm, wu_hbm, wd_hbm, y_ref, buf, sem, x_scr, a_scr, b_scr, hm_scr,
              wg_f, wu_f, wd_f, wsem, wg_s, wu_s, wd_s):
    j = pl.program_id(0)
    nused = nused_ref[0]
    slot = j % 3
    rows = buf.shape[1] // TOKEN_TILE
    e = blk_e_ref[j]
    new_expert = (j == 0) | (e != blk_e_ref[jnp.maximum(j - 1, 0)])
    ws = eord_ref[j] % 2

    def fetch_w(expert, s):
        return [pltpu.make_async_copy(src.at[expert], dst.at[s], wsem.at[s, i])
                for i, (src, dst) in enumerate(((wg_hbm, wg_f), (wu_hbm, wu_f), (wd_hbm, wd_f)))]

    @pl.when(j == 0)
    def _():
        for c in fetch_w(e, ws):
            c.start()
        _gather_rows(idx_cur_ref, h_hbm, buf.at[0], sem.at[0], rows, priority=1)
        _gather_rows(idx_nxt_ref, h_hbm, buf.at[1], sem.at[1], rows, priority=1)

    @pl.when(new_expert)
    def _():
        nxt = enext_ref[j]

        @pl.when(nxt >= 0)
        def _():
            for c in fetch_w(nxt, 1 - ws):
                c.start()

        for c in fetch_w(e, ws):
            c.wait()
        wg_s[...] = wg_f[ws].astype(BF16)
        wu_s[...] = wu_f[ws].astype(BF16)
        wd_s[...] = wd_f[ws].astype(BF16)

    def wait_rows(s):
        pltpu.make_async_copy(h_hbm.at[pl.ds(0, rows * TOKEN_TILE)], buf.at[s], sem.at[s]).wait()

    @pl.when(j < nused)
    def _():
        s2 = (j + 2) % 3
        per = rows // FFN_CHUNKS

        def issue(c):
            for i in range(c * per, (c + 1) * per):
                src = pl.multiple_of(idx_nx2_ref[0, 0, i], TOKEN_TILE)
                pltpu.make_async_copy(h_hbm.at[pl.ds(src, TOKEN_TILE)],
                                      buf.at[s2, pl.ds(i * TOKEN_TILE, TOKEN_TILE)], sem.at[s2]).start(priority=1)

        wait_rows(slot)
        x_scr[...] = _load_token_tiles(buf, slot, 0, rows, TOKEN_TILE).astype(BF16)
        de = a_scr.shape[1]
        dn = de // 2
        for c in range(2):
            issue(c)
            a_scr[:, c * dn:(c + 1) * dn] = jnp.dot(x_scr[...], wg_s[:, c * dn:(c + 1) * dn],
                                                     preferred_element_type=F32)
        for c in range(2):
            issue(2 + c)
            b_scr[:, c * dn:(c + 1) * dn] = jnp.dot(x_scr[...], wu_s[:, c * dn:(c + 1) * dn],
                                                     preferred_element_type=F32)
        a = a_scr[...]
        hm_scr[...] = (a * jax.nn.sigmoid(a) * b_scr[...]).astype(BF16)
        half = wd_s.shape[1] // 2
        n_dc = FFN_CHUNKS - 4
        dq = half // n_dc
        for c in range(n_dc):
            issue(4 + c)
            lo = jnp.dot(hm_scr[...], wd_s[:, c * dq:(c + 1) * dq], preferred_element_type=F32)
            hi = jnp.dot(hm_scr[...], wd_s[:, half + c * dq:half + (c + 1) * dq], preferred_element_type=F32)
            w = _pack_bf16_pair(jnp.concatenate([lo, hi], axis=1))
            for t in range(dq // LANES):
                y_ref[pl.ds(c * (dq // LANES) + t, rows, stride=TOKEN_TILE), :] = w[:, t * LANES:(t + 1) * LANES]

        @pl.when(j == nused - 1)
        def _():
            wait_rows((j + 1) % 3)
            wait_rows(s2)

    @pl.when(j >= nused)
    def _():
        y_ref[...] = jnp.zeros(y_ref.shape, y_ref.dtype)


def _ffn(hpk, src_tok, blk_e, nused, eord, enext, wg, wu, wd):
    n_blocks = blk_e.shape[0]
    rows = MOE_ROWS
    d, de = wg.shape[1], wg.shape[2]
    assert d // 2 == TOKEN_TILE * LANES and rows % FFN_CHUNKS == 0 and (d // 2) % ((FFN_CHUNKS - 4) * LANES) == 0
    assert n_blocks > N_EXPERTS + 1
    idx3 = (src_tok * TOKEN_TILE).reshape(n_blocks, 1, rows)
    hbm = pl.BlockSpec(memory_space=pl.ANY)

    def idx_spec(ahead):
        return pl.BlockSpec((1, 1, rows), lambda j, be, nu, *_: (jnp.minimum(j + ahead, nu[0] - 1), 0, 0),
                            memory_space=pltpu.SMEM)

    grid_spec = pltpu.PrefetchScalarGridSpec(
        num_scalar_prefetch=4, grid=(n_blocks,),
        in_specs=[idx_spec(0), idx_spec(1), idx_spec(2), hbm, hbm, hbm, hbm],
        out_specs=pl.BlockSpec((rows * TOKEN_TILE, LANES), lambda j, *_: (j, 0)),
        scratch_shapes=[pltpu.VMEM((3, rows * TOKEN_TILE, LANES), U32), pltpu.SemaphoreType.DMA((3,)),
                        pltpu.VMEM((rows, d), BF16), pltpu.VMEM((rows, de), F32), pltpu.VMEM((rows, de), F32),
                        pltpu.VMEM((rows, de), BF16),
                        pltpu.VMEM((2, d, de), wg.dtype), pltpu.VMEM((2, d, de), wu.dtype),
                        pltpu.VMEM((2, de, d), wd.dtype), pltpu.SemaphoreType.DMA((2, 3)),
                        pltpu.VMEM((d, de), BF16), pltpu.VMEM((d, de), BF16), pltpu.VMEM((de, d), BF16)],
    )
    return pl.pallas_call(
        _ffn_body, name="ffn", grid_spec=grid_spec,
        out_shape=jax.ShapeDtypeStruct((n_blocks * rows * TOKEN_TILE, LANES), U32), compiler_params=_cparams(1),
    )(blk_e, nused, eord, enext, idx3, idx3, idx3, hpk, wg, wu, wd)


def _combine_body(idx_cur_ref, idx_nxt_ref, x_ref, w_ref, y_hbm, o_ref, buf, sem, *, n):
    i = pl.program_id(0)
    slot = i % 2
    rows = buf.shape[1] // TOKEN_TILE
    tm = rows // TOP_K

    @pl.when(i == 0)
    def _():
        _gather_rows(idx_cur_ref, y_hbm, buf.at[0], sem.at[0], rows)

    if n > 1:
        @pl.when(i + 1 < n)
        def _():
            _gather_rows(idx_nxt_ref, y_hbm, buf.at[1 - slot], sem.at[1 - slot], rows)

    pltpu.make_async_copy(y_hbm.at[pl.ds(0, rows * TOKEN_TILE)], buf.at[slot], sem.at[slot]).wait()
    w = w_ref[...]
    y0 = _load_token_tiles(buf, slot, 0, tm, TOKEN_TILE)
    y1 = _load_token_tiles(buf, slot, tm, tm, TOKEN_TILE)
    o_ref[...] = x_ref[...] + (w[:, 0:1] * y0 + w[:, 1:2] * y1)


def _combine(xmid, wts, dest, ypk, tm):
    m, d = xmid.shape
    nt = m // tm
    assert d // 2 == TOKEN_TILE * LANES
    idx3 = (dest * TOKEN_TILE).reshape(nt, tm, TOP_K).transpose(0, 2, 1).reshape(nt, 1, TOP_K * tm)
    return pl.pallas_call(
        functools.partial(_combine_body, n=nt), name="combine", grid=(nt,),
        in_specs=[
            pl.BlockSpec((1, 1, TOP_K * tm), lambda i: (i, 0, 0), memory_space=pltpu.SMEM),
            pl.BlockSpec((1, 1, TOP_K * tm), lambda i: (jnp.minimum(i + 1, nt - 1), 0, 0), memory_space=pltpu.SMEM),
            pl.BlockSpec((tm, d), lambda i: (i, 0)),
            pl.BlockSpec((tm, LANES), lambda i: (i, 0)),
            pl.BlockSpec(memory_space=pl.ANY),
        ],
        out_specs=pl.BlockSpec((tm, d), lambda i: (i, 0)),
        out_shape=jax.ShapeDtypeStruct((m, d), F32),
        scratch_shapes=[pltpu.VMEM((2, TOP_K * tm * TOKEN_TILE, LANES), U32), pltpu.SemaphoreType.DMA((2,))],
        compiler_params=_cparams(1),
    )(idx3, idx3, xmid, wts, ypk)


def _moe_plan(eid2):
    n_slot = eid2.shape[0] * TOP_K
    eid = eid2.reshape(n_slot)
    onehot = (eid[:, None] == jnp.arange(N_EXPERTS, dtype=I32)[None, :]).astype(I32)
    csum = jnp.cumsum(onehot, axis=0)
    rank = jnp.take_along_axis(csum, eid[:, None], axis=1)[:, 0] - 1
    counts = csum[-1]
    padded = (counts + MOE_ROWS - 1) // MOE_ROWS * MOE_ROWS
    pend = jnp.cumsum(padded)
    pstart = pend - padded
    dest = pstart[eid] + rank
    n_blocks = -(-n_slot // MOE_ROWS) + N_EXPERTS
    nused = (pend[-1] // MOE_ROWS).astype(I32)
    blk = jnp.minimum(jnp.arange(n_blocks, dtype=I32), nused - 1)
    blk_e = jnp.minimum(jnp.searchsorted(pend, blk * MOE_ROWS, side="right"), N_EXPERTS - 1).astype(I32)
    src_tok = jnp.zeros((n_blocks * MOE_ROWS,), I32).at[dest].set(jnp.arange(n_slot, dtype=I32) // TOP_K)
    has_rows = counts > 0
    ord_e = jnp.cumsum(has_rows.astype(I32)) - 1
    ids = jnp.arange(N_EXPERTS, dtype=I32)
    later = has_rows[None, :] & (ids[None, :] > ids[:, None])
    next_e = jnp.where(later.any(axis=1), jnp.argmax(later, axis=1), -1).astype(I32)
    return dest.astype(I32), src_tok, blk_e, nused.reshape(1), ord_e[blk_e].astype(I32), next_e[blk_e]


def _mixer(x2d, tm, p, attn_fn, gmlp_fn, mem_fn):
    h = _prep(x2d, p["g_mix"], tm)
    qdt = p["q_dtype"]
    wi, sec = p["w_in"], p["sec"]
    o_list, dils, seq = attn_fn(h)
    ub = _proj(h, wi, (sec["u"], 1, 1, sec["vb"] - sec["u"]), tm, "none", BF16, name="proj_ub")
    n_vb = sec["qm"] - sec["vb"]
    vb = _proj(h, wi, (sec["vb"], n_vb, n_vb, 1), tm, "norm", p["vb_dtype"], p["g_vb"], n_vb * W_COLS,
               name="proj_vb")
    qm = _proj(h, wi, (sec["qm"], 1, 1, sec["gt"] - sec["qm"]), tm, "norm", qdt, p["g_qm"], HEAD_DIM_M,
               HEAD_DIM_M ** -0.5, name="proj_qm")
    bg = tuple(p.get("bg", ()))
    res = _proj(h, wi, (sec["gt"], 2, 2, (sec["end"] - sec["gt"]) // 2), tm, "sigmoid", BF16, name="proj_gates",
                bg=bg)
    gates, bg_out = (res[0], res[1:]) if bg else (res, ())
    ob = gmlp_fn(ub, vb)
    om = mem_fn(qm)
    z = _mix(gates, o_list, dils, seq, ob, om, p["w_pa"], p["w_pb"], p["w_pm"], min(tm, 256))
    xmid, hpk, eid, wts = _resid(x2d, z, p["w_o"], p["g_ffn"], p["w_r"], p["b_r"], min(tm, 256))
    return vb, xmid, hpk, eid, wts, bg_out


def kernel(x_prompt, x_sample, mem_prompt, cache_a0_kv, cache_a1_kv, cache_a2_kv, cache_mem_kv, rel_bias, g_mix, w_in, g_qa, g_ka, w_pa, g_vb, w_s, b_s, w_pb, g_mem, w_mk, w_mv, g_qm, g_km, w_pm, w_o, g_ffn, w_rg, b_rg, w_re, b_re, w_gate, w_up, w_down):
    n_b, seq, d = x_prompt.shape
    n_s, t_s, _ = x_sample.shape
    depth = w_in.shape[0]
    assert depth == 1
    l = 0
    caches = (cache_a0_kv, cache_a1_kv, cache_a2_kv)
    n_g = len(DIL_GROUPS)
    wb = w_pb.shape[1]
    wm = N_HEADS_M * HEAD_DIM_M

    offs = [0, WIDTH_A, 2 * WIDTH_A, 3 * WIDTH_A, 3 * WIDTH_A + wb, 3 * WIDTH_A + 2 * wb,
            3 * WIDTH_A + 2 * wb + wm, w_in.shape[2]]
    assert all(o % W_COLS == 0 for o in offs) and (offs[7] - offs[6]) % (2 * W_COLS) == 0
    sec = dict(zip(("q", "k", "v", "u", "vb", "qm", "gt", "end"), (o // W_COLS for o in offs)))
    p = {
        "g_mix": g_mix[l], "g_qa": g_qa[l], "g_ka": g_ka[l], "g_vb": g_vb[l], "g_qm": g_qm[l], "g_ffn": g_ffn[l],
        "w_in": w_in[l], "sec": sec,
        "w_pa": w_pa[l].astype(BF16), "w_pb": w_pb[l].astype(BF16), "w_pm": w_pm[l].astype(BF16),
        "w_o": w_o[l].astype(BF16),
    }
    n_r = N_EXPERT_GROUPS + N_EXPERTS
    p["w_r"] = jnp.concatenate([w_rg[l], w_re[l], jnp.zeros((d, LANES - n_r), F32)], axis=1)
    p["b_r"] = jnp.concatenate([b_rg[l], b_re[l], jnp.zeros((LANES - n_r,), F32)]).reshape(1, LANES)

    n_mem = mem_prompt.shape[1]
    h_mem = _prep(mem_prompt.reshape(n_b * n_mem, d), g_mem[l], 256)
    w_mkv = jnp.concatenate([w_mk[l], w_mv[l]], axis=1)
    n_mw = wm // W_COLS
    mkv = _proj(h_mem, w_mkv, (0, n_mw, n_mw, 2), 256, "norm_first", F32, g_km[l], HEAD_DIM_M, name="proj_mkv")
    new_mem_p = mkv.reshape(1, n_b, n_mem, 2, N_HEADS_M, HEAD_DIM_M)

    ps = dict(p, q_dtype=F32, vb_dtype=F32)
    m_s = n_s * t_s
    n_col = 2 * HEADS_PER_GROUP_A
    cache3 = [c[l].reshape(n_s, c.shape[2] * n_col, LANES) for c in caches]
    bg_jobs = []

    def attn_sample(h):
        o_list = []
        for g in range(n_g):
            q, kv, tok = _proj_qkv(h, w_in[l], sec, g, g_qa[l], g_ka[l], 1, m_s, m_s, 1, act_dtype=F32)
            o_list.append(_attn_sample(q, kv, cache3[g], rel_bias, g, t_s))
            bg_jobs.append((cache3[g], tok.reshape(n_s, t_s * n_col, LANES)))
        return o_list, [1] * n_g, t_s

    vb_s, xmid_s, hpk_s, eid_s, wts_s, _ = _mixer(
        x_sample.reshape(m_s, d), m_s, ps, attn_sample,
        lambda u, v: _gmlp_new(u, v, w_s[l], b_s[l], t_s),
        lambda qm: _memattn(qm, cache_mem_kv[l], t_s, F32))
    new_vb_s = vb_s.reshape(1, n_s, t_s, wb)

    pp = dict(p, q_dtype=BF16, vb_dtype=BF16, bg=bg_jobs)
    new_p = []

    def attn_prompt(h):
        o_list = []
        for g, (win, dil) in enumerate(DIL_GROUPS):
            q, kv, tok = _proj_qkv(h, w_in[l], sec, g, g_qa[l], g_ka[l], n_b, seq, 512, dil)
            o_list.append(_attn_prompt(q, kv, rel_bias, g, n_b, seq))
            keep = min(win, seq)
            kv5 = tok.reshape(n_b, seq, 2, HEADS_PER_GROUP_A, HEAD_DIM_A)
            new_p.append((kv5 if keep == seq else kv5[:, seq - keep:])[None])
        return o_list, [dl for _, dl in DIL_GROUPS], seq

    _, xmid_p, hpk_p, eid_p, wts_p, new_windows = _mixer(
        x_prompt.reshape(n_b * seq, d), 512, pp, attn_prompt,
        lambda u, v: _gmlp(u, v, w_s[l], b_s[l].T, 512, CHUNK),
        lambda qm: _memattn(qm, mkv.reshape(n_b, n_mem, 2 * wm), 256, BF16))
    new_s = [w.reshape(c[l:l + 1].shape) for w, c in zip(new_windows, caches)]

    n_p = n_b * seq
    hpk = jnp.concatenate([hpk_p, hpk_s], axis=0)
    eid2 = jnp.concatenate([eid_p[:, :TOP_K], eid_s[:, :TOP_K]], axis=0)
    dest, src_tok, blk_e, nused, eord, enext = _moe_plan(eid2)
    ypk = _ffn(hpk, src_tok, blk_e, nused, eord, enext, w_gate[l], w_up[l], w_down[l])
    dest2 = dest.reshape(-1, TOP_K)
    y_p = _combine(xmid_p, wts_p, dest2[:n_p], ypk, 256)
    y_s = _combine(xmid_s, wts_s, dest2[n_p:], ypk, m_s)

    return (y_p.reshape(n_b, seq, d), y_s.reshape(n_s, t_s, d), new_p[0], new_p[1], new_p[2], new_mem_p,
            new_s[0], new_s[1], new_s[2], new_vb_s)
```

```python
import functools
import math

import jax
import jax.numpy as jnp
from jax import lax
from jax.experimental import pallas as pl
from jax.experimental.pallas import tpu as pltpu

F32 = jnp.float32
BF16 = jnp.bfloat16
I32 = jnp.int32
U32 = jnp.uint32

EPS = 1e-6
NEG_INF = -1e30

HEAD_DIM_A = 128
HEADS_PER_GROUP_A = 4
DIL_GROUPS = ((128, 1), (512, 4), (2048, 16))
GROUP_W_A = HEADS_PER_GROUP_A * HEAD_DIM_A
WIDTH_A = len(DIL_GROUPS) * GROUP_W_A
CHUNK = 128
N_GROUPS_B = 8
N_HEADS_M = 4
HEAD_DIM_M = 256
N_BUCKETS = 32
MAX_EXACT = N_BUCKETS // 2
MAX_DISTANCE = 2048
N_EXPERT_GROUPS = 4
EXPERTS_PER_GROUP = 8
N_EXPERTS = N_EXPERT_GROUPS * EXPERTS_PER_GROUP
TOP_K = 2
LANES = 128
MOE_ROWS = 256
VMEM_LIMIT = 56 * 1024 * 1024


def _cparams(n_grid, vmem=VMEM_LIMIT):
    return pltpu.CompilerParams(dimension_semantics=("arbitrary",) * n_grid, vmem_limit_bytes=vmem)


def _rms(x, g):
    return x * lax.rsqrt(jnp.mean(x * x, axis=-1, keepdims=True) + EPS) * g


def _pack_bf16_pair(x):
    n = x.shape[1] // 2
    lo = lax.bitcast_convert_type(x[:, :n].astype(BF16).astype(F32), U32)
    hi = lax.bitcast_convert_type(x[:, n:].astype(BF16).astype(F32), U32)
    return (hi & jnp.uint32(0xFFFF0000)) | (lo >> 16)


def _unpack_bf16_pair(w):
    lo = lax.bitcast_convert_type(w << 16, F32)
    hi = lax.bitcast_convert_type(w & jnp.uint32(0xFFFF0000), F32)
    return lo, hi


def _store_token_tiles(ref, x):
    w = _pack_bf16_pair(x)
    m, n = w.shape
    nt = n // LANES
    for c in range(nt):
        ref[pl.ds(c, m, stride=nt), :] = w[:, c * LANES:(c + 1) * LANES]


def _load_token_tiles(ref, lead, row0, m, nt):
    los, his = [], []
    view = ref if lead is None else ref.at[lead]
    for c in range(nt):
        lo, hi = _unpack_bf16_pair(view[pl.ds(row0 * nt + c, m, stride=nt), :])
        los.append(lo)
        his.append(hi)
    return jnp.concatenate(los + his, axis=1)


def _rel_bucket(dist):
    d = jnp.maximum(dist, 1).astype(F32)
    large = MAX_EXACT + (jnp.log(d / MAX_EXACT) / math.log(MAX_DISTANCE / MAX_EXACT)
                         * (N_BUCKETS - MAX_EXACT)).astype(I32)
    return jnp.where(dist < MAX_EXACT, dist, jnp.minimum(large, N_BUCKETS - 1)).astype(I32)


def _prep_body(x_ref, g_ref, o_ref):
    o_ref[...] = _rms(x_ref[...], g_ref[...]).astype(o_ref.dtype)


def _prep(x2d, g, tm):
    m, d = x2d.shape
    return pl.pallas_call(
        _prep_body, name="prep", grid=(m // tm,),
        in_specs=[pl.BlockSpec((tm, d), lambda i: (i, 0)), pl.BlockSpec((1, d), lambda i: (0, 0))],
        out_specs=pl.BlockSpec((tm, d), lambda i: (i, 0)),
        out_shape=jax.ShapeDtypeStruct((m, d), BF16), compiler_params=_cparams(1),
    )(x2d, g.reshape(1, d))


def _headnorm(acc, g, hd, scale):
    outs = []
    for j in range(acc.shape[1] // hd):
        sl = acc[:, j * hd:(j + 1) * hd]
        outs.append(sl * lax.rsqrt(jnp.mean(sl * sl, axis=-1, keepdims=True) + EPS))
    y = outs[0] if len(outs) == 1 else jnp.concatenate(outs, axis=1)
    y = y * g
    return y * scale if scale != 1.0 else y


def _proj_body(h_ref, *refs, n_w, mode, hd, scale):
    w_refs, (g_ref, o_ref, w_scr) = refs[:n_w], refs[n_w:]
    cw = w_refs[0].shape[1]

    @pl.when(pl.program_id(1) == 0)
    def _():
        for i, w_ref in enumerate(w_refs):
            w_scr[:, i * cw:(i + 1) * cw] = w_ref[...].astype(w_scr.dtype)

    acc = jnp.dot(h_ref[...], w_scr[...], preferred_element_type=F32)
    if mode == "none":
        o_ref[...] = acc.astype(o_ref.dtype)
    elif mode == "sigmoid":
        o_ref[...] = jax.nn.sigmoid(acc).astype(o_ref.dtype)
    elif mode == "norm":
        o_ref[...] = _headnorm(acc, g_ref[...], hd, scale).astype(o_ref.dtype)
    else:
        @pl.when(pl.program_id(0) == 0)
        def _():
            o_ref[...] = _headnorm(acc, g_ref[...], hd, scale).astype(o_ref.dtype)

        @pl.when(pl.program_id(0) != 0)
        def _():
            o_ref[...] = acc.astype(o_ref.dtype)


W_COLS = 512


def _proj(h, w, cols, tm, mode, out_dtype, gain=None, hd=None, scale=1.0, name="proj"):
    m, k = h.shape
    base, stride, n_w, n_tiles = cols
    tn = n_w * W_COLS
    if gain is None:
        g = jnp.ones((1, tn), F32)
    else:
        g = jnp.tile(gain.astype(F32).reshape(1, -1), (1, tn // gain.shape[-1]))
    body = functools.partial(_proj_body, n_w=n_w, mode=mode, hd=hd, scale=scale)
    w_specs = [pl.BlockSpec((k, W_COLS), functools.partial(lambda j, i, off: (0, base + stride * j + off), off=off))
               for off in range(n_w)]
    return pl.pallas_call(
        body, name=name, grid=(n_tiles, m // tm),
        in_specs=[pl.BlockSpec((tm, k), lambda j, i: (i, 0))] + w_specs + [pl.BlockSpec((1, tn), lambda j, i: (0, 0))],
        out_specs=pl.BlockSpec((tm, tn), lambda j, i: (i, j)),
        out_shape=jax.ShapeDtypeStruct((m, n_tiles * tn), out_dtype),
        scratch_shapes=[pltpu.VMEM((k, tn), BF16)], compiler_params=_cparams(2),
    )(h, *([w] * n_w), g)


PERM_ROWS = 256


def _perm_matrix(n, dil, inverse=False):
    per = n // dil
    o = lax.broadcasted_iota(I32, (n, n), 1 if inverse else 0)
    s = lax.broadcasted_iota(I32, (n, n), 0 if inverse else 1)
    return (s == (o % per) * dil + o // per).astype(BF16)


def _qkv_body(h_ref, wq_ref, wk_ref, wv_ref, gq_ref, gk_ref, q_ref, kv_ref, tok_ref, w_scr, *, dil, scale):
    c = pl.program_id(1)
    tm = h_ref.shape[0]
    nh = HEADS_PER_GROUP_A

    @pl.when(pl.program_id(0) == 0)
    def _():
        for i, w_ref in enumerate((wq_ref, wk_ref, wv_ref)):
            @pl.when(c == i)
            def _():
                w_scr[i] = w_ref[...].astype(BF16)

    acc = jnp.dot(h_ref[...], w_scr[c], preferred_element_type=F32)

    def put_perm(dst_ref, y, col0):
        if dil == 1:
            dst_ref[0, :, col0:col0 + GROUP_W_A] = y.astype(dst_ref.dtype)
            return
        pm = _perm_matrix(PERM_ROWS, dil)
        per = PERM_ROWS // dil
        yb = y.astype(BF16)
        for t in range(tm // PERM_ROWS):
            yp = jnp.dot(pm, yb[t * PERM_ROWS:(t + 1) * PERM_ROWS], preferred_element_type=F32).astype(BF16)
            for r in range(dil):
                dst_ref[r, t * per:(t + 1) * per, col0:col0 + GROUP_W_A] = yp[r * per:(r + 1) * per]

    def put_tok(y, c0):
        for i in range(nh):
            tok_ref[pl.ds(c0 + i, tm, stride=2 * nh), :] = y[:, i * HEAD_DIM_A:(i + 1) * HEAD_DIM_A]

    @pl.when(c == 0)
    def _():
        put_perm(q_ref, _headnorm(acc, gq_ref[...], HEAD_DIM_A, scale), 0)

    @pl.when(c == 1)
    def _():
        y = _headnorm(acc, gk_ref[...], HEAD_DIM_A, 1.0)
        put_tok(y, 0)
        put_perm(kv_ref, y, 0)

    @pl.when(c == 2)
    def _():
        put_tok(acc, nh)
        put_perm(kv_ref, acc, 0)


def _proj_qkv(h, w, sec, g, g_qa, g_ka, n_batch, seq, tm, dil, act_dtype=BF16):
    n, k = h.shape
    ln = seq // dil
    per = tm // dil
    mt = seq // tm
    assert seq % tm == 0
    assert dil == 1 or (act_dtype == BF16 and tm % PERM_ROWS == 0 and (PERM_ROWS // dil) % 16 == 0)
    tile = lambda v: jnp.tile(v.astype(F32).reshape(1, -1), (1, GROUP_W_A // v.shape[-1]))
    wspec = lambda blk: pl.BlockSpec((k, W_COLS), lambda m, c: (0, blk))
    q, kv, tok = pl.pallas_call(
        functools.partial(_qkv_body, dil=dil, scale=HEAD_DIM_A ** -0.5), name=f"proj_qkv{g}", grid=(n // tm, 3),
        in_specs=[pl.BlockSpec((tm, k), lambda m, c: (m, 0)),
                  wspec(sec["q"] + g), wspec(sec["k"] + g), wspec(sec["v"] + g),
                  pl.BlockSpec((1, GROUP_W_A), lambda m, c: (0, 0)), pl.BlockSpec((1, GROUP_W_A), lambda m, c: (0, 0))],
        out_specs=[pl.BlockSpec((None, dil, per, GROUP_W_A), lambda m, c: (m // mt, 0, m % mt, 0)),
                   pl.BlockSpec((None, dil, per, GROUP_W_A), lambda m, c: (m // mt, 0, m % mt, jnp.maximum(c - 1, 0))),
                   pl.BlockSpec((tm * 2 * HEADS_PER_GROUP_A, LANES), lambda m, c: (m, 0))],
        out_shape=[jax.ShapeDtypeStruct((n_batch, dil, ln, GROUP_W_A), act_dtype),
                   jax.ShapeDtypeStruct((n_batch, dil, ln, 2 * GROUP_W_A), act_dtype),
                   jax.ShapeDtypeStruct((n * 2 * HEADS_PER_GROUP_A, LANES), F32)],
        scratch_shapes=[pltpu.VMEM((3, k, W_COLS), BF16)], compiler_params=_cparams(2),
    )(h, w, w, w, tile(g_qa), tile(g_ka))
    return q.reshape(n, GROUP_W_A), kv.reshape(n, 2 * GROUP_W_A), tok


def _bias_from_table(tab_ref, bidx, col):
    acc = jnp.zeros(bidx.shape, F32)
    for kb in range(N_BUCKETS):
        acc = jnp.where(bidx == kb, tab_ref[kb, col], acc)
    return acc


def _split_hi_lo(x):
    hi = x.astype(BF16)
    return hi, (x - hi.astype(F32)).astype(BF16)


def _attn_p_body(tab_ref, bidx_ref, q_ref, kp_ref, kc_ref, vp_ref, vc_ref, o_ref, bias_scr, *, g):
    first = (pl.program_id(0) == 0) & (pl.program_id(1) == 0)

    @pl.when(first)
    def _():
        bidx = bidx_ref[...]
        for h in range(HEADS_PER_GROUP_A):
            bias_scr[h] = _bias_from_table(tab_ref, bidx, g * HEADS_PER_GROUP_A + h)

    qb = pl.program_id(1)
    blk = q_ref.shape[0]
    q = q_ref[...]
    k = jnp.concatenate([kp_ref[...], kc_ref[...]], axis=0)
    v = jnp.concatenate([vp_ref[...], vc_ref[...]], axis=0)
    row = lax.broadcasted_iota(I32, (blk, 2 * blk), 0)
    col = lax.broadcasted_iota(I32, (blk, 2 * blk), 1)
    dist = blk + row - col
    valid = (dist >= 0) & (dist <= blk) & ((qb > 0) | (col >= blk))
    lane = lax.broadcasted_iota(I32, (blk, LANES), 1)
    lse_blk = jnp.zeros((blk, LANES), F32)
    outs = []
    for h in range(HEADS_PER_GROUP_A):
        hs = slice(h * HEAD_DIM_A, (h + 1) * HEAD_DIM_A)
        s = lax.dot_general(q[:, hs], k[:, hs], (((1,), (1,)), ((), ())), preferred_element_type=F32)
        s = jnp.where(valid, s + bias_scr[h], NEG_INF)
        m = jnp.max(s, axis=-1, keepdims=True)
        p = jnp.exp(s - m)
        l = jnp.sum(p, axis=-1, keepdims=True)
        o = jnp.dot(p.astype(BF16), v[:, hs], preferred_element_type=F32) / l
        outs.append(o)
        lse_blk = jnp.where(lane == h, m + jnp.log(l), lse_blk)
    lse_hi, lse_lo = _split_hi_lo(lse_blk)
    o_ref[...] = jnp.concatenate([jnp.concatenate(outs, axis=1).astype(BF16), lse_hi, lse_lo], axis=1)


O_EXT_W = GROUP_W_A + 2 * LANES


def _attn_prompt(q, kv, rel_bias, g, n_batch, seq):
    win, dil = DIL_GROUPS[g]
    blk = win // dil
    nb = seq // dil // blk
    n = n_batch * seq
    a = jnp.arange(blk)[:, None]
    c = jnp.arange(2 * blk)[None, :]
    bidx = _rel_bucket(jnp.maximum(blk + a - c, 0) * dil)

    def cur(s, j):
        return s * nb + j

    def prev(s, j):
        return s * nb + jnp.maximum(j - 1, 0)

    return pl.pallas_call(
        functools.partial(_attn_p_body, g=g), name=f"attn_p{g}", grid=(n_batch * dil, nb),
        in_specs=[
            pl.BlockSpec(memory_space=pltpu.SMEM),
            pl.BlockSpec((blk, 2 * blk), lambda s, j: (0, 0)),
            pl.BlockSpec((blk, GROUP_W_A), lambda s, j: (cur(s, j), 0)),
            pl.BlockSpec((blk, GROUP_W_A), lambda s, j: (prev(s, j), 0)),
            pl.BlockSpec((blk, GROUP_W_A), lambda s, j: (cur(s, j), 0)),
            pl.BlockSpec((blk, GROUP_W_A), lambda s, j: (prev(s, j), 1)),
            pl.BlockSpec((blk, GROUP_W_A), lambda s, j: (cur(s, j), 1)),
        ],
        out_specs=pl.BlockSpec((blk, O_EXT_W), lambda s, j: (cur(s, j), 0)),
        out_shape=jax.ShapeDtypeStruct((n, O_EXT_W), BF16),
        scratch_shapes=[pltpu.VMEM((HEADS_PER_GROUP_A, blk, 2 * blk), F32)],
        compiler_params=_cparams(2),
    )(rel_bias, bidx, q, kv, kv, kv, kv)


def _attn_s_body(tab_ref, bidx_ref, q_ref, kvn_ref, tokn_ref, cache_hbm, o_ref, newc_hbm,
                 raw, kv_scr, bias_scr, sem_in, sem_out, *, g, dil, lb, n_req):
    t_new = q_ref.shape[0]
    nh = HEADS_PER_GROUP_A
    n_rows = nh * t_new
    n_keys = kv_scr.shape[0]
    n_col = 2 * nh
    lb8 = lb * n_col
    n = pl.program_id(0)
    slot = n % 2

    def fetch(req, s):
        return pltpu.make_async_copy(cache_hbm.at[req], raw.at[s, pl.ds(0, lb8)], sem_in.at[s])

    def flush(req, s):
        return pltpu.make_async_copy(raw.at[s, pl.ds(t_new * n_col, lb8)], newc_hbm.at[req], sem_out.at[s])

    @pl.when(n == 0)
    def _():
        kv_scr[lb:, :] = jnp.zeros((n_keys - lb, kv_scr.shape[1]), BF16)
        bidx = bidx_ref[...]
        for h in range(nh):
            bias_scr[h * t_new:(h + 1) * t_new, :] = _bias_from_table(tab_ref, bidx, g * nh + h)
        fetch(0, 0).start()

    @pl.when(n >= 1)
    def _():
        flush(n - 1, 1 - slot).wait()

    @pl.when(n + 1 < n_req)
    def _():
        fetch(n + 1, 1 - slot).start()

    fetch(n, slot).wait()
    raw[slot, lb8:lb8 + t_new * n_col, :] = tokn_ref[...]
    flush(n, slot).start()
    new = kvn_ref[...]

    rows_per = min(lb, 256)
    for c in range(n_col):
        for r0 in range(0, lb, rows_per):
            kv_scr[r0:r0 + rows_per, c * LANES:(c + 1) * LANES] = raw[
                slot, pl.ds(r0 * n_col + c, rows_per, stride=n_col), :].astype(BF16)
    kv_scr[lb:lb + 2 * t_new, :] = jnp.concatenate([new, jnp.zeros_like(new)], axis=0).astype(BF16)

    q = q_ref[...]
    qt = jnp.concatenate([q] * nh, axis=0)
    rr = lax.broadcasted_iota(I32, (n_rows, GROUP_W_A), 0)
    cc = lax.broadcasted_iota(I32, (n_rows, GROUP_W_A), 1)
    qbd = jnp.where(cc // HEAD_DIM_A == rr // t_new, qt, 0.0).astype(BF16)

    s = lax.dot_general(qbd, kv_scr[:, :GROUP_W_A], (((1,), (1,)), ((), ())), preferred_element_type=F32)
    row = lax.broadcasted_iota(I32, (n_rows, n_keys), 0)
    col = lax.broadcasted_iota(I32, (n_rows, n_keys), 1)
    delta = lb + (row & (t_new - 1)) - col
    valid = (delta >= 0) & (delta <= lb) & ((delta & (dil - 1)) == 0)
    s = jnp.where(valid, s + bias_scr[...], NEG_INF)
    m = jnp.max(s, axis=-1, keepdims=True)
    p = jnp.exp(s - m)
    l = jnp.sum(p, axis=-1, keepdims=True)
    o = jnp.dot(p.astype(BF16), kv_scr[:, GROUP_W_A:], preferred_element_type=F32) / l
    lse = m + jnp.log(l)
    lane = lax.broadcasted_iota(I32, (t_new, LANES), 1)
    lse_blk = jnp.zeros((t_new, LANES), F32)
    outs = []
    for h in range(nh):
        outs.append(o[h * t_new:(h + 1) * t_new, h * HEAD_DIM_A:(h + 1) * HEAD_DIM_A])
        lse_blk = jnp.where(lane == h, lse[h * t_new:(h + 1) * t_new, :], lse_blk)
    o_ref[...] = jnp.concatenate(outs + [lse_blk, jnp.zeros_like(lse_blk)], axis=1)

    @pl.when(n == n_req - 1)
    def _():
        flush(n, slot).wait()


def _attn_sample(q, kv_new, tok_new, cache, rel_bias, g, t_new):
    win, dil = DIL_GROUPS[g]
    n_col = 2 * HEADS_PER_GROUP_A
    n_req, lb = cache.shape[0], cache.shape[1] // n_col
    assert t_new & (t_new - 1) == 0 and dil & (dil - 1) == 0 and lb == win
    n_keys = lb + LANES
    t = jnp.arange(t_new)[:, None]
    c = jnp.arange(n_keys)[None, :]
    bidx = _rel_bucket(jnp.clip(lb + t - c, 0, lb))
    return pl.pallas_call(
        functools.partial(_attn_s_body, g=g, dil=dil, lb=lb, n_req=n_req), name=f"attn_s{g}", grid=(n_req,),
        in_specs=[
            pl.BlockSpec(memory_space=pltpu.SMEM),
            pl.BlockSpec((t_new, n_keys), lambda n: (0, 0)),
            pl.BlockSpec((t_new, GROUP_W_A), lambda n: (n, 0)),
            pl.BlockSpec((t_new, 2 * GROUP_W_A), lambda n: (n, 0)),
            pl.BlockSpec((t_new * n_col, LANES), lambda n: (n, 0)),
            pl.BlockSpec(memory_space=pl.ANY),
        ],
        out_specs=[pl.BlockSpec((t_new, O_EXT_W), lambda n: (n, 0)), pl.BlockSpec(memory_space=pl.ANY)],
        out_shape=[jax.ShapeDtypeStruct((n_req * t_new, O_EXT_W), F32), jax.ShapeDtypeStruct(cache.shape, F32)],
        scratch_shapes=[pltpu.VMEM((2, (lb + t_new) * n_col, LANES), F32),
                        pltpu.VMEM((n_keys, 2 * GROUP_W_A), BF16),
                        pltpu.VMEM((HEADS_PER_GROUP_A * t_new, n_keys), F32),
                        pltpu.SemaphoreType.DMA((2,)), pltpu.SemaphoreType.DMA((2,))],
        compiler_params=_cparams(1),
    )(rel_bias, bidx, q, kv_new, tok_new, cache)


def _gmlp_body(u_ref, v_ref, w_ref, b_ref, o_ref, *, period):
    c = w_ref.shape[1]
    nch = u_ref.shape[0] // c
    gd = u_ref.shape[1] // N_GROUPS_B
    i = lax.broadcasted_iota(I32, (c, c), 0)
    j = lax.broadcasted_iota(I32, (c, c), 1)
    mask = (j <= i) & ((i // period) == (j // period))
    b = b_ref[...]
    for g in range(N_GROUPS_B):
        gs = slice(g * gd, (g + 1) * gd)
        wg = jnp.where(mask, w_ref[g], 0.0).astype(BF16)
        vg = [v_ref[ch * c:(ch + 1) * c, gs].astype(BF16) for ch in range(nch)]
        vg = vg[0] if nch == 1 else jnp.concatenate(vg, axis=1)
        sg = jnp.dot(wg, vg, preferred_element_type=F32) + b[:, g:g + 1]
        for ch in range(nch):
            u = u_ref[ch * c:(ch + 1) * c, gs].astype(F32)
            o_ref[ch * c:(ch + 1) * c, gs] = (u * sg[:, ch * gd:(ch + 1) * gd]).astype(o_ref.dtype)


def _gmlp(u, v, w, b, tm, period):
    m, wb = u.shape
    c = w.shape[1]
    return pl.pallas_call(
        functools.partial(_gmlp_body, period=period), name="gmlp", grid=(m // tm,),
        in_specs=[pl.BlockSpec((tm, wb), lambda i: (i, 0)), pl.BlockSpec((tm, wb), lambda i: (i, 0)),
                  pl.BlockSpec((N_GROUPS_B, c, c), lambda i: (0, 0, 0)),
                  pl.BlockSpec((c, N_GROUPS_B), lambda i: (0, 0))],
        out_specs=pl.BlockSpec((tm, wb), lambda i: (i, 0)),
        out_shape=jax.ShapeDtypeStruct((m, wb), BF16), compiler_params=_cparams(1),
    )(u, v, w, b)


def _gmlp_new_body(u_ref, v_ref, w_ref, b_ref, o_ref):
    t = w_ref.shape[0]
    n = u_ref.shape[0] // t
    width = u_ref.shape[1]
    v = v_ref[...].astype(F32).reshape(n, t, width)
    sg = jnp.broadcast_to(b_ref[...][None], (n, t, width))
    p_row = lax.broadcasted_iota(I32, (t, width), 0)
    for q in range(t):
        sg = sg + jnp.where(p_row >= q, w_ref[q], 0.0)[None] * v[:, q:q + 1, :]
    o_ref[...] = (u_ref[...].astype(F32).reshape(n, t, width) * sg).reshape(n * t, width).astype(o_ref.dtype)


def _gmlp_new(u, v, w_s, b_s, t):
    m, wb = u.shape
    gd = wb // N_GROUPS_B
    wq = jnp.repeat(jnp.transpose(w_s[:, :t, :t], (2, 1, 0)), gd, axis=-1)
    bq = jnp.repeat(b_s[:, :t].T, gd, axis=-1)
    whole = lambda a: pl.BlockSpec(a.shape, lambda i: (0,) * a.ndim)
    return pl.pallas_call(
        _gmlp_new_body, name="gmlp_new", grid=(1,),
        in_specs=[whole(u), whole(v), whole(wq), whole(bq)], out_specs=whole(u),
        out_shape=jax.ShapeDtypeStruct((m, wb), BF16), compiler_params=_cparams(1),
    )(u, v, wq, bq)


def _memattn_body(q_ref, kv_ref, o_ref):
    wm = N_HEADS_M * HEAD_DIM_M
    q = q_ref[...].astype(BF16)
    outs = []
    for h in range(N_HEADS_M):
        hs = slice(h * HEAD_DIM_M, (h + 1) * HEAD_DIM_M)
        k = kv_ref[:, hs].astype(BF16)
        v = kv_ref[:, wm + h * HEAD_DIM_M:wm + (h + 1) * HEAD_DIM_M].astype(BF16)
        s = lax.dot_general(q[:, hs], k, (((1,), (1,)), ((), ())), preferred_element_type=F32)
        m = jnp.max(s, axis=-1, keepdims=True)
        p = jnp.exp(s - m)
        l = jnp.sum(p, axis=-1, keepdims=True)
        outs.append(jnp.dot(p.astype(BF16), v, preferred_element_type=F32) / l)
    o_ref[...] = jnp.concatenate(outs, axis=1).astype(o_ref.dtype)


def _memattn(q, kv, tq, out_dtype):
    m, wm = q.shape
    n = kv.shape[0]
    per = m // n // tq
    kv_spec = pl.BlockSpec((None,) + kv.shape[1:], lambda i: (i // per,) + (0,) * (kv.ndim - 1))
    return pl.pallas_call(
        _memattn_body, name="memattn", grid=(m // tq,),
        in_specs=[pl.BlockSpec((tq, wm), lambda i: (i, 0)), kv_spec],
        out_specs=pl.BlockSpec((tq, wm), lambda i: (i, 0)),
        out_shape=jax.ShapeDtypeStruct((m, wm), out_dtype), compiler_params=_cparams(1),
    )(q, kv)


def _mix_body(gt_ref, o0_ref, o1_ref, o2_ref, ob_ref, om_ref, wpa_ref, wpb_ref, wpm_ref, z_ref, *, dils):
    d = z_ref.shape[1]
    tm = z_ref.shape[0]
    os_, ls = [], []
    for o_ref, dil in zip((o0_ref, o1_ref, o2_ref), dils):
        if dil == 1:
            x = o_ref[...].astype(F32)
        else:
            rows = jnp.concatenate([o_ref[r] for r in range(dil)], axis=0)
            x = jnp.dot(_perm_matrix(tm, dil, inverse=True), rows, preferred_element_type=F32)
        os_.append(x[:, :GROUP_W_A])
        ls.append(x[:, GROUP_W_A:GROUP_W_A + LANES] + x[:, GROUP_W_A + LANES:])
    l0, l1, l2 = ls
    mx = jnp.maximum(jnp.maximum(l0, l1), l2)
    e0, e1, e2 = jnp.exp(l0 - mx), jnp.exp(l1 - mx), jnp.exp(l2 - mx)
    den = e0 + e1 + e2
    w0, w1, w2 = e0 / den, e1 / den, e2 / den
    cols = []
    for h in range(HEADS_PER_GROUP_A):
        hs = slice(h * HEAD_DIM_A, (h + 1) * HEAD_DIM_A)
        cols.append(w0[:, h:h + 1] * os_[0][:, hs] + w1[:, h:h + 1] * os_[1][:, hs] + w2[:, h:h + 1] * os_[2][:, hs])
    oa = jnp.concatenate(cols, axis=1).astype(BF16)
    pa = jnp.dot(oa, wpa_ref[...], preferred_element_type=F32)
    pb = jnp.dot(ob_ref[...].astype(BF16), wpb_ref[...], preferred_element_type=F32)
    pm = jnp.dot(om_ref[...].astype(BF16), wpm_ref[...], preferred_element_type=F32)
    z = (gt_ref[:, 0:d].astype(F32) * pa + gt_ref[:, d:2 * d].astype(F32) * pb
         + gt_ref[:, 2 * d:3 * d].astype(F32) * pm)
    z_ref[...] = z.astype(z_ref.dtype)


def _mix(gates, o_list, dils, seq, ob, om, wpa, wpb, wpm, tm):
    m = gates.shape[0]
    d = wpa.shape[1]
    mt = seq // tm if any(dl > 1 for dl in dils) else 1

    def rows(width):
        return pl.BlockSpec((tm, width), lambda i: (i, 0))

    def whole(a):
        return pl.BlockSpec(a.shape, lambda i: (0, 0))

    o_specs, o_args = [], []
    for o, dil in zip(o_list, dils):
        if dil == 1:
            o_specs.append(rows(O_EXT_W))
            o_args.append(o)
        else:
            assert seq % tm == 0 and (tm // dil) % 16 == 0
            o_specs.append(pl.BlockSpec((None, dil, tm // dil, O_EXT_W), lambda i: (i // mt, 0, i % mt, 0)))
            o_args.append(o.reshape(m // seq, dil, seq // dil, O_EXT_W))
    return pl.pallas_call(
        functools.partial(_mix_body, dils=tuple(dils)), name="mix", grid=(m // tm,),
        in_specs=[rows(gates.shape[1])] + o_specs
                 + [rows(ob.shape[1]), rows(om.shape[1]), whole(wpa), whole(wpb), whole(wpm)],
        out_specs=rows(d),
        out_shape=jax.ShapeDtypeStruct((m, d), BF16), compiler_params=_cparams(1),
    )(gates, *o_args, ob, om, wpa, wpb, wpm)


def _route(logits):
    lane = lax.broadcasted_iota(I32, logits.shape, 1)
    lane_f = lane.astype(F32)
    is_g = lane < N_EXPERT_GROUPS
    gmax = jnp.max(jnp.where(is_g, logits, -jnp.inf), axis=1, keepdims=True)
    gsel = jnp.min(jnp.where(is_g & (logits == gmax), lane_f, float(LANES)), axis=1, keepdims=True).astype(I32)
    gden = jnp.sum(jnp.where(is_g, jnp.exp(logits - gmax), 0.0), axis=1, keepdims=True)
    pg = 1.0 / gden
    e_lane = lane - N_EXPERT_GROUPS
    in_grp = (e_lane >= 0) & (e_lane < N_EXPERTS) & ((e_lane // EXPERTS_PER_GROUP) == gsel)
    m1 = jnp.max(jnp.where(in_grp, logits, -jnp.inf), axis=1, keepdims=True)
    i1 = jnp.min(jnp.where(in_grp & (logits == m1), lane_f, float(LANES)), axis=1, keepdims=True).astype(I32)
    rest = in_grp & (lane != i1)
    m2 = jnp.max(jnp.where(rest, logits, -jnp.inf), axis=1, keepdims=True)
    i2 = jnp.min(jnp.where(rest & (logits == m2), lane_f, float(LANES)), axis=1, keepdims=True).astype(I32)
    e2 = jnp.exp(m2 - m1)
    w1 = pg / (1.0 + e2)
    w2 = pg * e2 / (1.0 + e2)
    eid = jnp.where(lane == 0, i1 - N_EXPERT_GROUPS, jnp.where(lane == 1, i2 - N_EXPERT_GROUPS, 0))
    wts = jnp.where(lane == 0, w1, jnp.where(lane == 1, w2, 0.0))
    return eid, wts


def _resid_body(x_ref, z_ref, wo_ref, gf_ref, wr_ref, br_ref, xmid_ref, hpk_ref, eid_ref, wts_ref):
    xm = x_ref[...] + jnp.dot(z_ref[...], wo_ref[...], preferred_element_type=F32)
    xmid_ref[...] = xm
    hf = _rms(xm, gf_ref[...])
    wr = wr_ref[...]
    wr_hi = wr.astype(BF16)
    wr_lo = (wr - wr_hi.astype(F32)).astype(BF16)
    hf_hi = hf.astype(BF16)
    hf_lo = (hf - hf_hi.astype(F32)).astype(BF16)
    logits = (jnp.dot(hf_hi, wr_hi, preferred_element_type=F32) + jnp.dot(hf_hi, wr_lo, preferred_element_type=F32)
              + jnp.dot(hf_lo, wr_hi, preferred_element_type=F32) + br_ref[...])
    eid, wts = _route(logits)
    eid_ref[...] = eid
    wts_ref[...] = wts
    _store_token_tiles(hpk_ref, hf)


def _resid(x2d, z, wo, g_ffn, w_r, b_r, tm):
    m, d = x2d.shape

    def rows(width):
        return pl.BlockSpec((tm, width), lambda i: (i, 0))

    def whole(a):
        return pl.BlockSpec(a.shape, lambda i: (0, 0))

    gf = g_ffn.reshape(1, d)
    nt = d // 2 // LANES
    return pl.pallas_call(
        _resid_body, name="resid", grid=(m // tm,),
        in_specs=[rows(d), rows(d), whole(wo), whole(gf), whole(w_r), whole(b_r)],
        out_specs=[rows(d), pl.BlockSpec((tm * nt, LANES), lambda i: (i, 0)), rows(LANES), rows(LANES)],
        out_shape=[jax.ShapeDtypeStruct((m, d), F32), jax.ShapeDtypeStruct((m * nt, LANES), U32),
                   jax.ShapeDtypeStruct((m, LANES), I32), jax.ShapeDtypeStruct((m, LANES), F32)],
        compiler_params=_cparams(1),
    )(x2d, z, wo, gf, w_r, b_r)


TOKEN_TILE = 8
GATHER_UNROLL = 8


def _gather_rows(idx_ref, src_hbm, dst, sem, n, priority=0):
    def body(i, carry):
        src = pl.multiple_of(idx_ref[0, 0, i], TOKEN_TILE)
        dst_row = pl.multiple_of(i * TOKEN_TILE, TOKEN_TILE)
        pltpu.make_async_copy(src_hbm.at[pl.ds(src, TOKEN_TILE)], dst.at[pl.ds(dst_row, TOKEN_TILE)],
                              sem).start(priority=priority)
        return carry
    lax.fori_loop(0, n, body, 0, unroll=GATHER_UNROLL)


def _dispatch_body(idx0_ref, idx1_ref, idx2_ref, h_hbm, xs_hbm, buf, sem, osem, *, n):
    j = pl.program_id(0)
    nbuf = buf.shape[0]
    tiles = buf.shape[1]
    rows = tiles // TOKEN_TILE

    def out_copy(blk):
        s = blk % nbuf
        return pltpu.make_async_copy(buf.at[s], xs_hbm.at[pl.ds(pl.multiple_of(blk * tiles, tiles), tiles)],
                                     osem.at[s])

    @pl.when(j == 0)
    def _():
        _gather_rows(idx0_ref, h_hbm, buf.at[0], sem.at[0], rows)
        _gather_rows(idx1_ref, h_hbm, buf.at[1], sem.at[1], rows)

    @pl.when(j >= 2)
    def _():
        out_copy(j - 2).wait()

    @pl.when(j + 2 < n)
    def _():
        s2 = (j + 2) % nbuf
        _gather_rows(idx2_ref, h_hbm, buf.at[s2], sem.at[s2], rows)

    s = j % nbuf
    pltpu.make_async_copy(h_hbm.at[pl.ds(0, tiles)], buf.at[s], sem.at[s]).wait()
    out_copy(j).start()

    @pl.when(j == n - 1)
    def _():
        out_copy(j - 1).wait()
        out_copy(j).wait()


def _dispatch(hpk, src_tok, n_blocks):
    rows = MOE_ROWS
    assert n_blocks >= 3
    idx3 = (src_tok * TOKEN_TILE).reshape(n_blocks, 1, rows)

    def idx_spec(ahead):
        return pl.BlockSpec((1, 1, rows), lambda j: (jnp.minimum(j + ahead, n_blocks - 1), 0, 0),
                            memory_space=pltpu.SMEM)

    hbm = pl.BlockSpec(memory_space=pl.ANY)
    return pl.pallas_call(
        functools.partial(_dispatch_body, n=n_blocks), name="dispatch", grid=(n_blocks,),
        in_specs=[idx_spec(0), idx_spec(1), idx_spec(2), hbm], out_specs=hbm,
        out_shape=jax.ShapeDtypeStruct((n_blocks * rows * TOKEN_TILE, LANES), U32),
        scratch_shapes=[pltpu.VMEM((4, rows * TOKEN_TILE, LANES), U32), pltpu.SemaphoreType.DMA((4,)),
                        pltpu.SemaphoreType.DMA((4,))],
        compiler_params=_cparams(1),
    )(idx3, idx3, idx3, hpk)


def _ffn_body(blk_e_ref, nused_ref, eord_ref, enext_ref, x_ref, wg_hbm, wu_hbm, wd_hbm, y_ref,
              wg_f, wu_f, wd_f, wsem, wg_s, wu_s, wd_s):
    j = pl.program_id(0)
    nused = nused_ref[0]
    rows = x_ref.shape[0] // TOKEN_TILE
    e = blk_e_ref[j]
    new_expert = (j == 0) | (e != blk_e_ref[jnp.maximum(j - 1, 0)])
    ws = eord_ref[j] % 2

    def fetch_w(expert, s):
        return [pltpu.make_async_copy(src.at[expert], dst.at[s], wsem.at[s, i])
                for i, (src, dst) in enumerate(((wg_hbm, wg_f), (wu_hbm, wu_f), (wd_hbm, wd_f)))]

    @pl.when(j == 0)
    def _():
        for c in fetch_w(e, ws):
            c.start()

    @pl.when(new_expert)
    def _():
        nxt = enext_ref[j]

        @pl.when(nxt >= 0)
        def _():
            for c in fetch_w(nxt, 1 - ws):
                c.start()

        for c in fetch_w(e, ws):
            c.wait()
        wg_s[...] = wg_f[ws].astype(BF16)
        wu_s[...] = wu_f[ws].astype(BF16)
        wd_s[...] = wd_f[ws].astype(BF16)

    @pl.when(j < nused)
    def _():
        x = _load_token_tiles(x_ref, None, 0, rows, TOKEN_TILE).astype(BF16)
        a = jnp.dot(x, wg_s[...], preferred_element_type=F32)
        b = jnp.dot(x, wu_s[...], preferred_element_type=F32)
        hm = (a * jax.nn.sigmoid(a) * b).astype(BF16)
        _store_token_tiles(y_ref, jnp.dot(hm, wd_s[...], preferred_element_type=F32))

    @pl.when(j >= nused)
    def _():
        y_ref[...] = jnp.zeros(y_ref.shape, y_ref.dtype)


def _ffn(xs, blk_e, nused, eord, enext, wg, wu, wd):
    n_blocks = blk_e.shape[0]
    tiles = MOE_ROWS * TOKEN_TILE
    d, de = wg.shape[1], wg.shape[2]
    assert d // 2 == TOKEN_TILE * LANES
    hbm = pl.BlockSpec(memory_space=pl.ANY)
    grid_spec = pltpu.PrefetchScalarGridSpec(
        num_scalar_prefetch=4, grid=(n_blocks,),
        in_specs=[pl.BlockSpec((tiles, LANES), lambda j, be, nu, *_: (jnp.minimum(j, nu[0] - 1), 0)),
                  hbm, hbm, hbm],
        out_specs=pl.BlockSpec((tiles, LANES), lambda j, *_: (j, 0)),
        scratch_shapes=[pltpu.VMEM((2, d, de), wg.dtype), pltpu.VMEM((2, d, de), wu.dtype),
                        pltpu.VMEM((2, de, d), wd.dtype), pltpu.SemaphoreType.DMA((2, 3)),
                        pltpu.VMEM((d, de), BF16), pltpu.VMEM((d, de), BF16), pltpu.VMEM((de, d), BF16)],
    )
    return pl.pallas_call(
        _ffn_body, name="ffn", grid_spec=grid_spec,
        out_shape=jax.ShapeDtypeStruct((n_blocks * tiles, LANES), U32), compiler_params=_cparams(1),
    )(blk_e, nused, eord, enext, xs, wg, wu, wd)


def _combine_body(idx_cur_ref, idx_nxt_ref, x_ref, w_ref, y_hbm, o_ref, buf, sem, *, n):
    i = pl.program_id(0)
    slot = i % 2
    rows = buf.shape[1] // TOKEN_TILE
    tm = rows // TOP_K

    @pl.when(i == 0)
    def _():
        _gather_rows(idx_cur_ref, y_hbm, buf.at[0], sem.at[0], rows)

    if n > 1:
        @pl.when(i + 1 < n)
        def _():
            _gather_rows(idx_nxt_ref, y_hbm, buf.at[1 - slot], sem.at[1 - slot], rows)

    pltpu.make_async_copy(y_hbm.at[pl.ds(0, rows * TOKEN_TILE)], buf.at[slot], sem.at[slot]).wait()
    w = w_ref[...]
    y0 = _load_token_tiles(buf, slot, 0, tm, TOKEN_TILE)
    y1 = _load_token_tiles(buf, slot, tm, tm, TOKEN_TILE)
    o_ref[...] = x_ref[...] + (w[:, 0:1] * y0 + w[:, 1:2] * y1)


def _combine(xmid, wts, dest, ypk, tm):
    m, d = xmid.shape
    nt = m // tm
    assert d // 2 == TOKEN_TILE * LANES
    idx3 = (dest * TOKEN_TILE).reshape(nt, tm, TOP_K).transpose(0, 2, 1).reshape(nt, 1, TOP_K * tm)
    return pl.pallas_call(
        functools.partial(_combine_body, n=nt), name="combine", grid=(nt,),
        in_specs=[
            pl.BlockSpec((1, 1, TOP_K * tm), lambda i: (i, 0, 0), memory_space=pltpu.SMEM),
            pl.BlockSpec((1, 1, TOP_K * tm), lambda i: (jnp.minimum(i + 1, nt - 1), 0, 0), memory_space=pltpu.SMEM),
            pl.BlockSpec((tm, d), lambda i: (i, 0)),
            pl.BlockSpec((tm, LANES), lambda i: (i, 0)),
            pl.BlockSpec(memory_space=pl.ANY),
        ],
        out_specs=pl.BlockSpec((tm, d), lambda i: (i, 0)),
        out_shape=jax.ShapeDtypeStruct((m, d), F32),
        scratch_shapes=[pltpu.VMEM((2, TOP_K * tm * TOKEN_TILE, LANES), U32), pltpu.SemaphoreType.DMA((2,))],
        compiler_params=_cparams(1),
    )(idx3, idx3, xmid, wts, ypk)


def _moe_plan(eid2):
    n_slot = eid2.shape[0] * TOP_K
    eid = eid2.reshape(n_slot)
    onehot = (eid[:, None] == jnp.arange(N_EXPERTS, dtype=I32)[None, :]).astype(I32)
    csum = jnp.cumsum(onehot, axis=0)
    rank = jnp.take_along_axis(csum, eid[:, None], axis=1)[:, 0] - 1
    counts = csum[-1]
    padded = (counts + MOE_ROWS - 1) // MOE_ROWS * MOE_ROWS
    pend = jnp.cumsum(padded)
    pstart = pend - padded
    dest = pstart[eid] + rank
    n_blocks = -(-n_slot // MOE_ROWS) + N_EXPERTS
    nused = (pend[-1] // MOE_ROWS).astype(I32)
    blk = jnp.minimum(jnp.arange(n_blocks, dtype=I32), nused - 1)
    blk_e = jnp.minimum(jnp.searchsorted(pend, blk * MOE_ROWS, side="right"), N_EXPERTS - 1).astype(I32)
    src_tok = jnp.zeros((n_blocks * MOE_ROWS,), I32).at[dest].set(jnp.arange(n_slot, dtype=I32) // TOP_K)
    has_rows = counts > 0
    ord_e = jnp.cumsum(has_rows.astype(I32)) - 1
    ids = jnp.arange(N_EXPERTS, dtype=I32)
    later = has_rows[None, :] & (ids[None, :] > ids[:, None])
    next_e = jnp.where(later.any(axis=1), jnp.argmax(later, axis=1), -1).astype(I32)
    return dest.astype(I32), src_tok, blk_e, nused.reshape(1), ord_e[blk_e].astype(I32), next_e[blk_e]


def _mixer(x2d, tm, p, attn_fn, gmlp_fn, mem_fn):
    h = _prep(x2d, p["g_mix"], tm)
    qdt = p["q_dtype"]
    wi, sec = p["w_in"], p["sec"]
    o_list, dils, seq = attn_fn(h)
    ub = _proj(h, wi, (sec["u"], 1, 1, sec["vb"] - sec["u"]), tm, "none", BF16, name="proj_ub")
    n_vb = sec["qm"] - sec["vb"]
    vb = _proj(h, wi, (sec["vb"], n_vb, n_vb, 1), tm, "norm", p["vb_dtype"], p["g_vb"], n_vb * W_COLS,
               name="proj_vb")
    qm = _proj(h, wi, (sec["qm"], 1, 1, sec["gt"] - sec["qm"]), tm, "norm", qdt, p["g_qm"], HEAD_DIM_M,
               HEAD_DIM_M ** -0.5, name="proj_qm")
    gates = _proj(h, wi, (sec["gt"], 2, 2, (sec["end"] - sec["gt"]) // 2), tm, "sigmoid", BF16, name="proj_gates")
    ob = gmlp_fn(ub, vb)
    om = mem_fn(qm)
    z = _mix(gates, o_list, dils, seq, ob, om, p["w_pa"], p["w_pb"], p["w_pm"], min(tm, 256))
    xmid, hpk, eid, wts = _resid(x2d, z, p["w_o"], p["g_ffn"], p["w_r"], p["b_r"], min(tm, 256))
    return vb, xmid, hpk, eid, wts


def kernel(x_prompt, x_sample, mem_prompt, cache_a0_kv, cache_a1_kv, cache_a2_kv, cache_mem_kv, rel_bias, g_mix, w_in, g_qa, g_ka, w_pa, g_vb, w_s, b_s, w_pb, g_mem, w_mk, w_mv, g_qm, g_km, w_pm, w_o, g_ffn, w_rg, b_rg, w_re, b_re, w_gate, w_up, w_down):
    n_b, seq, d = x_prompt.shape
    n_s, t_s, _ = x_sample.shape
    depth = w_in.shape[0]
    assert depth == 1
    l = 0
    caches = (cache_a0_kv, cache_a1_kv, cache_a2_kv)
    n_g = len(DIL_GROUPS)
    wb = w_pb.shape[1]
    wm = N_HEADS_M * HEAD_DIM_M

    offs = [0, WIDTH_A, 2 * WIDTH_A, 3 * WIDTH_A, 3 * WIDTH_A + wb, 3 * WIDTH_A + 2 * wb,
            3 * WIDTH_A + 2 * wb + wm, w_in.shape[2]]
    assert all(o % W_COLS == 0 for o in offs) and (offs[7] - offs[6]) % (2 * W_COLS) == 0
    sec = dict(zip(("q", "k", "v", "u", "vb", "qm", "gt", "end"), (o // W_COLS for o in offs)))
    p = {
        "g_mix": g_mix[l], "g_qa": g_qa[l], "g_ka": g_ka[l], "g_vb": g_vb[l], "g_qm": g_qm[l], "g_ffn": g_ffn[l],
        "w_in": w_in[l], "sec": sec,
        "w_pa": w_pa[l].astype(BF16), "w_pb": w_pb[l].astype(BF16), "w_pm": w_pm[l].astype(BF16),
        "w_o": w_o[l].astype(BF16),
    }
    n_r = N_EXPERT_GROUPS + N_EXPERTS
    p["w_r"] = jnp.concatenate([w_rg[l], w_re[l], jnp.zeros((d, LANES - n_r), F32)], axis=1)
    p["b_r"] = jnp.concatenate([b_rg[l], b_re[l], jnp.zeros((LANES - n_r,), F32)]).reshape(1, LANES)

    n_mem = mem_prompt.shape[1]
    h_mem = _prep(mem_prompt.reshape(n_b * n_mem, d), g_mem[l], 256)
    w_mkv = jnp.concatenate([w_mk[l], w_mv[l]], axis=1)
    n_mw = wm // W_COLS
    mkv = _proj(h_mem, w_mkv, (0, n_mw, n_mw, 2), 256, "norm_first", F32, g_km[l], HEAD_DIM_M, name="proj_mkv")
    new_mem_p = mkv.reshape(1, n_b, n_mem, 2, N_HEADS_M, HEAD_DIM_M)

    ps = dict(p, q_dtype=F32, vb_dtype=F32)
    m_s = n_s * t_s
    n_col = 2 * HEADS_PER_GROUP_A
    new_s = []

    def attn_sample(h):
        o_list = []
        for g in range(n_g):
            q, kv, tok = _proj_qkv(h, w_in[l], sec, g, g_qa[l], g_ka[l], 1, m_s, m_s, 1, act_dtype=F32)
            cache3 = caches[g][l].reshape(n_s, caches[g].shape[2] * n_col, LANES)
            o, newc = _attn_sample(q, kv, tok, cache3, rel_bias, g, t_s)
            o_list.append(o)
            new_s.append(newc.reshape(caches[g][l:l + 1].shape))
        return o_list, [1] * n_g, t_s

    vb_s, xmid_s, hpk_s, eid_s, wts_s = _mixer(
        x_sample.reshape(m_s, d), m_s, ps, attn_sample,
        lambda u, v: _gmlp_new(u, v, w_s[l], b_s[l], t_s),
        lambda qm: _memattn(qm, cache_mem_kv[l].reshape(n_s, n_mem, 2 * wm), t_s, F32))
    new_vb_s = vb_s.reshape(1, n_s, t_s, wb)

    pp = dict(p, q_dtype=BF16, vb_dtype=BF16)
    new_p = []

    def attn_prompt(h):
        o_list = []
        for g, (win, dil) in enumerate(DIL_GROUPS):
            q, kv, tok = _proj_qkv(h, w_in[l], sec, g, g_qa[l], g_ka[l], n_b, seq, 512, dil)
            o_list.append(_attn_prompt(q, kv, rel_bias, g, n_b, seq))
            keep = min(win, seq)
            kv5 = tok.reshape(n_b, seq, 2, HEADS_PER_GROUP_A, HEAD_DIM_A)
            new_p.append((kv5 if keep == seq else kv5[:, seq - keep:])[None])
        return o_list, [dl for _, dl in DIL_GROUPS], seq

    _, xmid_p, hpk_p, eid_p, wts_p = _mixer(
        x_prompt.reshape(n_b * seq, d), 512, pp, attn_prompt,
        lambda u, v: _gmlp(u, v, w_s[l], b_s[l].T, 512, CHUNK),
        lambda qm: _memattn(qm, mkv.reshape(n_b, n_mem, 2 * wm), 256, BF16))

    n_p = n_b * seq
    hpk = jnp.concatenate([hpk_p, hpk_s], axis=0)
    eid2 = jnp.concatenate([eid_p[:, :TOP_K], eid_s[:, :TOP_K]], axis=0)
    dest, src_tok, blk_e, nused, eord, enext = _moe_plan(eid2)
    xs = _dispatch(hpk, src_tok, blk_e.shape[0])
    ypk = _ffn(xs, blk_e, nused, eord, enext, w_gate[l], w_up[l], w_down[l])
    dest2 = dest.reshape(-1, TOP_K)
    y_p = _combine(xmid_p, wts_p, dest2[:n_p], ypk, 256)
    y_s = _combine(xmid_s, wts_s, dest2[n_p:], ypk, m_s)

    return (y_p.reshape(n_b, seq, d), y_s.reshape(n_s, t_s, d), new_p[0], new_p[1], new_p[2], new_mem_p,
            new_s[0], new_s[1], new_s[2], new_vb_s)
```

```python
import functools
import math

import jax
import jax.numpy as jnp
from jax import lax
from jax.experimental import pallas as pl
from jax.experimental.pallas import tpu as pltpu

F32 = jnp.float32
BF16 = jnp.bfloat16
I32 = jnp.int32
U32 = jnp.uint32

EPS = 1e-6
NEG_INF = -1e30

HEAD_DIM_A = 128
HEADS_PER_GROUP_A = 4
DIL_GROUPS = ((128, 1), (512, 4), (2048, 16))
GROUP_W_A = HEADS_PER_GROUP_A * HEAD_DIM_A
WIDTH_A = len(DIL_GROUPS) * GROUP_W_A
CHUNK = 128
N_GROUPS_B = 8
N_HEADS_M = 4
HEAD_DIM_M = 256
N_BUCKETS = 32
MAX_EXACT = N_BUCKETS // 2
MAX_DISTANCE = 2048
N_EXPERT_GROUPS = 4
EXPERTS_PER_GROUP = 8
N_EXPERTS = N_EXPERT_GROUPS * EXPERTS_PER_GROUP
TOP_K = 2
LANES = 128
MOE_ROWS = 256
VMEM_LIMIT = 56 * 1024 * 1024


def _cparams(n_grid, vmem=VMEM_LIMIT):
    return pltpu.CompilerParams(dimension_semantics=("arbitrary",) * n_grid, vmem_limit_bytes=vmem)


def _rms(x, g):
    return x * lax.rsqrt(jnp.mean(x * x, axis=-1, keepdims=True) + EPS) * g


def _pack_bf16_pair(x):
    n = x.shape[1] // 2
    lo = lax.bitcast_convert_type(x[:, :n].astype(BF16).astype(F32), U32)
    hi = lax.bitcast_convert_type(x[:, n:].astype(BF16).astype(F32), U32)
    return (hi & jnp.uint32(0xFFFF0000)) | (lo >> 16)


def _unpack_bf16_pair(w):
    lo = lax.bitcast_convert_type(w << 16, F32)
    hi = lax.bitcast_convert_type(w & jnp.uint32(0xFFFF0000), F32)
    return lo, hi


def _store_token_tiles(ref, x):
    w = _pack_bf16_pair(x)
    m, n = w.shape
    nt = n // LANES
    for c in range(nt):
        ref[pl.ds(c, m, stride=nt), :] = w[:, c * LANES:(c + 1) * LANES]


def _load_token_tiles(ref, lead, row0, m, nt):
    los, his = [], []
    view = ref if lead is None else ref.at[lead]
    for c in range(nt):
        lo, hi = _unpack_bf16_pair(view[pl.ds(row0 * nt + c, m, stride=nt), :])
        los.append(lo)
        his.append(hi)
    return jnp.concatenate(los + his, axis=1)


def _rel_bucket(dist):
    d = jnp.maximum(dist, 1).astype(F32)
    large = MAX_EXACT + (jnp.log(d / MAX_EXACT) / math.log(MAX_DISTANCE / MAX_EXACT)
                         * (N_BUCKETS - MAX_EXACT)).astype(I32)
    return jnp.where(dist < MAX_EXACT, dist, jnp.minimum(large, N_BUCKETS - 1)).astype(I32)


def _prep_body(x_ref, g_ref, o_ref):
    o_ref[...] = _rms(x_ref[...], g_ref[...]).astype(o_ref.dtype)


def _prep(x2d, g, tm):
    m, d = x2d.shape
    return pl.pallas_call(
        _prep_body, name="prep", grid=(m // tm,),
        in_specs=[pl.BlockSpec((tm, d), lambda i: (i, 0)), pl.BlockSpec((1, d), lambda i: (0, 0))],
        out_specs=pl.BlockSpec((tm, d), lambda i: (i, 0)),
        out_shape=jax.ShapeDtypeStruct((m, d), BF16), compiler_params=_cparams(1),
    )(x2d, g.reshape(1, d))


def _headnorm(acc, g, hd, scale):
    outs = []
    for j in range(acc.shape[1] // hd):
        sl = acc[:, j * hd:(j + 1) * hd]
        outs.append(sl * lax.rsqrt(jnp.mean(sl * sl, axis=-1, keepdims=True) + EPS))
    y = outs[0] if len(outs) == 1 else jnp.concatenate(outs, axis=1)
    y = y * g
    return y * scale if scale != 1.0 else y


def _proj_body(h_ref, *refs, n_w, mode, hd, scale):
    w_refs, (g_ref, o_ref, w_scr) = refs[:n_w], refs[n_w:]
    cw = w_refs[0].shape[1]

    @pl.when(pl.program_id(1) == 0)
    def _():
        for i, w_ref in enumerate(w_refs):
            w_scr[:, i * cw:(i + 1) * cw] = w_ref[...].astype(w_scr.dtype)

    acc = jnp.dot(h_ref[...], w_scr[...], preferred_element_type=F32)
    if mode == "none":
        o_ref[...] = acc.astype(o_ref.dtype)
    elif mode == "sigmoid":
        o_ref[...] = jax.nn.sigmoid(acc).astype(o_ref.dtype)
    elif mode == "norm":
        o_ref[...] = _headnorm(acc, g_ref[...], hd, scale).astype(o_ref.dtype)
    else:
        @pl.when(pl.program_id(0) == 0)
        def _():
            o_ref[...] = _headnorm(acc, g_ref[...], hd, scale).astype(o_ref.dtype)

        @pl.when(pl.program_id(0) != 0)
        def _():
            o_ref[...] = acc.astype(o_ref.dtype)


W_COLS = 512


def _proj(h, w, cols, tm, mode, out_dtype, gain=None, hd=None, scale=1.0, name="proj"):
    m, k = h.shape
    base, stride, n_w, n_tiles = cols
    tn = n_w * W_COLS
    if gain is None:
        g = jnp.ones((1, tn), F32)
    else:
        g = jnp.tile(gain.astype(F32).reshape(1, -1), (1, tn // gain.shape[-1]))
    body = functools.partial(_proj_body, n_w=n_w, mode=mode, hd=hd, scale=scale)
    w_specs = [pl.BlockSpec((k, W_COLS), functools.partial(lambda j, i, off: (0, base + stride * j + off), off=off))
               for off in range(n_w)]
    return pl.pallas_call(
        body, name=name, grid=(n_tiles, m // tm),
        in_specs=[pl.BlockSpec((tm, k), lambda j, i: (i, 0))] + w_specs + [pl.BlockSpec((1, tn), lambda j, i: (0, 0))],
        out_specs=pl.BlockSpec((tm, tn), lambda j, i: (i, j)),
        out_shape=jax.ShapeDtypeStruct((m, n_tiles * tn), out_dtype),
        scratch_shapes=[pltpu.VMEM((k, tn), BF16)], compiler_params=_cparams(2),
    )(h, *([w] * n_w), g)


PERM_ROWS = 256


def _perm_matrix(n, dil, inverse=False):
    per = n // dil
    o = lax.broadcasted_iota(I32, (n, n), 1 if inverse else 0)
    s = lax.broadcasted_iota(I32, (n, n), 0 if inverse else 1)
    return (s == (o % per) * dil + o // per).astype(BF16)


def _qkv_body(h_ref, wq_ref, wk_ref, wv_ref, gq_ref, gk_ref, q_ref, kv_ref, tok_ref, w_scr, *, dil, scale):
    c = pl.program_id(1)
    tm = h_ref.shape[0]
    nh = HEADS_PER_GROUP_A

    @pl.when(pl.program_id(0) == 0)
    def _():
        for i, w_ref in enumerate((wq_ref, wk_ref, wv_ref)):
            @pl.when(c == i)
            def _():
                w_scr[i] = w_ref[...].astype(BF16)

    acc = jnp.dot(h_ref[...], w_scr[c], preferred_element_type=F32)

    def put_perm(dst_ref, y, col0):
        if dil == 1:
            dst_ref[0, :, col0:col0 + GROUP_W_A] = y.astype(dst_ref.dtype)
            return
        pm = _perm_matrix(PERM_ROWS, dil)
        per = PERM_ROWS // dil
        yb = y.astype(BF16)
        for t in range(tm // PERM_ROWS):
            yp = jnp.dot(pm, yb[t * PERM_ROWS:(t + 1) * PERM_ROWS], preferred_element_type=F32).astype(BF16)
            for r in range(dil):
                dst_ref[r, t * per:(t + 1) * per, col0:col0 + GROUP_W_A] = yp[r * per:(r + 1) * per]

    def put_tok(y, c0):
        for i in range(nh):
            tok_ref[pl.ds(c0 + i, tm, stride=2 * nh), :] = y[:, i * HEAD_DIM_A:(i + 1) * HEAD_DIM_A]

    @pl.when(c == 0)
    def _():
        put_perm(q_ref, _headnorm(acc, gq_ref[...], HEAD_DIM_A, scale), 0)

    @pl.when(c == 1)
    def _():
        y = _headnorm(acc, gk_ref[...], HEAD_DIM_A, 1.0)
        put_tok(y, 0)
        put_perm(kv_ref, y, 0)

    @pl.when(c == 2)
    def _():
        put_tok(acc, nh)
        put_perm(kv_ref, acc, 0)


def _proj_qkv(h, w, sec, g, g_qa, g_ka, n_batch, seq, tm, dil, act_dtype=BF16):
    n, k = h.shape
    ln = seq // dil
    per = tm // dil
    mt = seq // tm
    assert seq % tm == 0
    assert dil == 1 or (act_dtype == BF16 and tm % PERM_ROWS == 0 and (PERM_ROWS // dil) % 16 == 0)
    tile = lambda v: jnp.tile(v.astype(F32).reshape(1, -1), (1, GROUP_W_A // v.shape[-1]))
    wspec = lambda blk: pl.BlockSpec((k, W_COLS), lambda m, c: (0, blk))
    q, kv, tok = pl.pallas_call(
        functools.partial(_qkv_body, dil=dil, scale=HEAD_DIM_A ** -0.5), name=f"proj_qkv{g}", grid=(n // tm, 3),
        in_specs=[pl.BlockSpec((tm, k), lambda m, c: (m, 0)),
                  wspec(sec["q"] + g), wspec(sec["k"] + g), wspec(sec["v"] + g),
                  pl.BlockSpec((1, GROUP_W_A), lambda m, c: (0, 0)), pl.BlockSpec((1, GROUP_W_A), lambda m, c: (0, 0))],
        out_specs=[pl.BlockSpec((None, dil, per, GROUP_W_A), lambda m, c: (m // mt, 0, m % mt, 0)),
                   pl.BlockSpec((None, dil, per, GROUP_W_A), lambda m, c: (m // mt, 0, m % mt, jnp.maximum(c - 1, 0))),
                   pl.BlockSpec((tm * 2 * HEADS_PER_GROUP_A, LANES), lambda m, c: (m, 0))],
        out_shape=[jax.ShapeDtypeStruct((n_batch, dil, ln, GROUP_W_A), act_dtype),
                   jax.ShapeDtypeStruct((n_batch, dil, ln, 2 * GROUP_W_A), act_dtype),
                   jax.ShapeDtypeStruct((n * 2 * HEADS_PER_GROUP_A, LANES), F32)],
        scratch_shapes=[pltpu.VMEM((3, k, W_COLS), BF16)], compiler_params=_cparams(2),
    )(h, w, w, w, tile(g_qa), tile(g_ka))
    return q.reshape(n, GROUP_W_A), kv.reshape(n, 2 * GROUP_W_A), tok


def _bias_from_table(tab_ref, bidx, col):
    acc = jnp.zeros(bidx.shape, F32)
    for kb in range(N_BUCKETS):
        acc = jnp.where(bidx == kb, tab_ref[kb, col], acc)
    return acc


def _split_hi_lo(x):
    hi = x.astype(BF16)
    return hi, (x - hi.astype(F32)).astype(BF16)


def _attn_p_body(tab_ref, bidx_ref, q_ref, kp_ref, kc_ref, vp_ref, vc_ref, o_ref, bias_scr, *, g):
    first = (pl.program_id(0) == 0) & (pl.program_id(1) == 0)

    @pl.when(first)
    def _():
        bidx = bidx_ref[...]
        for h in range(HEADS_PER_GROUP_A):
            bias_scr[h] = _bias_from_table(tab_ref, bidx, g * HEADS_PER_GROUP_A + h)

    qb = pl.program_id(1)
    blk = q_ref.shape[0]
    q = q_ref[...]
    k = jnp.concatenate([kp_ref[...], kc_ref[...]], axis=0)
    v = jnp.concatenate([vp_ref[...], vc_ref[...]], axis=0)
    row = lax.broadcasted_iota(I32, (blk, 2 * blk), 0)
    col = lax.broadcasted_iota(I32, (blk, 2 * blk), 1)
    dist = blk + row - col
    valid = (dist >= 0) & (dist <= blk) & ((qb > 0) | (col >= blk))
    lane = lax.broadcasted_iota(I32, (blk, LANES), 1)
    lse_blk = jnp.zeros((blk, LANES), F32)
    outs = []
    for h in range(HEADS_PER_GROUP_A):
        hs = slice(h * HEAD_DIM_A, (h + 1) * HEAD_DIM_A)
        s = lax.dot_general(q[:, hs], k[:, hs], (((1,), (1,)), ((), ())), preferred_element_type=F32)
        s = jnp.where(valid, s + bias_scr[h], NEG_INF)
        m = jnp.max(s, axis=-1, keepdims=True)
        p = jnp.exp(s - m)
        l = jnp.sum(p, axis=-1, keepdims=True)
        o = jnp.dot(p.astype(BF16), v[:, hs], preferred_element_type=F32) / l
        outs.append(o)
        lse_blk = jnp.where(lane == h, m + jnp.log(l), lse_blk)
    lse_hi, lse_lo = _split_hi_lo(lse_blk)
    o_ref[...] = jnp.concatenate([jnp.concatenate(outs, axis=1).astype(BF16), lse_hi, lse_lo], axis=1)


O_EXT_W = GROUP_W_A + 2 * LANES


def _attn_prompt(q, kv, rel_bias, g, n_batch, seq):
    win, dil = DIL_GROUPS[g]
    blk = win // dil
    nb = seq // dil // blk
    n = n_batch * seq
    a = jnp.arange(blk)[:, None]
    c = jnp.arange(2 * blk)[None, :]
    bidx = _rel_bucket(jnp.maximum(blk + a - c, 0) * dil)

    def cur(s, j):
        return s * nb + j

    def prev(s, j):
        return s * nb + jnp.maximum(j - 1, 0)

    return pl.pallas_call(
        functools.partial(_attn_p_body, g=g), name=f"attn_p{g}", grid=(n_batch * dil, nb),
        in_specs=[
            pl.BlockSpec(memory_space=pltpu.SMEM),
            pl.BlockSpec((blk, 2 * blk), lambda s, j: (0, 0)),
            pl.BlockSpec((blk, GROUP_W_A), lambda s, j: (cur(s, j), 0)),
            pl.BlockSpec((blk, GROUP_W_A), lambda s, j: (prev(s, j), 0)),
            pl.BlockSpec((blk, GROUP_W_A), lambda s, j: (cur(s, j), 0)),
            pl.BlockSpec((blk, GROUP_W_A), lambda s, j: (prev(s, j), 1)),
            pl.BlockSpec((blk, GROUP_W_A), lambda s, j: (cur(s, j), 1)),
        ],
        out_specs=pl.BlockSpec((blk, O_EXT_W), lambda s, j: (cur(s, j), 0)),
        out_shape=jax.ShapeDtypeStruct((n, O_EXT_W), BF16),
        scratch_shapes=[pltpu.VMEM((HEADS_PER_GROUP_A, blk, 2 * blk), F32)],
        compiler_params=_cparams(2),
    )(rel_bias, bidx, q, kv, kv, kv, kv)


def _attn_s_body(tab_ref, bidx_ref, q_ref, kvn_ref, tokn_ref, cache_hbm, o_ref, newc_hbm,
                 raw, kv_scr, bias_scr, sem_in, sem_out, *, g, dil, lb, n_req):
    t_new = q_ref.shape[0]
    nh = HEADS_PER_GROUP_A
    n_rows = nh * t_new
    n_keys = kv_scr.shape[0]
    n_col = 2 * nh
    lb8 = lb * n_col
    n = pl.program_id(0)
    slot = n % 2

    def fetch(req, s):
        return pltpu.make_async_copy(cache_hbm.at[req], raw.at[s, pl.ds(0, lb8)], sem_in.at[s])

    def flush(req, s):
        return pltpu.make_async_copy(raw.at[s, pl.ds(t_new * n_col, lb8)], newc_hbm.at[req], sem_out.at[s])

    @pl.when(n == 0)
    def _():
        kv_scr[lb:, :] = jnp.zeros((n_keys - lb, kv_scr.shape[1]), BF16)
        bidx = bidx_ref[...]
        for h in range(nh):
            bias_scr[h * t_new:(h + 1) * t_new, :] = _bias_from_table(tab_ref, bidx, g * nh + h)
        fetch(0, 0).start()

    @pl.when(n >= 1)
    def _():
        flush(n - 1, 1 - slot).wait()

    @pl.when(n + 1 < n_req)
    def _():
        fetch(n + 1, 1 - slot).start()

    fetch(n, slot).wait()
    raw[slot, lb8:lb8 + t_new * n_col, :] = tokn_ref[...]
    flush(n, slot).start()
    new = kvn_ref[...]

    rows_per = min(lb, 256)
    for c in range(n_col):
        for r0 in range(0, lb, rows_per):
            kv_scr[r0:r0 + rows_per, c * LANES:(c + 1) * LANES] = raw[
                slot, pl.ds(r0 * n_col + c, rows_per, stride=n_col), :].astype(BF16)
    kv_scr[lb:lb + 2 * t_new, :] = jnp.concatenate([new, jnp.zeros_like(new)], axis=0).astype(BF16)

    q = q_ref[...]
    qt = jnp.concatenate([q] * nh, axis=0)
    rr = lax.broadcasted_iota(I32, (n_rows, GROUP_W_A), 0)
    cc = lax.broadcasted_iota(I32, (n_rows, GROUP_W_A), 1)
    qbd = jnp.where(cc // HEAD_DIM_A == rr // t_new, qt, 0.0).astype(BF16)

    s = lax.dot_general(qbd, kv_scr[:, :GROUP_W_A], (((1,), (1,)), ((), ())), preferred_element_type=F32)
    row = lax.broadcasted_iota(I32, (n_rows, n_keys), 0)
    col = lax.broadcasted_iota(I32, (n_rows, n_keys), 1)
    delta = lb + (row & (t_new - 1)) - col
    valid = (delta >= 0) & (delta <= lb) & ((delta & (dil - 1)) == 0)
    s = jnp.where(valid, s + bias_scr[...], NEG_INF)
    m = jnp.max(s, axis=-1, keepdims=True)
    p = jnp.exp(s - m)
    l = jnp.sum(p, axis=-1, keepdims=True)
    o = jnp.dot(p.astype(BF16), kv_scr[:, GROUP_W_A:], preferred_element_type=F32) / l
    lse = m + jnp.log(l)
    lane = lax.broadcasted_iota(I32, (t_new, LANES), 1)
    lse_blk = jnp.zeros((t_new, LANES), F32)
    outs = []
    for h in range(nh):
        outs.append(o[h * t_new:(h + 1) * t_new, h * HEAD_DIM_A:(h + 1) * HEAD_DIM_A])
        lse_blk = jnp.where(lane == h, lse[h * t_new:(h + 1) * t_new, :], lse_blk)
    o_ref[...] = jnp.concatenate(outs + [lse_blk, jnp.zeros_like(lse_blk)], axis=1)

    @pl.when(n == n_req - 1)
    def _():
        flush(n, slot).wait()


def _attn_sample(q, kv_new, tok_new, cache, rel_bias, g, t_new):
    win, dil = DIL_GROUPS[g]
    n_col = 2 * HEADS_PER_GROUP_A
    n_req, lb = cache.shape[0], cache.shape[1] // n_col
    assert t_new & (t_new - 1) == 0 and dil & (dil - 1) == 0 and lb == win
    n_keys = lb + LANES
    t = jnp.arange(t_new)[:, None]
    c = jnp.arange(n_keys)[None, :]
    bidx = _rel_bucket(jnp.clip(lb + t - c, 0, lb))
    return pl.pallas_call(
        functools.partial(_attn_s_body, g=g, dil=dil, lb=lb, n_req=n_req), name=f"attn_s{g}", grid=(n_req,),
        in_specs=[
            pl.BlockSpec(memory_space=pltpu.SMEM),
            pl.BlockSpec((t_new, n_keys), lambda n: (0, 0)),
            pl.BlockSpec((t_new, GROUP_W_A), lambda n: (n, 0)),
            pl.BlockSpec((t_new, 2 * GROUP_W_A), lambda n: (n, 0)),
            pl.BlockSpec((t_new * n_col, LANES), lambda n: (n, 0)),
            pl.BlockSpec(memory_space=pl.ANY),
        ],
        out_specs=[pl.BlockSpec((t_new, O_EXT_W), lambda n: (n, 0)), pl.BlockSpec(memory_space=pl.ANY)],
        out_shape=[jax.ShapeDtypeStruct((n_req * t_new, O_EXT_W), F32), jax.ShapeDtypeStruct(cache.shape, F32)],
        scratch_shapes=[pltpu.VMEM((2, (lb + t_new) * n_col, LANES), F32),
                        pltpu.VMEM((n_keys, 2 * GROUP_W_A), BF16),
                        pltpu.VMEM((HEADS_PER_GROUP_A * t_new, n_keys), F32),
                        pltpu.SemaphoreType.DMA((2,)), pltpu.SemaphoreType.DMA((2,))],
        compiler_params=_cparams(1),
    )(rel_bias, bidx, q, kv_new, tok_new, cache)


def _gmlp_body(u_ref, v_ref, w_ref, b_ref, o_ref, *, period):
    c = w_ref.shape[1]
    nch = u_ref.shape[0] // c
    gd = u_ref.shape[1] // N_GROUPS_B
    i = lax.broadcasted_iota(I32, (c, c), 0)
    j = lax.broadcasted_iota(I32, (c, c), 1)
    mask = (j <= i) & ((i // period) == (j // period))
    b = b_ref[...]
    for g in range(N_GROUPS_B):
        gs = slice(g * gd, (g + 1) * gd)
        wg = jnp.where(mask, w_ref[g], 0.0).astype(BF16)
        vg = [v_ref[ch * c:(ch + 1) * c, gs].astype(BF16) for ch in range(nch)]
        vg = vg[0] if nch == 1 else jnp.concatenate(vg, axis=1)
        sg = jnp.dot(wg, vg, preferred_element_type=F32) + b[:, g:g + 1]
        for ch in range(nch):
            u = u_ref[ch * c:(ch + 1) * c, gs].astype(F32)
            o_ref[ch * c:(ch + 1) * c, gs] = (u * sg[:, ch * gd:(ch + 1) * gd]).astype(o_ref.dtype)


def _gmlp(u, v, w, b, tm, period):
    m, wb = u.shape
    c = w.shape[1]
    return pl.pallas_call(
        functools.partial(_gmlp_body, period=period), name="gmlp", grid=(m // tm,),
        in_specs=[pl.BlockSpec((tm, wb), lambda i: (i, 0)), pl.BlockSpec((tm, wb), lambda i: (i, 0)),
                  pl.BlockSpec((N_GROUPS_B, c, c), lambda i: (0, 0, 0)),
                  pl.BlockSpec((c, N_GROUPS_B), lambda i: (0, 0))],
        out_specs=pl.BlockSpec((tm, wb), lambda i: (i, 0)),
        out_shape=jax.ShapeDtypeStruct((m, wb), BF16), compiler_params=_cparams(1),
    )(u, v, w, b)


def _gmlp_new_body(u_ref, v_ref, w_ref, b_ref, o_ref):
    t = w_ref.shape[0]
    n = u_ref.shape[0] // t
    width = u_ref.shape[1]
    v = v_ref[...].astype(F32).reshape(n, t, width)
    sg = jnp.broadcast_to(b_ref[...][None], (n, t, width))
    p_row = lax.broadcasted_iota(I32, (t, width), 0)
    for q in range(t):
        sg = sg + jnp.where(p_row >= q, w_ref[q], 0.0)[None] * v[:, q:q + 1, :]
    o_ref[...] = (u_ref[...].astype(F32).reshape(n, t, width) * sg).reshape(n * t, width).astype(o_ref.dtype)


def _gmlp_new(u, v, w_s, b_s, t):
    m, wb = u.shape
    gd = wb // N_GROUPS_B
    wq = jnp.repeat(jnp.transpose(w_s[:, :t, :t], (2, 1, 0)), gd, axis=-1)
    bq = jnp.repeat(b_s[:, :t].T, gd, axis=-1)
    whole = lambda a: pl.BlockSpec(a.shape, lambda i: (0,) * a.ndim)
    return pl.pallas_call(
        _gmlp_new_body, name="gmlp_new", grid=(1,),
        in_specs=[whole(u), whole(v), whole(wq), whole(bq)], out_specs=whole(u),
        out_shape=jax.ShapeDtypeStruct((m, wb), BF16), compiler_params=_cparams(1),
    )(u, v, wq, bq)


def _memattn_body(q_ref, kv_ref, o_ref):
    wm = N_HEADS_M * HEAD_DIM_M
    q = q_ref[...].astype(BF16)
    outs = []
    for h in range(N_HEADS_M):
        hs = slice(h * HEAD_DIM_M, (h + 1) * HEAD_DIM_M)
        k = kv_ref[:, hs].astype(BF16)
        v = kv_ref[:, wm + h * HEAD_DIM_M:wm + (h + 1) * HEAD_DIM_M].astype(BF16)
        s = lax.dot_general(q[:, hs], k, (((1,), (1,)), ((), ())), preferred_element_type=F32)
        m = jnp.max(s, axis=-1, keepdims=True)
        p = jnp.exp(s - m)
        l = jnp.sum(p, axis=-1, keepdims=True)
        outs.append(jnp.dot(p.astype(BF16), v, preferred_element_type=F32) / l)
    o_ref[...] = jnp.concatenate(outs, axis=1).astype(o_ref.dtype)


def _memattn(q, kv, tq, out_dtype):
    m, wm = q.shape
    n = kv.shape[0]
    per = m // n // tq
    kv_spec = pl.BlockSpec((None,) + kv.shape[1:], lambda i: (i // per,) + (0,) * (kv.ndim - 1))
    return pl.pallas_call(
        _memattn_body, name="memattn", grid=(m // tq,),
        in_specs=[pl.BlockSpec((tq, wm), lambda i: (i, 0)), kv_spec],
        out_specs=pl.BlockSpec((tq, wm), lambda i: (i, 0)),
        out_shape=jax.ShapeDtypeStruct((m, wm), out_dtype), compiler_params=_cparams(1),
    )(q, kv)


def _mix_body(gt_ref, o0_ref, o1_ref, o2_ref, ob_ref, om_ref, wpa_ref, wpb_ref, wpm_ref, z_ref, *, dils):
    d = z_ref.shape[1]
    tm = z_ref.shape[0]
    os_, ls = [], []
    for o_ref, dil in zip((o0_ref, o1_ref, o2_ref), dils):
        if dil == 1:
            x = o_ref[...].astype(F32)
        else:
            rows = jnp.concatenate([o_ref[r] for r in range(dil)], axis=0)
            x = jnp.dot(_perm_matrix(tm, dil, inverse=True), rows, preferred_element_type=F32)
        os_.append(x[:, :GROUP_W_A])
        ls.append(x[:, GROUP_W_A:GROUP_W_A + LANES] + x[:, GROUP_W_A + LANES:])
    l0, l1, l2 = ls
    mx = jnp.maximum(jnp.maximum(l0, l1), l2)
    e0, e1, e2 = jnp.exp(l0 - mx), jnp.exp(l1 - mx), jnp.exp(l2 - mx)
    den = e0 + e1 + e2
    w0, w1, w2 = e0 / den, e1 / den, e2 / den
    cols = []
    for h in range(HEADS_PER_GROUP_A):
        hs = slice(h * HEAD_DIM_A, (h + 1) * HEAD_DIM_A)
        cols.append(w0[:, h:h + 1] * os_[0][:, hs] + w1[:, h:h + 1] * os_[1][:, hs] + w2[:, h:h + 1] * os_[2][:, hs])
    oa = jnp.concatenate(cols, axis=1).astype(BF16)
    pa = jnp.dot(oa, wpa_ref[...], preferred_element_type=F32)
    pb = jnp.dot(ob_ref[...].astype(BF16), wpb_ref[...], preferred_element_type=F32)
    pm = jnp.dot(om_ref[...].astype(BF16), wpm_ref[...], preferred_element_type=F32)
    z = (gt_ref[:, 0:d].astype(F32) * pa + gt_ref[:, d:2 * d].astype(F32) * pb
         + gt_ref[:, 2 * d:3 * d].astype(F32) * pm)
    z_ref[...] = z.astype(z_ref.dtype)


def _mix(gates, o_list, dils, seq, ob, om, wpa, wpb, wpm, tm):
    m = gates.shape[0]
    d = wpa.shape[1]
    mt = seq // tm if any(dl > 1 for dl in dils) else 1

    def rows(width):
        return pl.BlockSpec((tm, width), lambda i: (i, 0))

    def whole(a):
        return pl.BlockSpec(a.shape, lambda i: (0, 0))

    o_specs, o_args = [], []
    for o, dil in zip(o_list, dils):
        if dil == 1:
            o_specs.append(rows(O_EXT_W))
            o_args.append(o)
        else:
            assert seq % tm == 0 and (tm // dil) % 16 == 0
            o_specs.append(pl.BlockSpec((None, dil, tm // dil, O_EXT_W), lambda i: (i // mt, 0, i % mt, 0)))
            o_args.append(o.reshape(m // seq, dil, seq // dil, O_EXT_W))
    return pl.pallas_call(
        functools.partial(_mix_body, dils=tuple(dils)), name="mix", grid=(m // tm,),
        in_specs=[rows(gates.shape[1])] + o_specs
                 + [rows(ob.shape[1]), rows(om.shape[1]), whole(wpa), whole(wpb), whole(wpm)],
        out_specs=rows(d),
        out_shape=jax.ShapeDtypeStruct((m, d), BF16), compiler_params=_cparams(1),
    )(gates, *o_args, ob, om, wpa, wpb, wpm)


def _route(logits):
    lane = lax.broadcasted_iota(I32, logits.shape, 1)
    lane_f = lane.astype(F32)
    is_g = lane < N_EXPERT_GROUPS
    gmax = jnp.max(jnp.where(is_g, logits, -jnp.inf), axis=1, keepdims=True)
    gsel = jnp.min(jnp.where(is_g & (logits == gmax), lane_f, float(LANES)), axis=1, keepdims=True).astype(I32)
    gden = jnp.sum(jnp.where(is_g, jnp.exp(logits - gmax), 0.0), axis=1, keepdims=True)
    pg = 1.0 / gden
    e_lane = lane - N_EXPERT_GROUPS
    in_grp = (e_lane >= 0) & (e_lane < N_EXPERTS) & ((e_lane // EXPERTS_PER_GROUP) == gsel)
    m1 = jnp.max(jnp.where(in_grp, logits, -jnp.inf), axis=1, keepdims=True)
    i1 = jnp.min(jnp.where(in_grp & (logits == m1), lane_f, float(LANES)), axis=1, keepdims=True).astype(I32)
    rest = in_grp & (lane != i1)
    m2 = jnp.max(jnp.where(rest, logits, -jnp.inf), axis=1, keepdims=True)
    i2 = jnp.min(jnp.where(rest & (logits == m2), lane_f, float(LANES)), axis=1, keepdims=True).astype(I32)
    e2 = jnp.exp(m2 - m1)
    w1 = pg / (1.0 + e2)
    w2 = pg * e2 / (1.0 + e2)
    eid = jnp.where(lane == 0, i1 - N_EXPERT_GROUPS, jnp.where(lane == 1, i2 - N_EXPERT_GROUPS, 0))
    wts = jnp.where(lane == 0, w1, jnp.where(lane == 1, w2, 0.0))
    return eid, wts


def _resid_body(x_ref, z_ref, wo_ref, gf_ref, wr_ref, br_ref, xmid_ref, hpk_ref, eid_ref, wts_ref):
    xm = x_ref[...] + jnp.dot(z_ref[...], wo_ref[...], preferred_element_type=F32)
    xmid_ref[...] = xm
    hf = _rms(xm, gf_ref[...])
    wr = wr_ref[...]
    wr_hi = wr.astype(BF16)
    wr_lo = (wr - wr_hi.astype(F32)).astype(BF16)
    hf_hi = hf.astype(BF16)
    hf_lo = (hf - hf_hi.astype(F32)).astype(BF16)
    logits = (jnp.dot(hf_hi, wr_hi, preferred_element_type=F32) + jnp.dot(hf_hi, wr_lo, preferred_element_type=F32)
              + jnp.dot(hf_lo, wr_hi, preferred_element_type=F32) + br_ref[...])
    eid, wts = _route(logits)
    eid_ref[...] = eid
    wts_ref[...] = wts
    _store_token_tiles(hpk_ref, hf)


def _resid(x2d, z, wo, g_ffn, w_r, b_r, tm):
    m, d = x2d.shape

    def rows(width):
        return pl.BlockSpec((tm, width), lambda i: (i, 0))

    def whole(a):
        return pl.BlockSpec(a.shape, lambda i: (0, 0))

    gf = g_ffn.reshape(1, d)
    nt = d // 2 // LANES
    return pl.pallas_call(
        _resid_body, name="resid", grid=(m // tm,),
        in_specs=[rows(d), rows(d), whole(wo), whole(gf), whole(w_r), whole(b_r)],
        out_specs=[rows(d), pl.BlockSpec((tm * nt, LANES), lambda i: (i, 0)), rows(LANES), rows(LANES)],
        out_shape=[jax.ShapeDtypeStruct((m, d), F32), jax.ShapeDtypeStruct((m * nt, LANES), U32),
                   jax.ShapeDtypeStruct((m, LANES), I32), jax.ShapeDtypeStruct((m, LANES), F32)],
        compiler_params=_cparams(1),
    )(x2d, z, wo, gf, w_r, b_r)


TOKEN_TILE = 8
GATHER_UNROLL = 8


def _gather_rows(idx_s, slot, src_hbm, dst, sem, n):
    def body(i, carry):
        src = pl.multiple_of(idx_s[slot, i], TOKEN_TILE)
        dst_row = pl.multiple_of(i * TOKEN_TILE, TOKEN_TILE)
        pltpu.make_async_copy(src_hbm.at[pl.ds(src, TOKEN_TILE)], dst.at[pl.ds(dst_row, TOKEN_TILE)], sem).start()
        return carry
    lax.fori_loop(0, n, body, 0, unroll=GATHER_UNROLL)


def _dispatch_body(idx_hbm, h_hbm, xs_hbm, idx_s, isem, buf, sem, osem, *, n):
    j = pl.program_id(0)
    nbuf = buf.shape[0]
    tiles = buf.shape[1]
    rows = tiles // TOKEN_TILE

    def out_copy(blk):
        s = blk % nbuf
        return pltpu.make_async_copy(buf.at[s], xs_hbm.at[pl.ds(pl.multiple_of(blk * tiles, tiles), tiles)],
                                     osem.at[s])

    def idx_copy(blk):
        return pltpu.make_async_copy(idx_hbm.at[blk], idx_s.at[blk % nbuf], isem.at[blk % nbuf])

    def gather(blk):
        _gather_rows(idx_s, blk % nbuf, h_hbm, buf.at[blk % nbuf], sem.at[blk % nbuf], rows)

    @pl.when(j == 0)
    def _():
        for b in range(3):
            idx_copy(b).start()
        for b in range(2):
            idx_copy(b).wait()
            gather(b)

    @pl.when(j >= 2)
    def _():
        out_copy(j - 2).wait()

    @pl.when(j + 3 < n)
    def _():
        idx_copy(j + 3).start()

    @pl.when(j + 2 < n)
    def _():
        idx_copy(j + 2).wait()
        gather(j + 2)

    s = j % nbuf
    pltpu.make_async_copy(h_hbm.at[pl.ds(0, tiles)], buf.at[s], sem.at[s]).wait()
    out_copy(j).start()

    @pl.when(j == n - 1)
    def _():
        out_copy(j - 1).wait()
        out_copy(j).wait()


def _dispatch(hpk, src_tok, n_blocks):
    rows = MOE_ROWS
    nbuf = 4
    assert n_blocks >= 3
    idx = (src_tok * TOKEN_TILE).reshape(n_blocks, rows)
    hbm = pl.BlockSpec(memory_space=pl.ANY)
    return pl.pallas_call(
        functools.partial(_dispatch_body, n=n_blocks), name="dispatch", grid=(n_blocks,),
        in_specs=[hbm, hbm], out_specs=hbm,
        out_shape=jax.ShapeDtypeStruct((n_blocks * rows * TOKEN_TILE, LANES), U32),
        scratch_shapes=[pltpu.SMEM((nbuf, rows), I32), pltpu.SemaphoreType.DMA((nbuf,)),
                        pltpu.VMEM((nbuf, rows * TOKEN_TILE, LANES), U32), pltpu.SemaphoreType.DMA((nbuf,)),
                        pltpu.SemaphoreType.DMA((nbuf,))],
        compiler_params=_cparams(1),
    )(idx, hpk)


def _ffn_body(blk_e_ref, nused_ref, eord_ref, enext_ref, x_ref, wg_hbm, wu_hbm, wd_hbm, y_ref,
              wg_f, wu_f, wd_f, wsem, wg_s, wu_s, wd_s):
    j = pl.program_id(0)
    nused = nused_ref[0]
    rows = x_ref.shape[0] // TOKEN_TILE
    e = blk_e_ref[j]
    new_expert = (j == 0) | (e != blk_e_ref[jnp.maximum(j - 1, 0)])
    ws = eord_ref[j] % 2

    def fetch_w(expert, s):
        return [pltpu.make_async_copy(src.at[expert], dst.at[s], wsem.at[s, i])
                for i, (src, dst) in enumerate(((wg_hbm, wg_f), (wu_hbm, wu_f), (wd_hbm, wd_f)))]

    @pl.when(j == 0)
    def _():
        for c in fetch_w(e, ws):
            c.start()

    @pl.when(new_expert)
    def _():
        nxt = enext_ref[j]

        @pl.when(nxt >= 0)
        def _():
            for c in fetch_w(nxt, 1 - ws):
                c.start()

        for c in fetch_w(e, ws):
            c.wait()
        wg_s[...] = wg_f[ws].astype(BF16)
        wu_s[...] = wu_f[ws].astype(BF16)
        wd_s[...] = wd_f[ws].astype(BF16)

    @pl.when(j < nused)
    def _():
        x = _load_token_tiles(x_ref, None, 0, rows, TOKEN_TILE).astype(BF16)
        a = jnp.dot(x, wg_s[...], preferred_element_type=F32)
        b = jnp.dot(x, wu_s[...], preferred_element_type=F32)
        hm = (a * jax.nn.sigmoid(a) * b).astype(BF16)
        _store_token_tiles(y_ref, jnp.dot(hm, wd_s[...], preferred_element_type=F32))

    @pl.when(j >= nused)
    def _():
        y_ref[...] = jnp.zeros(y_ref.shape, y_ref.dtype)


def _ffn(xs, blk_e, nused, eord, enext, wg, wu, wd):
    n_blocks = blk_e.shape[0]
    tiles = MOE_ROWS * TOKEN_TILE
    d, de = wg.shape[1], wg.shape[2]
    assert d // 2 == TOKEN_TILE * LANES
    hbm = pl.BlockSpec(memory_space=pl.ANY)
    grid_spec = pltpu.PrefetchScalarGridSpec(
        num_scalar_prefetch=4, grid=(n_blocks,),
        in_specs=[pl.BlockSpec((tiles, LANES), lambda j, be, nu, *_: (jnp.minimum(j, nu[0] - 1), 0)),
                  hbm, hbm, hbm],
        out_specs=pl.BlockSpec((tiles, LANES), lambda j, *_: (j, 0)),
        scratch_shapes=[pltpu.VMEM((2, d, de), wg.dtype), pltpu.VMEM((2, d, de), wu.dtype),
                        pltpu.VMEM((2, de, d), wd.dtype), pltpu.SemaphoreType.DMA((2, 3)),
                        pltpu.VMEM((d, de), BF16), pltpu.VMEM((d, de), BF16), pltpu.VMEM((de, d), BF16)],
    )
    return pl.pallas_call(
        _ffn_body, name="ffn", grid_spec=grid_spec,
        out_shape=jax.ShapeDtypeStruct((n_blocks * tiles, LANES), U32), compiler_params=_cparams(1),
    )(blk_e, nused, eord, enext, xs, wg, wu, wd)


def _combine_body(idx_hbm, x_ref, w_ref, y_hbm, o_ref, idx_s, isem, buf, sem, *, n):
    i = pl.program_id(0)
    slot = i % 2
    rows = buf.shape[1] // TOKEN_TILE
    tm = rows // TOP_K

    def idx_copy(t):
        return pltpu.make_async_copy(idx_hbm.at[t], idx_s.at[t % 2], isem.at[t % 2])

    def gather(t):
        _gather_rows(idx_s, t % 2, y_hbm, buf.at[t % 2], sem.at[t % 2], rows)

    @pl.when(i == 0)
    def _():
        idx_copy(0).start()
        if n > 1:
            idx_copy(1).start()
        idx_copy(0).wait()
        gather(0)

    if n > 1:
        @pl.when(i + 1 < n)
        def _():
            idx_copy(i + 1).wait()
            gather(i + 1)

    if n > 2:
        @pl.when(i + 2 < n)
        def _():
            idx_copy(i + 2).start()

    pltpu.make_async_copy(y_hbm.at[pl.ds(0, rows * TOKEN_TILE)], buf.at[slot], sem.at[slot]).wait()
    w = w_ref[...]
    y0 = _load_token_tiles(buf, slot, 0, tm, TOKEN_TILE)
    y1 = _load_token_tiles(buf, slot, tm, tm, TOKEN_TILE)
    o_ref[...] = x_ref[...] + (w[:, 0:1] * y0 + w[:, 1:2] * y1)


def _combine(xmid, wts, dest, ypk, tm):
    m, d = xmid.shape
    nt = m // tm
    assert d // 2 == TOKEN_TILE * LANES
    idx = (dest * TOKEN_TILE).reshape(nt, tm, TOP_K).transpose(0, 2, 1).reshape(nt, TOP_K * tm)
    hbm = pl.BlockSpec(memory_space=pl.ANY)
    return pl.pallas_call(
        functools.partial(_combine_body, n=nt), name="combine", grid=(nt,),
        in_specs=[hbm, pl.BlockSpec((tm, d), lambda i: (i, 0)), pl.BlockSpec((tm, LANES), lambda i: (i, 0)), hbm],
        out_specs=pl.BlockSpec((tm, d), lambda i: (i, 0)),
        out_shape=jax.ShapeDtypeStruct((m, d), F32),
        scratch_shapes=[pltpu.SMEM((2, TOP_K * tm), I32), pltpu.SemaphoreType.DMA((2,)),
                        pltpu.VMEM((2, TOP_K * tm * TOKEN_TILE, LANES), U32), pltpu.SemaphoreType.DMA((2,))],
        compiler_params=_cparams(1),
    )(idx, xmid, wts, ypk)


def _moe_plan(eid2):
    n_slot = eid2.shape[0] * TOP_K
    eid = eid2.reshape(n_slot)
    onehot = (eid[:, None] == jnp.arange(N_EXPERTS, dtype=I32)[None, :]).astype(I32)
    csum = jnp.cumsum(onehot, axis=0)
    rank = jnp.take_along_axis(csum, eid[:, None], axis=1)[:, 0] - 1
    counts = csum[-1]
    padded = (counts + MOE_ROWS - 1) // MOE_ROWS * MOE_ROWS
    pend = jnp.cumsum(padded)
    pstart = pend - padded
    dest = pstart[eid] + rank
    n_blocks = -(-n_slot // MOE_ROWS) + N_EXPERTS
    nused = (pend[-1] // MOE_ROWS).astype(I32)
    blk = jnp.minimum(jnp.arange(n_blocks, dtype=I32), nused - 1)
    blk_e = jnp.minimum(jnp.searchsorted(pend, blk * MOE_ROWS, side="right"), N_EXPERTS - 1).astype(I32)
    src_tok = jnp.zeros((n_blocks * MOE_ROWS,), I32).at[dest].set(jnp.arange(n_slot, dtype=I32) // TOP_K)
    has_rows = counts > 0
    ord_e = jnp.cumsum(has_rows.astype(I32)) - 1
    ids = jnp.arange(N_EXPERTS, dtype=I32)
    later = has_rows[None, :] & (ids[None, :] > ids[:, None])
    next_e = jnp.where(later.any(axis=1), jnp.argmax(later, axis=1), -1).astype(I32)
    return dest.astype(I32), src_tok, blk_e, nused.reshape(1), ord_e[blk_e].astype(I32), next_e[blk_e]


def _mixer(x2d, tm, p, attn_fn, gmlp_fn, mem_fn):
    h = _prep(x2d, p["g_mix"], tm)
    qdt = p["q_dtype"]
    wi, sec = p["w_in"], p["sec"]
    o_list, dils, seq = attn_fn(h)
    ub = _proj(h, wi, (sec["u"], 1, 1, sec["vb"] - sec["u"]), tm, "none", BF16, name="proj_ub")
    n_vb = sec["qm"] - sec["vb"]
    vb = _proj(h, wi, (sec["vb"], n_vb, n_vb, 1), tm, "norm", p["vb_dtype"], p["g_vb"], n_vb * W_COLS,
               name="proj_vb")
    qm = _proj(h, wi, (sec["qm"], 1, 1, sec["gt"] - sec["qm"]), tm, "norm", qdt, p["g_qm"], HEAD_DIM_M,
               HEAD_DIM_M ** -0.5, name="proj_qm")
    gates = _proj(h, wi, (sec["gt"], 2, 2, (sec["end"] - sec["gt"]) // 2), tm, "sigmoid", BF16, name="proj_gates")
    ob = gmlp_fn(ub, vb)
    om = mem_fn(qm)
    z = _mix(gates, o_list, dils, seq, ob, om, p["w_pa"], p["w_pb"], p["w_pm"], min(tm, 256))
    xmid, hpk, eid, wts = _resid(x2d, z, p["w_o"], p["g_ffn"], p["w_r"], p["b_r"], min(tm, 256))
    return vb, xmid, hpk, eid, wts


def kernel(x_prompt, x_sample, mem_prompt, cache_a0_kv, cache_a1_kv, cache_a2_kv, cache_mem_kv, rel_bias, g_mix, w_in, g_qa, g_ka, w_pa, g_vb, w_s, b_s, w_pb, g_mem, w_mk, w_mv, g_qm, g_km, w_pm, w_o, g_ffn, w_rg, b_rg, w_re, b_re, w_gate, w_up, w_down):
    n_b, seq, d = x_prompt.shape
    n_s, t_s, _ = x_sample.shape
    depth = w_in.shape[0]
    assert depth == 1
    l = 0
    caches = (cache_a0_kv, cache_a1_kv, cache_a2_kv)
    n_g = len(DIL_GROUPS)
    wb = w_pb.shape[1]
    wm = N_HEADS_M * HEAD_DIM_M

    offs = [0, WIDTH_A, 2 * WIDTH_A, 3 * WIDTH_A, 3 * WIDTH_A + wb, 3 * WIDTH_A + 2 * wb,
            3 * WIDTH_A + 2 * wb + wm, w_in.shape[2]]
    assert all(o % W_COLS == 0 for o in offs) and (offs[7] - offs[6]) % (2 * W_COLS) == 0
    sec = dict(zip(("q", "k", "v", "u", "vb", "qm", "gt", "end"), (o // W_COLS for o in offs)))
    p = {
        "g_mix": g_mix[l], "g_qa": g_qa[l], "g_ka": g_ka[l], "g_vb": g_vb[l], "g_qm": g_qm[l], "g_ffn": g_ffn[l],
        "w_in": w_in[l], "sec": sec,
        "w_pa": w_pa[l].astype(BF16), "w_pb": w_pb[l].astype(BF16), "w_pm": w_pm[l].astype(BF16),
        "w_o": w_o[l].astype(BF16),
    }
    n_r = N_EXPERT_GROUPS + N_EXPERTS
    p["w_r"] = jnp.concatenate([w_rg[l], w_re[l], jnp.zeros((d, LANES - n_r), F32)], axis=1)
    p["b_r"] = jnp.concatenate([b_rg[l], b_re[l], jnp.zeros((LANES - n_r,), F32)]).reshape(1, LANES)

    n_mem = mem_prompt.shape[1]
    h_mem = _prep(mem_prompt.reshape(n_b * n_mem, d), g_mem[l], 256)
    w_mkv = jnp.concatenate([w_mk[l], w_mv[l]], axis=1)
    n_mw = wm // W_COLS
    mkv = _proj(h_mem, w_mkv, (0, n_mw, n_mw, 2), 256, "norm_first", F32, g_km[l], HEAD_DIM_M, name="proj_mkv")
    new_mem_p = mkv.reshape(1, n_b, n_mem, 2, N_HEADS_M, HEAD_DIM_M)

    ps = dict(p, q_dtype=F32, vb_dtype=F32)
    m_s = n_s * t_s
    n_col = 2 * HEADS_PER_GROUP_A
    new_s = []

    def attn_sample(h):
        o_list = []
        for g in range(n_g):
            q, kv, tok = _proj_qkv(h, w_in[l], sec, g, g_qa[l], g_ka[l], 1, m_s, m_s, 1, act_dtype=F32)
            cache3 = caches[g][l].reshape(n_s, caches[g].shape[2] * n_col, LANES)
            o, newc = _attn_sample(q, kv, tok, cache3, rel_bias, g, t_s)
            o_list.append(o)
            new_s.append(newc.reshape(caches[g][l:l + 1].shape))
        return o_list, [1] * n_g, t_s

    vb_s, xmid_s, hpk_s, eid_s, wts_s = _mixer(
        x_sample.reshape(m_s, d), m_s, ps, attn_sample,
        lambda u, v: _gmlp_new(u, v, w_s[l], b_s[l], t_s),
        lambda qm: _memattn(qm, cache_mem_kv[l].astype(BF16).reshape(n_s, n_mem, 2 * wm), t_s, F32))
    new_vb_s = vb_s.reshape(1, n_s, t_s, wb)

    pp = dict(p, q_dtype=BF16, vb_dtype=BF16)
    new_p = []

    def attn_prompt(h):
        o_list = []
        for g, (win, dil) in enumerate(DIL_GROUPS):
            q, kv, tok = _proj_qkv(h, w_in[l], sec, g, g_qa[l], g_ka[l], n_b, seq, 512, dil)
            o_list.append(_attn_prompt(q, kv, rel_bias, g, n_b, seq))
            keep = min(win, seq)
            kv5 = tok.reshape(n_b, seq, 2, HEADS_PER_GROUP_A, HEAD_DIM_A)
            new_p.append((kv5 if keep == seq else kv5[:, seq - keep:])[None])
        return o_list, [dl for _, dl in DIL_GROUPS], seq

    _, xmid_p, hpk_p, eid_p, wts_p = _mixer(
        x_prompt.reshape(n_b * seq, d), 512, pp, attn_prompt,
        lambda u, v: _gmlp(u, v, w_s[l], b_s[l].T, 512, CHUNK),
        lambda qm: _memattn(qm, mkv.reshape(n_b, n_mem, 2 * wm), 256, BF16))

    n_p = n_b * seq
    hpk = jnp.concatenate([hpk_p, hpk_s], axis=0)
    eid2 = jnp.concatenate([eid_p[:, :TOP_K], eid_s[:, :TOP_K]], axis=0)
    dest, src_tok, blk_e, nused, eord, enext = _moe_plan(eid2)
    xs = _dispatch(hpk, src_tok, blk_e.shape[0])
    ypk = _ffn(xs, blk_e, nused, eord, enext, w_gate[l], w_up[l], w_down[l])
    dest2 = dest.reshape(-1, TOP_K)
    y_p = _combine(xmid_p, wts_p, dest2[:n_p], ypk, 256)
    y_s = _combine(xmid_s, wts_s, dest2[n_p:], ypk, m_s)

    return (y_p.reshape(n_b, seq, d), y_s.reshape(n_s, t_s, d), new_p[0], new_p[1], new_p[2], new_mem_p,
            new_s[0], new_s[1], new_s[2], new_vb_s)
```

```python
import functools
import math

import jax
import jax.numpy as jnp
from jax import lax
from jax.experimental import pallas as pl
from jax.experimental.pallas import tpu as pltpu

F32 = jnp.float32
BF16 = jnp.bfloat16
I32 = jnp.int32
U32 = jnp.uint32

EPS = 1e-6
NEG_INF = -1e30

HEAD_DIM_A = 128
HEADS_PER_GROUP_A = 4
DIL_GROUPS = ((128, 1), (512, 4), (2048, 16))
GROUP_W_A = HEADS_PER_GROUP_A * HEAD_DIM_A
WIDTH_A = len(DIL_GROUPS) * GROUP_W_A
CHUNK = 128
N_GROUPS_B = 8
N_HEADS_M = 4
HEAD_DIM_M = 256
N_BUCKETS = 32
MAX_EXACT = N_BUCKETS // 2
MAX_DISTANCE = 2048
N_EXPERT_GROUPS = 4
EXPERTS_PER_GROUP = 8
N_EXPERTS = N_EXPERT_GROUPS * EXPERTS_PER_GROUP
TOP_K = 2
LANES = 128
MOE_ROWS = 256
VMEM_LIMIT = 56 * 1024 * 1024


def _cparams(n_grid, vmem=VMEM_LIMIT):
    return pltpu.CompilerParams(dimension_semantics=("arbitrary",) * n_grid, vmem_limit_bytes=vmem)


def _rms(x, g):
    return x * lax.rsqrt(jnp.mean(x * x, axis=-1, keepdims=True) + EPS) * g


def _pack_bf16_pair(x):
    n = x.shape[1] // 2
    lo = lax.bitcast_convert_type(x[:, :n].astype(BF16).astype(F32), U32)
    hi = lax.bitcast_convert_type(x[:, n:].astype(BF16).astype(F32), U32)
    return (hi & jnp.uint32(0xFFFF0000)) | (lo >> 16)


def _unpack_bf16_pair(w):
    lo = lax.bitcast_convert_type(w << 16, F32)
    hi = lax.bitcast_convert_type(w & jnp.uint32(0xFFFF0000), F32)
    return lo, hi


def _store_token_tiles(ref, x):
    w = _pack_bf16_pair(x)
    m, n = w.shape
    nt = n // LANES
    for c in range(nt):
        ref[pl.ds(c, m, stride=nt), :] = w[:, c * LANES:(c + 1) * LANES]


def _load_token_tiles(ref, lead, row0, m, nt):
    los, his = [], []
    view = ref if lead is None else ref.at[lead]
    for c in range(nt):
        lo, hi = _unpack_bf16_pair(view[pl.ds(row0 * nt + c, m, stride=nt), :])
        los.append(lo)
        his.append(hi)
    return jnp.concatenate(los + his, axis=1)


def _rel_bucket(dist):
    d = jnp.maximum(dist, 1).astype(F32)
    large = MAX_EXACT + (jnp.log(d / MAX_EXACT) / math.log(MAX_DISTANCE / MAX_EXACT)
                         * (N_BUCKETS - MAX_EXACT)).astype(I32)
    return jnp.where(dist < MAX_EXACT, dist, jnp.minimum(large, N_BUCKETS - 1)).astype(I32)


def _prep_body(x_ref, g_ref, o_ref):
    o_ref[...] = _rms(x_ref[...], g_ref[...]).astype(o_ref.dtype)


def _prep(x2d, g, tm):
    m, d = x2d.shape
    return pl.pallas_call(
        _prep_body, name="prep", grid=(m // tm,),
        in_specs=[pl.BlockSpec((tm, d), lambda i: (i, 0)), pl.BlockSpec((1, d), lambda i: (0, 0))],
        out_specs=pl.BlockSpec((tm, d), lambda i: (i, 0)),
        out_shape=jax.ShapeDtypeStruct((m, d), BF16), compiler_params=_cparams(1),
    )(x2d, g.reshape(1, d))


def _headnorm(acc, g, hd, scale):
    outs = []
    for j in range(acc.shape[1] // hd):
        sl = acc[:, j * hd:(j + 1) * hd]
        outs.append(sl * lax.rsqrt(jnp.mean(sl * sl, axis=-1, keepdims=True) + EPS))
    y = outs[0] if len(outs) == 1 else jnp.concatenate(outs, axis=1)
    y = y * g
    return y * scale if scale != 1.0 else y


def _proj_body(h_ref, *refs, n_w, mode, hd, scale):
    w_refs, (g_ref, o_ref, w_scr) = refs[:n_w], refs[n_w:]
    cw = w_refs[0].shape[1]

    @pl.when(pl.program_id(1) == 0)
    def _():
        for i, w_ref in enumerate(w_refs):
            w_scr[:, i * cw:(i + 1) * cw] = w_ref[...].astype(w_scr.dtype)

    acc = jnp.dot(h_ref[...], w_scr[...], preferred_element_type=F32)
    if mode == "none":
        o_ref[...] = acc.astype(o_ref.dtype)
    elif mode == "sigmoid":
        o_ref[...] = jax.nn.sigmoid(acc).astype(o_ref.dtype)
    elif mode == "norm":
        o_ref[...] = _headnorm(acc, g_ref[...], hd, scale).astype(o_ref.dtype)
    else:
        @pl.when(pl.program_id(0) == 0)
        def _():
            o_ref[...] = _headnorm(acc, g_ref[...], hd, scale).astype(o_ref.dtype)

        @pl.when(pl.program_id(0) != 0)
        def _():
            o_ref[...] = acc.astype(o_ref.dtype)


W_COLS = 512


def _proj(h, w, cols, tm, mode, out_dtype, gain=None, hd=None, scale=1.0, name="proj"):
    m, k = h.shape
    base, stride, n_w, n_tiles = cols
    tn = n_w * W_COLS
    if gain is None:
        g = jnp.ones((1, tn), F32)
    else:
        g = jnp.tile(gain.astype(F32).reshape(1, -1), (1, tn // gain.shape[-1]))
    body = functools.partial(_proj_body, n_w=n_w, mode=mode, hd=hd, scale=scale)
    w_specs = [pl.BlockSpec((k, W_COLS), functools.partial(lambda j, i, off: (0, base + stride * j + off), off=off))
               for off in range(n_w)]
    return pl.pallas_call(
        body, name=name, grid=(n_tiles, m // tm),
        in_specs=[pl.BlockSpec((tm, k), lambda j, i: (i, 0))] + w_specs + [pl.BlockSpec((1, tn), lambda j, i: (0, 0))],
        out_specs=pl.BlockSpec((tm, tn), lambda j, i: (i, j)),
        out_shape=jax.ShapeDtypeStruct((m, n_tiles * tn), out_dtype),
        scratch_shapes=[pltpu.VMEM((k, tn), BF16)], compiler_params=_cparams(2),
    )(h, *([w] * n_w), g)


PERM_ROWS = 256


def _perm_matrix(n, dil, inverse=False):
    per = n // dil
    o = lax.broadcasted_iota(I32, (n, n), 1 if inverse else 0)
    s = lax.broadcasted_iota(I32, (n, n), 0 if inverse else 1)
    return (s == (o % per) * dil + o // per).astype(BF16)


def _qkv_body(h_ref, wq_ref, wk_ref, wv_ref, gq_ref, gk_ref, q_ref, kv_ref, tok_ref, w_scr, *, dil, scale):
    c = pl.program_id(1)
    tm = h_ref.shape[0]
    nh = HEADS_PER_GROUP_A

    @pl.when(pl.program_id(0) == 0)
    def _():
        for i, w_ref in enumerate((wq_ref, wk_ref, wv_ref)):
            @pl.when(c == i)
            def _():
                w_scr[i] = w_ref[...].astype(BF16)

    acc = jnp.dot(h_ref[...], w_scr[c], preferred_element_type=F32)

    def put_perm(dst_ref, y, col0):
        if dil == 1:
            dst_ref[0, :, col0:col0 + GROUP_W_A] = y.astype(dst_ref.dtype)
            return
        pm = _perm_matrix(PERM_ROWS, dil)
        per = PERM_ROWS // dil
        yb = y.astype(BF16)
        for t in range(tm // PERM_ROWS):
            yp = jnp.dot(pm, yb[t * PERM_ROWS:(t + 1) * PERM_ROWS], preferred_element_type=F32).astype(BF16)
            for r in range(dil):
                dst_ref[r, t * per:(t + 1) * per, col0:col0 + GROUP_W_A] = yp[r * per:(r + 1) * per]

    def put_tok(y, c0):
        for i in range(nh):
            tok_ref[pl.ds(c0 + i, tm, stride=2 * nh), :] = y[:, i * HEAD_DIM_A:(i + 1) * HEAD_DIM_A]

    @pl.when(c == 0)
    def _():
        put_perm(q_ref, _headnorm(acc, gq_ref[...], HEAD_DIM_A, scale), 0)

    @pl.when(c == 1)
    def _():
        y = _headnorm(acc, gk_ref[...], HEAD_DIM_A, 1.0)
        put_tok(y, 0)
        put_perm(kv_ref, y, 0)

    @pl.when(c == 2)
    def _():
        put_tok(acc, nh)
        put_perm(kv_ref, acc, 0)


def _proj_qkv(h, w, sec, g, g_qa, g_ka, n_batch, seq, tm, dil, act_dtype=BF16):
    n, k = h.shape
    ln = seq // dil
    per = tm // dil
    mt = seq // tm
    assert seq % tm == 0
    assert dil == 1 or (act_dtype == BF16 and tm % PERM_ROWS == 0 and (PERM_ROWS // dil) % 16 == 0)
    tile = lambda v: jnp.tile(v.astype(F32).reshape(1, -1), (1, GROUP_W_A // v.shape[-1]))
    wspec = lambda blk: pl.BlockSpec((k, W_COLS), lambda m, c: (0, blk))
    q, kv, tok = pl.pallas_call(
        functools.partial(_qkv_body, dil=dil, scale=HEAD_DIM_A ** -0.5), name=f"proj_qkv{g}", grid=(n // tm, 3),
        in_specs=[pl.BlockSpec((tm, k), lambda m, c: (m, 0)),
                  wspec(sec["q"] + g), wspec(sec["k"] + g), wspec(sec["v"] + g),
                  pl.BlockSpec((1, GROUP_W_A), lambda m, c: (0, 0)), pl.BlockSpec((1, GROUP_W_A), lambda m, c: (0, 0))],
        out_specs=[pl.BlockSpec((None, dil, per, GROUP_W_A), lambda m, c: (m // mt, 0, m % mt, 0)),
                   pl.BlockSpec((None, dil, per, GROUP_W_A), lambda m, c: (m // mt, 0, m % mt, jnp.maximum(c - 1, 0))),
                   pl.BlockSpec((tm * 2 * HEADS_PER_GROUP_A, LANES), lambda m, c: (m, 0))],
        out_shape=[jax.ShapeDtypeStruct((n_batch, dil, ln, GROUP_W_A), act_dtype),
                   jax.ShapeDtypeStruct((n_batch, dil, ln, 2 * GROUP_W_A), act_dtype),
                   jax.ShapeDtypeStruct((n * 2 * HEADS_PER_GROUP_A, LANES), F32)],
        scratch_shapes=[pltpu.VMEM((3, k, W_COLS), BF16)], compiler_params=_cparams(2),
    )(h, w, w, w, tile(g_qa), tile(g_ka))
    return q.reshape(n, GROUP_W_A), kv.reshape(n, 2 * GROUP_W_A), tok


def _bias_from_table(tab_ref, bidx, col):
    acc = jnp.zeros(bidx.shape, F32)
    for kb in range(N_BUCKETS):
        acc = jnp.where(bidx == kb, tab_ref[kb, col], acc)
    return acc


def _split_hi_lo(x):
    hi = x.astype(BF16)
    return hi, (x - hi.astype(F32)).astype(BF16)


def _attn_p_body(tab_ref, bidx_ref, q_ref, kp_ref, kc_ref, vp_ref, vc_ref, o_ref, bias_scr, *, g):
    first = (pl.program_id(0) == 0) & (pl.program_id(1) == 0)

    @pl.when(first)
    def _():
        bidx = bidx_ref[...]
        for h in range(HEADS_PER_GROUP_A):
            bias_scr[h] = _bias_from_table(tab_ref, bidx, g * HEADS_PER_GROUP_A + h)

    qb = pl.program_id(1)
    blk = q_ref.shape[0]
    q = q_ref[...]
    k = jnp.concatenate([kp_ref[...], kc_ref[...]], axis=0)
    v = jnp.concatenate([vp_ref[...], vc_ref[...]], axis=0)
    row = lax.broadcasted_iota(I32, (blk, 2 * blk), 0)
    col = lax.broadcasted_iota(I32, (blk, 2 * blk), 1)
    dist = blk + row - col
    valid = (dist >= 0) & (dist <= blk) & ((qb > 0) | (col >= blk))
    lane = lax.broadcasted_iota(I32, (blk, LANES), 1)
    lse_blk = jnp.zeros((blk, LANES), F32)
    outs = []
    for h in range(HEADS_PER_GROUP_A):
        hs = slice(h * HEAD_DIM_A, (h + 1) * HEAD_DIM_A)
        s = lax.dot_general(q[:, hs], k[:, hs], (((1,), (1,)), ((), ())), preferred_element_type=F32)
        s = jnp.where(valid, s + bias_scr[h], NEG_INF)
        m = jnp.max(s, axis=-1, keepdims=True)
        p = jnp.exp(s - m)
        l = jnp.sum(p, axis=-1, keepdims=True)
        o = jnp.dot(p.astype(BF16), v[:, hs], preferred_element_type=F32) / l
        outs.append(o)
        lse_blk = jnp.where(lane == h, m + jnp.log(l), lse_blk)
    lse_hi, lse_lo = _split_hi_lo(lse_blk)
    o_ref[...] = jnp.concatenate([jnp.concatenate(outs, axis=1).astype(BF16), lse_hi, lse_lo], axis=1)


O_EXT_W = GROUP_W_A + 2 * LANES


def _attn_prompt(q, kv, rel_bias, g, n_batch, seq):
    win, dil = DIL_GROUPS[g]
    blk = win // dil
    nb = seq // dil // blk
    n = n_batch * seq
    a = jnp.arange(blk)[:, None]
    c = jnp.arange(2 * blk)[None, :]
    bidx = _rel_bucket(jnp.maximum(blk + a - c, 0) * dil)

    def cur(s, j):
        return s * nb + j

    def prev(s, j):
        return s * nb + jnp.maximum(j - 1, 0)

    return pl.pallas_call(
        functools.partial(_attn_p_body, g=g), name=f"attn_p{g}", grid=(n_batch * dil, nb),
        in_specs=[
            pl.BlockSpec(memory_space=pltpu.SMEM),
            pl.BlockSpec((blk, 2 * blk), lambda s, j: (0, 0)),
            pl.BlockSpec((blk, GROUP_W_A), lambda s, j: (cur(s, j), 0)),
            pl.BlockSpec((blk, GROUP_W_A), lambda s, j: (prev(s, j), 0)),
            pl.BlockSpec((blk, GROUP_W_A), lambda s, j: (cur(s, j), 0)),
            pl.BlockSpec((blk, GROUP_W_A), lambda s, j: (prev(s, j), 1)),
            pl.BlockSpec((blk, GROUP_W_A), lambda s, j: (cur(s, j), 1)),
        ],
        out_specs=pl.BlockSpec((blk, O_EXT_W), lambda s, j: (cur(s, j), 0)),
        out_shape=jax.ShapeDtypeStruct((n, O_EXT_W), BF16),
        scratch_shapes=[pltpu.VMEM((HEADS_PER_GROUP_A, blk, 2 * blk), F32)],
        compiler_params=_cparams(2),
    )(rel_bias, bidx, q, kv, kv, kv, kv)


def _attn_s_body(tab_ref, bidx_ref, q_ref, kvn_ref, tokn_ref, cache_hbm, o_ref, newc_hbm,
                 raw, kv_scr, bias_scr, sem_in, sem_out, *, g, dil, lb, n_req):
    t_new = q_ref.shape[0]
    nh = HEADS_PER_GROUP_A
    n_rows = nh * t_new
    n_keys = kv_scr.shape[0]
    n_col = 2 * nh
    lb8 = lb * n_col
    n = pl.program_id(0)
    slot = n % 2

    def fetch(req, s):
        return pltpu.make_async_copy(cache_hbm.at[req], raw.at[s, pl.ds(0, lb8)], sem_in.at[s])

    def flush(req, s):
        return pltpu.make_async_copy(raw.at[s, pl.ds(t_new * n_col, lb8)], newc_hbm.at[req], sem_out.at[s])

    @pl.when(n == 0)
    def _():
        kv_scr[lb:, :] = jnp.zeros((n_keys - lb, kv_scr.shape[1]), BF16)
        bidx = bidx_ref[...]
        for h in range(nh):
            bias_scr[h * t_new:(h + 1) * t_new, :] = _bias_from_table(tab_ref, bidx, g * nh + h)
        fetch(0, 0).start()

    @pl.when(n >= 1)
    def _():
        flush(n - 1, 1 - slot).wait()

    @pl.when(n + 1 < n_req)
    def _():
        fetch(n + 1, 1 - slot).start()

    fetch(n, slot).wait()
    raw[slot, lb8:lb8 + t_new * n_col, :] = tokn_ref[...]
    flush(n, slot).start()
    new = kvn_ref[...]

    rows_per = min(lb, 256)
    for c in range(n_col):
        for r0 in range(0, lb, rows_per):
            kv_scr[r0:r0 + rows_per, c * LANES:(c + 1) * LANES] = raw[
                slot, pl.ds(r0 * n_col + c, rows_per, stride=n_col), :].astype(BF16)
    kv_scr[lb:lb + 2 * t_new, :] = jnp.concatenate([new, jnp.zeros_like(new)], axis=0).astype(BF16)

    q = q_ref[...]
    qt = jnp.concatenate([q] * nh, axis=0)
    rr = lax.broadcasted_iota(I32, (n_rows, GROUP_W_A), 0)
    cc = lax.broadcasted_iota(I32, (n_rows, GROUP_W_A), 1)
    qbd = jnp.where(cc // HEAD_DIM_A == rr // t_new, qt, 0.0).astype(BF16)

    s = lax.dot_general(qbd, kv_scr[:, :GROUP_W_A], (((1,), (1,)), ((), ())), preferred_element_type=F32)
    row = lax.broadcasted_iota(I32, (n_rows, n_keys), 0)
    col = lax.broadcasted_iota(I32, (n_rows, n_keys), 1)
    delta = lb + (row & (t_new - 1)) - col
    valid = (delta >= 0) & (delta <= lb) & ((delta & (dil - 1)) == 0)
    s = jnp.where(valid, s + bias_scr[...], NEG_INF)
    m = jnp.max(s, axis=-1, keepdims=True)
    p = jnp.exp(s - m)
    l = jnp.sum(p, axis=-1, keepdims=True)
    o = jnp.dot(p.astype(BF16), kv_scr[:, GROUP_W_A:], preferred_element_type=F32) / l
    lse = m + jnp.log(l)
    lane = lax.broadcasted_iota(I32, (t_new, LANES), 1)
    lse_blk = jnp.zeros((t_new, LANES), F32)
    outs = []
    for h in range(nh):
        outs.append(o[h * t_new:(h + 1) * t_new, h * HEAD_DIM_A:(h + 1) * HEAD_DIM_A])
        lse_blk = jnp.where(lane == h, lse[h * t_new:(h + 1) * t_new, :], lse_blk)
    o_ref[...] = jnp.concatenate(outs + [lse_blk, jnp.zeros_like(lse_blk)], axis=1)

    @pl.when(n == n_req - 1)
    def _():
        flush(n, slot).wait()


def _attn_sample(q, kv_new, tok_new, cache, rel_bias, g, t_new):
    win, dil = DIL_GROUPS[g]
    n_col = 2 * HEADS_PER_GROUP_A
    n_req, lb = cache.shape[0], cache.shape[1] // n_col
    assert t_new & (t_new - 1) == 0 and dil & (dil - 1) == 0 and lb == win
    n_keys = lb + LANES
    t = jnp.arange(t_new)[:, None]
    c = jnp.arange(n_keys)[None, :]
    bidx = _rel_bucket(jnp.clip(lb + t - c, 0, lb))
    return pl.pallas_call(
        functools.partial(_attn_s_body, g=g, dil=dil, lb=lb, n_req=n_req), name=f"attn_s{g}", grid=(n_req,),
        in_specs=[
            pl.BlockSpec(memory_space=pltpu.SMEM),
            pl.BlockSpec((t_new, n_keys), lambda n: (0, 0)),
            pl.BlockSpec((t_new, GROUP_W_A), lambda n: (n, 0)),
            pl.BlockSpec((t_new, 2 * GROUP_W_A), lambda n: (n, 0)),
            pl.BlockSpec((t_new * n_col, LANES), lambda n: (n, 0)),
            pl.BlockSpec(memory_space=pl.ANY),
        ],
        out_specs=[pl.BlockSpec((t_new, O_EXT_W), lambda n: (n, 0)), pl.BlockSpec(memory_space=pl.ANY)],
        out_shape=[jax.ShapeDtypeStruct((n_req * t_new, O_EXT_W), F32), jax.ShapeDtypeStruct(cache.shape, F32)],
        scratch_shapes=[pltpu.VMEM((2, (lb + t_new) * n_col, LANES), F32),
                        pltpu.VMEM((n_keys, 2 * GROUP_W_A), BF16),
                        pltpu.VMEM((HEADS_PER_GROUP_A * t_new, n_keys), F32),
                        pltpu.SemaphoreType.DMA((2,)), pltpu.SemaphoreType.DMA((2,))],
        compiler_params=_cparams(1),
    )(rel_bias, bidx, q, kv_new, tok_new, cache)


def _gmlp_body(u_ref, v_ref, w_ref, b_ref, o_ref, *, period):
    c = w_ref.shape[1]
    nch = u_ref.shape[0] // c
    gd = u_ref.shape[1] // N_GROUPS_B
    i = lax.broadcasted_iota(I32, (c, c), 0)
    j = lax.broadcasted_iota(I32, (c, c), 1)
    mask = (j <= i) & ((i // period) == (j // period))
    b = b_ref[...]
    for g in range(N_GROUPS_B):
        gs = slice(g * gd, (g + 1) * gd)
        wg = jnp.where(mask, w_ref[g], 0.0).astype(BF16)
        vg = [v_ref[ch * c:(ch + 1) * c, gs].astype(BF16) for ch in range(nch)]
        vg = vg[0] if nch == 1 else jnp.concatenate(vg, axis=1)
        sg = jnp.dot(wg, vg, preferred_element_type=F32) + b[:, g:g + 1]
        for ch in range(nch):
            u = u_ref[ch * c:(ch + 1) * c, gs].astype(F32)
            o_ref[ch * c:(ch + 1) * c, gs] = (u * sg[:, ch * gd:(ch + 1) * gd]).astype(o_ref.dtype)


def _gmlp(u, v, w, b, tm, period):
    m, wb = u.shape
    c = w.shape[1]
    return pl.pallas_call(
        functools.partial(_gmlp_body, period=period), name="gmlp", grid=(m // tm,),
        in_specs=[pl.BlockSpec((tm, wb), lambda i: (i, 0)), pl.BlockSpec((tm, wb), lambda i: (i, 0)),
                  pl.BlockSpec((N_GROUPS_B, c, c), lambda i: (0, 0, 0)),
                  pl.BlockSpec((c, N_GROUPS_B), lambda i: (0, 0))],
        out_specs=pl.BlockSpec((tm, wb), lambda i: (i, 0)),
        out_shape=jax.ShapeDtypeStruct((m, wb), BF16), compiler_params=_cparams(1),
    )(u, v, w, b)


def _gmlp_new_body(u_ref, v_ref, w_ref, b_ref, o_ref):
    t = w_ref.shape[0]
    n = u_ref.shape[0] // t
    width = u_ref.shape[1]
    v = v_ref[...].astype(F32).reshape(n, t, width)
    sg = jnp.broadcast_to(b_ref[...][None], (n, t, width))
    p_row = lax.broadcasted_iota(I32, (t, width), 0)
    for q in range(t):
        sg = sg + jnp.where(p_row >= q, w_ref[q], 0.0)[None] * v[:, q:q + 1, :]
    o_ref[...] = (u_ref[...].astype(F32).reshape(n, t, width) * sg).reshape(n * t, width).astype(o_ref.dtype)


def _gmlp_new(u, v, w_s, b_s, t):
    m, wb = u.shape
    gd = wb // N_GROUPS_B
    wq = jnp.repeat(jnp.transpose(w_s[:, :t, :t], (2, 1, 0)), gd, axis=-1)
    bq = jnp.repeat(b_s[:, :t].T, gd, axis=-1)
    whole = lambda a: pl.BlockSpec(a.shape, lambda i: (0,) * a.ndim)
    return pl.pallas_call(
        _gmlp_new_body, name="gmlp_new", grid=(1,),
        in_specs=[whole(u), whole(v), whole(wq), whole(bq)], out_specs=whole(u),
        out_shape=jax.ShapeDtypeStruct((m, wb), BF16), compiler_params=_cparams(1),
    )(u, v, wq, bq)


def _memattn_body(q_ref, kv_ref, o_ref):
    wm = N_HEADS_M * HEAD_DIM_M
    q = q_ref[...].astype(BF16)
    outs = []
    for h in range(N_HEADS_M):
        hs = slice(h * HEAD_DIM_M, (h + 1) * HEAD_DIM_M)
        k = kv_ref[:, hs].astype(BF16)
        v = kv_ref[:, wm + h * HEAD_DIM_M:wm + (h + 1) * HEAD_DIM_M].astype(BF16)
        s = lax.dot_general(q[:, hs], k, (((1,), (1,)), ((), ())), preferred_element_type=F32)
        m = jnp.max(s, axis=-1, keepdims=True)
        p = jnp.exp(s - m)
        l = jnp.sum(p, axis=-1, keepdims=True)
        outs.append(jnp.dot(p.astype(BF16), v, preferred_element_type=F32) / l)
    o_ref[...] = jnp.concatenate(outs, axis=1).astype(o_ref.dtype)


def _memattn(q, kv, tq, out_dtype):
    m, wm = q.shape
    n = kv.shape[0]
    per = m // n // tq
    kv_spec = pl.BlockSpec((None,) + kv.shape[1:], lambda i: (i // per,) + (0,) * (kv.ndim - 1))
    return pl.pallas_call(
        _memattn_body, name="memattn", grid=(m // tq,),
        in_specs=[pl.BlockSpec((tq, wm), lambda i: (i, 0)), kv_spec],
        out_specs=pl.BlockSpec((tq, wm), lambda i: (i, 0)),
        out_shape=jax.ShapeDtypeStruct((m, wm), out_dtype), compiler_params=_cparams(1),
    )(q, kv)


def _mix_body(gt_ref, o0_ref, o1_ref, o2_ref, ob_ref, om_ref, wpa_ref, wpb_ref, wpm_ref, z_ref, *, dils):
    d = z_ref.shape[1]
    tm = z_ref.shape[0]
    os_, ls = [], []
    for o_ref, dil in zip((o0_ref, o1_ref, o2_ref), dils):
        if dil == 1:
            x = o_ref[...].astype(F32)
        else:
            rows = jnp.concatenate([o_ref[r] for r in range(dil)], axis=0)
            x = jnp.dot(_perm_matrix(tm, dil, inverse=True), rows, preferred_element_type=F32)
        os_.append(x[:, :GROUP_W_A])
        ls.append(x[:, GROUP_W_A:GROUP_W_A + LANES] + x[:, GROUP_W_A + LANES:])
    l0, l1, l2 = ls
    mx = jnp.maximum(jnp.maximum(l0, l1), l2)
    e0, e1, e2 = jnp.exp(l0 - mx), jnp.exp(l1 - mx), jnp.exp(l2 - mx)
    den = e0 + e1 + e2
    w0, w1, w2 = e0 / den, e1 / den, e2 / den
    cols = []
    for h in range(HEADS_PER_GROUP_A):
        hs = slice(h * HEAD_DIM_A, (h + 1) * HEAD_DIM_A)
        cols.append(w0[:, h:h + 1] * os_[0][:, hs] + w1[:, h:h + 1] * os_[1][:, hs] + w2[:, h:h + 1] * os_[2][:, hs])
    oa = jnp.concatenate(cols, axis=1).astype(BF16)
    pa = jnp.dot(oa, wpa_ref[...], preferred_element_type=F32)
    pb = jnp.dot(ob_ref[...].astype(BF16), wpb_ref[...], preferred_element_type=F32)
    pm = jnp.dot(om_ref[...].astype(BF16), wpm_ref[...], preferred_element_type=F32)
    z = (gt_ref[:, 0:d].astype(F32) * pa + gt_ref[:, d:2 * d].astype(F32) * pb
         + gt_ref[:, 2 * d:3 * d].astype(F32) * pm)
    z_ref[...] = z.astype(z_ref.dtype)


def _mix(gates, o_list, dils, seq, ob, om, wpa, wpb, wpm, tm):
    m = gates.shape[0]
    d = wpa.shape[1]
    mt = seq // tm if any(dl > 1 for dl in dils) else 1

    def rows(width):
        return pl.BlockSpec((tm, width), lambda i: (i, 0))

    def whole(a):
        return pl.BlockSpec(a.shape, lambda i: (0, 0))

    o_specs, o_args = [], []
    for o, dil in zip(o_list, dils):
        if dil == 1:
            o_specs.append(rows(O_EXT_W))
            o_args.append(o)
        else:
            assert seq % tm == 0 and (tm // dil) % 16 == 0
            o_specs.append(pl.BlockSpec((None, dil, tm // dil, O_EXT_W), lambda i: (i // mt, 0, i % mt, 0)))
            o_args.append(o.reshape(m // seq, dil, seq // dil, O_EXT_W))
    return pl.pallas_call(
        functools.partial(_mix_body, dils=tuple(dils)), name="mix", grid=(m // tm,),
        in_specs=[rows(gates.shape[1])] + o_specs
                 + [rows(ob.shape[1]), rows(om.shape[1]), whole(wpa), whole(wpb), whole(wpm)],
        out_specs=rows(d),
        out_shape=jax.ShapeDtypeStruct((m, d), BF16), compiler_params=_cparams(1),
    )(gates, *o_args, ob, om, wpa, wpb, wpm)


def _route(logits):
    lane = lax.broadcasted_iota(I32, logits.shape, 1)
    lane_f = lane.astype(F32)
    is_g = lane < N_EXPERT_GROUPS
    gmax = jnp.max(jnp.where(is_g, logits, -jnp.inf), axis=1, keepdims=True)
    gsel = jnp.min(jnp.where(is_g & (logits == gmax), lane_f, float(LANES)), axis=1, keepdims=True).astype(I32)
    gden = jnp.sum(jnp.where(is_g, jnp.exp(logits - gmax), 0.0), axis=1, keepdims=True)
    pg = 1.0 / gden
    e_lane = lane - N_EXPERT_GROUPS
    in_grp = (e_lane >= 0) & (e_lane < N_EXPERTS) & ((e_lane // EXPERTS_PER_GROUP) == gsel)
    m1 = jnp.max(jnp.where(in_grp, logits, -jnp.inf), axis=1, keepdims=True)
    i1 = jnp.min(jnp.where(in_grp & (logits == m1), lane_f, float(LANES)), axis=1, keepdims=True).astype(I32)
    rest = in_grp & (lane != i1)
    m2 = jnp.max(jnp.where(rest, logits, -jnp.inf), axis=1, keepdims=True)
    i2 = jnp.min(jnp.where(rest & (logits == m2), lane_f, float(LANES)), axis=1, keepdims=True).astype(I32)
    e2 = jnp.exp(m2 - m1)
    w1 = pg / (1.0 + e2)
    w2 = pg * e2 / (1.0 + e2)
    eid = jnp.where(lane == 0, i1 - N_EXPERT_GROUPS, jnp.where(lane == 1, i2 - N_EXPERT_GROUPS, 0))
    wts = jnp.where(lane == 0, w1, jnp.where(lane == 1, w2, 0.0))
    return eid, wts


def _resid_body(x_ref, z_ref, wo_ref, gf_ref, wr_ref, br_ref, xmid_ref, hpk_ref, eid_ref, wts_ref):
    xm = x_ref[...] + jnp.dot(z_ref[...], wo_ref[...], preferred_element_type=F32)
    xmid_ref[...] = xm
    hf = _rms(xm, gf_ref[...])
    wr = wr_ref[...]
    wr_hi = wr.astype(BF16)
    wr_lo = (wr - wr_hi.astype(F32)).astype(BF16)
    hf_hi = hf.astype(BF16)
    hf_lo = (hf - hf_hi.astype(F32)).astype(BF16)
    logits = (jnp.dot(hf_hi, wr_hi, preferred_element_type=F32) + jnp.dot(hf_hi, wr_lo, preferred_element_type=F32)
              + jnp.dot(hf_lo, wr_hi, preferred_element_type=F32) + br_ref[...])
    eid, wts = _route(logits)
    eid_ref[...] = eid
    wts_ref[...] = wts
    _store_token_tiles(hpk_ref, hf)


def _resid(x2d, z, wo, g_ffn, w_r, b_r, tm):
    m, d = x2d.shape

    def rows(width):
        return pl.BlockSpec((tm, width), lambda i: (i, 0))

    def whole(a):
        return pl.BlockSpec(a.shape, lambda i: (0, 0))

    gf = g_ffn.reshape(1, d)
    nt = d // 2 // LANES
    return pl.pallas_call(
        _resid_body, name="resid", grid=(m // tm,),
        in_specs=[rows(d), rows(d), whole(wo), whole(gf), whole(w_r), whole(b_r)],
        out_specs=[rows(d), pl.BlockSpec((tm * nt, LANES), lambda i: (i, 0)), rows(LANES), rows(LANES)],
        out_shape=[jax.ShapeDtypeStruct((m, d), F32), jax.ShapeDtypeStruct((m * nt, LANES), U32),
                   jax.ShapeDtypeStruct((m, LANES), I32), jax.ShapeDtypeStruct((m, LANES), F32)],
        compiler_params=_cparams(1),
    )(x2d, z, wo, gf, w_r, b_r)


TOKEN_TILE = 8
GATHER_UNROLL = 8


def _gather_rows(idx_s, slot, src_hbm, dst, sem, n):
    def body(i, carry):
        src = pl.multiple_of(idx_s[slot, i], TOKEN_TILE)
        dst_row = pl.multiple_of(i * TOKEN_TILE, TOKEN_TILE)
        pltpu.make_async_copy(src_hbm.at[pl.ds(src, TOKEN_TILE)], dst.at[pl.ds(dst_row, TOKEN_TILE)], sem).start()
        return carry
    lax.fori_loop(0, n, body, 0, unroll=GATHER_UNROLL)


def _dispatch_body(idx_hbm, h_hbm, xs_hbm, idx_s, isem, buf, sem, osem, *, n):
    j = pl.program_id(0)
    nbuf = buf.shape[0]
    tiles = buf.shape[1]
    rows = tiles // TOKEN_TILE

    def out_copy(blk):
        s = blk % nbuf
        return pltpu.make_async_copy(buf.at[s], xs_hbm.at[pl.ds(pl.multiple_of(blk * tiles, tiles), tiles)],
                                     osem.at[s])

    def idx_copy(blk):
        return pltpu.make_async_copy(idx_hbm.at[blk], idx_s.at[blk % nbuf], isem.at[blk % nbuf])

    def gather(blk):
        _gather_rows(idx_s, blk % nbuf, h_hbm, buf.at[blk % nbuf], sem.at[blk % nbuf], rows)

    @pl.when(j == 0)
    def _():
        for b in range(3):
            idx_copy(b).start()
        for b in range(2):
            idx_copy(b).wait()
            gather(b)

    @pl.when(j >= 2)
    def _():
        out_copy(j - 2).wait()

    @pl.when(j + 3 < n)
    def _():
        idx_copy(j + 3).start()

    @pl.when(j + 2 < n)
    def _():
        idx_copy(j + 2).wait()
        gather(j + 2)

    s = j % nbuf
    pltpu.make_async_copy(h_hbm.at[pl.ds(0, tiles)], buf.at[s], sem.at[s]).wait()
    out_copy(j).start()

    @pl.when(j == n - 1)
    def _():
        out_copy(j - 1).wait()
        out_copy(j).wait()


def _dispatch(hpk, src_tok, n_blocks):
    rows = MOE_ROWS
    nbuf = 4
    assert n_blocks >= 3
    idx = (src_tok * TOKEN_TILE).reshape(n_blocks, rows)
    hbm = pl.BlockSpec(memory_space=pl.ANY)
    return pl.pallas_call(
        functools.partial(_dispatch_body, n=n_blocks), name="dispatch", grid=(n_blocks,),
        in_specs=[hbm, hbm], out_specs=hbm,
        out_shape=jax.ShapeDtypeStruct((n_blocks * rows * TOKEN_TILE, LANES), U32),
        scratch_shapes=[pltpu.SMEM((nbuf, rows), I32), pltpu.SemaphoreType.DMA((nbuf,)),
                        pltpu.VMEM((nbuf, rows * TOKEN_TILE, LANES), U32), pltpu.SemaphoreType.DMA((nbuf,)),
                        pltpu.SemaphoreType.DMA((nbuf,))],
        compiler_params=_cparams(1),
    )(idx, hpk)


def _ffn_body(blk_e_ref, nused_ref, eord_ref, enext_ref, x_ref, wg_hbm, wu_hbm, wd_hbm, y_ref,
              wg_f, wu_f, wd_f, wsem, wg_s, wu_s, wd_s):
    j = pl.program_id(0)
    nused = nused_ref[0]
    rows = x_ref.shape[0] // TOKEN_TILE
    e = blk_e_ref[j]
    new_expert = (j == 0) | (e != blk_e_ref[jnp.maximum(j - 1, 0)])
    ws = eord_ref[j] % 2

    def fetch_w(expert, s):
        return [pltpu.make_async_copy(src.at[expert], dst.at[s], wsem.at[s, i])
                for i, (src, dst) in enumerate(((wg_hbm, wg_f), (wu_hbm, wu_f), (wd_hbm, wd_f)))]

    @pl.when(j == 0)
    def _():
        for c in fetch_w(e, ws):
            c.start()

    @pl.when(new_expert)
    def _():
        nxt = enext_ref[j]

        @pl.when(nxt >= 0)
        def _():
            for c in fetch_w(nxt, 1 - ws):
                c.start()

        for c in fetch_w(e, ws):
            c.wait()
        wg_s[...] = wg_f[ws].astype(BF16)
        wu_s[...] = wu_f[ws].astype(BF16)
        wd_s[...] = wd_f[ws].astype(BF16)

    @pl.when(j < nused)
    def _():
        x = _load_token_tiles(x_ref, None, 0, rows, TOKEN_TILE).astype(BF16)
        a = jnp.dot(x, wg_s[...], preferred_element_type=F32)
        b = jnp.dot(x, wu_s[...], preferred_element_type=F32)
        hm = (a * jax.nn.sigmoid(a) * b).astype(BF16)
        _store_token_tiles(y_ref, jnp.dot(hm, wd_s[...], preferred_element_type=F32))

    @pl.when(j >= nused)
    def _():
        y_ref[...] = jnp.zeros(y_ref.shape, y_ref.dtype)


def _ffn(xs, blk_e, nused, eord, enext, wg, wu, wd):
    n_blocks = blk_e.shape[0]
    tiles = MOE_ROWS * TOKEN_TILE
    d, de = wg.shape[1], wg.shape[2]
    assert d // 2 == TOKEN_TILE * LANES
    hbm = pl.BlockSpec(memory_space=pl.ANY)
    grid_spec = pltpu.PrefetchScalarGridSpec(
        num_scalar_prefetch=4, grid=(n_blocks,),
        in_specs=[pl.BlockSpec((tiles, LANES), lambda j, be, nu, *_: (jnp.minimum(j, nu[0] - 1), 0)),
                  hbm, hbm, hbm],
        out_specs=pl.BlockSpec((tiles, LANES), lambda j, *_: (j, 0)),
        scratch_shapes=[pltpu.VMEM((2, d, de), wg.dtype), pltpu.VMEM((2, d, de), wu.dtype),
                        pltpu.VMEM((2, de, d), wd.dtype), pltpu.SemaphoreType.DMA((2, 3)),
                        pltpu.VMEM((d, de), BF16), pltpu.VMEM((d, de), BF16), pltpu.VMEM((de, d), BF16)],
    )
    return pl.pallas_call(
        _ffn_body, name="ffn", grid_spec=grid_spec,
        out_shape=jax.ShapeDtypeStruct((n_blocks * tiles, LANES), U32), compiler_params=_cparams(1),
    )(blk_e, nused, eord, enext, xs, wg, wu, wd)


def _combine_body(idx_hbm, x_ref, w_ref, y_hbm, o_ref, idx_s, isem, buf, sem, *, n):
    i = pl.program_id(0)
    slot = i % 2
    rows = buf.shape[1] // TOKEN_TILE
    tm = rows // TOP_K

    def idx_copy(t):
        return pltpu.make_async_copy(idx_hbm.at[t], idx_s.at[t % 2], isem.at[t % 2])

    def gather(t):
        _gather_rows(idx_s, t % 2, y_hbm, buf.at[t % 2], sem.at[t % 2], rows)

    @pl.when(i == 0)
    def _():
        idx_copy(0).start()
        if n > 1:
            idx_copy(1).start()
        idx_copy(0).wait()
        gather(0)

    if n > 1:
        @pl.when(i + 1 < n)
        def _():
            idx_copy(i + 1).wait()
            gather(i + 1)

    if n > 2:
        @pl.when(i + 2 < n)
        def _():
            idx_copy(i + 2).start()

    pltpu.make_async_copy(y_hbm.at[pl.ds(0, rows * TOKEN_TILE)], buf.at[slot], sem.at[slot]).wait()
    w = w_ref[...]
    y0 = _load_token_tiles(buf, slot, 0, tm, TOKEN_TILE)
    y1 = _load_token_tiles(buf, slot, tm, tm, TOKEN_TILE)
    o_ref[...] = x_ref[...] + (w[:, 0:1] * y0 + w[:, 1:2] * y1)


def _combine(xmid, wts, dest, ypk, tm):
    m, d = xmid.shape
    nt = m // tm
    assert d // 2 == TOKEN_TILE * LANES
    idx = (dest * TOKEN_TILE).reshape(nt, tm, TOP_K).transpose(0, 2, 1).reshape(nt, TOP_K * tm)
    hbm = pl.BlockSpec(memory_space=pl.ANY)
    return pl.pallas_call(
        functools.partial(_combine_body, n=nt), name="combine", grid=(nt,),
        in_specs=[hbm, pl.BlockSpec((tm, d), lambda i: (i, 0)), pl.BlockSpec((tm, LANES), lambda i: (i, 0)), hbm],
        out_specs=pl.BlockSpec((tm, d), lambda i: (i, 0)),
        out_shape=jax.ShapeDtypeStruct((m, d), F32),
        scratch_shapes=[pltpu.SMEM((2, TOP_K * tm), I32), pltpu.SemaphoreType.DMA((2,)),
                        pltpu.VMEM((2, TOP_K * tm * TOKEN_TILE, LANES), U32), pltpu.SemaphoreType.DMA((2,))],
        compiler_params=_cparams(1),
    )(idx, xmid, wts, ypk)


def _moe_plan(eid2):
    n_slot = eid2.shape[0] * TOP_K
    eid = eid2.reshape(n_slot)
    onehot = (eid[:, None] == jnp.arange(N_EXPERTS, dtype=I32)[None, :]).astype(I32)
    csum = jnp.cumsum(onehot, axis=0)
    rank = jnp.take_along_axis(csum, eid[:, None], axis=1)[:, 0] - 1
    counts = csum[-1]
    padded = (counts + MOE_ROWS - 1) // MOE_ROWS * MOE_ROWS
    pend = jnp.cumsum(padded)
    pstart = pend - padded
    dest = pstart[eid] + rank
    n_blocks = -(-n_slot // MOE_ROWS) + N_EXPERTS
    nused = (pend[-1] // MOE_ROWS).astype(I32)
    blk = jnp.minimum(jnp.arange(n_blocks, dtype=I32), nused - 1)
    blk_e = jnp.minimum(jnp.searchsorted(pend, blk * MOE_ROWS, side="right"), N_EXPERTS - 1).astype(I32)
    n_rows = n_blocks * MOE_ROWS
    src_tok = (jnp.arange(n_rows, dtype=I32) % (n_slot // TOP_K)).at[dest].set(jnp.arange(n_slot, dtype=I32) // TOP_K)
    has_rows = counts > 0
    ord_e = jnp.cumsum(has_rows.astype(I32)) - 1
    ids = jnp.arange(N_EXPERTS, dtype=I32)
    later = has_rows[None, :] & (ids[None, :] > ids[:, None])
    next_e = jnp.where(later.any(axis=1), jnp.argmax(later, axis=1), -1).astype(I32)
    return dest.astype(I32), src_tok, blk_e, nused.reshape(1), ord_e[blk_e].astype(I32), next_e[blk_e]


def _mixer(x2d, tm, p, attn_fn, gmlp_fn, mem_fn):
    h = _prep(x2d, p["g_mix"], tm)
    qdt = p["q_dtype"]
    wi, sec = p["w_in"], p["sec"]
    o_list, dils, seq = attn_fn(h)
    ub = _proj(h, wi, (sec["u"], 1, 1, sec["vb"] - sec["u"]), tm, "none", BF16, name="proj_ub")
    n_vb = sec["qm"] - sec["vb"]
    vb = _proj(h, wi, (sec["vb"], n_vb, n_vb, 1), tm, "norm", p["vb_dtype"], p["g_vb"], n_vb * W_COLS,
               name="proj_vb")
    qm = _proj(h, wi, (sec["qm"], 1, 1, sec["gt"] - sec["qm"]), tm, "norm", qdt, p["g_qm"], HEAD_DIM_M,
               HEAD_DIM_M ** -0.5, name="proj_qm")
    gates = _proj(h, wi, (sec["gt"], 2, 2, (sec["end"] - sec["gt"]) // 2), tm, "sigmoid", BF16, name="proj_gates")
    ob = gmlp_fn(ub, vb)
    om = mem_fn(qm)
    z = _mix(gates, o_list, dils, seq, ob, om, p["w_pa"], p["w_pb"], p["w_pm"], min(tm, 256))
    xmid, hpk, eid, wts = _resid(x2d, z, p["w_o"], p["g_ffn"], p["w_r"], p["b_r"], min(tm, 256))
    return vb, xmid, hpk, eid, wts


def kernel(x_prompt, x_sample, mem_prompt, cache_a0_kv, cache_a1_kv, cache_a2_kv, cache_mem_kv, rel_bias, g_mix, w_in, g_qa, g_ka, w_pa, g_vb, w_s, b_s, w_pb, g_mem, w_mk, w_mv, g_qm, g_km, w_pm, w_o, g_ffn, w_rg, b_rg, w_re, b_re, w_gate, w_up, w_down):
    n_b, seq, d = x_prompt.shape
    n_s, t_s, _ = x_sample.shape
    depth = w_in.shape[0]
    assert depth == 1
    l = 0
    caches = (cache_a0_kv, cache_a1_kv, cache_a2_kv)
    n_g = len(DIL_GROUPS)
    wb = w_pb.shape[1]
    wm = N_HEADS_M * HEAD_DIM_M

    offs = [0, WIDTH_A, 2 * WIDTH_A, 3 * WIDTH_A, 3 * WIDTH_A + wb, 3 * WIDTH_A + 2 * wb,
            3 * WIDTH_A + 2 * wb + wm, w_in.shape[2]]
    assert all(o % W_COLS == 0 for o in offs) and (offs[7] - offs[6]) % (2 * W_COLS) == 0
    sec = dict(zip(("q", "k", "v", "u", "vb", "qm", "gt", "end"), (o // W_COLS for o in offs)))
    p = {
        "g_mix": g_mix[l], "g_qa": g_qa[l], "g_ka": g_ka[l], "g_vb": g_vb[l], "g_qm": g_qm[l], "g_ffn": g_ffn[l],
        "w_in": w_in[l], "sec": sec,
        "w_pa": w_pa[l].astype(BF16), "w_pb": w_pb[l].astype(BF16), "w_pm": w_pm[l].astype(BF16),
        "w_o": w_o[l].astype(BF16),
    }
    n_r = N_EXPERT_GROUPS + N_EXPERTS
    p["w_r"] = jnp.concatenate([w_rg[l], w_re[l], jnp.zeros((d, LANES - n_r), F32)], axis=1)
    p["b_r"] = jnp.concatenate([b_rg[l], b_re[l], jnp.zeros((LANES - n_r,), F32)]).reshape(1, LANES)

    n_mem = mem_prompt.shape[1]
    h_mem = _prep(mem_prompt.reshape(n_b * n_mem, d), g_mem[l], 256)
    w_mkv = jnp.concatenate([w_mk[l], w_mv[l]], axis=1)
    n_mw = wm // W_COLS
    mkv = _proj(h_mem, w_mkv, (0, n_mw, n_mw, 2), 256, "norm_first", F32, g_km[l], HEAD_DIM_M, name="proj_mkv")
    new_mem_p = mkv.reshape(1, n_b, n_mem, 2, N_HEADS_M, HEAD_DIM_M)

    ps = dict(p, q_dtype=F32, vb_dtype=F32)
    m_s = n_s * t_s
    n_col = 2 * HEADS_PER_GROUP_A
    new_s = []

    def attn_sample(h):
        o_list = []
        for g in range(n_g):
            q, kv, tok = _proj_qkv(h, w_in[l], sec, g, g_qa[l], g_ka[l], 1, m_s, m_s, 1, act_dtype=F32)
            cache3 = caches[g][l].reshape(n_s, caches[g].shape[2] * n_col, LANES)
            o, newc = _attn_sample(q, kv, tok, cache3, rel_bias, g, t_s)
            o_list.append(o)
            new_s.append(newc.reshape(caches[g][l:l + 1].shape))
        return o_list, [1] * n_g, t_s

    vb_s, xmid_s, hpk_s, eid_s, wts_s = _mixer(
        x_sample.reshape(m_s, d), m_s, ps, attn_sample,
        lambda u, v: _gmlp_new(u, v, w_s[l], b_s[l], t_s),
        lambda qm: _memattn(qm, cache_mem_kv[l].reshape(n_s, n_mem, 2 * wm), t_s, F32))
    new_vb_s = vb_s.reshape(1, n_s, t_s, wb)

    pp = dict(p, q_dtype=BF16, vb_dtype=BF16)
    new_p = []

    def attn_prompt(h):
        o_list = []
        for g, (win, dil) in enumerate(DIL_GROUPS):
            q, kv, tok = _proj_qkv(h, w_in[l], sec, g, g_qa[l], g_ka[l], n_b, seq, 512, dil)
            o_list.append(_attn_prompt(q, kv, rel_bias, g, n_b, seq))
            keep = min(win, seq)
            kv5 = tok.reshape(n_b, seq, 2, HEADS_PER_GROUP_A, HEAD_DIM_A)
            new_p.append((kv5 if keep == seq else kv5[:, seq - keep:])[None])
        return o_list, [dl for _, dl in DIL_GROUPS], seq

    _, xmid_p, hpk_p, eid_p, wts_p = _mixer(
        x_prompt.reshape(n_b * seq, d), 512, pp, attn_prompt,
        lambda u, v: _gmlp(u, v, w_s[l], b_s[l].T, 512, CHUNK),
        lambda qm: _memattn(qm, mkv.reshape(n_b, n_mem, 2 * wm), 256, BF16))

    n_p = n_b * seq
    hpk = jnp.concatenate([hpk_p, hpk_s], axis=0)
    eid2 = jnp.concatenate([eid_p[:, :TOP_K], eid_s[:, :TOP_K]], axis=0)
    dest, src_tok, blk_e, nused, eord, enext = _moe_plan(eid2)
    xs = _dispatch(hpk, src_tok, blk_e.shape[0])
    ypk = _ffn(xs, blk_e, nused, eord, enext, w_gate[l], w_up[l], w_down[l])
    dest2 = dest.reshape(-1, TOP_K)
    y_p = _combine(xmid_p, wts_p, dest2[:n_p], ypk, 256)
    y_s = _combine(xmid_s, wts_s, dest2[n_p:], ypk, m_s)

    return (y_p.reshape(n_b, seq, d), y_s.reshape(n_s, t_s, d), new_p[0], new_p[1], new_p[2], new_mem_p,
            new_s[0], new_s[1], new_s[2], new_vb_s)
```

```python
import functools
import math

import jax
import jax.numpy as jnp
from jax import lax
from jax.experimental import pallas as pl
from jax.experimental.pallas import tpu as pltpu

F32 = jnp.float32
BF16 = jnp.bfloat16
I32 = jnp.int32
U32 = jnp.uint32

EPS = 1e-6
NEG_INF = -1e30

HEAD_DIM_A = 128
HEADS_PER_GROUP_A = 4
DIL_GROUPS = ((128, 1), (512, 4), (2048, 16))
GROUP_W_A = HEADS_PER_GROUP_A * HEAD_DIM_A
WIDTH_A = len(DIL_GROUPS) * GROUP_W_A
CHUNK = 128
N_GROUPS_B = 8
N_HEADS_M = 4
HEAD_DIM_M = 256
N_BUCKETS = 32
MAX_EXACT = N_BUCKETS // 2
MAX_DISTANCE = 2048
N_EXPERT_GROUPS = 4
EXPERTS_PER_GROUP = 8
N_EXPERTS = N_EXPERT_GROUPS * EXPERTS_PER_GROUP
TOP_K = 2
LANES = 128
MOE_ROWS = 256
VMEM_LIMIT = 56 * 1024 * 1024


def _cparams(n_grid, vmem=VMEM_LIMIT):
    return pltpu.CompilerParams(dimension_semantics=("arbitrary",) * n_grid, vmem_limit_bytes=vmem)


def _rms(x, g):
    return x * lax.rsqrt(jnp.mean(x * x, axis=-1, keepdims=True) + EPS) * g


def _pack_bf16_pair(x):
    n = x.shape[1] // 2
    lo = lax.bitcast_convert_type(x[:, :n].astype(BF16).astype(F32), U32)
    hi = lax.bitcast_convert_type(x[:, n:].astype(BF16).astype(F32), U32)
    return (hi & jnp.uint32(0xFFFF0000)) | (lo >> 16)


def _unpack_bf16_pair(w):
    lo = lax.bitcast_convert_type(w << 16, F32)
    hi = lax.bitcast_convert_type(w & jnp.uint32(0xFFFF0000), F32)
    return lo, hi


def _store_token_tiles(ref, x):
    w = _pack_bf16_pair(x)
    m, n = w.shape
    nt = n // LANES
    for c in range(nt):
        ref[pl.ds(c, m, stride=nt), :] = w[:, c * LANES:(c + 1) * LANES]


def _load_token_tiles(ref, lead, row0, m, nt):
    los, his = [], []
    view = ref if lead is None else ref.at[lead]
    for c in range(nt):
        lo, hi = _unpack_bf16_pair(view[pl.ds(row0 * nt + c, m, stride=nt), :])
        los.append(lo)
        his.append(hi)
    return jnp.concatenate(los + his, axis=1)


def _rel_bucket(dist):
    d = jnp.maximum(dist, 1).astype(F32)
    large = MAX_EXACT + (jnp.log(d / MAX_EXACT) / math.log(MAX_DISTANCE / MAX_EXACT)
                         * (N_BUCKETS - MAX_EXACT)).astype(I32)
    return jnp.where(dist < MAX_EXACT, dist, jnp.minimum(large, N_BUCKETS - 1)).astype(I32)


def _prep_body(x_ref, g_ref, o_ref):
    o_ref[...] = _rms(x_ref[...], g_ref[...]).astype(o_ref.dtype)


def _prep(x2d, g, tm):
    m, d = x2d.shape
    return pl.pallas_call(
        _prep_body, name="prep", grid=(m // tm,),
        in_specs=[pl.BlockSpec((tm, d), lambda i: (i, 0)), pl.BlockSpec((1, d), lambda i: (0, 0))],
        out_specs=pl.BlockSpec((tm, d), lambda i: (i, 0)),
        out_shape=jax.ShapeDtypeStruct((m, d), BF16), compiler_params=_cparams(1),
    )(x2d, g.reshape(1, d))


def _headnorm(acc, g, hd, scale):
    outs = []
    for j in range(acc.shape[1] // hd):
        sl = acc[:, j * hd:(j + 1) * hd]
        outs.append(sl * lax.rsqrt(jnp.mean(sl * sl, axis=-1, keepdims=True) + EPS))
    y = outs[0] if len(outs) == 1 else jnp.concatenate(outs, axis=1)
    y = y * g
    return y * scale if scale != 1.0 else y


def _proj_body(h_ref, *refs, n_w, mode, hd, scale):
    w_refs, (g_ref, o_ref, w_scr) = refs[:n_w], refs[n_w:]
    cw = w_refs[0].shape[1]

    @pl.when(pl.program_id(1) == 0)
    def _():
        for i, w_ref in enumerate(w_refs):
            w_scr[:, i * cw:(i + 1) * cw] = w_ref[...].astype(w_scr.dtype)

    acc = jnp.dot(h_ref[...], w_scr[...], preferred_element_type=F32)
    if mode == "none":
        o_ref[...] = acc.astype(o_ref.dtype)
    elif mode == "sigmoid":
        o_ref[...] = jax.nn.sigmoid(acc).astype(o_ref.dtype)
    elif mode == "norm":
        o_ref[...] = _headnorm(acc, g_ref[...], hd, scale).astype(o_ref.dtype)
    else:
        @pl.when(pl.program_id(0) == 0)
        def _():
            o_ref[...] = _headnorm(acc, g_ref[...], hd, scale).astype(o_ref.dtype)

        @pl.when(pl.program_id(0) != 0)
        def _():
            o_ref[...] = acc.astype(o_ref.dtype)


W_COLS = 512


def _proj(h, w, cols, tm, mode, out_dtype, gain=None, hd=None, scale=1.0, name="proj"):
    m, k = h.shape
    base, stride, n_w, n_tiles = cols
    tn = n_w * W_COLS
    if gain is None:
        g = jnp.ones((1, tn), F32)
    else:
        g = jnp.tile(gain.astype(F32).reshape(1, -1), (1, tn // gain.shape[-1]))
    body = functools.partial(_proj_body, n_w=n_w, mode=mode, hd=hd, scale=scale)
    w_specs = [pl.BlockSpec((k, W_COLS), functools.partial(lambda j, i, off: (0, base + stride * j + off), off=off))
               for off in range(n_w)]
    return pl.pallas_call(
        body, name=name, grid=(n_tiles, m // tm),
        in_specs=[pl.BlockSpec((tm, k), lambda j, i: (i, 0))] + w_specs + [pl.BlockSpec((1, tn), lambda j, i: (0, 0))],
        out_specs=pl.BlockSpec((tm, tn), lambda j, i: (i, j)),
        out_shape=jax.ShapeDtypeStruct((m, n_tiles * tn), out_dtype),
        scratch_shapes=[pltpu.VMEM((k, tn), BF16)], compiler_params=_cparams(2),
    )(h, *([w] * n_w), g)


PERM_ROWS = 256


def _perm_matrix(n, dil, inverse=False):
    per = n // dil
    o = lax.broadcasted_iota(I32, (n, n), 1 if inverse else 0)
    s = lax.broadcasted_iota(I32, (n, n), 0 if inverse else 1)
    return (s == (o % per) * dil + o // per).astype(BF16)


def _qkv_body(h_ref, wq_ref, wk_ref, wv_ref, gq_ref, gk_ref, q_ref, kv_ref, tok_ref, w_scr, *, dil, scale):
    c = pl.program_id(1)
    tm = h_ref.shape[0]
    nh = HEADS_PER_GROUP_A

    @pl.when(pl.program_id(0) == 0)
    def _():
        for i, w_ref in enumerate((wq_ref, wk_ref, wv_ref)):
            @pl.when(c == i)
            def _():
                w_scr[i] = w_ref[...].astype(BF16)

    acc = jnp.dot(h_ref[...], w_scr[c], preferred_element_type=F32)

    def put_perm(dst_ref, y, col0):
        if dil == 1:
            dst_ref[0, :, col0:col0 + GROUP_W_A] = y.astype(dst_ref.dtype)
            return
        pm = _perm_matrix(PERM_ROWS, dil)
        per = PERM_ROWS // dil
        yb = y.astype(BF16)
        for t in range(tm // PERM_ROWS):
            yp = jnp.dot(pm, yb[t * PERM_ROWS:(t + 1) * PERM_ROWS], preferred_element_type=F32).astype(BF16)
            for r in range(dil):
                dst_ref[r, t * per:(t + 1) * per, col0:col0 + GROUP_W_A] = yp[r * per:(r + 1) * per]

    def put_tok(y, c0):
        for i in range(nh):
            tok_ref[pl.ds(c0 + i, tm, stride=2 * nh), :] = y[:, i * HEAD_DIM_A:(i + 1) * HEAD_DIM_A]

    @pl.when(c == 0)
    def _():
        put_perm(q_ref, _headnorm(acc, gq_ref[...], HEAD_DIM_A, scale), 0)

    @pl.when(c == 1)
    def _():
        y = _headnorm(acc, gk_ref[...], HEAD_DIM_A, 1.0)
        put_tok(y, 0)
        put_perm(kv_ref, y, 0)

    @pl.when(c == 2)
    def _():
        put_tok(acc, nh)
        put_perm(kv_ref, acc, 0)


def _proj_qkv(h, w, sec, g, g_qa, g_ka, n_batch, seq, tm, dil, act_dtype=BF16):
    n, k = h.shape
    ln = seq // dil
    per = tm // dil
    mt = seq // tm
    assert seq % tm == 0
    assert dil == 1 or (act_dtype == BF16 and tm % PERM_ROWS == 0 and (PERM_ROWS // dil) % 16 == 0)
    tile = lambda v: jnp.tile(v.astype(F32).reshape(1, -1), (1, GROUP_W_A // v.shape[-1]))
    wspec = lambda blk: pl.BlockSpec((k, W_COLS), lambda m, c: (0, blk))
    q, kv, tok = pl.pallas_call(
        functools.partial(_qkv_body, dil=dil, scale=HEAD_DIM_A ** -0.5), name=f"proj_qkv{g}", grid=(n // tm, 3),
        in_specs=[pl.BlockSpec((tm, k), lambda m, c: (m, 0)),
                  wspec(sec["q"] + g), wspec(sec["k"] + g), wspec(sec["v"] + g),
                  pl.BlockSpec((1, GROUP_W_A), lambda m, c: (0, 0)), pl.BlockSpec((1, GROUP_W_A), lambda m, c: (0, 0))],
        out_specs=[pl.BlockSpec((None, dil, per, GROUP_W_A), lambda m, c: (m // mt, 0, m % mt, 0)),
                   pl.BlockSpec((None, dil, per, GROUP_W_A), lambda m, c: (m // mt, 0, m % mt, jnp.maximum(c - 1, 0))),
                   pl.BlockSpec((tm * 2 * HEADS_PER_GROUP_A, LANES), lambda m, c: (m, 0))],
        out_shape=[jax.ShapeDtypeStruct((n_batch, dil, ln, GROUP_W_A), act_dtype),
                   jax.ShapeDtypeStruct((n_batch, dil, ln, 2 * GROUP_W_A), act_dtype),
                   jax.ShapeDtypeStruct((n * 2 * HEADS_PER_GROUP_A, LANES), F32)],
        scratch_shapes=[pltpu.VMEM((3, k, W_COLS), BF16)], compiler_params=_cparams(2),
    )(h, w, w, w, tile(g_qa), tile(g_ka))
    return q.reshape(n, GROUP_W_A), kv.reshape(n, 2 * GROUP_W_A), tok


def _bias_from_table(tab_ref, bidx, col):
    acc = jnp.zeros(bidx.shape, F32)
    for kb in range(N_BUCKETS):
        acc = jnp.where(bidx == kb, tab_ref[kb, col], acc)
    return acc


def _split_hi_lo(x):
    hi = x.astype(BF16)
    return hi, (x - hi.astype(F32)).astype(BF16)


def _attn_p_body(tab_ref, bidx_ref, q_ref, kp_ref, kc_ref, vp_ref, vc_ref, o_ref, bias_scr, *, g):
    first = (pl.program_id(0) == 0) & (pl.program_id(1) == 0)

    @pl.when(first)
    def _():
        bidx = bidx_ref[...]
        for h in range(HEADS_PER_GROUP_A):
            bias_scr[h] = _bias_from_table(tab_ref, bidx, g * HEADS_PER_GROUP_A + h)

    qb = pl.program_id(1)
    blk = q_ref.shape[0]
    q = q_ref[...]
    k = jnp.concatenate([kp_ref[...], kc_ref[...]], axis=0)
    v = jnp.concatenate([vp_ref[...], vc_ref[...]], axis=0)
    row = lax.broadcasted_iota(I32, (blk, 2 * blk), 0)
    col = lax.broadcasted_iota(I32, (blk, 2 * blk), 1)
    dist = blk + row - col
    valid = (dist >= 0) & (dist <= blk) & ((qb > 0) | (col >= blk))
    lane = lax.broadcasted_iota(I32, (blk, LANES), 1)
    lse_blk = jnp.zeros((blk, LANES), F32)
    outs = []
    for h in range(HEADS_PER_GROUP_A):
        hs = slice(h * HEAD_DIM_A, (h + 1) * HEAD_DIM_A)
        s = lax.dot_general(q[:, hs], k[:, hs], (((1,), (1,)), ((), ())), preferred_element_type=F32)
        s = jnp.where(valid, s + bias_scr[h], NEG_INF)
        m = jnp.max(s, axis=-1, keepdims=True)
        p = jnp.exp(s - m)
        l = jnp.sum(p, axis=-1, keepdims=True)
        o = jnp.dot(p.astype(BF16), v[:, hs], preferred_element_type=F32) / l
        outs.append(o)
        lse_blk = jnp.where(lane == h, m + jnp.log(l), lse_blk)
    lse_hi, lse_lo = _split_hi_lo(lse_blk)
    o_ref[...] = jnp.concatenate([jnp.concatenate(outs, axis=1).astype(BF16), lse_hi, lse_lo], axis=1)


O_EXT_W = GROUP_W_A + 2 * LANES


def _attn_prompt(q, kv, rel_bias, g, n_batch, seq):
    win, dil = DIL_GROUPS[g]
    blk = win // dil
    nb = seq // dil // blk
    n = n_batch * seq
    a = jnp.arange(blk)[:, None]
    c = jnp.arange(2 * blk)[None, :]
    bidx = _rel_bucket(jnp.maximum(blk + a - c, 0) * dil)

    def cur(s, j):
        return s * nb + j

    def prev(s, j):
        return s * nb + jnp.maximum(j - 1, 0)

    return pl.pallas_call(
        functools.partial(_attn_p_body, g=g), name=f"attn_p{g}", grid=(n_batch * dil, nb),
        in_specs=[
            pl.BlockSpec(memory_space=pltpu.SMEM),
            pl.BlockSpec((blk, 2 * blk), lambda s, j: (0, 0)),
            pl.BlockSpec((blk, GROUP_W_A), lambda s, j: (cur(s, j), 0)),
            pl.BlockSpec((blk, GROUP_W_A), lambda s, j: (prev(s, j), 0)),
            pl.BlockSpec((blk, GROUP_W_A), lambda s, j: (cur(s, j), 0)),
            pl.BlockSpec((blk, GROUP_W_A), lambda s, j: (prev(s, j), 1)),
            pl.BlockSpec((blk, GROUP_W_A), lambda s, j: (cur(s, j), 1)),
        ],
        out_specs=pl.BlockSpec((blk, O_EXT_W), lambda s, j: (cur(s, j), 0)),
        out_shape=jax.ShapeDtypeStruct((n, O_EXT_W), BF16),
        scratch_shapes=[pltpu.VMEM((HEADS_PER_GROUP_A, blk, 2 * blk), F32)],
        compiler_params=_cparams(2),
    )(rel_bias, bidx, q, kv, kv, kv, kv)


def _attn_s_body(tab_ref, bidx_ref, q_ref, kvn_ref, tokn_ref, cache_hbm, o_ref, newc_hbm,
                 raw, kv_scr, bias_scr, sem_in, sem_out, *, g, dil, lb, n_req):
    t_new = q_ref.shape[0]
    nh = HEADS_PER_GROUP_A
    n_rows = nh * t_new
    n_keys = kv_scr.shape[0]
    n_col = 2 * nh
    lb8 = lb * n_col
    n = pl.program_id(0)
    slot = n % 2

    def fetch(req, s):
        return pltpu.make_async_copy(cache_hbm.at[req], raw.at[s, pl.ds(0, lb8)], sem_in.at[s])

    def flush(req, s):
        return pltpu.make_async_copy(raw.at[s, pl.ds(t_new * n_col, lb8)], newc_hbm.at[req], sem_out.at[s])

    @pl.when(n == 0)
    def _():
        kv_scr[lb:, :] = jnp.zeros((n_keys - lb, kv_scr.shape[1]), BF16)
        bidx = bidx_ref[...]
        for h in range(nh):
            bias_scr[h * t_new:(h + 1) * t_new, :] = _bias_from_table(tab_ref, bidx, g * nh + h)
        fetch(0, 0).start()

    @pl.when(n >= 1)
    def _():
        flush(n - 1, 1 - slot).wait()

    @pl.when(n + 1 < n_req)
    def _():
        fetch(n + 1, 1 - slot).start()

    fetch(n, slot).wait()
    raw[slot, lb8:lb8 + t_new * n_col, :] = tokn_ref[...]
    flush(n, slot).start()
    new = kvn_ref[...]

    rows_per = min(lb, 256)
    for c in range(n_col):
        for r0 in range(0, lb, rows_per):
            kv_scr[r0:r0 + rows_per, c * LANES:(c + 1) * LANES] = raw[
                slot, pl.ds(r0 * n_col + c, rows_per, stride=n_col), :].astype(BF16)
    kv_scr[lb:lb + 2 * t_new, :] = jnp.concatenate([new, jnp.zeros_like(new)], axis=0).astype(BF16)

    q = q_ref[...]
    qt = jnp.concatenate([q] * nh, axis=0)
    rr = lax.broadcasted_iota(I32, (n_rows, GROUP_W_A), 0)
    cc = lax.broadcasted_iota(I32, (n_rows, GROUP_W_A), 1)
    qbd = jnp.where(cc // HEAD_DIM_A == rr // t_new, qt, 0.0).astype(BF16)

    s = lax.dot_general(qbd, kv_scr[:, :GROUP_W_A], (((1,), (1,)), ((), ())), preferred_element_type=F32)
    row = lax.broadcasted_iota(I32, (n_rows, n_keys), 0)
    col = lax.broadcasted_iota(I32, (n_rows, n_keys), 1)
    delta = lb + (row & (t_new - 1)) - col
    valid = (delta >= 0) & (delta <= lb) & ((delta & (dil - 1)) == 0)
    s = jnp.where(valid, s + bias_scr[...], NEG_INF)
    m = jnp.max(s, axis=-1, keepdims=True)
    p = jnp.exp(s - m)
    l = jnp.sum(p, axis=-1, keepdims=True)
    o = jnp.dot(p.astype(BF16), kv_scr[:, GROUP_W_A:], preferred_element_type=F32) / l
    lse = m + jnp.log(l)
    lane = lax.broadcasted_iota(I32, (t_new, LANES), 1)
    lse_blk = jnp.zeros((t_new, LANES), F32)
    outs = []
    for h in range(nh):
        outs.append(o[h * t_new:(h + 1) * t_new, h * HEAD_DIM_A:(h + 1) * HEAD_DIM_A])
        lse_blk = jnp.where(lane == h, lse[h * t_new:(h + 1) * t_new, :], lse_blk)
    o_ref[...] = jnp.concatenate(outs + [lse_blk, jnp.zeros_like(lse_blk)], axis=1)

    @pl.when(n == n_req - 1)
    def _():
        flush(n, slot).wait()


def _attn_sample(q, kv_new, tok_new, cache, rel_bias, g, t_new):
    win, dil = DIL_GROUPS[g]
    n_col = 2 * HEADS_PER_GROUP_A
    n_req, lb = cache.shape[0], cache.shape[1] // n_col
    assert t_new & (t_new - 1) == 0 and dil & (dil - 1) == 0 and lb == win
    n_keys = lb + LANES
    t = jnp.arange(t_new)[:, None]
    c = jnp.arange(n_keys)[None, :]
    bidx = _rel_bucket(jnp.clip(lb + t - c, 0, lb))
    return pl.pallas_call(
        functools.partial(_attn_s_body, g=g, dil=dil, lb=lb, n_req=n_req), name=f"attn_s{g}", grid=(n_req,),
        in_specs=[
            pl.BlockSpec(memory_space=pltpu.SMEM),
            pl.BlockSpec((t_new, n_keys), lambda n: (0, 0)),
            pl.BlockSpec((t_new, GROUP_W_A), lambda n: (n, 0)),
            pl.BlockSpec((t_new, 2 * GROUP_W_A), lambda n: (n, 0)),
            pl.BlockSpec((t_new * n_col, LANES), lambda n: (n, 0)),
            pl.BlockSpec(memory_space=pl.ANY),
        ],
        out_specs=[pl.BlockSpec((t_new, O_EXT_W), lambda n: (n, 0)), pl.BlockSpec(memory_space=pl.ANY)],
        out_shape=[jax.ShapeDtypeStruct((n_req * t_new, O_EXT_W), F32), jax.ShapeDtypeStruct(cache.shape, F32)],
        scratch_shapes=[pltpu.VMEM((2, (lb + t_new) * n_col, LANES), F32),
                        pltpu.VMEM((n_keys, 2 * GROUP_W_A), BF16),
                        pltpu.VMEM((HEADS_PER_GROUP_A * t_new, n_keys), F32),
                        pltpu.SemaphoreType.DMA((2,)), pltpu.SemaphoreType.DMA((2,))],
        compiler_params=_cparams(1),
    )(rel_bias, bidx, q, kv_new, tok_new, cache)


def _gmlp_body(u_ref, v_ref, w_ref, b_ref, o_ref, *, period):
    c = w_ref.shape[1]
    nch = u_ref.shape[0] // c
    gd = u_ref.shape[1] // N_GROUPS_B
    i = lax.broadcasted_iota(I32, (c, c), 0)
    j = lax.broadcasted_iota(I32, (c, c), 1)
    mask = (j <= i) & ((i // period) == (j // period))
    b = b_ref[...]
    for g in range(N_GROUPS_B):
        gs = slice(g * gd, (g + 1) * gd)
        wg = jnp.where(mask, w_ref[g], 0.0).astype(BF16)
        vg = [v_ref[ch * c:(ch + 1) * c, gs].astype(BF16) for ch in range(nch)]
        vg = vg[0] if nch == 1 else jnp.concatenate(vg, axis=1)
        sg = jnp.dot(wg, vg, preferred_element_type=F32) + b[:, g:g + 1]
        for ch in range(nch):
            u = u_ref[ch * c:(ch + 1) * c, gs].astype(F32)
            o_ref[ch * c:(ch + 1) * c, gs] = (u * sg[:, ch * gd:(ch + 1) * gd]).astype(o_ref.dtype)


def _gmlp(u, v, w, b, tm, period):
    m, wb = u.shape
    c = w.shape[1]
    return pl.pallas_call(
        functools.partial(_gmlp_body, period=period), name="gmlp", grid=(m // tm,),
        in_specs=[pl.BlockSpec((tm, wb), lambda i: (i, 0)), pl.BlockSpec((tm, wb), lambda i: (i, 0)),
                  pl.BlockSpec((N_GROUPS_B, c, c), lambda i: (0, 0, 0)),
                  pl.BlockSpec((c, N_GROUPS_B), lambda i: (0, 0))],
        out_specs=pl.BlockSpec((tm, wb), lambda i: (i, 0)),
        out_shape=jax.ShapeDtypeStruct((m, wb), BF16), compiler_params=_cparams(1),
    )(u, v, w, b)


def _gmlp_new_body(u_ref, v_ref, w_ref, b_ref, o_ref):
    t = w_ref.shape[0]
    n = u_ref.shape[0] // t
    width = u_ref.shape[1]
    v = v_ref[...].astype(F32).reshape(n, t, width)
    sg = jnp.broadcast_to(b_ref[...][None], (n, t, width))
    p_row = lax.broadcasted_iota(I32, (t, width), 0)
    for q in range(t):
        sg = sg + jnp.where(p_row >= q, w_ref[q], 0.0)[None] * v[:, q:q + 1, :]
    o_ref[...] = (u_ref[...].astype(F32).reshape(n, t, width) * sg).reshape(n * t, width).astype(o_ref.dtype)


def _gmlp_new(u, v, w_s, b_s, t):
    m, wb = u.shape
    gd = wb // N_GROUPS_B
    wq = jnp.repeat(jnp.transpose(w_s[:, :t, :t], (2, 1, 0)), gd, axis=-1)
    bq = jnp.repeat(b_s[:, :t].T, gd, axis=-1)
    whole = lambda a: pl.BlockSpec(a.shape, lambda i: (0,) * a.ndim)
    return pl.pallas_call(
        _gmlp_new_body, name="gmlp_new", grid=(1,),
        in_specs=[whole(u), whole(v), whole(wq), whole(bq)], out_specs=whole(u),
        out_shape=jax.ShapeDtypeStruct((m, wb), BF16), compiler_params=_cparams(1),
    )(u, v, wq, bq)


def _memattn_body(q_ref, kv_ref, o_ref):
    wm = N_HEADS_M * HEAD_DIM_M
    q = q_ref[...].astype(BF16)
    outs = []
    for h in range(N_HEADS_M):
        hs = slice(h * HEAD_DIM_M, (h + 1) * HEAD_DIM_M)
        k = kv_ref[:, hs].astype(BF16)
        v = kv_ref[:, wm + h * HEAD_DIM_M:wm + (h + 1) * HEAD_DIM_M].astype(BF16)
        s = lax.dot_general(q[:, hs], k, (((1,), (1,)), ((), ())), preferred_element_type=F32)
        m = jnp.max(s, axis=-1, keepdims=True)
        p = jnp.exp(s - m)
        l = jnp.sum(p, axis=-1, keepdims=True)
        outs.append(jnp.dot(p.astype(BF16), v, preferred_element_type=F32) / l)
    o_ref[...] = jnp.concatenate(outs, axis=1).astype(o_ref.dtype)


def _memattn(q, kv, tq, out_dtype):
    m, wm = q.shape
    n = kv.shape[0]
    per = m // n // tq
    kv_spec = pl.BlockSpec((None,) + kv.shape[1:], lambda i: (i // per,) + (0,) * (kv.ndim - 1))
    return pl.pallas_call(
        _memattn_body, name="memattn", grid=(m // tq,),
        in_specs=[pl.BlockSpec((tq, wm), lambda i: (i, 0)), kv_spec],
        out_specs=pl.BlockSpec((tq, wm), lambda i: (i, 0)),
        out_shape=jax.ShapeDtypeStruct((m, wm), out_dtype), compiler_params=_cparams(1),
    )(q, kv)


def _mix_body(gt_ref, o0_ref, o1_ref, o2_ref, ob_ref, om_ref, wpa_ref, wpb_ref, wpm_ref, z_ref, *, dils):
    d = z_ref.shape[1]
    tm = z_ref.shape[0]
    os_, ls = [], []
    for o_ref, dil in zip((o0_ref, o1_ref, o2_ref), dils):
        if dil == 1:
            x = o_ref[...].astype(F32)
        else:
            rows = jnp.concatenate([o_ref[r] for r in range(dil)], axis=0)
            x = jnp.dot(_perm_matrix(tm, dil, inverse=True), rows, preferred_element_type=F32)
        os_.append(x[:, :GROUP_W_A])
        ls.append(x[:, GROUP_W_A:GROUP_W_A + LANES] + x[:, GROUP_W_A + LANES:])
    l0, l1, l2 = ls
    mx = jnp.maximum(jnp.maximum(l0, l1), l2)
    e0, e1, e2 = jnp.exp(l0 - mx), jnp.exp(l1 - mx), jnp.exp(l2 - mx)
    den = e0 + e1 + e2
    w0, w1, w2 = e0 / den, e1 / den, e2 / den
    cols = []
    for h in range(HEADS_PER_GROUP_A):
        hs = slice(h * HEAD_DIM_A, (h + 1) * HEAD_DIM_A)
        cols.append(w0[:, h:h + 1] * os_[0][:, hs] + w1[:, h:h + 1] * os_[1][:, hs] + w2[:, h:h + 1] * os_[2][:, hs])
    oa = jnp.concatenate(cols, axis=1).astype(BF16)
    pa = jnp.dot(oa, wpa_ref[...], preferred_element_type=F32)
    pb = jnp.dot(ob_ref[...].astype(BF16), wpb_ref[...], preferred_element_type=F32)
    pm = jnp.dot(om_ref[...].astype(BF16), wpm_ref[...], preferred_element_type=F32)
    z = (gt_ref[:, 0:d].astype(F32) * pa + gt_ref[:, d:2 * d].astype(F32) * pb
         + gt_ref[:, 2 * d:3 * d].astype(F32) * pm)
    z_ref[...] = z.astype(z_ref.dtype)


def _mix(gates, o_list, dils, seq, ob, om, wpa, wpb, wpm, tm):
    m = gates.shape[0]
    d = wpa.shape[1]
    mt = seq // tm if any(dl > 1 for dl in dils) else 1

    def rows(width):
        return pl.BlockSpec((tm, width), lambda i: (i, 0))

    def whole(a):
        return pl.BlockSpec(a.shape, lambda i: (0, 0))

    o_specs, o_args = [], []
    for o, dil in zip(o_list, dils):
        if dil == 1:
            o_specs.append(rows(O_EXT_W))
            o_args.append(o)
        else:
            assert seq % tm == 0 and (tm // dil) % 16 == 0
            o_specs.append(pl.BlockSpec((None, dil, tm // dil, O_EXT_W), lambda i: (i // mt, 0, i % mt, 0)))
            o_args.append(o.reshape(m // seq, dil, seq // dil, O_EXT_W))
    return pl.pallas_call(
        functools.partial(_mix_body, dils=tuple(dils)), name="mix", grid=(m // tm,),
        in_specs=[rows(gates.shape[1])] + o_specs
                 + [rows(ob.shape[1]), rows(om.shape[1]), whole(wpa), whole(wpb), whole(wpm)],
        out_specs=rows(d),
        out_shape=jax.ShapeDtypeStruct((m, d), BF16), compiler_params=_cparams(1),
    )(gates, *o_args, ob, om, wpa, wpb, wpm)


def _route(logits):
    lane = lax.broadcasted_iota(I32, logits.shape, 1)
    lane_f = lane.astype(F32)
    is_g = lane < N_EXPERT_GROUPS
    gmax = jnp.max(jnp.where(is_g, logits, -jnp.inf), axis=1, keepdims=True)
    gsel = jnp.min(jnp.where(is_g & (logits == gmax), lane_f, float(LANES)), axis=1, keepdims=True).astype(I32)
    gden = jnp.sum(jnp.where(is_g, jnp.exp(logits - gmax), 0.0), axis=1, keepdims=True)
    pg = 1.0 / gden
    e_lane = lane - N_EXPERT_GROUPS
    in_grp = (e_lane >= 0) & (e_lane < N_EXPERTS) & ((e_lane // EXPERTS_PER_GROUP) == gsel)
    m1 = jnp.max(jnp.where(in_grp, logits, -jnp.inf), axis=1, keepdims=True)
    i1 = jnp.min(jnp.where(in_grp & (logits == m1), lane_f, float(LANES)), axis=1, keepdims=True).astype(I32)
    rest = in_grp & (lane != i1)
    m2 = jnp.max(jnp.where(rest, logits, -jnp.inf), axis=1, keepdims=True)
    i2 = jnp.min(jnp.where(rest & (logits == m2), lane_f, float(LANES)), axis=1, keepdims=True).astype(I32)
    e2 = jnp.exp(m2 - m1)
    w1 = pg / (1.0 + e2)
    w2 = pg * e2 / (1.0 + e2)
    eid = jnp.where(lane == 0, i1 - N_EXPERT_GROUPS, jnp.where(lane == 1, i2 - N_EXPERT_GROUPS, 0))
    wts = jnp.where(lane == 0, w1, jnp.where(lane == 1, w2, 0.0))
    return eid, wts


def _resid_body(x_ref, z_ref, wo_ref, gf_ref, wr_ref, br_ref, xmid_ref, hpk_ref, eid_ref, wts_ref):
    xm = x_ref[...] + jnp.dot(z_ref[...], wo_ref[...], preferred_element_type=F32)
    xmid_ref[...] = xm
    hf = _rms(xm, gf_ref[...])
    wr = wr_ref[...]
    wr_hi = wr.astype(BF16)
    wr_lo = (wr - wr_hi.astype(F32)).astype(BF16)
    hf_hi = hf.astype(BF16)
    hf_lo = (hf - hf_hi.astype(F32)).astype(BF16)
    logits = (jnp.dot(hf_hi, wr_hi, preferred_element_type=F32) + jnp.dot(hf_hi, wr_lo, preferred_element_type=F32)
              + jnp.dot(hf_lo, wr_hi, preferred_element_type=F32) + br_ref[...])
    eid, wts = _route(logits)
    eid_ref[...] = eid
    wts_ref[...] = wts
    _store_token_tiles(hpk_ref, hf)


def _resid(x2d, z, wo, g_ffn, w_r, b_r, tm):
    m, d = x2d.shape

    def rows(width):
        return pl.BlockSpec((tm, width), lambda i: (i, 0))

    def whole(a):
        return pl.BlockSpec(a.shape, lambda i: (0, 0))

    gf = g_ffn.reshape(1, d)
    nt = d // 2 // LANES
    return pl.pallas_call(
        _resid_body, name="resid", grid=(m // tm,),
        in_specs=[rows(d), rows(d), whole(wo), whole(gf), whole(w_r), whole(b_r)],
        out_specs=[rows(d), pl.BlockSpec((tm * nt, LANES), lambda i: (i, 0)), rows(LANES), rows(LANES)],
        out_shape=[jax.ShapeDtypeStruct((m, d), F32), jax.ShapeDtypeStruct((m * nt, LANES), U32),
                   jax.ShapeDtypeStruct((m, LANES), I32), jax.ShapeDtypeStruct((m, LANES), F32)],
        compiler_params=_cparams(1),
    )(x2d, z, wo, gf, w_r, b_r)


TOKEN_TILE = 8
GATHER_UNROLL = 8


def _gather_rows(idx_at, src_hbm, dst, sem, n):
    def body(i, carry):
        src = pl.multiple_of(idx_at(i), TOKEN_TILE)
        dst_row = pl.multiple_of(i * TOKEN_TILE, TOKEN_TILE)
        pltpu.make_async_copy(src_hbm.at[pl.ds(src, TOKEN_TILE)], dst.at[pl.ds(dst_row, TOKEN_TILE)], sem).start()
        return carry
    lax.fori_loop(0, n, body, 0, unroll=GATHER_UNROLL)


FFN_CHUNKS = 8
FFN_BUFS = 3


def _ffn_body(blk_e_ref, nused_ref, eord_ref, enext_ref, idx_hbm, h_hbm, wg_hbm, wu_hbm, wd_hbm, y_ref,
              idx_s, isem, buf, sem, x_scr, a_scr, b_scr, hm_scr, wg_f, wu_f, wd_f, wsem, wg_s, wu_s, wd_s):
    j = pl.program_id(0)
    nused = nused_ref[0]
    last = nused - 1
    rows = buf.shape[1] // TOKEN_TILE
    e = blk_e_ref[j]

    def idx_copy(b):
        s = b % FFN_BUFS
        return pltpu.make_async_copy(idx_hbm.at[jnp.minimum(b, last)], idx_s.at[s], isem.at[s])

    def wait_rows(s):
        pltpu.make_async_copy(h_hbm.at[pl.ds(0, rows * TOKEN_TILE)], buf.at[s], sem.at[s]).wait()
    new_expert = (j == 0) | (e != blk_e_ref[jnp.maximum(j - 1, 0)])
    ws = eord_ref[j] % 2

    def fetch_w(expert, s):
        return [pltpu.make_async_copy(src.at[expert], dst.at[s], wsem.at[s, i])
                for i, (src, dst) in enumerate(((wg_hbm, wg_f), (wu_hbm, wu_f), (wd_hbm, wd_f)))]

    @pl.when(j == 0)
    def _():
        for c in fetch_w(e, ws):
            c.start()
        for b in range(FFN_BUFS):
            idx_copy(b).start()
        for b in range(FFN_BUFS - 1):
            idx_copy(b).wait()
            _gather_rows(lambda i, b=b: idx_s[b, 0, i], h_hbm, buf.at[b], sem.at[b], rows)

    @pl.when(new_expert)
    def _():
        nxt = enext_ref[j]

        @pl.when(nxt >= 0)
        def _():
            for c in fetch_w(nxt, 1 - ws):
                c.start()

        for c in fetch_w(e, ws):
            c.wait()
        wg_s[...] = wg_f[ws].astype(BF16)
        wu_s[...] = wu_f[ws].astype(BF16)
        wd_s[...] = wd_f[ws].astype(BF16)

    @pl.when(j < nused)
    def _():
        slot = j % FFN_BUFS
        s2 = (j + 2) % FFN_BUFS
        per = rows // FFN_CHUNKS
        idx_copy(j + 2).wait()

        @pl.when(j < last)
        def _():
            idx_copy(j + 3).start()

        def issue(c):
            for i in range(c * per, (c + 1) * per):
                src = pl.multiple_of(idx_s[s2, 0, i], TOKEN_TILE)
                pltpu.make_async_copy(h_hbm.at[pl.ds(src, TOKEN_TILE)],
                                      buf.at[s2, pl.ds(i * TOKEN_TILE, TOKEN_TILE)], sem.at[s2]).start()

        wait_rows(slot)
        x_scr[...] = _load_token_tiles(buf, slot, 0, rows, TOKEN_TILE).astype(BF16)
        dn = a_scr.shape[1] // 2
        for c in range(2):
            issue(c)
            a_scr[:, c * dn:(c + 1) * dn] = jnp.dot(x_scr[...], wg_s[:, c * dn:(c + 1) * dn],
                                                     preferred_element_type=F32)
        for c in range(2):
            issue(2 + c)
            b_scr[:, c * dn:(c + 1) * dn] = jnp.dot(x_scr[...], wu_s[:, c * dn:(c + 1) * dn],
                                                     preferred_element_type=F32)
        a = a_scr[...]
        hm_scr[...] = (a * jax.nn.sigmoid(a) * b_scr[...]).astype(BF16)
        half = wd_s.shape[1] // 2
        n_dc = FFN_CHUNKS - 4
        dq = half // n_dc
        for c in range(n_dc):
            issue(4 + c)
            lo = jnp.dot(hm_scr[...], wd_s[:, c * dq:(c + 1) * dq], preferred_element_type=F32)
            hi = jnp.dot(hm_scr[...], wd_s[:, half + c * dq:half + (c + 1) * dq], preferred_element_type=F32)
            w = _pack_bf16_pair(jnp.concatenate([lo, hi], axis=1))
            for t in range(dq // LANES):
                y_ref[pl.ds(c * (dq // LANES) + t, rows, stride=TOKEN_TILE), :] = w[:, t * LANES:(t + 1) * LANES]

        @pl.when(j == last)
        def _():
            wait_rows((j + 1) % FFN_BUFS)
            wait_rows(s2)

    @pl.when(j >= nused)
    def _():
        y_ref[...] = jnp.zeros(y_ref.shape, y_ref.dtype)


def _ffn(hpk, src_tok, blk_e, nused, eord, enext, wg, wu, wd):
    n_blocks = blk_e.shape[0]
    rows = MOE_ROWS
    tiles = rows * TOKEN_TILE
    d, de = wg.shape[1], wg.shape[2]
    assert d // 2 == TOKEN_TILE * LANES and rows % FFN_CHUNKS == 0 and (d // 2) % ((FFN_CHUNKS - 4) * LANES) == 0
    assert n_blocks >= N_EXPERTS + FFN_BUFS
    idx = (src_tok * TOKEN_TILE).reshape(n_blocks, 1, rows)
    hbm = pl.BlockSpec(memory_space=pl.ANY)
    grid_spec = pltpu.PrefetchScalarGridSpec(
        num_scalar_prefetch=4, grid=(n_blocks,),
        in_specs=[hbm, hbm, hbm, hbm, hbm],
        out_specs=pl.BlockSpec((tiles, LANES), lambda j, *_: (j, 0)),
        scratch_shapes=[pltpu.SMEM((FFN_BUFS, 1, rows), I32), pltpu.SemaphoreType.DMA((FFN_BUFS,)),
                        pltpu.VMEM((FFN_BUFS, tiles, LANES), U32), pltpu.SemaphoreType.DMA((FFN_BUFS,)),
                        pltpu.VMEM((rows, d), BF16), pltpu.VMEM((rows, de), F32), pltpu.VMEM((rows, de), F32),
                        pltpu.VMEM((rows, de), BF16),
                        pltpu.VMEM((2, d, de), wg.dtype), pltpu.VMEM((2, d, de), wu.dtype),
                        pltpu.VMEM((2, de, d), wd.dtype), pltpu.SemaphoreType.DMA((2, 3)),
                        pltpu.VMEM((d, de), BF16), pltpu.VMEM((d, de), BF16), pltpu.VMEM((de, d), BF16)],
    )
    return pl.pallas_call(
        _ffn_body, name="ffn", grid_spec=grid_spec,
        out_shape=jax.ShapeDtypeStruct((n_blocks * tiles, LANES), U32), compiler_params=_cparams(1),
    )(blk_e, nused, eord, enext, idx, hpk, wg, wu, wd)


def _combine_body(idx_cur_ref, idx_nxt_ref, x_ref, w_ref, y_hbm, o_ref, buf, sem, *, n):
    i = pl.program_id(0)
    slot = i % 2
    rows = buf.shape[1] // TOKEN_TILE
    tm = rows // TOP_K

    @pl.when(i == 0)
    def _():
        _gather_rows(lambda r: idx_cur_ref[0, 0, r], y_hbm, buf.at[0], sem.at[0], rows)

    if n > 1:
        @pl.when(i + 1 < n)
        def _():
            _gather_rows(lambda r: idx_nxt_ref[0, 0, r], y_hbm, buf.at[1 - slot], sem.at[1 - slot], rows)

    pltpu.make_async_copy(y_hbm.at[pl.ds(0, rows * TOKEN_TILE)], buf.at[slot], sem.at[slot]).wait()
    w = w_ref[...]
    y0 = _load_token_tiles(buf, slot, 0, tm, TOKEN_TILE)
    y1 = _load_token_tiles(buf, slot, tm, tm, TOKEN_TILE)
    o_ref[...] = x_ref[...] + (w[:, 0:1] * y0 + w[:, 1:2] * y1)


def _combine(xmid, wts, dest, ypk, tm):
    m, d = xmid.shape
    nt = m // tm
    assert d // 2 == TOKEN_TILE * LANES
    idx = (dest * TOKEN_TILE).reshape(nt, tm, TOP_K).transpose(0, 2, 1).reshape(nt, 1, TOP_K * tm)

    def idx_spec(ahead):
        return pl.BlockSpec((1, 1, TOP_K * tm), lambda i: (jnp.minimum(i + ahead, nt - 1), 0, 0),
                            memory_space=pltpu.SMEM)

    return pl.pallas_call(
        functools.partial(_combine_body, n=nt), name="combine", grid=(nt,),
        in_specs=[idx_spec(0), idx_spec(1), pl.BlockSpec((tm, d), lambda i: (i, 0)),
                  pl.BlockSpec((tm, LANES), lambda i: (i, 0)), pl.BlockSpec(memory_space=pl.ANY)],
        out_specs=pl.BlockSpec((tm, d), lambda i: (i, 0)),
        out_shape=jax.ShapeDtypeStruct((m, d), F32),
        scratch_shapes=[pltpu.VMEM((2, TOP_K * tm * TOKEN_TILE, LANES), U32), pltpu.SemaphoreType.DMA((2,))],
        compiler_params=_cparams(1),
    )(idx, idx, xmid, wts, ypk)


def _moe_plan(eid2):
    n_slot = eid2.shape[0] * TOP_K
    eid = eid2.reshape(n_slot)
    onehot = (eid[:, None] == jnp.arange(N_EXPERTS, dtype=I32)[None, :]).astype(I32)
    csum = jnp.cumsum(onehot, axis=0)
    rank = jnp.take_along_axis(csum, eid[:, None], axis=1)[:, 0] - 1
    counts = csum[-1]
    padded = (counts + MOE_ROWS - 1) // MOE_ROWS * MOE_ROWS
    pend = jnp.cumsum(padded)
    pstart = pend - padded
    dest = pstart[eid] + rank
    n_blocks = -(-n_slot // MOE_ROWS) + N_EXPERTS
    nused = (pend[-1] // MOE_ROWS).astype(I32)
    blk = jnp.minimum(jnp.arange(n_blocks, dtype=I32), nused - 1)
    blk_e = jnp.minimum(jnp.searchsorted(pend, blk * MOE_ROWS, side="right"), N_EXPERTS - 1).astype(I32)
    n_rows = n_blocks * MOE_ROWS
    src_tok = (jnp.arange(n_rows, dtype=I32) % (n_slot // TOP_K)).at[dest].set(jnp.arange(n_slot, dtype=I32) // TOP_K)
    has_rows = counts > 0
    ord_e = jnp.cumsum(has_rows.astype(I32)) - 1
    ids = jnp.arange(N_EXPERTS, dtype=I32)
    later = has_rows[None, :] & (ids[None, :] > ids[:, None])
    next_e = jnp.where(later.any(axis=1), jnp.argmax(later, axis=1), -1).astype(I32)
    return dest.astype(I32), src_tok, blk_e, nused.reshape(1), ord_e[blk_e].astype(I32), next_e[blk_e]


def _mixer(x2d, tm, p, attn_fn, gmlp_fn, mem_fn):
    h = _prep(x2d, p["g_mix"], tm)
    qdt = p["q_dtype"]
    wi, sec = p["w_in"], p["sec"]
    o_list, dils, seq = attn_fn(h)
    ub = _proj(h, wi, (sec["u"], 1, 1, sec["vb"] - sec["u"]), tm, "none", BF16, name="proj_ub")
    n_vb = sec["qm"] - sec["vb"]
    vb = _proj(h, wi, (sec["vb"], n_vb, n_vb, 1), tm, "norm", p["vb_dtype"], p["g_vb"], n_vb * W_COLS,
               name="proj_vb")
    qm = _proj(h, wi, (sec["qm"], 1, 1, sec["gt"] - sec["qm"]), tm, "norm", qdt, p["g_qm"], HEAD_DIM_M,
               HEAD_DIM_M ** -0.5, name="proj_qm")
    gates = _proj(h, wi, (sec["gt"], 2, 2, (sec["end"] - sec["gt"]) // 2), tm, "sigmoid", BF16, name="proj_gates")
    ob = gmlp_fn(ub, vb)
    om = mem_fn(qm)
    z = _mix(gates, o_list, dils, seq, ob, om, p["w_pa"], p["w_pb"], p["w_pm"], min(tm, 256))
    xmid, hpk, eid, wts = _resid(x2d, z, p["w_o"], p["g_ffn"], p["w_r"], p["b_r"], min(tm, 256))
    return vb, xmid, hpk, eid, wts


def kernel(x_prompt, x_sample, mem_prompt, cache_a0_kv, cache_a1_kv, cache_a2_kv, cache_mem_kv, rel_bias, g_mix, w_in, g_qa, g_ka, w_pa, g_vb, w_s, b_s, w_pb, g_mem, w_mk, w_mv, g_qm, g_km, w_pm, w_o, g_ffn, w_rg, b_rg, w_re, b_re, w_gate, w_up, w_down):
    n_b, seq, d = x_prompt.shape
    n_s, t_s, _ = x_sample.shape
    depth = w_in.shape[0]
    assert depth == 1
    l = 0
    caches = (cache_a0_kv, cache_a1_kv, cache_a2_kv)
    n_g = len(DIL_GROUPS)
    wb = w_pb.shape[1]
    wm = N_HEADS_M * HEAD_DIM_M

    offs = [0, WIDTH_A, 2 * WIDTH_A, 3 * WIDTH_A, 3 * WIDTH_A + wb, 3 * WIDTH_A + 2 * wb,
            3 * WIDTH_A + 2 * wb + wm, w_in.shape[2]]
    assert all(o % W_COLS == 0 for o in offs) and (offs[7] - offs[6]) % (2 * W_COLS) == 0
    sec = dict(zip(("q", "k", "v", "u", "vb", "qm", "gt", "end"), (o // W_COLS for o in offs)))
    p = {
        "g_mix": g_mix[l], "g_qa": g_qa[l], "g_ka": g_ka[l], "g_vb": g_vb[l], "g_qm": g_qm[l], "g_ffn": g_ffn[l],
        "w_in": w_in[l], "sec": sec,
        "w_pa": w_pa[l].astype(BF16), "w_pb": w_pb[l].astype(BF16), "w_pm": w_pm[l].astype(BF16),
        "w_o": w_o[l].astype(BF16),
    }
    n_r = N_EXPERT_GROUPS + N_EXPERTS
    p["w_r"] = jnp.concatenate([w_rg[l], w_re[l], jnp.zeros((d, LANES - n_r), F32)], axis=1)
    p["b_r"] = jnp.concatenate([b_rg[l], b_re[l], jnp.zeros((LANES - n_r,), F32)]).reshape(1, LANES)

    n_mem = mem_prompt.shape[1]
    h_mem = _prep(mem_prompt.reshape(n_b * n_mem, d), g_mem[l], 256)
    w_mkv = jnp.concatenate([w_mk[l], w_mv[l]], axis=1)
    n_mw = wm // W_COLS
    mkv = _proj(h_mem, w_mkv, (0, n_mw, n_mw, 2), 256, "norm_first", F32, g_km[l], HEAD_DIM_M, name="proj_mkv")
    new_mem_p = mkv.reshape(1, n_b, n_mem, 2, N_HEADS_M, HEAD_DIM_M)

    ps = dict(p, q_dtype=F32, vb_dtype=F32)
    m_s = n_s * t_s
    n_col = 2 * HEADS_PER_GROUP_A
    new_s = []

    def attn_sample(h):
        o_list = []
        for g in range(n_g):
            q, kv, tok = _proj_qkv(h, w_in[l], sec, g, g_qa[l], g_ka[l], 1, m_s, m_s, 1, act_dtype=F32)
            cache3 = caches[g][l].reshape(n_s, caches[g].shape[2] * n_col, LANES)
            o, newc = _attn_sample(q, kv, tok, cache3, rel_bias, g, t_s)
            o_list.append(o)
            new_s.append(newc.reshape(caches[g][l:l + 1].shape))
        return o_list, [1] * n_g, t_s

    vb_s, xmid_s, hpk_s, eid_s, wts_s = _mixer(
        x_sample.reshape(m_s, d), m_s, ps, attn_sample,
        lambda u, v: _gmlp_new(u, v, w_s[l], b_s[l], t_s),
        lambda qm: _memattn(qm, cache_mem_kv[l].reshape(n_s, n_mem, 2 * wm), t_s, F32))
    new_vb_s = vb_s.reshape(1, n_s, t_s, wb)

    pp = dict(p, q_dtype=BF16, vb_dtype=BF16)
    new_p = []

    def attn_prompt(h):
        o_list = []
        for g, (win, dil) in enumerate(DIL_GROUPS):
            q, kv, tok = _proj_qkv(h, w_in[l], sec, g, g_qa[l], g_ka[l], n_b, seq, 512, dil)
            o_list.append(_attn_prompt(q, kv, rel_bias, g, n_b, seq))
            keep = min(win, seq)
            kv5 = tok.reshape(n_b, seq, 2, HEADS_PER_GROUP_A, HEAD_DIM_A)
            new_p.append((kv5 if keep == seq else kv5[:, seq - keep:])[None])
        return o_list, [dl for _, dl in DIL_GROUPS], seq

    _, xmid_p, hpk_p, eid_p, wts_p = _mixer(
        x_prompt.reshape(n_b * seq, d), 512, pp, attn_prompt,
        lambda u, v: _gmlp(u, v, w_s[l], b_s[l].T, 512, CHUNK),
        lambda qm: _memattn(qm, mkv.reshape(n_b, n_mem, 2 * wm), 256, BF16))

    n_p = n_b * seq
    hpk = jnp.concatenate([hpk_p, hpk_s], axis=0)
    eid2 = jnp.concatenate([eid_p[:, :TOP_K], eid_s[:, :TOP_K]], axis=0)
    dest, src_tok, blk_e, nused, eord, enext = _moe_plan(eid2)
    ypk = _ffn(hpk, src_tok, blk_e, nused, eord, enext, w_gate[l], w_up[l], w_down[l])
    dest2 = dest.reshape(-1, TOP_K)
    y_p = _combine(xmid_p, wts_p, dest2[:n_p], ypk, 256)
    y_s = _combine(xmid_s, wts_s, dest2[n_p:], ypk, m_s)

    return (y_p.reshape(n_b, seq, d), y_s.reshape(n_s, t_s, d), new_p[0], new_p[1], new_p[2], new_mem_p,
            new_s[0], new_s[1], new_s[2], new_vb_s)
```

```python
import functools
import math

import jax
import jax.numpy as jnp
from jax import lax
from jax.experimental import pallas as pl
from jax.experimental.pallas import tpu as pltpu

F32 = jnp.float32
BF16 = jnp.bfloat16
I32 = jnp.int32
U32 = jnp.uint32

EPS = 1e-6
NEG_INF = -1e30

HEAD_DIM_A = 128
HEADS_PER_GROUP_A = 4
DIL_GROUPS = ((128, 1), (512, 4), (2048, 16))
GROUP_W_A = HEADS_PER_GROUP_A * HEAD_DIM_A
WIDTH_A = len(DIL_GROUPS) * GROUP_W_A
CHUNK = 128
N_GROUPS_B = 8
N_HEADS_M = 4
HEAD_DIM_M = 256
N_BUCKETS = 32
MAX_EXACT = N_BUCKETS // 2
MAX_DISTANCE = 2048
N_EXPERT_GROUPS = 4
EXPERTS_PER_GROUP = 8
N_EXPERTS = N_EXPERT_GROUPS * EXPERTS_PER_GROUP
TOP_K = 2
LANES = 128
MOE_ROWS = 256
VMEM_LIMIT = 56 * 1024 * 1024


def _cparams(n_grid, vmem=VMEM_LIMIT):
    return pltpu.CompilerParams(dimension_semantics=("arbitrary",) * n_grid, vmem_limit_bytes=vmem)


def _rms(x, g):
    return x * lax.rsqrt(jnp.mean(x * x, axis=-1, keepdims=True) + EPS) * g


def _pack_bf16_pair(x):
    n = x.shape[1] // 2
    lo = lax.bitcast_convert_type(x[:, :n].astype(BF16).astype(F32), U32)
    hi = lax.bitcast_convert_type(x[:, n:].astype(BF16).astype(F32), U32)
    return (hi & jnp.uint32(0xFFFF0000)) | (lo >> 16)


def _unpack_bf16_pair(w):
    lo = lax.bitcast_convert_type(w << 16, F32)
    hi = lax.bitcast_convert_type(w & jnp.uint32(0xFFFF0000), F32)
    return lo, hi


def _store_token_tiles(ref, x):
    w = _pack_bf16_pair(x)
    m, n = w.shape
    nt = n // LANES
    for c in range(nt):
        ref[pl.ds(c, m, stride=nt), :] = w[:, c * LANES:(c + 1) * LANES]


def _load_token_tiles(ref, lead, row0, m, nt):
    los, his = [], []
    view = ref if lead is None else ref.at[lead]
    for c in range(nt):
        lo, hi = _unpack_bf16_pair(view[pl.ds(row0 * nt + c, m, stride=nt), :])
        los.append(lo)
        his.append(hi)
    return jnp.concatenate(los + his, axis=1)


def _rel_bucket(dist):
    d = jnp.maximum(dist, 1).astype(F32)
    large = MAX_EXACT + (jnp.log(d / MAX_EXACT) / math.log(MAX_DISTANCE / MAX_EXACT)
                         * (N_BUCKETS - MAX_EXACT)).astype(I32)
    return jnp.where(dist < MAX_EXACT, dist, jnp.minimum(large, N_BUCKETS - 1)).astype(I32)


def _prep_body(x_ref, g_ref, o_ref):
    o_ref[...] = _rms(x_ref[...], g_ref[...]).astype(o_ref.dtype)


def _prep(x2d, g, tm):
    m, d = x2d.shape
    return pl.pallas_call(
        _prep_body, name="prep", grid=(m // tm,),
        in_specs=[pl.BlockSpec((tm, d), lambda i: (i, 0)), pl.BlockSpec((1, d), lambda i: (0, 0))],
        out_specs=pl.BlockSpec((tm, d), lambda i: (i, 0)),
        out_shape=jax.ShapeDtypeStruct((m, d), BF16), compiler_params=_cparams(1),
    )(x2d, g.reshape(1, d))


def _headnorm(acc, g, hd, scale):
    outs = []
    for j in range(acc.shape[1] // hd):
        sl = acc[:, j * hd:(j + 1) * hd]
        outs.append(sl * lax.rsqrt(jnp.mean(sl * sl, axis=-1, keepdims=True) + EPS))
    y = outs[0] if len(outs) == 1 else jnp.concatenate(outs, axis=1)
    y = y * g
    return y * scale if scale != 1.0 else y


PROJ_SUB_ROWS = 256


def _proj_body(h_ref, *refs, n_w, mode, hd, scale):
    w_refs, (g_ref, o_ref, w_scr) = refs[:n_w], refs[n_w:]
    cw = w_refs[0].shape[1]

    @pl.when(pl.program_id(1) == 0)
    def _():
        for i, w_ref in enumerate(w_refs):
            w_scr[:, i * cw:(i + 1) * cw] = w_ref[...].astype(w_scr.dtype)

    tm = h_ref.shape[0]
    n_sub = max(1, tm // PROJ_SUB_ROWS) if mode in ("norm", "norm_first") else 1
    sub = tm // n_sub

    def run(epilogue):
        for t in range(n_sub):
            rs = slice(t * sub, (t + 1) * sub)
            acc = jnp.dot(h_ref[rs, :], w_scr[...], preferred_element_type=F32)
            o_ref[rs, :] = epilogue(acc).astype(o_ref.dtype)

    normed = lambda acc: _headnorm(acc, g_ref[...], hd, scale)
    if mode == "none":
        run(lambda acc: acc)
    elif mode == "sigmoid":
        run(jax.nn.sigmoid)
    elif mode == "norm":
        run(normed)
    else:
        @pl.when(pl.program_id(0) == 0)
        def _():
            run(normed)

        @pl.when(pl.program_id(0) != 0)
        def _():
            run(lambda acc: acc)


W_COLS = 512


def _proj(h, w, cols, tm, mode, out_dtype, gain=None, hd=None, scale=1.0, name="proj"):
    m, k = h.shape
    base, stride, n_w, n_tiles = cols
    tn = n_w * W_COLS
    if gain is None:
        g = jnp.ones((1, tn), F32)
    else:
        g = jnp.tile(gain.astype(F32).reshape(1, -1), (1, tn // gain.shape[-1]))
    body = functools.partial(_proj_body, n_w=n_w, mode=mode, hd=hd, scale=scale)
    w_specs = [pl.BlockSpec((k, W_COLS), functools.partial(lambda j, i, off: (0, base + stride * j + off), off=off))
               for off in range(n_w)]
    return pl.pallas_call(
        body, name=name, grid=(n_tiles, m // tm),
        in_specs=[pl.BlockSpec((tm, k), lambda j, i: (i, 0))] + w_specs + [pl.BlockSpec((1, tn), lambda j, i: (0, 0))],
        out_specs=pl.BlockSpec((tm, tn), lambda j, i: (i, j)),
        out_shape=jax.ShapeDtypeStruct((m, n_tiles * tn), out_dtype),
        scratch_shapes=[pltpu.VMEM((k, tn), BF16)], compiler_params=_cparams(2),
    )(h, *([w] * n_w), g)


PERM_ROWS = 256


def _perm_matrix(n, dil, inverse=False):
    per = n // dil
    o = lax.broadcasted_iota(I32, (n, n), 1 if inverse else 0)
    s = lax.broadcasted_iota(I32, (n, n), 0 if inverse else 1)
    return (s == (o % per) * dil + o // per).astype(BF16)


def _qkv_body(h_ref, wq_ref, wk_ref, wv_ref, gq_ref, gk_ref, q_ref, kv_ref, tok_ref, w_scr, *, dil, scale):
    c = pl.program_id(1)
    tm = h_ref.shape[0]
    nh = HEADS_PER_GROUP_A

    @pl.when(pl.program_id(0) == 0)
    def _():
        for i, w_ref in enumerate((wq_ref, wk_ref, wv_ref)):
            @pl.when(c == i)
            def _():
                w_scr[i] = w_ref[...].astype(BF16)

    n_sub = max(1, tm // PERM_ROWS)
    sub = tm // n_sub
    per = sub // dil

    def put_perm(dst_ref, y, t, pm):
        if dil == 1:
            dst_ref[0, t * sub:(t + 1) * sub, :] = y.astype(dst_ref.dtype)
            return
        yp = jnp.dot(pm, y.astype(BF16), preferred_element_type=F32).astype(BF16)
        for r in range(dil):
            dst_ref[r, t * per:(t + 1) * per, :] = yp[r * per:(r + 1) * per]

    def put_tok(y, c0, t):
        for i in range(nh):
            tok_ref[pl.ds(t * sub * 2 * nh + c0 + i, sub, stride=2 * nh), :] = y[:, i * HEAD_DIM_A:(i + 1) * HEAD_DIM_A]

    for ci in range(3):
        @pl.when(c == ci)
        def _(ci=ci):
            pm = None if dil == 1 else _perm_matrix(sub, dil)
            for t in range(n_sub):
                acc = jnp.dot(h_ref[t * sub:(t + 1) * sub, :], w_scr[ci], preferred_element_type=F32)
                if ci == 0:
                    put_perm(q_ref, _headnorm(acc, gq_ref[...], HEAD_DIM_A, scale), t, pm)
                elif ci == 1:
                    y = _headnorm(acc, gk_ref[...], HEAD_DIM_A, 1.0)
                    put_tok(y, 0, t)
                    put_perm(kv_ref, y, t, pm)
                else:
                    put_tok(acc, nh, t)
                    put_perm(kv_ref, acc, t, pm)


def _proj_qkv(h, w, sec, g, g_qa, g_ka, n_batch, seq, tm, dil, act_dtype=BF16):
    n, k = h.shape
    ln = seq // dil
    per = tm // dil
    mt = seq // tm
    assert seq % tm == 0
    assert dil == 1 or (act_dtype == BF16 and tm % PERM_ROWS == 0 and (PERM_ROWS // dil) % 16 == 0)
    tile = lambda v: jnp.tile(v.astype(F32).reshape(1, -1), (1, GROUP_W_A // v.shape[-1]))
    wspec = lambda blk: pl.BlockSpec((k, W_COLS), lambda m, c: (0, blk))
    q, kv, tok = pl.pallas_call(
        functools.partial(_qkv_body, dil=dil, scale=HEAD_DIM_A ** -0.5), name=f"proj_qkv{g}", grid=(n // tm, 3),
        in_specs=[pl.BlockSpec((tm, k), lambda m, c: (m, 0)),
                  wspec(sec["q"] + g), wspec(sec["k"] + g), wspec(sec["v"] + g),
                  pl.BlockSpec((1, GROUP_W_A), lambda m, c: (0, 0)), pl.BlockSpec((1, GROUP_W_A), lambda m, c: (0, 0))],
        out_specs=[pl.BlockSpec((None, dil, per, GROUP_W_A), lambda m, c: (m // mt, 0, m % mt, 0)),
                   pl.BlockSpec((None, dil, per, GROUP_W_A), lambda m, c: (m // mt, 0, m % mt, jnp.maximum(c - 1, 0))),
                   pl.BlockSpec((tm * 2 * HEADS_PER_GROUP_A, LANES), lambda m, c: (m, 0))],
        out_shape=[jax.ShapeDtypeStruct((n_batch, dil, ln, GROUP_W_A), act_dtype),
                   jax.ShapeDtypeStruct((n_batch, dil, ln, 2 * GROUP_W_A), act_dtype),
                   jax.ShapeDtypeStruct((n * 2 * HEADS_PER_GROUP_A, LANES), F32)],
        scratch_shapes=[pltpu.VMEM((3, k, W_COLS), BF16)], compiler_params=_cparams(2),
    )(h, w, w, w, tile(g_qa), tile(g_ka))
    return q.reshape(n, GROUP_W_A), kv.reshape(n, 2 * GROUP_W_A), tok


def _bias_from_table(tab_ref, bidx, col):
    acc = jnp.zeros(bidx.shape, F32)
    for kb in range(N_BUCKETS):
        acc = jnp.where(bidx == kb, tab_ref[kb, col], acc)
    return acc


def _split_hi_lo(x):
    hi = x.astype(BF16)
    return hi, (x - hi.astype(F32)).astype(BF16)


def _attn_p_body(tab_ref, bidx_ref, q_ref, kp_ref, kc_ref, vp_ref, vc_ref, o_ref, bias_scr, *, g):
    first = (pl.program_id(0) == 0) & (pl.program_id(1) == 0)

    @pl.when(first)
    def _():
        bidx = bidx_ref[...]
        for h in range(HEADS_PER_GROUP_A):
            bias_scr[h] = _bias_from_table(tab_ref, bidx, g * HEADS_PER_GROUP_A + h)

    qb = pl.program_id(1)
    blk = q_ref.shape[0]
    q = q_ref[...]
    k = jnp.concatenate([kp_ref[...], kc_ref[...]], axis=0)
    v = jnp.concatenate([vp_ref[...], vc_ref[...]], axis=0)
    row = lax.broadcasted_iota(I32, (blk, 2 * blk), 0)
    col = lax.broadcasted_iota(I32, (blk, 2 * blk), 1)
    dist = blk + row - col
    valid = (dist >= 0) & (dist <= blk) & ((qb > 0) | (col >= blk))
    lane = lax.broadcasted_iota(I32, (blk, LANES), 1)
    lse_blk = jnp.zeros((blk, LANES), F32)
    outs = []
    for h in range(HEADS_PER_GROUP_A):
        hs = slice(h * HEAD_DIM_A, (h + 1) * HEAD_DIM_A)
        s = lax.dot_general(q[:, hs], k[:, hs], (((1,), (1,)), ((), ())), preferred_element_type=F32)
        s = jnp.where(valid, s + bias_scr[h], NEG_INF)
        m = jnp.max(s, axis=-1, keepdims=True)
        p = jnp.exp(s - m)
        l = jnp.sum(p, axis=-1, keepdims=True)
        o = jnp.dot(p.astype(BF16), v[:, hs], preferred_element_type=F32) / l
        outs.append(o)
        lse_blk = jnp.where(lane == h, m + jnp.log(l), lse_blk)
    lse_hi, lse_lo = _split_hi_lo(lse_blk)
    o_ref[...] = jnp.concatenate([jnp.concatenate(outs, axis=1).astype(BF16), lse_hi, lse_lo], axis=1)


O_EXT_W = GROUP_W_A + 2 * LANES


def _attn_prompt(q, kv, rel_bias, g, n_batch, seq):
    win, dil = DIL_GROUPS[g]
    blk = win // dil
    nb = seq // dil // blk
    n = n_batch * seq
    a = jnp.arange(blk)[:, None]
    c = jnp.arange(2 * blk)[None, :]
    bidx = _rel_bucket(jnp.maximum(blk + a - c, 0) * dil)

    def cur(s, j):
        return s * nb + j

    def prev(s, j):
        return s * nb + jnp.maximum(j - 1, 0)

    return pl.pallas_call(
        functools.partial(_attn_p_body, g=g), name=f"attn_p{g}", grid=(n_batch * dil, nb),
        in_specs=[
            pl.BlockSpec(memory_space=pltpu.SMEM),
            pl.BlockSpec((blk, 2 * blk), lambda s, j: (0, 0)),
            pl.BlockSpec((blk, GROUP_W_A), lambda s, j: (cur(s, j), 0)),
            pl.BlockSpec((blk, GROUP_W_A), lambda s, j: (prev(s, j), 0)),
            pl.BlockSpec((blk, GROUP_W_A), lambda s, j: (cur(s, j), 0)),
            pl.BlockSpec((blk, GROUP_W_A), lambda s, j: (prev(s, j), 1)),
            pl.BlockSpec((blk, GROUP_W_A), lambda s, j: (cur(s, j), 1)),
        ],
        out_specs=pl.BlockSpec((blk, O_EXT_W), lambda s, j: (cur(s, j), 0)),
        out_shape=jax.ShapeDtypeStruct((n, O_EXT_W), BF16),
        scratch_shapes=[pltpu.VMEM((HEADS_PER_GROUP_A, blk, 2 * blk), F32)],
        compiler_params=_cparams(2),
    )(rel_bias, bidx, q, kv, kv, kv, kv)


def _attn_s_body(tab_ref, bidx_ref, q_ref, kvn_ref, tokn_ref, cache_hbm, o_ref, newc_hbm,
                 raw, kv_scr, bias_scr, sem_in, sem_out, *, g, dil, lb, n_req):
    t_new = q_ref.shape[0]
    nh = HEADS_PER_GROUP_A
    n_rows = nh * t_new
    n_keys = kv_scr.shape[0]
    n_col = 2 * nh
    lb8 = lb * n_col
    n = pl.program_id(0)
    slot = n % 2

    def fetch(req, s):
        return pltpu.make_async_copy(cache_hbm.at[req], raw.at[s, pl.ds(0, lb8)], sem_in.at[s])

    def flush(req, s):
        return pltpu.make_async_copy(raw.at[s, pl.ds(t_new * n_col, lb8)], newc_hbm.at[req], sem_out.at[s])

    @pl.when(n == 0)
    def _():
        kv_scr[lb:, :] = jnp.zeros((n_keys - lb, kv_scr.shape[1]), BF16)
        bidx = bidx_ref[...]
        for h in range(nh):
            bias_scr[h * t_new:(h + 1) * t_new, :] = _bias_from_table(tab_ref, bidx, g * nh + h)
        fetch(0, 0).start()

    @pl.when(n >= 1)
    def _():
        flush(n - 1, 1 - slot).wait()

    @pl.when(n + 1 < n_req)
    def _():
        fetch(n + 1, 1 - slot).start()

    fetch(n, slot).wait()
    raw[slot, lb8:lb8 + t_new * n_col, :] = tokn_ref[...]
    flush(n, slot).start()
    new = kvn_ref[...]

    rows_per = min(lb, 256)
    for c in range(n_col):
        for r0 in range(0, lb, rows_per):
            kv_scr[r0:r0 + rows_per, c * LANES:(c + 1) * LANES] = raw[
                slot, pl.ds(r0 * n_col + c, rows_per, stride=n_col), :].astype(BF16)
    kv_scr[lb:lb + 2 * t_new, :] = jnp.concatenate([new, jnp.zeros_like(new)], axis=0).astype(BF16)

    q = q_ref[...]
    qt = jnp.concatenate([q] * nh, axis=0)
    rr = lax.broadcasted_iota(I32, (n_rows, GROUP_W_A), 0)
    cc = lax.broadcasted_iota(I32, (n_rows, GROUP_W_A), 1)
    qbd = jnp.where(cc // HEAD_DIM_A == rr // t_new, qt, 0.0).astype(BF16)

    s = lax.dot_general(qbd, kv_scr[:, :GROUP_W_A], (((1,), (1,)), ((), ())), preferred_element_type=F32)
    row = lax.broadcasted_iota(I32, (n_rows, n_keys), 0)
    col = lax.broadcasted_iota(I32, (n_rows, n_keys), 1)
    delta = lb + (row & (t_new - 1)) - col
    valid = (delta >= 0) & (delta <= lb) & ((delta & (dil - 1)) == 0)
    s = jnp.where(valid, s + bias_scr[...], NEG_INF)
    m = jnp.max(s, axis=-1, keepdims=True)
    p = jnp.exp(s - m)
    l = jnp.sum(p, axis=-1, keepdims=True)
    o = jnp.dot(p.astype(BF16), kv_scr[:, GROUP_W_A:], preferred_element_type=F32) / l
    lse = m + jnp.log(l)
    lane = lax.broadcasted_iota(I32, (t_new, LANES), 1)
    lse_blk = jnp.zeros((t_new, LANES), F32)
    outs = []
    for h in range(nh):
        outs.append(o[h * t_new:(h + 1) * t_new, h * HEAD_DIM_A:(h + 1) * HEAD_DIM_A])
        lse_blk = jnp.where(lane == h, lse[h * t_new:(h + 1) * t_new, :], lse_blk)
    o_ref[...] = jnp.concatenate(outs + [lse_blk, jnp.zeros_like(lse_blk)], axis=1)

    @pl.when(n == n_req - 1)
    def _():
        flush(n, slot).wait()


def _attn_sample(q, kv_new, tok_new, cache, rel_bias, g, t_new):
    win, dil = DIL_GROUPS[g]
    n_col = 2 * HEADS_PER_GROUP_A
    n_req, lb = cache.shape[0], cache.shape[1] // n_col
    assert t_new & (t_new - 1) == 0 and dil & (dil - 1) == 0 and lb == win
    n_keys = lb + LANES
    t = jnp.arange(t_new)[:, None]
    c = jnp.arange(n_keys)[None, :]
    bidx = _rel_bucket(jnp.clip(lb + t - c, 0, lb))
    return pl.pallas_call(
        functools.partial(_attn_s_body, g=g, dil=dil, lb=lb, n_req=n_req), name=f"attn_s{g}", grid=(n_req,),
        in_specs=[
            pl.BlockSpec(memory_space=pltpu.SMEM),
            pl.BlockSpec((t_new, n_keys), lambda n: (0, 0)),
            pl.BlockSpec((t_new, GROUP_W_A), lambda n: (n, 0)),
            pl.BlockSpec((t_new, 2 * GROUP_W_A), lambda n: (n, 0)),
            pl.BlockSpec((t_new * n_col, LANES), lambda n: (n, 0)),
            pl.BlockSpec(memory_space=pl.ANY),
        ],
        out_specs=[pl.BlockSpec((t_new, O_EXT_W), lambda n: (n, 0)), pl.BlockSpec(memory_space=pl.ANY)],
        out_shape=[jax.ShapeDtypeStruct((n_req * t_new, O_EXT_W), F32), jax.ShapeDtypeStruct(cache.shape, F32)],
        scratch_shapes=[pltpu.VMEM((2, (lb + t_new) * n_col, LANES), F32),
                        pltpu.VMEM((n_keys, 2 * GROUP_W_A), BF16),
                        pltpu.VMEM((HEADS_PER_GROUP_A * t_new, n_keys), F32),
                        pltpu.SemaphoreType.DMA((2,)), pltpu.SemaphoreType.DMA((2,))],
        compiler_params=_cparams(1),
    )(rel_bias, bidx, q, kv_new, tok_new, cache)


def _gmlp_body(u_ref, v_ref, w_ref, b_ref, o_ref, *, period):
    c = w_ref.shape[1]
    nch = u_ref.shape[0] // c
    gd = u_ref.shape[1] // N_GROUPS_B
    i = lax.broadcasted_iota(I32, (c, c), 0)
    j = lax.broadcasted_iota(I32, (c, c), 1)
    mask = (j <= i) & ((i // period) == (j // period))
    b = b_ref[...]
    for g in range(N_GROUPS_B):
        gs = slice(g * gd, (g + 1) * gd)
        wg = jnp.where(mask, w_ref[g], 0.0).astype(BF16)
        vg = [v_ref[ch * c:(ch + 1) * c, gs].astype(BF16) for ch in range(nch)]
        vg = vg[0] if nch == 1 else jnp.concatenate(vg, axis=1)
        sg = jnp.dot(wg, vg, preferred_element_type=F32) + b[:, g:g + 1]
        for ch in range(nch):
            u = u_ref[ch * c:(ch + 1) * c, gs].astype(F32)
            o_ref[ch * c:(ch + 1) * c, gs] = (u * sg[:, ch * gd:(ch + 1) * gd]).astype(o_ref.dtype)


def _gmlp(u, v, w, b, tm, period):
    m, wb = u.shape
    c = w.shape[1]
    return pl.pallas_call(
        functools.partial(_gmlp_body, period=period), name="gmlp", grid=(m // tm,),
        in_specs=[pl.BlockSpec((tm, wb), lambda i: (i, 0)), pl.BlockSpec((tm, wb), lambda i: (i, 0)),
                  pl.BlockSpec((N_GROUPS_B, c, c), lambda i: (0, 0, 0)),
                  pl.BlockSpec((c, N_GROUPS_B), lambda i: (0, 0))],
        out_specs=pl.BlockSpec((tm, wb), lambda i: (i, 0)),
        out_shape=jax.ShapeDtypeStruct((m, wb), BF16), compiler_params=_cparams(1),
    )(u, v, w, b)


def _gmlp_new_body(u_ref, v_ref, w_ref, b_ref, o_ref):
    t = w_ref.shape[0]
    n = u_ref.shape[0] // t
    width = u_ref.shape[1]
    v = v_ref[...].astype(F32).reshape(n, t, width)
    sg = jnp.broadcast_to(b_ref[...][None], (n, t, width))
    p_row = lax.broadcasted_iota(I32, (t, width), 0)
    for q in range(t):
        sg = sg + jnp.where(p_row >= q, w_ref[q], 0.0)[None] * v[:, q:q + 1, :]
    o_ref[...] = (u_ref[...].astype(F32).reshape(n, t, width) * sg).reshape(n * t, width).astype(o_ref.dtype)


def _gmlp_new(u, v, w_s, b_s, t):
    m, wb = u.shape
    gd = wb // N_GROUPS_B
    wq = jnp.repeat(jnp.transpose(w_s[:, :t, :t], (2, 1, 0)), gd, axis=-1)
    bq = jnp.repeat(b_s[:, :t].T, gd, axis=-1)
    whole = lambda a: pl.BlockSpec(a.shape, lambda i: (0,) * a.ndim)
    return pl.pallas_call(
        _gmlp_new_body, name="gmlp_new", grid=(1,),
        in_specs=[whole(u), whole(v), whole(wq), whole(bq)], out_specs=whole(u),
        out_shape=jax.ShapeDtypeStruct((m, wb), BF16), compiler_params=_cparams(1),
    )(u, v, wq, bq)


def _memattn_body(q_ref, kv_ref, o_ref, *, tiles):
    wm = N_HEADS_M * HEAD_DIM_M
    n_ct = HEAD_DIM_M // LANES
    per_key = 2 * n_ct * N_HEADS_M
    q = q_ref[...].astype(BF16)

    def head_slab(kv, h):
        if not tiles:
            return kv_ref[:, kv * wm + h * HEAD_DIM_M:kv * wm + (h + 1) * HEAD_DIM_M].astype(BF16)
        n_keys = kv_ref.shape[0] // per_key
        parts = [kv_ref[pl.ds((kv * n_ct + ct) * N_HEADS_M + h, n_keys, stride=per_key), :] for ct in range(n_ct)]
        return jnp.concatenate(parts, axis=1).astype(BF16)

    outs = []
    for h in range(N_HEADS_M):
        hs = slice(h * HEAD_DIM_M, (h + 1) * HEAD_DIM_M)
        k = head_slab(0, h)
        v = head_slab(1, h)
        s = lax.dot_general(q[:, hs], k, (((1,), (1,)), ((), ())), preferred_element_type=F32)
        m = jnp.max(s, axis=-1, keepdims=True)
        p = jnp.exp(s - m)
        l = jnp.sum(p, axis=-1, keepdims=True)
        outs.append(jnp.dot(p.astype(BF16), v, preferred_element_type=F32) / l)
    o_ref[...] = jnp.concatenate(outs, axis=1).astype(o_ref.dtype)


def _memattn(q, kv, tq, out_dtype, tiles=False):
    m, wm = q.shape
    n = kv.shape[0]
    per = m // n // tq
    kv_spec = pl.BlockSpec((None,) + kv.shape[1:], lambda i: (i // per,) + (0,) * (kv.ndim - 1))
    return pl.pallas_call(
        functools.partial(_memattn_body, tiles=tiles), name="memattn", grid=(m // tq,),
        in_specs=[pl.BlockSpec((tq, wm), lambda i: (i, 0)), kv_spec],
        out_specs=pl.BlockSpec((tq, wm), lambda i: (i, 0)),
        out_shape=jax.ShapeDtypeStruct((m, wm), out_dtype), compiler_params=_cparams(1),
    )(q, kv)


def _mix_body(gt_ref, o0_ref, o1_ref, o2_ref, ob_ref, om_ref, wpa_ref, wpb_ref, wpm_ref, z_ref, *, dils):
    d = z_ref.shape[1]
    tm = z_ref.shape[0]
    os_, ls = [], []
    for o_ref, dil in zip((o0_ref, o1_ref, o2_ref), dils):
        if dil == 1:
            x = o_ref[...].astype(F32)
        else:
            rows = jnp.concatenate([o_ref[r] for r in range(dil)], axis=0)
            x = jnp.dot(_perm_matrix(tm, dil, inverse=True), rows, preferred_element_type=F32)
        os_.append(x[:, :GROUP_W_A])
        ls.append(x[:, GROUP_W_A:GROUP_W_A + LANES] + x[:, GROUP_W_A + LANES:])
    l0, l1, l2 = ls
    mx = jnp.maximum(jnp.maximum(l0, l1), l2)
    e0, e1, e2 = jnp.exp(l0 - mx), jnp.exp(l1 - mx), jnp.exp(l2 - mx)
    den = e0 + e1 + e2
    w0, w1, w2 = e0 / den, e1 / den, e2 / den
    cols = []
    for h in range(HEADS_PER_GROUP_A):
        hs = slice(h * HEAD_DIM_A, (h + 1) * HEAD_DIM_A)
        cols.append(w0[:, h:h + 1] * os_[0][:, hs] + w1[:, h:h + 1] * os_[1][:, hs] + w2[:, h:h + 1] * os_[2][:, hs])
    oa = jnp.concatenate(cols, axis=1).astype(BF16)
    pa = jnp.dot(oa, wpa_ref[...], preferred_element_type=F32)
    pb = jnp.dot(ob_ref[...].astype(BF16), wpb_ref[...], preferred_element_type=F32)
    pm = jnp.dot(om_ref[...].astype(BF16), wpm_ref[...], preferred_element_type=F32)
    z = (gt_ref[:, 0:d].astype(F32) * pa + gt_ref[:, d:2 * d].astype(F32) * pb
         + gt_ref[:, 2 * d:3 * d].astype(F32) * pm)
    z_ref[...] = z.astype(z_ref.dtype)


def _mix(gates, o_list, dils, seq, ob, om, wpa, wpb, wpm, tm):
    m = gates.shape[0]
    d = wpa.shape[1]
    mt = seq // tm if any(dl > 1 for dl in dils) else 1

    def rows(width):
        return pl.BlockSpec((tm, width), lambda i: (i, 0))

    def whole(a):
        return pl.BlockSpec(a.shape, lambda i: (0, 0))

    o_specs, o_args = [], []
    for o, dil in zip(o_list, dils):
        if dil == 1:
            o_specs.append(rows(O_EXT_W))
            o_args.append(o)
        else:
            assert seq % tm == 0 and (tm // dil) % 16 == 0
            o_specs.append(pl.BlockSpec((None, dil, tm // dil, O_EXT_W), lambda i: (i // mt, 0, i % mt, 0)))
            o_args.append(o.reshape(m // seq, dil, seq // dil, O_EXT_W))
    return pl.pallas_call(
        functools.partial(_mix_body, dils=tuple(dils)), name="mix", grid=(m // tm,),
        in_specs=[rows(gates.shape[1])] + o_specs
                 + [rows(ob.shape[1]), rows(om.shape[1]), whole(wpa), whole(wpb), whole(wpm)],
        out_specs=rows(d),
        out_shape=jax.ShapeDtypeStruct((m, d), BF16), compiler_params=_cparams(1),
    )(gates, *o_args, ob, om, wpa, wpb, wpm)


def _route(logits):
    lane = lax.broadcasted_iota(I32, logits.shape, 1)
    lane_f = lane.astype(F32)
    is_g = lane < N_EXPERT_GROUPS
    gmax = jnp.max(jnp.where(is_g, logits, -jnp.inf), axis=1, keepdims=True)
    gsel = jnp.min(jnp.where(is_g & (logits == gmax), lane_f, float(LANES)), axis=1, keepdims=True).astype(I32)
    gden = jnp.sum(jnp.where(is_g, jnp.exp(logits - gmax), 0.0), axis=1, keepdims=True)
    pg = 1.0 / gden
    e_lane = lane - N_EXPERT_GROUPS
    in_grp = (e_lane >= 0) & (e_lane < N_EXPERTS) & ((e_lane // EXPERTS_PER_GROUP) == gsel)
    m1 = jnp.max(jnp.where(in_grp, logits, -jnp.inf), axis=1, keepdims=True)
    i1 = jnp.min(jnp.where(in_grp & (logits == m1), lane_f, float(LANES)), axis=1, keepdims=True).astype(I32)
    rest = in_grp & (lane != i1)
    m2 = jnp.max(jnp.where(rest, logits, -jnp.inf), axis=1, keepdims=True)
    i2 = jnp.min(jnp.where(rest & (logits == m2), lane_f, float(LANES)), axis=1, keepdims=True).astype(I32)
    e2 = jnp.exp(m2 - m1)
    w1 = pg / (1.0 + e2)
    w2 = pg * e2 / (1.0 + e2)
    eid = jnp.where(lane == 0, i1 - N_EXPERT_GROUPS, jnp.where(lane == 1, i2 - N_EXPERT_GROUPS, 0))
    wts = jnp.where(lane == 0, w1, jnp.where(lane == 1, w2, 0.0))
    return eid, wts


def _resid_body(x_ref, z_ref, wo_ref, gf_ref, wr_ref, br_ref, xmid_ref, hpk_ref, eid_ref, wts_ref):
    xm = x_ref[...] + jnp.dot(z_ref[...], wo_ref[...], preferred_element_type=F32)
    xmid_ref[...] = xm
    hf = _rms(xm, gf_ref[...])
    wr = wr_ref[...]
    wr_hi = wr.astype(BF16)
    wr_lo = (wr - wr_hi.astype(F32)).astype(BF16)
    hf_hi = hf.astype(BF16)
    hf_lo = (hf - hf_hi.astype(F32)).astype(BF16)
    logits = (jnp.dot(hf_hi, wr_hi, preferred_element_type=F32) + jnp.dot(hf_hi, wr_lo, preferred_element_type=F32)
              + jnp.dot(hf_lo, wr_hi, preferred_element_type=F32) + br_ref[...])
    eid, wts = _route(logits)
    eid_ref[...] = eid
    wts_ref[...] = wts
    _store_token_tiles(hpk_ref, hf)


def _resid(x2d, z, wo, g_ffn, w_r, b_r, tm):
    m, d = x2d.shape

    def rows(width):
        return pl.BlockSpec((tm, width), lambda i: (i, 0))

    def whole(a):
        return pl.BlockSpec(a.shape, lambda i: (0, 0))

    gf = g_ffn.reshape(1, d)
    nt = d // 2 // LANES
    return pl.pallas_call(
        _resid_body, name="resid", grid=(m // tm,),
        in_specs=[rows(d), rows(d), whole(wo), whole(gf), whole(w_r), whole(b_r)],
        out_specs=[rows(d), pl.BlockSpec((tm * nt, LANES), lambda i: (i, 0)), rows(LANES), rows(LANES)],
        out_shape=[jax.ShapeDtypeStruct((m, d), F32), jax.ShapeDtypeStruct((m * nt, LANES), U32),
                   jax.ShapeDtypeStruct((m, LANES), I32), jax.ShapeDtypeStruct((m, LANES), F32)],
        compiler_params=_cparams(1),
    )(x2d, z, wo, gf, w_r, b_r)


TOKEN_TILE = 8
GATHER_UNROLL = 8


def _gather_rows(idx_at, src_hbm, dst, sem, n):
    def body(i, carry):
        src = pl.multiple_of(idx_at(i), TOKEN_TILE)
        dst_row = pl.multiple_of(i * TOKEN_TILE, TOKEN_TILE)
        pltpu.make_async_copy(src_hbm.at[pl.ds(src, TOKEN_TILE)], dst.at[pl.ds(dst_row, TOKEN_TILE)], sem).start()
        return carry
    lax.fori_loop(0, n, body, 0, unroll=GATHER_UNROLL)


FFN_CHUNKS = 8
FFN_BUFS = 3


def _ffn_body(blk_e_ref, nused_ref, eord_ref, enext_ref, idx_hbm, h_hbm, wg_hbm, wu_hbm, wd_hbm, y_ref,
              idx_s, isem, buf, sem, x_scr, a_scr, b_scr, hm_scr, wg_f, wu_f, wd_f, wsem, wg_s, wu_s, wd_s):
    j = pl.program_id(0)
    nused = nused_ref[0]
    last = nused - 1
    rows = buf.shape[1] // TOKEN_TILE
    e = blk_e_ref[j]

    def idx_copy(b):
        s = b % FFN_BUFS
        return pltpu.make_async_copy(idx_hbm.at[jnp.minimum(b, last)], idx_s.at[s], isem.at[s])

    def wait_rows(s):
        pltpu.make_async_copy(h_hbm.at[pl.ds(0, rows * TOKEN_TILE)], buf.at[s], sem.at[s]).wait()
    new_expert = (j == 0) | (e != blk_e_ref[jnp.maximum(j - 1, 0)])
    ws = eord_ref[j] % 2

    def fetch_w(expert, s):
        return [pltpu.make_async_copy(src.at[expert], dst.at[s], wsem.at[s, i])
                for i, (src, dst) in enumerate(((wg_hbm, wg_f), (wu_hbm, wu_f), (wd_hbm, wd_f)))]

    @pl.when(j == 0)
    def _():
        for c in fetch_w(e, ws):
            c.start()
        for b in range(FFN_BUFS):
            idx_copy(b).start()
        for b in range(FFN_BUFS - 1):
            idx_copy(b).wait()
            _gather_rows(lambda i, b=b: idx_s[b, 0, i], h_hbm, buf.at[b], sem.at[b], rows)

    @pl.when(new_expert)
    def _():
        nxt = enext_ref[j]

        @pl.when(nxt >= 0)
        def _():
            for c in fetch_w(nxt, 1 - ws):
                c.start()

        for c in fetch_w(e, ws):
            c.wait()
        wg_s[...] = wg_f[ws].astype(BF16)
        wu_s[...] = wu_f[ws].astype(BF16)
        wd_s[...] = wd_f[ws].astype(BF16)

    @pl.when(j < nused)
    def _():
        slot = j % FFN_BUFS
        s2 = (j + 2) % FFN_BUFS
        per = rows // FFN_CHUNKS
        idx_copy(j + 2).wait()

        @pl.when(j < last)
        def _():
            idx_copy(j + 3).start()

        def issue(c):
            for i in range(c * per, (c + 1) * per):
                src = pl.multiple_of(idx_s[s2, 0, i], TOKEN_TILE)
                pltpu.make_async_copy(h_hbm.at[pl.ds(src, TOKEN_TILE)],
                                      buf.at[s2, pl.ds(i * TOKEN_TILE, TOKEN_TILE)], sem.at[s2]).start()

        wait_rows(slot)
        x_scr[...] = _load_token_tiles(buf, slot, 0, rows, TOKEN_TILE).astype(BF16)
        dn = a_scr.shape[1] // 2
        for c in range(2):
            issue(c)
            a_scr[:, c * dn:(c + 1) * dn] = jnp.dot(x_scr[...], wg_s[:, c * dn:(c + 1) * dn],
                                                     preferred_element_type=F32)
        for c in range(2):
            issue(2 + c)
            b_scr[:, c * dn:(c + 1) * dn] = jnp.dot(x_scr[...], wu_s[:, c * dn:(c + 1) * dn],
                                                     preferred_element_type=F32)
        a = a_scr[...]
        hm_scr[...] = (a * jax.nn.sigmoid(a) * b_scr[...]).astype(BF16)
        half = wd_s.shape[1] // 2
        n_dc = FFN_CHUNKS - 4
        dq = half // n_dc
        for c in range(n_dc):
            issue(4 + c)
            lo = jnp.dot(hm_scr[...], wd_s[:, c * dq:(c + 1) * dq], preferred_element_type=F32)
            hi = jnp.dot(hm_scr[...], wd_s[:, half + c * dq:half + (c + 1) * dq], preferred_element_type=F32)
            w = _pack_bf16_pair(jnp.concatenate([lo, hi], axis=1))
            for t in range(dq // LANES):
                y_ref[pl.ds(c * (dq // LANES) + t, rows, stride=TOKEN_TILE), :] = w[:, t * LANES:(t + 1) * LANES]

        @pl.when(j == last)
        def _():
            wait_rows((j + 1) % FFN_BUFS)
            wait_rows(s2)

    @pl.when(j >= nused)
    def _():
        y_ref[...] = jnp.zeros(y_ref.shape, y_ref.dtype)


def _ffn(hpk, src_tok, blk_e, nused, eord, enext, wg, wu, wd):
    n_blocks = blk_e.shape[0]
    rows = MOE_ROWS
    tiles = rows * TOKEN_TILE
    d, de = wg.shape[1], wg.shape[2]
    assert d // 2 == TOKEN_TILE * LANES and rows % FFN_CHUNKS == 0 and (d // 2) % ((FFN_CHUNKS - 4) * LANES) == 0
    assert n_blocks >= N_EXPERTS + FFN_BUFS
    idx = (src_tok * TOKEN_TILE).reshape(n_blocks, 1, rows)
    hbm = pl.BlockSpec(memory_space=pl.ANY)
    grid_spec = pltpu.PrefetchScalarGridSpec(
        num_scalar_prefetch=4, grid=(n_blocks,),
        in_specs=[hbm, hbm, hbm, hbm, hbm],
        out_specs=pl.BlockSpec((tiles, LANES), lambda j, *_: (j, 0)),
        scratch_shapes=[pltpu.SMEM((FFN_BUFS, 1, rows), I32), pltpu.SemaphoreType.DMA((FFN_BUFS,)),
                        pltpu.VMEM((FFN_BUFS, tiles, LANES), U32), pltpu.SemaphoreType.DMA((FFN_BUFS,)),
                        pltpu.VMEM((rows, d), BF16), pltpu.VMEM((rows, de), F32), pltpu.VMEM((rows, de), F32),
                        pltpu.VMEM((rows, de), BF16),
                        pltpu.VMEM((2, d, de), wg.dtype), pltpu.VMEM((2, d, de), wu.dtype),
                        pltpu.VMEM((2, de, d), wd.dtype), pltpu.SemaphoreType.DMA((2, 3)),
                        pltpu.VMEM((d, de), BF16), pltpu.VMEM((d, de), BF16), pltpu.VMEM((de, d), BF16)],
    )
    return pl.pallas_call(
        _ffn_body, name="ffn", grid_spec=grid_spec,
        out_shape=jax.ShapeDtypeStruct((n_blocks * tiles, LANES), U32), compiler_params=_cparams(1),
    )(blk_e, nused, eord, enext, idx, hpk, wg, wu, wd)


def _combine_body(idx_cur_ref, idx_nxt_ref, x_ref, w_ref, y_hbm, o_ref, buf, sem, *, n):
    i = pl.program_id(0)
    slot = i % 2
    rows = buf.shape[1] // TOKEN_TILE
    tm = rows // TOP_K

    @pl.when(i == 0)
    def _():
        _gather_rows(lambda r: idx_cur_ref[0, 0, r], y_hbm, buf.at[0], sem.at[0], rows)

    if n > 1:
        @pl.when(i + 1 < n)
        def _():
            _gather_rows(lambda r: idx_nxt_ref[0, 0, r], y_hbm, buf.at[1 - slot], sem.at[1 - slot], rows)

    pltpu.make_async_copy(y_hbm.at[pl.ds(0, rows * TOKEN_TILE)], buf.at[slot], sem.at[slot]).wait()
    w = w_ref[...]
    y0 = _load_token_tiles(buf, slot, 0, tm, TOKEN_TILE)
    y1 = _load_token_tiles(buf, slot, tm, tm, TOKEN_TILE)
    o_ref[...] = x_ref[...] + (w[:, 0:1] * y0 + w[:, 1:2] * y1)


def _combine(xmid, wts, dest, ypk, tm):
    m, d = xmid.shape
    nt = m // tm
    assert d // 2 == TOKEN_TILE * LANES
    idx = (dest * TOKEN_TILE).reshape(nt, tm, TOP_K).transpose(0, 2, 1).reshape(nt, 1, TOP_K * tm)

    def idx_spec(ahead):
        return pl.BlockSpec((1, 1, TOP_K * tm), lambda i: (jnp.minimum(i + ahead, nt - 1), 0, 0),
                            memory_space=pltpu.SMEM)

    return pl.pallas_call(
        functools.partial(_combine_body, n=nt), name="combine", grid=(nt,),
        in_specs=[idx_spec(0), idx_spec(1), pl.BlockSpec((tm, d), lambda i: (i, 0)),
                  pl.BlockSpec((tm, LANES), lambda i: (i, 0)), pl.BlockSpec(memory_space=pl.ANY)],
        out_specs=pl.BlockSpec((tm, d), lambda i: (i, 0)),
        out_shape=jax.ShapeDtypeStruct((m, d), F32),
        scratch_shapes=[pltpu.VMEM((2, TOP_K * tm * TOKEN_TILE, LANES), U32), pltpu.SemaphoreType.DMA((2,))],
        compiler_params=_cparams(1),
    )(idx, idx, xmid, wts, ypk)


def _moe_plan(eid2):
    n_slot = eid2.shape[0] * TOP_K
    eid = eid2.reshape(n_slot)
    onehot = (eid[:, None] == jnp.arange(N_EXPERTS, dtype=I32)[None, :]).astype(I32)
    csum = jnp.cumsum(onehot, axis=0)
    rank = jnp.take_along_axis(csum, eid[:, None], axis=1)[:, 0] - 1
    counts = csum[-1]
    padded = (counts + MOE_ROWS - 1) // MOE_ROWS * MOE_ROWS
    pend = jnp.cumsum(padded)
    pstart = pend - padded
    dest = pstart[eid] + rank
    n_blocks = -(-n_slot // MOE_ROWS) + N_EXPERTS
    nused = (pend[-1] // MOE_ROWS).astype(I32)
    blk = jnp.minimum(jnp.arange(n_blocks, dtype=I32), nused - 1)
    blk_e = jnp.minimum(jnp.searchsorted(pend, blk * MOE_ROWS, side="right"), N_EXPERTS - 1).astype(I32)
    n_rows = n_blocks * MOE_ROWS
    src_tok = (jnp.arange(n_rows, dtype=I32) % (n_slot // TOP_K)).at[dest].set(jnp.arange(n_slot, dtype=I32) // TOP_K)
    has_rows = counts > 0
    ord_e = jnp.cumsum(has_rows.astype(I32)) - 1
    ids = jnp.arange(N_EXPERTS, dtype=I32)
    later = has_rows[None, :] & (ids[None, :] > ids[:, None])
    next_e = jnp.where(later.any(axis=1), jnp.argmax(later, axis=1), -1).astype(I32)
    return dest.astype(I32), src_tok, blk_e, nused.reshape(1), ord_e[blk_e].astype(I32), next_e[blk_e]


def _mixer(x2d, tm, p, attn_fn, gmlp_fn, mem_fn):
    h = _prep(x2d, p["g_mix"], tm)
    qdt = p["q_dtype"]
    wi, sec = p["w_in"], p["sec"]
    o_list, dils, seq = attn_fn(h)
    ub = _proj(h, wi, (sec["u"], 1, 1, sec["vb"] - sec["u"]), tm, "none", BF16, name="proj_ub")
    n_vb = sec["qm"] - sec["vb"]
    vb = _proj(h, wi, (sec["vb"], n_vb, n_vb, 1), tm, "norm", p["vb_dtype"], p["g_vb"], n_vb * W_COLS,
               name="proj_vb")
    qm = _proj(h, wi, (sec["qm"], 1, 1, sec["gt"] - sec["qm"]), tm, "norm", qdt, p["g_qm"], HEAD_DIM_M,
               HEAD_DIM_M ** -0.5, name="proj_qm")
    gates = _proj(h, wi, (sec["gt"], 2, 2, (sec["end"] - sec["gt"]) // 2), tm, "sigmoid", BF16, name="proj_gates")
    ob = gmlp_fn(ub, vb)
    om = mem_fn(qm)
    z = _mix(gates, o_list, dils, seq, ob, om, p["w_pa"], p["w_pb"], p["w_pm"], min(tm, 256))
    xmid, hpk, eid, wts = _resid(x2d, z, p["w_o"], p["g_ffn"], p["w_r"], p["b_r"], min(tm, 256))
    return vb, xmid, hpk, eid, wts


def kernel(x_prompt, x_sample, mem_prompt, cache_a0_kv, cache_a1_kv, cache_a2_kv, cache_mem_kv, rel_bias, g_mix, w_in, g_qa, g_ka, w_pa, g_vb, w_s, b_s, w_pb, g_mem, w_mk, w_mv, g_qm, g_km, w_pm, w_o, g_ffn, w_rg, b_rg, w_re, b_re, w_gate, w_up, w_down):
    n_b, seq, d = x_prompt.shape
    n_s, t_s, _ = x_sample.shape
    depth = w_in.shape[0]
    assert depth == 1
    l = 0
    caches = (cache_a0_kv, cache_a1_kv, cache_a2_kv)
    n_g = len(DIL_GROUPS)
    wb = w_pb.shape[1]
    wm = N_HEADS_M * HEAD_DIM_M

    offs = [0, WIDTH_A, 2 * WIDTH_A, 3 * WIDTH_A, 3 * WIDTH_A + wb, 3 * WIDTH_A + 2 * wb,
            3 * WIDTH_A + 2 * wb + wm, w_in.shape[2]]
    assert all(o % W_COLS == 0 for o in offs) and (offs[7] - offs[6]) % (2 * W_COLS) == 0
    sec = dict(zip(("q", "k", "v", "u", "vb", "qm", "gt", "end"), (o // W_COLS for o in offs)))
    p = {
        "g_mix": g_mix[l], "g_qa": g_qa[l], "g_ka": g_ka[l], "g_vb": g_vb[l], "g_qm": g_qm[l], "g_ffn": g_ffn[l],
        "w_in": w_in[l], "sec": sec,
        "w_pa": w_pa[l].astype(BF16), "w_pb": w_pb[l].astype(BF16), "w_pm": w_pm[l].astype(BF16),
        "w_o": w_o[l].astype(BF16),
    }
    n_r = N_EXPERT_GROUPS + N_EXPERTS
    p["w_r"] = jnp.concatenate([w_rg[l], w_re[l], jnp.zeros((d, LANES - n_r), F32)], axis=1)
    p["b_r"] = jnp.concatenate([b_rg[l], b_re[l], jnp.zeros((LANES - n_r,), F32)]).reshape(1, LANES)

    n_mem = mem_prompt.shape[1]
    h_mem = _prep(mem_prompt.reshape(n_b * n_mem, d), g_mem[l], 256)
    w_mkv = jnp.concatenate([w_mk[l], w_mv[l]], axis=1)
    n_mw = wm // W_COLS
    mkv = _proj(h_mem, w_mkv, (0, n_mw, n_mw, 2), 256, "norm_first", F32, g_km[l], HEAD_DIM_M, name="proj_mkv")
    new_mem_p = mkv.reshape(1, n_b, n_mem, 2, N_HEADS_M, HEAD_DIM_M)

    ps = dict(p, q_dtype=F32, vb_dtype=F32)
    m_s = n_s * t_s
    n_col = 2 * HEADS_PER_GROUP_A
    new_s = []

    def attn_sample(h):
        o_list = []
        for g in range(n_g):
            q, kv, tok = _proj_qkv(h, w_in[l], sec, g, g_qa[l], g_ka[l], 1, m_s, m_s, 1, act_dtype=F32)
            cache3 = caches[g][l].reshape(n_s, caches[g].shape[2] * n_col, LANES)
            o, newc = _attn_sample(q, kv, tok, cache3, rel_bias, g, t_s)
            o_list.append(o)
            new_s.append(newc.reshape(caches[g][l:l + 1].shape))
        return o_list, [1] * n_g, t_s

    n_ct = HEAD_DIM_M // LANES
    mem_tiles = (cache_mem_kv[l].reshape(n_s, n_mem, 2, N_HEADS_M, n_ct, LANES).transpose(0, 1, 2, 4, 3, 5)
                 .reshape(n_s, n_mem * 2 * n_ct * N_HEADS_M, LANES))
    vb_s, xmid_s, hpk_s, eid_s, wts_s = _mixer(
        x_sample.reshape(m_s, d), m_s, ps, attn_sample,
        lambda u, v: _gmlp_new(u, v, w_s[l], b_s[l], t_s),
        lambda qm: _memattn(qm, mem_tiles, t_s, F32, tiles=True))
    new_vb_s = vb_s.reshape(1, n_s, t_s, wb)

    pp = dict(p, q_dtype=BF16, vb_dtype=BF16)
    new_p = []

    def attn_prompt(h):
        o_list = []
        for g, (win, dil) in enumerate(DIL_GROUPS):
            q, kv, tok = _proj_qkv(h, w_in[l], sec, g, g_qa[l], g_ka[l], n_b, seq, 512, dil)
            o_list.append(_attn_prompt(q, kv, rel_bias, g, n_b, seq))
            keep = min(win, seq)
            kv5 = tok.reshape(n_b, seq, 2, HEADS_PER_GROUP_A, HEAD_DIM_A)
            new_p.append((kv5 if keep == seq else kv5[:, seq - keep:])[None])
        return o_list, [dl for _, dl in DIL_GROUPS], seq

    _, xmid_p, hpk_p, eid_p, wts_p = _mixer(
        x_prompt.reshape(n_b * seq, d), 512, pp, attn_prompt,
        lambda u, v: _gmlp(u, v, w_s[l], b_s[l].T, 512, CHUNK),
        lambda qm: _memattn(qm, mkv.reshape(n_b, n_mem, 2 * wm), 256, BF16))

    n_p = n_b * seq
    hpk = jnp.concatenate([hpk_p, hpk_s], axis=0)
    eid2 = jnp.concatenate([eid_p[:, :TOP_K], eid_s[:, :TOP_K]], axis=0)
    dest, src_tok, blk_e, nused, eord, enext = _moe_plan(eid2)
    ypk = _ffn(hpk, src_tok, blk_e, nused, eord, enext, w_gate[l], w_up[l], w_down[l])
    dest2 = dest.reshape(-1, TOP_K)
    y_p = _combine(xmid_p, wts_p, dest2[:n_p], ypk, 256)
    y_s = _combine(xmid_s, wts_s, dest2[n_p:], ypk, m_s)

    return (y_p.reshape(n_b, seq, d), y_s.reshape(n_s, t_s, d), new_p[0], new_p[1], new_p[2], new_mem_p,
            new_s[0], new_s[1], new_s[2], new_vb_s)
```

```python
import functools
import math

import jax
import jax.numpy as jnp
from jax import lax
from jax.experimental import pallas as pl
from jax.experimental.pallas import tpu as pltpu

F32 = jnp.float32
BF16 = jnp.bfloat16
I32 = jnp.int32
U32 = jnp.uint32

EPS = 1e-6
NEG_INF = -1e30

HEAD_DIM_A = 128
HEADS_PER_GROUP_A = 4
DIL_GROUPS = ((128, 1), (512, 4), (2048, 16))
GROUP_W_A = HEADS_PER_GROUP_A * HEAD_DIM_A
WIDTH_A = len(DIL_GROUPS) * GROUP_W_A
CHUNK = 128
N_GROUPS_B = 8
N_HEADS_M = 4
HEAD_DIM_M = 256
N_BUCKETS = 32
MAX_EXACT = N_BUCKETS // 2
MAX_DISTANCE = 2048
N_EXPERT_GROUPS = 4
EXPERTS_PER_GROUP = 8
N_EXPERTS = N_EXPERT_GROUPS * EXPERTS_PER_GROUP
TOP_K = 2
LANES = 128
MOE_ROWS = 256
VMEM_LIMIT = 56 * 1024 * 1024


def _cparams(n_grid, vmem=VMEM_LIMIT):
    return pltpu.CompilerParams(dimension_semantics=("arbitrary",) * n_grid, vmem_limit_bytes=vmem)


def _rms(x, g):
    return x * lax.rsqrt(jnp.mean(x * x, axis=-1, keepdims=True) + EPS) * g


def _pack_bf16_pair(x):
    n = x.shape[1] // 2
    lo = lax.bitcast_convert_type(x[:, :n].astype(BF16).astype(F32), U32)
    hi = lax.bitcast_convert_type(x[:, n:].astype(BF16).astype(F32), U32)
    return (hi & jnp.uint32(0xFFFF0000)) | (lo >> 16)


def _unpack_bf16_pair(w):
    lo = lax.bitcast_convert_type(w << 16, F32)
    hi = lax.bitcast_convert_type(w & jnp.uint32(0xFFFF0000), F32)
    return lo, hi


def _store_token_tiles(ref, x):
    w = _pack_bf16_pair(x)
    m, n = w.shape
    nt = n // LANES
    for c in range(nt):
        ref[pl.ds(c, m, stride=nt), :] = w[:, c * LANES:(c + 1) * LANES]


def _load_token_tiles(ref, lead, row0, m, nt):
    los, his = [], []
    view = ref if lead is None else ref.at[lead]
    for c in range(nt):
        lo, hi = _unpack_bf16_pair(view[pl.ds(row0 * nt + c, m, stride=nt), :])
        los.append(lo)
        his.append(hi)
    return jnp.concatenate(los + his, axis=1)


def _rel_bucket(dist):
    d = jnp.maximum(dist, 1).astype(F32)
    large = MAX_EXACT + (jnp.log(d / MAX_EXACT) / math.log(MAX_DISTANCE / MAX_EXACT)
                         * (N_BUCKETS - MAX_EXACT)).astype(I32)
    return jnp.where(dist < MAX_EXACT, dist, jnp.minimum(large, N_BUCKETS - 1)).astype(I32)


def _prep_body(x_ref, g_ref, o_ref):
    o_ref[...] = _rms(x_ref[...], g_ref[...]).astype(o_ref.dtype)


def _prep(x2d, g, tm):
    m, d = x2d.shape
    return pl.pallas_call(
        _prep_body, name="prep", grid=(m // tm,),
        in_specs=[pl.BlockSpec((tm, d), lambda i: (i, 0)), pl.BlockSpec((1, d), lambda i: (0, 0))],
        out_specs=pl.BlockSpec((tm, d), lambda i: (i, 0)),
        out_shape=jax.ShapeDtypeStruct((m, d), BF16), compiler_params=_cparams(1),
    )(x2d, g.reshape(1, d))


def _headnorm(acc, g, hd, scale):
    outs = []
    for j in range(acc.shape[1] // hd):
        sl = acc[:, j * hd:(j + 1) * hd]
        outs.append(sl * lax.rsqrt(jnp.mean(sl * sl, axis=-1, keepdims=True) + EPS))
    y = outs[0] if len(outs) == 1 else jnp.concatenate(outs, axis=1)
    y = y * g
    return y * scale if scale != 1.0 else y


def _proj_body(h_ref, *refs, n_w, mode, hd, scale):
    w_refs, (g_ref, o_ref, w_scr) = refs[:n_w], refs[n_w:]
    cw = w_refs[0].shape[1]

    @pl.when(pl.program_id(1) == 0)
    def _():
        for i, w_ref in enumerate(w_refs):
            w_scr[:, i * cw:(i + 1) * cw] = w_ref[...].astype(w_scr.dtype)

    def run(epilogue):
        acc = jnp.dot(h_ref[...], w_scr[...], preferred_element_type=F32)
        o_ref[...] = epilogue(acc).astype(o_ref.dtype)

    normed = lambda acc: _headnorm(acc, g_ref[...], hd, scale)
    if mode == "none":
        run(lambda acc: acc)
    elif mode == "sigmoid":
        run(jax.nn.sigmoid)
    elif mode == "norm":
        run(normed)
    else:
        @pl.when(pl.program_id(0) == 0)
        def _():
            run(normed)

        @pl.when(pl.program_id(0) != 0)
        def _():
            run(lambda acc: acc)


W_COLS = 512


def _proj(h, w, cols, tm, mode, out_dtype, gain=None, hd=None, scale=1.0, name="proj"):
    m, k = h.shape
    base, stride, n_w, n_tiles = cols
    tn = n_w * W_COLS
    if gain is None:
        g = jnp.ones((1, tn), F32)
    else:
        g = jnp.tile(gain.astype(F32).reshape(1, -1), (1, tn // gain.shape[-1]))
    body = functools.partial(_proj_body, n_w=n_w, mode=mode, hd=hd, scale=scale)
    w_specs = [pl.BlockSpec((k, W_COLS), functools.partial(lambda j, i, off: (0, base + stride * j + off), off=off))
               for off in range(n_w)]
    return pl.pallas_call(
        body, name=name, grid=(n_tiles, m // tm),
        in_specs=[pl.BlockSpec((tm, k), lambda j, i: (i, 0))] + w_specs + [pl.BlockSpec((1, tn), lambda j, i: (0, 0))],
        out_specs=pl.BlockSpec((tm, tn), lambda j, i: (i, j)),
        out_shape=jax.ShapeDtypeStruct((m, n_tiles * tn), out_dtype),
        scratch_shapes=[pltpu.VMEM((k, tn), BF16)], compiler_params=_cparams(2),
    )(h, *([w] * n_w), g)


PERM_ROWS = 256


def _perm_matrix(n, dil, inverse=False):
    per = n // dil
    o = lax.broadcasted_iota(I32, (n, n), 1 if inverse else 0)
    s = lax.broadcasted_iota(I32, (n, n), 0 if inverse else 1)
    return (s == (o % per) * dil + o // per).astype(BF16)


def _qkv_body(h_ref, wq_ref, wk_ref, wv_ref, gq_ref, gk_ref, q_ref, kv_ref, tok_ref, w_scr, *, dil, scale):
    c = pl.program_id(1)
    tm = h_ref.shape[0]
    nh = HEADS_PER_GROUP_A

    @pl.when(pl.program_id(0) == 0)
    def _():
        for i, w_ref in enumerate((wq_ref, wk_ref, wv_ref)):
            @pl.when(c == i)
            def _():
                w_scr[i] = w_ref[...].astype(BF16)

    n_sub = max(1, tm // PERM_ROWS)
    sub = tm // n_sub
    per = sub // dil

    def put_perm(dst_ref, y, t, pm):
        if dil == 1:
            dst_ref[0, t * sub:(t + 1) * sub, :] = y.astype(dst_ref.dtype)
            return
        yp = jnp.dot(pm, y.astype(BF16), preferred_element_type=F32).astype(BF16)
        for r in range(dil):
            dst_ref[r, t * per:(t + 1) * per, :] = yp[r * per:(r + 1) * per]

    def put_tok(y, c0, t):
        for i in range(nh):
            tok_ref[pl.ds(t * sub * 2 * nh + c0 + i, sub, stride=2 * nh), :] = y[:, i * HEAD_DIM_A:(i + 1) * HEAD_DIM_A]

    for ci in range(3):
        @pl.when(c == ci)
        def _(ci=ci):
            pm = None if dil == 1 else _perm_matrix(sub, dil)
            full = None if dil == 1 else jnp.dot(h_ref[...], w_scr[ci], preferred_element_type=F32)
            for t in range(n_sub):
                if dil == 1:
                    acc = jnp.dot(h_ref[t * sub:(t + 1) * sub, :], w_scr[ci], preferred_element_type=F32)
                else:
                    acc = full[t * sub:(t + 1) * sub]
                if ci == 0:
                    put_perm(q_ref, _headnorm(acc, gq_ref[...], HEAD_DIM_A, scale), t, pm)
                elif ci == 1:
                    y = _headnorm(acc, gk_ref[...], HEAD_DIM_A, 1.0)
                    put_tok(y, 0, t)
                    put_perm(kv_ref, y, t, pm)
                else:
                    put_tok(acc, nh, t)
                    put_perm(kv_ref, acc, t, pm)


def _proj_qkv(h, w, sec, g, g_qa, g_ka, n_batch, seq, tm, dil, act_dtype=BF16):
    n, k = h.shape
    ln = seq // dil
    per = tm // dil
    mt = seq // tm
    assert seq % tm == 0
    assert dil == 1 or (act_dtype == BF16 and tm % PERM_ROWS == 0 and (PERM_ROWS // dil) % 16 == 0)
    tile = lambda v: jnp.tile(v.astype(F32).reshape(1, -1), (1, GROUP_W_A // v.shape[-1]))
    wspec = lambda blk: pl.BlockSpec((k, W_COLS), lambda m, c: (0, blk))
    q, kv, tok = pl.pallas_call(
        functools.partial(_qkv_body, dil=dil, scale=HEAD_DIM_A ** -0.5), name=f"proj_qkv{g}", grid=(n // tm, 3),
        in_specs=[pl.BlockSpec((tm, k), lambda m, c: (m, 0)),
                  wspec(sec["q"] + g), wspec(sec["k"] + g), wspec(sec["v"] + g),
                  pl.BlockSpec((1, GROUP_W_A), lambda m, c: (0, 0)), pl.BlockSpec((1, GROUP_W_A), lambda m, c: (0, 0))],
        out_specs=[pl.BlockSpec((None, dil, per, GROUP_W_A), lambda m, c: (m // mt, 0, m % mt, 0)),
                   pl.BlockSpec((None, dil, per, GROUP_W_A), lambda m, c: (m // mt, 0, m % mt, jnp.maximum(c - 1, 0))),
                   pl.BlockSpec((tm * 2 * HEADS_PER_GROUP_A, LANES), lambda m, c: (m, 0))],
        out_shape=[jax.ShapeDtypeStruct((n_batch, dil, ln, GROUP_W_A), act_dtype),
                   jax.ShapeDtypeStruct((n_batch, dil, ln, 2 * GROUP_W_A), act_dtype),
                   jax.ShapeDtypeStruct((n * 2 * HEADS_PER_GROUP_A, LANES), F32)],
        scratch_shapes=[pltpu.VMEM((3, k, W_COLS), BF16)], compiler_params=_cparams(2),
    )(h, w, w, w, tile(g_qa), tile(g_ka))
    return q.reshape(n, GROUP_W_A), kv.reshape(n, 2 * GROUP_W_A), tok


def _bias_from_table(tab_ref, bidx, col):
    acc = jnp.zeros(bidx.shape, F32)
    for kb in range(N_BUCKETS):
        acc = jnp.where(bidx == kb, tab_ref[kb, col], acc)
    return acc


def _split_hi_lo(x):
    hi = x.astype(BF16)
    return hi, (x - hi.astype(F32)).astype(BF16)


ATTN_P_BLOCKS = 2


def _attn_p_body(tab_ref, bidx_ref, q_ref, kp_ref, kc_ref, vp_ref, vc_ref, o_ref, bias_scr, *, g, nb):
    i = pl.program_id(0)

    @pl.when(i == 0)
    def _():
        bidx = bidx_ref[...]
        for h in range(HEADS_PER_GROUP_A):
            bias_scr[h] = _bias_from_table(tab_ref, bidx, g * HEADS_PER_GROUP_A + h)

    blk = kp_ref.shape[0]
    k_all = jnp.concatenate([kp_ref[...], kc_ref[...]], axis=0)
    v_all = jnp.concatenate([vp_ref[...], vc_ref[...]], axis=0)
    row = lax.broadcasted_iota(I32, (blk, 2 * blk), 0)
    col = lax.broadcasted_iota(I32, (blk, 2 * blk), 1)
    dist = blk + row - col
    band = (dist >= 0) & (dist <= blk)
    lane = lax.broadcasted_iota(I32, (blk, LANES), 1)
    for u in range(ATTN_P_BLOCKS):
        has_prev = ((i * ATTN_P_BLOCKS + u) % nb) > 0
        valid = band & (has_prev | (col >= blk))
        q = q_ref[u * blk:(u + 1) * blk, :]
        k = k_all[u * blk:(u + 2) * blk]
        v = v_all[u * blk:(u + 2) * blk]
        lse_blk = jnp.zeros((blk, LANES), F32)
        outs = []
        for h in range(HEADS_PER_GROUP_A):
            hs = slice(h * HEAD_DIM_A, (h + 1) * HEAD_DIM_A)
            s = lax.dot_general(q[:, hs], k[:, hs], (((1,), (1,)), ((), ())), preferred_element_type=F32)
            s = jnp.where(valid, s + bias_scr[h], NEG_INF)
            m = jnp.max(s, axis=-1, keepdims=True)
            p = jnp.exp(s - m)
            l = jnp.sum(p, axis=-1, keepdims=True)
            o = jnp.dot(p.astype(BF16), v[:, hs], preferred_element_type=F32) / l
            outs.append(o)
            lse_blk = jnp.where(lane == h, m + jnp.log(l), lse_blk)
        lse_hi, lse_lo = _split_hi_lo(lse_blk)
        o_ref[u * blk:(u + 1) * blk, :] = jnp.concatenate(
            [jnp.concatenate(outs, axis=1).astype(BF16), lse_hi, lse_lo], axis=1)


O_EXT_W = GROUP_W_A + 2 * LANES


def _attn_prompt(q, kv, rel_bias, g, n_batch, seq):
    win, dil = DIL_GROUPS[g]
    blk = win // dil
    nb = seq // dil // blk
    n = n_batch * seq
    a = jnp.arange(blk)[:, None]
    c = jnp.arange(2 * blk)[None, :]
    bidx = _rel_bucket(jnp.maximum(blk + a - c, 0) * dil)

    nu = ATTN_P_BLOCKS
    assert (n // blk) % nu == 0 and (nb % nu == 0 or nb == 1)

    def prev(i):
        return jnp.maximum(i * nu - 1, 0)

    return pl.pallas_call(
        functools.partial(_attn_p_body, g=g, nb=nb), name=f"attn_p{g}", grid=(n // blk // nu,),
        in_specs=[
            pl.BlockSpec(memory_space=pltpu.SMEM),
            pl.BlockSpec((blk, 2 * blk), lambda i: (0, 0)),
            pl.BlockSpec((nu * blk, GROUP_W_A), lambda i: (i, 0)),
            pl.BlockSpec((blk, GROUP_W_A), lambda i: (prev(i), 0)),
            pl.BlockSpec((nu * blk, GROUP_W_A), lambda i: (i, 0)),
            pl.BlockSpec((blk, GROUP_W_A), lambda i: (prev(i), 1)),
            pl.BlockSpec((nu * blk, GROUP_W_A), lambda i: (i, 1)),
        ],
        out_specs=pl.BlockSpec((nu * blk, O_EXT_W), lambda i: (i, 0)),
        out_shape=jax.ShapeDtypeStruct((n, O_EXT_W), BF16),
        scratch_shapes=[pltpu.VMEM((HEADS_PER_GROUP_A, blk, 2 * blk), F32)],
        compiler_params=_cparams(1),
    )(rel_bias, bidx, q, kv, kv, kv, kv)


def _attn_s_body(tab_ref, bidx_ref, q_ref, kvn_ref, tokn_ref, cache_hbm, o_ref, newc_hbm,
                 raw, kv_scr, bias_scr, sem_in, sem_out, *, g, dil, lb, n_req):
    t_new = q_ref.shape[0]
    nh = HEADS_PER_GROUP_A
    n_rows = nh * t_new
    n_keys = kv_scr.shape[0]
    n_col = 2 * nh
    lb8 = lb * n_col
    n = pl.program_id(0)
    nbuf = raw.shape[0]
    slot = n % nbuf

    def fetch(req):
        s = req % nbuf
        return pltpu.make_async_copy(cache_hbm.at[req], raw.at[s, pl.ds(0, lb8)], sem_in.at[s])

    def flush(req):
        s = req % nbuf
        return pltpu.make_async_copy(raw.at[s, pl.ds(t_new * n_col, lb8)], newc_hbm.at[req], sem_out.at[s])

    @pl.when(n == 0)
    def _():
        kv_scr[lb:, :] = jnp.zeros((n_keys - lb, kv_scr.shape[1]), BF16)
        bidx = bidx_ref[...]
        for h in range(nh):
            bias_scr[h * t_new:(h + 1) * t_new, :] = _bias_from_table(tab_ref, bidx, g * nh + h)
        for r in range(nbuf - 1):
            fetch(r).start()

    @pl.when(n >= 1)
    def _():
        flush(n - 1).wait()

    @pl.when(n + nbuf - 1 < n_req)
    def _():
        fetch(n + nbuf - 1).start()

    fetch(n).wait()
    raw[slot, lb8:lb8 + t_new * n_col, :] = tokn_ref[...]
    flush(n).start()
    new = kvn_ref[...]

    rows_per = min(lb, 256)
    for c in range(n_col):
        for r0 in range(0, lb, rows_per):
            kv_scr[r0:r0 + rows_per, c * LANES:(c + 1) * LANES] = raw[
                slot, pl.ds(r0 * n_col + c, rows_per, stride=n_col), :].astype(BF16)
    kv_scr[lb:lb + 2 * t_new, :] = jnp.concatenate([new, jnp.zeros_like(new)], axis=0).astype(BF16)

    q = q_ref[...]
    qt = jnp.concatenate([q] * nh, axis=0)
    rr = lax.broadcasted_iota(I32, (n_rows, GROUP_W_A), 0)
    cc = lax.broadcasted_iota(I32, (n_rows, GROUP_W_A), 1)
    qbd = jnp.where(cc // HEAD_DIM_A == rr // t_new, qt, 0.0).astype(BF16)

    s = lax.dot_general(qbd, kv_scr[:, :GROUP_W_A], (((1,), (1,)), ((), ())), preferred_element_type=F32)
    row = lax.broadcasted_iota(I32, (n_rows, n_keys), 0)
    col = lax.broadcasted_iota(I32, (n_rows, n_keys), 1)
    delta = lb + (row & (t_new - 1)) - col
    valid = (delta >= 0) & (delta <= lb) & ((delta & (dil - 1)) == 0)
    s = jnp.where(valid, s + bias_scr[...], NEG_INF)
    m = jnp.max(s, axis=-1, keepdims=True)
    p = jnp.exp(s - m)
    l = jnp.sum(p, axis=-1, keepdims=True)
    o = jnp.dot(p.astype(BF16), kv_scr[:, GROUP_W_A:], preferred_element_type=F32) / l
    lse = m + jnp.log(l)
    lane = lax.broadcasted_iota(I32, (t_new, LANES), 1)
    lse_blk = jnp.zeros((t_new, LANES), F32)
    outs = []
    for h in range(nh):
        outs.append(o[h * t_new:(h + 1) * t_new, h * HEAD_DIM_A:(h + 1) * HEAD_DIM_A])
        lse_blk = jnp.where(lane == h, lse[h * t_new:(h + 1) * t_new, :], lse_blk)
    o_ref[...] = jnp.concatenate(outs + [lse_blk, jnp.zeros_like(lse_blk)], axis=1)

    @pl.when(n == n_req - 1)
    def _():
        flush(n).wait()


ATTN_S_BUFS = 3


def _attn_sample(q, kv_new, tok_new, cache, rel_bias, g, t_new):
    win, dil = DIL_GROUPS[g]
    n_col = 2 * HEADS_PER_GROUP_A
    n_req, lb = cache.shape[0], cache.shape[1] // n_col
    assert t_new & (t_new - 1) == 0 and dil & (dil - 1) == 0 and lb == win
    n_keys = lb + LANES
    t = jnp.arange(t_new)[:, None]
    c = jnp.arange(n_keys)[None, :]
    bidx = _rel_bucket(jnp.clip(lb + t - c, 0, lb))
    return pl.pallas_call(
        functools.partial(_attn_s_body, g=g, dil=dil, lb=lb, n_req=n_req), name=f"attn_s{g}", grid=(n_req,),
        in_specs=[
            pl.BlockSpec(memory_space=pltpu.SMEM),
            pl.BlockSpec((t_new, n_keys), lambda n: (0, 0)),
            pl.BlockSpec((t_new, GROUP_W_A), lambda n: (n, 0)),
            pl.BlockSpec((t_new, 2 * GROUP_W_A), lambda n: (n, 0)),
            pl.BlockSpec((t_new * n_col, LANES), lambda n: (n, 0)),
            pl.BlockSpec(memory_space=pl.ANY),
        ],
        out_specs=[pl.BlockSpec((t_new, O_EXT_W), lambda n: (n, 0)), pl.BlockSpec(memory_space=pl.ANY)],
        out_shape=[jax.ShapeDtypeStruct((n_req * t_new, O_EXT_W), F32), jax.ShapeDtypeStruct(cache.shape, F32)],
        scratch_shapes=[pltpu.VMEM((ATTN_S_BUFS, (lb + t_new) * n_col, LANES), F32),
                        pltpu.VMEM((n_keys, 2 * GROUP_W_A), BF16),
                        pltpu.VMEM((HEADS_PER_GROUP_A * t_new, n_keys), F32),
                        pltpu.SemaphoreType.DMA((ATTN_S_BUFS,)), pltpu.SemaphoreType.DMA((ATTN_S_BUFS,))],
        compiler_params=_cparams(1),
    )(rel_bias, bidx, q, kv_new, tok_new, cache)


def _gmlp_body(u_ref, v_ref, w_ref, b_ref, o_ref, *, period):
    c = w_ref.shape[1]
    nch = u_ref.shape[0] // c
    gd = u_ref.shape[1] // N_GROUPS_B
    i = lax.broadcasted_iota(I32, (c, c), 0)
    j = lax.broadcasted_iota(I32, (c, c), 1)
    mask = (j <= i) & ((i // period) == (j // period))
    b = b_ref[...]
    for g in range(N_GROUPS_B):
        gs = slice(g * gd, (g + 1) * gd)
        wg = jnp.where(mask, w_ref[g], 0.0).astype(BF16)
        vg = [v_ref[ch * c:(ch + 1) * c, gs].astype(BF16) for ch in range(nch)]
        vg = vg[0] if nch == 1 else jnp.concatenate(vg, axis=1)
        sg = jnp.dot(wg, vg, preferred_element_type=F32) + b[:, g:g + 1]
        for ch in range(nch):
            u = u_ref[ch * c:(ch + 1) * c, gs].astype(F32)
            o_ref[ch * c:(ch + 1) * c, gs] = (u * sg[:, ch * gd:(ch + 1) * gd]).astype(o_ref.dtype)


def _gmlp(u, v, w, b, tm, period):
    m, wb = u.shape
    c = w.shape[1]
    return pl.pallas_call(
        functools.partial(_gmlp_body, period=period), name="gmlp", grid=(m // tm,),
        in_specs=[pl.BlockSpec((tm, wb), lambda i: (i, 0)), pl.BlockSpec((tm, wb), lambda i: (i, 0)),
                  pl.BlockSpec((N_GROUPS_B, c, c), lambda i: (0, 0, 0)),
                  pl.BlockSpec((c, N_GROUPS_B), lambda i: (0, 0))],
        out_specs=pl.BlockSpec((tm, wb), lambda i: (i, 0)),
        out_shape=jax.ShapeDtypeStruct((m, wb), BF16), compiler_params=_cparams(1),
    )(u, v, w, b)


def _gmlp_new_body(u_ref, v_ref, w_ref, b_ref, o_ref):
    t = w_ref.shape[0]
    n = u_ref.shape[0] // t
    width = u_ref.shape[1]
    v = v_ref[...].astype(F32).reshape(n, t, width)
    sg = jnp.broadcast_to(b_ref[...][None], (n, t, width))
    p_row = lax.broadcasted_iota(I32, (t, width), 0)
    for q in range(t):
        sg = sg + jnp.where(p_row >= q, w_ref[q], 0.0)[None] * v[:, q:q + 1, :]
    o_ref[...] = (u_ref[...].astype(F32).reshape(n, t, width) * sg).reshape(n * t, width).astype(o_ref.dtype)


def _gmlp_new(u, v, w_s, b_s, t):
    m, wb = u.shape
    gd = wb // N_GROUPS_B
    wq = jnp.repeat(jnp.transpose(w_s[:, :t, :t], (2, 1, 0)), gd, axis=-1)
    bq = jnp.repeat(b_s[:, :t].T, gd, axis=-1)
    whole = lambda a: pl.BlockSpec(a.shape, lambda i: (0,) * a.ndim)
    return pl.pallas_call(
        _gmlp_new_body, name="gmlp_new", grid=(1,),
        in_specs=[whole(u), whole(v), whole(wq), whole(bq)], out_specs=whole(u),
        out_shape=jax.ShapeDtypeStruct((m, wb), BF16), compiler_params=_cparams(1),
    )(u, v, wq, bq)


def _memattn_body(q_ref, kv_ref, o_ref, *, tiles):
    wm = N_HEADS_M * HEAD_DIM_M
    n_ct = HEAD_DIM_M // LANES
    per_key = 2 * n_ct * N_HEADS_M
    q = q_ref[...].astype(BF16)

    def head_slab(kv, h):
        if not tiles:
            return kv_ref[:, kv * wm + h * HEAD_DIM_M:kv * wm + (h + 1) * HEAD_DIM_M].astype(BF16)
        n_keys = kv_ref.shape[0] // per_key
        parts = [kv_ref[pl.ds((kv * n_ct + ct) * N_HEADS_M + h, n_keys, stride=per_key), :] for ct in range(n_ct)]
        return jnp.concatenate(parts, axis=1).astype(BF16)

    outs = []
    for h in range(N_HEADS_M):
        hs = slice(h * HEAD_DIM_M, (h + 1) * HEAD_DIM_M)
        k = head_slab(0, h)
        v = head_slab(1, h)
        s = lax.dot_general(q[:, hs], k, (((1,), (1,)), ((), ())), preferred_element_type=F32)
        m = jnp.max(s, axis=-1, keepdims=True)
        p = jnp.exp(s - m)
        l = jnp.sum(p, axis=-1, keepdims=True)
        outs.append(jnp.dot(p.astype(BF16), v, preferred_element_type=F32) / l)
    o_ref[...] = jnp.concatenate(outs, axis=1).astype(o_ref.dtype)


def _memattn(q, kv, tq, out_dtype, tiles=False):
    m, wm = q.shape
    n = kv.shape[0]
    per = m // n // tq
    kv_spec = pl.BlockSpec((None,) + kv.shape[1:], lambda i: (i // per,) + (0,) * (kv.ndim - 1))
    return pl.pallas_call(
        functools.partial(_memattn_body, tiles=tiles), name="memattn", grid=(m // tq,),
        in_specs=[pl.BlockSpec((tq, wm), lambda i: (i, 0)), kv_spec],
        out_specs=pl.BlockSpec((tq, wm), lambda i: (i, 0)),
        out_shape=jax.ShapeDtypeStruct((m, wm), out_dtype), compiler_params=_cparams(1),
    )(q, kv)


def _mix_body(gt_ref, o0_ref, o1_ref, o2_ref, ob_ref, om_ref, wpa_ref, wpb_ref, wpm_ref, z_ref, *, dils):
    d = z_ref.shape[1]
    tm = z_ref.shape[0]
    os_, ls = [], []
    for o_ref, dil in zip((o0_ref, o1_ref, o2_ref), dils):
        if dil == 1:
            x = o_ref[...].astype(F32)
        else:
            rows = jnp.concatenate([o_ref[r] for r in range(dil)], axis=0)
            x = jnp.dot(_perm_matrix(tm, dil, inverse=True), rows, preferred_element_type=F32)
        os_.append(x[:, :GROUP_W_A])
        ls.append(x[:, GROUP_W_A:GROUP_W_A + LANES] + x[:, GROUP_W_A + LANES:])
    l0, l1, l2 = ls
    mx = jnp.maximum(jnp.maximum(l0, l1), l2)
    e0, e1, e2 = jnp.exp(l0 - mx), jnp.exp(l1 - mx), jnp.exp(l2 - mx)
    den = e0 + e1 + e2
    w0, w1, w2 = e0 / den, e1 / den, e2 / den
    cols = []
    for h in range(HEADS_PER_GROUP_A):
        hs = slice(h * HEAD_DIM_A, (h + 1) * HEAD_DIM_A)
        cols.append(w0[:, h:h + 1] * os_[0][:, hs] + w1[:, h:h + 1] * os_[1][:, hs] + w2[:, h:h + 1] * os_[2][:, hs])
    oa = jnp.concatenate(cols, axis=1).astype(BF16)
    pa = jnp.dot(oa, wpa_ref[...], preferred_element_type=F32)
    pb = jnp.dot(ob_ref[...].astype(BF16), wpb_ref[...], preferred_element_type=F32)
    pm = jnp.dot(om_ref[...].astype(BF16), wpm_ref[...], preferred_element_type=F32)
    z = (gt_ref[:, 0:d].astype(F32) * pa + gt_ref[:, d:2 * d].astype(F32) * pb
         + gt_ref[:, 2 * d:3 * d].astype(F32) * pm)
    z_ref[...] = z.astype(z_ref.dtype)


def _mix(gates, o_list, dils, seq, ob, om, wpa, wpb, wpm, tm):
    m = gates.shape[0]
    d = wpa.shape[1]
    mt = seq // tm if any(dl > 1 for dl in dils) else 1

    def rows(width):
        return pl.BlockSpec((tm, width), lambda i: (i, 0))

    def whole(a):
        return pl.BlockSpec(a.shape, lambda i: (0, 0))

    o_specs, o_args = [], []
    for o, dil in zip(o_list, dils):
        if dil == 1:
            o_specs.append(rows(O_EXT_W))
            o_args.append(o)
        else:
            assert seq % tm == 0 and (tm // dil) % 16 == 0
            o_specs.append(pl.BlockSpec((None, dil, tm // dil, O_EXT_W), lambda i: (i // mt, 0, i % mt, 0)))
            o_args.append(o.reshape(m // seq, dil, seq // dil, O_EXT_W))
    return pl.pallas_call(
        functools.partial(_mix_body, dils=tuple(dils)), name="mix", grid=(m // tm,),
        in_specs=[rows(gates.shape[1])] + o_specs
                 + [rows(ob.shape[1]), rows(om.shape[1]), whole(wpa), whole(wpb), whole(wpm)],
        out_specs=rows(d),
        out_shape=jax.ShapeDtypeStruct((m, d), BF16), compiler_params=_cparams(1),
    )(gates, *o_args, ob, om, wpa, wpb, wpm)


def _route(logits):
    lane = lax.broadcasted_iota(I32, logits.shape, 1)
    lane_f = lane.astype(F32)
    is_g = lane < N_EXPERT_GROUPS
    gmax = jnp.max(jnp.where(is_g, logits, -jnp.inf), axis=1, keepdims=True)
    gsel = jnp.min(jnp.where(is_g & (logits == gmax), lane_f, float(LANES)), axis=1, keepdims=True).astype(I32)
    gden = jnp.sum(jnp.where(is_g, jnp.exp(logits - gmax), 0.0), axis=1, keepdims=True)
    pg = 1.0 / gden
    e_lane = lane - N_EXPERT_GROUPS
    in_grp = (e_lane >= 0) & (e_lane < N_EXPERTS) & ((e_lane // EXPERTS_PER_GROUP) == gsel)
    m1 = jnp.max(jnp.where(in_grp, logits, -jnp.inf), axis=1, keepdims=True)
    i1 = jnp.min(jnp.where(in_grp & (logits == m1), lane_f, float(LANES)), axis=1, keepdims=True).astype(I32)
    rest = in_grp & (lane != i1)
    m2 = jnp.max(jnp.where(rest, logits, -jnp.inf), axis=1, keepdims=True)
    i2 = jnp.min(jnp.where(rest & (logits == m2), lane_f, float(LANES)), axis=1, keepdims=True).astype(I32)
    e2 = jnp.exp(m2 - m1)
    w1 = pg / (1.0 + e2)
    w2 = pg * e2 / (1.0 + e2)
    eid = jnp.where(lane == 0, i1 - N_EXPERT_GROUPS, jnp.where(lane == 1, i2 - N_EXPERT_GROUPS, 0))
    wts = jnp.where(lane == 0, w1, jnp.where(lane == 1, w2, 0.0))
    return eid, wts


def _resid_body(x_ref, z_ref, wo_ref, gf_ref, wr_ref, br_ref, xmid_ref, hpk_ref, eid_ref, wts_ref):
    xm = x_ref[...] + jnp.dot(z_ref[...], wo_ref[...], preferred_element_type=F32)
    xmid_ref[...] = xm
    hf = _rms(xm, gf_ref[...])
    wr = wr_ref[...]
    wr_hi = wr.astype(BF16)
    wr_lo = (wr - wr_hi.astype(F32)).astype(BF16)
    hf_hi = hf.astype(BF16)
    hf_lo = (hf - hf_hi.astype(F32)).astype(BF16)
    logits = (jnp.dot(hf_hi, wr_hi, preferred_element_type=F32) + jnp.dot(hf_hi, wr_lo, preferred_element_type=F32)
              + jnp.dot(hf_lo, wr_hi, preferred_element_type=F32) + br_ref[...])
    eid, wts = _route(logits)
    eid_ref[...] = eid
    wts_ref[...] = wts
    _store_token_tiles(hpk_ref, hf)


def _resid(x2d, z, wo, g_ffn, w_r, b_r, tm):
    m, d = x2d.shape

    def rows(width):
        return pl.BlockSpec((tm, width), lambda i: (i, 0))

    def whole(a):
        return pl.BlockSpec(a.shape, lambda i: (0, 0))

    gf = g_ffn.reshape(1, d)
    nt = d // 2 // LANES
    return pl.pallas_call(
        _resid_body, name="resid", grid=(m // tm,),
        in_specs=[rows(d), rows(d), whole(wo), whole(gf), whole(w_r), whole(b_r)],
        out_specs=[rows(d), pl.BlockSpec((tm * nt, LANES), lambda i: (i, 0)), rows(LANES), rows(LANES)],
        out_shape=[jax.ShapeDtypeStruct((m, d), F32), jax.ShapeDtypeStruct((m * nt, LANES), U32),
                   jax.ShapeDtypeStruct((m, LANES), I32), jax.ShapeDtypeStruct((m, LANES), F32)],
        compiler_params=_cparams(1),
    )(x2d, z, wo, gf, w_r, b_r)


TOKEN_TILE = 8
GATHER_UNROLL = 8


def _gather_rows(idx_at, src_hbm, dst, sem, n):
    def body(i, carry):
        src = pl.multiple_of(idx_at(i), TOKEN_TILE)
        dst_row = pl.multiple_of(i * TOKEN_TILE, TOKEN_TILE)
        pltpu.make_async_copy(src_hbm.at[pl.ds(src, TOKEN_TILE)], dst.at[pl.ds(dst_row, TOKEN_TILE)], sem).start()
        return carry
    lax.fori_loop(0, n, body, 0, unroll=GATHER_UNROLL)


FFN_CHUNKS = 8
FFN_BUFS = 3


def _ffn_body(blk_e_ref, nused_ref, eord_ref, enext_ref, idx_hbm, h_hbm, wg_hbm, wu_hbm, wd_hbm, y_ref,
              idx_s, isem, buf, sem, x_scr, a_scr, b_scr, hm_scr, wg_f, wu_f, wd_f, wsem, wg_s, wu_s, wd_s):
    j = pl.program_id(0)
    nused = nused_ref[0]
    last = nused - 1
    rows = buf.shape[1] // TOKEN_TILE
    e = blk_e_ref[j]

    def idx_copy(b):
        s = b % FFN_BUFS
        return pltpu.make_async_copy(idx_hbm.at[jnp.minimum(b, last)], idx_s.at[s], isem.at[s])

    def wait_rows(s):
        pltpu.make_async_copy(h_hbm.at[pl.ds(0, rows * TOKEN_TILE)], buf.at[s], sem.at[s]).wait()
    new_expert = (j == 0) | (e != blk_e_ref[jnp.maximum(j - 1, 0)])
    ws = eord_ref[j] % 2

    def fetch_w(expert, s):
        return [pltpu.make_async_copy(src.at[expert], dst.at[s], wsem.at[s, i])
                for i, (src, dst) in enumerate(((wg_hbm, wg_f), (wu_hbm, wu_f), (wd_hbm, wd_f)))]

    @pl.when(j == 0)
    def _():
        for c in fetch_w(e, ws):
            c.start()
        for b in range(FFN_BUFS):
            idx_copy(b).start()
        for b in range(FFN_BUFS - 1):
            idx_copy(b).wait()
            _gather_rows(lambda i, b=b: idx_s[b, 0, i], h_hbm, buf.at[b], sem.at[b], rows)

    @pl.when(new_expert)
    def _():
        nxt = enext_ref[j]

        @pl.when(nxt >= 0)
        def _():
            for c in fetch_w(nxt, 1 - ws):
                c.start()

        for c in fetch_w(e, ws):
            c.wait()
        wg_s[...] = wg_f[ws].astype(BF16)
        wu_s[...] = wu_f[ws].astype(BF16)
        wd_s[...] = wd_f[ws].astype(BF16)

    @pl.when(j < nused)
    def _():
        slot = j % FFN_BUFS
        s2 = (j + 2) % FFN_BUFS
        per = rows // FFN_CHUNKS
        idx_copy(j + 2).wait()

        @pl.when(j < last)
        def _():
            idx_copy(j + 3).start()

        def issue(c):
            for i in range(c * per, (c + 1) * per):
                src = pl.multiple_of(idx_s[s2, 0, i], TOKEN_TILE)
                pltpu.make_async_copy(h_hbm.at[pl.ds(src, TOKEN_TILE)],
                                      buf.at[s2, pl.ds(i * TOKEN_TILE, TOKEN_TILE)], sem.at[s2]).start()

        wait_rows(slot)
        x_scr[...] = _load_token_tiles(buf, slot, 0, rows, TOKEN_TILE).astype(BF16)
        dn = a_scr.shape[1] // 2
        for c in range(2):
            issue(c)
            a_scr[:, c * dn:(c + 1) * dn] = jnp.dot(x_scr[...], wg_s[:, c * dn:(c + 1) * dn],
                                                     preferred_element_type=F32)
        for c in range(2):
            issue(2 + c)
            b_scr[:, c * dn:(c + 1) * dn] = jnp.dot(x_scr[...], wu_s[:, c * dn:(c + 1) * dn],
                                                     preferred_element_type=F32)
        a = a_scr[...]
        hm_scr[...] = (a * jax.nn.sigmoid(a) * b_scr[...]).astype(BF16)
        half = wd_s.shape[1] // 2
        n_dc = FFN_CHUNKS - 4
        dq = half // n_dc
        for c in range(n_dc):
            issue(4 + c)
            lo = jnp.dot(hm_scr[...], wd_s[:, c * dq:(c + 1) * dq], preferred_element_type=F32)
            hi = jnp.dot(hm_scr[...], wd_s[:, half + c * dq:half + (c + 1) * dq], preferred_element_type=F32)
            w = _pack_bf16_pair(jnp.concatenate([lo, hi], axis=1))
            for t in range(dq // LANES):
                y_ref[pl.ds(c * (dq // LANES) + t, rows, stride=TOKEN_TILE), :] = w[:, t * LANES:(t + 1) * LANES]

        @pl.when(j == last)
        def _():
            wait_rows((j + 1) % FFN_BUFS)
            wait_rows(s2)

    @pl.when(j >= nused)
    def _():
        y_ref[...] = jnp.zeros(y_ref.shape, y_ref.dtype)


def _ffn(hpk, src_tok, blk_e, nused, eord, enext, wg, wu, wd):
    n_blocks = blk_e.shape[0]
    rows = MOE_ROWS
    tiles = rows * TOKEN_TILE
    d, de = wg.shape[1], wg.shape[2]
    assert d // 2 == TOKEN_TILE * LANES and rows % FFN_CHUNKS == 0 and (d // 2) % ((FFN_CHUNKS - 4) * LANES) == 0
    assert n_blocks >= N_EXPERTS + FFN_BUFS
    idx = (src_tok * TOKEN_TILE).reshape(n_blocks, 1, rows)
    hbm = pl.BlockSpec(memory_space=pl.ANY)
    grid_spec = pltpu.PrefetchScalarGridSpec(
        num_scalar_prefetch=4, grid=(n_blocks,),
        in_specs=[hbm, hbm, hbm, hbm, hbm],
        out_specs=pl.BlockSpec((tiles, LANES), lambda j, *_: (j, 0)),
        scratch_shapes=[pltpu.SMEM((FFN_BUFS, 1, rows), I32), pltpu.SemaphoreType.DMA((FFN_BUFS,)),
                        pltpu.VMEM((FFN_BUFS, tiles, LANES), U32), pltpu.SemaphoreType.DMA((FFN_BUFS,)),
                        pltpu.VMEM((rows, d), BF16), pltpu.VMEM((rows, de), F32), pltpu.VMEM((rows, de), F32),
                        pltpu.VMEM((rows, de), BF16),
                        pltpu.VMEM((2, d, de), wg.dtype), pltpu.VMEM((2, d, de), wu.dtype),
                        pltpu.VMEM((2, de, d), wd.dtype), pltpu.SemaphoreType.DMA((2, 3)),
                        pltpu.VMEM((d, de), BF16), pltpu.VMEM((d, de), BF16), pltpu.VMEM((de, d), BF16)],
    )
    return pl.pallas_call(
        _ffn_body, name="ffn", grid_spec=grid_spec,
        out_shape=jax.ShapeDtypeStruct((n_blocks * tiles, LANES), U32), compiler_params=_cparams(1),
    )(blk_e, nused, eord, enext, idx, hpk, wg, wu, wd)


def _combine_body(idx_cur_ref, idx_nxt_ref, x_ref, w_ref, y_hbm, o_ref, buf, sem, *, n):
    i = pl.program_id(0)
    slot = i % 2
    rows = buf.shape[1] // TOKEN_TILE
    tm = rows // TOP_K

    @pl.when(i == 0)
    def _():
        _gather_rows(lambda r: idx_cur_ref[0, 0, r], y_hbm, buf.at[0], sem.at[0], rows)

    if n > 1:
        @pl.when(i + 1 < n)
        def _():
            _gather_rows(lambda r: idx_nxt_ref[0, 0, r], y_hbm, buf.at[1 - slot], sem.at[1 - slot], rows)

    pltpu.make_async_copy(y_hbm.at[pl.ds(0, rows * TOKEN_TILE)], buf.at[slot], sem.at[slot]).wait()
    w = w_ref[...]
    y0 = _load_token_tiles(buf, slot, 0, tm, TOKEN_TILE)
    y1 = _load_token_tiles(buf, slot, tm, tm, TOKEN_TILE)
    o_ref[...] = x_ref[...] + (w[:, 0:1] * y0 + w[:, 1:2] * y1)


def _combine(xmid, wts, dest, ypk, tm):
    m, d = xmid.shape
    nt = m // tm
    assert d // 2 == TOKEN_TILE * LANES
    idx = (dest * TOKEN_TILE).reshape(nt, tm, TOP_K).transpose(0, 2, 1).reshape(nt, 1, TOP_K * tm)

    def idx_spec(ahead):
        return pl.BlockSpec((1, 1, TOP_K * tm), lambda i: (jnp.minimum(i + ahead, nt - 1), 0, 0),
                            memory_space=pltpu.SMEM)

    return pl.pallas_call(
        functools.partial(_combine_body, n=nt), name="combine", grid=(nt,),
        in_specs=[idx_spec(0), idx_spec(1), pl.BlockSpec((tm, d), lambda i: (i, 0)),
                  pl.BlockSpec((tm, LANES), lambda i: (i, 0)), pl.BlockSpec(memory_space=pl.ANY)],
        out_specs=pl.BlockSpec((tm, d), lambda i: (i, 0)),
        out_shape=jax.ShapeDtypeStruct((m, d), F32),
        scratch_shapes=[pltpu.VMEM((2, TOP_K * tm * TOKEN_TILE, LANES), U32), pltpu.SemaphoreType.DMA((2,))],
        compiler_params=_cparams(1),
    )(idx, idx, xmid, wts, ypk)


def _moe_plan(eid2):
    n_slot = eid2.shape[0] * TOP_K
    eid = eid2.reshape(n_slot)
    onehot = (eid[:, None] == jnp.arange(N_EXPERTS, dtype=I32)[None, :]).astype(I32)
    csum = jnp.cumsum(onehot, axis=0)
    rank = jnp.take_along_axis(csum, eid[:, None], axis=1)[:, 0] - 1
    counts = csum[-1]
    padded = (counts + MOE_ROWS - 1) // MOE_ROWS * MOE_ROWS
    pend = jnp.cumsum(padded)
    pstart = pend - padded
    dest = pstart[eid] + rank
    n_blocks = -(-n_slot // MOE_ROWS) + N_EXPERTS
    nused = (pend[-1] // MOE_ROWS).astype(I32)
    blk = jnp.minimum(jnp.arange(n_blocks, dtype=I32), nused - 1)
    blk_e = jnp.minimum(jnp.searchsorted(pend, blk * MOE_ROWS, side="right"), N_EXPERTS - 1).astype(I32)
    n_rows = n_blocks * MOE_ROWS
    src_tok = (jnp.arange(n_rows, dtype=I32) % (n_slot // TOP_K)).at[dest].set(jnp.arange(n_slot, dtype=I32) // TOP_K)
    has_rows = counts > 0
    ord_e = jnp.cumsum(has_rows.astype(I32)) - 1
    ids = jnp.arange(N_EXPERTS, dtype=I32)
    later = has_rows[None, :] & (ids[None, :] > ids[:, None])
    next_e = jnp.where(later.any(axis=1), jnp.argmax(later, axis=1), -1).astype(I32)
    return dest.astype(I32), src_tok, blk_e, nused.reshape(1), ord_e[blk_e].astype(I32), next_e[blk_e]


def _mixer(x2d, tm, p, attn_fn, gmlp_fn, mem_fn):
    h = _prep(x2d, p["g_mix"], tm)
    qdt = p["q_dtype"]
    wi, sec = p["w_in"], p["sec"]
    o_list, dils, seq = attn_fn(h)
    ub = _proj(h, wi, (sec["u"], 1, 1, sec["vb"] - sec["u"]), tm, "none", BF16, name="proj_ub")
    n_vb = sec["qm"] - sec["vb"]
    vb = _proj(h, wi, (sec["vb"], n_vb, n_vb, 1), tm, "norm", p["vb_dtype"], p["g_vb"], n_vb * W_COLS,
               name="proj_vb")
    qm = _proj(h, wi, (sec["qm"], 1, 1, sec["gt"] - sec["qm"]), tm, "norm", qdt, p["g_qm"], HEAD_DIM_M,
               HEAD_DIM_M ** -0.5, name="proj_qm")
    gates = _proj(h, wi, (sec["gt"], 2, 2, (sec["end"] - sec["gt"]) // 2), tm, "sigmoid", BF16, name="proj_gates")
    ob = gmlp_fn(ub, vb)
    om = mem_fn(qm)
    z = _mix(gates, o_list, dils, seq, ob, om, p["w_pa"], p["w_pb"], p["w_pm"], min(tm, 256))
    xmid, hpk, eid, wts = _resid(x2d, z, p["w_o"], p["g_ffn"], p["w_r"], p["b_r"], min(tm, 256))
    return vb, xmid, hpk, eid, wts


def kernel(x_prompt, x_sample, mem_prompt, cache_a0_kv, cache_a1_kv, cache_a2_kv, cache_mem_kv, rel_bias, g_mix, w_in, g_qa, g_ka, w_pa, g_vb, w_s, b_s, w_pb, g_mem, w_mk, w_mv, g_qm, g_km, w_pm, w_o, g_ffn, w_rg, b_rg, w_re, b_re, w_gate, w_up, w_down):
    n_b, seq, d = x_prompt.shape
    n_s, t_s, _ = x_sample.shape
    depth = w_in.shape[0]
    assert depth == 1
    l = 0
    caches = (cache_a0_kv, cache_a1_kv, cache_a2_kv)
    n_g = len(DIL_GROUPS)
    wb = w_pb.shape[1]
    wm = N_HEADS_M * HEAD_DIM_M

    offs = [0, WIDTH_A, 2 * WIDTH_A, 3 * WIDTH_A, 3 * WIDTH_A + wb, 3 * WIDTH_A + 2 * wb,
            3 * WIDTH_A + 2 * wb + wm, w_in.shape[2]]
    assert all(o % W_COLS == 0 for o in offs) and (offs[7] - offs[6]) % (2 * W_COLS) == 0
    sec = dict(zip(("q", "k", "v", "u", "vb", "qm", "gt", "end"), (o // W_COLS for o in offs)))
    p = {
        "g_mix": g_mix[l], "g_qa": g_qa[l], "g_ka": g_ka[l], "g_vb": g_vb[l], "g_qm": g_qm[l], "g_ffn": g_ffn[l],
        "w_in": w_in[l], "sec": sec,
        "w_pa": w_pa[l].astype(BF16), "w_pb": w_pb[l].astype(BF16), "w_pm": w_pm[l].astype(BF16),
        "w_o": w_o[l].astype(BF16),
    }
    n_r = N_EXPERT_GROUPS + N_EXPERTS
    p["w_r"] = jnp.concatenate([w_rg[l], w_re[l], jnp.zeros((d, LANES - n_r), F32)], axis=1)
    p["b_r"] = jnp.concatenate([b_rg[l], b_re[l], jnp.zeros((LANES - n_r,), F32)]).reshape(1, LANES)

    n_mem = mem_prompt.shape[1]
    h_mem = _prep(mem_prompt.reshape(n_b * n_mem, d), g_mem[l], 256)
    w_mkv = jnp.concatenate([w_mk[l], w_mv[l]], axis=1)
    n_mw = wm // W_COLS
    mkv = _proj(h_mem, w_mkv, (0, n_mw, n_mw, 2), 256, "norm_first", F32, g_km[l], HEAD_DIM_M, name="proj_mkv")
    new_mem_p = mkv.reshape(1, n_b, n_mem, 2, N_HEADS_M, HEAD_DIM_M)

    ps = dict(p, q_dtype=F32, vb_dtype=F32)
    m_s = n_s * t_s
    n_col = 2 * HEADS_PER_GROUP_A
    new_s = []

    def attn_sample(h):
        o_list = []
        for g in range(n_g):
            q, kv, tok = _proj_qkv(h, w_in[l], sec, g, g_qa[l], g_ka[l], 1, m_s, m_s, 1, act_dtype=F32)
            cache3 = caches[g][l].reshape(n_s, caches[g].shape[2] * n_col, LANES)
            o, newc = _attn_sample(q, kv, tok, cache3, rel_bias, g, t_s)
            o_list.append(o)
            new_s.append(newc.reshape(caches[g][l:l + 1].shape))
        return o_list, [1] * n_g, t_s

    n_ct = HEAD_DIM_M // LANES
    mem_tiles = (cache_mem_kv[l].reshape(n_s, n_mem, 2, N_HEADS_M, n_ct, LANES).transpose(0, 1, 2, 4, 3, 5)
                 .reshape(n_s, n_mem * 2 * n_ct * N_HEADS_M, LANES))
    vb_s, xmid_s, hpk_s, eid_s, wts_s = _mixer(
        x_sample.reshape(m_s, d), m_s, ps, attn_sample,
        lambda u, v: _gmlp_new(u, v, w_s[l], b_s[l], t_s),
        lambda qm: _memattn(qm, mem_tiles, t_s, F32, tiles=True))
    new_vb_s = vb_s.reshape(1, n_s, t_s, wb)

    pp = dict(p, q_dtype=BF16, vb_dtype=BF16)
    new_p = []

    def attn_prompt(h):
        o_list = []
        for g, (win, dil) in enumerate(DIL_GROUPS):
            q, kv, tok = _proj_qkv(h, w_in[l], sec, g, g_qa[l], g_ka[l], n_b, seq, 512, dil)
            o_list.append(_attn_prompt(q, kv, rel_bias, g, n_b, seq))
            keep = min(win, seq)
            kv5 = tok.reshape(n_b, seq, 2, HEADS_PER_GROUP_A, HEAD_DIM_A)
            new_p.append((kv5 if keep == seq else kv5[:, seq - keep:])[None])
        return o_list, [dl for _, dl in DIL_GROUPS], seq

    _, xmid_p, hpk_p, eid_p, wts_p = _mixer(
        x_prompt.reshape(n_b * seq, d), 512, pp, attn_prompt,
        lambda u, v: _gmlp(u, v, w_s[l], b_s[l].T, 512, CHUNK),
        lambda qm: _memattn(qm, mkv.reshape(n_b, n_mem, 2 * wm), 256, BF16))

    n_p = n_b * seq
    hpk = jnp.concatenate([hpk_p, hpk_s], axis=0)
    eid2 = jnp.concatenate([eid_p[:, :TOP_K], eid_s[:, :TOP_K]], axis=0)
    dest, src_tok, blk_e, nused, eord, enext = _moe_plan(eid2)
    ypk = _ffn(hpk, src_tok, blk_e, nused, eord, enext, w_gate[l], w_up[l], w_down[l])
    dest2 = dest.reshape(-1, TOP_K)
    y_p = _combine(xmid_p, wts_p, dest2[:n_p], ypk, 256)
    y_s = _combine(xmid_s, wts_s, dest2[n_p:], ypk, m_s)

    return (y_p.reshape(n_b, seq, d), y_s.reshape(n_s, t_s, d), new_p[0], new_p[1], new_p[2], new_mem_p,
            new_s[0], new_s[1], new_s[2], new_vb_s)
```

```python
import functools
import math

import jax
import jax.numpy as jnp
from jax import lax
from jax.experimental import pallas as pl
from jax.experimental.pallas import tpu as pltpu

F32 = jnp.float32
BF16 = jnp.bfloat16
I32 = jnp.int32
U32 = jnp.uint32

EPS = 1e-6
NEG_INF = -1e30

HEAD_DIM_A = 128
HEADS_PER_GROUP_A = 4
DIL_GROUPS = ((128, 1), (512, 4), (2048, 16))
GROUP_W_A = HEADS_PER_GROUP_A * HEAD_DIM_A
WIDTH_A = len(DIL_GROUPS) * GROUP_W_A
CHUNK = 128
N_GROUPS_B = 8
N_HEADS_M = 4
HEAD_DIM_M = 256
N_BUCKETS = 32
MAX_EXACT = N_BUCKETS // 2
MAX_DISTANCE = 2048
N_EXPERT_GROUPS = 4
EXPERTS_PER_GROUP = 8
N_EXPERTS = N_EXPERT_GROUPS * EXPERTS_PER_GROUP
TOP_K = 2
LANES = 128
MOE_ROWS = 256
VMEM_LIMIT = 56 * 1024 * 1024


def _cparams(n_grid, vmem=VMEM_LIMIT):
    return pltpu.CompilerParams(dimension_semantics=("arbitrary",) * n_grid, vmem_limit_bytes=vmem)


def _rms(x, g):
    return x * lax.rsqrt(jnp.mean(x * x, axis=-1, keepdims=True) + EPS) * g


def _pack_bf16_pair(x):
    n = x.shape[1] // 2
    lo = lax.bitcast_convert_type(x[:, :n].astype(BF16).astype(F32), U32)
    hi = lax.bitcast_convert_type(x[:, n:].astype(BF16).astype(F32), U32)
    return (hi & jnp.uint32(0xFFFF0000)) | (lo >> 16)


def _unpack_bf16_pair(w):
    lo = lax.bitcast_convert_type(w << 16, F32)
    hi = lax.bitcast_convert_type(w & jnp.uint32(0xFFFF0000), F32)
    return lo, hi


def _store_token_tiles(ref, x):
    w = _pack_bf16_pair(x)
    m, n = w.shape
    nt = n // LANES
    for c in range(nt):
        ref[pl.ds(c, m, stride=nt), :] = w[:, c * LANES:(c + 1) * LANES]


def _load_token_tiles(ref, lead, row0, m, nt):
    los, his = [], []
    view = ref if lead is None else ref.at[lead]
    for c in range(nt):
        lo, hi = _unpack_bf16_pair(view[pl.ds(row0 * nt + c, m, stride=nt), :])
        los.append(lo)
        his.append(hi)
    return jnp.concatenate(los + his, axis=1)


def _rel_bucket(dist):
    d = jnp.maximum(dist, 1).astype(F32)
    large = MAX_EXACT + (jnp.log(d / MAX_EXACT) / math.log(MAX_DISTANCE / MAX_EXACT)
                         * (N_BUCKETS - MAX_EXACT)).astype(I32)
    return jnp.where(dist < MAX_EXACT, dist, jnp.minimum(large, N_BUCKETS - 1)).astype(I32)


def _prep_body(x_ref, g_ref, o_ref):
    o_ref[...] = _rms(x_ref[...], g_ref[...]).astype(o_ref.dtype)


def _prep(x2d, g, tm):
    m, d = x2d.shape
    return pl.pallas_call(
        _prep_body, name="prep", grid=(m // tm,),
        in_specs=[pl.BlockSpec((tm, d), lambda i: (i, 0)), pl.BlockSpec((1, d), lambda i: (0, 0))],
        out_specs=pl.BlockSpec((tm, d), lambda i: (i, 0)),
        out_shape=jax.ShapeDtypeStruct((m, d), BF16), compiler_params=_cparams(1),
    )(x2d, g.reshape(1, d))


def _headnorm(acc, g, hd, scale):
    outs = []
    for j in range(acc.shape[1] // hd):
        sl = acc[:, j * hd:(j + 1) * hd]
        outs.append(sl * lax.rsqrt(jnp.mean(sl * sl, axis=-1, keepdims=True) + EPS))
    y = outs[0] if len(outs) == 1 else jnp.concatenate(outs, axis=1)
    y = y * g
    return y * scale if scale != 1.0 else y


def _proj_body(h_ref, *refs, n_w, mode, hd, scale):
    w_refs, (g_ref, o_ref, w_scr) = refs[:n_w], refs[n_w:]
    cw = w_refs[0].shape[1]

    @pl.when(pl.program_id(1) == 0)
    def _():
        for i, w_ref in enumerate(w_refs):
            w_scr[:, i * cw:(i + 1) * cw] = w_ref[...].astype(w_scr.dtype)

    def run(epilogue):
        acc = jnp.dot(h_ref[...], w_scr[...], preferred_element_type=F32)
        o_ref[...] = epilogue(acc).astype(o_ref.dtype)

    normed = lambda acc: _headnorm(acc, g_ref[...], hd, scale)
    if mode == "none":
        run(lambda acc: acc)
    elif mode == "sigmoid":
        run(jax.nn.sigmoid)
    elif mode == "norm":
        run(normed)
    else:
        @pl.when(pl.program_id(0) == 0)
        def _():
            run(normed)

        @pl.when(pl.program_id(0) != 0)
        def _():
            run(lambda acc: acc)


W_COLS = 512


def _proj(h, w, cols, tm, mode, out_dtype, gain=None, hd=None, scale=1.0, name="proj"):
    m, k = h.shape
    base, stride, n_w, n_tiles = cols
    tn = n_w * W_COLS
    if gain is None:
        g = jnp.ones((1, tn), F32)
    else:
        g = jnp.tile(gain.astype(F32).reshape(1, -1), (1, tn // gain.shape[-1]))
    body = functools.partial(_proj_body, n_w=n_w, mode=mode, hd=hd, scale=scale)
    w_specs = [pl.BlockSpec((k, W_COLS), functools.partial(lambda j, i, off: (0, base + stride * j + off), off=off))
               for off in range(n_w)]
    return pl.pallas_call(
        body, name=name, grid=(n_tiles, m // tm),
        in_specs=[pl.BlockSpec((tm, k), lambda j, i: (i, 0))] + w_specs + [pl.BlockSpec((1, tn), lambda j, i: (0, 0))],
        out_specs=pl.BlockSpec((tm, tn), lambda j, i: (i, j)),
        out_shape=jax.ShapeDtypeStruct((m, n_tiles * tn), out_dtype),
        scratch_shapes=[pltpu.VMEM((k, tn), BF16)], compiler_params=_cparams(2),
    )(h, *([w] * n_w), g)


PERM_ROWS = 256


def _perm_matrix(n, dil, inverse=False):
    per = n // dil
    o = lax.broadcasted_iota(I32, (n, n), 1 if inverse else 0)
    s = lax.broadcasted_iota(I32, (n, n), 0 if inverse else 1)
    return (s == (o % per) * dil + o // per).astype(BF16)


def _qkv_body(h_ref, wq_ref, wk_ref, wv_ref, gq_ref, gk_ref, q_ref, kv_ref, tok_ref, w_scr, *, dil, scale):
    c = pl.program_id(1)
    tm = h_ref.shape[0]
    nh = HEADS_PER_GROUP_A

    @pl.when(pl.program_id(0) == 0)
    def _():
        for i, w_ref in enumerate((wq_ref, wk_ref, wv_ref)):
            @pl.when(c == i)
            def _():
                w_scr[i] = w_ref[...].astype(BF16)

    n_sub = max(1, tm // PERM_ROWS)
    sub = tm // n_sub
    per = sub // dil

    def put_perm(dst_ref, y, t, pm):
        if dil == 1:
            dst_ref[0, t * sub:(t + 1) * sub, :] = y.astype(dst_ref.dtype)
            return
        yp = jnp.dot(pm, y.astype(BF16), preferred_element_type=F32).astype(BF16)
        for r in range(dil):
            dst_ref[r, t * per:(t + 1) * per, :] = yp[r * per:(r + 1) * per]

    def put_tok(y, c0, t):
        for i in range(nh):
            tok_ref[pl.ds(t * sub * 2 * nh + c0 + i, sub, stride=2 * nh), :] = y[:, i * HEAD_DIM_A:(i + 1) * HEAD_DIM_A]

    for ci in range(3):
        @pl.when(c == ci)
        def _(ci=ci):
            pm = None if dil == 1 else _perm_matrix(sub, dil)
            full = None if dil == 1 else jnp.dot(h_ref[...], w_scr[ci], preferred_element_type=F32)
            for t in range(n_sub):
                if dil == 1:
                    acc = jnp.dot(h_ref[t * sub:(t + 1) * sub, :], w_scr[ci], preferred_element_type=F32)
                else:
                    acc = full[t * sub:(t + 1) * sub]
                if ci == 0:
                    put_perm(q_ref, _headnorm(acc, gq_ref[...], HEAD_DIM_A, scale), t, pm)
                elif ci == 1:
                    y = _headnorm(acc, gk_ref[...], HEAD_DIM_A, 1.0)
                    put_tok(y, 0, t)
                    put_perm(kv_ref, y, t, pm)
                else:
                    put_tok(acc, nh, t)
                    put_perm(kv_ref, acc, t, pm)


def _proj_qkv(h, w, sec, g, g_qa, g_ka, n_batch, seq, tm, dil, act_dtype=BF16):
    n, k = h.shape
    ln = seq // dil
    per = tm // dil
    mt = seq // tm
    assert seq % tm == 0
    assert dil == 1 or (act_dtype == BF16 and tm % PERM_ROWS == 0 and (PERM_ROWS // dil) % 16 == 0)
    tile = lambda v: jnp.tile(v.astype(F32).reshape(1, -1), (1, GROUP_W_A // v.shape[-1]))
    wspec = lambda blk: pl.BlockSpec((k, W_COLS), lambda m, c: (0, blk))
    q, kv, tok = pl.pallas_call(
        functools.partial(_qkv_body, dil=dil, scale=HEAD_DIM_A ** -0.5), name=f"proj_qkv{g}", grid=(n // tm, 3),
        in_specs=[pl.BlockSpec((tm, k), lambda m, c: (m, 0)),
                  wspec(sec["q"] + g), wspec(sec["k"] + g), wspec(sec["v"] + g),
                  pl.BlockSpec((1, GROUP_W_A), lambda m, c: (0, 0)), pl.BlockSpec((1, GROUP_W_A), lambda m, c: (0, 0))],
        out_specs=[pl.BlockSpec((None, dil, per, GROUP_W_A), lambda m, c: (m // mt, 0, m % mt, 0)),
                   pl.BlockSpec((None, dil, per, GROUP_W_A), lambda m, c: (m // mt, 0, m % mt, jnp.maximum(c - 1, 0))),
                   pl.BlockSpec((tm * 2 * HEADS_PER_GROUP_A, LANES), lambda m, c: (m, 0))],
        out_shape=[jax.ShapeDtypeStruct((n_batch, dil, ln, GROUP_W_A), act_dtype),
                   jax.ShapeDtypeStruct((n_batch, dil, ln, 2 * GROUP_W_A), act_dtype),
                   jax.ShapeDtypeStruct((n * 2 * HEADS_PER_GROUP_A, LANES), F32)],
        scratch_shapes=[pltpu.VMEM((3, k, W_COLS), BF16)], compiler_params=_cparams(2),
    )(h, w, w, w, tile(g_qa), tile(g_ka))
    return q.reshape(n, GROUP_W_A), kv.reshape(n, 2 * GROUP_W_A), tok


def _bias_from_table(tab_ref, bidx, col):
    acc = jnp.zeros(bidx.shape, F32)
    for kb in range(N_BUCKETS):
        acc = jnp.where(bidx == kb, tab_ref[kb, col], acc)
    return acc


def _split_hi_lo(x):
    hi = x.astype(BF16)
    return hi, (x - hi.astype(F32)).astype(BF16)


ATTN_P_BLOCKS = 2


def _attn_p_body(tab_ref, bidx_ref, q_ref, kp_ref, kc_ref, vp_ref, vc_ref, o_ref, bias_scr, *, g, nb):
    i = pl.program_id(0)

    @pl.when(i == 0)
    def _():
        bidx = bidx_ref[...]
        for h in range(HEADS_PER_GROUP_A):
            bias_scr[h] = _bias_from_table(tab_ref, bidx, g * HEADS_PER_GROUP_A + h)

    blk = kp_ref.shape[0]
    k_all = jnp.concatenate([kp_ref[...], kc_ref[...]], axis=0)
    v_all = jnp.concatenate([vp_ref[...], vc_ref[...]], axis=0)
    row = lax.broadcasted_iota(I32, (blk, 2 * blk), 0)
    col = lax.broadcasted_iota(I32, (blk, 2 * blk), 1)
    dist = blk + row - col
    band = (dist >= 0) & (dist <= blk)
    lane = lax.broadcasted_iota(I32, (blk, LANES), 1)
    for u in range(ATTN_P_BLOCKS):
        has_prev = ((i * ATTN_P_BLOCKS + u) % nb) > 0
        valid = band & (has_prev | (col >= blk))
        q = q_ref[u * blk:(u + 1) * blk, :]
        k = k_all[u * blk:(u + 2) * blk]
        v = v_all[u * blk:(u + 2) * blk]
        lse_blk = jnp.zeros((blk, LANES), F32)
        outs = []
        for h in range(HEADS_PER_GROUP_A):
            hs = slice(h * HEAD_DIM_A, (h + 1) * HEAD_DIM_A)
            s = lax.dot_general(q[:, hs], k[:, hs], (((1,), (1,)), ((), ())), preferred_element_type=F32)
            s = jnp.where(valid, s + bias_scr[h], NEG_INF)
            m = jnp.max(s, axis=-1, keepdims=True)
            p = jnp.exp(s - m)
            l = jnp.sum(p, axis=-1, keepdims=True)
            o = jnp.dot(p.astype(BF16), v[:, hs], preferred_element_type=F32) / l
            outs.append(o)
            lse_blk = jnp.where(lane == h, m + jnp.log(l), lse_blk)
        lse_hi, lse_lo = _split_hi_lo(lse_blk)
        o_ref[u * blk:(u + 1) * blk, :] = jnp.concatenate(
            [jnp.concatenate(outs, axis=1).astype(BF16), lse_hi, lse_lo], axis=1)


O_EXT_W = GROUP_W_A + 2 * LANES


def _attn_prompt(q, kv, rel_bias, g, n_batch, seq):
    win, dil = DIL_GROUPS[g]
    blk = win // dil
    nb = seq // dil // blk
    n = n_batch * seq
    a = jnp.arange(blk)[:, None]
    c = jnp.arange(2 * blk)[None, :]
    bidx = _rel_bucket(jnp.maximum(blk + a - c, 0) * dil)

    nu = ATTN_P_BLOCKS
    assert (n // blk) % nu == 0 and (nb % nu == 0 or nb == 1)

    def prev(i):
        return jnp.maximum(i * nu - 1, 0)

    return pl.pallas_call(
        functools.partial(_attn_p_body, g=g, nb=nb), name=f"attn_p{g}", grid=(n // blk // nu,),
        in_specs=[
            pl.BlockSpec(memory_space=pltpu.SMEM),
            pl.BlockSpec((blk, 2 * blk), lambda i: (0, 0)),
            pl.BlockSpec((nu * blk, GROUP_W_A), lambda i: (i, 0)),
            pl.BlockSpec((blk, GROUP_W_A), lambda i: (prev(i), 0)),
            pl.BlockSpec((nu * blk, GROUP_W_A), lambda i: (i, 0)),
            pl.BlockSpec((blk, GROUP_W_A), lambda i: (prev(i), 1)),
            pl.BlockSpec((nu * blk, GROUP_W_A), lambda i: (i, 1)),
        ],
        out_specs=pl.BlockSpec((nu * blk, O_EXT_W), lambda i: (i, 0)),
        out_shape=jax.ShapeDtypeStruct((n, O_EXT_W), BF16),
        scratch_shapes=[pltpu.VMEM((HEADS_PER_GROUP_A, blk, 2 * blk), F32)],
        compiler_params=_cparams(1),
    )(rel_bias, bidx, q, kv, kv, kv, kv)


def _attn_s_body(tab_ref, bidx_ref, q_ref, kvn_ref, tokn_ref, cache_hbm, o_ref, newc_hbm,
                 raw, kv_scr, bias_scr, sem_in, sem_out, *, g, dil, lb, n_req):
    t_new = q_ref.shape[0]
    nh = HEADS_PER_GROUP_A
    n_rows = nh * t_new
    n_keys = kv_scr.shape[0]
    n_col = 2 * nh
    lb8 = lb * n_col
    n = pl.program_id(0)
    nbuf = raw.shape[0]
    ahead = nbuf // 2
    slot = n % nbuf

    def fetch(req):
        s = req % nbuf
        return pltpu.make_async_copy(cache_hbm.at[req], raw.at[s, pl.ds(0, lb8)], sem_in.at[s])

    def flush(req):
        s = req % nbuf
        return pltpu.make_async_copy(raw.at[s, pl.ds(t_new * n_col, lb8)], newc_hbm.at[req], sem_out.at[s])

    @pl.when(n == 0)
    def _():
        kv_scr[lb:, :] = jnp.zeros((n_keys - lb, kv_scr.shape[1]), BF16)
        bidx = bidx_ref[...]
        for h in range(nh):
            bias_scr[h * t_new:(h + 1) * t_new, :] = _bias_from_table(tab_ref, bidx, g * nh + h)
        for r in range(ahead):
            fetch(r).start()

    @pl.when(n >= nbuf - ahead)
    def _():
        flush(n - (nbuf - ahead)).wait()

    @pl.when(n + ahead < n_req)
    def _():
        fetch(n + ahead).start()

    fetch(n).wait()
    raw[slot, lb8:lb8 + t_new * n_col, :] = tokn_ref[...]
    flush(n).start()
    new = kvn_ref[...]

    rows_per = min(lb, 256)
    for c in range(n_col):
        for r0 in range(0, lb, rows_per):
            kv_scr[r0:r0 + rows_per, c * LANES:(c + 1) * LANES] = raw[
                slot, pl.ds(r0 * n_col + c, rows_per, stride=n_col), :].astype(BF16)
    kv_scr[lb:lb + 2 * t_new, :] = jnp.concatenate([new, jnp.zeros_like(new)], axis=0).astype(BF16)

    q = q_ref[...]
    qt = jnp.concatenate([q] * nh, axis=0)
    rr = lax.broadcasted_iota(I32, (n_rows, GROUP_W_A), 0)
    cc = lax.broadcasted_iota(I32, (n_rows, GROUP_W_A), 1)
    qbd = jnp.where(cc // HEAD_DIM_A == rr // t_new, qt, 0.0).astype(BF16)

    s = lax.dot_general(qbd, kv_scr[:, :GROUP_W_A], (((1,), (1,)), ((), ())), preferred_element_type=F32)
    row = lax.broadcasted_iota(I32, (n_rows, n_keys), 0)
    col = lax.broadcasted_iota(I32, (n_rows, n_keys), 1)
    delta = lb + (row & (t_new - 1)) - col
    valid = (delta >= 0) & (delta <= lb) & ((delta & (dil - 1)) == 0)
    s = jnp.where(valid, s + bias_scr[...], NEG_INF)
    m = jnp.max(s, axis=-1, keepdims=True)
    p = jnp.exp(s - m)
    l = jnp.sum(p, axis=-1, keepdims=True)
    o = jnp.dot(p.astype(BF16), kv_scr[:, GROUP_W_A:], preferred_element_type=F32) / l
    lse = m + jnp.log(l)
    lane = lax.broadcasted_iota(I32, (t_new, LANES), 1)
    lse_blk = jnp.zeros((t_new, LANES), F32)
    outs = []
    for h in range(nh):
        outs.append(o[h * t_new:(h + 1) * t_new, h * HEAD_DIM_A:(h + 1) * HEAD_DIM_A])
        lse_blk = jnp.where(lane == h, lse[h * t_new:(h + 1) * t_new, :], lse_blk)
    o_ref[...] = jnp.concatenate(outs + [lse_blk, jnp.zeros_like(lse_blk)], axis=1)

    @pl.when(n == n_req - 1)
    def _():
        for back in range(nbuf - ahead - 1, -1, -1):
            flush(n - back).wait()


ATTN_S_BUFS = 4


def _attn_sample(q, kv_new, tok_new, cache, rel_bias, g, t_new):
    win, dil = DIL_GROUPS[g]
    n_col = 2 * HEADS_PER_GROUP_A
    n_req, lb = cache.shape[0], cache.shape[1] // n_col
    assert t_new & (t_new - 1) == 0 and dil & (dil - 1) == 0 and lb == win
    n_keys = lb + LANES
    t = jnp.arange(t_new)[:, None]
    c = jnp.arange(n_keys)[None, :]
    bidx = _rel_bucket(jnp.clip(lb + t - c, 0, lb))
    return pl.pallas_call(
        functools.partial(_attn_s_body, g=g, dil=dil, lb=lb, n_req=n_req), name=f"attn_s{g}", grid=(n_req,),
        in_specs=[
            pl.BlockSpec(memory_space=pltpu.SMEM),
            pl.BlockSpec((t_new, n_keys), lambda n: (0, 0)),
            pl.BlockSpec((t_new, GROUP_W_A), lambda n: (n, 0)),
            pl.BlockSpec((t_new, 2 * GROUP_W_A), lambda n: (n, 0)),
            pl.BlockSpec((t_new * n_col, LANES), lambda n: (n, 0)),
            pl.BlockSpec(memory_space=pl.ANY),
        ],
        out_specs=[pl.BlockSpec((t_new, O_EXT_W), lambda n: (n, 0)), pl.BlockSpec(memory_space=pl.ANY)],
        out_shape=[jax.ShapeDtypeStruct((n_req * t_new, O_EXT_W), F32), jax.ShapeDtypeStruct(cache.shape, F32)],
        scratch_shapes=[pltpu.VMEM((ATTN_S_BUFS, (lb + t_new) * n_col, LANES), F32),
                        pltpu.VMEM((n_keys, 2 * GROUP_W_A), BF16),
                        pltpu.VMEM((HEADS_PER_GROUP_A * t_new, n_keys), F32),
                        pltpu.SemaphoreType.DMA((ATTN_S_BUFS,)), pltpu.SemaphoreType.DMA((ATTN_S_BUFS,))],
        compiler_params=_cparams(1),
    )(rel_bias, bidx, q, kv_new, tok_new, cache)


def _gmlp_body(u_ref, v_ref, w_ref, b_ref, o_ref, *, period):
    c = w_ref.shape[1]
    nch = u_ref.shape[0] // c
    gd = u_ref.shape[1] // N_GROUPS_B
    i = lax.broadcasted_iota(I32, (c, c), 0)
    j = lax.broadcasted_iota(I32, (c, c), 1)
    mask = (j <= i) & ((i // period) == (j // period))
    b = b_ref[...]
    for g in range(N_GROUPS_B):
        gs = slice(g * gd, (g + 1) * gd)
        wg = jnp.where(mask, w_ref[g], 0.0).astype(BF16)
        vg = [v_ref[ch * c:(ch + 1) * c, gs].astype(BF16) for ch in range(nch)]
        vg = vg[0] if nch == 1 else jnp.concatenate(vg, axis=1)
        sg = jnp.dot(wg, vg, preferred_element_type=F32) + b[:, g:g + 1]
        for ch in range(nch):
            u = u_ref[ch * c:(ch + 1) * c, gs].astype(F32)
            o_ref[ch * c:(ch + 1) * c, gs] = (u * sg[:, ch * gd:(ch + 1) * gd]).astype(o_ref.dtype)


def _gmlp(u, v, w, b, tm, period):
    m, wb = u.shape
    c = w.shape[1]
    return pl.pallas_call(
        functools.partial(_gmlp_body, period=period), name="gmlp", grid=(m // tm,),
        in_specs=[pl.BlockSpec((tm, wb), lambda i: (i, 0)), pl.BlockSpec((tm, wb), lambda i: (i, 0)),
                  pl.BlockSpec((N_GROUPS_B, c, c), lambda i: (0, 0, 0)),
                  pl.BlockSpec((c, N_GROUPS_B), lambda i: (0, 0))],
        out_specs=pl.BlockSpec((tm, wb), lambda i: (i, 0)),
        out_shape=jax.ShapeDtypeStruct((m, wb), BF16), compiler_params=_cparams(1),
    )(u, v, w, b)


def _gmlp_new_body(u_ref, v_ref, w_ref, b_ref, o_ref):
    t = w_ref.shape[0]
    n = u_ref.shape[0] // t
    width = u_ref.shape[1]
    v = v_ref[...].astype(F32).reshape(n, t, width)
    sg = jnp.broadcast_to(b_ref[...][None], (n, t, width))
    p_row = lax.broadcasted_iota(I32, (t, width), 0)
    for q in range(t):
        sg = sg + jnp.where(p_row >= q, w_ref[q], 0.0)[None] * v[:, q:q + 1, :]
    o_ref[...] = (u_ref[...].astype(F32).reshape(n, t, width) * sg).reshape(n * t, width).astype(o_ref.dtype)


def _gmlp_new(u, v, w_s, b_s, t):
    m, wb = u.shape
    gd = wb // N_GROUPS_B
    wq = jnp.repeat(jnp.transpose(w_s[:, :t, :t], (2, 1, 0)), gd, axis=-1)
    bq = jnp.repeat(b_s[:, :t].T, gd, axis=-1)
    whole = lambda a: pl.BlockSpec(a.shape, lambda i: (0,) * a.ndim)
    return pl.pallas_call(
        _gmlp_new_body, name="gmlp_new", grid=(1,),
        in_specs=[whole(u), whole(v), whole(wq), whole(bq)], out_specs=whole(u),
        out_shape=jax.ShapeDtypeStruct((m, wb), BF16), compiler_params=_cparams(1),
    )(u, v, wq, bq)


def _memattn_body(q_ref, kv_ref, o_ref, *, tiles):
    wm = N_HEADS_M * HEAD_DIM_M
    n_ct = HEAD_DIM_M // LANES
    per_key = 2 * n_ct * N_HEADS_M
    q = q_ref[...].astype(BF16)

    def head_slab(kv, h):
        if not tiles:
            return kv_ref[:, kv * wm + h * HEAD_DIM_M:kv * wm + (h + 1) * HEAD_DIM_M].astype(BF16)
        n_keys = kv_ref.shape[0] // per_key
        parts = [kv_ref[pl.ds((kv * n_ct + ct) * N_HEADS_M + h, n_keys, stride=per_key), :] for ct in range(n_ct)]
        return jnp.concatenate(parts, axis=1).astype(BF16)

    outs = []
    for h in range(N_HEADS_M):
        hs = slice(h * HEAD_DIM_M, (h + 1) * HEAD_DIM_M)
        k = head_slab(0, h)
        v = head_slab(1, h)
        s = lax.dot_general(q[:, hs], k, (((1,), (1,)), ((), ())), preferred_element_type=F32)
        m = jnp.max(s, axis=-1, keepdims=True)
        p = jnp.exp(s - m)
        l = jnp.sum(p, axis=-1, keepdims=True)
        outs.append(jnp.dot(p.astype(BF16), v, preferred_element_type=F32) / l)
    o_ref[...] = jnp.concatenate(outs, axis=1).astype(o_ref.dtype)


def _memattn(q, kv, tq, out_dtype, tiles=False):
    m, wm = q.shape
    n = kv.shape[0]
    per = m // n // tq
    kv_spec = pl.BlockSpec((None,) + kv.shape[1:], lambda i: (i // per,) + (0,) * (kv.ndim - 1))
    return pl.pallas_call(
        functools.partial(_memattn_body, tiles=tiles), name="memattn", grid=(m // tq,),
        in_specs=[pl.BlockSpec((tq, wm), lambda i: (i, 0)), kv_spec],
        out_specs=pl.BlockSpec((tq, wm), lambda i: (i, 0)),
        out_shape=jax.ShapeDtypeStruct((m, wm), out_dtype), compiler_params=_cparams(1),
    )(q, kv)


def _mix_body(gt_ref, o0_ref, o1_ref, o2_ref, ob_ref, om_ref, wpa_ref, wpb_ref, wpm_ref, z_ref, *, dils):
    d = z_ref.shape[1]
    tm = z_ref.shape[0]
    os_, ls = [], []
    for o_ref, dil in zip((o0_ref, o1_ref, o2_ref), dils):
        if dil == 1:
            x = o_ref[...].astype(F32)
        else:
            rows = jnp.concatenate([o_ref[r] for r in range(dil)], axis=0)
            x = jnp.dot(_perm_matrix(tm, dil, inverse=True), rows, preferred_element_type=F32)
        os_.append(x[:, :GROUP_W_A])
        ls.append(x[:, GROUP_W_A:GROUP_W_A + LANES] + x[:, GROUP_W_A + LANES:])
    l0, l1, l2 = ls
    mx = jnp.maximum(jnp.maximum(l0, l1), l2)
    e0, e1, e2 = jnp.exp(l0 - mx), jnp.exp(l1 - mx), jnp.exp(l2 - mx)
    den = e0 + e1 + e2
    w0, w1, w2 = e0 / den, e1 / den, e2 / den
    cols = []
    for h in range(HEADS_PER_GROUP_A):
        hs = slice(h * HEAD_DIM_A, (h + 1) * HEAD_DIM_A)
        cols.append(w0[:, h:h + 1] * os_[0][:, hs] + w1[:, h:h + 1] * os_[1][:, hs] + w2[:, h:h + 1] * os_[2][:, hs])
    oa = jnp.concatenate(cols, axis=1).astype(BF16)
    pa = jnp.dot(oa, wpa_ref[...], preferred_element_type=F32)
    pb = jnp.dot(ob_ref[...].astype(BF16), wpb_ref[...], preferred_element_type=F32)
    pm = jnp.dot(om_ref[...].astype(BF16), wpm_ref[...], preferred_element_type=F32)
    z = (gt_ref[:, 0:d].astype(F32) * pa + gt_ref[:, d:2 * d].astype(F32) * pb
         + gt_ref[:, 2 * d:3 * d].astype(F32) * pm)
    z_ref[...] = z.astype(z_ref.dtype)


def _mix(gates, o_list, dils, seq, ob, om, wpa, wpb, wpm, tm):
    m = gates.shape[0]
    d = wpa.shape[1]
    mt = seq // tm if any(dl > 1 for dl in dils) else 1

    def rows(width):
        return pl.BlockSpec((tm, width), lambda i: (i, 0))

    def whole(a):
        return pl.BlockSpec(a.shape, lambda i: (0, 0))

    o_specs, o_args = [], []
    for o, dil in zip(o_list, dils):
        if dil == 1:
            o_specs.append(rows(O_EXT_W))
            o_args.append(o)
        else:
            assert seq % tm == 0 and (tm // dil) % 16 == 0
            o_specs.append(pl.BlockSpec((None, dil, tm // dil, O_EXT_W), lambda i: (i // mt, 0, i % mt, 0)))
            o_args.append(o.reshape(m // seq, dil, seq // dil, O_EXT_W))
    return pl.pallas_call(
        functools.partial(_mix_body, dils=tuple(dils)), name="mix", grid=(m // tm,),
        in_specs=[rows(gates.shape[1])] + o_specs
                 + [rows(ob.shape[1]), rows(om.shape[1]), whole(wpa), whole(wpb), whole(wpm)],
        out_specs=rows(d),
        out_shape=jax.ShapeDtypeStruct((m, d), BF16), compiler_params=_cparams(1),
    )(gates, *o_args, ob, om, wpa, wpb, wpm)


def _route(logits):
    lane = lax.broadcasted_iota(I32, logits.shape, 1)
    lane_f = lane.astype(F32)
    is_g = lane < N_EXPERT_GROUPS
    gmax = jnp.max(jnp.where(is_g, logits, -jnp.inf), axis=1, keepdims=True)
    gsel = jnp.min(jnp.where(is_g & (logits == gmax), lane_f, float(LANES)), axis=1, keepdims=True).astype(I32)
    gden = jnp.sum(jnp.where(is_g, jnp.exp(logits - gmax), 0.0), axis=1, keepdims=True)
    pg = 1.0 / gden
    e_lane = lane - N_EXPERT_GROUPS
    in_grp = (e_lane >= 0) & (e_lane < N_EXPERTS) & ((e_lane // EXPERTS_PER_GROUP) == gsel)
    m1 = jnp.max(jnp.where(in_grp, logits, -jnp.inf), axis=1, keepdims=True)
    i1 = jnp.min(jnp.where(in_grp & (logits == m1), lane_f, float(LANES)), axis=1, keepdims=True).astype(I32)
    rest = in_grp & (lane != i1)
    m2 = jnp.max(jnp.where(rest, logits, -jnp.inf), axis=1, keepdims=True)
    i2 = jnp.min(jnp.where(rest & (logits == m2), lane_f, float(LANES)), axis=1, keepdims=True).astype(I32)
    e2 = jnp.exp(m2 - m1)
    w1 = pg / (1.0 + e2)
    w2 = pg * e2 / (1.0 + e2)
    eid = jnp.where(lane == 0, i1 - N_EXPERT_GROUPS, jnp.where(lane == 1, i2 - N_EXPERT_GROUPS, 0))
    wts = jnp.where(lane == 0, w1, jnp.where(lane == 1, w2, 0.0))
    return eid, wts


def _resid_body(x_ref, z_ref, wo_ref, gf_ref, wr_ref, br_ref, xmid_ref, hpk_ref, eid_ref, wts_ref):
    xm = x_ref[...] + jnp.dot(z_ref[...], wo_ref[...], preferred_element_type=F32)
    xmid_ref[...] = xm
    hf = _rms(xm, gf_ref[...])
    wr = wr_ref[...]
    wr_hi = wr.astype(BF16)
    wr_lo = (wr - wr_hi.astype(F32)).astype(BF16)
    hf_hi = hf.astype(BF16)
    hf_lo = (hf - hf_hi.astype(F32)).astype(BF16)
    logits = (jnp.dot(hf_hi, wr_hi, preferred_element_type=F32) + jnp.dot(hf_hi, wr_lo, preferred_element_type=F32)
              + jnp.dot(hf_lo, wr_hi, preferred_element_type=F32) + br_ref[...])
    eid, wts = _route(logits)
    eid_ref[...] = eid
    wts_ref[...] = wts
    _store_token_tiles(hpk_ref, hf)


def _resid(x2d, z, wo, g_ffn, w_r, b_r, tm):
    m, d = x2d.shape

    def rows(width):
        return pl.BlockSpec((tm, width), lambda i: (i, 0))

    def whole(a):
        return pl.BlockSpec(a.shape, lambda i: (0, 0))

    gf = g_ffn.reshape(1, d)
    nt = d // 2 // LANES
    return pl.pallas_call(
        _resid_body, name="resid", grid=(m // tm,),
        in_specs=[rows(d), rows(d), whole(wo), whole(gf), whole(w_r), whole(b_r)],
        out_specs=[rows(d), pl.BlockSpec((tm * nt, LANES), lambda i: (i, 0)), rows(LANES), rows(LANES)],
        out_shape=[jax.ShapeDtypeStruct((m, d), F32), jax.ShapeDtypeStruct((m * nt, LANES), U32),
                   jax.ShapeDtypeStruct((m, LANES), I32), jax.ShapeDtypeStruct((m, LANES), F32)],
        compiler_params=_cparams(1),
    )(x2d, z, wo, gf, w_r, b_r)


TOKEN_TILE = 8
GATHER_UNROLL = 8


def _gather_rows(idx_at, src_hbm, dst, sem, n):
    def body(i, carry):
        src = pl.multiple_of(idx_at(i), TOKEN_TILE)
        dst_row = pl.multiple_of(i * TOKEN_TILE, TOKEN_TILE)
        pltpu.make_async_copy(src_hbm.at[pl.ds(src, TOKEN_TILE)], dst.at[pl.ds(dst_row, TOKEN_TILE)], sem).start()
        return carry
    lax.fori_loop(0, n, body, 0, unroll=GATHER_UNROLL)


FFN_CHUNKS = 8
FFN_BUFS = 3


def _ffn_body(blk_e_ref, nused_ref, eord_ref, enext_ref, idx_hbm, h_hbm, wg_hbm, wu_hbm, wd_hbm, y_ref,
              idx_s, isem, buf, sem, x_scr, a_scr, b_scr, hm_scr, wg_f, wu_f, wd_f, wsem, wg_s, wu_s, wd_s):
    j = pl.program_id(0)
    nused = nused_ref[0]
    last = nused - 1
    rows = buf.shape[1] // TOKEN_TILE
    e = blk_e_ref[j]

    def idx_copy(b):
        s = b % FFN_BUFS
        return pltpu.make_async_copy(idx_hbm.at[jnp.minimum(b, last)], idx_s.at[s], isem.at[s])

    def wait_rows(s):
        pltpu.make_async_copy(h_hbm.at[pl.ds(0, rows * TOKEN_TILE)], buf.at[s], sem.at[s]).wait()
    new_expert = (j == 0) | (e != blk_e_ref[jnp.maximum(j - 1, 0)])
    ws = eord_ref[j] % 2

    def fetch_w(expert, s):
        return [pltpu.make_async_copy(src.at[expert], dst.at[s], wsem.at[s, i])
                for i, (src, dst) in enumerate(((wg_hbm, wg_f), (wu_hbm, wu_f), (wd_hbm, wd_f)))]

    @pl.when(j == 0)
    def _():
        for c in fetch_w(e, ws):
            c.start()
        for b in range(FFN_BUFS):
            idx_copy(b).start()
        for b in range(FFN_BUFS - 1):
            idx_copy(b).wait()
            _gather_rows(lambda i, b=b: idx_s[b, 0, i], h_hbm, buf.at[b], sem.at[b], rows)

    @pl.when(new_expert)
    def _():
        nxt = enext_ref[j]

        @pl.when(nxt >= 0)
        def _():
            for c in fetch_w(nxt, 1 - ws):
                c.start()

        for c in fetch_w(e, ws):
            c.wait()
        wg_s[...] = wg_f[ws].astype(BF16)
        wu_s[...] = wu_f[ws].astype(BF16)
        wd_s[...] = wd_f[ws].astype(BF16)

    @pl.when(j < nused)
    def _():
        slot = j % FFN_BUFS
        s2 = (j + 2) % FFN_BUFS
        per = rows // FFN_CHUNKS
        idx_copy(j + 2).wait()

        @pl.when(j < last)
        def _():
            idx_copy(j + 3).start()

        def issue(c):
            for i in range(c * per, (c + 1) * per):
                src = pl.multiple_of(idx_s[s2, 0, i], TOKEN_TILE)
                pltpu.make_async_copy(h_hbm.at[pl.ds(src, TOKEN_TILE)],
                                      buf.at[s2, pl.ds(i * TOKEN_TILE, TOKEN_TILE)], sem.at[s2]).start()

        wait_rows(slot)
        x_scr[...] = _load_token_tiles(buf, slot, 0, rows, TOKEN_TILE).astype(BF16)
        dn = a_scr.shape[1] // 2
        for c in range(2):
            issue(c)
            a_scr[:, c * dn:(c + 1) * dn] = jnp.dot(x_scr[...], wg_s[:, c * dn:(c + 1) * dn],
                                                     preferred_element_type=F32)
        for c in range(2):
            issue(2 + c)
            b_scr[:, c * dn:(c + 1) * dn] = jnp.dot(x_scr[...], wu_s[:, c * dn:(c + 1) * dn],
                                                     preferred_element_type=F32)
        a = a_scr[...]
        hm_scr[...] = (a * jax.nn.sigmoid(a) * b_scr[...]).astype(BF16)
        half = wd_s.shape[1] // 2
        n_dc = FFN_CHUNKS - 4
        dq = half // n_dc
        for c in range(n_dc):
            issue(4 + c)
            lo = jnp.dot(hm_scr[...], wd_s[:, c * dq:(c + 1) * dq], preferred_element_type=F32)
            hi = jnp.dot(hm_scr[...], wd_s[:, half + c * dq:half + (c + 1) * dq], preferred_element_type=F32)
            w = _pack_bf16_pair(jnp.concatenate([lo, hi], axis=1))
            for t in range(dq // LANES):
                y_ref[pl.ds(c * (dq // LANES) + t, rows, stride=TOKEN_TILE), :] = w[:, t * LANES:(t + 1) * LANES]

        @pl.when(j == last)
        def _():
            wait_rows((j + 1) % FFN_BUFS)
            wait_rows(s2)

    @pl.when(j >= nused)
    def _():
        y_ref[...] = jnp.zeros(y_ref.shape, y_ref.dtype)


def _ffn(hpk, src_tok, blk_e, nused, eord, enext, wg, wu, wd):
    n_blocks = blk_e.shape[0]
    rows = MOE_ROWS
    tiles = rows * TOKEN_TILE
    d, de = wg.shape[1], wg.shape[2]
    assert d // 2 == TOKEN_TILE * LANES and rows % FFN_CHUNKS == 0 and (d // 2) % ((FFN_CHUNKS - 4) * LANES) == 0
    assert n_blocks >= N_EXPERTS + FFN_BUFS
    idx = (src_tok * TOKEN_TILE).reshape(n_blocks, 1, rows)
    hbm = pl.BlockSpec(memory_space=pl.ANY)
    grid_spec = pltpu.PrefetchScalarGridSpec(
        num_scalar_prefetch=4, grid=(n_blocks,),
        in_specs=[hbm, hbm, hbm, hbm, hbm],
        out_specs=pl.BlockSpec((tiles, LANES), lambda j, *_: (j, 0)),
        scratch_shapes=[pltpu.SMEM((FFN_BUFS, 1, rows), I32), pltpu.SemaphoreType.DMA((FFN_BUFS,)),
                        pltpu.VMEM((FFN_BUFS, tiles, LANES), U32), pltpu.SemaphoreType.DMA((FFN_BUFS,)),
                        pltpu.VMEM((rows, d), BF16), pltpu.VMEM((rows, de), F32), pltpu.VMEM((rows, de), F32),
                        pltpu.VMEM((rows, de), BF16),
                        pltpu.VMEM((2, d, de), wg.dtype), pltpu.VMEM((2, d, de), wu.dtype),
                        pltpu.VMEM((2, de, d), wd.dtype), pltpu.SemaphoreType.DMA((2, 3)),
                        pltpu.VMEM((d, de), BF16), pltpu.VMEM((d, de), BF16), pltpu.VMEM((de, d), BF16)],
    )
    return pl.pallas_call(
        _ffn_body, name="ffn", grid_spec=grid_spec,
        out_shape=jax.ShapeDtypeStruct((n_blocks * tiles, LANES), U32), compiler_params=_cparams(1),
    )(blk_e, nused, eord, enext, idx, hpk, wg, wu, wd)


def _combine_body(idx_cur_ref, idx_nxt_ref, x_ref, w_ref, y_hbm, o_ref, buf, sem, *, n):
    i = pl.program_id(0)
    slot = i % 2
    rows = buf.shape[1] // TOKEN_TILE
    tm = rows // TOP_K

    @pl.when(i == 0)
    def _():
        _gather_rows(lambda r: idx_cur_ref[0, 0, r], y_hbm, buf.at[0], sem.at[0], rows)

    if n > 1:
        @pl.when(i + 1 < n)
        def _():
            _gather_rows(lambda r: idx_nxt_ref[0, 0, r], y_hbm, buf.at[1 - slot], sem.at[1 - slot], rows)

    pltpu.make_async_copy(y_hbm.at[pl.ds(0, rows * TOKEN_TILE)], buf.at[slot], sem.at[slot]).wait()
    w = w_ref[...]
    y0 = _load_token_tiles(buf, slot, 0, tm, TOKEN_TILE)
    y1 = _load_token_tiles(buf, slot, tm, tm, TOKEN_TILE)
    o_ref[...] = x_ref[...] + (w[:, 0:1] * y0 + w[:, 1:2] * y1)


def _combine(xmid, wts, dest, ypk, tm):
    m, d = xmid.shape
    nt = m // tm
    assert d // 2 == TOKEN_TILE * LANES
    idx = (dest * TOKEN_TILE).reshape(nt, tm, TOP_K).transpose(0, 2, 1).reshape(nt, 1, TOP_K * tm)

    def idx_spec(ahead):
        return pl.BlockSpec((1, 1, TOP_K * tm), lambda i: (jnp.minimum(i + ahead, nt - 1), 0, 0),
                            memory_space=pltpu.SMEM)

    return pl.pallas_call(
        functools.partial(_combine_body, n=nt), name="combine", grid=(nt,),
        in_specs=[idx_spec(0), idx_spec(1), pl.BlockSpec((tm, d), lambda i: (i, 0)),
                  pl.BlockSpec((tm, LANES), lambda i: (i, 0)), pl.BlockSpec(memory_space=pl.ANY)],
        out_specs=pl.BlockSpec((tm, d), lambda i: (i, 0)),
        out_shape=jax.ShapeDtypeStruct((m, d), F32),
        scratch_shapes=[pltpu.VMEM((2, TOP_K * tm * TOKEN_TILE, LANES), U32), pltpu.SemaphoreType.DMA((2,))],
        compiler_params=_cparams(1),
    )(idx, idx, xmid, wts, ypk)


def _moe_plan(eid2):
    n_slot = eid2.shape[0] * TOP_K
    eid = eid2.reshape(n_slot)
    onehot = (eid[:, None] == jnp.arange(N_EXPERTS, dtype=I32)[None, :]).astype(I32)
    csum = jnp.cumsum(onehot, axis=0)
    rank = jnp.take_along_axis(csum, eid[:, None], axis=1)[:, 0] - 1
    counts = csum[-1]
    padded = (counts + MOE_ROWS - 1) // MOE_ROWS * MOE_ROWS
    pend = jnp.cumsum(padded)
    pstart = pend - padded
    dest = pstart[eid] + rank
    n_blocks = -(-n_slot // MOE_ROWS) + N_EXPERTS
    nused = (pend[-1] // MOE_ROWS).astype(I32)
    blk = jnp.minimum(jnp.arange(n_blocks, dtype=I32), nused - 1)
    blk_e = jnp.minimum(jnp.searchsorted(pend, blk * MOE_ROWS, side="right"), N_EXPERTS - 1).astype(I32)
    n_rows = n_blocks * MOE_ROWS
    src_tok = (jnp.arange(n_rows, dtype=I32) % (n_slot // TOP_K)).at[dest].set(jnp.arange(n_slot, dtype=I32) // TOP_K)
    has_rows = counts > 0
    ord_e = jnp.cumsum(has_rows.astype(I32)) - 1
    ids = jnp.arange(N_EXPERTS, dtype=I32)
    later = has_rows[None, :] & (ids[None, :] > ids[:, None])
    next_e = jnp.where(later.any(axis=1), jnp.argmax(later, axis=1), -1).astype(I32)
    return dest.astype(I32), src_tok, blk_e, nused.reshape(1), ord_e[blk_e].astype(I32), next_e[blk_e]


def _mixer(x2d, tm, p, attn_fn, gmlp_fn, mem_fn):
    h = _prep(x2d, p["g_mix"], tm)
    qdt = p["q_dtype"]
    wi, sec = p["w_in"], p["sec"]
    o_list, dils, seq = attn_fn(h)
    ub = _proj(h, wi, (sec["u"], 1, 1, sec["vb"] - sec["u"]), tm, "none", BF16, name="proj_ub")
    n_vb = sec["qm"] - sec["vb"]
    vb = _proj(h, wi, (sec["vb"], n_vb, n_vb, 1), tm, "norm", p["vb_dtype"], p["g_vb"], n_vb * W_COLS,
               name="proj_vb")
    qm = _proj(h, wi, (sec["qm"], 1, 1, sec["gt"] - sec["qm"]), tm, "norm", qdt, p["g_qm"], HEAD_DIM_M,
               HEAD_DIM_M ** -0.5, name="proj_qm")
    gates = _proj(h, wi, (sec["gt"], 2, 2, (sec["end"] - sec["gt"]) // 2), tm, "sigmoid", BF16, name="proj_gates")
    ob = gmlp_fn(ub, vb)
    om = mem_fn(qm)
    z = _mix(gates, o_list, dils, seq, ob, om, p["w_pa"], p["w_pb"], p["w_pm"], min(tm, 256))
    xmid, hpk, eid, wts = _resid(x2d, z, p["w_o"], p["g_ffn"], p["w_r"], p["b_r"], min(tm, 256))
    return vb, xmid, hpk, eid, wts


def kernel(x_prompt, x_sample, mem_prompt, cache_a0_kv, cache_a1_kv, cache_a2_kv, cache_mem_kv, rel_bias, g_mix, w_in, g_qa, g_ka, w_pa, g_vb, w_s, b_s, w_pb, g_mem, w_mk, w_mv, g_qm, g_km, w_pm, w_o, g_ffn, w_rg, b_rg, w_re, b_re, w_gate, w_up, w_down):
    n_b, seq, d = x_prompt.shape
    n_s, t_s, _ = x_sample.shape
    depth = w_in.shape[0]
    assert depth == 1
    l = 0
    caches = (cache_a0_kv, cache_a1_kv, cache_a2_kv)
    n_g = len(DIL_GROUPS)
    wb = w_pb.shape[1]
    wm = N_HEADS_M * HEAD_DIM_M

    offs = [0, WIDTH_A, 2 * WIDTH_A, 3 * WIDTH_A, 3 * WIDTH_A + wb, 3 * WIDTH_A + 2 * wb,
            3 * WIDTH_A + 2 * wb + wm, w_in.shape[2]]
    assert all(o % W_COLS == 0 for o in offs) and (offs[7] - offs[6]) % (2 * W_COLS) == 0
    sec = dict(zip(("q", "k", "v", "u", "vb", "qm", "gt", "end"), (o // W_COLS for o in offs)))
    p = {
        "g_mix": g_mix[l], "g_qa": g_qa[l], "g_ka": g_ka[l], "g_vb": g_vb[l], "g_qm": g_qm[l], "g_ffn": g_ffn[l],
        "w_in": w_in[l], "sec": sec,
        "w_pa": w_pa[l].astype(BF16), "w_pb": w_pb[l].astype(BF16), "w_pm": w_pm[l].astype(BF16),
        "w_o": w_o[l].astype(BF16),
    }
    n_r = N_EXPERT_GROUPS + N_EXPERTS
    p["w_r"] = jnp.concatenate([w_rg[l], w_re[l], jnp.zeros((d, LANES - n_r), F32)], axis=1)
    p["b_r"] = jnp.concatenate([b_rg[l], b_re[l], jnp.zeros((LANES - n_r,), F32)]).reshape(1, LANES)

    n_mem = mem_prompt.shape[1]
    h_mem = _prep(mem_prompt.reshape(n_b * n_mem, d), g_mem[l], 256)
    w_mkv = jnp.concatenate([w_mk[l], w_mv[l]], axis=1)
    n_mw = wm // W_COLS
    mkv = _proj(h_mem, w_mkv, (0, n_mw, n_mw, 2), 256, "norm_first", F32, g_km[l], HEAD_DIM_M, name="proj_mkv")
    new_mem_p = mkv.reshape(1, n_b, n_mem, 2, N_HEADS_M, HEAD_DIM_M)

    ps = dict(p, q_dtype=F32, vb_dtype=F32)
    m_s = n_s * t_s
    n_col = 2 * HEADS_PER_GROUP_A
    new_s = []

    def attn_sample(h):
        o_list = []
        for g in range(n_g):
            q, kv, tok = _proj_qkv(h, w_in[l], sec, g, g_qa[l], g_ka[l], 1, m_s, m_s, 1, act_dtype=F32)
            cache3 = caches[g][l].reshape(n_s, caches[g].shape[2] * n_col, LANES)
            o, newc = _attn_sample(q, kv, tok, cache3, rel_bias, g, t_s)
            o_list.append(o)
            new_s.append(newc.reshape(caches[g][l:l + 1].shape))
        return o_list, [1] * n_g, t_s

    n_ct = HEAD_DIM_M // LANES
    mem_tiles = (cache_mem_kv[l].reshape(n_s, n_mem, 2, N_HEADS_M, n_ct, LANES).transpose(0, 1, 2, 4, 3, 5)
                 .reshape(n_s, n_mem * 2 * n_ct * N_HEADS_M, LANES))
    vb_s, xmid_s, hpk_s, eid_s, wts_s = _mixer(
        x_sample.reshape(m_s, d), m_s, ps, attn_sample,
        lambda u, v: _gmlp_new(u, v, w_s[l], b_s[l], t_s),
        lambda qm: _memattn(qm, mem_tiles, t_s, F32, tiles=True))
    new_vb_s = vb_s.reshape(1, n_s, t_s, wb)

    pp = dict(p, q_dtype=BF16, vb_dtype=BF16)
    new_p = []

    def attn_prompt(h):
        o_list = []
        for g, (win, dil) in enumerate(DIL_GROUPS):
            q, kv, tok = _proj_qkv(h, w_in[l], sec, g, g_qa[l], g_ka[l], n_b, seq, 512, dil)
            o_list.append(_attn_prompt(q, kv, rel_bias, g, n_b, seq))
            keep = min(win, seq)
            kv5 = tok.reshape(n_b, seq, 2, HEADS_PER_GROUP_A, HEAD_DIM_A)
            new_p.append((kv5 if keep == seq else kv5[:, seq - keep:])[None])
        return o_list, [dl for _, dl in DIL_GROUPS], seq

    _, xmid_p, hpk_p, eid_p, wts_p = _mixer(
        x_prompt.reshape(n_b * seq, d), 512, pp, attn_prompt,
        lambda u, v: _gmlp(u, v, w_s[l], b_s[l].T, 512, CHUNK),
        lambda qm: _memattn(qm, mkv.reshape(n_b, n_mem, 2 * wm), 256, BF16))

    n_p = n_b * seq
    hpk = jnp.concatenate([hpk_p, hpk_s], axis=0)
    eid2 = jnp.concatenate([eid_p[:, :TOP_K], eid_s[:, :TOP_K]], axis=0)
    dest, src_tok, blk_e, nused, eord, enext = _moe_plan(eid2)
    ypk = _ffn(hpk, src_tok, blk_e, nused, eord, enext, w_gate[l], w_up[l], w_down[l])
    dest2 = dest.reshape(-1, TOP_K)
    y_p = _combine(xmid_p, wts_p, dest2[:n_p], ypk, 256)
    y_s = _combine(xmid_s, wts_s, dest2[n_p:], ypk, m_s)

    return (y_p.reshape(n_b, seq, d), y_s.reshape(n_s, t_s, d), new_p[0], new_p[1], new_p[2], new_mem_p,
            new_s[0], new_s[1], new_s[2], new_vb_s)
```

```python
import functools
import math

import jax
import jax.numpy as jnp
from jax import lax
from jax.experimental import pallas as pl
from jax.experimental.pallas import tpu as pltpu

F32 = jnp.float32
BF16 = jnp.bfloat16
I32 = jnp.int32
U32 = jnp.uint32

EPS = 1e-6
NEG_INF = -1e30

HEAD_DIM_A = 128
HEADS_PER_GROUP_A = 4
DIL_GROUPS = ((128, 1), (512, 4), (2048, 16))
GROUP_W_A = HEADS_PER_GROUP_A * HEAD_DIM_A
WIDTH_A = len(DIL_GROUPS) * GROUP_W_A
CHUNK = 128
N_GROUPS_B = 8
N_HEADS_M = 4
HEAD_DIM_M = 256
N_BUCKETS = 32
MAX_EXACT = N_BUCKETS // 2
MAX_DISTANCE = 2048
N_EXPERT_GROUPS = 4
EXPERTS_PER_GROUP = 8
N_EXPERTS = N_EXPERT_GROUPS * EXPERTS_PER_GROUP
TOP_K = 2
LANES = 128
MOE_ROWS = 256
VMEM_LIMIT = 56 * 1024 * 1024


def _cparams(n_grid, vmem=VMEM_LIMIT):
    return pltpu.CompilerParams(dimension_semantics=("arbitrary",) * n_grid, vmem_limit_bytes=vmem)


def _rms(x, g):
    return x * lax.rsqrt(jnp.mean(x * x, axis=-1, keepdims=True) + EPS) * g


def _pack_bf16_pair(x):
    n = x.shape[1] // 2
    lo = lax.bitcast_convert_type(x[:, :n].astype(BF16).astype(F32), U32)
    hi = lax.bitcast_convert_type(x[:, n:].astype(BF16).astype(F32), U32)
    return (hi & jnp.uint32(0xFFFF0000)) | (lo >> 16)


def _unpack_bf16_pair(w):
    lo = lax.bitcast_convert_type(w << 16, F32)
    hi = lax.bitcast_convert_type(w & jnp.uint32(0xFFFF0000), F32)
    return lo, hi


def _store_token_tiles(ref, x):
    w = _pack_bf16_pair(x)
    m, n = w.shape
    nt = n // LANES
    for c in range(nt):
        ref[pl.ds(c, m, stride=nt), :] = w[:, c * LANES:(c + 1) * LANES]


def _load_token_tiles(ref, lead, row0, m, nt):
    los, his = [], []
    view = ref if lead is None else ref.at[lead]
    for c in range(nt):
        lo, hi = _unpack_bf16_pair(view[pl.ds(row0 * nt + c, m, stride=nt), :])
        los.append(lo)
        his.append(hi)
    return jnp.concatenate(los + his, axis=1)


def _rel_bucket(dist):
    d = jnp.maximum(dist, 1).astype(F32)
    large = MAX_EXACT + (jnp.log(d / MAX_EXACT) / math.log(MAX_DISTANCE / MAX_EXACT)
                         * (N_BUCKETS - MAX_EXACT)).astype(I32)
    return jnp.where(dist < MAX_EXACT, dist, jnp.minimum(large, N_BUCKETS - 1)).astype(I32)


def _prep_body(x_ref, g_ref, o_ref):
    o_ref[...] = _rms(x_ref[...], g_ref[...]).astype(o_ref.dtype)


def _prep(x2d, g, tm):
    m, d = x2d.shape
    return pl.pallas_call(
        _prep_body, name="prep", grid=(m // tm,),
        in_specs=[pl.BlockSpec((tm, d), lambda i: (i, 0)), pl.BlockSpec((1, d), lambda i: (0, 0))],
        out_specs=pl.BlockSpec((tm, d), lambda i: (i, 0)),
        out_shape=jax.ShapeDtypeStruct((m, d), BF16), compiler_params=_cparams(1),
    )(x2d, g.reshape(1, d))


def _headnorm(acc, g, hd, scale):
    outs = []
    for j in range(acc.shape[1] // hd):
        sl = acc[:, j * hd:(j + 1) * hd]
        outs.append(sl * lax.rsqrt(jnp.mean(sl * sl, axis=-1, keepdims=True) + EPS))
    y = outs[0] if len(outs) == 1 else jnp.concatenate(outs, axis=1)
    y = y * g
    return y * scale if scale != 1.0 else y


def _proj_body(h_ref, *refs, n_w, mode, hd, scale):
    w_refs, (g_ref, o_ref, w_scr) = refs[:n_w], refs[n_w:]
    cw = w_refs[0].shape[1]

    @pl.when(pl.program_id(1) == 0)
    def _():
        for i, w_ref in enumerate(w_refs):
            w_scr[:, i * cw:(i + 1) * cw] = w_ref[...].astype(w_scr.dtype)

    def run(epilogue):
        acc = jnp.dot(h_ref[...], w_scr[...], preferred_element_type=F32)
        o_ref[...] = epilogue(acc).astype(o_ref.dtype)

    normed = lambda acc: _headnorm(acc, g_ref[...], hd, scale)
    if mode == "none":
        run(lambda acc: acc)
    elif mode == "sigmoid":
        run(jax.nn.sigmoid)
    elif mode == "norm":
        run(normed)
    else:
        @pl.when(pl.program_id(0) == 0)
        def _():
            run(normed)

        @pl.when(pl.program_id(0) != 0)
        def _():
            run(lambda acc: acc)


W_COLS = 512


def _proj(h, w, cols, tm, mode, out_dtype, gain=None, hd=None, scale=1.0, name="proj"):
    m, k = h.shape
    base, stride, n_w, n_tiles = cols
    tn = n_w * W_COLS
    if gain is None:
        g = jnp.ones((1, tn), F32)
    else:
        g = jnp.tile(gain.astype(F32).reshape(1, -1), (1, tn // gain.shape[-1]))
    body = functools.partial(_proj_body, n_w=n_w, mode=mode, hd=hd, scale=scale)
    w_specs = [pl.BlockSpec((k, W_COLS), functools.partial(lambda j, i, off: (0, base + stride * j + off), off=off))
               for off in range(n_w)]
    return pl.pallas_call(
        body, name=name, grid=(n_tiles, m // tm),
        in_specs=[pl.BlockSpec((tm, k), lambda j, i: (i, 0))] + w_specs + [pl.BlockSpec((1, tn), lambda j, i: (0, 0))],
        out_specs=pl.BlockSpec((tm, tn), lambda j, i: (i, j)),
        out_shape=jax.ShapeDtypeStruct((m, n_tiles * tn), out_dtype),
        scratch_shapes=[pltpu.VMEM((k, tn), BF16)], compiler_params=_cparams(2),
    )(h, *([w] * n_w), g)


PERM_ROWS = 256


def _perm_matrix(n, dil, inverse=False):
    per = n // dil
    o = lax.broadcasted_iota(I32, (n, n), 1 if inverse else 0)
    s = lax.broadcasted_iota(I32, (n, n), 0 if inverse else 1)
    return (s == (o % per) * dil + o // per).astype(BF16)


def _qkv_body(h_ref, wq_ref, wk_ref, wv_ref, gq_ref, gk_ref, q_ref, kv_ref, tok_ref, w_scr, *, dil, scale):
    c = pl.program_id(1)
    tm = h_ref.shape[0]
    nh = HEADS_PER_GROUP_A

    @pl.when(pl.program_id(0) == 0)
    def _():
        for i, w_ref in enumerate((wq_ref, wk_ref, wv_ref)):
            @pl.when(c == i)
            def _():
                w_scr[i] = w_ref[...].astype(BF16)

    n_sub = max(1, tm // PERM_ROWS)
    sub = tm // n_sub
    per = sub // dil

    def put_perm(dst_ref, y, t, pm):
        if dil == 1:
            dst_ref[0, t * sub:(t + 1) * sub, :] = y.astype(dst_ref.dtype)
            return
        yp = jnp.dot(pm, y.astype(BF16), preferred_element_type=F32).astype(BF16)
        for r in range(dil):
            dst_ref[r, t * per:(t + 1) * per, :] = yp[r * per:(r + 1) * per]

    def put_tok(y, c0, t):
        for i in range(nh):
            tok_ref[pl.ds(t * sub * 2 * nh + c0 + i, sub, stride=2 * nh), :] = y[:, i * HEAD_DIM_A:(i + 1) * HEAD_DIM_A]

    for ci in range(3):
        @pl.when(c == ci)
        def _(ci=ci):
            pm = None if dil == 1 else _perm_matrix(sub, dil)
            full = None if dil == 1 else jnp.dot(h_ref[...], w_scr[ci], preferred_element_type=F32)
            for t in range(n_sub):
                if dil == 1:
                    acc = jnp.dot(h_ref[t * sub:(t + 1) * sub, :], w_scr[ci], preferred_element_type=F32)
                else:
                    acc = full[t * sub:(t + 1) * sub]
                if ci == 0:
                    put_perm(q_ref, _headnorm(acc, gq_ref[...], HEAD_DIM_A, scale), t, pm)
                elif ci == 1:
                    y = _headnorm(acc, gk_ref[...], HEAD_DIM_A, 1.0)
                    put_tok(y, 0, t)
                    put_perm(kv_ref, y, t, pm)
                else:
                    put_tok(acc, nh, t)
                    put_perm(kv_ref, acc, t, pm)


def _proj_qkv(h, w, sec, g, g_qa, g_ka, n_batch, seq, tm, dil, act_dtype=BF16):
    n, k = h.shape
    ln = seq // dil
    per = tm // dil
    mt = seq // tm
    assert seq % tm == 0
    assert dil == 1 or (act_dtype == BF16 and tm % PERM_ROWS == 0 and (PERM_ROWS // dil) % 16 == 0)
    tile = lambda v: jnp.tile(v.astype(F32).reshape(1, -1), (1, GROUP_W_A // v.shape[-1]))
    wspec = lambda blk: pl.BlockSpec((k, W_COLS), lambda m, c: (0, blk))
    q, kv, tok = pl.pallas_call(
        functools.partial(_qkv_body, dil=dil, scale=HEAD_DIM_A ** -0.5), name=f"proj_qkv{g}", grid=(n // tm, 3),
        in_specs=[pl.BlockSpec((tm, k), lambda m, c: (m, 0)),
                  wspec(sec["q"] + g), wspec(sec["k"] + g), wspec(sec["v"] + g),
                  pl.BlockSpec((1, GROUP_W_A), lambda m, c: (0, 0)), pl.BlockSpec((1, GROUP_W_A), lambda m, c: (0, 0))],
        out_specs=[pl.BlockSpec((None, dil, per, GROUP_W_A), lambda m, c: (m // mt, 0, m % mt, 0)),
                   pl.BlockSpec((None, dil, per, GROUP_W_A), lambda m, c: (m // mt, 0, m % mt, jnp.maximum(c - 1, 0))),
                   pl.BlockSpec((tm * 2 * HEADS_PER_GROUP_A, LANES), lambda m, c: (m, 0))],
        out_shape=[jax.ShapeDtypeStruct((n_batch, dil, ln, GROUP_W_A), act_dtype),
                   jax.ShapeDtypeStruct((n_batch, dil, ln, 2 * GROUP_W_A), act_dtype),
                   jax.ShapeDtypeStruct((n * 2 * HEADS_PER_GROUP_A, LANES), F32)],
        scratch_shapes=[pltpu.VMEM((3, k, W_COLS), BF16)], compiler_params=_cparams(2),
    )(h, w, w, w, tile(g_qa), tile(g_ka))
    return q.reshape(n, GROUP_W_A), kv.reshape(n, 2 * GROUP_W_A), tok


def _bias_from_table(tab_ref, bidx, col):
    acc = jnp.zeros(bidx.shape, F32)
    for kb in range(N_BUCKETS):
        acc = jnp.where(bidx == kb, tab_ref[kb, col], acc)
    return acc


def _split_hi_lo(x):
    hi = x.astype(BF16)
    return hi, (x - hi.astype(F32)).astype(BF16)


ATTN_P_BLOCKS = 4


def _attn_p_body(tab_ref, bidx_ref, q_ref, kp_ref, kc_ref, vp_ref, vc_ref, o_ref, bias_scr, *, g, nb):
    i = pl.program_id(0)

    @pl.when(i == 0)
    def _():
        bidx = bidx_ref[...]
        for h in range(HEADS_PER_GROUP_A):
            bias_scr[h] = _bias_from_table(tab_ref, bidx, g * HEADS_PER_GROUP_A + h)

    blk = kp_ref.shape[0]
    k_all = jnp.concatenate([kp_ref[...], kc_ref[...]], axis=0)
    v_all = jnp.concatenate([vp_ref[...], vc_ref[...]], axis=0)
    row = lax.broadcasted_iota(I32, (blk, 2 * blk), 0)
    col = lax.broadcasted_iota(I32, (blk, 2 * blk), 1)
    dist = blk + row - col
    band = (dist >= 0) & (dist <= blk)
    lane = lax.broadcasted_iota(I32, (blk, LANES), 1)
    for u in range(ATTN_P_BLOCKS):
        has_prev = ((i * ATTN_P_BLOCKS + u) % nb) > 0
        valid = band & (has_prev | (col >= blk))
        q = q_ref[u * blk:(u + 1) * blk, :]
        k = k_all[u * blk:(u + 2) * blk]
        v = v_all[u * blk:(u + 2) * blk]
        lse_blk = jnp.zeros((blk, LANES), F32)
        outs = []
        for h in range(HEADS_PER_GROUP_A):
            hs = slice(h * HEAD_DIM_A, (h + 1) * HEAD_DIM_A)
            s = lax.dot_general(q[:, hs], k[:, hs], (((1,), (1,)), ((), ())), preferred_element_type=F32)
            s = jnp.where(valid, s + bias_scr[h], NEG_INF)
            m = jnp.max(s, axis=-1, keepdims=True)
            p = jnp.exp(s - m)
            l = jnp.sum(p, axis=-1, keepdims=True)
            o = jnp.dot(p.astype(BF16), v[:, hs], preferred_element_type=F32) / l
            outs.append(o)
            lse_blk = jnp.where(lane == h, m + jnp.log(l), lse_blk)
        lse_hi, lse_lo = _split_hi_lo(lse_blk)
        o_ref[u * blk:(u + 1) * blk, :] = jnp.concatenate(
            [jnp.concatenate(outs, axis=1).astype(BF16), lse_hi, lse_lo], axis=1)


O_EXT_W = GROUP_W_A + 2 * LANES


def _attn_prompt(q, kv, rel_bias, g, n_batch, seq):
    win, dil = DIL_GROUPS[g]
    blk = win // dil
    nb = seq // dil // blk
    n = n_batch * seq
    a = jnp.arange(blk)[:, None]
    c = jnp.arange(2 * blk)[None, :]
    bidx = _rel_bucket(jnp.maximum(blk + a - c, 0) * dil)

    nu = ATTN_P_BLOCKS
    assert (n // blk) % nu == 0 and (nb % nu == 0 or nb == 1)

    def prev(i):
        return jnp.maximum(i * nu - 1, 0)

    return pl.pallas_call(
        functools.partial(_attn_p_body, g=g, nb=nb), name=f"attn_p{g}", grid=(n // blk // nu,),
        in_specs=[
            pl.BlockSpec(memory_space=pltpu.SMEM),
            pl.BlockSpec((blk, 2 * blk), lambda i: (0, 0)),
            pl.BlockSpec((nu * blk, GROUP_W_A), lambda i: (i, 0)),
            pl.BlockSpec((blk, GROUP_W_A), lambda i: (prev(i), 0)),
            pl.BlockSpec((nu * blk, GROUP_W_A), lambda i: (i, 0)),
            pl.BlockSpec((blk, GROUP_W_A), lambda i: (prev(i), 1)),
            pl.BlockSpec((nu * blk, GROUP_W_A), lambda i: (i, 1)),
        ],
        out_specs=pl.BlockSpec((nu * blk, O_EXT_W), lambda i: (i, 0)),
        out_shape=jax.ShapeDtypeStruct((n, O_EXT_W), BF16),
        scratch_shapes=[pltpu.VMEM((HEADS_PER_GROUP_A, blk, 2 * blk), F32)],
        compiler_params=_cparams(1),
    )(rel_bias, bidx, q, kv, kv, kv, kv)


def _attn_s_body(tab_ref, bidx_ref, q_ref, kvn_ref, tokn_ref, cache_hbm, o_ref, newc_hbm,
                 raw, kv_scr, bias_scr, sem_in, sem_out, *, g, dil, lb, n_req):
    t_new = q_ref.shape[0]
    nh = HEADS_PER_GROUP_A
    n_rows = nh * t_new
    n_keys = kv_scr.shape[0]
    n_col = 2 * nh
    lb8 = lb * n_col
    n = pl.program_id(0)
    nbuf = raw.shape[0]
    ahead = nbuf // 2
    slot = n % nbuf

    def fetch(req):
        s = req % nbuf
        return pltpu.make_async_copy(cache_hbm.at[req], raw.at[s, pl.ds(0, lb8)], sem_in.at[s])

    def flush(req):
        s = req % nbuf
        return pltpu.make_async_copy(raw.at[s, pl.ds(t_new * n_col, lb8)], newc_hbm.at[req], sem_out.at[s])

    @pl.when(n == 0)
    def _():
        kv_scr[lb:, :] = jnp.zeros((n_keys - lb, kv_scr.shape[1]), BF16)
        bidx = bidx_ref[...]
        for h in range(nh):
            bias_scr[h * t_new:(h + 1) * t_new, :] = _bias_from_table(tab_ref, bidx, g * nh + h)
        for r in range(ahead):
            fetch(r).start()

    @pl.when(n >= nbuf - ahead)
    def _():
        flush(n - (nbuf - ahead)).wait()

    @pl.when(n + ahead < n_req)
    def _():
        fetch(n + ahead).start()

    fetch(n).wait()
    raw[slot, lb8:lb8 + t_new * n_col, :] = tokn_ref[...]
    flush(n).start()
    new = kvn_ref[...]

    rows_per = min(lb, 256)
    for c in range(n_col):
        for r0 in range(0, lb, rows_per):
            kv_scr[r0:r0 + rows_per, c * LANES:(c + 1) * LANES] = raw[
                slot, pl.ds(r0 * n_col + c, rows_per, stride=n_col), :].astype(BF16)
    kv_scr[lb:lb + 2 * t_new, :] = jnp.concatenate([new, jnp.zeros_like(new)], axis=0).astype(BF16)

    q = q_ref[...]
    qt = jnp.concatenate([q] * nh, axis=0)
    rr = lax.broadcasted_iota(I32, (n_rows, GROUP_W_A), 0)
    cc = lax.broadcasted_iota(I32, (n_rows, GROUP_W_A), 1)
    qbd = jnp.where(cc // HEAD_DIM_A == rr // t_new, qt, 0.0).astype(BF16)

    s = lax.dot_general(qbd, kv_scr[:, :GROUP_W_A], (((1,), (1,)), ((), ())), preferred_element_type=F32)
    row = lax.broadcasted_iota(I32, (n_rows, n_keys), 0)
    col = lax.broadcasted_iota(I32, (n_rows, n_keys), 1)
    delta = lb + (row & (t_new - 1)) - col
    valid = (delta >= 0) & (delta <= lb) & ((delta & (dil - 1)) == 0)
    s = jnp.where(valid, s + bias_scr[...], NEG_INF)
    m = jnp.max(s, axis=-1, keepdims=True)
    p = jnp.exp(s - m)
    l = jnp.sum(p, axis=-1, keepdims=True)
    o = jnp.dot(p.astype(BF16), kv_scr[:, GROUP_W_A:], preferred_element_type=F32) / l
    lse = m + jnp.log(l)
    lane = lax.broadcasted_iota(I32, (t_new, LANES), 1)
    lse_blk = jnp.zeros((t_new, LANES), F32)
    outs = []
    for h in range(nh):
        outs.append(o[h * t_new:(h + 1) * t_new, h * HEAD_DIM_A:(h + 1) * HEAD_DIM_A])
        lse_blk = jnp.where(lane == h, lse[h * t_new:(h + 1) * t_new, :], lse_blk)
    o_ref[...] = jnp.concatenate(outs + [lse_blk, jnp.zeros_like(lse_blk)], axis=1)

    @pl.when(n == n_req - 1)
    def _():
        for back in range(nbuf - ahead - 1, -1, -1):
            flush(n - back).wait()


ATTN_S_BUFS = 4


def _attn_sample(q, kv_new, tok_new, cache, rel_bias, g, t_new):
    win, dil = DIL_GROUPS[g]
    n_col = 2 * HEADS_PER_GROUP_A
    n_req, lb = cache.shape[0], cache.shape[1] // n_col
    assert t_new & (t_new - 1) == 0 and dil & (dil - 1) == 0 and lb == win
    n_keys = lb + LANES
    t = jnp.arange(t_new)[:, None]
    c = jnp.arange(n_keys)[None, :]
    bidx = _rel_bucket(jnp.clip(lb + t - c, 0, lb))
    return pl.pallas_call(
        functools.partial(_attn_s_body, g=g, dil=dil, lb=lb, n_req=n_req), name=f"attn_s{g}", grid=(n_req,),
        in_specs=[
            pl.BlockSpec(memory_space=pltpu.SMEM),
            pl.BlockSpec((t_new, n_keys), lambda n: (0, 0)),
            pl.BlockSpec((t_new, GROUP_W_A), lambda n: (n, 0)),
            pl.BlockSpec((t_new, 2 * GROUP_W_A), lambda n: (n, 0)),
            pl.BlockSpec((t_new * n_col, LANES), lambda n: (n, 0)),
            pl.BlockSpec(memory_space=pl.ANY),
        ],
        out_specs=[pl.BlockSpec((t_new, O_EXT_W), lambda n: (n, 0)), pl.BlockSpec(memory_space=pl.ANY)],
        out_shape=[jax.ShapeDtypeStruct((n_req * t_new, O_EXT_W), F32), jax.ShapeDtypeStruct(cache.shape, F32)],
        scratch_shapes=[pltpu.VMEM((ATTN_S_BUFS, (lb + t_new) * n_col, LANES), F32),
                        pltpu.VMEM((n_keys, 2 * GROUP_W_A), BF16),
                        pltpu.VMEM((HEADS_PER_GROUP_A * t_new, n_keys), F32),
                        pltpu.SemaphoreType.DMA((ATTN_S_BUFS,)), pltpu.SemaphoreType.DMA((ATTN_S_BUFS,))],
        compiler_params=_cparams(1),
    )(rel_bias, bidx, q, kv_new, tok_new, cache)


def _gmlp_body(u_ref, v_ref, w_ref, b_ref, o_ref, *, period):
    c = w_ref.shape[1]
    nch = u_ref.shape[0] // c
    gd = u_ref.shape[1] // N_GROUPS_B
    i = lax.broadcasted_iota(I32, (c, c), 0)
    j = lax.broadcasted_iota(I32, (c, c), 1)
    mask = (j <= i) & ((i // period) == (j // period))
    b = b_ref[...]
    for g in range(N_GROUPS_B):
        gs = slice(g * gd, (g + 1) * gd)
        wg = jnp.where(mask, w_ref[g], 0.0).astype(BF16)
        vg = [v_ref[ch * c:(ch + 1) * c, gs].astype(BF16) for ch in range(nch)]
        vg = vg[0] if nch == 1 else jnp.concatenate(vg, axis=1)
        sg = jnp.dot(wg, vg, preferred_element_type=F32) + b[:, g:g + 1]
        for ch in range(nch):
            u = u_ref[ch * c:(ch + 1) * c, gs].astype(F32)
            o_ref[ch * c:(ch + 1) * c, gs] = (u * sg[:, ch * gd:(ch + 1) * gd]).astype(o_ref.dtype)


def _gmlp(u, v, w, b, tm, period):
    m, wb = u.shape
    c = w.shape[1]
    return pl.pallas_call(
        functools.partial(_gmlp_body, period=period), name="gmlp", grid=(m // tm,),
        in_specs=[pl.BlockSpec((tm, wb), lambda i: (i, 0)), pl.BlockSpec((tm, wb), lambda i: (i, 0)),
                  pl.BlockSpec((N_GROUPS_B, c, c), lambda i: (0, 0, 0)),
                  pl.BlockSpec((c, N_GROUPS_B), lambda i: (0, 0))],
        out_specs=pl.BlockSpec((tm, wb), lambda i: (i, 0)),
        out_shape=jax.ShapeDtypeStruct((m, wb), BF16), compiler_params=_cparams(1),
    )(u, v, w, b)


def _gmlp_new_body(u_ref, v_ref, w_ref, b_ref, o_ref):
    t = w_ref.shape[0]
    n = u_ref.shape[0] // t
    width = u_ref.shape[1]
    v = v_ref[...].astype(F32).reshape(n, t, width)
    sg = jnp.broadcast_to(b_ref[...][None], (n, t, width))
    p_row = lax.broadcasted_iota(I32, (t, width), 0)
    for q in range(t):
        sg = sg + jnp.where(p_row >= q, w_ref[q], 0.0)[None] * v[:, q:q + 1, :]
    o_ref[...] = (u_ref[...].astype(F32).reshape(n, t, width) * sg).reshape(n * t, width).astype(o_ref.dtype)


def _gmlp_new(u, v, w_s, b_s, t):
    m, wb = u.shape
    gd = wb // N_GROUPS_B
    wq = jnp.repeat(jnp.transpose(w_s[:, :t, :t], (2, 1, 0)), gd, axis=-1)
    bq = jnp.repeat(b_s[:, :t].T, gd, axis=-1)
    whole = lambda a: pl.BlockSpec(a.shape, lambda i: (0,) * a.ndim)
    return pl.pallas_call(
        _gmlp_new_body, name="gmlp_new", grid=(1,),
        in_specs=[whole(u), whole(v), whole(wq), whole(bq)], out_specs=whole(u),
        out_shape=jax.ShapeDtypeStruct((m, wb), BF16), compiler_params=_cparams(1),
    )(u, v, wq, bq)


def _memattn_body(q_ref, kv_ref, o_ref, *, tiles):
    wm = N_HEADS_M * HEAD_DIM_M
    n_ct = HEAD_DIM_M // LANES
    per_key = 2 * n_ct * N_HEADS_M
    reqs = kv_ref.shape[0]
    tq = q_ref.shape[0] // reqs

    for r in range(reqs):
        kvr = kv_ref.at[r]

        def head_slab(kv, h):
            if not tiles:
                return kvr[:, kv * wm + h * HEAD_DIM_M:kv * wm + (h + 1) * HEAD_DIM_M].astype(BF16)
            n_keys = kvr.shape[0] // per_key
            parts = [kvr[pl.ds((kv * n_ct + ct) * N_HEADS_M + h, n_keys, stride=per_key), :] for ct in range(n_ct)]
            return jnp.concatenate(parts, axis=1).astype(BF16)

        q = q_ref[r * tq:(r + 1) * tq, :].astype(BF16)
        outs = []
        for h in range(N_HEADS_M):
            hs = slice(h * HEAD_DIM_M, (h + 1) * HEAD_DIM_M)
            k = head_slab(0, h)
            v = head_slab(1, h)
            s = lax.dot_general(q[:, hs], k, (((1,), (1,)), ((), ())), preferred_element_type=F32)
            m = jnp.max(s, axis=-1, keepdims=True)
            p = jnp.exp(s - m)
            l = jnp.sum(p, axis=-1, keepdims=True)
            outs.append(jnp.dot(p.astype(BF16), v, preferred_element_type=F32) / l)
        o_ref[r * tq:(r + 1) * tq, :] = jnp.concatenate(outs, axis=1).astype(o_ref.dtype)


def _memattn(q, kv, tq, out_dtype, tiles=False, reqs=1):
    m, wm = q.shape
    n = kv.shape[0]
    per = m // n // tq
    assert reqs == 1 or (per == 1 and n % reqs == 0)
    kv_spec = pl.BlockSpec((reqs,) + kv.shape[1:], lambda i: (i // per,) + (0,) * (kv.ndim - 1))
    tq = tq * reqs
    return pl.pallas_call(
        functools.partial(_memattn_body, tiles=tiles), name="memattn", grid=(m // tq,),
        in_specs=[pl.BlockSpec((tq, wm), lambda i: (i, 0)), kv_spec],
        out_specs=pl.BlockSpec((tq, wm), lambda i: (i, 0)),
        out_shape=jax.ShapeDtypeStruct((m, wm), out_dtype), compiler_params=_cparams(1),
    )(q, kv)


def _mix_body(gt_ref, o0_ref, o1_ref, o2_ref, ob_ref, om_ref, wpa_ref, wpb_ref, wpm_ref, z_ref, *, dils):
    d = z_ref.shape[1]
    tm = z_ref.shape[0]
    os_, ls = [], []
    for o_ref, dil in zip((o0_ref, o1_ref, o2_ref), dils):
        if dil == 1:
            x = o_ref[...].astype(F32)
        else:
            rows = jnp.concatenate([o_ref[r] for r in range(dil)], axis=0)
            x = jnp.dot(_perm_matrix(tm, dil, inverse=True), rows, preferred_element_type=F32)
        os_.append(x[:, :GROUP_W_A])
        ls.append(x[:, GROUP_W_A:GROUP_W_A + LANES] + x[:, GROUP_W_A + LANES:])
    l0, l1, l2 = ls
    mx = jnp.maximum(jnp.maximum(l0, l1), l2)
    e0, e1, e2 = jnp.exp(l0 - mx), jnp.exp(l1 - mx), jnp.exp(l2 - mx)
    den = e0 + e1 + e2
    w0, w1, w2 = e0 / den, e1 / den, e2 / den
    cols = []
    for h in range(HEADS_PER_GROUP_A):
        hs = slice(h * HEAD_DIM_A, (h + 1) * HEAD_DIM_A)
        cols.append(w0[:, h:h + 1] * os_[0][:, hs] + w1[:, h:h + 1] * os_[1][:, hs] + w2[:, h:h + 1] * os_[2][:, hs])
    oa = jnp.concatenate(cols, axis=1).astype(BF16)
    pa = jnp.dot(oa, wpa_ref[...], preferred_element_type=F32)
    pb = jnp.dot(ob_ref[...].astype(BF16), wpb_ref[...], preferred_element_type=F32)
    pm = jnp.dot(om_ref[...].astype(BF16), wpm_ref[...], preferred_element_type=F32)
    z = (gt_ref[:, 0:d].astype(F32) * pa + gt_ref[:, d:2 * d].astype(F32) * pb
         + gt_ref[:, 2 * d:3 * d].astype(F32) * pm)
    z_ref[...] = z.astype(z_ref.dtype)


def _mix(gates, o_list, dils, seq, ob, om, wpa, wpb, wpm, tm):
    m = gates.shape[0]
    d = wpa.shape[1]
    mt = seq // tm if any(dl > 1 for dl in dils) else 1

    def rows(width):
        return pl.BlockSpec((tm, width), lambda i: (i, 0))

    def whole(a):
        return pl.BlockSpec(a.shape, lambda i: (0, 0))

    o_specs, o_args = [], []
    for o, dil in zip(o_list, dils):
        if dil == 1:
            o_specs.append(rows(O_EXT_W))
            o_args.append(o)
        else:
            assert seq % tm == 0 and (tm // dil) % 16 == 0
            o_specs.append(pl.BlockSpec((None, dil, tm // dil, O_EXT_W), lambda i: (i // mt, 0, i % mt, 0)))
            o_args.append(o.reshape(m // seq, dil, seq // dil, O_EXT_W))
    return pl.pallas_call(
        functools.partial(_mix_body, dils=tuple(dils)), name="mix", grid=(m // tm,),
        in_specs=[rows(gates.shape[1])] + o_specs
                 + [rows(ob.shape[1]), rows(om.shape[1]), whole(wpa), whole(wpb), whole(wpm)],
        out_specs=rows(d),
        out_shape=jax.ShapeDtypeStruct((m, d), BF16), compiler_params=_cparams(1),
    )(gates, *o_args, ob, om, wpa, wpb, wpm)


def _route(logits):
    lane = lax.broadcasted_iota(I32, logits.shape, 1)
    lane_f = lane.astype(F32)
    is_g = lane < N_EXPERT_GROUPS
    gmax = jnp.max(jnp.where(is_g, logits, -jnp.inf), axis=1, keepdims=True)
    gsel = jnp.min(jnp.where(is_g & (logits == gmax), lane_f, float(LANES)), axis=1, keepdims=True).astype(I32)
    gden = jnp.sum(jnp.where(is_g, jnp.exp(logits - gmax), 0.0), axis=1, keepdims=True)
    pg = 1.0 / gden
    e_lane = lane - N_EXPERT_GROUPS
    in_grp = (e_lane >= 0) & (e_lane < N_EXPERTS) & ((e_lane // EXPERTS_PER_GROUP) == gsel)
    m1 = jnp.max(jnp.where(in_grp, logits, -jnp.inf), axis=1, keepdims=True)
    i1 = jnp.min(jnp.where(in_grp & (logits == m1), lane_f, float(LANES)), axis=1, keepdims=True).astype(I32)
    rest = in_grp & (lane != i1)
    m2 = jnp.max(jnp.where(rest, logits, -jnp.inf), axis=1, keepdims=True)
    i2 = jnp.min(jnp.where(rest & (logits == m2), lane_f, float(LANES)), axis=1, keepdims=True).astype(I32)
    e2 = jnp.exp(m2 - m1)
    w1 = pg / (1.0 + e2)
    w2 = pg * e2 / (1.0 + e2)
    eid = jnp.where(lane == 0, i1 - N_EXPERT_GROUPS, jnp.where(lane == 1, i2 - N_EXPERT_GROUPS, 0))
    wts = jnp.where(lane == 0, w1, jnp.where(lane == 1, w2, 0.0))
    return eid, wts


def _resid_body(x_ref, z_ref, wo_ref, gf_ref, wr_ref, br_ref, xmid_ref, hpk_ref, eid_ref, wts_ref):
    xm = x_ref[...] + jnp.dot(z_ref[...], wo_ref[...], preferred_element_type=F32)
    xmid_ref[...] = xm
    hf = _rms(xm, gf_ref[...])
    wr = wr_ref[...]
    wr_hi = wr.astype(BF16)
    wr_lo = (wr - wr_hi.astype(F32)).astype(BF16)
    hf_hi = hf.astype(BF16)
    hf_lo = (hf - hf_hi.astype(F32)).astype(BF16)
    logits = (jnp.dot(hf_hi, wr_hi, preferred_element_type=F32) + jnp.dot(hf_hi, wr_lo, preferred_element_type=F32)
              + jnp.dot(hf_lo, wr_hi, preferred_element_type=F32) + br_ref[...])
    eid, wts = _route(logits)
    eid_ref[...] = eid
    wts_ref[...] = wts
    _store_token_tiles(hpk_ref, hf)


def _resid(x2d, z, wo, g_ffn, w_r, b_r, tm):
    m, d = x2d.shape

    def rows(width):
        return pl.BlockSpec((tm, width), lambda i: (i, 0))

    def whole(a):
        return pl.BlockSpec(a.shape, lambda i: (0, 0))

    gf = g_ffn.reshape(1, d)
    nt = d // 2 // LANES
    return pl.pallas_call(
        _resid_body, name="resid", grid=(m // tm,),
        in_specs=[rows(d), rows(d), whole(wo), whole(gf), whole(w_r), whole(b_r)],
        out_specs=[rows(d), pl.BlockSpec((tm * nt, LANES), lambda i: (i, 0)), rows(LANES), rows(LANES)],
        out_shape=[jax.ShapeDtypeStruct((m, d), F32), jax.ShapeDtypeStruct((m * nt, LANES), U32),
                   jax.ShapeDtypeStruct((m, LANES), I32), jax.ShapeDtypeStruct((m, LANES), F32)],
        compiler_params=_cparams(1),
    )(x2d, z, wo, gf, w_r, b_r)


TOKEN_TILE = 8
GATHER_UNROLL = 8


def _gather_rows(idx_at, src_hbm, dst, sem, n):
    def body(i, carry):
        src = pl.multiple_of(idx_at(i), TOKEN_TILE)
        dst_row = pl.multiple_of(i * TOKEN_TILE, TOKEN_TILE)
        pltpu.make_async_copy(src_hbm.at[pl.ds(src, TOKEN_TILE)], dst.at[pl.ds(dst_row, TOKEN_TILE)], sem).start()
        return carry
    lax.fori_loop(0, n, body, 0, unroll=GATHER_UNROLL)


FFN_CHUNKS = 8
FFN_BUFS = 3


def _ffn_body(blk_e_ref, nused_ref, eord_ref, enext_ref, idx_hbm, h_hbm, wg_hbm, wu_hbm, wd_hbm, y_ref,
              idx_s, isem, buf, sem, x_scr, a_scr, b_scr, hm_scr, wg_f, wu_f, wd_f, wsem, wg_s, wu_s, wd_s):
    j = pl.program_id(0)
    nused = nused_ref[0]
    last = nused - 1
    rows = buf.shape[1] // TOKEN_TILE
    e = blk_e_ref[j]

    def idx_copy(b):
        s = b % FFN_BUFS
        return pltpu.make_async_copy(idx_hbm.at[jnp.minimum(b, last)], idx_s.at[s], isem.at[s])

    def wait_rows(s):
        pltpu.make_async_copy(h_hbm.at[pl.ds(0, rows * TOKEN_TILE)], buf.at[s], sem.at[s]).wait()
    new_expert = (j == 0) | (e != blk_e_ref[jnp.maximum(j - 1, 0)])
    ws = eord_ref[j] % 2

    def fetch_w(expert, s):
        return [pltpu.make_async_copy(src.at[expert], dst.at[s], wsem.at[s, i])
                for i, (src, dst) in enumerate(((wg_hbm, wg_f), (wu_hbm, wu_f), (wd_hbm, wd_f)))]

    @pl.when(j == 0)
    def _():
        for c in fetch_w(e, ws):
            c.start()
        for b in range(FFN_BUFS):
            idx_copy(b).start()
        for b in range(FFN_BUFS - 1):
            idx_copy(b).wait()
            _gather_rows(lambda i, b=b: idx_s[b, 0, i], h_hbm, buf.at[b], sem.at[b], rows)

    @pl.when(new_expert)
    def _():
        nxt = enext_ref[j]

        @pl.when(nxt >= 0)
        def _():
            for c in fetch_w(nxt, 1 - ws):
                c.start()

        for c in fetch_w(e, ws):
            c.wait()
        wg_s[...] = wg_f[ws].astype(BF16)
        wu_s[...] = wu_f[ws].astype(BF16)
        wd_s[...] = wd_f[ws].astype(BF16)

    @pl.when(j < nused)
    def _():
        slot = j % FFN_BUFS
        s2 = (j + 2) % FFN_BUFS
        per = rows // FFN_CHUNKS
        idx_copy(j + 2).wait()

        @pl.when(j < last)
        def _():
            idx_copy(j + 3).start()

        def issue(c):
            for i in range(c * per, (c + 1) * per):
                src = pl.multiple_of(idx_s[s2, 0, i], TOKEN_TILE)
                pltpu.make_async_copy(h_hbm.at[pl.ds(src, TOKEN_TILE)],
                                      buf.at[s2, pl.ds(i * TOKEN_TILE, TOKEN_TILE)], sem.at[s2]).start()

        wait_rows(slot)
        x_scr[...] = _load_token_tiles(buf, slot, 0, rows, TOKEN_TILE).astype(BF16)
        dn = a_scr.shape[1] // 2
        for c in range(2):
            issue(c)
            a_scr[:, c * dn:(c + 1) * dn] = jnp.dot(x_scr[...], wg_s[:, c * dn:(c + 1) * dn],
                                                     preferred_element_type=F32)
        for c in range(2):
            issue(2 + c)
            b_scr[:, c * dn:(c + 1) * dn] = jnp.dot(x_scr[...], wu_s[:, c * dn:(c + 1) * dn],
                                                     preferred_element_type=F32)
        a = a_scr[...]
        hm_scr[...] = (a * jax.nn.sigmoid(a) * b_scr[...]).astype(BF16)
        half = wd_s.shape[1] // 2
        n_dc = FFN_CHUNKS - 4
        dq = half // n_dc
        for c in range(n_dc):
            issue(4 + c)
            lo = jnp.dot(hm_scr[...], wd_s[:, c * dq:(c + 1) * dq], preferred_element_type=F32)
            hi = jnp.dot(hm_scr[...], wd_s[:, half + c * dq:half + (c + 1) * dq], preferred_element_type=F32)
            w = _pack_bf16_pair(jnp.concatenate([lo, hi], axis=1))
            for t in range(dq // LANES):
                y_ref[pl.ds(c * (dq // LANES) + t, rows, stride=TOKEN_TILE), :] = w[:, t * LANES:(t + 1) * LANES]

        @pl.when(j == last)
        def _():
            wait_rows((j + 1) % FFN_BUFS)
            wait_rows(s2)

    @pl.when(j >= nused)
    def _():
        y_ref[...] = jnp.zeros(y_ref.shape, y_ref.dtype)


def _ffn(hpk, src_tok, blk_e, nused, eord, enext, wg, wu, wd):
    n_blocks = blk_e.shape[0]
    rows = MOE_ROWS
    tiles = rows * TOKEN_TILE
    d, de = wg.shape[1], wg.shape[2]
    assert d // 2 == TOKEN_TILE * LANES and rows % FFN_CHUNKS == 0 and (d // 2) % ((FFN_CHUNKS - 4) * LANES) == 0
    assert n_blocks >= N_EXPERTS + FFN_BUFS
    idx = (src_tok * TOKEN_TILE).reshape(n_blocks, 1, rows)
    hbm = pl.BlockSpec(memory_space=pl.ANY)
    grid_spec = pltpu.PrefetchScalarGridSpec(
        num_scalar_prefetch=4, grid=(n_blocks,),
        in_specs=[hbm, hbm, hbm, hbm, hbm],
        out_specs=pl.BlockSpec((tiles, LANES), lambda j, *_: (j, 0)),
        scratch_shapes=[pltpu.SMEM((FFN_BUFS, 1, rows), I32), pltpu.SemaphoreType.DMA((FFN_BUFS,)),
                        pltpu.VMEM((FFN_BUFS, tiles, LANES), U32), pltpu.SemaphoreType.DMA((FFN_BUFS,)),
                        pltpu.VMEM((rows, d), BF16), pltpu.VMEM((rows, de), F32), pltpu.VMEM((rows, de), F32),
                        pltpu.VMEM((rows, de), BF16),
                        pltpu.VMEM((2, d, de), wg.dtype), pltpu.VMEM((2, d, de), wu.dtype),
                        pltpu.VMEM((2, de, d), wd.dtype), pltpu.SemaphoreType.DMA((2, 3)),
                        pltpu.VMEM((d, de), BF16), pltpu.VMEM((d, de), BF16), pltpu.VMEM((de, d), BF16)],
    )
    return pl.pallas_call(
        _ffn_body, name="ffn", grid_spec=grid_spec,
        out_shape=jax.ShapeDtypeStruct((n_blocks * tiles, LANES), U32), compiler_params=_cparams(1),
    )(blk_e, nused, eord, enext, idx, hpk, wg, wu, wd)


def _combine_body(idx_cur_ref, idx_nxt_ref, x_ref, w_ref, y_hbm, o_ref, buf, sem, *, n):
    i = pl.program_id(0)
    slot = i % 2
    rows = buf.shape[1] // TOKEN_TILE
    tm = rows // TOP_K

    @pl.when(i == 0)
    def _():
        _gather_rows(lambda r: idx_cur_ref[0, 0, r], y_hbm, buf.at[0], sem.at[0], rows)

    if n > 1:
        @pl.when(i + 1 < n)
        def _():
            _gather_rows(lambda r: idx_nxt_ref[0, 0, r], y_hbm, buf.at[1 - slot], sem.at[1 - slot], rows)

    pltpu.make_async_copy(y_hbm.at[pl.ds(0, rows * TOKEN_TILE)], buf.at[slot], sem.at[slot]).wait()
    w = w_ref[...]
    y0 = _load_token_tiles(buf, slot, 0, tm, TOKEN_TILE)
    y1 = _load_token_tiles(buf, slot, tm, tm, TOKEN_TILE)
    o_ref[...] = x_ref[...] + (w[:, 0:1] * y0 + w[:, 1:2] * y1)


def _combine(xmid, wts, dest, ypk, tm):
    m, d = xmid.shape
    nt = m // tm
    assert d // 2 == TOKEN_TILE * LANES
    idx = (dest * TOKEN_TILE).reshape(nt, tm, TOP_K).transpose(0, 2, 1).reshape(nt, 1, TOP_K * tm)

    def idx_spec(ahead):
        return pl.BlockSpec((1, 1, TOP_K * tm), lambda i: (jnp.minimum(i + ahead, nt - 1), 0, 0),
                            memory_space=pltpu.SMEM)

    return pl.pallas_call(
        functools.partial(_combine_body, n=nt), name="combine", grid=(nt,),
        in_specs=[idx_spec(0), idx_spec(1), pl.BlockSpec((tm, d), lambda i: (i, 0)),
                  pl.BlockSpec((tm, LANES), lambda i: (i, 0)), pl.BlockSpec(memory_space=pl.ANY)],
        out_specs=pl.BlockSpec((tm, d), lambda i: (i, 0)),
        out_shape=jax.ShapeDtypeStruct((m, d), F32),
        scratch_shapes=[pltpu.VMEM((2, TOP_K * tm * TOKEN_TILE, LANES), U32), pltpu.SemaphoreType.DMA((2,))],
        compiler_params=_cparams(1),
    )(idx, idx, xmid, wts, ypk)


def _moe_plan(eid2):
    n_slot = eid2.shape[0] * TOP_K
    eid = eid2.reshape(n_slot)
    onehot = (eid[:, None] == jnp.arange(N_EXPERTS, dtype=I32)[None, :]).astype(I32)
    csum = jnp.cumsum(onehot, axis=0)
    rank = jnp.take_along_axis(csum, eid[:, None], axis=1)[:, 0] - 1
    counts = csum[-1]
    padded = (counts + MOE_ROWS - 1) // MOE_ROWS * MOE_ROWS
    pend = jnp.cumsum(padded)
    pstart = pend - padded
    dest = pstart[eid] + rank
    n_blocks = -(-n_slot // MOE_ROWS) + N_EXPERTS
    nused = (pend[-1] // MOE_ROWS).astype(I32)
    blk = jnp.minimum(jnp.arange(n_blocks, dtype=I32), nused - 1)
    blk_e = jnp.minimum(jnp.searchsorted(pend, blk * MOE_ROWS, side="right"), N_EXPERTS - 1).astype(I32)
    n_rows = n_blocks * MOE_ROWS
    src_tok = (jnp.arange(n_rows, dtype=I32) % (n_slot // TOP_K)).at[dest].set(
        jnp.arange(n_slot, dtype=I32) // TOP_K, unique_indices=True, mode="promise_in_bounds")
    has_rows = counts > 0
    ord_e = jnp.cumsum(has_rows.astype(I32)) - 1
    ids = jnp.arange(N_EXPERTS, dtype=I32)
    later = has_rows[None, :] & (ids[None, :] > ids[:, None])
    next_e = jnp.where(later.any(axis=1), jnp.argmax(later, axis=1), -1).astype(I32)
    return dest.astype(I32), src_tok, blk_e, nused.reshape(1), ord_e[blk_e].astype(I32), next_e[blk_e]


def _mixer(x2d, tm, p, attn_fn, gmlp_fn, mem_fn):
    h = _prep(x2d, p["g_mix"], tm)
    qdt = p["q_dtype"]
    wi, sec = p["w_in"], p["sec"]
    o_list, dils, seq = attn_fn(h)
    ub = _proj(h, wi, (sec["u"], 1, 1, sec["vb"] - sec["u"]), tm, "none", BF16, name="proj_ub")
    n_vb = sec["qm"] - sec["vb"]
    vb = _proj(h, wi, (sec["vb"], n_vb, n_vb, 1), tm, "norm", p["vb_dtype"], p["g_vb"], n_vb * W_COLS,
               name="proj_vb")
    qm = _proj(h, wi, (sec["qm"], 1, 1, sec["gt"] - sec["qm"]), tm, "norm", qdt, p["g_qm"], HEAD_DIM_M,
               HEAD_DIM_M ** -0.5, name="proj_qm")
    gates = _proj(h, wi, (sec["gt"], 2, 2, (sec["end"] - sec["gt"]) // 2), tm, "sigmoid", BF16, name="proj_gates")
    ob = gmlp_fn(ub, vb)
    om = mem_fn(qm)
    z = _mix(gates, o_list, dils, seq, ob, om, p["w_pa"], p["w_pb"], p["w_pm"], min(tm, 256))
    xmid, hpk, eid, wts = _resid(x2d, z, p["w_o"], p["g_ffn"], p["w_r"], p["b_r"], min(tm, 256))
    return vb, xmid, hpk, eid, wts


def kernel(x_prompt, x_sample, mem_prompt, cache_a0_kv, cache_a1_kv, cache_a2_kv, cache_mem_kv, rel_bias, g_mix, w_in, g_qa, g_ka, w_pa, g_vb, w_s, b_s, w_pb, g_mem, w_mk, w_mv, g_qm, g_km, w_pm, w_o, g_ffn, w_rg, b_rg, w_re, b_re, w_gate, w_up, w_down):
    n_b, seq, d = x_prompt.shape
    n_s, t_s, _ = x_sample.shape
    depth = w_in.shape[0]
    assert depth == 1
    l = 0
    caches = (cache_a0_kv, cache_a1_kv, cache_a2_kv)
    n_g = len(DIL_GROUPS)
    wb = w_pb.shape[1]
    wm = N_HEADS_M * HEAD_DIM_M

    offs = [0, WIDTH_A, 2 * WIDTH_A, 3 * WIDTH_A, 3 * WIDTH_A + wb, 3 * WIDTH_A + 2 * wb,
            3 * WIDTH_A + 2 * wb + wm, w_in.shape[2]]
    assert all(o % W_COLS == 0 for o in offs) and (offs[7] - offs[6]) % (2 * W_COLS) == 0
    sec = dict(zip(("q", "k", "v", "u", "vb", "qm", "gt", "end"), (o // W_COLS for o in offs)))
    p = {
        "g_mix": g_mix[l], "g_qa": g_qa[l], "g_ka": g_ka[l], "g_vb": g_vb[l], "g_qm": g_qm[l], "g_ffn": g_ffn[l],
        "w_in": w_in[l], "sec": sec,
        "w_pa": w_pa[l].astype(BF16), "w_pb": w_pb[l].astype(BF16), "w_pm": w_pm[l].astype(BF16),
        "w_o": w_o[l].astype(BF16),
    }
    n_r = N_EXPERT_GROUPS + N_EXPERTS
    p["w_r"] = jnp.concatenate([w_rg[l], w_re[l], jnp.zeros((d, LANES - n_r), F32)], axis=1)
    p["b_r"] = jnp.concatenate([b_rg[l], b_re[l], jnp.zeros((LANES - n_r,), F32)]).reshape(1, LANES)

    n_mem = mem_prompt.shape[1]
    h_mem = _prep(mem_prompt.reshape(n_b * n_mem, d), g_mem[l], 256)
    w_mkv = jnp.concatenate([w_mk[l], w_mv[l]], axis=1)
    n_mw = wm // W_COLS
    mkv = _proj(h_mem, w_mkv, (0, n_mw, n_mw, 2), 256, "norm_first", F32, g_km[l], HEAD_DIM_M, name="proj_mkv")
    new_mem_p = mkv.reshape(1, n_b, n_mem, 2, N_HEADS_M, HEAD_DIM_M)

    ps = dict(p, q_dtype=F32, vb_dtype=F32)
    m_s = n_s * t_s
    n_col = 2 * HEADS_PER_GROUP_A
    new_s = []

    def attn_sample(h):
        o_list = []
        for g in range(n_g):
            q, kv, tok = _proj_qkv(h, w_in[l], sec, g, g_qa[l], g_ka[l], 1, m_s, m_s, 1, act_dtype=F32)
            cache3 = caches[g][l].reshape(n_s, caches[g].shape[2] * n_col, LANES)
            o, newc = _attn_sample(q, kv, tok, cache3, rel_bias, g, t_s)
            o_list.append(o)
            new_s.append(newc.reshape(caches[g][l:l + 1].shape))
        return o_list, [1] * n_g, t_s

    n_ct = HEAD_DIM_M // LANES
    mem_tiles = (cache_mem_kv[l].reshape(n_s, n_mem, 2, N_HEADS_M, n_ct, LANES).transpose(0, 1, 2, 4, 3, 5)
                 .reshape(n_s, n_mem * 2 * n_ct * N_HEADS_M, LANES))
    vb_s, xmid_s, hpk_s, eid_s, wts_s = _mixer(
        x_sample.reshape(m_s, d), m_s, ps, attn_sample,
        lambda u, v: _gmlp_new(u, v, w_s[l], b_s[l], t_s),
        lambda qm: _memattn(qm, mem_tiles, t_s, F32, tiles=True, reqs=4))
    new_vb_s = vb_s.reshape(1, n_s, t_s, wb)

    pp = dict(p, q_dtype=BF16, vb_dtype=BF16)
    new_p = []

    def attn_prompt(h):
        o_list = []
        for g, (win, dil) in enumerate(DIL_GROUPS):
            q, kv, tok = _proj_qkv(h, w_in[l], sec, g, g_qa[l], g_ka[l], n_b, seq, 512, dil)
            o_list.append(_attn_prompt(q, kv, rel_bias, g, n_b, seq))
            keep = min(win, seq)
            kv5 = tok.reshape(n_b, seq, 2, HEADS_PER_GROUP_A, HEAD_DIM_A)
            new_p.append((kv5 if keep == seq else kv5[:, seq - keep:])[None])
        return o_list, [dl for _, dl in DIL_GROUPS], seq

    _, xmid_p, hpk_p, eid_p, wts_p = _mixer(
        x_prompt.reshape(n_b * seq, d), 512, pp, attn_prompt,
        lambda u, v: _gmlp(u, v, w_s[l], b_s[l].T, 512, CHUNK),
        lambda qm: _memattn(qm, mkv.reshape(n_b, n_mem, 2 * wm), 512, BF16))

    n_p = n_b * seq
    hpk = jnp.concatenate([hpk_p, hpk_s], axis=0)
    eid2 = jnp.concatenate([eid_p[:, :TOP_K], eid_s[:, :TOP_K]], axis=0)
    dest, src_tok, blk_e, nused, eord, enext = _moe_plan(eid2)
    ypk = _ffn(hpk, src_tok, blk_e, nused, eord, enext, w_gate[l], w_up[l], w_down[l])
    dest2 = dest.reshape(-1, TOP_K)
    y_p = _combine(xmid_p, wts_p, dest2[:n_p], ypk, 256)
    y_s = _combine(xmid_s, wts_s, dest2[n_p:], ypk, m_s)

    return (y_p.reshape(n_b, seq, d), y_s.reshape(n_s, t_s, d), new_p[0], new_p[1], new_p[2], new_mem_p,
            new_s[0], new_s[1], new_s[2], new_vb_s)
```

```python
import functools
import math

import jax
import jax.numpy as jnp
from jax import lax
from jax.experimental import pallas as pl
from jax.experimental.pallas import tpu as pltpu

F32 = jnp.float32
BF16 = jnp.bfloat16
I32 = jnp.int32
U32 = jnp.uint32

EPS = 1e-6
NEG_INF = -1e30

HEAD_DIM_A = 128
HEADS_PER_GROUP_A = 4
DIL_GROUPS = ((128, 1), (512, 4), (2048, 16))
GROUP_W_A = HEADS_PER_GROUP_A * HEAD_DIM_A
WIDTH_A = len(DIL_GROUPS) * GROUP_W_A
CHUNK = 128
N_GROUPS_B = 8
N_HEADS_M = 4
HEAD_DIM_M = 256
N_BUCKETS = 32
MAX_EXACT = N_BUCKETS // 2
MAX_DISTANCE = 2048
N_EXPERT_GROUPS = 4
EXPERTS_PER_GROUP = 8
N_EXPERTS = N_EXPERT_GROUPS * EXPERTS_PER_GROUP
TOP_K = 2
LANES = 128
MOE_ROWS = 256
VMEM_LIMIT = 56 * 1024 * 1024


def _cparams(n_grid, vmem=VMEM_LIMIT):
    return pltpu.CompilerParams(dimension_semantics=("arbitrary",) * n_grid, vmem_limit_bytes=vmem)


def _rms(x, g):
    return x * lax.rsqrt(jnp.mean(x * x, axis=-1, keepdims=True) + EPS) * g


def _pack_bf16_pair(x):
    n = x.shape[1] // 2
    lo = lax.bitcast_convert_type(x[:, :n].astype(BF16).astype(F32), U32)
    hi = lax.bitcast_convert_type(x[:, n:].astype(BF16).astype(F32), U32)
    return (hi & jnp.uint32(0xFFFF0000)) | (lo >> 16)


def _unpack_bf16_pair(w):
    lo = lax.bitcast_convert_type(w << 16, F32)
    hi = lax.bitcast_convert_type(w & jnp.uint32(0xFFFF0000), F32)
    return lo, hi


def _store_token_tiles(ref, x):
    w = _pack_bf16_pair(x)
    m, n = w.shape
    nt = n // LANES
    for c in range(nt):
        ref[pl.ds(c, m, stride=nt), :] = w[:, c * LANES:(c + 1) * LANES]


def _load_token_tiles(ref, lead, row0, m, nt):
    los, his = [], []
    view = ref if lead is None else ref.at[lead]
    for c in range(nt):
        lo, hi = _unpack_bf16_pair(view[pl.ds(row0 * nt + c, m, stride=nt), :])
        los.append(lo)
        his.append(hi)
    return jnp.concatenate(los + his, axis=1)


def _rel_bucket(dist):
    d = jnp.maximum(dist, 1).astype(F32)
    large = MAX_EXACT + (jnp.log(d / MAX_EXACT) / math.log(MAX_DISTANCE / MAX_EXACT)
                         * (N_BUCKETS - MAX_EXACT)).astype(I32)
    return jnp.where(dist < MAX_EXACT, dist, jnp.minimum(large, N_BUCKETS - 1)).astype(I32)


def _prep_body(x_ref, g_ref, o_ref):
    o_ref[...] = _rms(x_ref[...], g_ref[...]).astype(o_ref.dtype)


def _prep(x2d, g, tm):
    m, d = x2d.shape
    return pl.pallas_call(
        _prep_body, name="prep", grid=(m // tm,),
        in_specs=[pl.BlockSpec((tm, d), lambda i: (i, 0)), pl.BlockSpec((1, d), lambda i: (0, 0))],
        out_specs=pl.BlockSpec((tm, d), lambda i: (i, 0)),
        out_shape=jax.ShapeDtypeStruct((m, d), BF16), compiler_params=_cparams(1),
    )(x2d, g.reshape(1, d))


def _headnorm(acc, g, hd, scale):
    outs = []
    for j in range(acc.shape[1] // hd):
        sl = acc[:, j * hd:(j + 1) * hd]
        outs.append(sl * lax.rsqrt(jnp.mean(sl * sl, axis=-1, keepdims=True) + EPS))
    y = outs[0] if len(outs) == 1 else jnp.concatenate(outs, axis=1)
    y = y * g
    return y * scale if scale != 1.0 else y


def _proj_body(h_ref, *refs, n_w, mode, hd, scale):
    w_refs, (g_ref, o_ref, w_scr) = refs[:n_w], refs[n_w:]
    cw = w_refs[0].shape[1]

    @pl.when(pl.program_id(1) == 0)
    def _():
        for i, w_ref in enumerate(w_refs):
            w_scr[:, i * cw:(i + 1) * cw] = w_ref[...].astype(w_scr.dtype)

    def run(epilogue):
        acc = jnp.dot(h_ref[...], w_scr[...], preferred_element_type=F32)
        o_ref[...] = epilogue(acc).astype(o_ref.dtype)

    normed = lambda acc: _headnorm(acc, g_ref[...], hd, scale)
    if mode == "none":
        run(lambda acc: acc)
    elif mode == "sigmoid":
        run(jax.nn.sigmoid)
    elif mode == "norm":
        run(normed)
    else:
        @pl.when(pl.program_id(0) == 0)
        def _():
            run(normed)

        @pl.when(pl.program_id(0) != 0)
        def _():
            run(lambda acc: acc)


W_COLS = 512


def _proj(h, w, cols, tm, mode, out_dtype, gain=None, hd=None, scale=1.0, name="proj"):
    m, k = h.shape
    base, stride, n_w, n_tiles = cols
    tn = n_w * W_COLS
    if gain is None:
        g = jnp.ones((1, tn), F32)
    else:
        g = jnp.tile(gain.astype(F32).reshape(1, -1), (1, tn // gain.shape[-1]))
    body = functools.partial(_proj_body, n_w=n_w, mode=mode, hd=hd, scale=scale)
    w_specs = [pl.BlockSpec((k, W_COLS), functools.partial(lambda j, i, off: (0, base + stride * j + off), off=off))
               for off in range(n_w)]
    return pl.pallas_call(
        body, name=name, grid=(n_tiles, m // tm),
        in_specs=[pl.BlockSpec((tm, k), lambda j, i: (i, 0))] + w_specs + [pl.BlockSpec((1, tn), lambda j, i: (0, 0))],
        out_specs=pl.BlockSpec((tm, tn), lambda j, i: (i, j)),
        out_shape=jax.ShapeDtypeStruct((m, n_tiles * tn), out_dtype),
        scratch_shapes=[pltpu.VMEM((k, tn), BF16)], compiler_params=_cparams(2),
    )(h, *([w] * n_w), g)


PERM_ROWS = 256


def _perm_matrix(n, dil, inverse=False):
    per = n // dil
    o = lax.broadcasted_iota(I32, (n, n), 1 if inverse else 0)
    s = lax.broadcasted_iota(I32, (n, n), 0 if inverse else 1)
    return (s == (o % per) * dil + o // per).astype(BF16)


def _qkv_body(h_ref, wq_ref, wk_ref, wv_ref, gq_ref, gk_ref, q_ref, kv_ref, tok_ref, w_scr, *, dil, scale):
    tm = h_ref.shape[0]
    nh = HEADS_PER_GROUP_A
    gw = GROUP_W_A

    @pl.when(pl.program_id(0) == 0)
    def _():
        for i, w_ref in enumerate((wq_ref, wk_ref, wv_ref)):
            w_scr[:, i * gw:(i + 1) * gw] = w_ref[...].astype(BF16)

    acc = jnp.dot(h_ref[...], w_scr[...], preferred_element_type=F32)
    q = _headnorm(acc[:, :gw], gq_ref[...], HEAD_DIM_A, scale)
    k = _headnorm(acc[:, gw:2 * gw], gk_ref[...], HEAD_DIM_A, 1.0)
    v = acc[:, 2 * gw:]
    for i in range(nh):
        tok_ref[pl.ds(i, tm, stride=2 * nh), :] = k[:, i * HEAD_DIM_A:(i + 1) * HEAD_DIM_A]
        tok_ref[pl.ds(nh + i, tm, stride=2 * nh), :] = v[:, i * HEAD_DIM_A:(i + 1) * HEAD_DIM_A]

    if dil == 1:
        q_ref[0] = q.astype(q_ref.dtype)
        kv_ref[0, :, :gw] = k.astype(kv_ref.dtype)
        kv_ref[0, :, gw:] = v.astype(kv_ref.dtype)
        return
    sub = min(tm, PERM_ROWS)
    per = sub // dil
    pm = _perm_matrix(sub, dil)
    qkv = jnp.concatenate([q, k, v], axis=1).astype(BF16)
    for t in range(tm // sub):
        yp = jnp.dot(pm, qkv[t * sub:(t + 1) * sub], preferred_element_type=F32).astype(BF16)
        for r in range(dil):
            q_ref[r, t * per:(t + 1) * per, :] = yp[r * per:(r + 1) * per, :gw]
            kv_ref[r, t * per:(t + 1) * per, :] = yp[r * per:(r + 1) * per, gw:]


def _proj_qkv(h, w, sec, g, g_qa, g_ka, n_batch, seq, tm, dil, act_dtype=BF16):
    n, k = h.shape
    ln = seq // dil
    per = tm // dil
    mt = seq // tm
    assert seq % tm == 0
    assert dil == 1 or (act_dtype == BF16 and tm % PERM_ROWS == 0 and (PERM_ROWS // dil) % 16 == 0)
    tile = lambda v: jnp.tile(v.astype(F32).reshape(1, -1), (1, GROUP_W_A // v.shape[-1]))
    assert W_COLS == GROUP_W_A
    wspec = lambda blk: pl.BlockSpec((k, W_COLS), lambda m: (0, blk))
    q, kv, tok = pl.pallas_call(
        functools.partial(_qkv_body, dil=dil, scale=HEAD_DIM_A ** -0.5), name=f"proj_qkv{g}", grid=(n // tm,),
        in_specs=[pl.BlockSpec((tm, k), lambda m: (m, 0)),
                  wspec(sec["q"] + g), wspec(sec["k"] + g), wspec(sec["v"] + g),
                  pl.BlockSpec((1, GROUP_W_A), lambda m: (0, 0)), pl.BlockSpec((1, GROUP_W_A), lambda m: (0, 0))],
        out_specs=[pl.BlockSpec((None, dil, per, GROUP_W_A), lambda m: (m // mt, 0, m % mt, 0)),
                   pl.BlockSpec((None, dil, per, 2 * GROUP_W_A), lambda m: (m // mt, 0, m % mt, 0)),
                   pl.BlockSpec((tm * 2 * HEADS_PER_GROUP_A, LANES), lambda m: (m, 0))],
        out_shape=[jax.ShapeDtypeStruct((n_batch, dil, ln, GROUP_W_A), act_dtype),
                   jax.ShapeDtypeStruct((n_batch, dil, ln, 2 * GROUP_W_A), act_dtype),
                   jax.ShapeDtypeStruct((n * 2 * HEADS_PER_GROUP_A, LANES), F32)],
        scratch_shapes=[pltpu.VMEM((k, 3 * W_COLS), BF16)], compiler_params=_cparams(1),
    )(h, w, w, w, tile(g_qa), tile(g_ka))
    return q.reshape(n, GROUP_W_A), kv.reshape(n, 2 * GROUP_W_A), tok


def _bias_from_table(tab_ref, bidx, col):
    acc = jnp.zeros(bidx.shape, F32)
    for kb in range(N_BUCKETS):
        acc = jnp.where(bidx == kb, tab_ref[kb, col], acc)
    return acc


def _split_hi_lo(x):
    hi = x.astype(BF16)
    return hi, (x - hi.astype(F32)).astype(BF16)


ATTN_P_BLOCKS = 4


def _attn_p_body(tab_ref, bidx_ref, q_ref, kp_ref, kc_ref, vp_ref, vc_ref, o_ref, bias_scr, *, g, nb):
    i = pl.program_id(0)

    @pl.when(i == 0)
    def _():
        bidx = bidx_ref[...]
        for h in range(HEADS_PER_GROUP_A):
            bias_scr[h] = _bias_from_table(tab_ref, bidx, g * HEADS_PER_GROUP_A + h)

    blk = kp_ref.shape[0]
    k_all = jnp.concatenate([kp_ref[...], kc_ref[...]], axis=0)
    v_all = jnp.concatenate([vp_ref[...], vc_ref[...]], axis=0)
    row = lax.broadcasted_iota(I32, (blk, 2 * blk), 0)
    col = lax.broadcasted_iota(I32, (blk, 2 * blk), 1)
    dist = blk + row - col
    band = (dist >= 0) & (dist <= blk)
    lane = lax.broadcasted_iota(I32, (blk, LANES), 1)
    for u in range(ATTN_P_BLOCKS):
        has_prev = ((i * ATTN_P_BLOCKS + u) % nb) > 0
        valid = band & (has_prev | (col >= blk))
        q = q_ref[u * blk:(u + 1) * blk, :]
        k = k_all[u * blk:(u + 2) * blk]
        v = v_all[u * blk:(u + 2) * blk]
        lse_blk = jnp.zeros((blk, LANES), F32)
        outs = []
        for h in range(HEADS_PER_GROUP_A):
            hs = slice(h * HEAD_DIM_A, (h + 1) * HEAD_DIM_A)
            s = lax.dot_general(q[:, hs], k[:, hs], (((1,), (1,)), ((), ())), preferred_element_type=F32)
            s = jnp.where(valid, s + bias_scr[h], NEG_INF)
            m = jnp.max(s, axis=-1, keepdims=True)
            p = jnp.exp(s - m)
            l = jnp.sum(p, axis=-1, keepdims=True)
            o = jnp.dot(p.astype(BF16), v[:, hs], preferred_element_type=F32) / l
            outs.append(o)
            lse_blk = jnp.where(lane == h, m + jnp.log(l), lse_blk)
        lse_hi, lse_lo = _split_hi_lo(lse_blk)
        o_ref[u * blk:(u + 1) * blk, :] = jnp.concatenate(
            [jnp.concatenate(outs, axis=1).astype(BF16), lse_hi, lse_lo], axis=1)


O_EXT_W = GROUP_W_A + 2 * LANES


def _attn_prompt(q, kv, rel_bias, g, n_batch, seq):
    win, dil = DIL_GROUPS[g]
    blk = win // dil
    nb = seq // dil // blk
    n = n_batch * seq
    a = jnp.arange(blk)[:, None]
    c = jnp.arange(2 * blk)[None, :]
    bidx = _rel_bucket(jnp.maximum(blk + a - c, 0) * dil)

    nu = ATTN_P_BLOCKS
    assert (n // blk) % nu == 0 and (nb % nu == 0 or nb == 1)

    def prev(i):
        return jnp.maximum(i * nu - 1, 0)

    return pl.pallas_call(
        functools.partial(_attn_p_body, g=g, nb=nb), name=f"attn_p{g}", grid=(n // blk // nu,),
        in_specs=[
            pl.BlockSpec(memory_space=pltpu.SMEM),
            pl.BlockSpec((blk, 2 * blk), lambda i: (0, 0)),
            pl.BlockSpec((nu * blk, GROUP_W_A), lambda i: (i, 0)),
            pl.BlockSpec((blk, GROUP_W_A), lambda i: (prev(i), 0)),
            pl.BlockSpec((nu * blk, GROUP_W_A), lambda i: (i, 0)),
            pl.BlockSpec((blk, GROUP_W_A), lambda i: (prev(i), 1)),
            pl.BlockSpec((nu * blk, GROUP_W_A), lambda i: (i, 1)),
        ],
        out_specs=pl.BlockSpec((nu * blk, O_EXT_W), lambda i: (i, 0)),
        out_shape=jax.ShapeDtypeStruct((n, O_EXT_W), BF16),
        scratch_shapes=[pltpu.VMEM((HEADS_PER_GROUP_A, blk, 2 * blk), F32)],
        compiler_params=_cparams(1),
    )(rel_bias, bidx, q, kv, kv, kv, kv)


def _attn_s_body(tab_ref, bidx_ref, q_ref, kvn_ref, tokn_ref, cache_hbm, o_ref, newc_hbm,
                 raw, kv_scr, bias_scr, sem_in, sem_out, *, g, dil, lb, n_req):
    t_new = q_ref.shape[0]
    nh = HEADS_PER_GROUP_A
    n_rows = nh * t_new
    n_keys = kv_scr.shape[0]
    n_col = 2 * nh
    lb8 = lb * n_col
    n = pl.program_id(0)
    nbuf = raw.shape[0]
    ahead = nbuf // 2
    slot = n % nbuf

    def fetch(req):
        s = req % nbuf
        return pltpu.make_async_copy(cache_hbm.at[req], raw.at[s, pl.ds(0, lb8)], sem_in.at[s])

    def flush(req):
        s = req % nbuf
        return pltpu.make_async_copy(raw.at[s, pl.ds(t_new * n_col, lb8)], newc_hbm.at[req], sem_out.at[s])

    @pl.when(n == 0)
    def _():
        kv_scr[lb:, :] = jnp.zeros((n_keys - lb, kv_scr.shape[1]), BF16)
        bidx = bidx_ref[...]
        for h in range(nh):
            bias_scr[h * t_new:(h + 1) * t_new, :] = _bias_from_table(tab_ref, bidx, g * nh + h)
        for r in range(ahead):
            fetch(r).start()

    @pl.when(n >= nbuf - ahead)
    def _():
        flush(n - (nbuf - ahead)).wait()

    @pl.when(n + ahead < n_req)
    def _():
        fetch(n + ahead).start()

    fetch(n).wait()
    raw[slot, lb8:lb8 + t_new * n_col, :] = tokn_ref[...]
    flush(n).start()
    new = kvn_ref[...]

    rows_per = min(lb, 256)
    for c in range(n_col):
        for r0 in range(0, lb, rows_per):
            kv_scr[r0:r0 + rows_per, c * LANES:(c + 1) * LANES] = raw[
                slot, pl.ds(r0 * n_col + c, rows_per, stride=n_col), :].astype(BF16)
    kv_scr[lb:lb + 2 * t_new, :] = jnp.concatenate([new, jnp.zeros_like(new)], axis=0).astype(BF16)

    q = q_ref[...]
    qt = jnp.concatenate([q] * nh, axis=0)
    rr = lax.broadcasted_iota(I32, (n_rows, GROUP_W_A), 0)
    cc = lax.broadcasted_iota(I32, (n_rows, GROUP_W_A), 1)
    qbd = jnp.where(cc // HEAD_DIM_A == rr // t_new, qt, 0.0).astype(BF16)

    s = lax.dot_general(qbd, kv_scr[:, :GROUP_W_A], (((1,), (1,)), ((), ())), preferred_element_type=F32)
    row = lax.broadcasted_iota(I32, (n_rows, n_keys), 0)
    col = lax.broadcasted_iota(I32, (n_rows, n_keys), 1)
    delta = lb + (row & (t_new - 1)) - col
    valid = (delta >= 0) & (delta <= lb) & ((delta & (dil - 1)) == 0)
    s = jnp.where(valid, s + bias_scr[...], NEG_INF)
    m = jnp.max(s, axis=-1, keepdims=True)
    p = jnp.exp(s - m)
    l = jnp.sum(p, axis=-1, keepdims=True)
    o = jnp.dot(p.astype(BF16), kv_scr[:, GROUP_W_A:], preferred_element_type=F32) / l
    lse = m + jnp.log(l)
    lane = lax.broadcasted_iota(I32, (t_new, LANES), 1)
    lse_blk = jnp.zeros((t_new, LANES), F32)
    outs = []
    for h in range(nh):
        outs.append(o[h * t_new:(h + 1) * t_new, h * HEAD_DIM_A:(h + 1) * HEAD_DIM_A])
        lse_blk = jnp.where(lane == h, lse[h * t_new:(h + 1) * t_new, :], lse_blk)
    o_ref[...] = jnp.concatenate(outs + [lse_blk, jnp.zeros_like(lse_blk)], axis=1)

    @pl.when(n == n_req - 1)
    def _():
        for back in range(nbuf - ahead - 1, -1, -1):
            flush(n - back).wait()


ATTN_S_BUFS = 4


def _attn_sample(q, kv_new, tok_new, cache, rel_bias, g, t_new):
    win, dil = DIL_GROUPS[g]
    n_col = 2 * HEADS_PER_GROUP_A
    n_req, lb = cache.shape[0], cache.shape[1] // n_col
    assert t_new & (t_new - 1) == 0 and dil & (dil - 1) == 0 and lb == win
    n_keys = lb + LANES
    t = jnp.arange(t_new)[:, None]
    c = jnp.arange(n_keys)[None, :]
    bidx = _rel_bucket(jnp.clip(lb + t - c, 0, lb))
    return pl.pallas_call(
        functools.partial(_attn_s_body, g=g, dil=dil, lb=lb, n_req=n_req), name=f"attn_s{g}", grid=(n_req,),
        in_specs=[
            pl.BlockSpec(memory_space=pltpu.SMEM),
            pl.BlockSpec((t_new, n_keys), lambda n: (0, 0)),
            pl.BlockSpec((t_new, GROUP_W_A), lambda n: (n, 0)),
            pl.BlockSpec((t_new, 2 * GROUP_W_A), lambda n: (n, 0)),
            pl.BlockSpec((t_new * n_col, LANES), lambda n: (n, 0)),
            pl.BlockSpec(memory_space=pl.ANY),
        ],
        out_specs=[pl.BlockSpec((t_new, O_EXT_W), lambda n: (n, 0)), pl.BlockSpec(memory_space=pl.ANY)],
        out_shape=[jax.ShapeDtypeStruct((n_req * t_new, O_EXT_W), F32), jax.ShapeDtypeStruct(cache.shape, F32)],
        scratch_shapes=[pltpu.VMEM((ATTN_S_BUFS, (lb + t_new) * n_col, LANES), F32),
                        pltpu.VMEM((n_keys, 2 * GROUP_W_A), BF16),
                        pltpu.VMEM((HEADS_PER_GROUP_A * t_new, n_keys), F32),
                        pltpu.SemaphoreType.DMA((ATTN_S_BUFS,)), pltpu.SemaphoreType.DMA((ATTN_S_BUFS,))],
        compiler_params=_cparams(1),
    )(rel_bias, bidx, q, kv_new, tok_new, cache)


def _gmlp_body(u_ref, v_ref, w_ref, b_ref, o_ref, *, period):
    c = w_ref.shape[1]
    nch = u_ref.shape[0] // c
    gd = u_ref.shape[1] // N_GROUPS_B
    i = lax.broadcasted_iota(I32, (c, c), 0)
    j = lax.broadcasted_iota(I32, (c, c), 1)
    mask = (j <= i) & ((i // period) == (j // period))
    b = b_ref[...]
    for g in range(N_GROUPS_B):
        gs = slice(g * gd, (g + 1) * gd)
        wg = jnp.where(mask, w_ref[g], 0.0).astype(BF16)
        vg = [v_ref[ch * c:(ch + 1) * c, gs].astype(BF16) for ch in range(nch)]
        vg = vg[0] if nch == 1 else jnp.concatenate(vg, axis=1)
        sg = jnp.dot(wg, vg, preferred_element_type=F32) + b[:, g:g + 1]
        for ch in range(nch):
            u = u_ref[ch * c:(ch + 1) * c, gs].astype(F32)
            o_ref[ch * c:(ch + 1) * c, gs] = (u * sg[:, ch * gd:(ch + 1) * gd]).astype(o_ref.dtype)


def _gmlp(u, v, w, b, tm, period):
    m, wb = u.shape
    c = w.shape[1]
    return pl.pallas_call(
        functools.partial(_gmlp_body, period=period), name="gmlp", grid=(m // tm,),
        in_specs=[pl.BlockSpec((tm, wb), lambda i: (i, 0)), pl.BlockSpec((tm, wb), lambda i: (i, 0)),
                  pl.BlockSpec((N_GROUPS_B, c, c), lambda i: (0, 0, 0)),
                  pl.BlockSpec((c, N_GROUPS_B), lambda i: (0, 0))],
        out_specs=pl.BlockSpec((tm, wb), lambda i: (i, 0)),
        out_shape=jax.ShapeDtypeStruct((m, wb), BF16), compiler_params=_cparams(1),
    )(u, v, w, b)


def _gmlp_new_body(u_ref, v_ref, w_ref, b_ref, o_ref):
    t = w_ref.shape[0]
    n = u_ref.shape[0] // t
    width = u_ref.shape[1]
    v = v_ref[...].astype(F32).reshape(n, t, width)
    sg = jnp.broadcast_to(b_ref[...][None], (n, t, width))
    p_row = lax.broadcasted_iota(I32, (t, width), 0)
    for q in range(t):
        sg = sg + jnp.where(p_row >= q, w_ref[q], 0.0)[None] * v[:, q:q + 1, :]
    o_ref[...] = (u_ref[...].astype(F32).reshape(n, t, width) * sg).reshape(n * t, width).astype(o_ref.dtype)


def _gmlp_new(u, v, w_s, b_s, t):
    m, wb = u.shape
    gd = wb // N_GROUPS_B
    wq = jnp.repeat(jnp.transpose(w_s[:, :t, :t], (2, 1, 0)), gd, axis=-1)
    bq = jnp.repeat(b_s[:, :t].T, gd, axis=-1)
    whole = lambda a: pl.BlockSpec(a.shape, lambda i: (0,) * a.ndim)
    return pl.pallas_call(
        _gmlp_new_body, name="gmlp_new", grid=(1,),
        in_specs=[whole(u), whole(v), whole(wq), whole(bq)], out_specs=whole(u),
        out_shape=jax.ShapeDtypeStruct((m, wb), BF16), compiler_params=_cparams(1),
    )(u, v, wq, bq)


def _memattn_body(q_ref, kv_ref, o_ref, *, tiles):
    wm = N_HEADS_M * HEAD_DIM_M
    n_ct = HEAD_DIM_M // LANES
    per_key = 2 * n_ct * N_HEADS_M
    reqs = kv_ref.shape[0]
    tq = q_ref.shape[0] // reqs

    for r in range(reqs):
        kvr = kv_ref.at[r]

        def head_slab(kv, h):
            if not tiles:
                return kvr[:, kv * wm + h * HEAD_DIM_M:kv * wm + (h + 1) * HEAD_DIM_M].astype(BF16)
            n_keys = kvr.shape[0] // per_key
            parts = [kvr[pl.ds((kv * n_ct + ct) * N_HEADS_M + h, n_keys, stride=per_key), :] for ct in range(n_ct)]
            return jnp.concatenate(parts, axis=1).astype(BF16)

        q = q_ref[r * tq:(r + 1) * tq, :].astype(BF16)
        outs = []
        for h in range(N_HEADS_M):
            hs = slice(h * HEAD_DIM_M, (h + 1) * HEAD_DIM_M)
            k = head_slab(0, h)
            v = head_slab(1, h)
            s = lax.dot_general(q[:, hs], k, (((1,), (1,)), ((), ())), preferred_element_type=F32)
            m = jnp.max(s, axis=-1, keepdims=True)
            p = jnp.exp(s - m)
            l = jnp.sum(p, axis=-1, keepdims=True)
            outs.append(jnp.dot(p.astype(BF16), v, preferred_element_type=F32) / l)
        o_ref[r * tq:(r + 1) * tq, :] = jnp.concatenate(outs, axis=1).astype(o_ref.dtype)


def _memattn(q, kv, tq, out_dtype, tiles=False, reqs=1):
    m, wm = q.shape
    n = kv.shape[0]
    per = m // n // tq
    assert reqs == 1 or (per == 1 and n % reqs == 0)
    kv_spec = pl.BlockSpec((reqs,) + kv.shape[1:], lambda i: (i // per,) + (0,) * (kv.ndim - 1))
    tq = tq * reqs
    return pl.pallas_call(
        functools.partial(_memattn_body, tiles=tiles), name="memattn", grid=(m // tq,),
        in_specs=[pl.BlockSpec((tq, wm), lambda i: (i, 0)), kv_spec],
        out_specs=pl.BlockSpec((tq, wm), lambda i: (i, 0)),
        out_shape=jax.ShapeDtypeStruct((m, wm), out_dtype), compiler_params=_cparams(1),
    )(q, kv)


def _mix_body(gt_ref, o0_ref, o1_ref, o2_ref, ob_ref, om_ref, wpa_ref, wpb_ref, wpm_ref, z_ref, *, dils):
    d = z_ref.shape[1]
    tm = z_ref.shape[0]
    os_, ls = [], []
    for o_ref, dil in zip((o0_ref, o1_ref, o2_ref), dils):
        if dil == 1:
            x = o_ref[...].astype(F32)
        else:
            rows = jnp.concatenate([o_ref[r] for r in range(dil)], axis=0)
            x = jnp.dot(_perm_matrix(tm, dil, inverse=True), rows, preferred_element_type=F32)
        os_.append(x[:, :GROUP_W_A])
        ls.append(x[:, GROUP_W_A:GROUP_W_A + LANES] + x[:, GROUP_W_A + LANES:])
    l0, l1, l2 = ls
    mx = jnp.maximum(jnp.maximum(l0, l1), l2)
    e0, e1, e2 = jnp.exp(l0 - mx), jnp.exp(l1 - mx), jnp.exp(l2 - mx)
    den = e0 + e1 + e2
    w0, w1, w2 = e0 / den, e1 / den, e2 / den
    cols = []
    for h in range(HEADS_PER_GROUP_A):
        hs = slice(h * HEAD_DIM_A, (h + 1) * HEAD_DIM_A)
        cols.append(w0[:, h:h + 1] * os_[0][:, hs] + w1[:, h:h + 1] * os_[1][:, hs] + w2[:, h:h + 1] * os_[2][:, hs])
    oa = jnp.concatenate(cols, axis=1).astype(BF16)
    pa = jnp.dot(oa, wpa_ref[...], preferred_element_type=F32)
    pb = jnp.dot(ob_ref[...].astype(BF16), wpb_ref[...], preferred_element_type=F32)
    pm = jnp.dot(om_ref[...].astype(BF16), wpm_ref[...], preferred_element_type=F32)
    z = (gt_ref[:, 0:d].astype(F32) * pa + gt_ref[:, d:2 * d].astype(F32) * pb
         + gt_ref[:, 2 * d:3 * d].astype(F32) * pm)
    z_ref[...] = z.astype(z_ref.dtype)


def _mix(gates, o_list, dils, seq, ob, om, wpa, wpb, wpm, tm):
    m = gates.shape[0]
    d = wpa.shape[1]
    mt = seq // tm if any(dl > 1 for dl in dils) else 1

    def rows(width):
        return pl.BlockSpec((tm, width), lambda i: (i, 0))

    def whole(a):
        return pl.BlockSpec(a.shape, lambda i: (0, 0))

    o_specs, o_args = [], []
    for o, dil in zip(o_list, dils):
        if dil == 1:
            o_specs.append(rows(O_EXT_W))
            o_args.append(o)
        else:
            assert seq % tm == 0 and (tm // dil) % 16 == 0
            o_specs.append(pl.BlockSpec((None, dil, tm // dil, O_EXT_W), lambda i: (i // mt, 0, i % mt, 0)))
            o_args.append(o.reshape(m // seq, dil, seq // dil, O_EXT_W))
    return pl.pallas_call(
        functools.partial(_mix_body, dils=tuple(dils)), name="mix", grid=(m // tm,),
        in_specs=[rows(gates.shape[1])] + o_specs
                 + [rows(ob.shape[1]), rows(om.shape[1]), whole(wpa), whole(wpb), whole(wpm)],
        out_specs=rows(d),
        out_shape=jax.ShapeDtypeStruct((m, d), BF16), compiler_params=_cparams(1),
    )(gates, *o_args, ob, om, wpa, wpb, wpm)


def _route(logits):
    lane = lax.broadcasted_iota(I32, logits.shape, 1)
    lane_f = lane.astype(F32)
    is_g = lane < N_EXPERT_GROUPS
    gmax = jnp.max(jnp.where(is_g, logits, -jnp.inf), axis=1, keepdims=True)
    gsel = jnp.min(jnp.where(is_g & (logits == gmax), lane_f, float(LANES)), axis=1, keepdims=True).astype(I32)
    gden = jnp.sum(jnp.where(is_g, jnp.exp(logits - gmax), 0.0), axis=1, keepdims=True)
    pg = 1.0 / gden
    e_lane = lane - N_EXPERT_GROUPS
    in_grp = (e_lane >= 0) & (e_lane < N_EXPERTS) & ((e_lane // EXPERTS_PER_GROUP) == gsel)
    m1 = jnp.max(jnp.where(in_grp, logits, -jnp.inf), axis=1, keepdims=True)
    i1 = jnp.min(jnp.where(in_grp & (logits == m1), lane_f, float(LANES)), axis=1, keepdims=True).astype(I32)
    rest = in_grp & (lane != i1)
    m2 = jnp.max(jnp.where(rest, logits, -jnp.inf), axis=1, keepdims=True)
    i2 = jnp.min(jnp.where(rest & (logits == m2), lane_f, float(LANES)), axis=1, keepdims=True).astype(I32)
    e2 = jnp.exp(m2 - m1)
    w1 = pg / (1.0 + e2)
    w2 = pg * e2 / (1.0 + e2)
    eid = jnp.where(lane == 0, i1 - N_EXPERT_GROUPS, jnp.where(lane == 1, i2 - N_EXPERT_GROUPS, 0))
    wts = jnp.where(lane == 0, w1, jnp.where(lane == 1, w2, 0.0))
    return eid, wts


def _resid_body(x_ref, z_ref, wo_ref, gf_ref, wr_ref, br_ref, xmid_ref, hpk_ref, eid_ref, wts_ref):
    xm = x_ref[...] + jnp.dot(z_ref[...], wo_ref[...], preferred_element_type=F32)
    xmid_ref[...] = xm
    hf = _rms(xm, gf_ref[...])
    wr = wr_ref[...]
    wr_hi = wr.astype(BF16)
    wr_lo = (wr - wr_hi.astype(F32)).astype(BF16)
    hf_hi = hf.astype(BF16)
    hf_lo = (hf - hf_hi.astype(F32)).astype(BF16)
    logits = (jnp.dot(hf_hi, wr_hi, preferred_element_type=F32) + jnp.dot(hf_hi, wr_lo, preferred_element_type=F32)
              + jnp.dot(hf_lo, wr_hi, preferred_element_type=F32) + br_ref[...])
    eid, wts = _route(logits)
    eid_ref[...] = eid
    wts_ref[...] = wts
    _store_token_tiles(hpk_ref, hf)


def _resid(x2d, z, wo, g_ffn, w_r, b_r, tm):
    m, d = x2d.shape

    def rows(width):
        return pl.BlockSpec((tm, width), lambda i: (i, 0))

    def whole(a):
        return pl.BlockSpec(a.shape, lambda i: (0, 0))

    gf = g_ffn.reshape(1, d)
    nt = d // 2 // LANES
    return pl.pallas_call(
        _resid_body, name="resid", grid=(m // tm,),
        in_specs=[rows(d), rows(d), whole(wo), whole(gf), whole(w_r), whole(b_r)],
        out_specs=[rows(d), pl.BlockSpec((tm * nt, LANES), lambda i: (i, 0)), rows(LANES), rows(LANES)],
        out_shape=[jax.ShapeDtypeStruct((m, d), F32), jax.ShapeDtypeStruct((m * nt, LANES), U32),
                   jax.ShapeDtypeStruct((m, LANES), I32), jax.ShapeDtypeStruct((m, LANES), F32)],
        compiler_params=_cparams(1),
    )(x2d, z, wo, gf, w_r, b_r)


TOKEN_TILE = 8
GATHER_UNROLL = 8


def _gather_rows(idx_at, src_hbm, dst, sem, n):
    def body(i, carry):
        src = pl.multiple_of(idx_at(i), TOKEN_TILE)
        dst_row = pl.multiple_of(i * TOKEN_TILE, TOKEN_TILE)
        pltpu.make_async_copy(src_hbm.at[pl.ds(src, TOKEN_TILE)], dst.at[pl.ds(dst_row, TOKEN_TILE)], sem).start()
        return carry
    lax.fori_loop(0, n, body, 0, unroll=GATHER_UNROLL)


FFN_CHUNKS = 8
FFN_BUFS = 3


def _ffn_body(blk_e_ref, nused_ref, eord_ref, enext_ref, idx_hbm, h_hbm, wg_hbm, wu_hbm, wd_hbm, y_ref,
              idx_s, isem, buf, sem, x_scr, a_scr, b_scr, hm_scr, wg_f, wu_f, wd_f, wsem, wg_s, wu_s, wd_s):
    j = pl.program_id(0)
    nused = nused_ref[0]
    last = nused - 1
    rows = buf.shape[1] // TOKEN_TILE
    e = blk_e_ref[j]

    def idx_copy(b):
        s = b % FFN_BUFS
        return pltpu.make_async_copy(idx_hbm.at[jnp.minimum(b, last)], idx_s.at[s], isem.at[s])

    def wait_rows(s):
        pltpu.make_async_copy(h_hbm.at[pl.ds(0, rows * TOKEN_TILE)], buf.at[s], sem.at[s]).wait()
    new_expert = (j == 0) | (e != blk_e_ref[jnp.maximum(j - 1, 0)])
    ws = eord_ref[j] % 2

    def fetch_w(expert, s):
        return [pltpu.make_async_copy(src.at[expert], dst.at[s], wsem.at[s, i])
                for i, (src, dst) in enumerate(((wg_hbm, wg_f), (wu_hbm, wu_f), (wd_hbm, wd_f)))]

    @pl.when(j == 0)
    def _():
        for c in fetch_w(e, ws):
            c.start()
        for b in range(FFN_BUFS):
            idx_copy(b).start()
        for b in range(FFN_BUFS - 1):
            idx_copy(b).wait()
            _gather_rows(lambda i, b=b: idx_s[b, 0, i], h_hbm, buf.at[b], sem.at[b], rows)

    @pl.when(new_expert)
    def _():
        nxt = enext_ref[j]

        @pl.when(nxt >= 0)
        def _():
            for c in fetch_w(nxt, 1 - ws):
                c.start()

        for c in fetch_w(e, ws):
            c.wait()
        wg_s[...] = wg_f[ws].astype(BF16)
        wu_s[...] = wu_f[ws].astype(BF16)
        wd_s[...] = wd_f[ws].astype(BF16)

    @pl.when(j < nused)
    def _():
        slot = j % FFN_BUFS
        s2 = (j + 2) % FFN_BUFS
        per = rows // FFN_CHUNKS
        idx_copy(j + 2).wait()

        @pl.when(j < last)
        def _():
            idx_copy(j + 3).start()

        def issue(c):
            for i in range(c * per, (c + 1) * per):
                src = pl.multiple_of(idx_s[s2, 0, i], TOKEN_TILE)
                pltpu.make_async_copy(h_hbm.at[pl.ds(src, TOKEN_TILE)],
                                      buf.at[s2, pl.ds(i * TOKEN_TILE, TOKEN_TILE)], sem.at[s2]).start()

        wait_rows(slot)
        x_scr[...] = _load_token_tiles(buf, slot, 0, rows, TOKEN_TILE).astype(BF16)
        dn = a_scr.shape[1] // 2
        for c in range(2):
            issue(c)
            a_scr[:, c * dn:(c + 1) * dn] = jnp.dot(x_scr[...], wg_s[:, c * dn:(c + 1) * dn],
                                                     preferred_element_type=F32)
        for c in range(2):
            issue(2 + c)
            b_scr[:, c * dn:(c + 1) * dn] = jnp.dot(x_scr[...], wu_s[:, c * dn:(c + 1) * dn],
                                                     preferred_element_type=F32)
        a = a_scr[...]
        hm_scr[...] = (a * jax.nn.sigmoid(a) * b_scr[...]).astype(BF16)
        half = wd_s.shape[1] // 2
        n_dc = FFN_CHUNKS - 4
        dq = half // n_dc
        for c in range(n_dc):
            issue(4 + c)
            lo = jnp.dot(hm_scr[...], wd_s[:, c * dq:(c + 1) * dq], preferred_element_type=F32)
            hi = jnp.dot(hm_scr[...], wd_s[:, half + c * dq:half + (c + 1) * dq], preferred_element_type=F32)
            w = _pack_bf16_pair(jnp.concatenate([lo, hi], axis=1))
            for t in range(dq // LANES):
                y_ref[pl.ds(c * (dq // LANES) + t, rows, stride=TOKEN_TILE), :] = w[:, t * LANES:(t + 1) * LANES]

        @pl.when(j == last)
        def _():
            wait_rows((j + 1) % FFN_BUFS)
            wait_rows(s2)

    @pl.when(j >= nused)
    def _():
        y_ref[...] = jnp.zeros(y_ref.shape, y_ref.dtype)


def _ffn(hpk, src_tok, blk_e, nused, eord, enext, wg, wu, wd):
    n_blocks = blk_e.shape[0]
    rows = MOE_ROWS
    tiles = rows * TOKEN_TILE
    d, de = wg.shape[1], wg.shape[2]
    assert d // 2 == TOKEN_TILE * LANES and rows % FFN_CHUNKS == 0 and (d // 2) % ((FFN_CHUNKS - 4) * LANES) == 0
    assert n_blocks >= N_EXPERTS + FFN_BUFS
    idx = (src_tok * TOKEN_TILE).reshape(n_blocks, 1, rows)
    hbm = pl.BlockSpec(memory_space=pl.ANY)
    grid_spec = pltpu.PrefetchScalarGridSpec(
        num_scalar_prefetch=4, grid=(n_blocks,),
        in_specs=[hbm, hbm, hbm, hbm, hbm],
        out_specs=pl.BlockSpec((tiles, LANES), lambda j, *_: (j, 0)),
        scratch_shapes=[pltpu.SMEM((FFN_BUFS, 1, rows), I32), pltpu.SemaphoreType.DMA((FFN_BUFS,)),
                        pltpu.VMEM((FFN_BUFS, tiles, LANES), U32), pltpu.SemaphoreType.DMA((FFN_BUFS,)),
                        pltpu.VMEM((rows, d), BF16), pltpu.VMEM((rows, de), F32), pltpu.VMEM((rows, de), F32),
                        pltpu.VMEM((rows, de), BF16),
                        pltpu.VMEM((2, d, de), wg.dtype), pltpu.VMEM((2, d, de), wu.dtype),
                        pltpu.VMEM((2, de, d), wd.dtype), pltpu.SemaphoreType.DMA((2, 3)),
                        pltpu.VMEM((d, de), BF16), pltpu.VMEM((d, de), BF16), pltpu.VMEM((de, d), BF16)],
    )
    return pl.pallas_call(
        _ffn_body, name="ffn", grid_spec=grid_spec,
        out_shape=jax.ShapeDtypeStruct((n_blocks * tiles, LANES), U32), compiler_params=_cparams(1),
    )(blk_e, nused, eord, enext, idx, hpk, wg, wu, wd)


def _combine_body(idx_cur_ref, idx_nxt_ref, x_ref, w_ref, y_hbm, o_ref, buf, sem, *, n):
    i = pl.program_id(0)
    slot = i % 2
    rows = buf.shape[1] // TOKEN_TILE
    tm = rows // TOP_K

    @pl.when(i == 0)
    def _():
        _gather_rows(lambda r: idx_cur_ref[0, 0, r], y_hbm, buf.at[0], sem.at[0], rows)

    if n > 1:
        @pl.when(i + 1 < n)
        def _():
            _gather_rows(lambda r: idx_nxt_ref[0, 0, r], y_hbm, buf.at[1 - slot], sem.at[1 - slot], rows)

    pltpu.make_async_copy(y_hbm.at[pl.ds(0, rows * TOKEN_TILE)], buf.at[slot], sem.at[slot]).wait()
    w = w_ref[...]
    y0 = _load_token_tiles(buf, slot, 0, tm, TOKEN_TILE)
    y1 = _load_token_tiles(buf, slot, tm, tm, TOKEN_TILE)
    o_ref[...] = x_ref[...] + (w[:, 0:1] * y0 + w[:, 1:2] * y1)


def _combine(xmid, wts, dest, ypk, tm):
    m, d = xmid.shape
    nt = m // tm
    assert d // 2 == TOKEN_TILE * LANES
    idx = (dest * TOKEN_TILE).reshape(nt, tm, TOP_K).transpose(0, 2, 1).reshape(nt, 1, TOP_K * tm)

    def idx_spec(ahead):
        return pl.BlockSpec((1, 1, TOP_K * tm), lambda i: (jnp.minimum(i + ahead, nt - 1), 0, 0),
                            memory_space=pltpu.SMEM)

    return pl.pallas_call(
        functools.partial(_combine_body, n=nt), name="combine", grid=(nt,),
        in_specs=[idx_spec(0), idx_spec(1), pl.BlockSpec((tm, d), lambda i: (i, 0)),
                  pl.BlockSpec((tm, LANES), lambda i: (i, 0)), pl.BlockSpec(memory_space=pl.ANY)],
        out_specs=pl.BlockSpec((tm, d), lambda i: (i, 0)),
        out_shape=jax.ShapeDtypeStruct((m, d), F32),
        scratch_shapes=[pltpu.VMEM((2, TOP_K * tm * TOKEN_TILE, LANES), U32), pltpu.SemaphoreType.DMA((2,))],
        compiler_params=_cparams(1),
    )(idx, idx, xmid, wts, ypk)


def _moe_plan(eid2):
    n_slot = eid2.shape[0] * TOP_K
    eid = eid2.reshape(n_slot)
    onehot = (eid[:, None] == jnp.arange(N_EXPERTS, dtype=I32)[None, :]).astype(I32)
    csum = jnp.cumsum(onehot, axis=0)
    rank = jnp.take_along_axis(csum, eid[:, None], axis=1)[:, 0] - 1
    counts = csum[-1]
    padded = (counts + MOE_ROWS - 1) // MOE_ROWS * MOE_ROWS
    pend = jnp.cumsum(padded)
    pstart = pend - padded
    dest = pstart[eid] + rank
    n_blocks = -(-n_slot // MOE_ROWS) + N_EXPERTS
    nused = (pend[-1] // MOE_ROWS).astype(I32)
    blk = jnp.minimum(jnp.arange(n_blocks, dtype=I32), nused - 1)
    blk_e = jnp.minimum(jnp.searchsorted(pend, blk * MOE_ROWS, side="right"), N_EXPERTS - 1).astype(I32)
    n_rows = n_blocks * MOE_ROWS
    src_tok = (jnp.arange(n_rows, dtype=I32) % (n_slot // TOP_K)).at[dest].set(
        jnp.arange(n_slot, dtype=I32) // TOP_K, unique_indices=True, mode="promise_in_bounds")
    has_rows = counts > 0
    ord_e = jnp.cumsum(has_rows.astype(I32)) - 1
    ids = jnp.arange(N_EXPERTS, dtype=I32)
    later = has_rows[None, :] & (ids[None, :] > ids[:, None])
    next_e = jnp.where(later.any(axis=1), jnp.argmax(later, axis=1), -1).astype(I32)
    return dest.astype(I32), src_tok, blk_e, nused.reshape(1), ord_e[blk_e].astype(I32), next_e[blk_e]


def _mixer(x2d, tm, p, attn_fn, gmlp_fn, mem_fn):
    h = _prep(x2d, p["g_mix"], tm)
    qdt = p["q_dtype"]
    wi, sec = p["w_in"], p["sec"]
    o_list, dils, seq = attn_fn(h)
    n_ub = sec["vb"] - sec["u"]
    ub = _proj(h, wi, (sec["u"], n_ub, n_ub, 1), tm, "none", BF16, name="proj_ub")
    n_vb = sec["qm"] - sec["vb"]
    vb = _proj(h, wi, (sec["vb"], n_vb, n_vb, 1), tm, "norm", p["vb_dtype"], p["g_vb"], n_vb * W_COLS,
               name="proj_vb")
    n_qm = sec["gt"] - sec["qm"]
    qm = _proj(h, wi, (sec["qm"], n_qm, n_qm, 1), tm, "norm", qdt, p["g_qm"], HEAD_DIM_M,
               HEAD_DIM_M ** -0.5, name="proj_qm")
    gates = _proj(h, wi, (sec["gt"], 2, 2, (sec["end"] - sec["gt"]) // 2), tm, "sigmoid", BF16, name="proj_gates")
    ob = gmlp_fn(ub, vb)
    om = mem_fn(qm)
    z = _mix(gates, o_list, dils, seq, ob, om, p["w_pa"], p["w_pb"], p["w_pm"], min(tm, 256))
    xmid, hpk, eid, wts = _resid(x2d, z, p["w_o"], p["g_ffn"], p["w_r"], p["b_r"], min(tm, 256))
    return vb, xmid, hpk, eid, wts


def kernel(x_prompt, x_sample, mem_prompt, cache_a0_kv, cache_a1_kv, cache_a2_kv, cache_mem_kv, rel_bias, g_mix, w_in, g_qa, g_ka, w_pa, g_vb, w_s, b_s, w_pb, g_mem, w_mk, w_mv, g_qm, g_km, w_pm, w_o, g_ffn, w_rg, b_rg, w_re, b_re, w_gate, w_up, w_down):
    n_b, seq, d = x_prompt.shape
    n_s, t_s, _ = x_sample.shape
    depth = w_in.shape[0]
    assert depth == 1
    l = 0
    caches = (cache_a0_kv, cache_a1_kv, cache_a2_kv)
    n_g = len(DIL_GROUPS)
    wb = w_pb.shape[1]
    wm = N_HEADS_M * HEAD_DIM_M

    offs = [0, WIDTH_A, 2 * WIDTH_A, 3 * WIDTH_A, 3 * WIDTH_A + wb, 3 * WIDTH_A + 2 * wb,
            3 * WIDTH_A + 2 * wb + wm, w_in.shape[2]]
    assert all(o % W_COLS == 0 for o in offs) and (offs[7] - offs[6]) % (2 * W_COLS) == 0
    sec = dict(zip(("q", "k", "v", "u", "vb", "qm", "gt", "end"), (o // W_COLS for o in offs)))
    p = {
        "g_mix": g_mix[l], "g_qa": g_qa[l], "g_ka": g_ka[l], "g_vb": g_vb[l], "g_qm": g_qm[l], "g_ffn": g_ffn[l],
        "w_in": w_in[l], "sec": sec,
        "w_pa": w_pa[l].astype(BF16), "w_pb": w_pb[l].astype(BF16), "w_pm": w_pm[l].astype(BF16),
        "w_o": w_o[l].astype(BF16),
    }
    n_r = N_EXPERT_GROUPS + N_EXPERTS
    p["w_r"] = jnp.concatenate([w_rg[l], w_re[l], jnp.zeros((d, LANES - n_r), F32)], axis=1)
    p["b_r"] = jnp.concatenate([b_rg[l], b_re[l], jnp.zeros((LANES - n_r,), F32)]).reshape(1, LANES)

    n_mem = mem_prompt.shape[1]
    h_mem = _prep(mem_prompt.reshape(n_b * n_mem, d), g_mem[l], 256)
    w_mkv = jnp.concatenate([w_mk[l], w_mv[l]], axis=1)
    n_mw = wm // W_COLS
    mkv = _proj(h_mem, w_mkv, (0, n_mw, n_mw, 2), 256, "norm_first", F32, g_km[l], HEAD_DIM_M, name="proj_mkv")
    new_mem_p = mkv.reshape(1, n_b, n_mem, 2, N_HEADS_M, HEAD_DIM_M)

    ps = dict(p, q_dtype=F32, vb_dtype=F32)
    m_s = n_s * t_s
    n_col = 2 * HEADS_PER_GROUP_A
    new_s = []

    def attn_sample(h):
        o_list = []
        for g in range(n_g):
            q, kv, tok = _proj_qkv(h, w_in[l], sec, g, g_qa[l], g_ka[l], 1, m_s, m_s, 1, act_dtype=F32)
            cache3 = caches[g][l].reshape(n_s, caches[g].shape[2] * n_col, LANES)
            o, newc = _attn_sample(q, kv, tok, cache3, rel_bias, g, t_s)
            o_list.append(o)
            new_s.append(newc.reshape(caches[g][l:l + 1].shape))
        return o_list, [1] * n_g, t_s

    n_ct = HEAD_DIM_M // LANES
    mem_tiles = (cache_mem_kv[l].reshape(n_s, n_mem, 2, N_HEADS_M, n_ct, LANES).transpose(0, 1, 2, 4, 3, 5)
                 .reshape(n_s, n_mem * 2 * n_ct * N_HEADS_M, LANES))
    vb_s, xmid_s, hpk_s, eid_s, wts_s = _mixer(
        x_sample.reshape(m_s, d), m_s, ps, attn_sample,
        lambda u, v: _gmlp_new(u, v, w_s[l], b_s[l], t_s),
        lambda qm: _memattn(qm, mem_tiles, t_s, F32, tiles=True, reqs=4))
    new_vb_s = vb_s.reshape(1, n_s, t_s, wb)

    pp = dict(p, q_dtype=BF16, vb_dtype=BF16)
    new_p = []

    def attn_prompt(h):
        o_list = []
        for g, (win, dil) in enumerate(DIL_GROUPS):
            q, kv, tok = _proj_qkv(h, w_in[l], sec, g, g_qa[l], g_ka[l], n_b, seq, 512, dil)
            o_list.append(_attn_prompt(q, kv, rel_bias, g, n_b, seq))
            keep = min(win, seq)
            kv5 = tok.reshape(n_b, seq, 2, HEADS_PER_GROUP_A, HEAD_DIM_A)
            new_p.append((kv5 if keep == seq else kv5[:, seq - keep:])[None])
        return o_list, [dl for _, dl in DIL_GROUPS], seq

    _, xmid_p, hpk_p, eid_p, wts_p = _mixer(
        x_prompt.reshape(n_b * seq, d), 512, pp, attn_prompt,
        lambda u, v: _gmlp(u, v, w_s[l], b_s[l].T, 512, CHUNK),
        lambda qm: _memattn(qm, mkv.reshape(n_b, n_mem, 2 * wm), 512, BF16))

    n_p = n_b * seq
    hpk = jnp.concatenate([hpk_p, hpk_s], axis=0)
    eid2 = jnp.concatenate([eid_p[:, :TOP_K], eid_s[:, :TOP_K]], axis=0)
    dest, src_tok, blk_e, nused, eord, enext = _moe_plan(eid2)
    ypk = _ffn(hpk, src_tok, blk_e, nused, eord, enext, w_gate[l], w_up[l], w_down[l])
    dest2 = dest.reshape(-1, TOP_K)
    y_p = _combine(xmid_p, wts_p, dest2[:n_p], ypk, 256)
    y_s = _combine(xmid_s, wts_s, dest2[n_p:], ypk, m_s)

    return (y_p.reshape(n_b, seq, d), y_s.reshape(n_s, t_s, d), new_p[0], new_p[1], new_p[2], new_mem_p,
            new_s[0], new_s[1], new_s[2], new_vb_s)
```

```python
import functools
import math

import jax
import jax.numpy as jnp
from jax import lax
from jax.experimental import pallas as pl
from jax.experimental.pallas import tpu as pltpu

F32 = jnp.float32
BF16 = jnp.bfloat16
I32 = jnp.int32
U32 = jnp.uint32

EPS = 1e-6
NEG_INF = -1e30

HEAD_DIM_A = 128
HEADS_PER_GROUP_A = 4
DIL_GROUPS = ((128, 1), (512, 4), (2048, 16))
GROUP_W_A = HEADS_PER_GROUP_A * HEAD_DIM_A
WIDTH_A = len(DIL_GROUPS) * GROUP_W_A
CHUNK = 128
N_GROUPS_B = 8
N_HEADS_M = 4
HEAD_DIM_M = 256
N_BUCKETS = 32
MAX_EXACT = N_BUCKETS // 2
MAX_DISTANCE = 2048
N_EXPERT_GROUPS = 4
EXPERTS_PER_GROUP = 8
N_EXPERTS = N_EXPERT_GROUPS * EXPERTS_PER_GROUP
TOP_K = 2
LANES = 128
MOE_ROWS = 256
VMEM_LIMIT = 56 * 1024 * 1024


def _cparams(n_grid, vmem=VMEM_LIMIT):
    return pltpu.CompilerParams(dimension_semantics=("arbitrary",) * n_grid, vmem_limit_bytes=vmem)


def _rms(x, g):
    return x * lax.rsqrt(jnp.mean(x * x, axis=-1, keepdims=True) + EPS) * g


def _pack_bf16_pair(x):
    n = x.shape[1] // 2
    lo = lax.bitcast_convert_type(x[:, :n].astype(BF16).astype(F32), U32)
    hi = lax.bitcast_convert_type(x[:, n:].astype(BF16).astype(F32), U32)
    return (hi & jnp.uint32(0xFFFF0000)) | (lo >> 16)


def _unpack_bf16_pair(w):
    lo = lax.bitcast_convert_type(w << 16, F32)
    hi = lax.bitcast_convert_type(w & jnp.uint32(0xFFFF0000), F32)
    return lo, hi


def _store_token_tiles(ref, x):
    w = _pack_bf16_pair(x)
    m, n = w.shape
    nt = n // LANES
    for c in range(nt):
        ref[pl.ds(c, m, stride=nt), :] = w[:, c * LANES:(c + 1) * LANES]


def _load_token_tiles(ref, lead, row0, m, nt):
    los, his = [], []
    view = ref if lead is None else ref.at[lead]
    for c in range(nt):
        lo, hi = _unpack_bf16_pair(view[pl.ds(row0 * nt + c, m, stride=nt), :])
        los.append(lo)
        his.append(hi)
    return jnp.concatenate(los + his, axis=1)


def _rel_bucket(dist):
    d = jnp.maximum(dist, 1).astype(F32)
    large = MAX_EXACT + (jnp.log(d / MAX_EXACT) / math.log(MAX_DISTANCE / MAX_EXACT)
                         * (N_BUCKETS - MAX_EXACT)).astype(I32)
    return jnp.where(dist < MAX_EXACT, dist, jnp.minimum(large, N_BUCKETS - 1)).astype(I32)


def _prep_body(x_ref, g_ref, o_ref):
    o_ref[...] = _rms(x_ref[...], g_ref[...]).astype(o_ref.dtype)


def _prep(x2d, g, tm):
    m, d = x2d.shape
    return pl.pallas_call(
        _prep_body, name="prep", grid=(m // tm,),
        in_specs=[pl.BlockSpec((tm, d), lambda i: (i, 0)), pl.BlockSpec((1, d), lambda i: (0, 0))],
        out_specs=pl.BlockSpec((tm, d), lambda i: (i, 0)),
        out_shape=jax.ShapeDtypeStruct((m, d), BF16), compiler_params=_cparams(1),
    )(x2d, g.reshape(1, d))


def _headnorm(acc, g, hd, scale):
    outs = []
    for j in range(acc.shape[1] // hd):
        sl = acc[:, j * hd:(j + 1) * hd]
        outs.append(sl * lax.rsqrt(jnp.mean(sl * sl, axis=-1, keepdims=True) + EPS))
    y = outs[0] if len(outs) == 1 else jnp.concatenate(outs, axis=1)
    y = y * g
    return y * scale if scale != 1.0 else y


def _proj_body(h_ref, *refs, n_w, mode, hd, scale):
    w_refs, (g_ref, o_ref, w_scr) = refs[:n_w], refs[n_w:]
    cw = w_refs[0].shape[1]

    @pl.when(pl.program_id(1) == 0)
    def _():
        for i, w_ref in enumerate(w_refs):
            w_scr[:, i * cw:(i + 1) * cw] = w_ref[...].astype(w_scr.dtype)

    def run(epilogue):
        acc = jnp.dot(h_ref[...], w_scr[...], preferred_element_type=F32)
        o_ref[...] = epilogue(acc).astype(o_ref.dtype)

    normed = lambda acc: _headnorm(acc, g_ref[...], hd, scale)
    if mode == "none":
        run(lambda acc: acc)
    elif mode == "sigmoid":
        run(jax.nn.sigmoid)
    elif mode == "norm":
        run(normed)
    else:
        @pl.when(pl.program_id(0) == 0)
        def _():
            run(normed)

        @pl.when(pl.program_id(0) != 0)
        def _():
            run(lambda acc: acc)


W_COLS = 512


def _proj(h, w, cols, tm, mode, out_dtype, gain=None, hd=None, scale=1.0, name="proj"):
    m, k = h.shape
    base, stride, n_w, n_tiles = cols
    tn = n_w * W_COLS
    if gain is None:
        g = jnp.ones((1, tn), F32)
    else:
        g = jnp.tile(gain.astype(F32).reshape(1, -1), (1, tn // gain.shape[-1]))
    body = functools.partial(_proj_body, n_w=n_w, mode=mode, hd=hd, scale=scale)
    w_specs = [pl.BlockSpec((k, W_COLS), functools.partial(lambda j, i, off: (0, base + stride * j + off), off=off))
               for off in range(n_w)]
    return pl.pallas_call(
        body, name=name, grid=(n_tiles, m // tm),
        in_specs=[pl.BlockSpec((tm, k), lambda j, i: (i, 0))] + w_specs + [pl.BlockSpec((1, tn), lambda j, i: (0, 0))],
        out_specs=pl.BlockSpec((tm, tn), lambda j, i: (i, j)),
        out_shape=jax.ShapeDtypeStruct((m, n_tiles * tn), out_dtype),
        scratch_shapes=[pltpu.VMEM((k, tn), BF16)], compiler_params=_cparams(2),
    )(h, *([w] * n_w), g)


PERM_ROWS = 256


def _perm_matrix(n, dil, inverse=False):
    per = n // dil
    o = lax.broadcasted_iota(I32, (n, n), 1 if inverse else 0)
    s = lax.broadcasted_iota(I32, (n, n), 0 if inverse else 1)
    return (s == (o % per) * dil + o // per).astype(BF16)


def _qkv_body(h_ref, wq_ref, wk_ref, wv_ref, gq_ref, gk_ref, q_ref, kv_ref, tok_ref, w_scr, *, dil, scale):
    tm = h_ref.shape[0]
    nh = HEADS_PER_GROUP_A
    gw = GROUP_W_A

    @pl.when(pl.program_id(0) == 0)
    def _():
        for i, w_ref in enumerate((wq_ref, wk_ref, wv_ref)):
            w_scr[:, i * gw:(i + 1) * gw] = w_ref[...].astype(BF16)

    acc = jnp.dot(h_ref[...], w_scr[...], preferred_element_type=F32)
    q = _headnorm(acc[:, :gw], gq_ref[...], HEAD_DIM_A, scale)
    k = _headnorm(acc[:, gw:2 * gw], gk_ref[...], HEAD_DIM_A, 1.0)
    v = acc[:, 2 * gw:]
    for i in range(nh):
        tok_ref[pl.ds(i, tm, stride=2 * nh), :] = k[:, i * HEAD_DIM_A:(i + 1) * HEAD_DIM_A]
        tok_ref[pl.ds(nh + i, tm, stride=2 * nh), :] = v[:, i * HEAD_DIM_A:(i + 1) * HEAD_DIM_A]

    if dil == 1:
        q_ref[0] = q.astype(q_ref.dtype)
        kv_ref[0, :, :gw] = k.astype(kv_ref.dtype)
        kv_ref[0, :, gw:] = v.astype(kv_ref.dtype)
        return
    sub = min(tm, PERM_ROWS)
    per = sub // dil
    pm = _perm_matrix(sub, dil)
    qkv = jnp.concatenate([q, k, v], axis=1).astype(BF16)
    for t in range(tm // sub):
        yp = jnp.dot(pm, qkv[t * sub:(t + 1) * sub], preferred_element_type=F32).astype(BF16)
        for r in range(dil):
            q_ref[r, t * per:(t + 1) * per, :] = yp[r * per:(r + 1) * per, :gw]
            kv_ref[r, t * per:(t + 1) * per, :] = yp[r * per:(r + 1) * per, gw:]


def _proj_qkv(h, w, sec, g, g_qa, g_ka, n_batch, seq, tm, dil, act_dtype=BF16):
    n, k = h.shape
    ln = seq // dil
    per = tm // dil
    mt = seq // tm
    assert seq % tm == 0
    assert dil == 1 or (act_dtype == BF16 and tm % PERM_ROWS == 0 and (PERM_ROWS // dil) % 16 == 0)
    tile = lambda v: jnp.tile(v.astype(F32).reshape(1, -1), (1, GROUP_W_A // v.shape[-1]))
    assert W_COLS == GROUP_W_A
    wspec = lambda blk: pl.BlockSpec((k, W_COLS), lambda m: (0, blk))
    q, kv, tok = pl.pallas_call(
        functools.partial(_qkv_body, dil=dil, scale=HEAD_DIM_A ** -0.5), name=f"proj_qkv{g}", grid=(n // tm,),
        in_specs=[pl.BlockSpec((tm, k), lambda m: (m, 0)),
                  wspec(sec["q"] + g), wspec(sec["k"] + g), wspec(sec["v"] + g),
                  pl.BlockSpec((1, GROUP_W_A), lambda m: (0, 0)), pl.BlockSpec((1, GROUP_W_A), lambda m: (0, 0))],
        out_specs=[pl.BlockSpec((None, dil, per, GROUP_W_A), lambda m: (m // mt, 0, m % mt, 0)),
                   pl.BlockSpec((None, dil, per, 2 * GROUP_W_A), lambda m: (m // mt, 0, m % mt, 0)),
                   pl.BlockSpec((tm * 2 * HEADS_PER_GROUP_A, LANES), lambda m: (m, 0))],
        out_shape=[jax.ShapeDtypeStruct((n_batch, dil, ln, GROUP_W_A), act_dtype),
                   jax.ShapeDtypeStruct((n_batch, dil, ln, 2 * GROUP_W_A), act_dtype),
                   jax.ShapeDtypeStruct((n * 2 * HEADS_PER_GROUP_A, LANES), F32)],
        scratch_shapes=[pltpu.VMEM((k, 3 * W_COLS), BF16)], compiler_params=_cparams(1),
    )(h, w, w, w, tile(g_qa), tile(g_ka))
    return q.reshape(n, GROUP_W_A), kv.reshape(n, 2 * GROUP_W_A), tok


def _bias_from_table(tab_ref, bidx, col):
    acc = jnp.zeros(bidx.shape, F32)
    for kb in range(N_BUCKETS):
        acc = jnp.where(bidx == kb, tab_ref[kb, col], acc)
    return acc


def _split_hi_lo(x):
    hi = x.astype(BF16)
    return hi, (x - hi.astype(F32)).astype(BF16)


ATTN_P_BLOCKS = 4


def _attn_p_body(tab_ref, bidx_ref, q_ref, kp_ref, kc_ref, vp_ref, vc_ref, o_ref, bias_scr, *, g, nb):
    i = pl.program_id(0)

    @pl.when(i == 0)
    def _():
        bidx = bidx_ref[...]
        for h in range(HEADS_PER_GROUP_A):
            bias_scr[h] = _bias_from_table(tab_ref, bidx, g * HEADS_PER_GROUP_A + h)

    blk = kp_ref.shape[0]
    k_all = jnp.concatenate([kp_ref[...], kc_ref[...]], axis=0)
    v_all = jnp.concatenate([vp_ref[...], vc_ref[...]], axis=0)
    row = lax.broadcasted_iota(I32, (blk, 2 * blk), 0)
    col = lax.broadcasted_iota(I32, (blk, 2 * blk), 1)
    dist = blk + row - col
    band = (dist >= 0) & (dist <= blk)
    lane = lax.broadcasted_iota(I32, (blk, LANES), 1)
    for u in range(ATTN_P_BLOCKS):
        has_prev = ((i * ATTN_P_BLOCKS + u) % nb) > 0
        valid = band & (has_prev | (col >= blk))
        q = q_ref[u * blk:(u + 1) * blk, :]
        k = k_all[u * blk:(u + 2) * blk]
        v = v_all[u * blk:(u + 2) * blk]
        lse_blk = jnp.zeros((blk, LANES), F32)
        outs = []
        for h in range(HEADS_PER_GROUP_A):
            hs = slice(h * HEAD_DIM_A, (h + 1) * HEAD_DIM_A)
            s = lax.dot_general(q[:, hs], k[:, hs], (((1,), (1,)), ((), ())), preferred_element_type=F32)
            s = jnp.where(valid, s + bias_scr[h], NEG_INF)
            m = jnp.max(s, axis=-1, keepdims=True)
            p = jnp.exp(s - m)
            l = jnp.sum(p, axis=-1, keepdims=True)
            o = jnp.dot(p.astype(BF16), v[:, hs], preferred_element_type=F32) / l
            outs.append(o)
            lse_blk = jnp.where(lane == h, m + jnp.log(l), lse_blk)
        lse_hi, lse_lo = _split_hi_lo(lse_blk)
        o_ref[u * blk:(u + 1) * blk, :] = jnp.concatenate(
            [jnp.concatenate(outs, axis=1).astype(BF16), lse_hi, lse_lo], axis=1)


O_EXT_W = GROUP_W_A + 2 * LANES


def _attn_prompt(q, kv, rel_bias, g, n_batch, seq):
    win, dil = DIL_GROUPS[g]
    blk = win // dil
    nb = seq // dil // blk
    n = n_batch * seq
    a = jnp.arange(blk)[:, None]
    c = jnp.arange(2 * blk)[None, :]
    bidx = _rel_bucket(jnp.maximum(blk + a - c, 0) * dil)

    nu = ATTN_P_BLOCKS
    assert (n // blk) % nu == 0 and (nb % nu == 0 or nb == 1)

    def prev(i):
        return jnp.maximum(i * nu - 1, 0)

    return pl.pallas_call(
        functools.partial(_attn_p_body, g=g, nb=nb), name=f"attn_p{g}", grid=(n // blk // nu,),
        in_specs=[
            pl.BlockSpec(memory_space=pltpu.SMEM),
            pl.BlockSpec((blk, 2 * blk), lambda i: (0, 0)),
            pl.BlockSpec((nu * blk, GROUP_W_A), lambda i: (i, 0)),
            pl.BlockSpec((blk, GROUP_W_A), lambda i: (prev(i), 0)),
            pl.BlockSpec((nu * blk, GROUP_W_A), lambda i: (i, 0)),
            pl.BlockSpec((blk, GROUP_W_A), lambda i: (prev(i), 1)),
            pl.BlockSpec((nu * blk, GROUP_W_A), lambda i: (i, 1)),
        ],
        out_specs=pl.BlockSpec((nu * blk, O_EXT_W), lambda i: (i, 0)),
        out_shape=jax.ShapeDtypeStruct((n, O_EXT_W), BF16),
        scratch_shapes=[pltpu.VMEM((HEADS_PER_GROUP_A, blk, 2 * blk), F32)],
        compiler_params=_cparams(1),
    )(rel_bias, bidx, q, kv, kv, kv, kv)


def _attn_s_body(tab_ref, bidx_ref, q_ref, kvn_ref, tokn_ref, cache_hbm, o_ref, newc_hbm,
                 raw, kv_scr, bias_scr, sem_in, sem_out, *, g, dil, lb, n_req):
    t_new = q_ref.shape[0]
    nh = HEADS_PER_GROUP_A
    n_rows = nh * t_new
    n_keys = kv_scr.shape[0]
    n_col = 2 * nh
    lb8 = lb * n_col
    n = pl.program_id(0)
    nbuf = raw.shape[0]
    ahead = nbuf // 2
    slot = n % nbuf

    def fetch(req):
        s = req % nbuf
        return pltpu.make_async_copy(cache_hbm.at[req], raw.at[s, pl.ds(0, lb8)], sem_in.at[s])

    def flush(req):
        s = req % nbuf
        return pltpu.make_async_copy(raw.at[s, pl.ds(t_new * n_col, lb8)], newc_hbm.at[req], sem_out.at[s])

    @pl.when(n == 0)
    def _():
        kv_scr[lb:, :] = jnp.zeros((n_keys - lb, kv_scr.shape[1]), BF16)
        bidx = bidx_ref[...]
        for h in range(nh):
            bias_scr[h * t_new:(h + 1) * t_new, :] = _bias_from_table(tab_ref, bidx, g * nh + h)
        for r in range(ahead):
            fetch(r).start()

    @pl.when(n >= nbuf - ahead)
    def _():
        flush(n - (nbuf - ahead)).wait()

    @pl.when(n + ahead < n_req)
    def _():
        fetch(n + ahead).start()

    fetch(n).wait()
    raw[slot, lb8:lb8 + t_new * n_col, :] = tokn_ref[...]
    flush(n).start()
    new = kvn_ref[...]

    rows_per = min(lb, 256)
    for c in range(n_col):
        for r0 in range(0, lb, rows_per):
            kv_scr[r0:r0 + rows_per, c * LANES:(c + 1) * LANES] = raw[
                slot, pl.ds(r0 * n_col + c, rows_per, stride=n_col), :].astype(BF16)
    kv_scr[lb:lb + 2 * t_new, :] = jnp.concatenate([new, jnp.zeros_like(new)], axis=0).astype(BF16)

    q = q_ref[...]
    qt = jnp.concatenate([q] * nh, axis=0)
    rr = lax.broadcasted_iota(I32, (n_rows, GROUP_W_A), 0)
    cc = lax.broadcasted_iota(I32, (n_rows, GROUP_W_A), 1)
    qbd = jnp.where(cc // HEAD_DIM_A == rr // t_new, qt, 0.0).astype(BF16)

    s = lax.dot_general(qbd, kv_scr[:, :GROUP_W_A], (((1,), (1,)), ((), ())), preferred_element_type=F32)
    row = lax.broadcasted_iota(I32, (n_rows, n_keys), 0)
    col = lax.broadcasted_iota(I32, (n_rows, n_keys), 1)
    delta = lb + (row & (t_new - 1)) - col
    valid = (delta >= 0) & (delta <= lb) & ((delta & (dil - 1)) == 0)
    s = jnp.where(valid, s + bias_scr[...], NEG_INF)
    m = jnp.max(s, axis=-1, keepdims=True)
    p = jnp.exp(s - m)
    l = jnp.sum(p, axis=-1, keepdims=True)
    o = jnp.dot(p.astype(BF16), kv_scr[:, GROUP_W_A:], preferred_element_type=F32) / l
    lse = m + jnp.log(l)
    lane = lax.broadcasted_iota(I32, (t_new, LANES), 1)
    lse_blk = jnp.zeros((t_new, LANES), F32)
    outs = []
    for h in range(nh):
        outs.append(o[h * t_new:(h + 1) * t_new, h * HEAD_DIM_A:(h + 1) * HEAD_DIM_A])
        lse_blk = jnp.where(lane == h, lse[h * t_new:(h + 1) * t_new, :], lse_blk)
    o_ref[...] = jnp.concatenate(outs + [lse_blk, jnp.zeros_like(lse_blk)], axis=1)

    @pl.when(n == n_req - 1)
    def _():
        for back in range(nbuf - ahead - 1, -1, -1):
            flush(n - back).wait()


ATTN_S_BUFS = 4


def _attn_sample(q, kv_new, tok_new, cache, rel_bias, g, t_new):
    win, dil = DIL_GROUPS[g]
    n_col = 2 * HEADS_PER_GROUP_A
    n_req, lb = cache.shape[0], cache.shape[1] // n_col
    assert t_new & (t_new - 1) == 0 and dil & (dil - 1) == 0 and lb == win
    n_keys = lb + LANES
    t = jnp.arange(t_new)[:, None]
    c = jnp.arange(n_keys)[None, :]
    bidx = _rel_bucket(jnp.clip(lb + t - c, 0, lb))
    return pl.pallas_call(
        functools.partial(_attn_s_body, g=g, dil=dil, lb=lb, n_req=n_req), name=f"attn_s{g}", grid=(n_req,),
        in_specs=[
            pl.BlockSpec(memory_space=pltpu.SMEM),
            pl.BlockSpec((t_new, n_keys), lambda n: (0, 0)),
            pl.BlockSpec((t_new, GROUP_W_A), lambda n: (n, 0)),
            pl.BlockSpec((t_new, 2 * GROUP_W_A), lambda n: (n, 0)),
            pl.BlockSpec((t_new * n_col, LANES), lambda n: (n, 0)),
            pl.BlockSpec(memory_space=pl.ANY),
        ],
        out_specs=[pl.BlockSpec((t_new, O_EXT_W), lambda n: (n, 0)), pl.BlockSpec(memory_space=pl.ANY)],
        out_shape=[jax.ShapeDtypeStruct((n_req * t_new, O_EXT_W), F32), jax.ShapeDtypeStruct(cache.shape, F32)],
        scratch_shapes=[pltpu.VMEM((ATTN_S_BUFS, (lb + t_new) * n_col, LANES), F32),
                        pltpu.VMEM((n_keys, 2 * GROUP_W_A), BF16),
                        pltpu.VMEM((HEADS_PER_GROUP_A * t_new, n_keys), F32),
                        pltpu.SemaphoreType.DMA((ATTN_S_BUFS,)), pltpu.SemaphoreType.DMA((ATTN_S_BUFS,))],
        compiler_params=_cparams(1),
    )(rel_bias, bidx, q, kv_new, tok_new, cache)


def _gmlp_body(u_ref, v_ref, w_ref, b_ref, o_ref, *, period):
    c = w_ref.shape[1]
    nch = u_ref.shape[0] // c
    gd = u_ref.shape[1] // N_GROUPS_B
    i = lax.broadcasted_iota(I32, (c, c), 0)
    j = lax.broadcasted_iota(I32, (c, c), 1)
    mask = (j <= i) & ((i // period) == (j // period))
    b = b_ref[...]
    for g in range(N_GROUPS_B):
        gs = slice(g * gd, (g + 1) * gd)
        wg = jnp.where(mask, w_ref[g], 0.0).astype(BF16)
        vg = [v_ref[ch * c:(ch + 1) * c, gs].astype(BF16) for ch in range(nch)]
        vg = vg[0] if nch == 1 else jnp.concatenate(vg, axis=1)
        sg = jnp.dot(wg, vg, preferred_element_type=F32) + b[:, g:g + 1]
        for ch in range(nch):
            u = u_ref[ch * c:(ch + 1) * c, gs].astype(F32)
            o_ref[ch * c:(ch + 1) * c, gs] = (u * sg[:, ch * gd:(ch + 1) * gd]).astype(o_ref.dtype)


def _gmlp(u, v, w, b, tm, period):
    m, wb = u.shape
    c = w.shape[1]
    return pl.pallas_call(
        functools.partial(_gmlp_body, period=period), name="gmlp", grid=(m // tm,),
        in_specs=[pl.BlockSpec((tm, wb), lambda i: (i, 0)), pl.BlockSpec((tm, wb), lambda i: (i, 0)),
                  pl.BlockSpec((N_GROUPS_B, c, c), lambda i: (0, 0, 0)),
                  pl.BlockSpec((c, N_GROUPS_B), lambda i: (0, 0))],
        out_specs=pl.BlockSpec((tm, wb), lambda i: (i, 0)),
        out_shape=jax.ShapeDtypeStruct((m, wb), BF16), compiler_params=_cparams(1),
    )(u, v, w, b)


def _gmlp_new_body(u_ref, v_ref, w_ref, b_ref, o_ref):
    t = w_ref.shape[0]
    n = u_ref.shape[0] // t
    width = u_ref.shape[1]
    v = v_ref[...].astype(F32).reshape(n, t, width)
    sg = jnp.broadcast_to(b_ref[...][None], (n, t, width))
    p_row = lax.broadcasted_iota(I32, (t, width), 0)
    for q in range(t):
        sg = sg + jnp.where(p_row >= q, w_ref[q], 0.0)[None] * v[:, q:q + 1, :]
    o_ref[...] = (u_ref[...].astype(F32).reshape(n, t, width) * sg).reshape(n * t, width).astype(o_ref.dtype)


def _gmlp_new(u, v, w_s, b_s, t):
    m, wb = u.shape
    gd = wb // N_GROUPS_B
    wq = jnp.repeat(jnp.transpose(w_s[:, :t, :t], (2, 1, 0)), gd, axis=-1)
    bq = jnp.repeat(b_s[:, :t].T, gd, axis=-1)
    whole = lambda a: pl.BlockSpec(a.shape, lambda i: (0,) * a.ndim)
    return pl.pallas_call(
        _gmlp_new_body, name="gmlp_new", grid=(1,),
        in_specs=[whole(u), whole(v), whole(wq), whole(bq)], out_specs=whole(u),
        out_shape=jax.ShapeDtypeStruct((m, wb), BF16), compiler_params=_cparams(1),
    )(u, v, wq, bq)


def _memattn_body(q_ref, kv_ref, o_ref, *, tiles):
    wm = N_HEADS_M * HEAD_DIM_M
    n_ct = HEAD_DIM_M // LANES
    per_key = 2 * n_ct * N_HEADS_M
    reqs = kv_ref.shape[0]
    tq = q_ref.shape[0] // reqs

    for r in range(reqs):
        kvr = kv_ref.at[r]

        def head_slab(kv, h):
            if not tiles:
                return kvr[:, kv * wm + h * HEAD_DIM_M:kv * wm + (h + 1) * HEAD_DIM_M].astype(BF16)
            n_keys = kvr.shape[0] // per_key
            parts = [kvr[pl.ds((kv * n_ct + ct) * N_HEADS_M + h, n_keys, stride=per_key), :] for ct in range(n_ct)]
            return jnp.concatenate(parts, axis=1).astype(BF16)

        q = q_ref[r * tq:(r + 1) * tq, :].astype(BF16)
        outs = []
        for h in range(N_HEADS_M):
            hs = slice(h * HEAD_DIM_M, (h + 1) * HEAD_DIM_M)
            k = head_slab(0, h)
            v = head_slab(1, h)
            s = lax.dot_general(q[:, hs], k, (((1,), (1,)), ((), ())), preferred_element_type=F32)
            m = jnp.max(s, axis=-1, keepdims=True)
            p = jnp.exp(s - m)
            l = jnp.sum(p, axis=-1, keepdims=True)
            outs.append(jnp.dot(p.astype(BF16), v, preferred_element_type=F32) / l)
        o_ref[r * tq:(r + 1) * tq, :] = jnp.concatenate(outs, axis=1).astype(o_ref.dtype)


def _memattn(q, kv, tq, out_dtype, tiles=False, reqs=1):
    m, wm = q.shape
    n = kv.shape[0]
    per = m // n // tq
    assert reqs == 1 or (per == 1 and n % reqs == 0)
    kv_spec = pl.BlockSpec((reqs,) + kv.shape[1:], lambda i: (i // per,) + (0,) * (kv.ndim - 1))
    tq = tq * reqs
    return pl.pallas_call(
        functools.partial(_memattn_body, tiles=tiles), name="memattn", grid=(m // tq,),
        in_specs=[pl.BlockSpec((tq, wm), lambda i: (i, 0)), kv_spec],
        out_specs=pl.BlockSpec((tq, wm), lambda i: (i, 0)),
        out_shape=jax.ShapeDtypeStruct((m, wm), out_dtype), compiler_params=_cparams(1),
    )(q, kv)


def _mix_body(gt_ref, o0_ref, o1_ref, o2_ref, ob_ref, om_ref, wpa_ref, wpb_ref, wpm_ref, z_ref, *, dils):
    d = z_ref.shape[1]
    tm = z_ref.shape[0]
    os_, ls = [], []
    for o_ref, dil in zip((o0_ref, o1_ref, o2_ref), dils):
        if dil == 1:
            x = o_ref[...].astype(F32)
        else:
            rows = jnp.concatenate([o_ref[r] for r in range(dil)], axis=0)
            x = jnp.dot(_perm_matrix(tm, dil, inverse=True), rows, preferred_element_type=F32)
        os_.append(x[:, :GROUP_W_A])
        ls.append(x[:, GROUP_W_A:GROUP_W_A + LANES] + x[:, GROUP_W_A + LANES:])
    l0, l1, l2 = ls
    mx = jnp.maximum(jnp.maximum(l0, l1), l2)
    e0, e1, e2 = jnp.exp(l0 - mx), jnp.exp(l1 - mx), jnp.exp(l2 - mx)
    den = e0 + e1 + e2
    w0, w1, w2 = e0 / den, e1 / den, e2 / den
    cols = []
    for h in range(HEADS_PER_GROUP_A):
        hs = slice(h * HEAD_DIM_A, (h + 1) * HEAD_DIM_A)
        cols.append(w0[:, h:h + 1] * os_[0][:, hs] + w1[:, h:h + 1] * os_[1][:, hs] + w2[:, h:h + 1] * os_[2][:, hs])
    oa = jnp.concatenate(cols, axis=1).astype(BF16)
    pa = jnp.dot(oa, wpa_ref[...], preferred_element_type=F32)
    pb = jnp.dot(ob_ref[...].astype(BF16), wpb_ref[...], preferred_element_type=F32)
    pm = jnp.dot(om_ref[...].astype(BF16), wpm_ref[...], preferred_element_type=F32)
    z = (gt_ref[:, 0:d].astype(F32) * pa + gt_ref[:, d:2 * d].astype(F32) * pb
         + gt_ref[:, 2 * d:3 * d].astype(F32) * pm)
    z_ref[...] = z.astype(z_ref.dtype)


def _mix(gates, o_list, dils, seq, ob, om, wpa, wpb, wpm, tm):
    m = gates.shape[0]
    d = wpa.shape[1]
    mt = seq // tm if any(dl > 1 for dl in dils) else 1

    def rows(width):
        return pl.BlockSpec((tm, width), lambda i: (i, 0))

    def whole(a):
        return pl.BlockSpec(a.shape, lambda i: (0, 0))

    o_specs, o_args = [], []
    for o, dil in zip(o_list, dils):
        if dil == 1:
            o_specs.append(rows(O_EXT_W))
            o_args.append(o)
        else:
            assert seq % tm == 0 and (tm // dil) % 16 == 0
            o_specs.append(pl.BlockSpec((None, dil, tm // dil, O_EXT_W), lambda i: (i // mt, 0, i % mt, 0)))
            o_args.append(o.reshape(m // seq, dil, seq // dil, O_EXT_W))
    return pl.pallas_call(
        functools.partial(_mix_body, dils=tuple(dils)), name="mix", grid=(m // tm,),
        in_specs=[rows(gates.shape[1])] + o_specs
                 + [rows(ob.shape[1]), rows(om.shape[1]), whole(wpa), whole(wpb), whole(wpm)],
        out_specs=rows(d),
        out_shape=jax.ShapeDtypeStruct((m, d), BF16), compiler_params=_cparams(1),
    )(gates, *o_args, ob, om, wpa, wpb, wpm)


def _route(logits):
    lane = lax.broadcasted_iota(I32, logits.shape, 1)
    lane_f = lane.astype(F32)
    is_g = lane < N_EXPERT_GROUPS
    gmax = jnp.max(jnp.where(is_g, logits, -jnp.inf), axis=1, keepdims=True)
    gsel = jnp.min(jnp.where(is_g & (logits == gmax), lane_f, float(LANES)), axis=1, keepdims=True).astype(I32)
    gden = jnp.sum(jnp.where(is_g, jnp.exp(logits - gmax), 0.0), axis=1, keepdims=True)
    pg = 1.0 / gden
    e_lane = lane - N_EXPERT_GROUPS
    in_grp = (e_lane >= 0) & (e_lane < N_EXPERTS) & ((e_lane // EXPERTS_PER_GROUP) == gsel)
    m1 = jnp.max(jnp.where(in_grp, logits, -jnp.inf), axis=1, keepdims=True)
    i1 = jnp.min(jnp.where(in_grp & (logits == m1), lane_f, float(LANES)), axis=1, keepdims=True).astype(I32)
    rest = in_grp & (lane != i1)
    m2 = jnp.max(jnp.where(rest, logits, -jnp.inf), axis=1, keepdims=True)
    i2 = jnp.min(jnp.where(rest & (logits == m2), lane_f, float(LANES)), axis=1, keepdims=True).astype(I32)
    e2 = jnp.exp(m2 - m1)
    w1 = pg / (1.0 + e2)
    w2 = pg * e2 / (1.0 + e2)
    eid = jnp.where(lane == 0, i1 - N_EXPERT_GROUPS, jnp.where(lane == 1, i2 - N_EXPERT_GROUPS, 0))
    wts = jnp.where(lane == 0, w1, jnp.where(lane == 1, w2, 0.0))
    return eid, wts


def _resid_body(x_ref, z_ref, wo_ref, gf_ref, wr_ref, br_ref, xmid_ref, hpk_ref, eid_ref, wts_ref):
    xm = x_ref[...] + jnp.dot(z_ref[...], wo_ref[...], preferred_element_type=F32)
    xmid_ref[...] = xm
    hf = _rms(xm, gf_ref[...])
    wr = wr_ref[...]
    wr_hi, wr_lo = _split_hi_lo(wr)
    hf_hi, hf_lo = _split_hi_lo(hf)
    t = jnp.dot(hf_hi, jnp.concatenate([wr_hi, wr_lo], axis=1), preferred_element_type=F32)
    logits = (t[:, :LANES] + t[:, LANES:] + jnp.dot(hf_lo, wr_hi, preferred_element_type=F32) + br_ref[...])
    eid, wts = _route(logits)
    eid_ref[...] = eid
    wts_ref[...] = wts
    _store_token_tiles(hpk_ref, hf)


def _resid(x2d, z, wo, g_ffn, w_r, b_r, tm):
    m, d = x2d.shape

    def rows(width):
        return pl.BlockSpec((tm, width), lambda i: (i, 0))

    def whole(a):
        return pl.BlockSpec(a.shape, lambda i: (0, 0))

    gf = g_ffn.reshape(1, d)
    nt = d // 2 // LANES
    return pl.pallas_call(
        _resid_body, name="resid", grid=(m // tm,),
        in_specs=[rows(d), rows(d), whole(wo), whole(gf), whole(w_r), whole(b_r)],
        out_specs=[rows(d), pl.BlockSpec((tm * nt, LANES), lambda i: (i, 0)), rows(LANES), rows(LANES)],
        out_shape=[jax.ShapeDtypeStruct((m, d), F32), jax.ShapeDtypeStruct((m * nt, LANES), U32),
                   jax.ShapeDtypeStruct((m, LANES), I32), jax.ShapeDtypeStruct((m, LANES), F32)],
        compiler_params=_cparams(1),
    )(x2d, z, wo, gf, w_r, b_r)


TOKEN_TILE = 8
GATHER_UNROLL = 8


def _gather_rows(idx_at, src_hbm, dst, sem, n):
    def body(i, carry):
        src = pl.multiple_of(idx_at(i), TOKEN_TILE)
        dst_row = pl.multiple_of(i * TOKEN_TILE, TOKEN_TILE)
        pltpu.make_async_copy(src_hbm.at[pl.ds(src, TOKEN_TILE)], dst.at[pl.ds(dst_row, TOKEN_TILE)], sem).start()
        return carry
    lax.fori_loop(0, n, body, 0, unroll=GATHER_UNROLL)


FFN_CHUNKS = 8
FFN_BUFS = 3


def _ffn_body(blk_e_ref, nused_ref, eord_ref, enext_ref, idx_hbm, h_hbm, wg_hbm, wu_hbm, wd_hbm, y_ref,
              idx_s, isem, buf, sem, x_scr, a_scr, b_scr, hm_scr, wg_f, wu_f, wd_f, wsem, wg_s, wu_s, wd_s):
    j = pl.program_id(0)
    nused = nused_ref[0]
    last = nused - 1
    rows = buf.shape[1] // TOKEN_TILE
    e = blk_e_ref[j]

    def idx_copy(b):
        s = b % FFN_BUFS
        return pltpu.make_async_copy(idx_hbm.at[jnp.minimum(b, last)], idx_s.at[s], isem.at[s])

    def wait_rows(s):
        pltpu.make_async_copy(h_hbm.at[pl.ds(0, rows * TOKEN_TILE)], buf.at[s], sem.at[s]).wait()
    new_expert = (j == 0) | (e != blk_e_ref[jnp.maximum(j - 1, 0)])
    ws = eord_ref[j] % 2

    def fetch_w(expert, s):
        return [pltpu.make_async_copy(src.at[expert], dst.at[s], wsem.at[s, i])
                for i, (src, dst) in enumerate(((wg_hbm, wg_f), (wu_hbm, wu_f), (wd_hbm, wd_f)))]

    @pl.when(j == 0)
    def _():
        for c in fetch_w(e, ws):
            c.start()
        for b in range(FFN_BUFS):
            idx_copy(b).start()
        for b in range(FFN_BUFS - 1):
            idx_copy(b).wait()
            _gather_rows(lambda i, b=b: idx_s[b, 0, i], h_hbm, buf.at[b], sem.at[b], rows)

    @pl.when(new_expert)
    def _():
        nxt = enext_ref[j]

        @pl.when(nxt >= 0)
        def _():
            for c in fetch_w(nxt, 1 - ws):
                c.start()

        for c in fetch_w(e, ws):
            c.wait()
        wg_s[...] = wg_f[ws].astype(BF16)
        wu_s[...] = wu_f[ws].astype(BF16)
        wd_s[...] = wd_f[ws].astype(BF16)

    @pl.when(j < nused)
    def _():
        slot = j % FFN_BUFS
        s2 = (j + 2) % FFN_BUFS
        per = rows // FFN_CHUNKS
        idx_copy(j + 2).wait()

        @pl.when(j < last)
        def _():
            idx_copy(j + 3).start()

        def issue(c):
            for i in range(c * per, (c + 1) * per):
                src = pl.multiple_of(idx_s[s2, 0, i], TOKEN_TILE)
                pltpu.make_async_copy(h_hbm.at[pl.ds(src, TOKEN_TILE)],
                                      buf.at[s2, pl.ds(i * TOKEN_TILE, TOKEN_TILE)], sem.at[s2]).start()

        wait_rows(slot)
        x_scr[...] = _load_token_tiles(buf, slot, 0, rows, TOKEN_TILE).astype(BF16)
        dn = a_scr.shape[1] // 2
        for c in range(2):
            issue(c)
            a_scr[:, c * dn:(c + 1) * dn] = jnp.dot(x_scr[...], wg_s[:, c * dn:(c + 1) * dn],
                                                     preferred_element_type=F32)
        for c in range(2):
            issue(2 + c)
            b_scr[:, c * dn:(c + 1) * dn] = jnp.dot(x_scr[...], wu_s[:, c * dn:(c + 1) * dn],
                                                     preferred_element_type=F32)
        a = a_scr[...]
        hm_scr[...] = (a * jax.nn.sigmoid(a) * b_scr[...]).astype(BF16)
        half = wd_s.shape[1] // 2
        n_dc = FFN_CHUNKS - 4
        dq = half // n_dc
        for c in range(n_dc):
            issue(4 + c)
            lo = jnp.dot(hm_scr[...], wd_s[:, c * dq:(c + 1) * dq], preferred_element_type=F32)
            hi = jnp.dot(hm_scr[...], wd_s[:, half + c * dq:half + (c + 1) * dq], preferred_element_type=F32)
            w = _pack_bf16_pair(jnp.concatenate([lo, hi], axis=1))
            for t in range(dq // LANES):
                y_ref[pl.ds(c * (dq // LANES) + t, rows, stride=TOKEN_TILE), :] = w[:, t * LANES:(t + 1) * LANES]

        @pl.when(j == last)
        def _():
            wait_rows((j + 1) % FFN_BUFS)
            wait_rows(s2)

    @pl.when(j >= nused)
    def _():
        y_ref[...] = jnp.zeros(y_ref.shape, y_ref.dtype)


def _ffn(hpk, src_tok, blk_e, nused, eord, enext, wg, wu, wd):
    n_blocks = blk_e.shape[0]
    rows = MOE_ROWS
    tiles = rows * TOKEN_TILE
    d, de = wg.shape[1], wg.shape[2]
    assert d // 2 == TOKEN_TILE * LANES and rows % FFN_CHUNKS == 0 and (d // 2) % ((FFN_CHUNKS - 4) * LANES) == 0
    assert n_blocks >= N_EXPERTS + FFN_BUFS
    idx = (src_tok * TOKEN_TILE).reshape(n_blocks, 1, rows)
    hbm = pl.BlockSpec(memory_space=pl.ANY)
    grid_spec = pltpu.PrefetchScalarGridSpec(
        num_scalar_prefetch=4, grid=(n_blocks,),
        in_specs=[hbm, hbm, hbm, hbm, hbm],
        out_specs=pl.BlockSpec((tiles, LANES), lambda j, *_: (j, 0)),
        scratch_shapes=[pltpu.SMEM((FFN_BUFS, 1, rows), I32), pltpu.SemaphoreType.DMA((FFN_BUFS,)),
                        pltpu.VMEM((FFN_BUFS, tiles, LANES), U32), pltpu.SemaphoreType.DMA((FFN_BUFS,)),
                        pltpu.VMEM((rows, d), BF16), pltpu.VMEM((rows, de), F32), pltpu.VMEM((rows, de), F32),
                        pltpu.VMEM((rows, de), BF16),
                        pltpu.VMEM((2, d, de), wg.dtype), pltpu.VMEM((2, d, de), wu.dtype),
                        pltpu.VMEM((2, de, d), wd.dtype), pltpu.SemaphoreType.DMA((2, 3)),
                        pltpu.VMEM((d, de), BF16), pltpu.VMEM((d, de), BF16), pltpu.VMEM((de, d), BF16)],
    )
    return pl.pallas_call(
        _ffn_body, name="ffn", grid_spec=grid_spec,
        out_shape=jax.ShapeDtypeStruct((n_blocks * tiles, LANES), U32), compiler_params=_cparams(1),
    )(blk_e, nused, eord, enext, idx, hpk, wg, wu, wd)


def _combine_body(idx_cur_ref, idx_nxt_ref, x_ref, w_ref, y_hbm, o_ref, buf, sem, *, n):
    i = pl.program_id(0)
    slot = i % 2
    rows = buf.shape[1] // TOKEN_TILE
    tm = rows // TOP_K

    @pl.when(i == 0)
    def _():
        _gather_rows(lambda r: idx_cur_ref[0, 0, r], y_hbm, buf.at[0], sem.at[0], rows)

    if n > 1:
        @pl.when(i + 1 < n)
        def _():
            _gather_rows(lambda r: idx_nxt_ref[0, 0, r], y_hbm, buf.at[1 - slot], sem.at[1 - slot], rows)

    pltpu.make_async_copy(y_hbm.at[pl.ds(0, rows * TOKEN_TILE)], buf.at[slot], sem.at[slot]).wait()
    w = w_ref[...]
    y0 = _load_token_tiles(buf, slot, 0, tm, TOKEN_TILE)
    y1 = _load_token_tiles(buf, slot, tm, tm, TOKEN_TILE)
    o_ref[...] = x_ref[...] + (w[:, 0:1] * y0 + w[:, 1:2] * y1)


def _combine(xmid, wts, dest, ypk, tm):
    m, d = xmid.shape
    nt = m // tm
    assert d // 2 == TOKEN_TILE * LANES
    idx = (dest * TOKEN_TILE).reshape(nt, tm, TOP_K).transpose(0, 2, 1).reshape(nt, 1, TOP_K * tm)

    def idx_spec(ahead):
        return pl.BlockSpec((1, 1, TOP_K * tm), lambda i: (jnp.minimum(i + ahead, nt - 1), 0, 0),
                            memory_space=pltpu.SMEM)

    return pl.pallas_call(
        functools.partial(_combine_body, n=nt), name="combine", grid=(nt,),
        in_specs=[idx_spec(0), idx_spec(1), pl.BlockSpec((tm, d), lambda i: (i, 0)),
                  pl.BlockSpec((tm, LANES), lambda i: (i, 0)), pl.BlockSpec(memory_space=pl.ANY)],
        out_specs=pl.BlockSpec((tm, d), lambda i: (i, 0)),
        out_shape=jax.ShapeDtypeStruct((m, d), F32),
        scratch_shapes=[pltpu.VMEM((2, TOP_K * tm * TOKEN_TILE, LANES), U32), pltpu.SemaphoreType.DMA((2,))],
        compiler_params=_cparams(1),
    )(idx, idx, xmid, wts, ypk)


def _moe_plan(eid2):
    n_slot = eid2.shape[0] * TOP_K
    eid = eid2.reshape(n_slot)
    onehot = (eid[:, None] == jnp.arange(N_EXPERTS, dtype=I32)[None, :]).astype(I32)
    csum = jnp.cumsum(onehot, axis=0)
    rank = jnp.take_along_axis(csum, eid[:, None], axis=1)[:, 0] - 1
    counts = csum[-1]
    padded = (counts + MOE_ROWS - 1) // MOE_ROWS * MOE_ROWS
    pend = jnp.cumsum(padded)
    pstart = pend - padded
    dest = pstart[eid] + rank
    n_blocks = -(-n_slot // MOE_ROWS) + N_EXPERTS
    nused = (pend[-1] // MOE_ROWS).astype(I32)
    blk = jnp.minimum(jnp.arange(n_blocks, dtype=I32), nused - 1)
    blk_e = jnp.minimum(jnp.searchsorted(pend, blk * MOE_ROWS, side="right"), N_EXPERTS - 1).astype(I32)
    n_rows = n_blocks * MOE_ROWS
    src_tok = (jnp.arange(n_rows, dtype=I32) % (n_slot // TOP_K)).at[dest].set(
        jnp.arange(n_slot, dtype=I32) // TOP_K, unique_indices=True, mode="promise_in_bounds")
    has_rows = counts > 0
    ord_e = jnp.cumsum(has_rows.astype(I32)) - 1
    ids = jnp.arange(N_EXPERTS, dtype=I32)
    later = has_rows[None, :] & (ids[None, :] > ids[:, None])
    next_e = jnp.where(later.any(axis=1), jnp.argmax(later, axis=1), -1).astype(I32)
    return dest.astype(I32), src_tok, blk_e, nused.reshape(1), ord_e[blk_e].astype(I32), next_e[blk_e]


def _mixer(x2d, tm, p, attn_fn, gmlp_fn, mem_fn):
    h = _prep(x2d, p["g_mix"], tm)
    qdt = p["q_dtype"]
    wi, sec = p["w_in"], p["sec"]
    o_list, dils, seq = attn_fn(h)
    n_ub = sec["vb"] - sec["u"]
    ub = _proj(h, wi, (sec["u"], n_ub, n_ub, 1), tm, "none", BF16, name="proj_ub")
    n_vb = sec["qm"] - sec["vb"]
    vb = _proj(h, wi, (sec["vb"], n_vb, n_vb, 1), tm, "norm", p["vb_dtype"], p["g_vb"], n_vb * W_COLS,
               name="proj_vb")
    n_qm = sec["gt"] - sec["qm"]
    qm = _proj(h, wi, (sec["qm"], n_qm, n_qm, 1), tm, "norm", qdt, p["g_qm"], HEAD_DIM_M,
               HEAD_DIM_M ** -0.5, name="proj_qm")
    gates = _proj(h, wi, (sec["gt"], 2, 2, (sec["end"] - sec["gt"]) // 2), tm, "sigmoid", BF16, name="proj_gates")
    ob = gmlp_fn(ub, vb)
    om = mem_fn(qm)
    z = _mix(gates, o_list, dils, seq, ob, om, p["w_pa"], p["w_pb"], p["w_pm"], min(tm, 256))
    xmid, hpk, eid, wts = _resid(x2d, z, p["w_o"], p["g_ffn"], p["w_r"], p["b_r"], min(tm, 256))
    return vb, xmid, hpk, eid, wts


def kernel(x_prompt, x_sample, mem_prompt, cache_a0_kv, cache_a1_kv, cache_a2_kv, cache_mem_kv, rel_bias, g_mix, w_in, g_qa, g_ka, w_pa, g_vb, w_s, b_s, w_pb, g_mem, w_mk, w_mv, g_qm, g_km, w_pm, w_o, g_ffn, w_rg, b_rg, w_re, b_re, w_gate, w_up, w_down):
    n_b, seq, d = x_prompt.shape
    n_s, t_s, _ = x_sample.shape
    depth = w_in.shape[0]
    assert depth == 1
    l = 0
    caches = (cache_a0_kv, cache_a1_kv, cache_a2_kv)
    n_g = len(DIL_GROUPS)
    wb = w_pb.shape[1]
    wm = N_HEADS_M * HEAD_DIM_M

    offs = [0, WIDTH_A, 2 * WIDTH_A, 3 * WIDTH_A, 3 * WIDTH_A + wb, 3 * WIDTH_A + 2 * wb,
            3 * WIDTH_A + 2 * wb + wm, w_in.shape[2]]
    assert all(o % W_COLS == 0 for o in offs) and (offs[7] - offs[6]) % (2 * W_COLS) == 0
    sec = dict(zip(("q", "k", "v", "u", "vb", "qm", "gt", "end"), (o // W_COLS for o in offs)))
    p = {
        "g_mix": g_mix[l], "g_qa": g_qa[l], "g_ka": g_ka[l], "g_vb": g_vb[l], "g_qm": g_qm[l], "g_ffn": g_ffn[l],
        "w_in": w_in[l], "sec": sec,
        "w_pa": w_pa[l].astype(BF16), "w_pb": w_pb[l].astype(BF16), "w_pm": w_pm[l].astype(BF16),
        "w_o": w_o[l].astype(BF16),
    }
    n_r = N_EXPERT_GROUPS + N_EXPERTS
    p["w_r"] = jnp.concatenate([w_rg[l], w_re[l], jnp.zeros((d, LANES - n_r), F32)], axis=1)
    p["b_r"] = jnp.concatenate([b_rg[l], b_re[l], jnp.zeros((LANES - n_r,), F32)]).reshape(1, LANES)

    n_mem = mem_prompt.shape[1]
    h_mem = _prep(mem_prompt.reshape(n_b * n_mem, d), g_mem[l], 256)
    w_mkv = jnp.concatenate([w_mk[l], w_mv[l]], axis=1)
    n_mw = wm // W_COLS
    mkv = _proj(h_mem, w_mkv, (0, n_mw, n_mw, 2), 256, "norm_first", F32, g_km[l], HEAD_DIM_M, name="proj_mkv")
    new_mem_p = mkv.reshape(1, n_b, n_mem, 2, N_HEADS_M, HEAD_DIM_M)

    ps = dict(p, q_dtype=F32, vb_dtype=F32)
    m_s = n_s * t_s
    n_col = 2 * HEADS_PER_GROUP_A
    new_s = []

    def attn_sample(h):
        o_list = []
        for g in range(n_g):
            q, kv, tok = _proj_qkv(h, w_in[l], sec, g, g_qa[l], g_ka[l], 1, m_s, m_s, 1, act_dtype=F32)
            cache3 = caches[g][l].reshape(n_s, caches[g].shape[2] * n_col, LANES)
            o, newc = _attn_sample(q, kv, tok, cache3, rel_bias, g, t_s)
            o_list.append(o)
            new_s.append(newc.reshape(caches[g][l:l + 1].shape))
        return o_list, [1] * n_g, t_s

    n_ct = HEAD_DIM_M // LANES
    mem_tiles = (cache_mem_kv[l].reshape(n_s, n_mem, 2, N_HEADS_M, n_ct, LANES).transpose(0, 1, 2, 4, 3, 5)
                 .reshape(n_s, n_mem * 2 * n_ct * N_HEADS_M, LANES))
    vb_s, xmid_s, hpk_s, eid_s, wts_s = _mixer(
        x_sample.reshape(m_s, d), m_s, ps, attn_sample,
        lambda u, v: _gmlp_new(u, v, w_s[l], b_s[l], t_s),
        lambda qm: _memattn(qm, mem_tiles, t_s, F32, tiles=True, reqs=4))
    new_vb_s = vb_s.reshape(1, n_s, t_s, wb)

    pp = dict(p, q_dtype=BF16, vb_dtype=BF16)
    new_p = []

    def attn_prompt(h):
        o_list = []
        for g, (win, dil) in enumerate(DIL_GROUPS):
            q, kv, tok = _proj_qkv(h, w_in[l], sec, g, g_qa[l], g_ka[l], n_b, seq, 512, dil)
            o_list.append(_attn_prompt(q, kv, rel_bias, g, n_b, seq))
            keep = min(win, seq)
            kv5 = tok.reshape(n_b, seq, 2, HEADS_PER_GROUP_A, HEAD_DIM_A)
            new_p.append((kv5 if keep == seq else kv5[:, seq - keep:])[None])
        return o_list, [dl for _, dl in DIL_GROUPS], seq

    _, xmid_p, hpk_p, eid_p, wts_p = _mixer(
        x_prompt.reshape(n_b * seq, d), 512, pp, attn_prompt,
        lambda u, v: _gmlp(u, v, w_s[l], b_s[l].T, 512, CHUNK),
        lambda qm: _memattn(qm, mkv.reshape(n_b, n_mem, 2 * wm), 512, BF16))

    n_p = n_b * seq
    hpk = jnp.concatenate([hpk_p, hpk_s], axis=0)
    eid2 = jnp.concatenate([eid_p[:, :TOP_K], eid_s[:, :TOP_K]], axis=0)
    dest, src_tok, blk_e, nused, eord, enext = _moe_plan(eid2)
    ypk = _ffn(hpk, src_tok, blk_e, nused, eord, enext, w_gate[l], w_up[l], w_down[l])
    dest2 = dest.reshape(-1, TOP_K)
    y_p = _combine(xmid_p, wts_p, dest2[:n_p], ypk, 512)
    y_s = _combine(xmid_s, wts_s, dest2[n_p:], ypk, m_s)

    return (y_p.reshape(n_b, seq, d), y_s.reshape(n_s, t_s, d), new_p[0], new_p[1], new_p[2], new_mem_p,
            new_s[0], new_s[1], new_s[2], new_vb_s)
```

```python
import functools
import math

import jax
import jax.numpy as jnp
from jax import lax
from jax.experimental import pallas as pl
from jax.experimental.pallas import tpu as pltpu

F32 = jnp.float32
BF16 = jnp.bfloat16
I32 = jnp.int32
U32 = jnp.uint32

EPS = 1e-6
NEG_INF = -1e30

HEAD_DIM_A = 128
HEADS_PER_GROUP_A = 4
DIL_GROUPS = ((128, 1), (512, 4), (2048, 16))
GROUP_W_A = HEADS_PER_GROUP_A * HEAD_DIM_A
WIDTH_A = len(DIL_GROUPS) * GROUP_W_A
CHUNK = 128
N_GROUPS_B = 8
N_HEADS_M = 4
HEAD_DIM_M = 256
N_BUCKETS = 32
MAX_EXACT = N_BUCKETS // 2
MAX_DISTANCE = 2048
N_EXPERT_GROUPS = 4
EXPERTS_PER_GROUP = 8
N_EXPERTS = N_EXPERT_GROUPS * EXPERTS_PER_GROUP
TOP_K = 2
LANES = 128
MOE_ROWS = 256
VMEM_LIMIT = 56 * 1024 * 1024


def _cparams(n_grid, vmem=VMEM_LIMIT):
    return pltpu.CompilerParams(dimension_semantics=("arbitrary",) * n_grid, vmem_limit_bytes=vmem)


def _rms(x, g):
    return x * lax.rsqrt(jnp.mean(x * x, axis=-1, keepdims=True) + EPS) * g


def _pack_bf16_pair(x):
    n = x.shape[1] // 2
    lo = lax.bitcast_convert_type(x[:, :n].astype(BF16).astype(F32), U32)
    hi = lax.bitcast_convert_type(x[:, n:].astype(BF16).astype(F32), U32)
    return (hi & jnp.uint32(0xFFFF0000)) | (lo >> 16)


def _unpack_bf16_pair(w):
    lo = lax.bitcast_convert_type(w << 16, F32)
    hi = lax.bitcast_convert_type(w & jnp.uint32(0xFFFF0000), F32)
    return lo, hi


def _store_token_tiles(ref, x):
    w = _pack_bf16_pair(x)
    m, n = w.shape
    nt = n // LANES
    for c in range(nt):
        ref[pl.ds(c, m, stride=nt), :] = w[:, c * LANES:(c + 1) * LANES]


def _load_token_tiles(ref, lead, row0, m, nt):
    los, his = [], []
    view = ref if lead is None else ref.at[lead]
    for c in range(nt):
        lo, hi = _unpack_bf16_pair(view[pl.ds(row0 * nt + c, m, stride=nt), :])
        los.append(lo)
        his.append(hi)
    return jnp.concatenate(los + his, axis=1)


def _rel_bucket(dist):
    d = jnp.maximum(dist, 1).astype(F32)
    large = MAX_EXACT + (jnp.log(d / MAX_EXACT) / math.log(MAX_DISTANCE / MAX_EXACT)
                         * (N_BUCKETS - MAX_EXACT)).astype(I32)
    return jnp.where(dist < MAX_EXACT, dist, jnp.minimum(large, N_BUCKETS - 1)).astype(I32)


def _prep_body(x_ref, g_ref, o_ref):
    o_ref[...] = _rms(x_ref[...], g_ref[...]).astype(o_ref.dtype)


def _prep(x2d, g, tm):
    m, d = x2d.shape
    return pl.pallas_call(
        _prep_body, name="prep", grid=(m // tm,),
        in_specs=[pl.BlockSpec((tm, d), lambda i: (i, 0)), pl.BlockSpec((1, d), lambda i: (0, 0))],
        out_specs=pl.BlockSpec((tm, d), lambda i: (i, 0)),
        out_shape=jax.ShapeDtypeStruct((m, d), BF16), compiler_params=_cparams(1),
    )(x2d, g.reshape(1, d))


def _headnorm(acc, g, hd, scale):
    outs = []
    for j in range(acc.shape[1] // hd):
        sl = acc[:, j * hd:(j + 1) * hd]
        outs.append(sl * lax.rsqrt(jnp.mean(sl * sl, axis=-1, keepdims=True) + EPS))
    y = outs[0] if len(outs) == 1 else jnp.concatenate(outs, axis=1)
    y = y * g
    return y * scale if scale != 1.0 else y


def _proj_body(h_ref, *refs, n_w, mode, hd, scale):
    w_refs, (g_ref, o_ref, w_scr) = refs[:n_w], refs[n_w:]
    cw = w_refs[0].shape[1]

    @pl.when(pl.program_id(1) == 0)
    def _():
        for i, w_ref in enumerate(w_refs):
            w_scr[:, i * cw:(i + 1) * cw] = w_ref[...].astype(w_scr.dtype)

    def run(epilogue):
        acc = jnp.dot(h_ref[...], w_scr[...], preferred_element_type=F32)
        o_ref[...] = epilogue(acc).astype(o_ref.dtype)

    normed = lambda acc: _headnorm(acc, g_ref[...], hd, scale)
    if mode == "none":
        run(lambda acc: acc)
    elif mode == "sigmoid":
        run(jax.nn.sigmoid)
    elif mode == "norm":
        run(normed)
    else:
        @pl.when(pl.program_id(0) == 0)
        def _():
            run(normed)

        @pl.when(pl.program_id(0) != 0)
        def _():
            run(lambda acc: acc)


W_COLS = 512


def _proj(h, w, cols, tm, mode, out_dtype, gain=None, hd=None, scale=1.0, name="proj"):
    m, k = h.shape
    base, stride, n_w, n_tiles = cols
    tn = n_w * W_COLS
    if gain is None:
        g = jnp.ones((1, tn), F32)
    else:
        g = jnp.tile(gain.astype(F32).reshape(1, -1), (1, tn // gain.shape[-1]))
    body = functools.partial(_proj_body, n_w=n_w, mode=mode, hd=hd, scale=scale)
    w_specs = [pl.BlockSpec((k, W_COLS), functools.partial(lambda j, i, off: (0, base + stride * j + off), off=off))
               for off in range(n_w)]
    return pl.pallas_call(
        body, name=name, grid=(n_tiles, m // tm),
        in_specs=[pl.BlockSpec((tm, k), lambda j, i: (i, 0))] + w_specs + [pl.BlockSpec((1, tn), lambda j, i: (0, 0))],
        out_specs=pl.BlockSpec((tm, tn), lambda j, i: (i, j)),
        out_shape=jax.ShapeDtypeStruct((m, n_tiles * tn), out_dtype),
        scratch_shapes=[pltpu.VMEM((k, tn), BF16)], compiler_params=_cparams(2),
    )(h, *([w] * n_w), g)


PERM_ROWS = 256


def _perm_matrix(n, dil, inverse=False):
    per = n // dil
    o = lax.broadcasted_iota(I32, (n, n), 1 if inverse else 0)
    s = lax.broadcasted_iota(I32, (n, n), 0 if inverse else 1)
    return (s == (o % per) * dil + o // per).astype(BF16)


def _qkv_body(h_ref, wq_ref, wk_ref, wv_ref, gq_ref, gk_ref, q_ref, kv_ref, tok_ref, w_scr, *, dil, scale):
    tm = h_ref.shape[0]
    nh = HEADS_PER_GROUP_A
    gw = GROUP_W_A

    @pl.when(pl.program_id(0) == 0)
    def _():
        for i, w_ref in enumerate((wq_ref, wk_ref, wv_ref)):
            w_scr[:, i * gw:(i + 1) * gw] = w_ref[...].astype(BF16)

    acc = jnp.dot(h_ref[...], w_scr[...], preferred_element_type=F32)
    q = _headnorm(acc[:, :gw], gq_ref[...], HEAD_DIM_A, scale)
    k = _headnorm(acc[:, gw:2 * gw], gk_ref[...], HEAD_DIM_A, 1.0)
    v = acc[:, 2 * gw:]
    for i in range(nh):
        tok_ref[pl.ds(i, tm, stride=2 * nh), :] = k[:, i * HEAD_DIM_A:(i + 1) * HEAD_DIM_A]
        tok_ref[pl.ds(nh + i, tm, stride=2 * nh), :] = v[:, i * HEAD_DIM_A:(i + 1) * HEAD_DIM_A]

    if dil == 1:
        q_ref[0] = q.astype(q_ref.dtype)
        kv_ref[0, :, :gw] = k.astype(kv_ref.dtype)
        kv_ref[0, :, gw:] = v.astype(kv_ref.dtype)
        return
    sub = min(tm, PERM_ROWS)
    per = sub // dil
    pm = _perm_matrix(sub, dil)
    qkv = jnp.concatenate([q, k, v], axis=1).astype(BF16)
    for t in range(tm // sub):
        yp = jnp.dot(pm, qkv[t * sub:(t + 1) * sub], preferred_element_type=F32).astype(BF16)
        for r in range(dil):
            q_ref[r, t * per:(t + 1) * per, :] = yp[r * per:(r + 1) * per, :gw]
            kv_ref[r, t * per:(t + 1) * per, :] = yp[r * per:(r + 1) * per, gw:]


def _proj_qkv(h, w, sec, g, g_qa, g_ka, n_batch, seq, tm, dil, act_dtype=BF16):
    n, k = h.shape
    ln = seq // dil
    per = tm // dil
    mt = seq // tm
    assert seq % tm == 0
    assert dil == 1 or (act_dtype == BF16 and tm % PERM_ROWS == 0 and (PERM_ROWS // dil) % 16 == 0)
    tile = lambda v: jnp.tile(v.astype(F32).reshape(1, -1), (1, GROUP_W_A // v.shape[-1]))
    assert W_COLS == GROUP_W_A
    wspec = lambda blk: pl.BlockSpec((k, W_COLS), lambda m: (0, blk))
    q, kv, tok = pl.pallas_call(
        functools.partial(_qkv_body, dil=dil, scale=HEAD_DIM_A ** -0.5), name=f"proj_qkv{g}", grid=(n // tm,),
        in_specs=[pl.BlockSpec((tm, k), lambda m: (m, 0)),
                  wspec(sec["q"] + g), wspec(sec["k"] + g), wspec(sec["v"] + g),
                  pl.BlockSpec((1, GROUP_W_A), lambda m: (0, 0)), pl.BlockSpec((1, GROUP_W_A), lambda m: (0, 0))],
        out_specs=[pl.BlockSpec((None, dil, per, GROUP_W_A), lambda m: (m // mt, 0, m % mt, 0)),
                   pl.BlockSpec((None, dil, per, 2 * GROUP_W_A), lambda m: (m // mt, 0, m % mt, 0)),
                   pl.BlockSpec((tm * 2 * HEADS_PER_GROUP_A, LANES), lambda m: (m, 0))],
        out_shape=[jax.ShapeDtypeStruct((n_batch, dil, ln, GROUP_W_A), act_dtype),
                   jax.ShapeDtypeStruct((n_batch, dil, ln, 2 * GROUP_W_A), act_dtype),
                   jax.ShapeDtypeStruct((n * 2 * HEADS_PER_GROUP_A, LANES), F32)],
        scratch_shapes=[pltpu.VMEM((k, 3 * W_COLS), BF16)], compiler_params=_cparams(1),
    )(h, w, w, w, tile(g_qa), tile(g_ka))
    return q.reshape(n, GROUP_W_A), kv.reshape(n, 2 * GROUP_W_A), tok


def _bias_from_table(tab_ref, bidx, col):
    acc = jnp.zeros(bidx.shape, F32)
    for kb in range(N_BUCKETS):
        acc = jnp.where(bidx == kb, tab_ref[kb, col], acc)
    return acc


def _split_hi_lo(x):
    hi = x.astype(BF16)
    return hi, (x - hi.astype(F32)).astype(BF16)


ATTN_P_BLOCKS = 4


def _attn_p_body(tab_ref, bidx_ref, q_ref, kp_ref, kc_ref, vp_ref, vc_ref, o_ref, bias_scr, *, g, nb):
    i = pl.program_id(0)

    @pl.when(i == 0)
    def _():
        bidx = bidx_ref[...]
        for h in range(HEADS_PER_GROUP_A):
            bias_scr[h] = _bias_from_table(tab_ref, bidx, g * HEADS_PER_GROUP_A + h)

    blk = kp_ref.shape[0]
    k_all = jnp.concatenate([kp_ref[...], kc_ref[...]], axis=0)
    v_all = jnp.concatenate([vp_ref[...], vc_ref[...]], axis=0)
    row = lax.broadcasted_iota(I32, (blk, 2 * blk), 0)
    col = lax.broadcasted_iota(I32, (blk, 2 * blk), 1)
    dist = blk + row - col
    band = (dist >= 0) & (dist <= blk)
    lane = lax.broadcasted_iota(I32, (blk, LANES), 1)
    for u in range(ATTN_P_BLOCKS):
        has_prev = ((i * ATTN_P_BLOCKS + u) % nb) > 0
        valid = band & (has_prev | (col >= blk))
        q = q_ref[u * blk:(u + 1) * blk, :]
        k = k_all[u * blk:(u + 2) * blk]
        v = v_all[u * blk:(u + 2) * blk]
        lse_blk = jnp.zeros((blk, LANES), F32)
        outs = []
        for h in range(HEADS_PER_GROUP_A):
            hs = slice(h * HEAD_DIM_A, (h + 1) * HEAD_DIM_A)
            s = lax.dot_general(q[:, hs], k[:, hs], (((1,), (1,)), ((), ())), preferred_element_type=F32)
            s = jnp.where(valid, s + bias_scr[h], NEG_INF)
            m = jnp.max(s, axis=-1, keepdims=True)
            p = jnp.exp(s - m)
            l = jnp.sum(p, axis=-1, keepdims=True)
            o = jnp.dot(p.astype(BF16), v[:, hs], preferred_element_type=F32) / l
            outs.append(o)
            lse_blk = jnp.where(lane == h, m + jnp.log(l), lse_blk)
        lse_hi, lse_lo = _split_hi_lo(lse_blk)
        o_ref[u * blk:(u + 1) * blk, :] = jnp.concatenate(
            [jnp.concatenate(outs, axis=1).astype(BF16), lse_hi, lse_lo], axis=1)


O_EXT_W = GROUP_W_A + 2 * LANES


def _attn_prompt(q, kv, rel_bias, g, n_batch, seq):
    win, dil = DIL_GROUPS[g]
    blk = win // dil
    nb = seq // dil // blk
    n = n_batch * seq
    a = jnp.arange(blk)[:, None]
    c = jnp.arange(2 * blk)[None, :]
    bidx = _rel_bucket(jnp.maximum(blk + a - c, 0) * dil)

    nu = ATTN_P_BLOCKS
    assert (n // blk) % nu == 0 and (nb % nu == 0 or nb == 1)

    def prev(i):
        return jnp.maximum(i * nu - 1, 0)

    return pl.pallas_call(
        functools.partial(_attn_p_body, g=g, nb=nb), name=f"attn_p{g}", grid=(n // blk // nu,),
        in_specs=[
            pl.BlockSpec(memory_space=pltpu.SMEM),
            pl.BlockSpec((blk, 2 * blk), lambda i: (0, 0)),
            pl.BlockSpec((nu * blk, GROUP_W_A), lambda i: (i, 0)),
            pl.BlockSpec((blk, GROUP_W_A), lambda i: (prev(i), 0)),
            pl.BlockSpec((nu * blk, GROUP_W_A), lambda i: (i, 0)),
            pl.BlockSpec((blk, GROUP_W_A), lambda i: (prev(i), 1)),
            pl.BlockSpec((nu * blk, GROUP_W_A), lambda i: (i, 1)),
        ],
        out_specs=pl.BlockSpec((nu * blk, O_EXT_W), lambda i: (i, 0)),
        out_shape=jax.ShapeDtypeStruct((n, O_EXT_W), BF16),
        scratch_shapes=[pltpu.VMEM((HEADS_PER_GROUP_A, blk, 2 * blk), F32)],
        compiler_params=_cparams(1),
    )(rel_bias, bidx, q, kv, kv, kv, kv)


def _attn_s_body(tab_ref, bidx_ref, q_ref, kvn_ref, tokn_ref, cache_hbm, o_ref, newc_hbm,
                 raw, kv_scr, bias_scr, sem_in, sem_out, *, g, dil, lb, n_req):
    t_new = q_ref.shape[0]
    nh = HEADS_PER_GROUP_A
    n_rows = nh * t_new
    n_keys = kv_scr.shape[0]
    n_col = 2 * nh
    lb8 = lb * n_col
    n = pl.program_id(0)
    nbuf = raw.shape[0]
    ahead = nbuf // 2
    slot = n % nbuf

    def fetch(req):
        s = req % nbuf
        return pltpu.make_async_copy(cache_hbm.at[req], raw.at[s, pl.ds(0, lb8)], sem_in.at[s])

    def flush(req):
        s = req % nbuf
        return pltpu.make_async_copy(raw.at[s, pl.ds(t_new * n_col, lb8)], newc_hbm.at[req], sem_out.at[s])

    @pl.when(n == 0)
    def _():
        kv_scr[lb:, :] = jnp.zeros((n_keys - lb, kv_scr.shape[1]), BF16)
        bidx = bidx_ref[...]
        for h in range(nh):
            bias_scr[h * t_new:(h + 1) * t_new, :] = _bias_from_table(tab_ref, bidx, g * nh + h)
        for r in range(ahead):
            fetch(r).start()

    @pl.when(n >= nbuf - ahead)
    def _():
        flush(n - (nbuf - ahead)).wait()

    @pl.when(n + ahead < n_req)
    def _():
        fetch(n + ahead).start()

    fetch(n).wait()
    raw[slot, lb8:lb8 + t_new * n_col, :] = tokn_ref[...]
    flush(n).start()
    new = kvn_ref[...]

    rows_per = min(lb, 256)
    for c in range(n_col):
        for r0 in range(0, lb, rows_per):
            kv_scr[r0:r0 + rows_per, c * LANES:(c + 1) * LANES] = raw[
                slot, pl.ds(r0 * n_col + c, rows_per, stride=n_col), :].astype(BF16)
    kv_scr[lb:lb + 2 * t_new, :] = jnp.concatenate([new, jnp.zeros_like(new)], axis=0).astype(BF16)

    q = q_ref[...]
    qt = jnp.concatenate([q] * nh, axis=0)
    rr = lax.broadcasted_iota(I32, (n_rows, GROUP_W_A), 0)
    cc = lax.broadcasted_iota(I32, (n_rows, GROUP_W_A), 1)
    qbd = jnp.where(cc // HEAD_DIM_A == rr // t_new, qt, 0.0).astype(BF16)

    s = lax.dot_general(qbd, kv_scr[:, :GROUP_W_A], (((1,), (1,)), ((), ())), preferred_element_type=F32)
    row = lax.broadcasted_iota(I32, (n_rows, n_keys), 0)
    col = lax.broadcasted_iota(I32, (n_rows, n_keys), 1)
    delta = lb + (row & (t_new - 1)) - col
    valid = (delta >= 0) & (delta <= lb) & ((delta & (dil - 1)) == 0)
    s = jnp.where(valid, s + bias_scr[...], NEG_INF)
    m = jnp.max(s, axis=-1, keepdims=True)
    p = jnp.exp(s - m)
    l = jnp.sum(p, axis=-1, keepdims=True)
    o = jnp.dot(p.astype(BF16), kv_scr[:, GROUP_W_A:], preferred_element_type=F32) / l
    lse = m + jnp.log(l)
    lane = lax.broadcasted_iota(I32, (t_new, LANES), 1)
    lse_blk = jnp.zeros((t_new, LANES), F32)
    outs = []
    for h in range(nh):
        outs.append(o[h * t_new:(h + 1) * t_new, h * HEAD_DIM_A:(h + 1) * HEAD_DIM_A])
        lse_blk = jnp.where(lane == h, lse[h * t_new:(h + 1) * t_new, :], lse_blk)
    o_ref[...] = jnp.concatenate(outs + [lse_blk, jnp.zeros_like(lse_blk)], axis=1)

    @pl.when(n == n_req - 1)
    def _():
        for back in range(nbuf - ahead - 1, -1, -1):
            flush(n - back).wait()


ATTN_S_BUFS = 4


def _attn_sample(q, kv_new, tok_new, cache, rel_bias, g, t_new):
    win, dil = DIL_GROUPS[g]
    n_col = 2 * HEADS_PER_GROUP_A
    n_req, lb = cache.shape[0], cache.shape[1] // n_col
    assert t_new & (t_new - 1) == 0 and dil & (dil - 1) == 0 and lb == win
    n_keys = lb + LANES
    t = jnp.arange(t_new)[:, None]
    c = jnp.arange(n_keys)[None, :]
    bidx = _rel_bucket(jnp.clip(lb + t - c, 0, lb))
    return pl.pallas_call(
        functools.partial(_attn_s_body, g=g, dil=dil, lb=lb, n_req=n_req), name=f"attn_s{g}", grid=(n_req,),
        in_specs=[
            pl.BlockSpec(memory_space=pltpu.SMEM),
            pl.BlockSpec((t_new, n_keys), lambda n: (0, 0)),
            pl.BlockSpec((t_new, GROUP_W_A), lambda n: (n, 0)),
            pl.BlockSpec((t_new, 2 * GROUP_W_A), lambda n: (n, 0)),
            pl.BlockSpec((t_new * n_col, LANES), lambda n: (n, 0)),
            pl.BlockSpec(memory_space=pl.ANY),
        ],
        out_specs=[pl.BlockSpec((t_new, O_EXT_W), lambda n: (n, 0)), pl.BlockSpec(memory_space=pl.ANY)],
        out_shape=[jax.ShapeDtypeStruct((n_req * t_new, O_EXT_W), F32), jax.ShapeDtypeStruct(cache.shape, F32)],
        scratch_shapes=[pltpu.VMEM((ATTN_S_BUFS, (lb + t_new) * n_col, LANES), F32),
                        pltpu.VMEM((n_keys, 2 * GROUP_W_A), BF16),
                        pltpu.VMEM((HEADS_PER_GROUP_A * t_new, n_keys), F32),
                        pltpu.SemaphoreType.DMA((ATTN_S_BUFS,)), pltpu.SemaphoreType.DMA((ATTN_S_BUFS,))],
        compiler_params=_cparams(1),
    )(rel_bias, bidx, q, kv_new, tok_new, cache)


def _gmlp_body(u_ref, v_ref, w_ref, b_ref, o_ref, *, period):
    c = w_ref.shape[1]
    nch = u_ref.shape[0] // c
    gd = u_ref.shape[1] // N_GROUPS_B
    i = lax.broadcasted_iota(I32, (c, c), 0)
    j = lax.broadcasted_iota(I32, (c, c), 1)
    mask = (j <= i) & ((i // period) == (j // period))
    b = b_ref[...]
    for g in range(N_GROUPS_B):
        gs = slice(g * gd, (g + 1) * gd)
        wg = jnp.where(mask, w_ref[g], 0.0).astype(BF16)
        vg = [v_ref[ch * c:(ch + 1) * c, gs].astype(BF16) for ch in range(nch)]
        vg = vg[0] if nch == 1 else jnp.concatenate(vg, axis=1)
        sg = jnp.dot(wg, vg, preferred_element_type=F32) + b[:, g:g + 1]
        for ch in range(nch):
            u = u_ref[ch * c:(ch + 1) * c, gs].astype(F32)
            o_ref[ch * c:(ch + 1) * c, gs] = (u * sg[:, ch * gd:(ch + 1) * gd]).astype(o_ref.dtype)


def _gmlp(u, v, w, b, tm, period):
    m, wb = u.shape
    c = w.shape[1]
    return pl.pallas_call(
        functools.partial(_gmlp_body, period=period), name="gmlp", grid=(m // tm,),
        in_specs=[pl.BlockSpec((tm, wb), lambda i: (i, 0)), pl.BlockSpec((tm, wb), lambda i: (i, 0)),
                  pl.BlockSpec((N_GROUPS_B, c, c), lambda i: (0, 0, 0)),
                  pl.BlockSpec((c, N_GROUPS_B), lambda i: (0, 0))],
        out_specs=pl.BlockSpec((tm, wb), lambda i: (i, 0)),
        out_shape=jax.ShapeDtypeStruct((m, wb), BF16), compiler_params=_cparams(1),
    )(u, v, w, b)


def _gmlp_new_body(u_ref, v_ref, w_ref, b_ref, o_ref):
    t = w_ref.shape[0]
    n = u_ref.shape[0] // t
    width = u_ref.shape[1]
    v = v_ref[...].astype(F32).reshape(n, t, width)
    sg = jnp.broadcast_to(b_ref[...][None], (n, t, width))
    p_row = lax.broadcasted_iota(I32, (t, width), 0)
    for q in range(t):
        sg = sg + jnp.where(p_row >= q, w_ref[q], 0.0)[None] * v[:, q:q + 1, :]
    o_ref[...] = (u_ref[...].astype(F32).reshape(n, t, width) * sg).reshape(n * t, width).astype(o_ref.dtype)


def _gmlp_new(u, v, w_s, b_s, t):
    m, wb = u.shape
    gd = wb // N_GROUPS_B
    wq = jnp.repeat(jnp.transpose(w_s[:, :t, :t], (2, 1, 0)), gd, axis=-1)
    bq = jnp.repeat(b_s[:, :t].T, gd, axis=-1)
    whole = lambda a: pl.BlockSpec(a.shape, lambda i: (0,) * a.ndim)
    return pl.pallas_call(
        _gmlp_new_body, name="gmlp_new", grid=(1,),
        in_specs=[whole(u), whole(v), whole(wq), whole(bq)], out_specs=whole(u),
        out_shape=jax.ShapeDtypeStruct((m, wb), BF16), compiler_params=_cparams(1),
    )(u, v, wq, bq)


def _memattn_body(q_ref, kv_ref, o_ref, *, tiles):
    wm = N_HEADS_M * HEAD_DIM_M
    n_ct = HEAD_DIM_M // LANES
    per_key = 2 * n_ct * N_HEADS_M
    reqs = kv_ref.shape[0]
    tq = q_ref.shape[0] // reqs

    for r in range(reqs):
        kvr = kv_ref.at[r]

        def head_slab(kv, h):
            if not tiles:
                return kvr[:, kv * wm + h * HEAD_DIM_M:kv * wm + (h + 1) * HEAD_DIM_M].astype(BF16)
            n_keys = kvr.shape[0] // per_key
            parts = [kvr[pl.ds((kv * n_ct + ct) * N_HEADS_M + h, n_keys, stride=per_key), :] for ct in range(n_ct)]
            return jnp.concatenate(parts, axis=1).astype(BF16)

        q = q_ref[r * tq:(r + 1) * tq, :].astype(BF16)
        outs = []
        for h in range(N_HEADS_M):
            hs = slice(h * HEAD_DIM_M, (h + 1) * HEAD_DIM_M)
            k = head_slab(0, h)
            v = head_slab(1, h)
            s = lax.dot_general(q[:, hs], k, (((1,), (1,)), ((), ())), preferred_element_type=F32)
            m = jnp.max(s, axis=-1, keepdims=True)
            p = jnp.exp(s - m)
            l = jnp.sum(p, axis=-1, keepdims=True)
            outs.append(jnp.dot(p.astype(BF16), v, preferred_element_type=F32) / l)
        o_ref[r * tq:(r + 1) * tq, :] = jnp.concatenate(outs, axis=1).astype(o_ref.dtype)


def _memattn(q, kv, tq, out_dtype, tiles=False, reqs=1):
    m, wm = q.shape
    n = kv.shape[0]
    per = m // n // tq
    assert reqs == 1 or (per == 1 and n % reqs == 0)
    kv_spec = pl.BlockSpec((reqs,) + kv.shape[1:], lambda i: (i // per,) + (0,) * (kv.ndim - 1))
    tq = tq * reqs
    return pl.pallas_call(
        functools.partial(_memattn_body, tiles=tiles), name="memattn", grid=(m // tq,),
        in_specs=[pl.BlockSpec((tq, wm), lambda i: (i, 0)), kv_spec],
        out_specs=pl.BlockSpec((tq, wm), lambda i: (i, 0)),
        out_shape=jax.ShapeDtypeStruct((m, wm), out_dtype), compiler_params=_cparams(1),
    )(q, kv)


def _mix_body(gt_ref, o0_ref, o1_ref, o2_ref, ob_ref, om_ref, wpa_ref, wpb_ref, wpm_ref, z_ref, *, dils):
    d = z_ref.shape[1]
    tm = z_ref.shape[0]
    os_, ls = [], []
    for o_ref, dil in zip((o0_ref, o1_ref, o2_ref), dils):
        if dil == 1:
            x = o_ref[...].astype(F32)
        else:
            rows = jnp.concatenate([o_ref[r] for r in range(dil)], axis=0)
            x = jnp.dot(_perm_matrix(tm, dil, inverse=True), rows, preferred_element_type=F32)
        os_.append(x[:, :GROUP_W_A])
        ls.append(x[:, GROUP_W_A:GROUP_W_A + LANES] + x[:, GROUP_W_A + LANES:])
    l0, l1, l2 = ls
    mx = jnp.maximum(jnp.maximum(l0, l1), l2)
    e0, e1, e2 = jnp.exp(l0 - mx), jnp.exp(l1 - mx), jnp.exp(l2 - mx)
    den = e0 + e1 + e2
    w0, w1, w2 = e0 / den, e1 / den, e2 / den
    cols = []
    for h in range(HEADS_PER_GROUP_A):
        hs = slice(h * HEAD_DIM_A, (h + 1) * HEAD_DIM_A)
        cols.append(w0[:, h:h + 1] * os_[0][:, hs] + w1[:, h:h + 1] * os_[1][:, hs] + w2[:, h:h + 1] * os_[2][:, hs])
    oa = jnp.concatenate(cols, axis=1).astype(BF16)
    pa = jnp.dot(oa, wpa_ref[...], preferred_element_type=F32)
    pb = jnp.dot(ob_ref[...].astype(BF16), wpb_ref[...], preferred_element_type=F32)
    pm = jnp.dot(om_ref[...].astype(BF16), wpm_ref[...], preferred_element_type=F32)
    z = (gt_ref[:, 0:d].astype(F32) * pa + gt_ref[:, d:2 * d].astype(F32) * pb
         + gt_ref[:, 2 * d:3 * d].astype(F32) * pm)
    z_ref[...] = z.astype(z_ref.dtype)


def _mix(gates, o_list, dils, seq, ob, om, wpa, wpb, wpm, tm):
    m = gates.shape[0]
    d = wpa.shape[1]
    mt = seq // tm if any(dl > 1 for dl in dils) else 1

    def rows(width):
        return pl.BlockSpec((tm, width), lambda i: (i, 0))

    def whole(a):
        return pl.BlockSpec(a.shape, lambda i: (0, 0))

    o_specs, o_args = [], []
    for o, dil in zip(o_list, dils):
        if dil == 1:
            o_specs.append(rows(O_EXT_W))
            o_args.append(o)
        else:
            assert seq % tm == 0 and (tm // dil) % 16 == 0
            o_specs.append(pl.BlockSpec((None, dil, tm // dil, O_EXT_W), lambda i: (i // mt, 0, i % mt, 0)))
            o_args.append(o.reshape(m // seq, dil, seq // dil, O_EXT_W))
    return pl.pallas_call(
        functools.partial(_mix_body, dils=tuple(dils)), name="mix", grid=(m // tm,),
        in_specs=[rows(gates.shape[1])] + o_specs
                 + [rows(ob.shape[1]), rows(om.shape[1]), whole(wpa), whole(wpb), whole(wpm)],
        out_specs=rows(d),
        out_shape=jax.ShapeDtypeStruct((m, d), BF16), compiler_params=_cparams(1),
    )(gates, *o_args, ob, om, wpa, wpb, wpm)


def _route(logits):
    lane = lax.broadcasted_iota(I32, logits.shape, 1)
    lane_f = lane.astype(F32)
    is_g = lane < N_EXPERT_GROUPS
    gmax = jnp.max(jnp.where(is_g, logits, -jnp.inf), axis=1, keepdims=True)
    gsel = jnp.min(jnp.where(is_g & (logits == gmax), lane_f, float(LANES)), axis=1, keepdims=True).astype(I32)
    gden = jnp.sum(jnp.where(is_g, jnp.exp(logits - gmax), 0.0), axis=1, keepdims=True)
    pg = 1.0 / gden
    e_lane = lane - N_EXPERT_GROUPS
    in_grp = (e_lane >= 0) & (e_lane < N_EXPERTS) & ((e_lane // EXPERTS_PER_GROUP) == gsel)
    m1 = jnp.max(jnp.where(in_grp, logits, -jnp.inf), axis=1, keepdims=True)
    i1 = jnp.min(jnp.where(in_grp & (logits == m1), lane_f, float(LANES)), axis=1, keepdims=True).astype(I32)
    rest = in_grp & (lane != i1)
    m2 = jnp.max(jnp.where(rest, logits, -jnp.inf), axis=1, keepdims=True)
    i2 = jnp.min(jnp.where(rest & (logits == m2), lane_f, float(LANES)), axis=1, keepdims=True).astype(I32)
    e2 = jnp.exp(m2 - m1)
    w1 = pg / (1.0 + e2)
    w2 = pg * e2 / (1.0 + e2)
    eid = jnp.where(lane == 0, i1 - N_EXPERT_GROUPS, jnp.where(lane == 1, i2 - N_EXPERT_GROUPS, 0))
    wts = jnp.where(lane == 0, w1, jnp.where(lane == 1, w2, 0.0))
    return eid, wts


def _resid_body(x_ref, z_ref, wo_ref, gf_ref, wr_ref, br_ref, xmid_ref, hpk_ref, eid_ref, wts_ref):
    xm = x_ref[...] + jnp.dot(z_ref[...], wo_ref[...], preferred_element_type=F32)
    xmid_ref[...] = xm
    hf = _rms(xm, gf_ref[...])
    wr = wr_ref[...]
    wr_hi, wr_lo = _split_hi_lo(wr)
    hf_hi, hf_lo = _split_hi_lo(hf)
    t = jnp.dot(hf_hi, jnp.concatenate([wr_hi, wr_lo], axis=1), preferred_element_type=F32)
    logits = (t[:, :LANES] + t[:, LANES:] + jnp.dot(hf_lo, wr_hi, preferred_element_type=F32) + br_ref[...])
    eid, wts = _route(logits)
    eid_ref[...] = eid
    wts_ref[...] = wts
    _store_token_tiles(hpk_ref, hf)


def _resid(x2d, z, wo, g_ffn, w_r, b_r, tm):
    m, d = x2d.shape

    def rows(width):
        return pl.BlockSpec((tm, width), lambda i: (i, 0))

    def whole(a):
        return pl.BlockSpec(a.shape, lambda i: (0, 0))

    gf = g_ffn.reshape(1, d)
    nt = d // 2 // LANES
    return pl.pallas_call(
        _resid_body, name="resid", grid=(m // tm,),
        in_specs=[rows(d), rows(d), whole(wo), whole(gf), whole(w_r), whole(b_r)],
        out_specs=[rows(d), pl.BlockSpec((tm * nt, LANES), lambda i: (i, 0)), rows(LANES), rows(LANES)],
        out_shape=[jax.ShapeDtypeStruct((m, d), F32), jax.ShapeDtypeStruct((m * nt, LANES), U32),
                   jax.ShapeDtypeStruct((m, LANES), I32), jax.ShapeDtypeStruct((m, LANES), F32)],
        compiler_params=_cparams(1),
    )(x2d, z, wo, gf, w_r, b_r)


TOKEN_TILE = 8
GATHER_UNROLL = 8


def _gather_rows(idx_at, src_hbm, dst, sem, n):
    def body(i, carry):
        src = pl.multiple_of(idx_at(i), TOKEN_TILE)
        dst_row = pl.multiple_of(i * TOKEN_TILE, TOKEN_TILE)
        pltpu.make_async_copy(src_hbm.at[pl.ds(src, TOKEN_TILE)], dst.at[pl.ds(dst_row, TOKEN_TILE)], sem).start()
        return carry
    lax.fori_loop(0, n, body, 0, unroll=GATHER_UNROLL)


FFN_CHUNKS = 8
FFN_BUFS = 3


def _ffn_body(blk_e_ref, nused_ref, eord_ref, enext_ref, idx_hbm, h_hbm, wg_hbm, wu_hbm, wd_hbm, y_ref,
              idx_s, isem, buf, sem, x_scr, a_scr, b_scr, hm_scr, wg_f, wu_f, wd_f, wsem, wg_s, wu_s, wd_s):
    j = pl.program_id(0)
    nused = nused_ref[0]
    last = nused - 1
    rows = buf.shape[1] // TOKEN_TILE
    e = blk_e_ref[j]

    def idx_copy(b):
        s = b % FFN_BUFS
        return pltpu.make_async_copy(idx_hbm.at[jnp.minimum(b, last)], idx_s.at[s], isem.at[s])

    def wait_rows(s):
        pltpu.make_async_copy(h_hbm.at[pl.ds(0, rows * TOKEN_TILE)], buf.at[s], sem.at[s]).wait()
    new_expert = (j == 0) | (e != blk_e_ref[jnp.maximum(j - 1, 0)])
    ws = eord_ref[j] % 2

    def fetch_w(expert, s):
        return [pltpu.make_async_copy(src.at[expert], dst.at[s], wsem.at[s, i])
                for i, (src, dst) in enumerate(((wg_hbm, wg_f), (wu_hbm, wu_f), (wd_hbm, wd_f)))]

    @pl.when(j == 0)
    def _():
        for c in fetch_w(e, ws):
            c.start()
        for b in range(FFN_BUFS):
            idx_copy(b).start()
        for b in range(FFN_BUFS - 1):
            idx_copy(b).wait()
            _gather_rows(lambda i, b=b: idx_s[b, 0, i], h_hbm, buf.at[b], sem.at[b], rows)

    @pl.when(new_expert)
    def _():
        nxt = enext_ref[j]

        @pl.when(nxt >= 0)
        def _():
            for c in fetch_w(nxt, 1 - ws):
                c.start()

        for c in fetch_w(e, ws):
            c.wait()
        wg_s[...] = wg_f[ws].astype(BF16)
        wu_s[...] = wu_f[ws].astype(BF16)
        wd_s[...] = wd_f[ws].astype(BF16)

    @pl.when(j < nused)
    def _():
        slot = j % FFN_BUFS
        s2 = (j + 2) % FFN_BUFS
        per = rows // FFN_CHUNKS
        idx_copy(j + 2).wait()

        @pl.when(j < last)
        def _():
            idx_copy(j + 3).start()

        def issue(c):
            for i in range(c * per, (c + 1) * per):
                src = pl.multiple_of(idx_s[s2, 0, i], TOKEN_TILE)
                pltpu.make_async_copy(h_hbm.at[pl.ds(src, TOKEN_TILE)],
                                      buf.at[s2, pl.ds(i * TOKEN_TILE, TOKEN_TILE)], sem.at[s2]).start()

        wait_rows(slot)
        x_scr[...] = _load_token_tiles(buf, slot, 0, rows, TOKEN_TILE).astype(BF16)
        dn = a_scr.shape[1] // 2
        for c in range(2):
            issue(c)
            a_scr[:, c * dn:(c + 1) * dn] = jnp.dot(x_scr[...], wg_s[:, c * dn:(c + 1) * dn],
                                                     preferred_element_type=F32)
        for c in range(2):
            issue(2 + c)
            b_scr[:, c * dn:(c + 1) * dn] = jnp.dot(x_scr[...], wu_s[:, c * dn:(c + 1) * dn],
                                                     preferred_element_type=F32)
        a = a_scr[...]
        hm_scr[...] = (a * jax.nn.sigmoid(a) * b_scr[...]).astype(BF16)
        half = wd_s.shape[1] // 2
        n_dc = FFN_CHUNKS - 4
        dq = half // n_dc
        for c in range(n_dc):
            issue(4 + c)
            lo = jnp.dot(hm_scr[...], wd_s[:, c * dq:(c + 1) * dq], preferred_element_type=F32)
            hi = jnp.dot(hm_scr[...], wd_s[:, half + c * dq:half + (c + 1) * dq], preferred_element_type=F32)
            w = _pack_bf16_pair(jnp.concatenate([lo, hi], axis=1))
            for t in range(dq // LANES):
                y_ref[pl.ds(c * (dq // LANES) + t, rows, stride=TOKEN_TILE), :] = w[:, t * LANES:(t + 1) * LANES]

        @pl.when(j == last)
        def _():
            wait_rows((j + 1) % FFN_BUFS)
            wait_rows(s2)

    @pl.when(j >= nused)
    def _():
        y_ref[...] = jnp.zeros(y_ref.shape, y_ref.dtype)


def _ffn(hpk, src_tok, blk_e, nused, eord, enext, wg, wu, wd):
    n_blocks = blk_e.shape[0]
    rows = MOE_ROWS
    tiles = rows * TOKEN_TILE
    d, de = wg.shape[1], wg.shape[2]
    assert d // 2 == TOKEN_TILE * LANES and rows % FFN_CHUNKS == 0 and (d // 2) % ((FFN_CHUNKS - 4) * LANES) == 0
    assert n_blocks >= N_EXPERTS + FFN_BUFS
    idx = (src_tok * TOKEN_TILE).reshape(n_blocks, 1, rows)
    hbm = pl.BlockSpec(memory_space=pl.ANY)
    grid_spec = pltpu.PrefetchScalarGridSpec(
        num_scalar_prefetch=4, grid=(n_blocks,),
        in_specs=[hbm, hbm, hbm, hbm, hbm],
        out_specs=pl.BlockSpec((tiles, LANES), lambda j, *_: (j, 0)),
        scratch_shapes=[pltpu.SMEM((FFN_BUFS, 1, rows), I32), pltpu.SemaphoreType.DMA((FFN_BUFS,)),
                        pltpu.VMEM((FFN_BUFS, tiles, LANES), U32), pltpu.SemaphoreType.DMA((FFN_BUFS,)),
                        pltpu.VMEM((rows, d), BF16), pltpu.VMEM((rows, de), F32), pltpu.VMEM((rows, de), F32),
                        pltpu.VMEM((rows, de), BF16),
                        pltpu.VMEM((2, d, de), wg.dtype), pltpu.VMEM((2, d, de), wu.dtype),
                        pltpu.VMEM((2, de, d), wd.dtype), pltpu.SemaphoreType.DMA((2, 3)),
                        pltpu.VMEM((d, de), BF16), pltpu.VMEM((d, de), BF16), pltpu.VMEM((de, d), BF16)],
    )
    return pl.pallas_call(
        _ffn_body, name="ffn", grid_spec=grid_spec,
        out_shape=jax.ShapeDtypeStruct((n_blocks * tiles, LANES), U32), compiler_params=_cparams(1),
    )(blk_e, nused, eord, enext, idx, hpk, wg, wu, wd)


def _combine_body(idx_cur_ref, idx_nxt_ref, x_ref, w_ref, y_hbm, o_ref, buf, sem, *, n):
    i = pl.program_id(0)
    slot = i % 2
    rows = buf.shape[1] // TOKEN_TILE
    tm = rows // TOP_K

    @pl.when(i == 0)
    def _():
        _gather_rows(lambda r: idx_cur_ref[0, 0, r], y_hbm, buf.at[0], sem.at[0], rows)

    if n > 1:
        @pl.when(i + 1 < n)
        def _():
            _gather_rows(lambda r: idx_nxt_ref[0, 0, r], y_hbm, buf.at[1 - slot], sem.at[1 - slot], rows)

    pltpu.make_async_copy(y_hbm.at[pl.ds(0, rows * TOKEN_TILE)], buf.at[slot], sem.at[slot]).wait()
    w = w_ref[...]
    y0 = _load_token_tiles(buf, slot, 0, tm, TOKEN_TILE)
    y1 = _load_token_tiles(buf, slot, tm, tm, TOKEN_TILE)
    o_ref[...] = x_ref[...] + (w[:, 0:1] * y0 + w[:, 1:2] * y1)


def _combine(xmid, wts, dest, ypk, tm):
    m, d = xmid.shape
    nt = m // tm
    assert d // 2 == TOKEN_TILE * LANES
    idx = (dest * TOKEN_TILE).reshape(nt, tm, TOP_K).transpose(0, 2, 1).reshape(nt, 1, TOP_K * tm)

    def idx_spec(ahead):
        return pl.BlockSpec((1, 1, TOP_K * tm), lambda i: (jnp.minimum(i + ahead, nt - 1), 0, 0),
                            memory_space=pltpu.SMEM)

    return pl.pallas_call(
        functools.partial(_combine_body, n=nt), name="combine", grid=(nt,),
        in_specs=[idx_spec(0), idx_spec(1), pl.BlockSpec((tm, d), lambda i: (i, 0)),
                  pl.BlockSpec((tm, LANES), lambda i: (i, 0)), pl.BlockSpec(memory_space=pl.ANY)],
        out_specs=pl.BlockSpec((tm, d), lambda i: (i, 0)),
        out_shape=jax.ShapeDtypeStruct((m, d), F32),
        scratch_shapes=[pltpu.VMEM((2, TOP_K * tm * TOKEN_TILE, LANES), U32), pltpu.SemaphoreType.DMA((2,))],
        compiler_params=_cparams(1),
    )(idx, idx, xmid, wts, ypk)


def _moe_plan(eid2):
    n_slot = eid2.shape[0] * TOP_K
    eid = eid2.reshape(n_slot)
    onehot = (eid[:, None] == jnp.arange(N_EXPERTS, dtype=I32)[None, :]).astype(I32)
    csum = jnp.cumsum(onehot, axis=0)
    rank = jnp.take_along_axis(csum, eid[:, None], axis=1)[:, 0] - 1
    counts = csum[-1]
    padded = (counts + MOE_ROWS - 1) // MOE_ROWS * MOE_ROWS
    pend = jnp.cumsum(padded)
    pstart = pend - padded
    dest = pstart[eid] + rank
    n_blocks = -(-n_slot // MOE_ROWS) + N_EXPERTS
    nused = (pend[-1] // MOE_ROWS).astype(I32)
    blk = jnp.minimum(jnp.arange(n_blocks, dtype=I32), nused - 1)
    blk_e = jnp.minimum(jnp.searchsorted(pend, blk * MOE_ROWS, side="right"), N_EXPERTS - 1).astype(I32)
    n_rows = n_blocks * MOE_ROWS
    src_tok = (jnp.arange(n_rows, dtype=I32) % (n_slot // TOP_K)).at[dest].set(
        jnp.arange(n_slot, dtype=I32) // TOP_K, unique_indices=True, mode="promise_in_bounds")
    has_rows = counts > 0
    ord_e = jnp.cumsum(has_rows.astype(I32)) - 1
    ids = jnp.arange(N_EXPERTS, dtype=I32)
    later = has_rows[None, :] & (ids[None, :] > ids[:, None])
    next_e = jnp.where(later.any(axis=1), jnp.argmax(later, axis=1), -1).astype(I32)
    return dest.astype(I32), src_tok, blk_e, nused.reshape(1), ord_e[blk_e].astype(I32), next_e[blk_e]


def _mixer(x2d, tm, p, attn_fn, gmlp_fn, mem_fn):
    h = _prep(x2d, p["g_mix"], tm)
    qdt = p["q_dtype"]
    wi, sec = p["w_in"], p["sec"]
    o_list, dils, seq = attn_fn(h)
    n_ub = sec["vb"] - sec["u"]
    ub = _proj(h, wi, (sec["u"], n_ub, n_ub, 1), tm, "none", BF16, name="proj_ub")
    n_vb = sec["qm"] - sec["vb"]
    vb = _proj(h, wi, (sec["vb"], n_vb, n_vb, 1), tm, "norm", p["vb_dtype"], p["g_vb"], n_vb * W_COLS,
               name="proj_vb")
    n_qm = sec["gt"] - sec["qm"]
    qm = _proj(h, wi, (sec["qm"], n_qm, n_qm, 1), tm, "norm", qdt, p["g_qm"], HEAD_DIM_M,
               HEAD_DIM_M ** -0.5, name="proj_qm")
    gates = _proj(h, wi, (sec["gt"], 2, 2, (sec["end"] - sec["gt"]) // 2), tm, "sigmoid", BF16, name="proj_gates")
    ob = gmlp_fn(ub, vb)
    om = mem_fn(qm)
    z = _mix(gates, o_list, dils, seq, ob, om, p["w_pa"], p["w_pb"], p["w_pm"], min(tm, 256))
    xmid, hpk, eid, wts = _resid(x2d, z, p["w_o"], p["g_ffn"], p["w_r"], p["b_r"], min(tm, 256))
    return vb, xmid, hpk, eid, wts


def kernel(x_prompt, x_sample, mem_prompt, cache_a0_kv, cache_a1_kv, cache_a2_kv, cache_mem_kv, rel_bias, g_mix, w_in, g_qa, g_ka, w_pa, g_vb, w_s, b_s, w_pb, g_mem, w_mk, w_mv, g_qm, g_km, w_pm, w_o, g_ffn, w_rg, b_rg, w_re, b_re, w_gate, w_up, w_down):
    n_b, seq, d = x_prompt.shape
    n_s, t_s, _ = x_sample.shape
    depth = w_in.shape[0]
    assert depth == 1
    l = 0
    caches = (cache_a0_kv, cache_a1_kv, cache_a2_kv)
    n_g = len(DIL_GROUPS)
    wb = w_pb.shape[1]
    wm = N_HEADS_M * HEAD_DIM_M

    offs = [0, WIDTH_A, 2 * WIDTH_A, 3 * WIDTH_A, 3 * WIDTH_A + wb, 3 * WIDTH_A + 2 * wb,
            3 * WIDTH_A + 2 * wb + wm, w_in.shape[2]]
    assert all(o % W_COLS == 0 for o in offs) and (offs[7] - offs[6]) % (2 * W_COLS) == 0
    sec = dict(zip(("q", "k", "v", "u", "vb", "qm", "gt", "end"), (o // W_COLS for o in offs)))
    p = {
        "g_mix": g_mix[l], "g_qa": g_qa[l], "g_ka": g_ka[l], "g_vb": g_vb[l], "g_qm": g_qm[l], "g_ffn": g_ffn[l],
        "w_in": w_in[l], "sec": sec,
        "w_pa": w_pa[l].astype(BF16), "w_pb": w_pb[l].astype(BF16), "w_pm": w_pm[l].astype(BF16),
        "w_o": w_o[l].astype(BF16),
    }
    n_r = N_EXPERT_GROUPS + N_EXPERTS
    p["w_r"] = jnp.concatenate([w_rg[l], w_re[l], jnp.zeros((d, LANES - n_r), F32)], axis=1)
    p["b_r"] = jnp.concatenate([b_rg[l], b_re[l], jnp.zeros((LANES - n_r,), F32)]).reshape(1, LANES)

    n_mem = mem_prompt.shape[1]
    h_mem = _prep(mem_prompt.reshape(n_b * n_mem, d), g_mem[l], 256)
    w_mkv = jnp.concatenate([w_mk[l], w_mv[l]], axis=1)
    n_mw = wm // W_COLS
    mkv = _proj(h_mem, w_mkv, (0, n_mw, n_mw, 2), 256, "norm_first", F32, g_km[l], HEAD_DIM_M, name="proj_mkv")
    new_mem_p = mkv.reshape(1, n_b, n_mem, 2, N_HEADS_M, HEAD_DIM_M)

    ps = dict(p, q_dtype=F32, vb_dtype=F32)
    m_s = n_s * t_s
    n_col = 2 * HEADS_PER_GROUP_A
    new_s = []

    def attn_sample(h):
        o_list = []
        for g in range(n_g):
            q, kv, tok = _proj_qkv(h, w_in[l], sec, g, g_qa[l], g_ka[l], 1, m_s, m_s, 1, act_dtype=F32)
            cache3 = caches[g][l].reshape(n_s, caches[g].shape[2] * n_col, LANES)
            o, newc = _attn_sample(q, kv, tok, cache3, rel_bias, g, t_s)
            o_list.append(o)
            new_s.append(newc.reshape(caches[g][l:l + 1].shape))
        return o_list, [1] * n_g, t_s

    n_ct = HEAD_DIM_M // LANES
    mem_tiles = (cache_mem_kv[l].reshape(n_s, n_mem, 2, N_HEADS_M, n_ct, LANES).transpose(0, 1, 2, 4, 3, 5)
                 .reshape(n_s, n_mem * 2 * n_ct * N_HEADS_M, LANES))
    vb_s, xmid_s, hpk_s, eid_s, wts_s = _mixer(
        x_sample.reshape(m_s, d), m_s, ps, attn_sample,
        lambda u, v: _gmlp_new(u, v, w_s[l], b_s[l], t_s),
        lambda qm: _memattn(qm, mem_tiles, t_s, F32, tiles=True, reqs=4))
    new_vb_s = vb_s.reshape(1, n_s, t_s, wb)

    pp = dict(p, q_dtype=BF16, vb_dtype=BF16)
    new_p = []

    def attn_prompt(h):
        o_list = []
        for g, (win, dil) in enumerate(DIL_GROUPS):
            q, kv, tok = _proj_qkv(h, w_in[l], sec, g, g_qa[l], g_ka[l], n_b, seq, 512, dil)
            o_list.append(_attn_prompt(q, kv, rel_bias, g, n_b, seq))
            keep = min(win, seq)
            kv5 = tok.reshape(n_b, seq, 2, HEADS_PER_GROUP_A, HEAD_DIM_A)
            new_p.append((kv5 if keep == seq else kv5[:, seq - keep:])[None])
        return o_list, [dl for _, dl in DIL_GROUPS], seq

    _, xmid_p, hpk_p, eid_p, wts_p = _mixer(
        x_prompt.reshape(n_b * seq, d), 512, pp, attn_prompt,
        lambda u, v: _gmlp(u, v, w_s[l], b_s[l].T, 512, CHUNK),
        lambda qm: _memattn(qm, mkv.reshape(n_b, n_mem, 2 * wm), 512, BF16))

    n_p = n_b * seq
    hpk = jnp.concatenate([hpk_p, hpk_s], axis=0)
    eid2 = jnp.concatenate([eid_p[:, :TOP_K], eid_s[:, :TOP_K]], axis=0)
    dest, src_tok, blk_e, nused, eord, enext = _moe_plan(eid2)
    ypk = _ffn(hpk, src_tok, blk_e, nused, eord, enext, w_gate[l], w_up[l], w_down[l])
    dest2 = dest.reshape(-1, TOP_K)
    y_p = _combine(xmid_p, wts_p, dest2[:n_p], ypk, 256)
    y_s = _combine(xmid_s, wts_s, dest2[n_p:], ypk, m_s)

    return (y_p.reshape(n_b, seq, d), y_s.reshape(n_s, t_s, d), new_p[0], new_p[1], new_p[2], new_mem_p,
            new_s[0], new_s[1], new_s[2], new_vb_s)
```

```python
import functools
import math

import jax
import jax.numpy as jnp
from jax import lax
from jax.experimental import pallas as pl
from jax.experimental.pallas import tpu as pltpu

F32 = jnp.float32
BF16 = jnp.bfloat16
I32 = jnp.int32
U32 = jnp.uint32

EPS = 1e-6
NEG_INF = -1e30

HEAD_DIM_A = 128
HEADS_PER_GROUP_A = 4
DIL_GROUPS = ((128, 1), (512, 4), (2048, 16))
GROUP_W_A = HEADS_PER_GROUP_A * HEAD_DIM_A
WIDTH_A = len(DIL_GROUPS) * GROUP_W_A
CHUNK = 128
N_GROUPS_B = 8
N_HEADS_M = 4
HEAD_DIM_M = 256
N_BUCKETS = 32
MAX_EXACT = N_BUCKETS // 2
MAX_DISTANCE = 2048
N_EXPERT_GROUPS = 4
EXPERTS_PER_GROUP = 8
N_EXPERTS = N_EXPERT_GROUPS * EXPERTS_PER_GROUP
TOP_K = 2
LANES = 128
MOE_ROWS = 256
VMEM_LIMIT = 56 * 1024 * 1024


def _cparams(n_grid, vmem=VMEM_LIMIT):
    return pltpu.CompilerParams(dimension_semantics=("arbitrary",) * n_grid, vmem_limit_bytes=vmem)


def _rms(x, g):
    return x * lax.rsqrt(jnp.mean(x * x, axis=-1, keepdims=True) + EPS) * g


def _pack_bf16_pair(x):
    n = x.shape[1] // 2
    lo = lax.bitcast_convert_type(x[:, :n].astype(BF16).astype(F32), U32)
    hi = lax.bitcast_convert_type(x[:, n:].astype(BF16).astype(F32), U32)
    return (hi & jnp.uint32(0xFFFF0000)) | (lo >> 16)


def _unpack_bf16_pair(w):
    lo = lax.bitcast_convert_type(w << 16, F32)
    hi = lax.bitcast_convert_type(w & jnp.uint32(0xFFFF0000), F32)
    return lo, hi


def _store_token_tiles(ref, x):
    w = _pack_bf16_pair(x)
    m, n = w.shape
    nt = n // LANES
    for c in range(nt):
        ref[pl.ds(c, m, stride=nt), :] = w[:, c * LANES:(c + 1) * LANES]


def _load_token_tiles(ref, lead, row0, m, nt):
    los, his = [], []
    view = ref if lead is None else ref.at[lead]
    for c in range(nt):
        lo, hi = _unpack_bf16_pair(view[pl.ds(row0 * nt + c, m, stride=nt), :])
        los.append(lo)
        his.append(hi)
    return jnp.concatenate(los + his, axis=1)


def _rel_bucket(dist):
    d = jnp.maximum(dist, 1).astype(F32)
    large = MAX_EXACT + (jnp.log(d / MAX_EXACT) / math.log(MAX_DISTANCE / MAX_EXACT)
                         * (N_BUCKETS - MAX_EXACT)).astype(I32)
    return jnp.where(dist < MAX_EXACT, dist, jnp.minimum(large, N_BUCKETS - 1)).astype(I32)


def _prep_body(x_ref, g_ref, o_ref):
    o_ref[...] = _rms(x_ref[...], g_ref[...]).astype(o_ref.dtype)


def _prep(x2d, g, tm):
    m, d = x2d.shape
    return pl.pallas_call(
        _prep_body, name="prep", grid=(m // tm,),
        in_specs=[pl.BlockSpec((tm, d), lambda i: (i, 0)), pl.BlockSpec((1, d), lambda i: (0, 0))],
        out_specs=pl.BlockSpec((tm, d), lambda i: (i, 0)),
        out_shape=jax.ShapeDtypeStruct((m, d), BF16), compiler_params=_cparams(1),
    )(x2d, g.reshape(1, d))


def _headnorm(acc, g, hd, scale):
    outs = []
    for j in range(acc.shape[1] // hd):
        sl = acc[:, j * hd:(j + 1) * hd]
        outs.append(sl * lax.rsqrt(jnp.mean(sl * sl, axis=-1, keepdims=True) + EPS))
    y = outs[0] if len(outs) == 1 else jnp.concatenate(outs, axis=1)
    y = y * g
    return y * scale if scale != 1.0 else y


def _proj_body(h_ref, *refs, n_w, mode, hd, scale):
    w_refs, (g_ref, o_ref, w_scr) = refs[:n_w], refs[n_w:]
    cw = w_refs[0].shape[1]

    @pl.when(pl.program_id(1) == 0)
    def _():
        for i, w_ref in enumerate(w_refs):
            w_scr[:, i * cw:(i + 1) * cw] = w_ref[...].astype(w_scr.dtype)

    def run(epilogue):
        acc = jnp.dot(h_ref[...], w_scr[...], preferred_element_type=F32)
        o_ref[...] = epilogue(acc).astype(o_ref.dtype)

    normed = lambda acc: _headnorm(acc, g_ref[...], hd, scale)
    if mode == "none":
        run(lambda acc: acc)
    elif mode == "sigmoid":
        run(jax.nn.sigmoid)
    elif mode == "norm":
        run(normed)
    else:
        @pl.when(pl.program_id(0) == 0)
        def _():
            run(normed)

        @pl.when(pl.program_id(0) != 0)
        def _():
            run(lambda acc: acc)


W_COLS = 512


def _proj(h, w, cols, tm, mode, out_dtype, gain=None, hd=None, scale=1.0, name="proj"):
    m, k = h.shape
    base, stride, n_w, n_tiles = cols
    tn = n_w * W_COLS
    if gain is None:
        g = jnp.ones((1, tn), F32)
    else:
        g = jnp.tile(gain.astype(F32).reshape(1, -1), (1, tn // gain.shape[-1]))
    body = functools.partial(_proj_body, n_w=n_w, mode=mode, hd=hd, scale=scale)
    w_specs = [pl.BlockSpec((k, W_COLS), functools.partial(lambda j, i, off: (0, base + stride * j + off), off=off))
               for off in range(n_w)]
    return pl.pallas_call(
        body, name=name, grid=(n_tiles, m // tm),
        in_specs=[pl.BlockSpec((tm, k), lambda j, i: (i, 0))] + w_specs + [pl.BlockSpec((1, tn), lambda j, i: (0, 0))],
        out_specs=pl.BlockSpec((tm, tn), lambda j, i: (i, j)),
        out_shape=jax.ShapeDtypeStruct((m, n_tiles * tn), out_dtype),
        scratch_shapes=[pltpu.VMEM((k, tn), BF16)], compiler_params=_cparams(2),
    )(h, *([w] * n_w), g)


PERM_ROWS = 256


def _perm_matrix(n, dil, inverse=False):
    per = n // dil
    o = lax.broadcasted_iota(I32, (n, n), 1 if inverse else 0)
    s = lax.broadcasted_iota(I32, (n, n), 0 if inverse else 1)
    return (s == (o % per) * dil + o // per).astype(BF16)


def _qkv_body(h_ref, wq_ref, wk_ref, wv_ref, gq_ref, gk_ref, q_ref, kv_ref, tok_ref, w_scr, *, dil, scale):
    tm = h_ref.shape[0]
    nh = HEADS_PER_GROUP_A
    gw = GROUP_W_A

    @pl.when(pl.program_id(0) == 0)
    def _():
        for i, w_ref in enumerate((wq_ref, wk_ref, wv_ref)):
            w_scr[:, i * gw:(i + 1) * gw] = w_ref[...].astype(BF16)

    acc = jnp.dot(h_ref[...], w_scr[...], preferred_element_type=F32)
    q = _headnorm(acc[:, :gw], gq_ref[...], HEAD_DIM_A, scale)
    k = _headnorm(acc[:, gw:2 * gw], gk_ref[...], HEAD_DIM_A, 1.0)
    v = acc[:, 2 * gw:]
    for i in range(nh):
        tok_ref[pl.ds(i, tm, stride=2 * nh), :] = k[:, i * HEAD_DIM_A:(i + 1) * HEAD_DIM_A]
        tok_ref[pl.ds(nh + i, tm, stride=2 * nh), :] = v[:, i * HEAD_DIM_A:(i + 1) * HEAD_DIM_A]

    if dil == 1:
        q_ref[0] = q.astype(q_ref.dtype)
        kv_ref[0, :, :gw] = k.astype(kv_ref.dtype)
        kv_ref[0, :, gw:] = v.astype(kv_ref.dtype)
        return
    sub = min(tm, PERM_ROWS)
    per = sub // dil
    pm = _perm_matrix(sub, dil)
    qkv = jnp.concatenate([q, k, v], axis=1).astype(BF16)
    for t in range(tm // sub):
        yp = jnp.dot(pm, qkv[t * sub:(t + 1) * sub], preferred_element_type=F32).astype(BF16)
        for r in range(dil):
            q_ref[r, t * per:(t + 1) * per, :] = yp[r * per:(r + 1) * per, :gw]
            kv_ref[r, t * per:(t + 1) * per, :] = yp[r * per:(r + 1) * per, gw:]


def _proj_qkv(h, w, sec, g, g_qa, g_ka, n_batch, seq, tm, dil, act_dtype=BF16):
    n, k = h.shape
    ln = seq // dil
    per = tm // dil
    mt = seq // tm
    assert seq % tm == 0
    assert dil == 1 or (act_dtype == BF16 and tm % PERM_ROWS == 0 and (PERM_ROWS // dil) % 16 == 0)
    tile = lambda v: jnp.tile(v.astype(F32).reshape(1, -1), (1, GROUP_W_A // v.shape[-1]))
    assert W_COLS == GROUP_W_A
    wspec = lambda blk: pl.BlockSpec((k, W_COLS), lambda m: (0, blk))
    q, kv, tok = pl.pallas_call(
        functools.partial(_qkv_body, dil=dil, scale=HEAD_DIM_A ** -0.5), name=f"proj_qkv{g}", grid=(n // tm,),
        in_specs=[pl.BlockSpec((tm, k), lambda m: (m, 0)),
                  wspec(sec["q"] + g), wspec(sec["k"] + g), wspec(sec["v"] + g),
                  pl.BlockSpec((1, GROUP_W_A), lambda m: (0, 0)), pl.BlockSpec((1, GROUP_W_A), lambda m: (0, 0))],
        out_specs=[pl.BlockSpec((None, dil, per, GROUP_W_A), lambda m: (m // mt, 0, m % mt, 0)),
                   pl.BlockSpec((None, dil, per, 2 * GROUP_W_A), lambda m: (m // mt, 0, m % mt, 0)),
                   pl.BlockSpec((tm * 2 * HEADS_PER_GROUP_A, LANES), lambda m: (m, 0))],
        out_shape=[jax.ShapeDtypeStruct((n_batch, dil, ln, GROUP_W_A), act_dtype),
                   jax.ShapeDtypeStruct((n_batch, dil, ln, 2 * GROUP_W_A), act_dtype),
                   jax.ShapeDtypeStruct((n * 2 * HEADS_PER_GROUP_A, LANES), F32)],
        scratch_shapes=[pltpu.VMEM((k, 3 * W_COLS), BF16)], compiler_params=_cparams(1),
    )(h, w, w, w, tile(g_qa), tile(g_ka))
    return q.reshape(n, GROUP_W_A), kv.reshape(n, 2 * GROUP_W_A), tok


def _bias_from_table(tab_ref, bidx, col):
    acc = jnp.zeros(bidx.shape, F32)
    for kb in range(N_BUCKETS):
        acc = jnp.where(bidx == kb, tab_ref[kb, col], acc)
    return acc


def _split_hi_lo(x):
    hi = x.astype(BF16)
    return hi, (x - hi.astype(F32)).astype(BF16)


ATTN_P_BLOCKS = 4


def _attn_p_body(tab_ref, bidx_ref, q_ref, kp_ref, kc_ref, vp_ref, vc_ref, o_ref, bias_scr, *, g, nb):
    i = pl.program_id(0)

    @pl.when(i == 0)
    def _():
        bidx = bidx_ref[...]
        for h in range(HEADS_PER_GROUP_A):
            bias_scr[h] = _bias_from_table(tab_ref, bidx, g * HEADS_PER_GROUP_A + h)

    blk = kp_ref.shape[0]
    k_all = jnp.concatenate([kp_ref[...], kc_ref[...]], axis=0)
    v_all = jnp.concatenate([vp_ref[...], vc_ref[...]], axis=0)
    row = lax.broadcasted_iota(I32, (blk, 2 * blk), 0)
    col = lax.broadcasted_iota(I32, (blk, 2 * blk), 1)
    dist = blk + row - col
    band = (dist >= 0) & (dist <= blk)
    lane = lax.broadcasted_iota(I32, (blk, LANES), 1)
    for u in range(ATTN_P_BLOCKS):
        has_prev = ((i * ATTN_P_BLOCKS + u) % nb) > 0
        valid = band & (has_prev | (col >= blk))
        q = q_ref[u * blk:(u + 1) * blk, :]
        k = k_all[u * blk:(u + 2) * blk]
        v = v_all[u * blk:(u + 2) * blk]
        lse_blk = jnp.zeros((blk, LANES), F32)
        outs = []
        for h in range(HEADS_PER_GROUP_A):
            hs = slice(h * HEAD_DIM_A, (h + 1) * HEAD_DIM_A)
            s = lax.dot_general(q[:, hs], k[:, hs], (((1,), (1,)), ((), ())), preferred_element_type=F32)
            s = jnp.where(valid, s + bias_scr[h], NEG_INF)
            m = jnp.max(s, axis=-1, keepdims=True)
            p = jnp.exp(s - m)
            l = jnp.sum(p, axis=-1, keepdims=True)
            o = jnp.dot(p.astype(BF16), v[:, hs], preferred_element_type=F32) / l
            outs.append(o)
            lse_blk = jnp.where(lane == h, m + jnp.log(l), lse_blk)
        lse_hi, lse_lo = _split_hi_lo(lse_blk)
        o_ref[u * blk:(u + 1) * blk, :] = jnp.concatenate(
            [jnp.concatenate(outs, axis=1).astype(BF16), lse_hi, lse_lo], axis=1)


O_EXT_W = GROUP_W_A + 2 * LANES


def _attn_prompt(q, kv, rel_bias, g, n_batch, seq):
    win, dil = DIL_GROUPS[g]
    blk = win // dil
    nb = seq // dil // blk
    n = n_batch * seq
    a = jnp.arange(blk)[:, None]
    c = jnp.arange(2 * blk)[None, :]
    bidx = _rel_bucket(jnp.maximum(blk + a - c, 0) * dil)

    nu = ATTN_P_BLOCKS
    assert (n // blk) % nu == 0 and (nb % nu == 0 or nb == 1)

    def prev(i):
        return jnp.maximum(i * nu - 1, 0)

    return pl.pallas_call(
        functools.partial(_attn_p_body, g=g, nb=nb), name=f"attn_p{g}", grid=(n // blk // nu,),
        in_specs=[
            pl.BlockSpec(memory_space=pltpu.SMEM),
            pl.BlockSpec((blk, 2 * blk), lambda i: (0, 0)),
            pl.BlockSpec((nu * blk, GROUP_W_A), lambda i: (i, 0)),
            pl.BlockSpec((blk, GROUP_W_A), lambda i: (prev(i), 0)),
            pl.BlockSpec((nu * blk, GROUP_W_A), lambda i: (i, 0)),
            pl.BlockSpec((blk, GROUP_W_A), lambda i: (prev(i), 1)),
            pl.BlockSpec((nu * blk, GROUP_W_A), lambda i: (i, 1)),
        ],
        out_specs=pl.BlockSpec((nu * blk, O_EXT_W), lambda i: (i, 0)),
        out_shape=jax.ShapeDtypeStruct((n, O_EXT_W), BF16),
        scratch_shapes=[pltpu.VMEM((HEADS_PER_GROUP_A, blk, 2 * blk), F32)],
        compiler_params=_cparams(1),
    )(rel_bias, bidx, q, kv, kv, kv, kv)


def _attn_s_body(tab_ref, bidx_ref, q_ref, kvn_ref, tokn_ref, cache_hbm, o_ref, newc_hbm,
                 raw, kv_scr, bias_scr, sem_in, sem_out, *, g, dil, lb, n_req):
    t_new = q_ref.shape[0]
    nh = HEADS_PER_GROUP_A
    n_rows = nh * t_new
    n_keys = kv_scr.shape[0]
    n_col = 2 * nh
    lb8 = lb * n_col
    n = pl.program_id(0)
    nbuf = raw.shape[0]
    ahead = nbuf // 2
    slot = n % nbuf

    def fetch(req):
        s = req % nbuf
        return pltpu.make_async_copy(cache_hbm.at[req], raw.at[s, pl.ds(0, lb8)], sem_in.at[s])

    def flush(req):
        s = req % nbuf
        return pltpu.make_async_copy(raw.at[s, pl.ds(t_new * n_col, lb8)], newc_hbm.at[req], sem_out.at[s])

    @pl.when(n == 0)
    def _():
        kv_scr[lb:, :] = jnp.zeros((n_keys - lb, kv_scr.shape[1]), BF16)
        bidx = bidx_ref[...]
        for h in range(nh):
            bias_scr[h * t_new:(h + 1) * t_new, :] = _bias_from_table(tab_ref, bidx, g * nh + h)
        for r in range(ahead):
            fetch(r).start()

    @pl.when(n >= nbuf - ahead)
    def _():
        flush(n - (nbuf - ahead)).wait()

    @pl.when(n + ahead < n_req)
    def _():
        fetch(n + ahead).start()

    fetch(n).wait()
    raw[slot, lb8:lb8 + t_new * n_col, :] = tokn_ref[...]
    flush(n).start()
    new = kvn_ref[...]

    rows_per = min(lb, 256)
    for c in range(n_col):
        for r0 in range(0, lb, rows_per):
            kv_scr[r0:r0 + rows_per, c * LANES:(c + 1) * LANES] = raw[
                slot, pl.ds(r0 * n_col + c, rows_per, stride=n_col), :].astype(BF16)
    kv_scr[lb:lb + 2 * t_new, :] = jnp.concatenate([new, jnp.zeros_like(new)], axis=0).astype(BF16)

    q = q_ref[...]
    qt = jnp.concatenate([q] * nh, axis=0)
    rr = lax.broadcasted_iota(I32, (n_rows, GROUP_W_A), 0)
    cc = lax.broadcasted_iota(I32, (n_rows, GROUP_W_A), 1)
    qbd = jnp.where(cc // HEAD_DIM_A == rr // t_new, qt, 0.0).astype(BF16)

    s = lax.dot_general(qbd, kv_scr[:, :GROUP_W_A], (((1,), (1,)), ((), ())), preferred_element_type=F32)
    row = lax.broadcasted_iota(I32, (n_rows, n_keys), 0)
    col = lax.broadcasted_iota(I32, (n_rows, n_keys), 1)
    delta = lb + (row & (t_new - 1)) - col
    valid = (delta >= 0) & (delta <= lb) & ((delta & (dil - 1)) == 0)
    s = jnp.where(valid, s + bias_scr[...], NEG_INF)
    m = jnp.max(s, axis=-1, keepdims=True)
    p = jnp.exp(s - m)
    l = jnp.sum(p, axis=-1, keepdims=True)
    o = jnp.dot(p.astype(BF16), kv_scr[:, GROUP_W_A:], preferred_element_type=F32) / l
    lse = m + jnp.log(l)
    lane = lax.broadcasted_iota(I32, (t_new, LANES), 1)
    lse_blk = jnp.zeros((t_new, LANES), F32)
    outs = []
    for h in range(nh):
        outs.append(o[h * t_new:(h + 1) * t_new, h * HEAD_DIM_A:(h + 1) * HEAD_DIM_A])
        lse_blk = jnp.where(lane == h, lse[h * t_new:(h + 1) * t_new, :], lse_blk)
    o_ref[...] = jnp.concatenate(outs + [lse_blk, jnp.zeros_like(lse_blk)], axis=1)

    @pl.when(n == n_req - 1)
    def _():
        for back in range(nbuf - ahead - 1, -1, -1):
            flush(n - back).wait()


ATTN_S_BUFS = 4


def _attn_sample(q, kv_new, tok_new, cache, rel_bias, g, t_new):
    win, dil = DIL_GROUPS[g]
    n_col = 2 * HEADS_PER_GROUP_A
    n_req, lb = cache.shape[0], cache.shape[1] // n_col
    assert t_new & (t_new - 1) == 0 and dil & (dil - 1) == 0 and lb == win
    n_keys = lb + LANES
    t = jnp.arange(t_new)[:, None]
    c = jnp.arange(n_keys)[None, :]
    bidx = _rel_bucket(jnp.clip(lb + t - c, 0, lb))
    return pl.pallas_call(
        functools.partial(_attn_s_body, g=g, dil=dil, lb=lb, n_req=n_req), name=f"attn_s{g}", grid=(n_req,),
        in_specs=[
            pl.BlockSpec(memory_space=pltpu.SMEM),
            pl.BlockSpec((t_new, n_keys), lambda n: (0, 0)),
            pl.BlockSpec((t_new, GROUP_W_A), lambda n: (n, 0)),
            pl.BlockSpec((t_new, 2 * GROUP_W_A), lambda n: (n, 0)),
            pl.BlockSpec((t_new * n_col, LANES), lambda n: (n, 0)),
            pl.BlockSpec(memory_space=pl.ANY),
        ],
        out_specs=[pl.BlockSpec((t_new, O_EXT_W), lambda n: (n, 0)), pl.BlockSpec(memory_space=pl.ANY)],
        out_shape=[jax.ShapeDtypeStruct((n_req * t_new, O_EXT_W), F32), jax.ShapeDtypeStruct(cache.shape, F32)],
        scratch_shapes=[pltpu.VMEM((ATTN_S_BUFS, (lb + t_new) * n_col, LANES), F32),
                        pltpu.VMEM((n_keys, 2 * GROUP_W_A), BF16),
                        pltpu.VMEM((HEADS_PER_GROUP_A * t_new, n_keys), F32),
                        pltpu.SemaphoreType.DMA((ATTN_S_BUFS,)), pltpu.SemaphoreType.DMA((ATTN_S_BUFS,))],
        compiler_params=_cparams(1),
    )(rel_bias, bidx, q, kv_new, tok_new, cache)


def _gmlp_body(u_ref, v_ref, w_ref, b_ref, o_ref, *, period):
    c = w_ref.shape[1]
    nch = u_ref.shape[0] // c
    gd = u_ref.shape[1] // N_GROUPS_B
    i = lax.broadcasted_iota(I32, (c, c), 0)
    j = lax.broadcasted_iota(I32, (c, c), 1)
    mask = (j <= i) & ((i // period) == (j // period))
    b = b_ref[...]
    for g in range(N_GROUPS_B):
        gs = slice(g * gd, (g + 1) * gd)
        wg = jnp.where(mask, w_ref[g], 0.0).astype(BF16)
        vg = [v_ref[ch * c:(ch + 1) * c, gs].astype(BF16) for ch in range(nch)]
        vg = vg[0] if nch == 1 else jnp.concatenate(vg, axis=1)
        sg = jnp.dot(wg, vg, preferred_element_type=F32) + b[:, g:g + 1]
        for ch in range(nch):
            u = u_ref[ch * c:(ch + 1) * c, gs].astype(F32)
            o_ref[ch * c:(ch + 1) * c, gs] = (u * sg[:, ch * gd:(ch + 1) * gd]).astype(o_ref.dtype)


def _gmlp(u, v, w, b, tm, period):
    m, wb = u.shape
    c = w.shape[1]
    return pl.pallas_call(
        functools.partial(_gmlp_body, period=period), name="gmlp", grid=(m // tm,),
        in_specs=[pl.BlockSpec((tm, wb), lambda i: (i, 0)), pl.BlockSpec((tm, wb), lambda i: (i, 0)),
                  pl.BlockSpec((N_GROUPS_B, c, c), lambda i: (0, 0, 0)),
                  pl.BlockSpec((c, N_GROUPS_B), lambda i: (0, 0))],
        out_specs=pl.BlockSpec((tm, wb), lambda i: (i, 0)),
        out_shape=jax.ShapeDtypeStruct((m, wb), BF16), compiler_params=_cparams(1),
    )(u, v, w, b)


def _gmlp_new_body(u_ref, v_ref, w_ref, b_ref, o_ref):
    t = w_ref.shape[0]
    n = u_ref.shape[0] // t
    width = u_ref.shape[1]
    v = v_ref[...].astype(F32).reshape(n, t, width)
    sg = jnp.broadcast_to(b_ref[...][None], (n, t, width))
    p_row = lax.broadcasted_iota(I32, (t, width), 0)
    for q in range(t):
        sg = sg + jnp.where(p_row >= q, w_ref[q], 0.0)[None] * v[:, q:q + 1, :]
    o_ref[...] = (u_ref[...].astype(F32).reshape(n, t, width) * sg).reshape(n * t, width).astype(o_ref.dtype)


def _gmlp_new(u, v, w_s, b_s, t):
    m, wb = u.shape
    gd = wb // N_GROUPS_B
    wq = jnp.repeat(jnp.transpose(w_s[:, :t, :t], (2, 1, 0)), gd, axis=-1)
    bq = jnp.repeat(b_s[:, :t].T, gd, axis=-1)
    whole = lambda a: pl.BlockSpec(a.shape, lambda i: (0,) * a.ndim)
    return pl.pallas_call(
        _gmlp_new_body, name="gmlp_new", grid=(1,),
        in_specs=[whole(u), whole(v), whole(wq), whole(bq)], out_specs=whole(u),
        out_shape=jax.ShapeDtypeStruct((m, wb), BF16), compiler_params=_cparams(1),
    )(u, v, wq, bq)


def _memattn_body(q_ref, kv_ref, o_ref, *, tiles):
    wm = N_HEADS_M * HEAD_DIM_M
    n_ct = HEAD_DIM_M // LANES
    per_key = 2 * n_ct * N_HEADS_M
    reqs = kv_ref.shape[0]
    tq = q_ref.shape[0] // reqs

    for r in range(reqs):
        kvr = kv_ref.at[r]

        def head_slab(kv, h):
            if not tiles:
                return kvr[:, kv * wm + h * HEAD_DIM_M:kv * wm + (h + 1) * HEAD_DIM_M].astype(BF16)
            n_keys = kvr.shape[0] // per_key
            parts = [kvr[pl.ds((kv * n_ct + ct) * N_HEADS_M + h, n_keys, stride=per_key), :] for ct in range(n_ct)]
            return jnp.concatenate(parts, axis=1).astype(BF16)

        q = q_ref[r * tq:(r + 1) * tq, :].astype(BF16)
        outs = []
        for h in range(N_HEADS_M):
            hs = slice(h * HEAD_DIM_M, (h + 1) * HEAD_DIM_M)
            k = head_slab(0, h)
            v = head_slab(1, h)
            s = lax.dot_general(q[:, hs], k, (((1,), (1,)), ((), ())), preferred_element_type=F32)
            m = jnp.max(s, axis=-1, keepdims=True)
            p = jnp.exp(s - m)
            l = jnp.sum(p, axis=-1, keepdims=True)
            outs.append(jnp.dot(p.astype(BF16), v, preferred_element_type=F32) / l)
        o_ref[r * tq:(r + 1) * tq, :] = jnp.concatenate(outs, axis=1).astype(o_ref.dtype)


def _memattn(q, kv, tq, out_dtype, tiles=False, reqs=1):
    m, wm = q.shape
    n = kv.shape[0]
    per = m // n // tq
    assert reqs == 1 or (per == 1 and n % reqs == 0)
    kv_spec = pl.BlockSpec((reqs,) + kv.shape[1:], lambda i: (i // per,) + (0,) * (kv.ndim - 1))
    tq = tq * reqs
    return pl.pallas_call(
        functools.partial(_memattn_body, tiles=tiles), name="memattn", grid=(m // tq,),
        in_specs=[pl.BlockSpec((tq, wm), lambda i: (i, 0)), kv_spec],
        out_specs=pl.BlockSpec((tq, wm), lambda i: (i, 0)),
        out_shape=jax.ShapeDtypeStruct((m, wm), out_dtype), compiler_params=_cparams(1),
    )(q, kv)


def _mix_body(gt_ref, o0_ref, o1_ref, o2_ref, ob_ref, om_ref, wpa_ref, wpb_ref, wpm_ref, z_ref, *, dils):
    d = z_ref.shape[1]
    tm = z_ref.shape[0]
    os_, ls = [], []
    for o_ref, dil in zip((o0_ref, o1_ref, o2_ref), dils):
        if dil == 1:
            x = o_ref[...].astype(F32)
        else:
            rows = jnp.concatenate([o_ref[r] for r in range(dil)], axis=0)
            x = jnp.dot(_perm_matrix(tm, dil, inverse=True), rows, preferred_element_type=F32)
        os_.append(x[:, :GROUP_W_A])
        ls.append(x[:, GROUP_W_A:GROUP_W_A + LANES] + x[:, GROUP_W_A + LANES:])
    l0, l1, l2 = ls
    mx = jnp.maximum(jnp.maximum(l0, l1), l2)
    e0, e1, e2 = jnp.exp(l0 - mx), jnp.exp(l1 - mx), jnp.exp(l2 - mx)
    den = e0 + e1 + e2
    w0, w1, w2 = e0 / den, e1 / den, e2 / den
    cols = []
    for h in range(HEADS_PER_GROUP_A):
        hs = slice(h * HEAD_DIM_A, (h + 1) * HEAD_DIM_A)
        cols.append(w0[:, h:h + 1] * os_[0][:, hs] + w1[:, h:h + 1] * os_[1][:, hs] + w2[:, h:h + 1] * os_[2][:, hs])
    oa = jnp.concatenate(cols, axis=1).astype(BF16)
    pa = jnp.dot(oa, wpa_ref[...], preferred_element_type=F32)
    pb = jnp.dot(ob_ref[...].astype(BF16), wpb_ref[...], preferred_element_type=F32)
    pm = jnp.dot(om_ref[...].astype(BF16), wpm_ref[...], preferred_element_type=F32)
    z = (gt_ref[:, 0:d].astype(F32) * pa + gt_ref[:, d:2 * d].astype(F32) * pb
         + gt_ref[:, 2 * d:3 * d].astype(F32) * pm)
    z_ref[...] = z.astype(z_ref.dtype)


def _mix(gates, o_list, dils, seq, ob, om, wpa, wpb, wpm, tm):
    m = gates.shape[0]
    d = wpa.shape[1]
    mt = seq // tm if any(dl > 1 for dl in dils) else 1

    def rows(width):
        return pl.BlockSpec((tm, width), lambda i: (i, 0))

    def whole(a):
        return pl.BlockSpec(a.shape, lambda i: (0, 0))

    o_specs, o_args = [], []
    for o, dil in zip(o_list, dils):
        if dil == 1:
            o_specs.append(rows(O_EXT_W))
            o_args.append(o)
        else:
            assert seq % tm == 0 and (tm // dil) % 16 == 0
            o_specs.append(pl.BlockSpec((None, dil, tm // dil, O_EXT_W), lambda i: (i // mt, 0, i % mt, 0)))
            o_args.append(o.reshape(m // seq, dil, seq // dil, O_EXT_W))
    return pl.pallas_call(
        functools.partial(_mix_body, dils=tuple(dils)), name="mix", grid=(m // tm,),
        in_specs=[rows(gates.shape[1])] + o_specs
                 + [rows(ob.shape[1]), rows(om.shape[1]), whole(wpa), whole(wpb), whole(wpm)],
        out_specs=rows(d),
        out_shape=jax.ShapeDtypeStruct((m, d), BF16), compiler_params=_cparams(1),
    )(gates, *o_args, ob, om, wpa, wpb, wpm)


def _route(logits):
    lane = lax.broadcasted_iota(I32, logits.shape, 1)
    lane_f = lane.astype(F32)
    is_g = lane < N_EXPERT_GROUPS
    gmax = jnp.max(jnp.where(is_g, logits, -jnp.inf), axis=1, keepdims=True)
    gsel = jnp.min(jnp.where(is_g & (logits == gmax), lane_f, float(LANES)), axis=1, keepdims=True).astype(I32)
    gden = jnp.sum(jnp.where(is_g, jnp.exp(logits - gmax), 0.0), axis=1, keepdims=True)
    pg = 1.0 / gden
    e_lane = lane - N_EXPERT_GROUPS
    in_grp = (e_lane >= 0) & (e_lane < N_EXPERTS) & ((e_lane // EXPERTS_PER_GROUP) == gsel)
    m1 = jnp.max(jnp.where(in_grp, logits, -jnp.inf), axis=1, keepdims=True)
    i1 = jnp.min(jnp.where(in_grp & (logits == m1), lane_f, float(LANES)), axis=1, keepdims=True).astype(I32)
    rest = in_grp & (lane != i1)
    m2 = jnp.max(jnp.where(rest, logits, -jnp.inf), axis=1, keepdims=True)
    i2 = jnp.min(jnp.where(rest & (logits == m2), lane_f, float(LANES)), axis=1, keepdims=True).astype(I32)
    e2 = jnp.exp(m2 - m1)
    w1 = pg / (1.0 + e2)
    w2 = pg * e2 / (1.0 + e2)
    eid = jnp.where(lane == 0, i1 - N_EXPERT_GROUPS, jnp.where(lane == 1, i2 - N_EXPERT_GROUPS, 0))
    wts = jnp.where(lane == 0, w1, jnp.where(lane == 1, w2, 0.0))
    return eid, wts


def _resid_body(x_ref, z_ref, wo_ref, gf_ref, wr_ref, br_ref, xmid_ref, hpk_ref, eid_ref, wts_ref):
    xm = x_ref[...] + jnp.dot(z_ref[...], wo_ref[...], preferred_element_type=F32)
    xmid_ref[...] = xm
    hf = _rms(xm, gf_ref[...])
    wr = wr_ref[...]
    wr_hi, wr_lo = _split_hi_lo(wr)
    hf_hi, hf_lo = _split_hi_lo(hf)
    t = jnp.dot(hf_hi, jnp.concatenate([wr_hi, wr_lo], axis=1), preferred_element_type=F32)
    logits = (t[:, :LANES] + t[:, LANES:] + jnp.dot(hf_lo, wr_hi, preferred_element_type=F32) + br_ref[...])
    eid, wts = _route(logits)
    eid_ref[...] = eid
    wts_ref[...] = wts
    _store_token_tiles(hpk_ref, hf)


def _resid(x2d, z, wo, g_ffn, w_r, b_r, tm):
    m, d = x2d.shape

    def rows(width):
        return pl.BlockSpec((tm, width), lambda i: (i, 0))

    def whole(a):
        return pl.BlockSpec(a.shape, lambda i: (0, 0))

    gf = g_ffn.reshape(1, d)
    nt = d // 2 // LANES
    return pl.pallas_call(
        _resid_body, name="resid", grid=(m // tm,),
        in_specs=[rows(d), rows(d), whole(wo), whole(gf), whole(w_r), whole(b_r)],
        out_specs=[rows(d), pl.BlockSpec((tm * nt, LANES), lambda i: (i, 0)), rows(LANES), rows(LANES)],
        out_shape=[jax.ShapeDtypeStruct((m, d), F32), jax.ShapeDtypeStruct((m * nt, LANES), U32),
                   jax.ShapeDtypeStruct((m, LANES), I32), jax.ShapeDtypeStruct((m, LANES), F32)],
        compiler_params=_cparams(1),
    )(x2d, z, wo, gf, w_r, b_r)


TOKEN_TILE = 8
GATHER_UNROLL = 8


def _gather_rows(idx_at, src_hbm, dst, sem, n):
    def body(i, carry):
        src = pl.multiple_of(idx_at(i), TOKEN_TILE)
        dst_row = pl.multiple_of(i * TOKEN_TILE, TOKEN_TILE)
        pltpu.make_async_copy(src_hbm.at[pl.ds(src, TOKEN_TILE)], dst.at[pl.ds(dst_row, TOKEN_TILE)], sem).start()
        return carry
    lax.fori_loop(0, n, body, 0, unroll=GATHER_UNROLL)


FFN_CHUNKS = 8
FFN_BUFS = 3


def _ffn_body(blk_e_ref, nused_ref, eord_ref, enext_ref, idx_hbm, h_hbm, wg_hbm, wu_hbm, wd_hbm, y_ref,
              idx_s, isem, buf, sem, x_scr, a_scr, b_scr, hm_scr, wg_f, wu_f, wd_f, wsem, wg_s, wu_s, wd_s):
    j = pl.program_id(0)
    nused = nused_ref[0]
    last = nused - 1
    rows = buf.shape[1] // TOKEN_TILE
    e = blk_e_ref[j]

    def idx_copy(b):
        s = b % FFN_BUFS
        return pltpu.make_async_copy(idx_hbm.at[jnp.minimum(b, last)], idx_s.at[s], isem.at[s])

    def wait_rows(s):
        pltpu.make_async_copy(h_hbm.at[pl.ds(0, rows * TOKEN_TILE)], buf.at[s], sem.at[s]).wait()
    new_expert = (j == 0) | (e != blk_e_ref[jnp.maximum(j - 1, 0)])
    ws = eord_ref[j] % 2

    def fetch_w(expert, s):
        return [pltpu.make_async_copy(src.at[expert], dst.at[s], wsem.at[s, i])
                for i, (src, dst) in enumerate(((wg_hbm, wg_f), (wu_hbm, wu_f), (wd_hbm, wd_f)))]

    @pl.when(j == 0)
    def _():
        for c in fetch_w(e, ws):
            c.start()
        for b in range(FFN_BUFS):
            idx_copy(b).start()
        for b in range(FFN_BUFS - 1):
            idx_copy(b).wait()
            _gather_rows(lambda i, b=b: idx_s[b, 0, i], h_hbm, buf.at[b], sem.at[b], rows)

    @pl.when(new_expert)
    def _():
        nxt = enext_ref[j]

        @pl.when(nxt >= 0)
        def _():
            for c in fetch_w(nxt, 1 - ws):
                c.start()

        for c in fetch_w(e, ws):
            c.wait()
        wg_s[...] = wg_f[ws].astype(BF16)
        wu_s[...] = wu_f[ws].astype(BF16)
        wd_s[...] = wd_f[ws].astype(BF16)

    @pl.when(j < nused)
    def _():
        slot = j % FFN_BUFS
        s2 = (j + 2) % FFN_BUFS
        per = rows // FFN_CHUNKS
        idx_copy(j + 2).wait()

        @pl.when(j < last)
        def _():
            idx_copy(j + 3).start()

        def issue(c):
            for i in range(c * per, (c + 1) * per):
                src = pl.multiple_of(idx_s[s2, 0, i], TOKEN_TILE)
                pltpu.make_async_copy(h_hbm.at[pl.ds(src, TOKEN_TILE)],
                                      buf.at[s2, pl.ds(i * TOKEN_TILE, TOKEN_TILE)], sem.at[s2]).start()

        wait_rows(slot)
        x_scr[...] = _load_token_tiles(buf, slot, 0, rows, TOKEN_TILE).astype(BF16)
        dn = a_scr.shape[1] // 2
        for c in range(2):
            issue(c)
            a_scr[:, c * dn:(c + 1) * dn] = jnp.dot(x_scr[...], wg_s[:, c * dn:(c + 1) * dn],
                                                     preferred_element_type=F32)
        for c in range(2):
            issue(2 + c)
            b_scr[:, c * dn:(c + 1) * dn] = jnp.dot(x_scr[...], wu_s[:, c * dn:(c + 1) * dn],
                                                     preferred_element_type=F32)
        a = a_scr[...]
        hm_scr[...] = (a * jax.nn.sigmoid(a) * b_scr[...]).astype(BF16)
        half = wd_s.shape[1] // 2
        n_dc = FFN_CHUNKS - 4
        dq = half // n_dc
        for c in range(n_dc):
            issue(4 + c)
            lo = jnp.dot(hm_scr[...], wd_s[:, c * dq:(c + 1) * dq], preferred_element_type=F32)
            hi = jnp.dot(hm_scr[...], wd_s[:, half + c * dq:half + (c + 1) * dq], preferred_element_type=F32)
            w = _pack_bf16_pair(jnp.concatenate([lo, hi], axis=1))
            for t in range(dq // LANES):
                y_ref[pl.ds(c * (dq // LANES) + t, rows, stride=TOKEN_TILE), :] = w[:, t * LANES:(t + 1) * LANES]

        @pl.when(j == last)
        def _():
            wait_rows((j + 1) % FFN_BUFS)
            wait_rows(s2)

    @pl.when(j >= nused)
    def _():
        y_ref[...] = jnp.zeros(y_ref.shape, y_ref.dtype)


def _ffn(hpk, src_tok, blk_e, nused, eord, enext, wg, wu, wd):
    n_blocks = blk_e.shape[0]
    rows = MOE_ROWS
    tiles = rows * TOKEN_TILE
    d, de = wg.shape[1], wg.shape[2]
    assert d // 2 == TOKEN_TILE * LANES and rows % FFN_CHUNKS == 0 and (d // 2) % ((FFN_CHUNKS - 4) * LANES) == 0
    assert n_blocks >= N_EXPERTS + FFN_BUFS
    idx = (src_tok * TOKEN_TILE).reshape(n_blocks, 1, rows)
    hbm = pl.BlockSpec(memory_space=pl.ANY)
    grid_spec = pltpu.PrefetchScalarGridSpec(
        num_scalar_prefetch=4, grid=(n_blocks,),
        in_specs=[hbm, hbm, hbm, hbm, hbm],
        out_specs=pl.BlockSpec((tiles, LANES), lambda j, *_: (j, 0)),
        scratch_shapes=[pltpu.SMEM((FFN_BUFS, 1, rows), I32), pltpu.SemaphoreType.DMA((FFN_BUFS,)),
                        pltpu.VMEM((FFN_BUFS, tiles, LANES), U32), pltpu.SemaphoreType.DMA((FFN_BUFS,)),
                        pltpu.VMEM((rows, d), BF16), pltpu.VMEM((rows, de), F32), pltpu.VMEM((rows, de), F32),
                        pltpu.VMEM((rows, de), BF16),
                        pltpu.VMEM((2, d, de), wg.dtype), pltpu.VMEM((2, d, de), wu.dtype),
                        pltpu.VMEM((2, de, d), wd.dtype), pltpu.SemaphoreType.DMA((2, 3)),
                        pltpu.VMEM((d, de), BF16), pltpu.VMEM((d, de), BF16), pltpu.VMEM((de, d), BF16)],
    )
    return pl.pallas_call(
        _ffn_body, name="ffn", grid_spec=grid_spec,
        out_shape=jax.ShapeDtypeStruct((n_blocks * tiles, LANES), U32), compiler_params=_cparams(1),
    )(blk_e, nused, eord, enext, idx, hpk, wg, wu, wd)


def _combine_body(idx_cur_ref, idx_nxt_ref, x_ref, w_ref, y_hbm, o_ref, buf, sem, *, n):
    i = pl.program_id(0)
    slot = i % 2
    rows = buf.shape[1] // TOKEN_TILE
    tm = rows // TOP_K
    half = o_ref.shape[1] // 2
    per = rows // TOKEN_TILE

    def wait_rows(s):
        pltpu.make_async_copy(y_hbm.at[pl.ds(0, rows * TOKEN_TILE)], buf.at[s], sem.at[s]).wait()

    @pl.when(i == 0)
    def _():
        _gather_rows(lambda r: idx_cur_ref[0, 0, r], y_hbm, buf.at[0], sem.at[0], rows)

    wait_rows(slot)
    w = w_ref[...]
    w0, w1 = w[:, 0:1], w[:, 1:2]
    for c in range(TOKEN_TILE):
        for r in range(c * per, (c + 1) * per):
            src = pl.multiple_of(idx_nxt_ref[0, 0, r], TOKEN_TILE)
            pltpu.make_async_copy(y_hbm.at[pl.ds(src, TOKEN_TILE)],
                                  buf.at[1 - slot, pl.ds(r * TOKEN_TILE, TOKEN_TILE)], sem.at[1 - slot]).start()
        lo0, hi0 = _unpack_bf16_pair(buf[slot, pl.ds(c, tm, stride=TOKEN_TILE), :])
        lo1, hi1 = _unpack_bf16_pair(buf[slot, pl.ds(tm * TOKEN_TILE + c, tm, stride=TOKEN_TILE), :])
        cs = slice(c * LANES, (c + 1) * LANES)
        hs = slice(half + c * LANES, half + (c + 1) * LANES)
        o_ref[:, cs] = x_ref[:, cs] + (w0 * lo0 + w1 * lo1)
        o_ref[:, hs] = x_ref[:, hs] + (w0 * hi0 + w1 * hi1)

    @pl.when(i == n - 1)
    def _():
        wait_rows(1 - slot)


def _combine(xmid, wts, dest, ypk, tm):
    m, d = xmid.shape
    nt = m // tm
    assert d // 2 == TOKEN_TILE * LANES
    idx = (dest * TOKEN_TILE).reshape(nt, tm, TOP_K).transpose(0, 2, 1).reshape(nt, 1, TOP_K * tm)

    def idx_spec(ahead):
        return pl.BlockSpec((1, 1, TOP_K * tm), lambda i: (jnp.minimum(i + ahead, nt - 1), 0, 0),
                            memory_space=pltpu.SMEM)

    return pl.pallas_call(
        functools.partial(_combine_body, n=nt), name="combine", grid=(nt,),
        in_specs=[idx_spec(0), idx_spec(1), pl.BlockSpec((tm, d), lambda i: (i, 0)),
                  pl.BlockSpec((tm, LANES), lambda i: (i, 0)), pl.BlockSpec(memory_space=pl.ANY)],
        out_specs=pl.BlockSpec((tm, d), lambda i: (i, 0)),
        out_shape=jax.ShapeDtypeStruct((m, d), F32),
        scratch_shapes=[pltpu.VMEM((2, TOP_K * tm * TOKEN_TILE, LANES), U32), pltpu.SemaphoreType.DMA((2,))],
        compiler_params=_cparams(1),
    )(idx, idx, xmid, wts, ypk)


def _moe_plan(eid2):
    n_slot = eid2.shape[0] * TOP_K
    eid = eid2.reshape(n_slot)
    onehot = (eid[:, None] == jnp.arange(N_EXPERTS, dtype=I32)[None, :]).astype(I32)
    csum = jnp.cumsum(onehot, axis=0)
    rank = jnp.take_along_axis(csum, eid[:, None], axis=1)[:, 0] - 1
    counts = csum[-1]
    padded = (counts + MOE_ROWS - 1) // MOE_ROWS * MOE_ROWS
    pend = jnp.cumsum(padded)
    pstart = pend - padded
    dest = pstart[eid] + rank
    n_blocks = -(-n_slot // MOE_ROWS) + N_EXPERTS
    nused = (pend[-1] // MOE_ROWS).astype(I32)
    blk = jnp.minimum(jnp.arange(n_blocks, dtype=I32), nused - 1)
    blk_e = jnp.minimum(jnp.searchsorted(pend, blk * MOE_ROWS, side="right"), N_EXPERTS - 1).astype(I32)
    n_rows = n_blocks * MOE_ROWS
    src_tok = (jnp.arange(n_rows, dtype=I32) % (n_slot // TOP_K)).at[dest].set(
        jnp.arange(n_slot, dtype=I32) // TOP_K, unique_indices=True, mode="promise_in_bounds")
    has_rows = counts > 0
    ord_e = jnp.cumsum(has_rows.astype(I32)) - 1
    ids = jnp.arange(N_EXPERTS, dtype=I32)
    later = has_rows[None, :] & (ids[None, :] > ids[:, None])
    next_e = jnp.where(later.any(axis=1), jnp.argmax(later, axis=1), -1).astype(I32)
    return dest.astype(I32), src_tok, blk_e, nused.reshape(1), ord_e[blk_e].astype(I32), next_e[blk_e]


def _mixer(x2d, tm, p, attn_fn, gmlp_fn, mem_fn):
    h = _prep(x2d, p["g_mix"], tm)
    qdt = p["q_dtype"]
    wi, sec = p["w_in"], p["sec"]
    o_list, dils, seq = attn_fn(h)
    n_ub = sec["vb"] - sec["u"]
    ub = _proj(h, wi, (sec["u"], n_ub, n_ub, 1), tm, "none", BF16, name="proj_ub")
    n_vb = sec["qm"] - sec["vb"]
    vb = _proj(h, wi, (sec["vb"], n_vb, n_vb, 1), tm, "norm", p["vb_dtype"], p["g_vb"], n_vb * W_COLS,
               name="proj_vb")
    n_qm = sec["gt"] - sec["qm"]
    qm = _proj(h, wi, (sec["qm"], n_qm, n_qm, 1), tm, "norm", qdt, p["g_qm"], HEAD_DIM_M,
               HEAD_DIM_M ** -0.5, name="proj_qm")
    gates = _proj(h, wi, (sec["gt"], 2, 2, (sec["end"] - sec["gt"]) // 2), tm, "sigmoid", BF16, name="proj_gates")
    ob = gmlp_fn(ub, vb)
    om = mem_fn(qm)
    z = _mix(gates, o_list, dils, seq, ob, om, p["w_pa"], p["w_pb"], p["w_pm"], min(tm, 256))
    xmid, hpk, eid, wts = _resid(x2d, z, p["w_o"], p["g_ffn"], p["w_r"], p["b_r"], min(tm, 256))
    return vb, xmid, hpk, eid, wts


def kernel(x_prompt, x_sample, mem_prompt, cache_a0_kv, cache_a1_kv, cache_a2_kv, cache_mem_kv, rel_bias, g_mix, w_in, g_qa, g_ka, w_pa, g_vb, w_s, b_s, w_pb, g_mem, w_mk, w_mv, g_qm, g_km, w_pm, w_o, g_ffn, w_rg, b_rg, w_re, b_re, w_gate, w_up, w_down):
    n_b, seq, d = x_prompt.shape
    n_s, t_s, _ = x_sample.shape
    depth = w_in.shape[0]
    assert depth == 1
    l = 0
    caches = (cache_a0_kv, cache_a1_kv, cache_a2_kv)
    n_g = len(DIL_GROUPS)
    wb = w_pb.shape[1]
    wm = N_HEADS_M * HEAD_DIM_M

    offs = [0, WIDTH_A, 2 * WIDTH_A, 3 * WIDTH_A, 3 * WIDTH_A + wb, 3 * WIDTH_A + 2 * wb,
            3 * WIDTH_A + 2 * wb + wm, w_in.shape[2]]
    assert all(o % W_COLS == 0 for o in offs) and (offs[7] - offs[6]) % (2 * W_COLS) == 0
    sec = dict(zip(("q", "k", "v", "u", "vb", "qm", "gt", "end"), (o // W_COLS for o in offs)))
    p = {
        "g_mix": g_mix[l], "g_qa": g_qa[l], "g_ka": g_ka[l], "g_vb": g_vb[l], "g_qm": g_qm[l], "g_ffn": g_ffn[l],
        "w_in": w_in[l], "sec": sec,
        "w_pa": w_pa[l].astype(BF16), "w_pb": w_pb[l].astype(BF16), "w_pm": w_pm[l].astype(BF16),
        "w_o": w_o[l].astype(BF16),
    }
    n_r = N_EXPERT_GROUPS + N_EXPERTS
    p["w_r"] = jnp.concatenate([w_rg[l], w_re[l], jnp.zeros((d, LANES - n_r), F32)], axis=1)
    p["b_r"] = jnp.concatenate([b_rg[l], b_re[l], jnp.zeros((LANES - n_r,), F32)]).reshape(1, LANES)

    n_mem = mem_prompt.shape[1]
    h_mem = _prep(mem_prompt.reshape(n_b * n_mem, d), g_mem[l], 256)
    w_mkv = jnp.concatenate([w_mk[l], w_mv[l]], axis=1)
    n_mw = wm // W_COLS
    mkv = _proj(h_mem, w_mkv, (0, n_mw, n_mw, 2), 256, "norm_first", F32, g_km[l], HEAD_DIM_M, name="proj_mkv")
    new_mem_p = mkv.reshape(1, n_b, n_mem, 2, N_HEADS_M, HEAD_DIM_M)

    ps = dict(p, q_dtype=F32, vb_dtype=F32)
    m_s = n_s * t_s
    n_col = 2 * HEADS_PER_GROUP_A
    new_s = []

    def attn_sample(h):
        o_list = []
        for g in range(n_g):
            q, kv, tok = _proj_qkv(h, w_in[l], sec, g, g_qa[l], g_ka[l], 1, m_s, m_s, 1, act_dtype=F32)
            cache3 = caches[g][l].reshape(n_s, caches[g].shape[2] * n_col, LANES)
            o, newc = _attn_sample(q, kv, tok, cache3, rel_bias, g, t_s)
            o_list.append(o)
            new_s.append(newc.reshape(caches[g][l:l + 1].shape))
        return o_list, [1] * n_g, t_s

    n_ct = HEAD_DIM_M // LANES
    mem_tiles = (cache_mem_kv[l].reshape(n_s, n_mem, 2, N_HEADS_M, n_ct, LANES).transpose(0, 1, 2, 4, 3, 5)
                 .reshape(n_s, n_mem * 2 * n_ct * N_HEADS_M, LANES))
    vb_s, xmid_s, hpk_s, eid_s, wts_s = _mixer(
        x_sample.reshape(m_s, d), m_s, ps, attn_sample,
        lambda u, v: _gmlp_new(u, v, w_s[l], b_s[l], t_s),
        lambda qm: _memattn(qm, mem_tiles, t_s, F32, tiles=True, reqs=4))
    new_vb_s = vb_s.reshape(1, n_s, t_s, wb)

    pp = dict(p, q_dtype=BF16, vb_dtype=BF16)
    new_p = []

    def attn_prompt(h):
        o_list = []
        for g, (win, dil) in enumerate(DIL_GROUPS):
            q, kv, tok = _proj_qkv(h, w_in[l], sec, g, g_qa[l], g_ka[l], n_b, seq, 512, dil)
            o_list.append(_attn_prompt(q, kv, rel_bias, g, n_b, seq))
            keep = min(win, seq)
            kv5 = tok.reshape(n_b, seq, 2, HEADS_PER_GROUP_A, HEAD_DIM_A)
            new_p.append((kv5 if keep == seq else kv5[:, seq - keep:])[None])
        return o_list, [dl for _, dl in DIL_GROUPS], seq

    _, xmid_p, hpk_p, eid_p, wts_p = _mixer(
        x_prompt.reshape(n_b * seq, d), 512, pp, attn_prompt,
        lambda u, v: _gmlp(u, v, w_s[l], b_s[l].T, 512, CHUNK),
        lambda qm: _memattn(qm, mkv.reshape(n_b, n_mem, 2 * wm), 512, BF16))

    n_p = n_b * seq
    hpk = jnp.concatenate([hpk_p, hpk_s], axis=0)
    eid2 = jnp.concatenate([eid_p[:, :TOP_K], eid_s[:, :TOP_K]], axis=0)
    dest, src_tok, blk_e, nused, eord, enext = _moe_plan(eid2)
    ypk = _ffn(hpk, src_tok, blk_e, nused, eord, enext, w_gate[l], w_up[l], w_down[l])
    dest2 = dest.reshape(-1, TOP_K)
    y_p = _combine(xmid_p, wts_p, dest2[:n_p], ypk, 256)
    y_s = _combine(xmid_s, wts_s, dest2[n_p:], ypk, m_s)

    return (y_p.reshape(n_b, seq, d), y_s.reshape(n_s, t_s, d), new_p[0], new_p[1], new_p[2], new_mem_p,
            new_s[0], new_s[1], new_s[2], new_vb_s)
```

```python
import functools
import math

import jax
import jax.numpy as jnp
from jax import lax
from jax.experimental import pallas as pl
from jax.experimental.pallas import tpu as pltpu

F32 = jnp.float32
BF16 = jnp.bfloat16
I32 = jnp.int32
U32 = jnp.uint32

EPS = 1e-6
NEG_INF = -1e30

HEAD_DIM_A = 128
HEADS_PER_GROUP_A = 4
DIL_GROUPS = ((128, 1), (512, 4), (2048, 16))
GROUP_W_A = HEADS_PER_GROUP_A * HEAD_DIM_A
WIDTH_A = len(DIL_GROUPS) * GROUP_W_A
CHUNK = 128
N_GROUPS_B = 8
N_HEADS_M = 4
HEAD_DIM_M = 256
N_BUCKETS = 32
MAX_EXACT = N_BUCKETS // 2
MAX_DISTANCE = 2048
N_EXPERT_GROUPS = 4
EXPERTS_PER_GROUP = 8
N_EXPERTS = N_EXPERT_GROUPS * EXPERTS_PER_GROUP
TOP_K = 2
LANES = 128
MOE_ROWS = 256
VMEM_LIMIT = 56 * 1024 * 1024


def _cparams(n_grid, vmem=VMEM_LIMIT):
    return pltpu.CompilerParams(dimension_semantics=("arbitrary",) * n_grid, vmem_limit_bytes=vmem)


def _rms(x, g):
    return x * lax.rsqrt(jnp.mean(x * x, axis=-1, keepdims=True) + EPS) * g


def _pack_bf16_pair(x):
    n = x.shape[1] // 2
    lo = lax.bitcast_convert_type(x[:, :n].astype(BF16).astype(F32), U32)
    hi = lax.bitcast_convert_type(x[:, n:].astype(BF16).astype(F32), U32)
    return (hi & jnp.uint32(0xFFFF0000)) | (lo >> 16)


def _unpack_bf16_pair(w):
    lo = lax.bitcast_convert_type(w << 16, F32)
    hi = lax.bitcast_convert_type(w & jnp.uint32(0xFFFF0000), F32)
    return lo, hi


def _store_token_tiles(ref, x):
    w = _pack_bf16_pair(x)
    m, n = w.shape
    nt = n // LANES
    for c in range(nt):
        ref[pl.ds(c, m, stride=nt), :] = w[:, c * LANES:(c + 1) * LANES]


def _load_token_tiles(ref, lead, row0, m, nt):
    los, his = [], []
    view = ref if lead is None else ref.at[lead]
    for c in range(nt):
        lo, hi = _unpack_bf16_pair(view[pl.ds(row0 * nt + c, m, stride=nt), :])
        los.append(lo)
        his.append(hi)
    return jnp.concatenate(los + his, axis=1)


def _rel_bucket(dist):
    d = jnp.maximum(dist, 1).astype(F32)
    large = MAX_EXACT + (jnp.log(d / MAX_EXACT) / math.log(MAX_DISTANCE / MAX_EXACT)
                         * (N_BUCKETS - MAX_EXACT)).astype(I32)
    return jnp.where(dist < MAX_EXACT, dist, jnp.minimum(large, N_BUCKETS - 1)).astype(I32)


def _prep_body(x_ref, g_ref, o_ref):
    o_ref[...] = _rms(x_ref[...], g_ref[...]).astype(o_ref.dtype)


def _prep(x2d, g, tm):
    m, d = x2d.shape
    return pl.pallas_call(
        _prep_body, name="prep", grid=(m // tm,),
        in_specs=[pl.BlockSpec((tm, d), lambda i: (i, 0)), pl.BlockSpec((1, d), lambda i: (0, 0))],
        out_specs=pl.BlockSpec((tm, d), lambda i: (i, 0)),
        out_shape=jax.ShapeDtypeStruct((m, d), BF16), compiler_params=_cparams(1),
    )(x2d, g.reshape(1, d))


def _headnorm(acc, g, hd, scale):
    outs = []
    for j in range(acc.shape[1] // hd):
        sl = acc[:, j * hd:(j + 1) * hd]
        outs.append(sl * lax.rsqrt(jnp.mean(sl * sl, axis=-1, keepdims=True) + EPS))
    y = outs[0] if len(outs) == 1 else jnp.concatenate(outs, axis=1)
    y = y * g
    return y * scale if scale != 1.0 else y


def _proj_body(h_ref, *refs, n_w, mode, hd, scale):
    w_refs, (g_ref, o_ref, w_scr) = refs[:n_w], refs[n_w:]
    cw = w_refs[0].shape[1]

    @pl.when(pl.program_id(1) == 0)
    def _():
        for i, w_ref in enumerate(w_refs):
            w_scr[:, i * cw:(i + 1) * cw] = w_ref[...].astype(w_scr.dtype)

    def run(epilogue):
        acc = jnp.dot(h_ref[...], w_scr[...], preferred_element_type=F32)
        o_ref[...] = epilogue(acc).astype(o_ref.dtype)

    normed = lambda acc: _headnorm(acc, g_ref[...], hd, scale)
    if mode == "none":
        run(lambda acc: acc)
    elif mode == "sigmoid":
        run(jax.nn.sigmoid)
    elif mode == "norm":
        run(normed)
    else:
        @pl.when(pl.program_id(0) == 0)
        def _():
            run(normed)

        @pl.when(pl.program_id(0) != 0)
        def _():
            run(lambda acc: acc)


W_COLS = 512


def _proj(h, w, cols, tm, mode, out_dtype, gain=None, hd=None, scale=1.0, name="proj"):
    m, k = h.shape
    base, stride, n_w, n_tiles = cols
    tn = n_w * W_COLS
    if gain is None:
        g = jnp.ones((1, tn), F32)
    else:
        g = jnp.tile(gain.astype(F32).reshape(1, -1), (1, tn // gain.shape[-1]))
    body = functools.partial(_proj_body, n_w=n_w, mode=mode, hd=hd, scale=scale)
    w_specs = [pl.BlockSpec((k, W_COLS), functools.partial(lambda j, i, off: (0, base + stride * j + off), off=off))
               for off in range(n_w)]
    return pl.pallas_call(
        body, name=name, grid=(n_tiles, m // tm),
        in_specs=[pl.BlockSpec((tm, k), lambda j, i: (i, 0))] + w_specs + [pl.BlockSpec((1, tn), lambda j, i: (0, 0))],
        out_specs=pl.BlockSpec((tm, tn), lambda j, i: (i, j)),
        out_shape=jax.ShapeDtypeStruct((m, n_tiles * tn), out_dtype),
        scratch_shapes=[pltpu.VMEM((k, tn), BF16)], compiler_params=_cparams(2),
    )(h, *([w] * n_w), g)


PERM_ROWS = 256


def _perm_matrix(n, dil, inverse=False):
    per = n // dil
    o = lax.broadcasted_iota(I32, (n, n), 1 if inverse else 0)
    s = lax.broadcasted_iota(I32, (n, n), 0 if inverse else 1)
    return (s == (o % per) * dil + o // per).astype(BF16)


def _qkv_body(h_ref, wq_ref, wk_ref, wv_ref, gq_ref, gk_ref, q_ref, kv_ref, tok_ref, w_scr, *, dil, scale):
    tm = h_ref.shape[0]
    nh = HEADS_PER_GROUP_A
    gw = GROUP_W_A

    @pl.when(pl.program_id(0) == 0)
    def _():
        for i, w_ref in enumerate((wq_ref, wk_ref, wv_ref)):
            w_scr[:, i * gw:(i + 1) * gw] = w_ref[...].astype(BF16)

    acc = jnp.dot(h_ref[...], w_scr[...], preferred_element_type=F32)
    q = _headnorm(acc[:, :gw], gq_ref[...], HEAD_DIM_A, scale)
    k = _headnorm(acc[:, gw:2 * gw], gk_ref[...], HEAD_DIM_A, 1.0)
    v = acc[:, 2 * gw:]
    for i in range(nh):
        tok_ref[pl.ds(i, tm, stride=2 * nh), :] = k[:, i * HEAD_DIM_A:(i + 1) * HEAD_DIM_A]
        tok_ref[pl.ds(nh + i, tm, stride=2 * nh), :] = v[:, i * HEAD_DIM_A:(i + 1) * HEAD_DIM_A]

    if dil == 1:
        q_ref[0] = q.astype(q_ref.dtype)
        kv_ref[0, :, :gw] = k.astype(kv_ref.dtype)
        kv_ref[0, :, gw:] = v.astype(kv_ref.dtype)
        return
    sub = min(tm, PERM_ROWS)
    per = sub // dil
    pm = _perm_matrix(sub, dil)
    qkv = jnp.concatenate([q, k, v], axis=1).astype(BF16)
    for t in range(tm // sub):
        yp = jnp.dot(pm, qkv[t * sub:(t + 1) * sub], preferred_element_type=F32).astype(BF16)
        for r in range(dil):
            q_ref[r, t * per:(t + 1) * per, :] = yp[r * per:(r + 1) * per, :gw]
            kv_ref[r, t * per:(t + 1) * per, :] = yp[r * per:(r + 1) * per, gw:]


def _proj_qkv(h, w, sec, g, g_qa, g_ka, n_batch, seq, tm, dil, act_dtype=BF16):
    n, k = h.shape
    ln = seq // dil
    per = tm // dil
    mt = seq // tm
    assert seq % tm == 0
    assert dil == 1 or (act_dtype == BF16 and tm % PERM_ROWS == 0 and (PERM_ROWS // dil) % 16 == 0)
    tile = lambda v: jnp.tile(v.astype(F32).reshape(1, -1), (1, GROUP_W_A // v.shape[-1]))
    assert W_COLS == GROUP_W_A
    wspec = lambda blk: pl.BlockSpec((k, W_COLS), lambda m: (0, blk))
    q, kv, tok = pl.pallas_call(
        functools.partial(_qkv_body, dil=dil, scale=HEAD_DIM_A ** -0.5), name=f"proj_qkv{g}", grid=(n // tm,),
        in_specs=[pl.BlockSpec((tm, k), lambda m: (m, 0)),
                  wspec(sec["q"] + g), wspec(sec["k"] + g), wspec(sec["v"] + g),
                  pl.BlockSpec((1, GROUP_W_A), lambda m: (0, 0)), pl.BlockSpec((1, GROUP_W_A), lambda m: (0, 0))],
        out_specs=[pl.BlockSpec((None, dil, per, GROUP_W_A), lambda m: (m // mt, 0, m % mt, 0)),
                   pl.BlockSpec((None, dil, per, 2 * GROUP_W_A), lambda m: (m // mt, 0, m % mt, 0)),
                   pl.BlockSpec((tm * 2 * HEADS_PER_GROUP_A, LANES), lambda m: (m, 0))],
        out_shape=[jax.ShapeDtypeStruct((n_batch, dil, ln, GROUP_W_A), act_dtype),
                   jax.ShapeDtypeStruct((n_batch, dil, ln, 2 * GROUP_W_A), act_dtype),
                   jax.ShapeDtypeStruct((n * 2 * HEADS_PER_GROUP_A, LANES), F32)],
        scratch_shapes=[pltpu.VMEM((k, 3 * W_COLS), BF16)], compiler_params=_cparams(1),
    )(h, w, w, w, tile(g_qa), tile(g_ka))
    return q.reshape(n, GROUP_W_A), kv.reshape(n, 2 * GROUP_W_A), tok


def _bias_from_table(tab_ref, bidx, col):
    acc = jnp.zeros(bidx.shape, F32)
    for kb in range(N_BUCKETS):
        acc = jnp.where(bidx == kb, tab_ref[kb, col], acc)
    return acc


def _split_hi_lo(x):
    hi = x.astype(BF16)
    return hi, (x - hi.astype(F32)).astype(BF16)


ATTN_P_BLOCKS = 4


def _attn_p_body(tab_ref, bidx_ref, q_ref, kp_ref, kc_ref, vp_ref, vc_ref, o_ref, bias_scr, *, g, nb):
    i = pl.program_id(0)

    @pl.when(i == 0)
    def _():
        bidx = bidx_ref[...]
        for h in range(HEADS_PER_GROUP_A):
            bias_scr[h] = _bias_from_table(tab_ref, bidx, g * HEADS_PER_GROUP_A + h)

    blk = kp_ref.shape[0]
    k_all = jnp.concatenate([kp_ref[...], kc_ref[...]], axis=0)
    v_all = jnp.concatenate([vp_ref[...], vc_ref[...]], axis=0)
    row = lax.broadcasted_iota(I32, (blk, 2 * blk), 0)
    col = lax.broadcasted_iota(I32, (blk, 2 * blk), 1)
    dist = blk + row - col
    band = (dist >= 0) & (dist <= blk)
    lane = lax.broadcasted_iota(I32, (blk, LANES), 1)
    for u in range(ATTN_P_BLOCKS):
        has_prev = ((i * ATTN_P_BLOCKS + u) % nb) > 0
        valid = band & (has_prev | (col >= blk))
        q = q_ref[u * blk:(u + 1) * blk, :]
        k = k_all[u * blk:(u + 2) * blk]
        v = v_all[u * blk:(u + 2) * blk]
        lse_blk = jnp.zeros((blk, LANES), F32)
        outs = []
        for h in range(HEADS_PER_GROUP_A):
            hs = slice(h * HEAD_DIM_A, (h + 1) * HEAD_DIM_A)
            s = lax.dot_general(q[:, hs], k[:, hs], (((1,), (1,)), ((), ())), preferred_element_type=F32)
            s = jnp.where(valid, s + bias_scr[h], NEG_INF)
            m = jnp.max(s, axis=-1, keepdims=True)
            p = jnp.exp(s - m)
            l = jnp.sum(p, axis=-1, keepdims=True)
            o = jnp.dot(p.astype(BF16), v[:, hs], preferred_element_type=F32) / l
            outs.append(o)
            lse_blk = jnp.where(lane == h, m + jnp.log(l), lse_blk)
        lse_hi, lse_lo = _split_hi_lo(lse_blk)
        o_ref[u * blk:(u + 1) * blk, :] = jnp.concatenate(
            [jnp.concatenate(outs, axis=1).astype(BF16), lse_hi, lse_lo], axis=1)


O_EXT_W = GROUP_W_A + 2 * LANES


def _attn_prompt(q, kv, rel_bias, g, n_batch, seq):
    win, dil = DIL_GROUPS[g]
    blk = win // dil
    nb = seq // dil // blk
    n = n_batch * seq
    a = jnp.arange(blk)[:, None]
    c = jnp.arange(2 * blk)[None, :]
    bidx = _rel_bucket(jnp.maximum(blk + a - c, 0) * dil)

    nu = ATTN_P_BLOCKS
    assert (n // blk) % nu == 0 and (nb % nu == 0 or nb == 1)

    def prev(i):
        return jnp.maximum(i * nu - 1, 0)

    return pl.pallas_call(
        functools.partial(_attn_p_body, g=g, nb=nb), name=f"attn_p{g}", grid=(n // blk // nu,),
        in_specs=[
            pl.BlockSpec(memory_space=pltpu.SMEM),
            pl.BlockSpec((blk, 2 * blk), lambda i: (0, 0)),
            pl.BlockSpec((nu * blk, GROUP_W_A), lambda i: (i, 0)),
            pl.BlockSpec((blk, GROUP_W_A), lambda i: (prev(i), 0)),
            pl.BlockSpec((nu * blk, GROUP_W_A), lambda i: (i, 0)),
            pl.BlockSpec((blk, GROUP_W_A), lambda i: (prev(i), 1)),
            pl.BlockSpec((nu * blk, GROUP_W_A), lambda i: (i, 1)),
        ],
        out_specs=pl.BlockSpec((nu * blk, O_EXT_W), lambda i: (i, 0)),
        out_shape=jax.ShapeDtypeStruct((n, O_EXT_W), BF16),
        scratch_shapes=[pltpu.VMEM((HEADS_PER_GROUP_A, blk, 2 * blk), F32)],
        compiler_params=_cparams(1),
    )(rel_bias, bidx, q, kv, kv, kv, kv)


def _attn_s_body(tab_ref, bidx_ref, q_ref, kvn_ref, tokn_ref, cache_hbm, o_ref, newc_hbm,
                 raw, kv_scr, bias_scr, sem_in, sem_out, *, g, dil, lb, n_req):
    t_new = q_ref.shape[0]
    nh = HEADS_PER_GROUP_A
    n_rows = nh * t_new
    n_keys = kv_scr.shape[0]
    n_col = 2 * nh
    lb8 = lb * n_col
    n = pl.program_id(0)
    nbuf = raw.shape[0]
    ahead = nbuf // 2
    slot = n % nbuf

    def fetch(req):
        s = req % nbuf
        return pltpu.make_async_copy(cache_hbm.at[req], raw.at[s, pl.ds(0, lb8)], sem_in.at[s])

    def flush(req):
        s = req % nbuf
        return pltpu.make_async_copy(raw.at[s, pl.ds(t_new * n_col, lb8)], newc_hbm.at[req], sem_out.at[s])

    @pl.when(n == 0)
    def _():
        kv_scr[lb:, :] = jnp.zeros((n_keys - lb, kv_scr.shape[1]), BF16)
        bidx = bidx_ref[...]
        for h in range(nh):
            bias_scr[h * t_new:(h + 1) * t_new, :] = _bias_from_table(tab_ref, bidx, g * nh + h)
        for r in range(ahead):
            fetch(r).start()

    @pl.when(n >= nbuf - ahead)
    def _():
        flush(n - (nbuf - ahead)).wait()

    @pl.when(n + ahead < n_req)
    def _():
        fetch(n + ahead).start()

    fetch(n).wait()
    raw[slot, lb8:lb8 + t_new * n_col, :] = tokn_ref[...]
    flush(n).start()
    new = kvn_ref[...]

    rows_per = min(lb, 256)
    for c in range(n_col):
        for r0 in range(0, lb, rows_per):
            kv_scr[r0:r0 + rows_per, c * LANES:(c + 1) * LANES] = raw[
                slot, pl.ds(r0 * n_col + c, rows_per, stride=n_col), :].astype(BF16)
    kv_scr[lb:lb + 2 * t_new, :] = jnp.concatenate([new, jnp.zeros_like(new)], axis=0).astype(BF16)

    q = q_ref[...]
    qt = jnp.concatenate([q] * nh, axis=0)
    rr = lax.broadcasted_iota(I32, (n_rows, GROUP_W_A), 0)
    cc = lax.broadcasted_iota(I32, (n_rows, GROUP_W_A), 1)
    qbd = jnp.where(cc // HEAD_DIM_A == rr // t_new, qt, 0.0).astype(BF16)

    s = lax.dot_general(qbd, kv_scr[:, :GROUP_W_A], (((1,), (1,)), ((), ())), preferred_element_type=F32)
    row = lax.broadcasted_iota(I32, (n_rows, n_keys), 0)
    col = lax.broadcasted_iota(I32, (n_rows, n_keys), 1)
    delta = lb + (row & (t_new - 1)) - col
    valid = (delta >= 0) & (delta <= lb) & ((delta & (dil - 1)) == 0)
    s = jnp.where(valid, s + bias_scr[...], NEG_INF)
    m = jnp.max(s, axis=-1, keepdims=True)
    p = jnp.exp(s - m)
    l = jnp.sum(p, axis=-1, keepdims=True)
    o = jnp.dot(p.astype(BF16), kv_scr[:, GROUP_W_A:], preferred_element_type=F32) / l
    lse = m + jnp.log(l)
    lane = lax.broadcasted_iota(I32, (t_new, LANES), 1)
    lse_blk = jnp.zeros((t_new, LANES), F32)
    outs = []
    for h in range(nh):
        outs.append(o[h * t_new:(h + 1) * t_new, h * HEAD_DIM_A:(h + 1) * HEAD_DIM_A])
        lse_blk = jnp.where(lane == h, lse[h * t_new:(h + 1) * t_new, :], lse_blk)
    o_ref[...] = jnp.concatenate(outs + [lse_blk, jnp.zeros_like(lse_blk)], axis=1)

    @pl.when(n == n_req - 1)
    def _():
        for back in range(nbuf - ahead - 1, -1, -1):
            flush(n - back).wait()


ATTN_S_BUFS = 4


def _attn_sample(q, kv_new, tok_new, cache, rel_bias, g, t_new):
    win, dil = DIL_GROUPS[g]
    n_col = 2 * HEADS_PER_GROUP_A
    n_req, lb = cache.shape[0], cache.shape[1] // n_col
    assert t_new & (t_new - 1) == 0 and dil & (dil - 1) == 0 and lb == win
    n_keys = lb + LANES
    t = jnp.arange(t_new)[:, None]
    c = jnp.arange(n_keys)[None, :]
    bidx = _rel_bucket(jnp.clip(lb + t - c, 0, lb))
    return pl.pallas_call(
        functools.partial(_attn_s_body, g=g, dil=dil, lb=lb, n_req=n_req), name=f"attn_s{g}", grid=(n_req,),
        in_specs=[
            pl.BlockSpec(memory_space=pltpu.SMEM),
            pl.BlockSpec((t_new, n_keys), lambda n: (0, 0)),
            pl.BlockSpec((t_new, GROUP_W_A), lambda n: (n, 0)),
            pl.BlockSpec((t_new, 2 * GROUP_W_A), lambda n: (n, 0)),
            pl.BlockSpec((t_new * n_col, LANES), lambda n: (n, 0)),
            pl.BlockSpec(memory_space=pl.ANY),
        ],
        out_specs=[pl.BlockSpec((t_new, O_EXT_W), lambda n: (n, 0)), pl.BlockSpec(memory_space=pl.ANY)],
        out_shape=[jax.ShapeDtypeStruct((n_req * t_new, O_EXT_W), F32), jax.ShapeDtypeStruct(cache.shape, F32)],
        scratch_shapes=[pltpu.VMEM((ATTN_S_BUFS, (lb + t_new) * n_col, LANES), F32),
                        pltpu.VMEM((n_keys, 2 * GROUP_W_A), BF16),
                        pltpu.VMEM((HEADS_PER_GROUP_A * t_new, n_keys), F32),
                        pltpu.SemaphoreType.DMA((ATTN_S_BUFS,)), pltpu.SemaphoreType.DMA((ATTN_S_BUFS,))],
        compiler_params=_cparams(1),
    )(rel_bias, bidx, q, kv_new, tok_new, cache)


def _gmlp_body(u_ref, v_ref, w_ref, b_ref, o_ref, *, period):
    c = w_ref.shape[1]
    nch = u_ref.shape[0] // c
    gd = u_ref.shape[1] // N_GROUPS_B
    i = lax.broadcasted_iota(I32, (c, c), 0)
    j = lax.broadcasted_iota(I32, (c, c), 1)
    mask = (j <= i) & ((i // period) == (j // period))
    b = b_ref[...]
    for g in range(N_GROUPS_B):
        gs = slice(g * gd, (g + 1) * gd)
        wg = jnp.where(mask, w_ref[g], 0.0).astype(BF16)
        vg = [v_ref[ch * c:(ch + 1) * c, gs].astype(BF16) for ch in range(nch)]
        vg = vg[0] if nch == 1 else jnp.concatenate(vg, axis=1)
        sg = jnp.dot(wg, vg, preferred_element_type=F32) + b[:, g:g + 1]
        for ch in range(nch):
            u = u_ref[ch * c:(ch + 1) * c, gs].astype(F32)
            o_ref[ch * c:(ch + 1) * c, gs] = (u * sg[:, ch * gd:(ch + 1) * gd]).astype(o_ref.dtype)


def _gmlp(u, v, w, b, tm, period):
    m, wb = u.shape
    c = w.shape[1]
    return pl.pallas_call(
        functools.partial(_gmlp_body, period=period), name="gmlp", grid=(m // tm,),
        in_specs=[pl.BlockSpec((tm, wb), lambda i: (i, 0)), pl.BlockSpec((tm, wb), lambda i: (i, 0)),
                  pl.BlockSpec((N_GROUPS_B, c, c), lambda i: (0, 0, 0)),
                  pl.BlockSpec((c, N_GROUPS_B), lambda i: (0, 0))],
        out_specs=pl.BlockSpec((tm, wb), lambda i: (i, 0)),
        out_shape=jax.ShapeDtypeStruct((m, wb), BF16), compiler_params=_cparams(1),
    )(u, v, w, b)


def _gmlp_new_body(u_ref, v_ref, w_ref, b_ref, o_ref):
    t = w_ref.shape[0]
    n = u_ref.shape[0] // t
    width = u_ref.shape[1]
    v = v_ref[...].astype(F32).reshape(n, t, width)
    sg = jnp.broadcast_to(b_ref[...][None], (n, t, width))
    p_row = lax.broadcasted_iota(I32, (t, width), 0)
    for q in range(t):
        sg = sg + jnp.where(p_row >= q, w_ref[q], 0.0)[None] * v[:, q:q + 1, :]
    o_ref[...] = (u_ref[...].astype(F32).reshape(n, t, width) * sg).reshape(n * t, width).astype(o_ref.dtype)


def _gmlp_new(u, v, w_s, b_s, t):
    m, wb = u.shape
    gd = wb // N_GROUPS_B
    wq = jnp.repeat(jnp.transpose(w_s[:, :t, :t], (2, 1, 0)), gd, axis=-1)
    bq = jnp.repeat(b_s[:, :t].T, gd, axis=-1)
    whole = lambda a: pl.BlockSpec(a.shape, lambda i: (0,) * a.ndim)
    return pl.pallas_call(
        _gmlp_new_body, name="gmlp_new", grid=(1,),
        in_specs=[whole(u), whole(v), whole(wq), whole(bq)], out_specs=whole(u),
        out_shape=jax.ShapeDtypeStruct((m, wb), BF16), compiler_params=_cparams(1),
    )(u, v, wq, bq)


def _memattn_body(q_ref, kv_ref, o_ref, *, tiles):
    wm = N_HEADS_M * HEAD_DIM_M
    n_ct = HEAD_DIM_M // LANES
    per_key = 2 * n_ct * N_HEADS_M
    reqs = kv_ref.shape[0]
    tq = q_ref.shape[0] // reqs

    for r in range(reqs):
        kvr = kv_ref.at[r]

        def head_slab(kv, h):
            if not tiles:
                return kvr[:, kv * wm + h * HEAD_DIM_M:kv * wm + (h + 1) * HEAD_DIM_M].astype(BF16)
            n_keys = kvr.shape[0] // per_key
            parts = [kvr[pl.ds((kv * n_ct + ct) * N_HEADS_M + h, n_keys, stride=per_key), :] for ct in range(n_ct)]
            return jnp.concatenate(parts, axis=1).astype(BF16)

        q = q_ref[r * tq:(r + 1) * tq, :].astype(BF16)
        outs = []
        for h in range(N_HEADS_M):
            hs = slice(h * HEAD_DIM_M, (h + 1) * HEAD_DIM_M)
            k = head_slab(0, h)
            v = head_slab(1, h)
            s = lax.dot_general(q[:, hs], k, (((1,), (1,)), ((), ())), preferred_element_type=F32)
            m = jnp.max(s, axis=-1, keepdims=True)
            p = jnp.exp(s - m)
            l = jnp.sum(p, axis=-1, keepdims=True)
            outs.append(jnp.dot(p.astype(BF16), v, preferred_element_type=F32) / l)
        o_ref[r * tq:(r + 1) * tq, :] = jnp.concatenate(outs, axis=1).astype(o_ref.dtype)


def _memattn(q, kv, tq, out_dtype, tiles=False, reqs=1):
    m, wm = q.shape
    n = kv.shape[0]
    per = m // n // tq
    assert reqs == 1 or (per == 1 and n % reqs == 0)
    kv_spec = pl.BlockSpec((reqs,) + kv.shape[1:], lambda i: (i // per,) + (0,) * (kv.ndim - 1))
    tq = tq * reqs
    return pl.pallas_call(
        functools.partial(_memattn_body, tiles=tiles), name="memattn", grid=(m // tq,),
        in_specs=[pl.BlockSpec((tq, wm), lambda i: (i, 0)), kv_spec],
        out_specs=pl.BlockSpec((tq, wm), lambda i: (i, 0)),
        out_shape=jax.ShapeDtypeStruct((m, wm), out_dtype), compiler_params=_cparams(1),
    )(q, kv)


def _mix_body(gt_ref, o0_ref, o1_ref, o2_ref, ob_ref, om_ref, wpa_ref, wpb_ref, wpm_ref, z_ref, *, dils):
    d = z_ref.shape[1]
    tm = z_ref.shape[0]
    os_, ls = [], []
    for o_ref, dil in zip((o0_ref, o1_ref, o2_ref), dils):
        if dil == 1:
            x = o_ref[...].astype(F32)
        else:
            rows = jnp.concatenate([o_ref[r] for r in range(dil)], axis=0)
            x = jnp.dot(_perm_matrix(tm, dil, inverse=True), rows, preferred_element_type=F32)
        os_.append(x[:, :GROUP_W_A])
        ls.append(x[:, GROUP_W_A:GROUP_W_A + LANES] + x[:, GROUP_W_A + LANES:])
    l0, l1, l2 = ls
    mx = jnp.maximum(jnp.maximum(l0, l1), l2)
    e0, e1, e2 = jnp.exp(l0 - mx), jnp.exp(l1 - mx), jnp.exp(l2 - mx)
    den = e0 + e1 + e2
    w0, w1, w2 = e0 / den, e1 / den, e2 / den
    cols = []
    for h in range(HEADS_PER_GROUP_A):
        hs = slice(h * HEAD_DIM_A, (h + 1) * HEAD_DIM_A)
        cols.append(w0[:, h:h + 1] * os_[0][:, hs] + w1[:, h:h + 1] * os_[1][:, hs] + w2[:, h:h + 1] * os_[2][:, hs])
    oa = jnp.concatenate(cols, axis=1).astype(BF16)
    pa = jnp.dot(oa, wpa_ref[...], preferred_element_type=F32)
    pb = jnp.dot(ob_ref[...].astype(BF16), wpb_ref[...], preferred_element_type=F32)
    pm = jnp.dot(om_ref[...].astype(BF16), wpm_ref[...], preferred_element_type=F32)
    z = (gt_ref[:, 0:d].astype(F32) * pa + gt_ref[:, d:2 * d].astype(F32) * pb
         + gt_ref[:, 2 * d:3 * d].astype(F32) * pm)
    z_ref[...] = z.astype(z_ref.dtype)


def _mix(gates, o_list, dils, seq, ob, om, wpa, wpb, wpm, tm):
    m = gates.shape[0]
    d = wpa.shape[1]
    mt = seq // tm if any(dl > 1 for dl in dils) else 1

    def rows(width):
        return pl.BlockSpec((tm, width), lambda i: (i, 0))

    def whole(a):
        return pl.BlockSpec(a.shape, lambda i: (0, 0))

    o_specs, o_args = [], []
    for o, dil in zip(o_list, dils):
        if dil == 1:
            o_specs.append(rows(O_EXT_W))
            o_args.append(o)
        else:
            assert seq % tm == 0 and (tm // dil) % 16 == 0
            o_specs.append(pl.BlockSpec((None, dil, tm // dil, O_EXT_W), lambda i: (i // mt, 0, i % mt, 0)))
            o_args.append(o.reshape(m // seq, dil, seq // dil, O_EXT_W))
    return pl.pallas_call(
        functools.partial(_mix_body, dils=tuple(dils)), name="mix", grid=(m // tm,),
        in_specs=[rows(gates.shape[1])] + o_specs
                 + [rows(ob.shape[1]), rows(om.shape[1]), whole(wpa), whole(wpb), whole(wpm)],
        out_specs=rows(d),
        out_shape=jax.ShapeDtypeStruct((m, d), BF16), compiler_params=_cparams(1),
    )(gates, *o_args, ob, om, wpa, wpb, wpm)


def _route(logits):
    lane = lax.broadcasted_iota(I32, logits.shape, 1)
    lane_f = lane.astype(F32)
    is_g = lane < N_EXPERT_GROUPS
    gmax = jnp.max(jnp.where(is_g, logits, -jnp.inf), axis=1, keepdims=True)
    gsel = jnp.min(jnp.where(is_g & (logits == gmax), lane_f, float(LANES)), axis=1, keepdims=True).astype(I32)
    gden = jnp.sum(jnp.where(is_g, jnp.exp(logits - gmax), 0.0), axis=1, keepdims=True)
    pg = 1.0 / gden
    e_lane = lane - N_EXPERT_GROUPS
    in_grp = (e_lane >= 0) & (e_lane < N_EXPERTS) & ((e_lane // EXPERTS_PER_GROUP) == gsel)
    m1 = jnp.max(jnp.where(in_grp, logits, -jnp.inf), axis=1, keepdims=True)
    i1 = jnp.min(jnp.where(in_grp & (logits == m1), lane_f, float(LANES)), axis=1, keepdims=True).astype(I32)
    rest = in_grp & (lane != i1)
    m2 = jnp.max(jnp.where(rest, logits, -jnp.inf), axis=1, keepdims=True)
    i2 = jnp.min(jnp.where(rest & (logits == m2), lane_f, float(LANES)), axis=1, keepdims=True).astype(I32)
    e2 = jnp.exp(m2 - m1)
    w1 = pg / (1.0 + e2)
    w2 = pg * e2 / (1.0 + e2)
    eid = jnp.where(lane == 0, i1 - N_EXPERT_GROUPS, jnp.where(lane == 1, i2 - N_EXPERT_GROUPS, 0))
    wts = jnp.where(lane == 0, w1, jnp.where(lane == 1, w2, 0.0))
    return eid, wts


def _resid_body(x_ref, z_ref, wo_ref, gf_ref, wr_ref, br_ref, *rest, n_main):
    tail_ref = rest[0] if len(rest) == 5 else None
    xmid_ref, hpk_ref, eid_ref, wts_ref = rest[-4:]

    def main():
        xm = x_ref[...] + jnp.dot(z_ref[...], wo_ref[...], preferred_element_type=F32)
        xmid_ref[...] = xm
        hf = _rms(xm, gf_ref[...])
        wr = wr_ref[...]
        wr_hi, wr_lo = _split_hi_lo(wr)
        hf_hi, hf_lo = _split_hi_lo(hf)
        t = jnp.dot(hf_hi, jnp.concatenate([wr_hi, wr_lo], axis=1), preferred_element_type=F32)
        logits = (t[:, :LANES] + t[:, LANES:] + jnp.dot(hf_lo, wr_hi, preferred_element_type=F32) + br_ref[...])
        eid, wts = _route(logits)
        eid_ref[...] = eid
        wts_ref[...] = wts
        _store_token_tiles(hpk_ref, hf)

    if tail_ref is None:
        main()
        return
    pl.when(pl.program_id(0) < n_main)(main)

    @pl.when(pl.program_id(0) == n_main)
    def _():
        hpk_ref[...] = tail_ref[...]


def _resid(x2d, z, wo, g_ffn, w_r, b_r, tm, tail=None):
    m, d = x2d.shape
    n_main = m // tm
    last = n_main - 1

    def rows(width):
        return pl.BlockSpec((tm, width), lambda i: (jnp.minimum(i, last), 0))

    def whole(a):
        return pl.BlockSpec(a.shape, lambda i: (0, 0))

    gf = g_ffn.reshape(1, d)
    nt = d // 2 // LANES
    assert tail is None or tail.shape == (tm * nt, LANES)
    n_tail = 0 if tail is None else 1
    return pl.pallas_call(
        functools.partial(_resid_body, n_main=n_main), name="resid", grid=(n_main + n_tail,),
        in_specs=[rows(d), rows(d), whole(wo), whole(gf), whole(w_r), whole(b_r)]
                 + ([whole(tail)] if n_tail else []),
        out_specs=[rows(d), pl.BlockSpec((tm * nt, LANES), lambda i: (i, 0)), rows(LANES), rows(LANES)],
        out_shape=[jax.ShapeDtypeStruct((m, d), F32), jax.ShapeDtypeStruct(((m + n_tail * tm) * nt, LANES), U32),
                   jax.ShapeDtypeStruct((m, LANES), I32), jax.ShapeDtypeStruct((m, LANES), F32)],
        compiler_params=_cparams(1),
    )(x2d, z, wo, gf, w_r, b_r, *([tail] * n_tail))


TOKEN_TILE = 8
GATHER_UNROLL = 8


def _gather_rows(idx_at, src_hbm, dst, sem, n):
    def body(i, carry):
        src = pl.multiple_of(idx_at(i), TOKEN_TILE)
        dst_row = pl.multiple_of(i * TOKEN_TILE, TOKEN_TILE)
        pltpu.make_async_copy(src_hbm.at[pl.ds(src, TOKEN_TILE)], dst.at[pl.ds(dst_row, TOKEN_TILE)], sem).start()
        return carry
    lax.fori_loop(0, n, body, 0, unroll=GATHER_UNROLL)


FFN_CHUNKS = 8
FFN_BUFS = 3


def _ffn_body(blk_e_ref, nused_ref, eord_ref, enext_ref, idx_hbm, h_hbm, wg_hbm, wu_hbm, wd_hbm, y_ref,
              idx_s, isem, buf, sem, x_scr, a_scr, b_scr, hm_scr, wg_f, wu_f, wd_f, wsem, wg_s, wu_s, wd_s):
    j = pl.program_id(0)
    nused = nused_ref[0]
    last = nused - 1
    rows = buf.shape[1] // TOKEN_TILE
    e = blk_e_ref[j]

    def idx_copy(b):
        s = b % FFN_BUFS
        return pltpu.make_async_copy(idx_hbm.at[jnp.minimum(b, last)], idx_s.at[s], isem.at[s])

    def wait_rows(s):
        pltpu.make_async_copy(h_hbm.at[pl.ds(0, rows * TOKEN_TILE)], buf.at[s], sem.at[s]).wait()
    new_expert = (j == 0) | (e != blk_e_ref[jnp.maximum(j - 1, 0)])
    ws = eord_ref[j] % 2

    def fetch_w(expert, s):
        return [pltpu.make_async_copy(src.at[expert], dst.at[s], wsem.at[s, i])
                for i, (src, dst) in enumerate(((wg_hbm, wg_f), (wu_hbm, wu_f), (wd_hbm, wd_f)))]

    @pl.when(j == 0)
    def _():
        for c in fetch_w(e, ws):
            c.start()
        for b in range(FFN_BUFS):
            idx_copy(b).start()
        for b in range(FFN_BUFS - 1):
            idx_copy(b).wait()
            _gather_rows(lambda i, b=b: idx_s[b, 0, i], h_hbm, buf.at[b], sem.at[b], rows)

    @pl.when(new_expert)
    def _():
        nxt = enext_ref[j]

        @pl.when(nxt >= 0)
        def _():
            for c in fetch_w(nxt, 1 - ws):
                c.start()

        for c in fetch_w(e, ws):
            c.wait()
        wg_s[...] = wg_f[ws].astype(BF16)
        wu_s[...] = wu_f[ws].astype(BF16)
        wd_s[...] = wd_f[ws].astype(BF16)

    @pl.when(j < nused)
    def _():
        slot = j % FFN_BUFS
        s2 = (j + 2) % FFN_BUFS
        per = rows // FFN_CHUNKS
        idx_copy(j + 2).wait()

        @pl.when(j < last)
        def _():
            idx_copy(j + 3).start()

        def issue(c):
            for i in range(c * per, (c + 1) * per):
                src = pl.multiple_of(idx_s[s2, 0, i], TOKEN_TILE)
                pltpu.make_async_copy(h_hbm.at[pl.ds(src, TOKEN_TILE)],
                                      buf.at[s2, pl.ds(i * TOKEN_TILE, TOKEN_TILE)], sem.at[s2]).start()

        wait_rows(slot)
        x_scr[...] = _load_token_tiles(buf, slot, 0, rows, TOKEN_TILE).astype(BF16)
        dn = a_scr.shape[1] // 2
        for c in range(2):
            issue(c)
            a_scr[:, c * dn:(c + 1) * dn] = jnp.dot(x_scr[...], wg_s[:, c * dn:(c + 1) * dn],
                                                     preferred_element_type=F32)
        for c in range(2):
            issue(2 + c)
            b_scr[:, c * dn:(c + 1) * dn] = jnp.dot(x_scr[...], wu_s[:, c * dn:(c + 1) * dn],
                                                     preferred_element_type=F32)
        a = a_scr[...]
        hm_scr[...] = (a * jax.nn.sigmoid(a) * b_scr[...]).astype(BF16)
        half = wd_s.shape[1] // 2
        n_dc = FFN_CHUNKS - 4
        dq = half // n_dc
        for c in range(n_dc):
            issue(4 + c)
            lo = jnp.dot(hm_scr[...], wd_s[:, c * dq:(c + 1) * dq], preferred_element_type=F32)
            hi = jnp.dot(hm_scr[...], wd_s[:, half + c * dq:half + (c + 1) * dq], preferred_element_type=F32)
            w = _pack_bf16_pair(jnp.concatenate([lo, hi], axis=1))
            for t in range(dq // LANES):
                y_ref[pl.ds(c * (dq // LANES) + t, rows, stride=TOKEN_TILE), :] = w[:, t * LANES:(t + 1) * LANES]

        @pl.when(j == last)
        def _():
            wait_rows((j + 1) % FFN_BUFS)
            wait_rows(s2)

    @pl.when(j >= nused)
    def _():
        y_ref[...] = jnp.zeros(y_ref.shape, y_ref.dtype)


def _ffn(hpk, src_tok, blk_e, nused, eord, enext, wg, wu, wd):
    n_blocks = blk_e.shape[0]
    rows = MOE_ROWS
    tiles = rows * TOKEN_TILE
    d, de = wg.shape[1], wg.shape[2]
    assert d // 2 == TOKEN_TILE * LANES and rows % FFN_CHUNKS == 0 and (d // 2) % ((FFN_CHUNKS - 4) * LANES) == 0
    assert n_blocks >= N_EXPERTS + FFN_BUFS
    idx = (src_tok * TOKEN_TILE).reshape(n_blocks, 1, rows)
    hbm = pl.BlockSpec(memory_space=pl.ANY)
    grid_spec = pltpu.PrefetchScalarGridSpec(
        num_scalar_prefetch=4, grid=(n_blocks,),
        in_specs=[hbm, hbm, hbm, hbm, hbm],
        out_specs=pl.BlockSpec((tiles, LANES), lambda j, *_: (j, 0)),
        scratch_shapes=[pltpu.SMEM((FFN_BUFS, 1, rows), I32), pltpu.SemaphoreType.DMA((FFN_BUFS,)),
                        pltpu.VMEM((FFN_BUFS, tiles, LANES), U32), pltpu.SemaphoreType.DMA((FFN_BUFS,)),
                        pltpu.VMEM((rows, d), BF16), pltpu.VMEM((rows, de), F32), pltpu.VMEM((rows, de), F32),
                        pltpu.VMEM((rows, de), BF16),
                        pltpu.VMEM((2, d, de), wg.dtype), pltpu.VMEM((2, d, de), wu.dtype),
                        pltpu.VMEM((2, de, d), wd.dtype), pltpu.SemaphoreType.DMA((2, 3)),
                        pltpu.VMEM((d, de), BF16), pltpu.VMEM((d, de), BF16), pltpu.VMEM((de, d), BF16)],
    )
    return pl.pallas_call(
        _ffn_body, name="ffn", grid_spec=grid_spec,
        out_shape=jax.ShapeDtypeStruct((n_blocks * tiles, LANES), U32), compiler_params=_cparams(1),
    )(blk_e, nused, eord, enext, idx, hpk, wg, wu, wd)


def _combine_body(idx_cur_ref, idx_nxt_ref, x_ref, w_ref, y_hbm, o_ref, buf, sem, *, n):
    i = pl.program_id(0)
    slot = i % 2
    rows = buf.shape[1] // TOKEN_TILE
    tm = rows // TOP_K

    @pl.when(i == 0)
    def _():
        _gather_rows(lambda r: idx_cur_ref[0, 0, r], y_hbm, buf.at[0], sem.at[0], rows)

    if n > 1:
        @pl.when(i + 1 < n)
        def _():
            _gather_rows(lambda r: idx_nxt_ref[0, 0, r], y_hbm, buf.at[1 - slot], sem.at[1 - slot], rows)

    pltpu.make_async_copy(y_hbm.at[pl.ds(0, rows * TOKEN_TILE)], buf.at[slot], sem.at[slot]).wait()
    w = w_ref[...]
    y0 = _load_token_tiles(buf, slot, 0, tm, TOKEN_TILE)
    y1 = _load_token_tiles(buf, slot, tm, tm, TOKEN_TILE)
    o_ref[...] = x_ref[...] + (w[:, 0:1] * y0 + w[:, 1:2] * y1)


def _combine(xmid, wts, dest, ypk, tm):
    m, d = xmid.shape
    nt = m // tm
    assert d // 2 == TOKEN_TILE * LANES
    idx = (dest * TOKEN_TILE).reshape(nt, tm, TOP_K).transpose(0, 2, 1).reshape(nt, 1, TOP_K * tm)

    def idx_spec(ahead):
        return pl.BlockSpec((1, 1, TOP_K * tm), lambda i: (jnp.minimum(i + ahead, nt - 1), 0, 0),
                            memory_space=pltpu.SMEM)

    return pl.pallas_call(
        functools.partial(_combine_body, n=nt), name="combine", grid=(nt,),
        in_specs=[idx_spec(0), idx_spec(1), pl.BlockSpec((tm, d), lambda i: (i, 0)),
                  pl.BlockSpec((tm, LANES), lambda i: (i, 0)), pl.BlockSpec(memory_space=pl.ANY)],
        out_specs=pl.BlockSpec((tm, d), lambda i: (i, 0)),
        out_shape=jax.ShapeDtypeStruct((m, d), F32),
        scratch_shapes=[pltpu.VMEM((2, TOP_K * tm * TOKEN_TILE, LANES), U32), pltpu.SemaphoreType.DMA((2,))],
        compiler_params=_cparams(1),
    )(idx, idx, xmid, wts, ypk)


def _moe_plan(eid2):
    n_slot = eid2.shape[0] * TOP_K
    eid = eid2.reshape(n_slot)
    onehot = (eid[:, None] == jnp.arange(N_EXPERTS, dtype=I32)[None, :]).astype(I32)
    csum = jnp.cumsum(onehot, axis=0)
    rank = jnp.take_along_axis(csum, eid[:, None], axis=1)[:, 0] - 1
    counts = csum[-1]
    padded = (counts + MOE_ROWS - 1) // MOE_ROWS * MOE_ROWS
    pend = jnp.cumsum(padded)
    pstart = pend - padded
    dest = pstart[eid] + rank
    n_blocks = -(-n_slot // MOE_ROWS) + N_EXPERTS
    nused = (pend[-1] // MOE_ROWS).astype(I32)
    blk = jnp.minimum(jnp.arange(n_blocks, dtype=I32), nused - 1)
    blk_e = jnp.minimum(jnp.searchsorted(pend, blk * MOE_ROWS, side="right"), N_EXPERTS - 1).astype(I32)
    n_rows = n_blocks * MOE_ROWS
    src_tok = (jnp.arange(n_rows, dtype=I32) % (n_slot // TOP_K)).at[dest].set(
        jnp.arange(n_slot, dtype=I32) // TOP_K, unique_indices=True, mode="promise_in_bounds")
    has_rows = counts > 0
    ord_e = jnp.cumsum(has_rows.astype(I32)) - 1
    ids = jnp.arange(N_EXPERTS, dtype=I32)
    later = has_rows[None, :] & (ids[None, :] > ids[:, None])
    next_e = jnp.where(later.any(axis=1), jnp.argmax(later, axis=1), -1).astype(I32)
    return dest.astype(I32), src_tok, blk_e, nused.reshape(1), ord_e[blk_e].astype(I32), next_e[blk_e]


def _mixer(x2d, tm, p, attn_fn, gmlp_fn, mem_fn):
    h = _prep(x2d, p["g_mix"], tm)
    qdt = p["q_dtype"]
    wi, sec = p["w_in"], p["sec"]
    o_list, dils, seq = attn_fn(h)
    n_ub = sec["vb"] - sec["u"]
    ub = _proj(h, wi, (sec["u"], n_ub, n_ub, 1), tm, "none", BF16, name="proj_ub")
    n_vb = sec["qm"] - sec["vb"]
    vb = _proj(h, wi, (sec["vb"], n_vb, n_vb, 1), tm, "norm", p["vb_dtype"], p["g_vb"], n_vb * W_COLS,
               name="proj_vb")
    n_qm = sec["gt"] - sec["qm"]
    qm = _proj(h, wi, (sec["qm"], n_qm, n_qm, 1), tm, "norm", qdt, p["g_qm"], HEAD_DIM_M,
               HEAD_DIM_M ** -0.5, name="proj_qm")
    gates = _proj(h, wi, (sec["gt"], 2, 2, (sec["end"] - sec["gt"]) // 2), tm, "sigmoid", BF16, name="proj_gates")
    ob = gmlp_fn(ub, vb)
    om = mem_fn(qm)
    z = _mix(gates, o_list, dils, seq, ob, om, p["w_pa"], p["w_pb"], p["w_pm"], min(tm, 256))
    xmid, hpk, eid, wts = _resid(x2d, z, p["w_o"], p["g_ffn"], p["w_r"], p["b_r"], min(tm, 256),
                                 tail=p.get("hpk_tail"))
    return vb, xmid, hpk, eid, wts


def kernel(x_prompt, x_sample, mem_prompt, cache_a0_kv, cache_a1_kv, cache_a2_kv, cache_mem_kv, rel_bias, g_mix, w_in, g_qa, g_ka, w_pa, g_vb, w_s, b_s, w_pb, g_mem, w_mk, w_mv, g_qm, g_km, w_pm, w_o, g_ffn, w_rg, b_rg, w_re, b_re, w_gate, w_up, w_down):
    n_b, seq, d = x_prompt.shape
    n_s, t_s, _ = x_sample.shape
    depth = w_in.shape[0]
    assert depth == 1
    l = 0
    caches = (cache_a0_kv, cache_a1_kv, cache_a2_kv)
    n_g = len(DIL_GROUPS)
    wb = w_pb.shape[1]
    wm = N_HEADS_M * HEAD_DIM_M

    offs = [0, WIDTH_A, 2 * WIDTH_A, 3 * WIDTH_A, 3 * WIDTH_A + wb, 3 * WIDTH_A + 2 * wb,
            3 * WIDTH_A + 2 * wb + wm, w_in.shape[2]]
    assert all(o % W_COLS == 0 for o in offs) and (offs[7] - offs[6]) % (2 * W_COLS) == 0
    sec = dict(zip(("q", "k", "v", "u", "vb", "qm", "gt", "end"), (o // W_COLS for o in offs)))
    p = {
        "g_mix": g_mix[l], "g_qa": g_qa[l], "g_ka": g_ka[l], "g_vb": g_vb[l], "g_qm": g_qm[l], "g_ffn": g_ffn[l],
        "w_in": w_in[l], "sec": sec,
        "w_pa": w_pa[l].astype(BF16), "w_pb": w_pb[l].astype(BF16), "w_pm": w_pm[l].astype(BF16),
        "w_o": w_o[l].astype(BF16),
    }
    n_r = N_EXPERT_GROUPS + N_EXPERTS
    p["w_r"] = jnp.concatenate([w_rg[l], w_re[l], jnp.zeros((d, LANES - n_r), F32)], axis=1)
    p["b_r"] = jnp.concatenate([b_rg[l], b_re[l], jnp.zeros((LANES - n_r,), F32)]).reshape(1, LANES)

    n_mem = mem_prompt.shape[1]
    h_mem = _prep(mem_prompt.reshape(n_b * n_mem, d), g_mem[l], 256)
    w_mkv = jnp.concatenate([w_mk[l], w_mv[l]], axis=1)
    n_mw = wm // W_COLS
    mkv = _proj(h_mem, w_mkv, (0, n_mw, n_mw, 2), 256, "norm_first", F32, g_km[l], HEAD_DIM_M, name="proj_mkv")
    new_mem_p = mkv.reshape(1, n_b, n_mem, 2, N_HEADS_M, HEAD_DIM_M)

    ps = dict(p, q_dtype=F32, vb_dtype=F32)
    m_s = n_s * t_s
    n_col = 2 * HEADS_PER_GROUP_A
    new_s = []

    def attn_sample(h):
        o_list = []
        for g in range(n_g):
            q, kv, tok = _proj_qkv(h, w_in[l], sec, g, g_qa[l], g_ka[l], 1, m_s, m_s, 1, act_dtype=F32)
            cache3 = caches[g][l].reshape(n_s, caches[g].shape[2] * n_col, LANES)
            o, newc = _attn_sample(q, kv, tok, cache3, rel_bias, g, t_s)
            o_list.append(o)
            new_s.append(newc.reshape(caches[g][l:l + 1].shape))
        return o_list, [1] * n_g, t_s

    n_ct = HEAD_DIM_M // LANES
    mem_tiles = (cache_mem_kv[l].reshape(n_s, n_mem, 2, N_HEADS_M, n_ct, LANES).transpose(0, 1, 2, 4, 3, 5)
                 .reshape(n_s, n_mem * 2 * n_ct * N_HEADS_M, LANES))
    vb_s, xmid_s, hpk_s, eid_s, wts_s = _mixer(
        x_sample.reshape(m_s, d), m_s, ps, attn_sample,
        lambda u, v: _gmlp_new(u, v, w_s[l], b_s[l], t_s),
        lambda qm: _memattn(qm, mem_tiles, t_s, F32, tiles=True, reqs=4))
    new_vb_s = vb_s.reshape(1, n_s, t_s, wb)

    pp = dict(p, q_dtype=BF16, vb_dtype=BF16, hpk_tail=hpk_s)
    new_p = []

    def attn_prompt(h):
        o_list = []
        for g, (win, dil) in enumerate(DIL_GROUPS):
            q, kv, tok = _proj_qkv(h, w_in[l], sec, g, g_qa[l], g_ka[l], n_b, seq, 512, dil)
            o_list.append(_attn_prompt(q, kv, rel_bias, g, n_b, seq))
            keep = min(win, seq)
            kv5 = tok.reshape(n_b, seq, 2, HEADS_PER_GROUP_A, HEAD_DIM_A)
            new_p.append((kv5 if keep == seq else kv5[:, seq - keep:])[None])
        return o_list, [dl for _, dl in DIL_GROUPS], seq

    _, xmid_p, hpk_p, eid_p, wts_p = _mixer(
        x_prompt.reshape(n_b * seq, d), 512, pp, attn_prompt,
        lambda u, v: _gmlp(u, v, w_s[l], b_s[l].T, 512, CHUNK),
        lambda qm: _memattn(qm, mkv.reshape(n_b, n_mem, 2 * wm), 512, BF16))

    n_p = n_b * seq
    hpk = hpk_p
    eid2 = jnp.concatenate([eid_p[:, :TOP_K], eid_s[:, :TOP_K]], axis=0)
    dest, src_tok, blk_e, nused, eord, enext = _moe_plan(eid2)
    ypk = _ffn(hpk, src_tok, blk_e, nused, eord, enext, w_gate[l], w_up[l], w_down[l])
    dest2 = dest.reshape(-1, TOP_K)
    y_p = _combine(xmid_p, wts_p, dest2[:n_p], ypk, 256)
    y_s = _combine(xmid_s, wts_s, dest2[n_p:], ypk, m_s)

    return (y_p.reshape(n_b, seq, d), y_s.reshape(n_s, t_s, d), new_p[0], new_p[1], new_p[2], new_mem_p,
            new_s[0], new_s[1], new_s[2], new_vb_s)
```

```python
import functools
import math

import jax
import jax.numpy as jnp
from jax import lax
from jax.experimental import pallas as pl
from jax.experimental.pallas import tpu as pltpu

F32 = jnp.float32
BF16 = jnp.bfloat16
I32 = jnp.int32
U32 = jnp.uint32

EPS = 1e-6
NEG_INF = -1e30

HEAD_DIM_A = 128
HEADS_PER_GROUP_A = 4
DIL_GROUPS = ((128, 1), (512, 4), (2048, 16))
GROUP_W_A = HEADS_PER_GROUP_A * HEAD_DIM_A
WIDTH_A = len(DIL_GROUPS) * GROUP_W_A
CHUNK = 128
N_GROUPS_B = 8
N_HEADS_M = 4
HEAD_DIM_M = 256
N_BUCKETS = 32
MAX_EXACT = N_BUCKETS // 2
MAX_DISTANCE = 2048
N_EXPERT_GROUPS = 4
EXPERTS_PER_GROUP = 8
N_EXPERTS = N_EXPERT_GROUPS * EXPERTS_PER_GROUP
TOP_K = 2
LANES = 128
MOE_ROWS = 256
VMEM_LIMIT = 56 * 1024 * 1024


def _cparams(n_grid, vmem=VMEM_LIMIT):
    return pltpu.CompilerParams(dimension_semantics=("arbitrary",) * n_grid, vmem_limit_bytes=vmem)


def _rms(x, g):
    return x * lax.rsqrt(jnp.mean(x * x, axis=-1, keepdims=True) + EPS) * g


def _pack_bf16_pair(x):
    n = x.shape[1] // 2
    lo = lax.bitcast_convert_type(x[:, :n].astype(BF16).astype(F32), U32)
    hi = lax.bitcast_convert_type(x[:, n:].astype(BF16).astype(F32), U32)
    return (hi & jnp.uint32(0xFFFF0000)) | (lo >> 16)


def _unpack_bf16_pair(w):
    lo = lax.bitcast_convert_type(w << 16, F32)
    hi = lax.bitcast_convert_type(w & jnp.uint32(0xFFFF0000), F32)
    return lo, hi


def _store_token_tiles(ref, x):
    w = _pack_bf16_pair(x)
    m, n = w.shape
    nt = n // LANES
    for c in range(nt):
        ref[pl.ds(c, m, stride=nt), :] = w[:, c * LANES:(c + 1) * LANES]


def _load_token_tiles(ref, lead, row0, m, nt):
    los, his = [], []
    view = ref if lead is None else ref.at[lead]
    for c in range(nt):
        lo, hi = _unpack_bf16_pair(view[pl.ds(row0 * nt + c, m, stride=nt), :])
        los.append(lo)
        his.append(hi)
    return jnp.concatenate(los + his, axis=1)


def _rel_bucket(dist):
    d = jnp.maximum(dist, 1).astype(F32)
    large = MAX_EXACT + (jnp.log(d / MAX_EXACT) / math.log(MAX_DISTANCE / MAX_EXACT)
                         * (N_BUCKETS - MAX_EXACT)).astype(I32)
    return jnp.where(dist < MAX_EXACT, dist, jnp.minimum(large, N_BUCKETS - 1)).astype(I32)


def _prep_body(x_ref, g_ref, o_ref):
    o_ref[...] = _rms(x_ref[...], g_ref[...]).astype(o_ref.dtype)


def _prep(x2d, g, tm):
    m, d = x2d.shape
    return pl.pallas_call(
        _prep_body, name="prep", grid=(m // tm,),
        in_specs=[pl.BlockSpec((tm, d), lambda i: (i, 0)), pl.BlockSpec((1, d), lambda i: (0, 0))],
        out_specs=pl.BlockSpec((tm, d), lambda i: (i, 0)),
        out_shape=jax.ShapeDtypeStruct((m, d), BF16), compiler_params=_cparams(1),
    )(x2d, g.reshape(1, d))


def _headnorm(acc, g, hd, scale):
    outs = []
    for j in range(acc.shape[1] // hd):
        sl = acc[:, j * hd:(j + 1) * hd]
        outs.append(sl * lax.rsqrt(jnp.mean(sl * sl, axis=-1, keepdims=True) + EPS))
    y = outs[0] if len(outs) == 1 else jnp.concatenate(outs, axis=1)
    y = y * g
    return y * scale if scale != 1.0 else y


def _proj_body(h_ref, *refs, n_w, mode, hd, scale):
    w_refs, (g_ref, o_ref, w_scr) = refs[:n_w], refs[n_w:]
    cw = w_refs[0].shape[1]

    @pl.when(pl.program_id(1) == 0)
    def _():
        for i, w_ref in enumerate(w_refs):
            w_scr[:, i * cw:(i + 1) * cw] = w_ref[...].astype(w_scr.dtype)

    def run(epilogue):
        acc = jnp.dot(h_ref[...], w_scr[...], preferred_element_type=F32)
        o_ref[...] = epilogue(acc).astype(o_ref.dtype)

    normed = lambda acc: _headnorm(acc, g_ref[...], hd, scale)
    if mode == "none":
        run(lambda acc: acc)
    elif mode == "sigmoid":
        run(jax.nn.sigmoid)
    elif mode == "norm":
        run(normed)
    else:
        @pl.when(pl.program_id(0) == 0)
        def _():
            run(normed)

        @pl.when(pl.program_id(0) != 0)
        def _():
            run(lambda acc: acc)


W_COLS = 512


def _proj(h, w, cols, tm, mode, out_dtype, gain=None, hd=None, scale=1.0, name="proj"):
    m, k = h.shape
    base, stride, n_w, n_tiles = cols
    tn = n_w * W_COLS
    if gain is None:
        g = jnp.ones((1, tn), F32)
    else:
        g = jnp.tile(gain.astype(F32).reshape(1, -1), (1, tn // gain.shape[-1]))
    body = functools.partial(_proj_body, n_w=n_w, mode=mode, hd=hd, scale=scale)
    w_specs = [pl.BlockSpec((k, W_COLS), functools.partial(lambda j, i, off: (0, base + stride * j + off), off=off))
               for off in range(n_w)]
    return pl.pallas_call(
        body, name=name, grid=(n_tiles, m // tm),
        in_specs=[pl.BlockSpec((tm, k), lambda j, i: (i, 0))] + w_specs + [pl.BlockSpec((1, tn), lambda j, i: (0, 0))],
        out_specs=pl.BlockSpec((tm, tn), lambda j, i: (i, j)),
        out_shape=jax.ShapeDtypeStruct((m, n_tiles * tn), out_dtype),
        scratch_shapes=[pltpu.VMEM((k, tn), BF16)], compiler_params=_cparams(2),
    )(h, *([w] * n_w), g)


PERM_ROWS = 256


def _perm_matrix(n, dil, inverse=False):
    per = n // dil
    o = lax.broadcasted_iota(I32, (n, n), 1 if inverse else 0)
    s = lax.broadcasted_iota(I32, (n, n), 0 if inverse else 1)
    return (s == (o % per) * dil + o // per).astype(BF16)


def _qkv_body(h_ref, wq_ref, wk_ref, wv_ref, gq_ref, gk_ref, q_ref, kv_ref, tok_ref, w_scr, *, dil, scale):
    tm = h_ref.shape[0]
    nh = HEADS_PER_GROUP_A
    gw = GROUP_W_A

    @pl.when(pl.program_id(0) == 0)
    def _():
        for i, w_ref in enumerate((wq_ref, wk_ref, wv_ref)):
            w_scr[:, i * gw:(i + 1) * gw] = w_ref[...].astype(BF16)

    acc = jnp.dot(h_ref[...], w_scr[...], preferred_element_type=F32)
    q = _headnorm(acc[:, :gw], gq_ref[...], HEAD_DIM_A, scale)
    k = _headnorm(acc[:, gw:2 * gw], gk_ref[...], HEAD_DIM_A, 1.0)
    v = acc[:, 2 * gw:]
    for i in range(nh):
        tok_ref[pl.ds(i, tm, stride=2 * nh), :] = k[:, i * HEAD_DIM_A:(i + 1) * HEAD_DIM_A]
        tok_ref[pl.ds(nh + i, tm, stride=2 * nh), :] = v[:, i * HEAD_DIM_A:(i + 1) * HEAD_DIM_A]

    if dil == 1:
        q_ref[0] = q.astype(q_ref.dtype)
        kv_ref[0, :, :gw] = k.astype(kv_ref.dtype)
        kv_ref[0, :, gw:] = v.astype(kv_ref.dtype)
        return
    sub = min(tm, PERM_ROWS)
    per = sub // dil
    pm = _perm_matrix(sub, dil)
    qkv = jnp.concatenate([q, k, v], axis=1).astype(BF16)
    for t in range(tm // sub):
        yp = jnp.dot(pm, qkv[t * sub:(t + 1) * sub], preferred_element_type=F32).astype(BF16)
        for r in range(dil):
            q_ref[r, t * per:(t + 1) * per, :] = yp[r * per:(r + 1) * per, :gw]
            kv_ref[r, t * per:(t + 1) * per, :] = yp[r * per:(r + 1) * per, gw:]


def _proj_qkv(h, w, sec, g, g_qa, g_ka, n_batch, seq, tm, dil, act_dtype=BF16):
    n, k = h.shape
    ln = seq // dil
    per = tm // dil
    mt = seq // tm
    assert seq % tm == 0
    assert dil == 1 or (act_dtype == BF16 and tm % PERM_ROWS == 0 and (PERM_ROWS // dil) % 16 == 0)
    tile = lambda v: jnp.tile(v.astype(F32).reshape(1, -1), (1, GROUP_W_A // v.shape[-1]))
    assert W_COLS == GROUP_W_A
    wspec = lambda blk: pl.BlockSpec((k, W_COLS), lambda m: (0, blk))
    q, kv, tok = pl.pallas_call(
        functools.partial(_qkv_body, dil=dil, scale=HEAD_DIM_A ** -0.5), name=f"proj_qkv{g}", grid=(n // tm,),
        in_specs=[pl.BlockSpec((tm, k), lambda m: (m, 0)),
                  wspec(sec["q"] + g), wspec(sec["k"] + g), wspec(sec["v"] + g),
                  pl.BlockSpec((1, GROUP_W_A), lambda m: (0, 0)), pl.BlockSpec((1, GROUP_W_A), lambda m: (0, 0))],
        out_specs=[pl.BlockSpec((None, dil, per, GROUP_W_A), lambda m: (m // mt, 0, m % mt, 0)),
                   pl.BlockSpec((None, dil, per, 2 * GROUP_W_A), lambda m: (m // mt, 0, m % mt, 0)),
                   pl.BlockSpec((tm * 2 * HEADS_PER_GROUP_A, LANES), lambda m: (m, 0))],
        out_shape=[jax.ShapeDtypeStruct((n_batch, dil, ln, GROUP_W_A), act_dtype),
                   jax.ShapeDtypeStruct((n_batch, dil, ln, 2 * GROUP_W_A), act_dtype),
                   jax.ShapeDtypeStruct((n * 2 * HEADS_PER_GROUP_A, LANES), F32)],
        scratch_shapes=[pltpu.VMEM((k, 3 * W_COLS), BF16)], compiler_params=_cparams(1),
    )(h, w, w, w, tile(g_qa), tile(g_ka))
    return q.reshape(n, GROUP_W_A), kv.reshape(n, 2 * GROUP_W_A), tok


def _bias_from_table(tab_ref, bidx, col):
    acc = jnp.zeros(bidx.shape, F32)
    for kb in range(N_BUCKETS):
        acc = jnp.where(bidx == kb, tab_ref[kb, col], acc)
    return acc


def _split_hi_lo(x):
    hi = x.astype(BF16)
    return hi, (x - hi.astype(F32)).astype(BF16)


ATTN_P_BLOCKS = 4


def _attn_p_body(tab_ref, bidx_ref, q_ref, kp_ref, kc_ref, vp_ref, vc_ref, o_ref, bias_scr, *, g, nb):
    i = pl.program_id(0)

    @pl.when(i == 0)
    def _():
        bidx = bidx_ref[...]
        for h in range(HEADS_PER_GROUP_A):
            bias_scr[h] = _bias_from_table(tab_ref, bidx, g * HEADS_PER_GROUP_A + h)

    blk = kp_ref.shape[0]
    k_all = jnp.concatenate([kp_ref[...], kc_ref[...]], axis=0)
    v_all = jnp.concatenate([vp_ref[...], vc_ref[...]], axis=0)
    row = lax.broadcasted_iota(I32, (blk, 2 * blk), 0)
    col = lax.broadcasted_iota(I32, (blk, 2 * blk), 1)
    dist = blk + row - col
    band = (dist >= 0) & (dist <= blk)
    lane = lax.broadcasted_iota(I32, (blk, LANES), 1)
    for u in range(ATTN_P_BLOCKS):
        has_prev = ((i * ATTN_P_BLOCKS + u) % nb) > 0
        valid = band & (has_prev | (col >= blk))
        q = q_ref[u * blk:(u + 1) * blk, :]
        k = k_all[u * blk:(u + 2) * blk]
        v = v_all[u * blk:(u + 2) * blk]
        lse_blk = jnp.zeros((blk, LANES), F32)
        outs = []
        for h in range(HEADS_PER_GROUP_A):
            hs = slice(h * HEAD_DIM_A, (h + 1) * HEAD_DIM_A)
            s = lax.dot_general(q[:, hs], k[:, hs], (((1,), (1,)), ((), ())), preferred_element_type=F32)
            s = jnp.where(valid, s + bias_scr[h], NEG_INF)
            m = jnp.max(s, axis=-1, keepdims=True)
            p = jnp.exp(s - m)
            l = jnp.sum(p, axis=-1, keepdims=True)
            o = jnp.dot(p.astype(BF16), v[:, hs], preferred_element_type=F32) / l
            outs.append(o)
            lse_blk = jnp.where(lane == h, m + jnp.log(l), lse_blk)
        lse_hi, lse_lo = _split_hi_lo(lse_blk)
        o_ref[u * blk:(u + 1) * blk, :] = jnp.concatenate(
            [jnp.concatenate(outs, axis=1).astype(BF16), lse_hi, lse_lo], axis=1)


O_EXT_W = GROUP_W_A + 2 * LANES


def _attn_prompt(q, kv, rel_bias, g, n_batch, seq):
    win, dil = DIL_GROUPS[g]
    blk = win // dil
    nb = seq // dil // blk
    n = n_batch * seq
    a = jnp.arange(blk)[:, None]
    c = jnp.arange(2 * blk)[None, :]
    bidx = _rel_bucket(jnp.maximum(blk + a - c, 0) * dil)

    nu = ATTN_P_BLOCKS
    assert (n // blk) % nu == 0 and (nb % nu == 0 or nb == 1)

    def prev(i):
        return jnp.maximum(i * nu - 1, 0)

    return pl.pallas_call(
        functools.partial(_attn_p_body, g=g, nb=nb), name=f"attn_p{g}", grid=(n // blk // nu,),
        in_specs=[
            pl.BlockSpec(memory_space=pltpu.SMEM),
            pl.BlockSpec((blk, 2 * blk), lambda i: (0, 0)),
            pl.BlockSpec((nu * blk, GROUP_W_A), lambda i: (i, 0)),
            pl.BlockSpec((blk, GROUP_W_A), lambda i: (prev(i), 0)),
            pl.BlockSpec((nu * blk, GROUP_W_A), lambda i: (i, 0)),
            pl.BlockSpec((blk, GROUP_W_A), lambda i: (prev(i), 1)),
            pl.BlockSpec((nu * blk, GROUP_W_A), lambda i: (i, 1)),
        ],
        out_specs=pl.BlockSpec((nu * blk, O_EXT_W), lambda i: (i, 0)),
        out_shape=jax.ShapeDtypeStruct((n, O_EXT_W), BF16),
        scratch_shapes=[pltpu.VMEM((HEADS_PER_GROUP_A, blk, 2 * blk), F32)],
        compiler_params=_cparams(1),
    )(rel_bias, bidx, q, kv, kv, kv, kv)


def _attn_s_body(tab_ref, bidx_ref, q_ref, kvn_ref, tokn_ref, cache_hbm, o_ref, newc_hbm,
                 raw, kv_scr, bias_scr, sem_in, sem_out, *, g, dil, lb, n_req):
    t_new = q_ref.shape[0]
    nh = HEADS_PER_GROUP_A
    n_rows = nh * t_new
    n_keys = kv_scr.shape[0]
    n_col = 2 * nh
    lb8 = lb * n_col
    n = pl.program_id(0)
    nbuf = raw.shape[0]
    ahead = nbuf // 2
    slot = n % nbuf

    def fetch(req):
        s = req % nbuf
        return pltpu.make_async_copy(cache_hbm.at[req], raw.at[s, pl.ds(0, lb8)], sem_in.at[s])

    def flush(req):
        s = req % nbuf
        return pltpu.make_async_copy(raw.at[s, pl.ds(t_new * n_col, lb8)], newc_hbm.at[req], sem_out.at[s])

    @pl.when(n == 0)
    def _():
        kv_scr[lb:, :] = jnp.zeros((n_keys - lb, kv_scr.shape[1]), BF16)
        bidx = bidx_ref[...]
        for h in range(nh):
            bias_scr[h * t_new:(h + 1) * t_new, :] = _bias_from_table(tab_ref, bidx, g * nh + h)
        for r in range(ahead):
            fetch(r).start()

    @pl.when(n >= nbuf - ahead)
    def _():
        flush(n - (nbuf - ahead)).wait()

    @pl.when(n + ahead < n_req)
    def _():
        fetch(n + ahead).start()

    fetch(n).wait()
    raw[slot, lb8:lb8 + t_new * n_col, :] = tokn_ref[...]
    flush(n).start()
    new = kvn_ref[...]

    rows_per = min(lb, 256)
    for c in range(n_col):
        for r0 in range(0, lb, rows_per):
            kv_scr[r0:r0 + rows_per, c * LANES:(c + 1) * LANES] = raw[
                slot, pl.ds(r0 * n_col + c, rows_per, stride=n_col), :].astype(BF16)
    kv_scr[lb:lb + 2 * t_new, :] = jnp.concatenate([new, jnp.zeros_like(new)], axis=0).astype(BF16)

    q = q_ref[...]
    qt = jnp.concatenate([q] * nh, axis=0)
    rr = lax.broadcasted_iota(I32, (n_rows, GROUP_W_A), 0)
    cc = lax.broadcasted_iota(I32, (n_rows, GROUP_W_A), 1)
    qbd = jnp.where(cc // HEAD_DIM_A == rr // t_new, qt, 0.0).astype(BF16)

    s = lax.dot_general(qbd, kv_scr[:, :GROUP_W_A], (((1,), (1,)), ((), ())), preferred_element_type=F32)
    row = lax.broadcasted_iota(I32, (n_rows, n_keys), 0)
    col = lax.broadcasted_iota(I32, (n_rows, n_keys), 1)
    delta = lb + (row & (t_new - 1)) - col
    valid = (delta >= 0) & (delta <= lb) & ((delta & (dil - 1)) == 0)
    s = jnp.where(valid, s + bias_scr[...], NEG_INF)
    m = jnp.max(s, axis=-1, keepdims=True)
    p = jnp.exp(s - m)
    l = jnp.sum(p, axis=-1, keepdims=True)
    o = jnp.dot(p.astype(BF16), kv_scr[:, GROUP_W_A:], preferred_element_type=F32) / l
    lse = m + jnp.log(l)
    lane = lax.broadcasted_iota(I32, (t_new, LANES), 1)
    lse_blk = jnp.zeros((t_new, LANES), F32)
    outs = []
    for h in range(nh):
        outs.append(o[h * t_new:(h + 1) * t_new, h * HEAD_DIM_A:(h + 1) * HEAD_DIM_A])
        lse_blk = jnp.where(lane == h, lse[h * t_new:(h + 1) * t_new, :], lse_blk)
    o_ref[...] = jnp.concatenate(outs + [lse_blk, jnp.zeros_like(lse_blk)], axis=1)

    @pl.when(n == n_req - 1)
    def _():
        for back in range(nbuf - ahead - 1, -1, -1):
            flush(n - back).wait()


ATTN_S_BUFS = 4


def _attn_sample(q, kv_new, tok_new, cache, rel_bias, g, t_new):
    win, dil = DIL_GROUPS[g]
    n_col = 2 * HEADS_PER_GROUP_A
    n_req, lb = cache.shape[0], cache.shape[1] // n_col
    assert t_new & (t_new - 1) == 0 and dil & (dil - 1) == 0 and lb == win
    n_keys = lb + LANES
    t = jnp.arange(t_new)[:, None]
    c = jnp.arange(n_keys)[None, :]
    bidx = _rel_bucket(jnp.clip(lb + t - c, 0, lb))
    return pl.pallas_call(
        functools.partial(_attn_s_body, g=g, dil=dil, lb=lb, n_req=n_req), name=f"attn_s{g}", grid=(n_req,),
        in_specs=[
            pl.BlockSpec(memory_space=pltpu.SMEM),
            pl.BlockSpec((t_new, n_keys), lambda n: (0, 0)),
            pl.BlockSpec((t_new, GROUP_W_A), lambda n: (n, 0)),
            pl.BlockSpec((t_new, 2 * GROUP_W_A), lambda n: (n, 0)),
            pl.BlockSpec((t_new * n_col, LANES), lambda n: (n, 0)),
            pl.BlockSpec(memory_space=pl.ANY),
        ],
        out_specs=[pl.BlockSpec((t_new, O_EXT_W), lambda n: (n, 0)), pl.BlockSpec(memory_space=pl.ANY)],
        out_shape=[jax.ShapeDtypeStruct((n_req * t_new, O_EXT_W), F32), jax.ShapeDtypeStruct(cache.shape, F32)],
        scratch_shapes=[pltpu.VMEM((ATTN_S_BUFS, (lb + t_new) * n_col, LANES), F32),
                        pltpu.VMEM((n_keys, 2 * GROUP_W_A), BF16),
                        pltpu.VMEM((HEADS_PER_GROUP_A * t_new, n_keys), F32),
                        pltpu.SemaphoreType.DMA((ATTN_S_BUFS,)), pltpu.SemaphoreType.DMA((ATTN_S_BUFS,))],
        compiler_params=_cparams(1),
    )(rel_bias, bidx, q, kv_new, tok_new, cache)


def _gmlp_body(u_ref, v_ref, w_ref, b_ref, o_ref, *, period):
    c = w_ref.shape[1]
    nch = u_ref.shape[0] // c
    gd = u_ref.shape[1] // N_GROUPS_B
    i = lax.broadcasted_iota(I32, (c, c), 0)
    j = lax.broadcasted_iota(I32, (c, c), 1)
    mask = (j <= i) & ((i // period) == (j // period))
    b = b_ref[...]
    for g in range(N_GROUPS_B):
        gs = slice(g * gd, (g + 1) * gd)
        wg = jnp.where(mask, w_ref[g], 0.0).astype(BF16)
        vg = [v_ref[ch * c:(ch + 1) * c, gs].astype(BF16) for ch in range(nch)]
        vg = vg[0] if nch == 1 else jnp.concatenate(vg, axis=1)
        sg = jnp.dot(wg, vg, preferred_element_type=F32) + b[:, g:g + 1]
        for ch in range(nch):
            u = u_ref[ch * c:(ch + 1) * c, gs].astype(F32)
            o_ref[ch * c:(ch + 1) * c, gs] = (u * sg[:, ch * gd:(ch + 1) * gd]).astype(o_ref.dtype)


def _gmlp(u, v, w, b, tm, period):
    m, wb = u.shape
    c = w.shape[1]
    return pl.pallas_call(
        functools.partial(_gmlp_body, period=period), name="gmlp", grid=(m // tm,),
        in_specs=[pl.BlockSpec((tm, wb), lambda i: (i, 0)), pl.BlockSpec((tm, wb), lambda i: (i, 0)),
                  pl.BlockSpec((N_GROUPS_B, c, c), lambda i: (0, 0, 0)),
                  pl.BlockSpec((c, N_GROUPS_B), lambda i: (0, 0))],
        out_specs=pl.BlockSpec((tm, wb), lambda i: (i, 0)),
        out_shape=jax.ShapeDtypeStruct((m, wb), BF16), compiler_params=_cparams(1),
    )(u, v, w, b)


def _gmlp_new_body(u_ref, v_ref, w_ref, b_ref, o_ref):
    t = w_ref.shape[0]
    n = u_ref.shape[0] // t
    width = u_ref.shape[1]
    v = v_ref[...].astype(F32).reshape(n, t, width)
    sg = jnp.broadcast_to(b_ref[...][None], (n, t, width))
    p_row = lax.broadcasted_iota(I32, (t, width), 0)
    for q in range(t):
        sg = sg + jnp.where(p_row >= q, w_ref[q], 0.0)[None] * v[:, q:q + 1, :]
    o_ref[...] = (u_ref[...].astype(F32).reshape(n, t, width) * sg).reshape(n * t, width).astype(o_ref.dtype)


def _gmlp_new(u, v, w_s, b_s, t):
    m, wb = u.shape
    gd = wb // N_GROUPS_B
    wq = jnp.repeat(jnp.transpose(w_s[:, :t, :t], (2, 1, 0)), gd, axis=-1)
    bq = jnp.repeat(b_s[:, :t].T, gd, axis=-1)
    whole = lambda a: pl.BlockSpec(a.shape, lambda i: (0,) * a.ndim)
    return pl.pallas_call(
        _gmlp_new_body, name="gmlp_new", grid=(1,),
        in_specs=[whole(u), whole(v), whole(wq), whole(bq)], out_specs=whole(u),
        out_shape=jax.ShapeDtypeStruct((m, wb), BF16), compiler_params=_cparams(1),
    )(u, v, wq, bq)


def _memattn_body(q_ref, kv_ref, o_ref, *, tiles):
    wm = N_HEADS_M * HEAD_DIM_M
    n_ct = HEAD_DIM_M // LANES
    per_key = 2 * n_ct * N_HEADS_M
    reqs = kv_ref.shape[0]
    tq = q_ref.shape[0] // reqs

    for r in range(reqs):
        kvr = kv_ref.at[r]

        def head_slab(kv, h):
            if not tiles:
                return kvr[:, kv * wm + h * HEAD_DIM_M:kv * wm + (h + 1) * HEAD_DIM_M].astype(BF16)
            n_keys = kvr.shape[0] // per_key
            parts = [kvr[pl.ds((kv * n_ct + ct) * N_HEADS_M + h, n_keys, stride=per_key), :] for ct in range(n_ct)]
            return jnp.concatenate(parts, axis=1).astype(BF16)

        q = q_ref[r * tq:(r + 1) * tq, :].astype(BF16)
        outs = []
        for h in range(N_HEADS_M):
            hs = slice(h * HEAD_DIM_M, (h + 1) * HEAD_DIM_M)
            k = head_slab(0, h)
            v = head_slab(1, h)
            s = lax.dot_general(q[:, hs], k, (((1,), (1,)), ((), ())), preferred_element_type=F32)
            m = jnp.max(s, axis=-1, keepdims=True)
            p = jnp.exp(s - m)
            l = jnp.sum(p, axis=-1, keepdims=True)
            outs.append(jnp.dot(p.astype(BF16), v, preferred_element_type=F32) / l)
        o_ref[r * tq:(r + 1) * tq, :] = jnp.concatenate(outs, axis=1).astype(o_ref.dtype)


def _memattn(q, kv, tq, out_dtype, tiles=False, reqs=1):
    m, wm = q.shape
    n = kv.shape[0]
    per = m // n // tq
    assert reqs == 1 or (per == 1 and n % reqs == 0)
    kv_spec = pl.BlockSpec((reqs,) + kv.shape[1:], lambda i: (i // per,) + (0,) * (kv.ndim - 1))
    tq = tq * reqs
    return pl.pallas_call(
        functools.partial(_memattn_body, tiles=tiles), name="memattn", grid=(m // tq,),
        in_specs=[pl.BlockSpec((tq, wm), lambda i: (i, 0)), kv_spec],
        out_specs=pl.BlockSpec((tq, wm), lambda i: (i, 0)),
        out_shape=jax.ShapeDtypeStruct((m, wm), out_dtype), compiler_params=_cparams(1),
    )(q, kv)


def _mix_body(gt_ref, o0_ref, o1_ref, o2_ref, ob_ref, om_ref, wpa_ref, wpb_ref, wpm_ref, z_ref, *, dils):
    d = z_ref.shape[1]
    tm = z_ref.shape[0]
    os_, ls = [], []
    for o_ref, dil in zip((o0_ref, o1_ref, o2_ref), dils):
        if dil == 1:
            x = o_ref[...].astype(F32)
        else:
            rows = jnp.concatenate([o_ref[r] for r in range(dil)], axis=0)
            x = jnp.dot(_perm_matrix(tm, dil, inverse=True), rows, preferred_element_type=F32)
        os_.append(x[:, :GROUP_W_A])
        ls.append(x[:, GROUP_W_A:GROUP_W_A + LANES] + x[:, GROUP_W_A + LANES:])
    l0, l1, l2 = ls
    mx = jnp.maximum(jnp.maximum(l0, l1), l2)
    e0, e1, e2 = jnp.exp(l0 - mx), jnp.exp(l1 - mx), jnp.exp(l2 - mx)
    den = e0 + e1 + e2
    w0, w1, w2 = e0 / den, e1 / den, e2 / den
    cols = []
    for h in range(HEADS_PER_GROUP_A):
        hs = slice(h * HEAD_DIM_A, (h + 1) * HEAD_DIM_A)
        cols.append(w0[:, h:h + 1] * os_[0][:, hs] + w1[:, h:h + 1] * os_[1][:, hs] + w2[:, h:h + 1] * os_[2][:, hs])
    oa = jnp.concatenate(cols, axis=1).astype(BF16)
    pa = jnp.dot(oa, wpa_ref[...], preferred_element_type=F32)
    pb = jnp.dot(ob_ref[...].astype(BF16), wpb_ref[...], preferred_element_type=F32)
    pm = jnp.dot(om_ref[...].astype(BF16), wpm_ref[...], preferred_element_type=F32)
    z = (gt_ref[:, 0:d].astype(F32) * pa + gt_ref[:, d:2 * d].astype(F32) * pb
         + gt_ref[:, 2 * d:3 * d].astype(F32) * pm)
    z_ref[...] = z.astype(z_ref.dtype)


def _mix(gates, o_list, dils, seq, ob, om, wpa, wpb, wpm, tm):
    m = gates.shape[0]
    d = wpa.shape[1]
    mt = seq // tm if any(dl > 1 for dl in dils) else 1

    def rows(width):
        return pl.BlockSpec((tm, width), lambda i: (i, 0))

    def whole(a):
        return pl.BlockSpec(a.shape, lambda i: (0, 0))

    o_specs, o_args = [], []
    for o, dil in zip(o_list, dils):
        if dil == 1:
            o_specs.append(rows(O_EXT_W))
            o_args.append(o)
        else:
            assert seq % tm == 0 and (tm // dil) % 16 == 0
            o_specs.append(pl.BlockSpec((None, dil, tm // dil, O_EXT_W), lambda i: (i // mt, 0, i % mt, 0)))
            o_args.append(o.reshape(m // seq, dil, seq // dil, O_EXT_W))
    return pl.pallas_call(
        functools.partial(_mix_body, dils=tuple(dils)), name="mix", grid=(m // tm,),
        in_specs=[rows(gates.shape[1])] + o_specs
                 + [rows(ob.shape[1]), rows(om.shape[1]), whole(wpa), whole(wpb), whole(wpm)],
        out_specs=rows(d),
        out_shape=jax.ShapeDtypeStruct((m, d), BF16), compiler_params=_cparams(1),
    )(gates, *o_args, ob, om, wpa, wpb, wpm)


def _route(logits):
    lane = lax.broadcasted_iota(I32, logits.shape, 1)
    lane_f = lane.astype(F32)
    is_g = lane < N_EXPERT_GROUPS
    gmax = jnp.max(jnp.where(is_g, logits, -jnp.inf), axis=1, keepdims=True)
    gsel = jnp.min(jnp.where(is_g & (logits == gmax), lane_f, float(LANES)), axis=1, keepdims=True).astype(I32)
    gden = jnp.sum(jnp.where(is_g, jnp.exp(logits - gmax), 0.0), axis=1, keepdims=True)
    pg = 1.0 / gden
    e_lane = lane - N_EXPERT_GROUPS
    in_grp = (e_lane >= 0) & (e_lane < N_EXPERTS) & ((e_lane // EXPERTS_PER_GROUP) == gsel)
    m1 = jnp.max(jnp.where(in_grp, logits, -jnp.inf), axis=1, keepdims=True)
    i1 = jnp.min(jnp.where(in_grp & (logits == m1), lane_f, float(LANES)), axis=1, keepdims=True).astype(I32)
    rest = in_grp & (lane != i1)
    m2 = jnp.max(jnp.where(rest, logits, -jnp.inf), axis=1, keepdims=True)
    i2 = jnp.min(jnp.where(rest & (logits == m2), lane_f, float(LANES)), axis=1, keepdims=True).astype(I32)
    e2 = jnp.exp(m2 - m1)
    w1 = pg / (1.0 + e2)
    w2 = pg * e2 / (1.0 + e2)
    eid = jnp.where(lane == 0, i1 - N_EXPERT_GROUPS, jnp.where(lane == 1, i2 - N_EXPERT_GROUPS, 0))
    wts = jnp.where(lane == 0, w1, jnp.where(lane == 1, w2, 0.0))
    return eid, wts


def _resid_body(x_ref, z_ref, wo_ref, gf_ref, wr_ref, br_ref, *rest, n_main):
    tail_ref = rest[0] if len(rest) == 5 else None
    xmid_ref, hpk_ref, eid_ref, wts_ref = rest[-4:]

    def main():
        xm = x_ref[...] + jnp.dot(z_ref[...], wo_ref[...], preferred_element_type=F32)
        xmid_ref[...] = xm
        hf = _rms(xm, gf_ref[...])
        wr = wr_ref[...]
        wr_hi, wr_lo = _split_hi_lo(wr)
        hf_hi, hf_lo = _split_hi_lo(hf)
        t = jnp.dot(hf_hi, jnp.concatenate([wr_hi, wr_lo], axis=1), preferred_element_type=F32)
        logits = (t[:, :LANES] + t[:, LANES:] + jnp.dot(hf_lo, wr_hi, preferred_element_type=F32) + br_ref[...])
        eid, wts = _route(logits)
        eid_ref[...] = eid
        wts_ref[...] = wts
        _store_token_tiles(hpk_ref, hf)

    if tail_ref is None:
        main()
        return
    pl.when(pl.program_id(0) < n_main)(main)

    @pl.when(pl.program_id(0) == n_main)
    def _():
        hpk_ref[...] = tail_ref[...]


def _resid(x2d, z, wo, g_ffn, w_r, b_r, tm, tail=None):
    m, d = x2d.shape
    n_main = m // tm
    last = n_main - 1

    def rows(width):
        return pl.BlockSpec((tm, width), lambda i: (jnp.minimum(i, last), 0))

    def whole(a):
        return pl.BlockSpec(a.shape, lambda i: (0, 0))

    gf = g_ffn.reshape(1, d)
    nt = d // 2 // LANES
    assert tail is None or tail.shape == (tm * nt, LANES)
    n_tail = 0 if tail is None else 1
    return pl.pallas_call(
        functools.partial(_resid_body, n_main=n_main), name="resid", grid=(n_main + n_tail,),
        in_specs=[rows(d), rows(d), whole(wo), whole(gf), whole(w_r), whole(b_r)]
                 + ([whole(tail)] if n_tail else []),
        out_specs=[rows(d), pl.BlockSpec((tm * nt, LANES), lambda i: (i, 0)), rows(LANES), rows(LANES)],
        out_shape=[jax.ShapeDtypeStruct((m, d), F32), jax.ShapeDtypeStruct(((m + n_tail * tm) * nt, LANES), U32),
                   jax.ShapeDtypeStruct((m, LANES), I32), jax.ShapeDtypeStruct((m, LANES), F32)],
        compiler_params=_cparams(1),
    )(x2d, z, wo, gf, w_r, b_r, *([tail] * n_tail))


def _mixres_body(gt_ref, o0_ref, o1_ref, o2_ref, ob_ref, om_ref, wpa_ref, wpb_ref, wpm_ref,
                 x_ref, wo_ref, gf_ref, wr_ref, br_ref, tail_ref,
                 xmid_ref, hpk_ref, eid_ref, wts_ref, z_scr, *, dils, n_main):
    def main():
        _mix_body(gt_ref, o0_ref, o1_ref, o2_ref, ob_ref, om_ref, wpa_ref, wpb_ref, wpm_ref, z_scr, dils=dils)
        _resid_body(x_ref, z_scr, wo_ref, gf_ref, wr_ref, br_ref, xmid_ref, hpk_ref, eid_ref, wts_ref,
                    n_main=n_main)

    pl.when(pl.program_id(0) < n_main)(main)

    @pl.when(pl.program_id(0) == n_main)
    def _():
        hpk_ref[...] = tail_ref[...]


def _mixres(gates, o_list, dils, seq, ob, om, wpa, wpb, wpm, x2d, wo, g_ffn, w_r, b_r, tm, tail):
    m, d = x2d.shape
    n_main = m // tm
    last = n_main - 1
    mt = seq // tm
    nt = d // 2 // LANES
    assert tail.shape == (tm * nt, LANES) and seq % tm == 0

    def rows(width):
        return pl.BlockSpec((tm, width), lambda i: (jnp.minimum(i, last), 0))

    def whole(a):
        return pl.BlockSpec(a.shape, lambda i: (0, 0), pipeline_mode=pl.Buffered(1))

    o_specs, o_args = [], []
    for o, dil in zip(o_list, dils):
        if dil == 1:
            o_specs.append(rows(O_EXT_W))
            o_args.append(o)
        else:
            assert (tm // dil) % 16 == 0
            o_specs.append(pl.BlockSpec(
                (None, dil, tm // dil, O_EXT_W),
                lambda i: (jnp.minimum(i, last) // mt, 0, jnp.minimum(i, last) % mt, 0)))
            o_args.append(o.reshape(m // seq, dil, seq // dil, O_EXT_W))
    gf = g_ffn.reshape(1, d)
    return pl.pallas_call(
        functools.partial(_mixres_body, dils=tuple(dils), n_main=n_main), name="mixres", grid=(n_main + 1,),
        in_specs=[rows(gates.shape[1])] + o_specs
                 + [rows(ob.shape[1]), rows(om.shape[1]), whole(wpa), whole(wpb), whole(wpm),
                    rows(d), whole(wo), whole(gf), whole(w_r), whole(b_r), whole(tail)],
        out_specs=[rows(d), pl.BlockSpec((tm * nt, LANES), lambda i: (i, 0)), rows(LANES), rows(LANES)],
        out_shape=[jax.ShapeDtypeStruct((m, d), F32), jax.ShapeDtypeStruct(((m + tm) * nt, LANES), U32),
                   jax.ShapeDtypeStruct((m, LANES), I32), jax.ShapeDtypeStruct((m, LANES), F32)],
        scratch_shapes=[pltpu.VMEM((tm, d), BF16)], compiler_params=_cparams(1),
    )(gates, *o_args, ob, om, wpa, wpb, wpm, x2d, wo, gf, w_r, b_r, tail)


TOKEN_TILE = 8
GATHER_UNROLL = 8


def _gather_rows(idx_at, src_hbm, dst, sem, n):
    def body(i, carry):
        src = pl.multiple_of(idx_at(i), TOKEN_TILE)
        dst_row = pl.multiple_of(i * TOKEN_TILE, TOKEN_TILE)
        pltpu.make_async_copy(src_hbm.at[pl.ds(src, TOKEN_TILE)], dst.at[pl.ds(dst_row, TOKEN_TILE)], sem).start()
        return carry
    lax.fori_loop(0, n, body, 0, unroll=GATHER_UNROLL)


FFN_CHUNKS = 8
FFN_BUFS = 3


def _ffn_body(blk_e_ref, nused_ref, eord_ref, enext_ref, idx_hbm, h_hbm, wg_hbm, wu_hbm, wd_hbm, y_ref,
              idx_s, isem, buf, sem, x_scr, a_scr, b_scr, hm_scr, wg_f, wu_f, wd_f, wsem, wg_s, wu_s, wd_s):
    j = pl.program_id(0)
    nused = nused_ref[0]
    last = nused - 1
    rows = buf.shape[1] // TOKEN_TILE
    e = blk_e_ref[j]

    def idx_copy(b):
        s = b % FFN_BUFS
        return pltpu.make_async_copy(idx_hbm.at[jnp.minimum(b, last)], idx_s.at[s], isem.at[s])

    def wait_rows(s):
        pltpu.make_async_copy(h_hbm.at[pl.ds(0, rows * TOKEN_TILE)], buf.at[s], sem.at[s]).wait()
    new_expert = (j == 0) | (e != blk_e_ref[jnp.maximum(j - 1, 0)])
    ws = eord_ref[j] % 2

    def fetch_w(expert, s):
        return [pltpu.make_async_copy(src.at[expert], dst.at[s], wsem.at[s, i])
                for i, (src, dst) in enumerate(((wg_hbm, wg_f), (wu_hbm, wu_f), (wd_hbm, wd_f)))]

    @pl.when(j == 0)
    def _():
        for c in fetch_w(e, ws):
            c.start()
        for b in range(FFN_BUFS):
            idx_copy(b).start()
        for b in range(FFN_BUFS - 1):
            idx_copy(b).wait()
            _gather_rows(lambda i, b=b: idx_s[b, 0, i], h_hbm, buf.at[b], sem.at[b], rows)

    @pl.when(new_expert)
    def _():
        nxt = enext_ref[j]

        @pl.when(nxt >= 0)
        def _():
            for c in fetch_w(nxt, 1 - ws):
                c.start()

        for c in fetch_w(e, ws):
            c.wait()
        wg_s[...] = wg_f[ws].astype(BF16)
        wu_s[...] = wu_f[ws].astype(BF16)
        wd_s[...] = wd_f[ws].astype(BF16)

    @pl.when(j < nused)
    def _():
        slot = j % FFN_BUFS
        s2 = (j + 2) % FFN_BUFS
        per = rows // FFN_CHUNKS
        idx_copy(j + 2).wait()

        @pl.when(j < last)
        def _():
            idx_copy(j + 3).start()

        def issue(c):
            for i in range(c * per, (c + 1) * per):
                src = pl.multiple_of(idx_s[s2, 0, i], TOKEN_TILE)
                pltpu.make_async_copy(h_hbm.at[pl.ds(src, TOKEN_TILE)],
                                      buf.at[s2, pl.ds(i * TOKEN_TILE, TOKEN_TILE)], sem.at[s2]).start()

        wait_rows(slot)
        x_scr[...] = _load_token_tiles(buf, slot, 0, rows, TOKEN_TILE).astype(BF16)
        dn = a_scr.shape[1] // 2
        for c in range(2):
            issue(c)
            a_scr[:, c * dn:(c + 1) * dn] = jnp.dot(x_scr[...], wg_s[:, c * dn:(c + 1) * dn],
                                                     preferred_element_type=F32)
        for c in range(2):
            issue(2 + c)
            b_scr[:, c * dn:(c + 1) * dn] = jnp.dot(x_scr[...], wu_s[:, c * dn:(c + 1) * dn],
                                                     preferred_element_type=F32)
        a = a_scr[...]
        hm_scr[...] = (a * jax.nn.sigmoid(a) * b_scr[...]).astype(BF16)
        half = wd_s.shape[1] // 2
        n_dc = FFN_CHUNKS - 4
        dq = half // n_dc
        for c in range(n_dc):
            issue(4 + c)
            lo = jnp.dot(hm_scr[...], wd_s[:, c * dq:(c + 1) * dq], preferred_element_type=F32)
            hi = jnp.dot(hm_scr[...], wd_s[:, half + c * dq:half + (c + 1) * dq], preferred_element_type=F32)
            w = _pack_bf16_pair(jnp.concatenate([lo, hi], axis=1))
            for t in range(dq // LANES):
                y_ref[pl.ds(c * (dq // LANES) + t, rows, stride=TOKEN_TILE), :] = w[:, t * LANES:(t + 1) * LANES]

        @pl.when(j == last)
        def _():
            wait_rows((j + 1) % FFN_BUFS)
            wait_rows(s2)

    @pl.when(j >= nused)
    def _():
        y_ref[...] = jnp.zeros(y_ref.shape, y_ref.dtype)


def _ffn(hpk, src_tok, blk_e, nused, eord, enext, wg, wu, wd):
    n_blocks = blk_e.shape[0]
    rows = MOE_ROWS
    tiles = rows * TOKEN_TILE
    d, de = wg.shape[1], wg.shape[2]
    assert d // 2 == TOKEN_TILE * LANES and rows % FFN_CHUNKS == 0 and (d // 2) % ((FFN_CHUNKS - 4) * LANES) == 0
    assert n_blocks >= N_EXPERTS + FFN_BUFS
    idx = (src_tok * TOKEN_TILE).reshape(n_blocks, 1, rows)
    hbm = pl.BlockSpec(memory_space=pl.ANY)
    grid_spec = pltpu.PrefetchScalarGridSpec(
        num_scalar_prefetch=4, grid=(n_blocks,),
        in_specs=[hbm, hbm, hbm, hbm, hbm],
        out_specs=pl.BlockSpec((tiles, LANES), lambda j, *_: (j, 0)),
        scratch_shapes=[pltpu.SMEM((FFN_BUFS, 1, rows), I32), pltpu.SemaphoreType.DMA((FFN_BUFS,)),
                        pltpu.VMEM((FFN_BUFS, tiles, LANES), U32), pltpu.SemaphoreType.DMA((FFN_BUFS,)),
                        pltpu.VMEM((rows, d), BF16), pltpu.VMEM((rows, de), F32), pltpu.VMEM((rows, de), F32),
                        pltpu.VMEM((rows, de), BF16),
                        pltpu.VMEM((2, d, de), wg.dtype), pltpu.VMEM((2, d, de), wu.dtype),
                        pltpu.VMEM((2, de, d), wd.dtype), pltpu.SemaphoreType.DMA((2, 3)),
                        pltpu.VMEM((d, de), BF16), pltpu.VMEM((d, de), BF16), pltpu.VMEM((de, d), BF16)],
    )
    return pl.pallas_call(
        _ffn_body, name="ffn", grid_spec=grid_spec,
        out_shape=jax.ShapeDtypeStruct((n_blocks * tiles, LANES), U32), compiler_params=_cparams(1),
    )(blk_e, nused, eord, enext, idx, hpk, wg, wu, wd)


def _combine_body(idx_cur_ref, idx_nxt_ref, x_ref, w_ref, y_hbm, o_ref, buf, sem, *, n):
    i = pl.program_id(0)
    slot = i % 2
    rows = buf.shape[1] // TOKEN_TILE
    tm = rows // TOP_K

    @pl.when(i == 0)
    def _():
        _gather_rows(lambda r: idx_cur_ref[0, 0, r], y_hbm, buf.at[0], sem.at[0], rows)

    if n > 1:
        @pl.when(i + 1 < n)
        def _():
            _gather_rows(lambda r: idx_nxt_ref[0, 0, r], y_hbm, buf.at[1 - slot], sem.at[1 - slot], rows)

    pltpu.make_async_copy(y_hbm.at[pl.ds(0, rows * TOKEN_TILE)], buf.at[slot], sem.at[slot]).wait()
    w = w_ref[...]
    y0 = _load_token_tiles(buf, slot, 0, tm, TOKEN_TILE)
    y1 = _load_token_tiles(buf, slot, tm, tm, TOKEN_TILE)
    o_ref[...] = x_ref[...] + (w[:, 0:1] * y0 + w[:, 1:2] * y1)


def _combine(xmid, wts, dest, ypk, tm):
    m, d = xmid.shape
    nt = m // tm
    assert d // 2 == TOKEN_TILE * LANES
    idx = (dest * TOKEN_TILE).reshape(nt, tm, TOP_K).transpose(0, 2, 1).reshape(nt, 1, TOP_K * tm)

    def idx_spec(ahead):
        return pl.BlockSpec((1, 1, TOP_K * tm), lambda i: (jnp.minimum(i + ahead, nt - 1), 0, 0),
                            memory_space=pltpu.SMEM)

    return pl.pallas_call(
        functools.partial(_combine_body, n=nt), name="combine", grid=(nt,),
        in_specs=[idx_spec(0), idx_spec(1), pl.BlockSpec((tm, d), lambda i: (i, 0)),
                  pl.BlockSpec((tm, LANES), lambda i: (i, 0)), pl.BlockSpec(memory_space=pl.ANY)],
        out_specs=pl.BlockSpec((tm, d), lambda i: (i, 0)),
        out_shape=jax.ShapeDtypeStruct((m, d), F32),
        scratch_shapes=[pltpu.VMEM((2, TOP_K * tm * TOKEN_TILE, LANES), U32), pltpu.SemaphoreType.DMA((2,))],
        compiler_params=_cparams(1),
    )(idx, idx, xmid, wts, ypk)


def _moe_plan(eid2):
    n_slot = eid2.shape[0] * TOP_K
    eid = eid2.reshape(n_slot)
    onehot = (eid[:, None] == jnp.arange(N_EXPERTS, dtype=I32)[None, :]).astype(I32)
    csum = jnp.cumsum(onehot, axis=0)
    rank = jnp.take_along_axis(csum, eid[:, None], axis=1)[:, 0] - 1
    counts = csum[-1]
    padded = (counts + MOE_ROWS - 1) // MOE_ROWS * MOE_ROWS
    pend = jnp.cumsum(padded)
    pstart = pend - padded
    dest = pstart[eid] + rank
    n_blocks = -(-n_slot // MOE_ROWS) + N_EXPERTS
    nused = (pend[-1] // MOE_ROWS).astype(I32)
    blk = jnp.minimum(jnp.arange(n_blocks, dtype=I32), nused - 1)
    blk_e = jnp.minimum(jnp.searchsorted(pend, blk * MOE_ROWS, side="right"), N_EXPERTS - 1).astype(I32)
    n_rows = n_blocks * MOE_ROWS
    src_tok = (jnp.arange(n_rows, dtype=I32) % (n_slot // TOP_K)).at[dest].set(
        jnp.arange(n_slot, dtype=I32) // TOP_K, unique_indices=True, mode="promise_in_bounds")
    has_rows = counts > 0
    ord_e = jnp.cumsum(has_rows.astype(I32)) - 1
    ids = jnp.arange(N_EXPERTS, dtype=I32)
    later = has_rows[None, :] & (ids[None, :] > ids[:, None])
    next_e = jnp.where(later.any(axis=1), jnp.argmax(later, axis=1), -1).astype(I32)
    return dest.astype(I32), src_tok, blk_e, nused.reshape(1), ord_e[blk_e].astype(I32), next_e[blk_e]


def _mixer(x2d, tm, p, attn_fn, gmlp_fn, mem_fn):
    h = _prep(x2d, p["g_mix"], tm)
    qdt = p["q_dtype"]
    wi, sec = p["w_in"], p["sec"]
    o_list, dils, seq = attn_fn(h)
    n_ub = sec["vb"] - sec["u"]
    ub = _proj(h, wi, (sec["u"], n_ub, n_ub, 1), tm, "none", BF16, name="proj_ub")
    n_vb = sec["qm"] - sec["vb"]
    vb = _proj(h, wi, (sec["vb"], n_vb, n_vb, 1), tm, "norm", p["vb_dtype"], p["g_vb"], n_vb * W_COLS,
               name="proj_vb")
    n_qm = sec["gt"] - sec["qm"]
    qm = _proj(h, wi, (sec["qm"], n_qm, n_qm, 1), tm, "norm", qdt, p["g_qm"], HEAD_DIM_M,
               HEAD_DIM_M ** -0.5, name="proj_qm")
    gates = _proj(h, wi, (sec["gt"], 2, 2, (sec["end"] - sec["gt"]) // 2), tm, "sigmoid", BF16, name="proj_gates")
    ob = gmlp_fn(ub, vb)
    om = mem_fn(qm)
    if p.get("hpk_tail") is not None:
        xmid, hpk, eid, wts = _mixres(gates, o_list, dils, seq, ob, om, p["w_pa"], p["w_pb"], p["w_pm"], x2d,
                                      p["w_o"], p["g_ffn"], p["w_r"], p["b_r"], min(tm, 256), p["hpk_tail"])
        return vb, xmid, hpk, eid, wts
    z = _mix(gates, o_list, dils, seq, ob, om, p["w_pa"], p["w_pb"], p["w_pm"], min(tm, 256))
    xmid, hpk, eid, wts = _resid(x2d, z, p["w_o"], p["g_ffn"], p["w_r"], p["b_r"], min(tm, 256),
                                 tail=p.get("hpk_tail"))
    return vb, xmid, hpk, eid, wts


def kernel(x_prompt, x_sample, mem_prompt, cache_a0_kv, cache_a1_kv, cache_a2_kv, cache_mem_kv, rel_bias, g_mix, w_in, g_qa, g_ka, w_pa, g_vb, w_s, b_s, w_pb, g_mem, w_mk, w_mv, g_qm, g_km, w_pm, w_o, g_ffn, w_rg, b_rg, w_re, b_re, w_gate, w_up, w_down):
    n_b, seq, d = x_prompt.shape
    n_s, t_s, _ = x_sample.shape
    depth = w_in.shape[0]
    assert depth == 1
    l = 0
    caches = (cache_a0_kv, cache_a1_kv, cache_a2_kv)
    n_g = len(DIL_GROUPS)
    wb = w_pb.shape[1]
    wm = N_HEADS_M * HEAD_DIM_M

    offs = [0, WIDTH_A, 2 * WIDTH_A, 3 * WIDTH_A, 3 * WIDTH_A + wb, 3 * WIDTH_A + 2 * wb,
            3 * WIDTH_A + 2 * wb + wm, w_in.shape[2]]
    assert all(o % W_COLS == 0 for o in offs) and (offs[7] - offs[6]) % (2 * W_COLS) == 0
    sec = dict(zip(("q", "k", "v", "u", "vb", "qm", "gt", "end"), (o // W_COLS for o in offs)))
    p = {
        "g_mix": g_mix[l], "g_qa": g_qa[l], "g_ka": g_ka[l], "g_vb": g_vb[l], "g_qm": g_qm[l], "g_ffn": g_ffn[l],
        "w_in": w_in[l], "sec": sec,
        "w_pa": w_pa[l].astype(BF16), "w_pb": w_pb[l].astype(BF16), "w_pm": w_pm[l].astype(BF16),
        "w_o": w_o[l].astype(BF16),
    }
    n_r = N_EXPERT_GROUPS + N_EXPERTS
    p["w_r"] = jnp.concatenate([w_rg[l], w_re[l], jnp.zeros((d, LANES - n_r), F32)], axis=1)
    p["b_r"] = jnp.concatenate([b_rg[l], b_re[l], jnp.zeros((LANES - n_r,), F32)]).reshape(1, LANES)

    n_mem = mem_prompt.shape[1]
    h_mem = _prep(mem_prompt.reshape(n_b * n_mem, d), g_mem[l], 256)
    w_mkv = jnp.concatenate([w_mk[l], w_mv[l]], axis=1)
    n_mw = wm // W_COLS
    mkv = _proj(h_mem, w_mkv, (0, n_mw, n_mw, 2), 256, "norm_first", F32, g_km[l], HEAD_DIM_M, name="proj_mkv")
    new_mem_p = mkv.reshape(1, n_b, n_mem, 2, N_HEADS_M, HEAD_DIM_M)

    ps = dict(p, q_dtype=F32, vb_dtype=F32)
    m_s = n_s * t_s
    n_col = 2 * HEADS_PER_GROUP_A
    new_s = []

    def attn_sample(h):
        o_list = []
        for g in range(n_g):
            q, kv, tok = _proj_qkv(h, w_in[l], sec, g, g_qa[l], g_ka[l], 1, m_s, m_s, 1, act_dtype=F32)
            cache3 = caches[g][l].reshape(n_s, caches[g].shape[2] * n_col, LANES)
            o, newc = _attn_sample(q, kv, tok, cache3, rel_bias, g, t_s)
            o_list.append(o)
            new_s.append(newc.reshape(caches[g][l:l + 1].shape))
        return o_list, [1] * n_g, t_s

    n_ct = HEAD_DIM_M // LANES
    mem_tiles = (cache_mem_kv[l].reshape(n_s, n_mem, 2, N_HEADS_M, n_ct, LANES).transpose(0, 1, 2, 4, 3, 5)
                 .reshape(n_s, n_mem * 2 * n_ct * N_HEADS_M, LANES))
    vb_s, xmid_s, hpk_s, eid_s, wts_s = _mixer(
        x_sample.reshape(m_s, d), m_s, ps, attn_sample,
        lambda u, v: _gmlp_new(u, v, w_s[l], b_s[l], t_s),
        lambda qm: _memattn(qm, mem_tiles, t_s, F32, tiles=True, reqs=4))
    new_vb_s = vb_s.reshape(1, n_s, t_s, wb)

    pp = dict(p, q_dtype=BF16, vb_dtype=BF16, hpk_tail=hpk_s)
    new_p = []

    def attn_prompt(h):
        o_list = []
        for g, (win, dil) in enumerate(DIL_GROUPS):
            q, kv, tok = _proj_qkv(h, w_in[l], sec, g, g_qa[l], g_ka[l], n_b, seq, 512, dil)
            o_list.append(_attn_prompt(q, kv, rel_bias, g, n_b, seq))
            keep = min(win, seq)
            kv5 = tok.reshape(n_b, seq, 2, HEADS_PER_GROUP_A, HEAD_DIM_A)
            new_p.append((kv5 if keep == seq else kv5[:, seq - keep:])[None])
        return o_list, [dl for _, dl in DIL_GROUPS], seq

    _, xmid_p, hpk_p, eid_p, wts_p = _mixer(
        x_prompt.reshape(n_b * seq, d), 512, pp, attn_prompt,
        lambda u, v: _gmlp(u, v, w_s[l], b_s[l].T, 512, CHUNK),
        lambda qm: _memattn(qm, mkv.reshape(n_b, n_mem, 2 * wm), 512, BF16))

    n_p = n_b * seq
    hpk = hpk_p
    eid2 = jnp.concatenate([eid_p[:, :TOP_K], eid_s[:, :TOP_K]], axis=0)
    dest, src_tok, blk_e, nused, eord, enext = _moe_plan(eid2)
    ypk = _ffn(hpk, src_tok, blk_e, nused, eord, enext, w_gate[l], w_up[l], w_down[l])
    dest2 = dest.reshape(-1, TOP_K)
    y_p = _combine(xmid_p, wts_p, dest2[:n_p], ypk, 256)
    y_s = _combine(xmid_s, wts_s, dest2[n_p:], ypk, m_s)

    return (y_p.reshape(n_b, seq, d), y_s.reshape(n_s, t_s, d), new_p[0], new_p[1], new_p[2], new_mem_p,
            new_s[0], new_s[1], new_s[2], new_vb_s)
```
